```python
import math
import jax
import jax.numpy as jnp
from jax import lax
import numpy as np

D_MODEL = 1024
BATCH = 16
SEQ = 256
DEPTH = 2
DEC_BATCH = 8
DEC_SEQ = 1024
PAST_LEN = 512

GRID_W = 64
Q_BLOCK = 128
ROPE_BASE = 10000.0
EPS = 1e-6
N_BRANCH = 4
W_BRANCH = D_MODEL // 2

H_A = 8
NOPE_A = 64
ROPE_A = 32
V_A = W_BRANCH // H_A
Q_RANK = D_MODEL // 4
KV_RANK = D_MODEL // 8
MLA_SCALE = (NOPE_A + ROPE_A) ** -0.5

W_B = W_BRANCH
NB_B = 8
BW_B = W_B // NB_B
CONV_W = 4
RG_C = 8.0

H_C = 4
DH_C = W_BRANCH // H_C
CHUNK = 64

H_D = 4
DH_D = W_BRANCH // (2 * H_D)
DIFF_SCALE = DH_D ** -0.5

F_DENSE = 2816
N_EXP = 8
TOP_K = 2
F_EXP = 3584
N_DENSE = (DEPTH + 1) // 2
N_MOE = DEPTH // 2

IN_SPLITS = (Q_RANK, KV_RANK, ROPE_A,
             W_B, W_B,
             W_BRANCH, W_BRANCH, W_BRANCH, W_BRANCH, 2 * H_C, 2 * H_C,
             2 * H_D * DH_D, 2 * H_D * DH_D, 2 * H_D * DH_D,
             N_BRANCH * D_MODEL)
IN_OFFSETS = tuple(int(v) for v in np.cumsum(IN_SPLITS)[:-1])
N_IN = int(sum(IN_SPLITS))

kernel_name = 'hybrid_diffusion_mla_rglru_mlstm_diffattn_step'


def _rms(x, g):
    xf = x.astype(jnp.float32)
    y = xf * lax.rsqrt(jnp.mean(xf * xf, axis=-1, keepdims=True) + EPS)
    return (y * g.astype(jnp.float32)).astype(x.dtype)


def _axial_rope(n_tokens, rot_dim):
    rows = n_tokens // GRID_W
    r, c = jnp.meshgrid(jnp.arange(rows, dtype=jnp.float32), jnp.arange(GRID_W, dtype=jnp.float32), indexing='ij')
    n_freq = rot_dim // 4
    inv = ROPE_BASE ** (-jnp.arange(n_freq, dtype=jnp.float32) / n_freq)
    ang = jnp.stack([r.reshape(-1)[:, None] * inv, c.reshape(-1)[:, None] * inv], axis=1)
    return jnp.cos(ang), jnp.sin(ang)


def _apply_rope(x, cs):
    cos, sin = cs
    b, t, h, rd = x.shape
    nf = rd // 4
    xr = x.reshape(b, t, h, 2, 2, nf)
    x1, x2 = xr[..., 0, :], xr[..., 1, :]
    cos = cos[:, None].astype(x.dtype)
    sin = sin[:, None].astype(x.dtype)
    out = jnp.stack([x1 * cos - x2 * sin, x1 * sin + x2 * cos], axis=-2)
    return out.reshape(b, t, h, rd)


def _by_query_blocks(fn, *qs):
    b, t = qs[0].shape[:2]
    nb = t // Q_BLOCK
    blocks = tuple(jnp.moveaxis(q.reshape(b, nb, Q_BLOCK, *q.shape[2:]), 1, 0) for q in qs)
    out = lax.map(lambda qb: fn(*qb), blocks)
    return jnp.moveaxis(out, 0, 1).reshape(b, t, *out.shape[3:])


def _softmax_attend(q, k, v, scale):
    s = jnp.einsum('bqhd,bkhd->bhqk', q, k).astype(jnp.float32) * scale
    p = jax.nn.softmax(s, axis=-1).astype(v.dtype)
    return jnp.einsum('bhqk,bkhv->bqhv', p, v)


def _diff_attend(q1, q2, k1, k2, v, lam, scale):
    p1 = jax.nn.softmax(jnp.einsum('bqhd,bkhd->bhqk', q1, k1).astype(jnp.float32) * scale, axis=-1)
    p2 = jax.nn.softmax(jnp.einsum('bqhd,bkhd->bhqk', q2, k2).astype(jnp.float32) * scale, axis=-1)
    return jnp.einsum('bhqk,bkhv->bqhv', (p1 - lam * p2).astype(v.dtype), v)


def _mla_queries(q_lat, lp, cs):
    b, t, _ = q_lat.shape
    q = (_rms(q_lat, lp['g_mla_qlat']) @ lp['w_mla_uq']).reshape(b, t, H_A, NOPE_A + ROPE_A)
    q = _rms(q, lp['g_mla_qn'])
    if cs is None:
        return q
    return jnp.concatenate([q[..., :NOPE_A], _apply_rope(q[..., NOPE_A:], cs)], axis=-1)


def _mla_keys_values(ckv, k_rope, lp, cs):
    b, t, _ = ckv.shape
    kv = (ckv @ lp['w_mla_ukv']).reshape(b, t, H_A, NOPE_A + V_A)
    k = jnp.concatenate([kv[..., :NOPE_A], jnp.broadcast_to(k_rope[:, :, None, :], (b, t, H_A, ROPE_A))], axis=-1)
    k = _rms(k, lp['g_mla_kn'])
    if cs is not None:
        k = jnp.concatenate([k[..., :NOPE_A], _apply_rope(k[..., NOPE_A:], cs)], axis=-1)
    return k, kv[..., NOPE_A:]


def _conv_centred(x, w, bias):
    left = CONV_W // 2
    out = lax.conv_general_dilated(x, w[:, None, :], window_strides=(1,), padding=[(left, CONV_W - 1 - left)],
                                   dimension_numbers=('NWC', 'WIO', 'NWC'), feature_group_count=x.shape[-1])
    return out + bias


def _lin_combine(c1, c2):
    a1, b1 = c1
    a2, b2 = c2
    return a1 * a2, a2 * b1 + b2


def _rglru_scan(x, w_a, b_a, w_x, b_x, lam, h0, reverse):
    b, t, _ = x.shape
    xf = x.astype(jnp.float32)
    xb = xf.reshape(b, t, NB_B, BW_B)
    r = jax.nn.sigmoid(jnp.einsum('btnc,ncd->btnd', xb, w_a.astype(jnp.float32)).reshape(b, t, W_B) + b_a.astype(jnp.float32))
    i = jax.nn.sigmoid(jnp.einsum('btnc,ncd->btnd', xb, w_x.astype(jnp.float32)).reshape(b, t, W_B) + b_x.astype(jnp.float32))
    log_a = -RG_C * r * jax.nn.softplus(-lam.astype(jnp.float32))
    a = jnp.exp(log_a)
    u = jnp.sqrt(-jnp.expm1(2.0 * log_a)) * (i * xf)
    edge = -1 if reverse else 0
    u = u.at[:, edge].add(a[:, edge] * h0.astype(jnp.float32))
    _, h = lax.associative_scan(_lin_combine, (a, u), axis=1, reverse=reverse)
    return h


def _mlstm_scan(q, k, v, log_i, log_f, state):
    b, t, h, d = q.shape
    nc = t // CHUNK
    tril = jnp.tril(jnp.ones((CHUNK, CHUNK), dtype=bool))

    def chunks(z):
        return jnp.moveaxis(z.reshape(b, nc, CHUNK, *z.shape[2:]), 1, 0)

    def step(carry, xs):
        c_mem, n_mem, m_mem = carry
        qc, kc, vc, li, lf = xs
        cum = jnp.cumsum(lf, axis=1).transpose(0, 2, 1)
        li_t = li.transpose(0, 2, 1)
        log_d = jnp.where(tril, cum[..., :, None] - cum[..., None, :] + li_t[..., None, :], -jnp.inf)
        inter = cum + m_mem[..., None]
        m_row = jnp.maximum(inter, jnp.max(log_d, axis=-1))
        dmat = jnp.exp(log_d - m_row[..., None])
        w_inter = jnp.exp(inter - m_row)
        s = jnp.einsum('blhd,bshd->bhls', qc, kc) * dmat
        num = (jnp.einsum('bhls,bshv->blhv', s, vc)
               + jnp.einsum('blhd,bhdv->blhv', qc, c_mem) * w_inter.transpose(0, 2, 1)[..., None])
        den = jnp.sum(s, axis=-1) + w_inter * jnp.einsum('blhd,bhd->bhl', qc, n_mem)
        den = jnp.maximum(jnp.abs(den), jnp.exp(-m_row))
        h_out = num / den.transpose(0, 2, 1)[..., None]
        last = cum[..., -1]
        w_s = last[..., None] - cum + li_t
        m_new = jnp.maximum(last + m_mem, jnp.max(w_s, axis=-1))
        decay = jnp.exp(last + m_mem - m_new)
        ws = jnp.exp(w_s - m_new[..., None])
        c_new = decay[..., None, None] * c_mem + jnp.einsum('bhs,bshd,bshv->bhdv', ws, kc, vc)
        n_new = decay[..., None] * n_mem + jnp.einsum('bhs,bshd->bhd', ws, kc)
        return (c_new, n_new, m_new), h_out

    final, hs = lax.scan(step, state, tuple(chunks(z) for z in (q, k, v, log_i, log_f)))
    return jnp.moveaxis(hs, 0, 1).reshape(b, t, h, d), final


def _token_mixers(xn, lp, l, cs, ctx):
    b, t, _ = xn.shape
    dt = xn.dtype
    f32 = jnp.float32
    (q_lat, kv_lat, k_rope, rg_x, rg_gate, mq, mk, mv, mo, mi, mf,
     dq, dk, dv, gate_pre) = jnp.split(xn @ lp['w_in'], IN_OFFSETS, axis=-1)

    cs_a = None if cs is None else cs['a']
    ckv = _rms(kv_lat, lp['g_mla_kvlat'])
    qa = _mla_queries(q_lat, lp, cs_a)
    ka, va = _mla_keys_values(ckv, k_rope, lp, cs_a)
    if ctx is not None:
        kc, vc = _mla_keys_values(ctx['mla_ckv'], ctx['mla_krope'], lp, None)
        ka = jnp.concatenate([kc, ka], axis=1)
        va = jnp.concatenate([vc, va], axis=1)
    ya = _by_query_blocks(lambda qb: _softmax_attend(qb, ka, va, MLA_SCALE), qa).reshape(b, t, W_BRANCH)

    xc = _conv_centred(rg_x, lp['w_conv_rg'], lp['b_conv_rg'])
    h0 = jnp.zeros((b, 2, W_B), f32) if ctx is None else ctx['rglru']
    hf = _rglru_scan(xc, lp['w_rg_a'][0], lp['b_rg_a'][0], lp['w_rg_x'][0], lp['b_rg_x'][0], lp['rg_lambda'][0], h0[:, 0], False)
    hb = _rglru_scan(xc, lp['w_rg_a'][1], lp['b_rg_a'][1], lp['w_rg_x'][1], lp['b_rg_x'][1], lp['rg_lambda'][1], h0[:, 1], True)
    yb = (jax.nn.gelu(rg_gate.astype(f32)) * (hf + hb)).astype(dt)

    qm = mq.reshape(b, t, H_C, DH_C).astype(f32) * (DH_C ** -0.5)
    km = mk.reshape(b, t, H_C, DH_C).astype(f32)
    vm = mv.reshape(b, t, H_C, DH_C).astype(f32)
    li = mi.reshape(b, t, 2, H_C).astype(f32) + lp['b_ml_i'].astype(f32)
    lf = jax.nn.log_sigmoid(mf.reshape(b, t, 2, H_C).astype(f32) + lp['b_ml_f'].astype(f32))
    if ctx is None:
        c0 = jnp.zeros((b, 2, H_C, DH_C, DH_C), f32)
        n0 = jnp.zeros((b, 2, H_C, DH_C), f32)
        m0 = jnp.zeros((b, 2, H_C), f32)
    else:
        c0 = ctx['mlstm_C'].astype(f32)
        n0 = ctx['mlstm_n'].astype(f32)
        m0 = ctx['mlstm_m'].astype(f32)
    rev = lambda z: jnp.flip(z, axis=1)
    hm_f, st_f = _mlstm_scan(qm, km, vm, li[:, :, 0], lf[:, :, 0], (c0[:, 0], n0[:, 0], m0[:, 0]))
    hm_b, st_b = _mlstm_scan(rev(qm), rev(km), rev(vm), rev(li[:, :, 1]), rev(lf[:, :, 1]), (c0[:, 1], n0[:, 1], m0[:, 1]))
    hm = hm_f + rev(hm_b)
    yc = (jax.nn.sigmoid(mo.astype(f32)) * _rms(hm, lp['g_ml_out']).reshape(b, t, W_BRANCH)).astype(dt)

    lambda_init = 0.8 - 0.6 * math.exp(-0.3 * l)
    qd = _rms(dq.reshape(b, t, 2 * H_D, DH_D), lp['g_diff_qn'])
    kd = _rms(dk.reshape(b, t, 2 * H_D, DH_D), lp['g_diff_kn'])
    vd = dv.reshape(b, t, H_D, 2 * DH_D)
    kd_own = kd
    if cs is not None:
        qd = _apply_rope(qd, cs['d'])
        kd_own = _apply_rope(kd, cs['d'])
    qd = qd.reshape(b, t, 2, H_D, DH_D)
    kd_all = kd_own.reshape(b, t, 2, H_D, DH_D)
    vd_all = vd
    if ctx is not None:
        kd_all = jnp.concatenate([ctx['diff_k'], kd_all], axis=1)
        vd_all = jnp.concatenate([ctx['diff_v'], vd_all], axis=1)
    dl = lp['diff_lambda'].astype(f32)
    lam = jnp.exp(jnp.sum(dl[0] * dl[1])) - jnp.exp(jnp.sum(dl[2] * dl[3])) + lambda_init
    k1, k2 = kd_all[:, :, 0], kd_all[:, :, 1]
    yd = _by_query_blocks(lambda q1b, q2b: _diff_attend(q1b, q2b, k1, k2, vd_all, lam, DIFF_SCALE), qd[:, :, 0], qd[:, :, 1])
    yd = (_rms(yd, lp['g_diff_sub']) * (1.0 - lambda_init)).reshape(b, t, W_BRANCH).astype(dt)

    y_all = jnp.stack([ya, yb, yc, yd], axis=2)
    proj = jnp.einsum('btgw,gwd->btgd', y_all, lp['w_br'])
    gates = jax.nn.sigmoid(gate_pre.reshape(b, t, N_BRANCH, D_MODEL))
    y = jnp.einsum('btgd,btgd->btd', gates, proj) @ lp['w_out']

    if ctx is not None:
        return y, None
    ctx_out = (ckv, k_rope, kd.reshape(b, t, 2, H_D, DH_D), vd,
               jnp.stack([hf[:, -1], hb[:, 0]], axis=1),
               jnp.stack([st_f[0], st_b[0]], axis=1),
               jnp.stack([st_f[1], st_b[1]], axis=1),
               jnp.stack([st_f[2], st_b[2]], axis=1))
    return y, ctx_out


def _swiglu(x, wg, wu, wd):
    return (jax.nn.silu(x @ wg) * (x @ wu)) @ wd


def _moe(x, w_router, wg, wu, wd):
    logits = (x @ w_router).astype(jnp.float32)
    top_v, top_i = lax.top_k(logits, TOP_K)
    gate = jax.nn.softmax(top_v, axis=-1)
    combine = jnp.einsum('btk,btke->bte', gate, jax.nn.one_hot(top_i, N_EXP, dtype=jnp.float32)).astype(x.dtype)
    out = combine[..., 0:1] * _swiglu(x, wg[0], wu[0], wd[0])
    for e in range(1, N_EXP):
        out = out + combine[..., e:e + 1] * _swiglu(x, wg[e], wu[e], wd[e])
    return out


def _layer(x, cond, lp, l, cs, ctx):
    mod = (jax.nn.silu(cond) @ lp['w_mod'] + lp['b_mod']).reshape(cond.shape[0], 6, D_MODEL)[:, :, None, :]
    sh1, sc1, g1, sh2, sc2, g2 = (mod[:, i] for i in range(6))
    xn = _rms(x, lp['g_norm_mix']) * (1.0 + sc1) + sh1
    y, ctx_out = _token_mixers(xn, lp, l, cs, ctx)
    x = x + g1 * y
    xn = _rms(x, lp['g_norm_ffn']) * (1.0 + sc2) + sh2
    f = _swiglu(xn, *lp['ffn']) if 'ffn' in lp else _moe(xn, *lp['moe'])
    return x + g2 * f, ctx_out


def setup_inputs(seed: int = 0) -> dict:
    key = jax.random.key(seed)
    ks = iter(jax.random.split(key, 64))
    f32 = jnp.float32
    nrm = lambda shape, s=1.0: s * jax.random.normal(next(ks), shape, f32)
    gain = lambda shape: 1.0 + 0.1 * jax.random.normal(next(ks), shape, f32)
    D = D_MODEL
    s_lam = jax.random.uniform(next(ks), (DEPTH, 2, W_B), f32, 0.9, 0.999) ** (1.0 / RG_C)
    rg_lambda = jnp.log(s_lam) - jnp.log1p(-s_lam)
    b_ml_f = 3.0 + 3.0 * jax.random.uniform(next(ks), (DEPTH, 2, H_C), f32)
    return {
        'x_prompt': nrm((BATCH, SEQ, D)),
        'x_sample': nrm((DEC_BATCH, DEC_SEQ, D)),
        'cache_mla_ckv': nrm((DEC_BATCH, DEPTH, PAST_LEN, KV_RANK)),
        'cache_mla_krope': nrm((DEC_BATCH, DEPTH, PAST_LEN, ROPE_A)),
        'cache_diff_k': nrm((DEC_BATCH, DEPTH, PAST_LEN, 2, H_D, DH_D)),
        'cache_diff_v': nrm((DEC_BATCH, DEPTH, PAST_LEN, H_D, 2 * DH_D)),
        'state_rglru': nrm((DEC_BATCH, DEPTH, 2, W_B), 0.5),
        'state_mlstm_C': nrm((DEC_BATCH, DEPTH, 2, H_C, DH_C, DH_C), 0.5),
        'state_mlstm_n': nrm((DEC_BATCH, DEPTH, 2, H_C, DH_C), 0.5),
        'state_mlstm_m': nrm((DEC_BATCH, DEPTH, 2, H_C), 0.5),
        'c': nrm((DEC_BATCH, D)),
        'c_ctx': nrm((D,)),
        'w_mod': nrm((DEPTH, D, 6 * D), 0.5 * D ** -0.5),
        'b_mod': nrm((DEPTH, 6 * D), 0.02),
        'g_norm_mix': gain((DEPTH, D)),
        'g_norm_ffn': gain((DEPTH, D)),
        'w_in': nrm((DEPTH, D, N_IN), D ** -0.5),
        'g_mla_qlat': gain((DEPTH, Q_RANK)),
        'w_mla_uq': nrm((DEPTH, Q_RANK, H_A * (NOPE_A + ROPE_A)), Q_RANK ** -0.5),
        'g_mla_kvlat': gain((DEPTH, KV_RANK)),
        'w_mla_ukv': nrm((DEPTH, KV_RANK, H_A * (NOPE_A + V_A)), KV_RANK ** -0.5),
        'g_mla_qn': gain((DEPTH, NOPE_A + ROPE_A)),
        'g_mla_kn': gain((DEPTH, NOPE_A + ROPE_A)),
        'w_conv_rg': nrm((DEPTH, CONV_W, W_B), 0.5),
        'b_conv_rg': nrm((DEPTH, W_B), 0.02),
        'w_rg_a': nrm((DEPTH, 2, NB_B, BW_B, BW_B), BW_B ** -0.5),
        'b_rg_a': nrm((DEPTH, 2, W_B), 0.02),
        'w_rg_x': nrm((DEPTH, 2, NB_B, BW_B, BW_B), BW_B ** -0.5),
        'b_rg_x': nrm((DEPTH, 2, W_B), 0.02),
        'rg_lambda': rg_lambda,
        'b_ml_i': nrm((DEPTH, 2, H_C), 0.1),
        'b_ml_f': b_ml_f,
        'g_ml_out': gain((DEPTH, DH_C)),
        'g_diff_qn': gain((DEPTH, DH_D)),
        'g_diff_kn': gain((DEPTH, DH_D)),
        'diff_lambda': nrm((DEPTH, 4, DH_D), 0.1),
        'g_diff_sub': gain((DEPTH, 2 * DH_D)),
        'w_br': nrm((DEPTH, N_BRANCH, W_BRANCH, D), W_BRANCH ** -0.5),
        'w_out': nrm((DEPTH, D, D), D ** -0.5),
        'w_ffn_gate': nrm((N_DENSE, D, F_DENSE), D ** -0.5),
        'w_ffn_up': nrm((N_DENSE, D, F_DENSE), D ** -0.5),
        'w_ffn_down': nrm((N_DENSE, F_DENSE, D), F_DENSE ** -0.5),
        'w_router': nrm((N_MOE, D, N_EXP), D ** -0.5),
        'w_moe_gate': nrm((N_MOE, N_EXP, D, F_EXP), D ** -0.5),
        'w_moe_up': nrm((N_MOE, N_EXP, D, F_EXP), D ** -0.5),
        'w_moe_down': nrm((N_MOE, N_EXP, F_EXP, D), F_EXP ** -0.5),
    }


def reference(x_prompt, x_sample, cache_mla_ckv, cache_mla_krope, cache_diff_k, cache_diff_v,
              state_rglru, state_mlstm_C, state_mlstm_n, state_mlstm_m, c, c_ctx,
              w_mod, b_mod, g_norm_mix, g_norm_ffn, w_in, g_mla_qlat, w_mla_uq, g_mla_kvlat, w_mla_ukv,
              g_mla_qn, g_mla_kn, w_conv_rg, b_conv_rg, w_rg_a, b_rg_a, w_rg_x, b_rg_x, rg_lambda,
              b_ml_i, b_ml_f, g_ml_out, g_diff_qn, g_diff_kn, diff_lambda, g_diff_sub, w_br, w_out,
              w_ffn_gate, w_ffn_up, w_ffn_down, w_router, w_moe_gate, w_moe_up, w_moe_down):
    n_lat = x_sample.shape[1]
    cs_lat = {'a': _axial_rope(n_lat, ROPE_A), 'd': _axial_rope(n_lat, DH_D)}
    cond_ctx = jnp.broadcast_to(c_ctx, (x_prompt.shape[0], D_MODEL))
    h_ctx, h_lat = x_prompt, x_sample
    new = []
    for l in range(DEPTH):
        lp = dict(w_mod=w_mod[l], b_mod=b_mod[l], g_norm_mix=g_norm_mix[l], g_norm_ffn=g_norm_ffn[l], w_in=w_in[l],
                  g_mla_qlat=g_mla_qlat[l], w_mla_uq=w_mla_uq[l], g_mla_kvlat=g_mla_kvlat[l], w_mla_ukv=w_mla_ukv[l],
                  g_mla_qn=g_mla_qn[l], g_mla_kn=g_mla_kn[l], w_conv_rg=w_conv_rg[l], b_conv_rg=b_conv_rg[l],
                  w_rg_a=w_rg_a[l], b_rg_a=b_rg_a[l], w_rg_x=w_rg_x[l], b_rg_x=b_rg_x[l], rg_lambda=rg_lambda[l],
                  b_ml_i=b_ml_i[l], b_ml_f=b_ml_f[l], g_ml_out=g_ml_out[l], g_diff_qn=g_diff_qn[l],
                  g_diff_kn=g_diff_kn[l], diff_lambda=diff_lambda[l], g_diff_sub=g_diff_sub[l],
                  w_br=w_br[l], w_out=w_out[l])
        if l % 2 == 0:
            lp['ffn'] = (w_ffn_gate[l // 2], w_ffn_up[l // 2], w_ffn_down[l // 2])
        else:
            lp['moe'] = (w_router[l // 2], w_moe_gate[l // 2], w_moe_up[l // 2], w_moe_down[l // 2])
        ctx_l = dict(mla_ckv=cache_mla_ckv[:, l], mla_krope=cache_mla_krope[:, l], diff_k=cache_diff_k[:, l],
                     diff_v=cache_diff_v[:, l], rglru=state_rglru[:, l], mlstm_C=state_mlstm_C[:, l],
                     mlstm_n=state_mlstm_n[:, l], mlstm_m=state_mlstm_m[:, l])
        h_ctx, st = _layer(h_ctx, cond_ctx, lp, l, None, None)
        h_lat, _ = _layer(h_lat, c, lp, l, cs_lat, ctx_l)
        new.append(st)
    new_mla_ckv = jnp.stack([s[0] for s in new], axis=1)
    new_mla_krope = jnp.stack([s[1] for s in new], axis=1)
    new_diff_k = jnp.stack([s[2] for s in new], axis=1)
    new_diff_v = jnp.stack([s[3] for s in new], axis=1)
    new_rglru = jnp.stack([s[4] for s in new], axis=1)
    new_mlstm_C = jnp.stack([s[5] for s in new], axis=1)
    new_mlstm_n = jnp.stack([s[6] for s in new], axis=1)
    new_mlstm_m = jnp.stack([s[7] for s in new], axis=1)
    return (h_ctx, h_lat, new_mla_ckv, new_mla_krope, new_diff_k, new_diff_v, new_rglru, new_mlstm_C, new_mlstm_n, new_mlstm_m)
```

```python
import functools
import math

import jax
import jax.numpy as jnp
import numpy as np
from jax import lax
from jax.experimental import pallas as pl
from jax.experimental.pallas import tpu as pltpu

D_MODEL = 1024
DEPTH = 2
GRID_W = 64
Q_BLOCK = 128
ROPE_BASE = 10000.0
EPS = 1e-6
N_BRANCH = 4
W_BRANCH = D_MODEL // 2

H_A = 8
NOPE_A = 64
ROPE_A = 32
V_A = W_BRANCH // H_A
Q_RANK = D_MODEL // 4
KV_RANK = D_MODEL // 8
MLA_SCALE = (NOPE_A + ROPE_A) ** -0.5

W_B = W_BRANCH
NB_B = 8
BW_B = W_B // NB_B
CONV_W = 4
RG_C = 8.0

H_C = 4
DH_C = W_BRANCH // H_C
CHUNK = 64

H_D = 4
DH_D = W_BRANCH // (2 * H_D)
DIFF_SCALE = DH_D ** -0.5

N_EXP = 8
TOP_K = 2

IN_SPLITS = (Q_RANK, KV_RANK, ROPE_A,
             W_B, W_B,
             W_BRANCH, W_BRANCH, W_BRANCH, W_BRANCH, 2 * H_C, 2 * H_C,
             2 * H_D * DH_D, 2 * H_D * DH_D, 2 * H_D * DH_D,
             N_BRANCH * D_MODEL)
IN_OFFSETS = tuple(int(v) for v in np.cumsum(IN_SPLITS)[:-1])
N_IN = int(sum(IN_SPLITS))

V7X_VMEM_LIMIT_BYTES = 56 * 1024 * 1024

BF16 = jnp.bfloat16
F32 = jnp.float32


def _round_up(x, m):
    return (x + m - 1) // m * m


def _mm_kernel(x_ref, w_ref, o_ref):
    o_ref[...] = jnp.dot(x_ref[...].astype(BF16), w_ref[...].astype(BF16),
                         preferred_element_type=F32).astype(o_ref.dtype)


def pmatmul(x, w, out_dtype=F32, tm=1024, tn=512):
    m, k = x.shape
    _, n = w.shape
    tm = min(tm, m)
    tn = min(tn, n)
    grid = (pl.cdiv(n, tn), pl.cdiv(m, tm))
    return pl.pallas_call(
        _mm_kernel,
        grid=grid,
        in_specs=[pl.BlockSpec((tm, k), lambda j, i: (i, 0)),
                  pl.BlockSpec((k, tn), lambda j, i: (0, j))],
        out_specs=pl.BlockSpec((tm, tn), lambda j, i: (i, j)),
        out_shape=jax.ShapeDtypeStruct((m, n), out_dtype),
        compiler_params=pltpu.CompilerParams(
            dimension_semantics=("arbitrary", "arbitrary"),
            vmem_limit_bytes=V7X_VMEM_LIMIT_BYTES),
        name="dense_matmul",
    )(x, w)


def _ffn_up_kernel(te_ref, nt_ref, x_ref, wg_ref, wu_ref, h_ref):
    i = pl.program_id(1)

    @pl.when(i < nt_ref[0])
    def _():
        x = x_ref[...]
        g = jnp.dot(x, wg_ref[0].astype(BF16), preferred_element_type=F32)
        u = jnp.dot(x, wu_ref[0].astype(BF16), preferred_element_type=F32)
        h_ref[...] = (g * jax.nn.sigmoid(g) * u).astype(h_ref.dtype)


def ffn_up(tile_expert, n_tiles, xs, wg, wu, tm, tf):
    r, d = xs.shape
    _, _, f = wg.shape
    grid = (f // tf, r // tm)
    return pl.pallas_call(
        _ffn_up_kernel,
        grid_spec=pltpu.PrefetchScalarGridSpec(
            num_scalar_prefetch=2,
            grid=grid,
            in_specs=[pl.BlockSpec((tm, d), lambda j, i, te, nt: (i, 0)),
                      pl.BlockSpec((1, d, tf), lambda j, i, te, nt: (te[i], 0, j)),
                      pl.BlockSpec((1, d, tf), lambda j, i, te, nt: (te[i], 0, j))],
            out_specs=pl.BlockSpec((tm, tf), lambda j, i, te, nt: (i, j))),
        out_shape=jax.ShapeDtypeStruct((r, f), BF16),
        compiler_params=pltpu.CompilerParams(
            dimension_semantics=("arbitrary", "arbitrary"),
            vmem_limit_bytes=V7X_VMEM_LIMIT_BYTES),
        name="ffn_up",
    )(tile_expert, n_tiles, xs, wg, wu)


def _ffn_down_kernel(te_ref, nt_ref, h_ref, wd_ref, s_ref, y_ref):
    i = pl.program_id(1)

    @pl.when(i < nt_ref[0])
    def _():
        y = jnp.dot(h_ref[...], wd_ref[0].astype(BF16), preferred_element_type=F32)
        y_ref[...] = (y * s_ref[...]).astype(y_ref.dtype)


def ffn_down(tile_expert, n_tiles, h, wd, row_scale, tm, tn, out_dtype=F32):
    r, f = h.shape
    _, _, d = wd.shape
    grid = (d // tn, r // tm)
    return pl.pallas_call(
        _ffn_down_kernel,
        grid_spec=pltpu.PrefetchScalarGridSpec(
            num_scalar_prefetch=2,
            grid=grid,
            in_specs=[pl.BlockSpec((tm, f), lambda j, i, te, nt: (i, 0)),
                      pl.BlockSpec((1, f, tn), lambda j, i, te, nt: (te[i], 0, j)),
                      pl.BlockSpec((tm, 1), lambda j, i, te, nt: (i, 0))],
            out_specs=pl.BlockSpec((tm, tn), lambda j, i, te, nt: (i, j))),
        out_shape=jax.ShapeDtypeStruct((r, d), out_dtype),
        compiler_params=pltpu.CompilerParams(
            dimension_semantics=("arbitrary", "arbitrary"),
            vmem_limit_bytes=V7X_VMEM_LIMIT_BYTES),
        name="ffn_down",
    )(tile_expert, n_tiles, h, wd, row_scale)


def dense_swiglu(x2d, wg, wu, wd):
    t = x2d.shape[0]
    tm = 1024
    n_tiles = t // tm
    te = jnp.zeros((n_tiles,), jnp.int32)
    nt = jnp.full((1,), n_tiles, jnp.int32)
    h = ffn_up(te, nt, x2d.astype(BF16), wg[None], wu[None], tm=tm, tf=256)
    ones = jnp.ones((t, 1), F32)
    return ffn_down(te, nt, h, wd[None], ones, tm=tm, tn=512)


MOE_TM = 512


def moe_swiglu(x2d, w_router, wg, wu, wd):
    t, d = x2d.shape
    tm = MOE_TM
    logits = jnp.dot(x2d, w_router, precision=lax.Precision.HIGHEST).astype(F32)
    top_v, top_i = lax.top_k(logits, TOP_K)
    gate = jax.nn.softmax(top_v, axis=-1)

    flat_e = top_i.reshape(-1).astype(jnp.int32)
    flat_g = gate.reshape(-1)
    order = jnp.argsort(flat_e, stable=True)
    sorted_e = flat_e[order]
    counts = jnp.sum(jax.nn.one_hot(flat_e, N_EXP, dtype=jnp.int32), axis=0)
    padded = (counts + tm - 1) // tm * tm
    grp_start = jnp.cumsum(padded) - padded
    raw_start = jnp.cumsum(counts) - counts
    rank = jnp.arange(t * TOP_K, dtype=jnp.int32) - raw_start[sorted_e]
    dest = grp_start[sorted_e] + rank
    r_max = t * TOP_K + N_EXP * tm
    n_tiles_max = r_max // tm
    tok_of_slot = jnp.zeros((r_max,), jnp.int32).at[dest].set((order // TOP_K).astype(jnp.int32))
    gate_of_slot = jnp.zeros((r_max,), F32).at[dest].set(flat_g[order])
    slot_of_assign = jnp.zeros((t * TOP_K,), jnp.int32).at[order].set(dest)

    tile_start = jnp.arange(n_tiles_max, dtype=jnp.int32) * tm
    grp_end = grp_start + padded
    tile_expert = jnp.sum((tile_start[:, None] >= grp_end[None, :]).astype(jnp.int32), axis=1)
    tile_expert = jnp.minimum(tile_expert, N_EXP - 1).astype(jnp.int32)
    n_tiles = (jnp.sum(padded) // tm).astype(jnp.int32).reshape(1)

    xs = jnp.take(x2d.astype(BF16), tok_of_slot, axis=0)
    h = ffn_up(tile_expert, n_tiles, xs, wg, wu, tm=tm, tf=512)
    ys = ffn_down(tile_expert, n_tiles, h, wd, gate_of_slot[:, None], tm=tm, tn=512)
    pair = jnp.take(ys, slot_of_assign, axis=0).reshape(t, TOP_K, d)
    return pair[:, 0] + pair[:, 1]


def _rms(x, g):
    xf = x.astype(F32)
    y = xf * lax.rsqrt(jnp.mean(xf * xf, axis=-1, keepdims=True) + EPS)
    return (y * g.astype(F32)).astype(x.dtype)


def _axial_rope(n_tokens, rot_dim):
    rows = n_tokens // GRID_W
    r, c = jnp.meshgrid(jnp.arange(rows, dtype=F32), jnp.arange(GRID_W, dtype=F32), indexing='ij')
    n_freq = rot_dim // 4
    inv = ROPE_BASE ** (-jnp.arange(n_freq, dtype=F32) / n_freq)
    ang = jnp.stack([r.reshape(-1)[:, None] * inv, c.reshape(-1)[:, None] * inv], axis=1)
    return jnp.cos(ang), jnp.sin(ang)


def _apply_rope(x, cs):
    cos, sin = cs
    b, t, h, rd = x.shape
    nf = rd // 4
    xr = x.reshape(b, t, h, 2, 2, nf)
    x1, x2 = xr[..., 0, :], xr[..., 1, :]
    cos = cos[:, None].astype(x.dtype)
    sin = sin[:, None].astype(x.dtype)
    out = jnp.stack([x1 * cos - x2 * sin, x1 * sin + x2 * cos], axis=-2)
    return out.reshape(b, t, h, rd)


def _by_query_blocks(fn, *qs):
    b, t = qs[0].shape[:2]
    nb = t // Q_BLOCK
    blocks = tuple(jnp.moveaxis(q.reshape(b, nb, Q_BLOCK, *q.shape[2:]), 1, 0) for q in qs)
    out = lax.map(lambda qb: fn(*qb), blocks)
    return jnp.moveaxis(out, 0, 1).reshape(b, t, *out.shape[3:])


def _softmax_attend(q, k, v, scale):
    s = jnp.einsum('bqhd,bkhd->bhqk', q, k).astype(F32) * scale
    p = jax.nn.softmax(s, axis=-1).astype(v.dtype)
    return jnp.einsum('bhqk,bkhv->bqhv', p, v)


def _diff_attend(q1, q2, k1, k2, v, lam, scale):
    p1 = jax.nn.softmax(jnp.einsum('bqhd,bkhd->bhqk', q1, k1).astype(F32) * scale, axis=-1)
    p2 = jax.nn.softmax(jnp.einsum('bqhd,bkhd->bhqk', q2, k2).astype(F32) * scale, axis=-1)
    return jnp.einsum('bhqk,bkhv->bqhv', (p1 - lam * p2).astype(v.dtype), v)


def _mla_queries(q_lat, lp, cs):
    b, t, _ = q_lat.shape
    q = (_rms(q_lat, lp['g_mla_qlat']) @ lp['w_mla_uq']).reshape(b, t, H_A, NOPE_A + ROPE_A)
    q = _rms(q, lp['g_mla_qn'])
    if cs is None:
        return q
    return jnp.concatenate([q[..., :NOPE_A], _apply_rope(q[..., NOPE_A:], cs)], axis=-1)


def _mla_keys_values(ckv, k_rope, lp, cs):
    b, t, _ = ckv.shape
    kv = (ckv @ lp['w_mla_ukv']).reshape(b, t, H_A, NOPE_A + V_A)
    k = jnp.concatenate([kv[..., :NOPE_A], jnp.broadcast_to(k_rope[:, :, None, :], (b, t, H_A, ROPE_A))], axis=-1)
    k = _rms(k, lp['g_mla_kn'])
    if cs is not None:
        k = jnp.concatenate([k[..., :NOPE_A], _apply_rope(k[..., NOPE_A:], cs)], axis=-1)
    return k, kv[..., NOPE_A:]


def _conv_centred(x, w, bias):
    left = CONV_W // 2
    out = lax.conv_general_dilated(x, w[:, None, :], window_strides=(1,), padding=[(left, CONV_W - 1 - left)],
                                   dimension_numbers=('NWC', 'WIO', 'NWC'), feature_group_count=x.shape[-1])
    return out + bias


def _lin_combine(c1, c2):
    a1, b1 = c1
    a2, b2 = c2
    return a1 * a2, a2 * b1 + b2


def _rglru_scan(x, w_a, b_a, w_x, b_x, lam, h0, reverse):
    b, t, _ = x.shape
    xf = x.astype(F32)
    xb = xf.reshape(b, t, NB_B, BW_B)
    r = jax.nn.sigmoid(jnp.einsum('btnc,ncd->btnd', xb, w_a.astype(F32)).reshape(b, t, W_B) + b_a.astype(F32))
    i = jax.nn.sigmoid(jnp.einsum('btnc,ncd->btnd', xb, w_x.astype(F32)).reshape(b, t, W_B) + b_x.astype(F32))
    log_a = -RG_C * r * jax.nn.softplus(-lam.astype(F32))
    a = jnp.exp(log_a)
    u = jnp.sqrt(-jnp.expm1(2.0 * log_a)) * (i * xf)
    edge = -1 if reverse else 0
    u = u.at[:, edge].add(a[:, edge] * h0.astype(F32))
    _, h = lax.associative_scan(_lin_combine, (a, u), axis=1, reverse=reverse)
    return h


def _mlstm_scan(q, k, v, log_i, log_f, state):
    b, t, h, d = q.shape
    nc = t // CHUNK
    tril = jnp.tril(jnp.ones((CHUNK, CHUNK), dtype=bool))

    def chunks(z):
        return jnp.moveaxis(z.reshape(b, nc, CHUNK, *z.shape[2:]), 1, 0)

    def step(carry, xs):
        c_mem, n_mem, m_mem = carry
        qc, kc, vc, li, lf = xs
        cum = jnp.cumsum(lf, axis=1).transpose(0, 2, 1)
        li_t = li.transpose(0, 2, 1)
        log_d = jnp.where(tril, cum[..., :, None] - cum[..., None, :] + li_t[..., None, :], -jnp.inf)
        inter = cum + m_mem[..., None]
        m_row = jnp.maximum(inter, jnp.max(log_d, axis=-1))
        dmat = jnp.exp(log_d - m_row[..., None])
        w_inter = jnp.exp(inter - m_row)
        s = jnp.einsum('blhd,bshd->bhls', qc, kc) * dmat
        num = (jnp.einsum('bhls,bshv->blhv', s, vc)
               + jnp.einsum('blhd,bhdv->blhv', qc, c_mem) * w_inter.transpose(0, 2, 1)[..., None])
        den = jnp.sum(s, axis=-1) + w_inter * jnp.einsum('blhd,bhd->bhl', qc, n_mem)
        den = jnp.maximum(jnp.abs(den), jnp.exp(-m_row))
        h_out = num / den.transpose(0, 2, 1)[..., None]
        last = cum[..., -1]
        w_s = last[..., None] - cum + li_t
        m_new = jnp.maximum(last + m_mem, jnp.max(w_s, axis=-1))
        decay = jnp.exp(last + m_mem - m_new)
        ws = jnp.exp(w_s - m_new[..., None])
        c_new = decay[..., None, None] * c_mem + jnp.einsum('bhs,bshd,bshv->bhdv', ws, kc, vc)
        n_new = decay[..., None] * n_mem + jnp.einsum('bhs,bshd->bhd', ws, kc)
        return (c_new, n_new, m_new), h_out

    final, hs = lax.scan(step, state, tuple(chunks(z) for z in (q, k, v, log_i, log_f)))
    return jnp.moveaxis(hs, 0, 1).reshape(b, t, h, d), final


def _token_mixers(proj_in, lp, l, cs, ctx):
    b, t, _ = proj_in.shape
    dt = proj_in.dtype
    (q_lat, kv_lat, k_rope, rg_x, rg_gate, mq, mk, mv, mo, mi, mf,
     dq, dk, dv, gate_pre) = jnp.split(proj_in, IN_OFFSETS, axis=-1)

    cs_a = None if cs is None else cs['a']
    ckv = _rms(kv_lat, lp['g_mla_kvlat'])
    qa = _mla_queries(q_lat, lp, cs_a)
    ka, va = _mla_keys_values(ckv, k_rope, lp, cs_a)
    if ctx is not None:
        kc, vc = _mla_keys_values(ctx['mla_ckv'], ctx['mla_krope'], lp, None)
        ka = jnp.concatenate([kc, ka], axis=1)
        va = jnp.concatenate([vc, va], axis=1)
    ya = _by_query_blocks(lambda qb: _softmax_attend(qb, ka, va, MLA_SCALE), qa).reshape(b, t, W_BRANCH)

    xc = _conv_centred(rg_x, lp['w_conv_rg'], lp['b_conv_rg'])
    h0 = jnp.zeros((b, 2, W_B), F32) if ctx is None else ctx['rglru']
    hf = _rglru_scan(xc, lp['w_rg_a'][0], lp['b_rg_a'][0], lp['w_rg_x'][0], lp['b_rg_x'][0], lp['rg_lambda'][0], h0[:, 0], False)
    hb = _rglru_scan(xc, lp['w_rg_a'][1], lp['b_rg_a'][1], lp['w_rg_x'][1], lp['b_rg_x'][1], lp['rg_lambda'][1], h0[:, 1], True)
    yb = (jax.nn.gelu(rg_gate.astype(F32)) * (hf + hb)).astype(dt)

    qm = mq.reshape(b, t, H_C, DH_C).astype(F32) * (DH_C ** -0.5)
    km = mk.reshape(b, t, H_C, DH_C).astype(F32)
    vm = mv.reshape(b, t, H_C, DH_C).astype(F32)
    li = mi.reshape(b, t, 2, H_C).astype(F32) + lp['b_ml_i'].astype(F32)
    lf = jax.nn.log_sigmoid(mf.reshape(b, t, 2, H_C).astype(F32) + lp['b_ml_f'].astype(F32))
    if ctx is None:
        c0 = jnp.zeros((b, 2, H_C, DH_C, DH_C), F32)
        n0 = jnp.zeros((b, 2, H_C, DH_C), F32)
        m0 = jnp.zeros((b, 2, H_C), F32)
    else:
        c0 = ctx['mlstm_C'].astype(F32)
        n0 = ctx['mlstm_n'].astype(F32)
        m0 = ctx['mlstm_m'].astype(F32)
    rev = lambda z: jnp.flip(z, axis=1)
    hm_f, st_f = _mlstm_scan(qm, km, vm, li[:, :, 0], lf[:, :, 0], (c0[:, 0], n0[:, 0], m0[:, 0]))
    hm_b, st_b = _mlstm_scan(rev(qm), rev(km), rev(vm), rev(li[:, :, 1]), rev(lf[:, :, 1]), (c0[:, 1], n0[:, 1], m0[:, 1]))
    hm = hm_f + rev(hm_b)
    yc = (jax.nn.sigmoid(mo.astype(F32)) * _rms(hm, lp['g_ml_out']).reshape(b, t, W_BRANCH)).astype(dt)

    lambda_init = 0.8 - 0.6 * math.exp(-0.3 * l)
    qd = _rms(dq.reshape(b, t, 2 * H_D, DH_D), lp['g_diff_qn'])
    kd = _rms(dk.reshape(b, t, 2 * H_D, DH_D), lp['g_diff_kn'])
    vd = dv.reshape(b, t, H_D, 2 * DH_D)
    kd_own = kd
    if cs is not None:
        qd = _apply_rope(qd, cs['d'])
        kd_own = _apply_rope(kd, cs['d'])
    qd = qd.reshape(b, t, 2, H_D, DH_D)
    kd_all = kd_own.reshape(b, t, 2, H_D, DH_D)
    vd_all = vd
    if ctx is not None:
        kd_all = jnp.concatenate([ctx['diff_k'], kd_all], axis=1)
        vd_all = jnp.concatenate([ctx['diff_v'], vd_all], axis=1)
    dl = lp['diff_lambda'].astype(F32)
    lam = jnp.exp(jnp.sum(dl[0] * dl[1])) - jnp.exp(jnp.sum(dl[2] * dl[3])) + lambda_init
    k1, k2 = kd_all[:, :, 0], kd_all[:, :, 1]
    yd = _by_query_blocks(lambda q1b, q2b: _diff_attend(q1b, q2b, k1, k2, vd_all, lam, DIFF_SCALE), qd[:, :, 0], qd[:, :, 1])
    yd = (_rms(yd, lp['g_diff_sub']) * (1.0 - lambda_init)).reshape(b, t, W_BRANCH).astype(dt)

    branches = tuple(z.reshape(b * t, W_BRANCH) for z in (ya, yb, yc, yd))
    gate_pre = gate_pre.reshape(b * t, N_BRANCH, D_MODEL)
    if ctx is not None:
        return branches, gate_pre, None
    ctx_out = (ckv, k_rope, kd.reshape(b, t, 2, H_D, DH_D), vd,
               jnp.stack([hf[:, -1], hb[:, 0]], axis=1),
               jnp.stack([st_f[0], st_b[0]], axis=1),
               jnp.stack([st_f[1], st_b[1]], axis=1),
               jnp.stack([st_f[2], st_b[2]], axis=1))
    return branches, gate_pre, ctx_out


def _modulation(cond, lp):
    mod = (jax.nn.silu(cond) @ lp['w_mod'] + lp['b_mod']).reshape(cond.shape[0], 6, D_MODEL)[:, :, None, :]
    return tuple(mod[:, i] for i in range(6))


def _layer(x_ctx, x_lat, cond_ctx, cond_lat, lp, l, cs, ctx):
    shapes = (x_ctx.shape, x_lat.shape)
    n_ctx = x_ctx.shape[0] * x_ctx.shape[1]
    mods = (_modulation(cond_ctx, lp), _modulation(cond_lat, lp))
    xs = (x_ctx, x_lat)

    def cat(zs):
        return jnp.concatenate([z.reshape(-1, z.shape[-1]) for z in zs], axis=0)

    def uncat(z):
        return (z[:n_ctx].reshape(shapes[0][0], shapes[0][1], -1), z[n_ctx:].reshape(shapes[1][0], shapes[1][1], -1))

    xn = cat([_rms(x, lp['g_norm_mix']) * (1.0 + m[1]) + m[0] for x, m in zip(xs, mods)])
    p_ctx, p_lat = uncat(pmatmul(xn, lp['w_in']))
    br_c, gp_c, ctx_out = _token_mixers(p_ctx, lp, l, None, None)
    br_l, gp_l, _ = _token_mixers(p_lat, lp, l, cs, ctx)
    gates = jax.nn.sigmoid(jnp.concatenate([gp_c, gp_l], axis=0))
    merged = jnp.zeros((xn.shape[0], D_MODEL), F32)
    for g in range(N_BRANCH):
        pg = pmatmul(jnp.concatenate([br_c[g], br_l[g]], axis=0), lp['w_br'][g])
        merged = merged + gates[:, g] * pg
    y = uncat(pmatmul(merged, lp['w_out']))
    xs = tuple(x + m[2] * yy for x, m, yy in zip(xs, mods, y))
    xn = cat([_rms(x, lp['g_norm_ffn']) * (1.0 + m[4]) + m[3] for x, m in zip(xs, mods)])
    if 'ffn' in lp:
        f = dense_swiglu(xn, *lp['ffn'])
    else:
        f = moe_swiglu(xn, *lp['moe'])
    f = uncat(f)
    xs = tuple(x + m[5] * ff for x, m, ff in zip(xs, mods, f))
    return xs[0], xs[1], ctx_out


def kernel(x_prompt, x_sample, cache_mla_ckv, cache_mla_krope, cache_diff_k, cache_diff_v,
           state_rglru, state_mlstm_C, state_mlstm_n, state_mlstm_m, c, c_ctx,
           w_mod, b_mod, g_norm_mix, g_norm_ffn, w_in, g_mla_qlat, w_mla_uq, g_mla_kvlat, w_mla_ukv,
           g_mla_qn, g_mla_kn, w_conv_rg, b_conv_rg, w_rg_a, b_rg_a, w_rg_x, b_rg_x, rg_lambda,
           b_ml_i, b_ml_f, g_ml_out, g_diff_qn, g_diff_kn, diff_lambda, g_diff_sub, w_br, w_out,
           w_ffn_gate, w_ffn_up, w_ffn_down, w_router, w_moe_gate, w_moe_up, w_moe_down):
    n_lat = x_sample.shape[1]
    cs_lat = {'a': _axial_rope(n_lat, ROPE_A), 'd': _axial_rope(n_lat, DH_D)}
    cond_ctx = jnp.broadcast_to(c_ctx, (x_prompt.shape[0], D_MODEL))
    h_ctx, h_lat = x_prompt, x_sample
    new = []
    for l in range(DEPTH):
        lp = dict(w_mod=w_mod[l], b_mod=b_mod[l], g_norm_mix=g_norm_mix[l], g_norm_ffn=g_norm_ffn[l], w_in=w_in[l],
                  g_mla_qlat=g_mla_qlat[l], w_mla_uq=w_mla_uq[l], g_mla_kvlat=g_mla_kvlat[l], w_mla_ukv=w_mla_ukv[l],
                  g_mla_qn=g_mla_qn[l], g_mla_kn=g_mla_kn[l], w_conv_rg=w_conv_rg[l], b_conv_rg=b_conv_rg[l],
                  w_rg_a=w_rg_a[l], b_rg_a=b_rg_a[l], w_rg_x=w_rg_x[l], b_rg_x=b_rg_x[l], rg_lambda=rg_lambda[l],
                  b_ml_i=b_ml_i[l], b_ml_f=b_ml_f[l], g_ml_out=g_ml_out[l], g_diff_qn=g_diff_qn[l],
                  g_diff_kn=g_diff_kn[l], diff_lambda=diff_lambda[l], g_diff_sub=g_diff_sub[l],
                  w_br=w_br[l], w_out=w_out[l])
        if l % 2 == 0:
            lp['ffn'] = (w_ffn_gate[l // 2], w_ffn_up[l // 2], w_ffn_down[l // 2])
        else:
            lp['moe'] = (w_router[l // 2], w_moe_gate[l // 2], w_moe_up[l // 2], w_moe_down[l // 2])
        ctx_l = dict(mla_ckv=cache_mla_ckv[:, l], mla_krope=cache_mla_krope[:, l], diff_k=cache_diff_k[:, l],
                     diff_v=cache_diff_v[:, l], rglru=state_rglru[:, l], mlstm_C=state_mlstm_C[:, l],
                     mlstm_n=state_mlstm_n[:, l], mlstm_m=state_mlstm_m[:, l])
        h_ctx, h_lat, st = _layer(h_ctx, h_lat, cond_ctx, c, lp, l, cs_lat, ctx_l)
        new.append(st)
    outs = tuple(jnp.stack([s[i] for s in new], axis=1) for i in range(8))
    return (h_ctx, h_lat) + outs
```

```python
import functools
import math

import jax
import jax.numpy as jnp
import numpy as np
from jax import lax
from jax.experimental import pallas as pl
from jax.experimental.pallas import tpu as pltpu

D_MODEL = 1024
DEPTH = 2
GRID_W = 64
ROPE_BASE = 10000.0
EPS = 1e-6
N_BRANCH = 4
W_BRANCH = D_MODEL // 2

H_A = 8
NOPE_A = 64
ROPE_A = 32
V_A = W_BRANCH // H_A
Q_RANK = D_MODEL // 4
KV_RANK = D_MODEL // 8
MLA_SCALE = (NOPE_A + ROPE_A) ** -0.5

W_B = W_BRANCH
NB_B = 8
BW_B = W_B // NB_B
CONV_W = 4
RG_C = 8.0

H_C = 4
DH_C = W_BRANCH // H_C
ML_CHUNK = 128

H_D = 4
DH_D = W_BRANCH // (2 * H_D)
DIFF_SCALE = DH_D ** -0.5

N_EXP = 8
TOP_K = 2

V7X_VMEM_LIMIT_BYTES = 56 * 1024 * 1024
LANES = 128
SUBLANES = 8

BF16 = jnp.bfloat16
F32 = jnp.float32

P_GATE = 0
P_SEG = N_BRANCH * D_MODEL
SEG_NAMES = ('rg_x', 'rg_gate', 'mq', 'mk', 'mv', 'mo', 'dq', 'dk', 'dv')
P_QLAT = P_SEG + 9 * W_BRANCH
P_KVLAT = P_QLAT + Q_RANK
P_SMALL = P_KVLAT + KV_RANK
P_WIDTH = P_SMALL + LANES
SM_MI = ROPE_A
SM_MF = ROPE_A + 2 * H_C


def _seg_block(name):
    return (P_SEG + SEG_NAMES.index(name) * W_BRANCH) // W_BRANCH


N_CTX_SEQ, T_CTX = 16, 256
N_LAT_SEQ, T_LAT = 8, 1024
PAST_LEN = 512
R_CTX = N_CTX_SEQ * T_CTX
R_LAT = N_LAT_SEQ * T_LAT
R_ALL = R_CTX + R_LAT


def _cparams(sem):
    return pltpu.CompilerParams(dimension_semantics=sem, vmem_limit_bytes=V7X_VMEM_LIMIT_BYTES)


def _mod_row_of_tile(i, tm):
    n_ctx_tiles = R_CTX // tm
    per_seq = T_LAT // tm
    return jnp.where(i < n_ctx_tiles, 0, 1 + (i - n_ctx_tiles) // per_seq)


def _pos_block_of_tile(i, tm):
    n_ctx_tiles = R_CTX // tm
    per_seq = T_LAT // tm
    return jnp.where(i < n_ctx_tiles, per_seq, (i - n_ctx_tiles) % per_seq)


def _mod_kernel(c_ref, w_ref, b_ref, o_ref):
    c = c_ref[...]
    s = (c * jax.nn.sigmoid(c)).astype(BF16)
    o_ref[...] = jnp.dot(s, w_ref[...].astype(BF16), preferred_element_type=F32) + b_ref[...]


def modulation(cond, w_mod, b_mod):
    m, d = cond.shape
    n = w_mod.shape[1]
    tn = 1536
    return pl.pallas_call(
        _mod_kernel,
        grid=(n // tn,),
        in_specs=[pl.BlockSpec((m, d), lambda j: (0, 0)),
                  pl.BlockSpec((d, tn), lambda j: (0, j)),
                  pl.BlockSpec((1, tn), lambda j: (0, j))],
        out_specs=pl.BlockSpec((m, tn), lambda j: (0, j)),
        out_shape=jax.ShapeDtypeStruct((m, n), F32),
        compiler_params=_cparams(("arbitrary",)),
        name="modulation",
    )(cond, w_mod, b_mod.reshape(1, n))


def _in_proj_kernel(x_ref, mod_ref, g_ref, w_ref, o_ref, xn_ref):
    @pl.when(pl.program_id(1) == 0)
    def _():
        x = x_ref[...]
        r = lax.rsqrt(jnp.mean(x * x, axis=-1, keepdims=True) + EPS)
        sh = mod_ref[0, 0:1, :]
        sc = mod_ref[0, 1:2, :]
        xn_ref[...] = ((x * r * g_ref[...]) * (1.0 + sc) + sh).astype(BF16)

    o_ref[...] = jnp.dot(xn_ref[...], w_ref[...], preferred_element_type=F32).astype(o_ref.dtype)


def in_proj(x, mod, g, w_p):
    tm, tn = 1024, 1024
    m, d = x.shape
    n = w_p.shape[1]
    return pl.pallas_call(
        _in_proj_kernel,
        grid=(m // tm, n // tn),
        in_specs=[pl.BlockSpec((tm, d), lambda i, j: (i, 0)),
                  pl.BlockSpec((1, 8, d), lambda i, j: (_mod_row_of_tile(i, tm), 0, 0)),
                  pl.BlockSpec((1, d), lambda i, j: (0, 0)),
                  pl.BlockSpec((d, tn), lambda i, j: (0, j))],
        out_specs=pl.BlockSpec((tm, tn), lambda i, j: (i, j)),
        out_shape=jax.ShapeDtypeStruct((m, n), BF16),
        scratch_shapes=[pltpu.VMEM((tm, d), BF16)],
        compiler_params=_cparams(("arbitrary", "arbitrary")),
        name="in_proj",
    )(x, mod, g.reshape(1, d), w_p)


def pack_w_in(w_in):
    offs = np.cumsum((0, Q_RANK, KV_RANK, ROPE_A, W_B, W_B, W_BRANCH, W_BRANCH, W_BRANCH, W_BRANCH, 2 * H_C, 2 * H_C,
                      2 * H_D * DH_D, 2 * H_D * DH_D, 2 * H_D * DH_D, N_BRANCH * D_MODEL))
    names = ('q_lat', 'kv_lat', 'k_rope', 'rg_x', 'rg_gate', 'mq', 'mk', 'mv', 'mo', 'mi', 'mf', 'dq', 'dk', 'dv', 'gate')
    col = {nm: w_in[:, int(offs[k]):int(offs[k + 1])] for k, nm in enumerate(names)}
    pad = jnp.zeros((w_in.shape[0], LANES - ROPE_A - 4 * H_C), w_in.dtype)
    parts = [col['gate']] + [col[nm] for nm in SEG_NAMES] + [col['q_lat'], col['kv_lat'],
                                                              col['k_rope'], col['mi'], col['mf'], pad]
    return jnp.concatenate(parts, axis=1).astype(BF16)


def _rope_tables(rot_dim, lane_starts, tm):
    rows = T_LAT // GRID_W
    r, c = np.meshgrid(np.arange(rows, dtype=np.float32), np.arange(GRID_W, dtype=np.float32), indexing='ij')
    nf = rot_dim // 4
    inv = (np.float32(ROPE_BASE) ** (-np.arange(nf, dtype=np.float32) / np.float32(nf))).astype(np.float32)
    ang = np.stack([r.reshape(-1)[:, None] * inv, c.reshape(-1)[:, None] * inv], axis=1).astype(np.float32)
    cos, sin = np.cos(ang).astype(np.float32), np.sin(ang).astype(np.float32)
    tc = np.ones((T_LAT + tm, LANES), np.float32)
    ta = np.zeros((T_LAT + tm, LANES), np.float32)
    tb = np.zeros((T_LAT + tm, LANES), np.float32)
    for s0 in lane_starts:
        for a in range(2):
            lo = s0 + a * 2 * nf
            tc[:T_LAT, lo:lo + nf] = cos[:, a]
            tc[:T_LAT, lo + nf:lo + 2 * nf] = cos[:, a]
            ta[:T_LAT, lo:lo + nf] = -sin[:, a]
            tb[:T_LAT, lo + nf:lo + 2 * nf] = sin[:, a]
    return jnp.asarray(tc), jnp.asarray(ta), jnp.asarray(tb)


def _rope(x, c, sa, sb, half):
    return x * c + pltpu.roll(x, LANES - half, 1) * sa + pltpu.roll(x, half, 1) * sb


MLA_TM = 512
QK_A = NOPE_A + ROPE_A


def _mla_prep_kernel(*refs, has_q, norm_ckv):
    if has_q:
        (qlat_ref, gq_ref, wuq_ref, gqn_ref, kv_ref, sm_ref, gkv_ref, wkc_ref, wv_ref, gkn_ref,
         c_ref, sa_ref, sb_ref, q_o, k_o, v_o, ckv_o, kr_o) = refs
    else:
        (kv_ref, sm_ref, gkv_ref, wkc_ref, wv_ref, gkn_ref, c_ref, sa_ref, sb_ref, k_o, v_o) = refs
    c, sa, sb = c_ref[...], sa_ref[...], sb_ref[...]

    def heads(z, g, o_ref, scale):
        for h in range(H_A):
            s = z[:, h * LANES:(h + 1) * LANES]
            r = lax.rsqrt(jnp.sum(s * s, axis=-1, keepdims=True) * (1.0 / QK_A) + EPS)
            y = _rope(s * r * g, c, sa, sb, ROPE_A // 4)
            if scale != 1.0:
                y = y * scale
            o_ref[:, h * LANES:(h + 1) * LANES] = y.astype(o_ref.dtype)

    if has_q:
        ql = qlat_ref[...].astype(F32)
        qn = ql * lax.rsqrt(jnp.mean(ql * ql, axis=-1, keepdims=True) + EPS) * gq_ref[...]
        q = jnp.dot(qn.astype(BF16), wuq_ref[...], preferred_element_type=F32)
        heads(q, gqn_ref[...], q_o, MLA_SCALE)

    kv = kv_ref[...].astype(F32)
    if norm_ckv:
        ckv = kv * lax.rsqrt(jnp.mean(kv * kv, axis=-1, keepdims=True) + EPS) * gkv_ref[...]
    else:
        ckv = kv
    sm = sm_ref[...]
    ckv_b = ckv.astype(BF16)
    kin = jnp.concatenate([ckv_b, sm.astype(BF16)], axis=1)
    k = jnp.dot(kin, wkc_ref[...], preferred_element_type=F32)
    heads(k, gkn_ref[...], k_o, 1.0)
    v_o[...] = jnp.dot(ckv_b, wv_ref[...], preferred_element_type=F32).astype(v_o.dtype)
    if has_q:
        ckv_o[...] = ckv
        kr_o[...] = sm.astype(F32)


def pack_mla_weights(lp):
    wuq = lp['w_mla_uq'].reshape(Q_RANK, H_A, QK_A)
    wuq_p = jnp.pad(wuq, ((0, 0), (0, 0), (0, LANES - QK_A))).reshape(Q_RANK, H_A * LANES).astype(BF16)
    wukv = lp['w_mla_ukv'].reshape(KV_RANK, H_A, NOPE_A + V_A)
    wk = jnp.pad(wukv[:, :, :NOPE_A], ((0, 0), (0, 0), (0, LANES - NOPE_A))).reshape(KV_RANK, H_A * LANES)
    place = np.zeros((LANES, H_A, LANES), np.float32)
    for h in range(H_A):
        place[np.arange(ROPE_A), h, NOPE_A + np.arange(ROPE_A)] = 1.0
    wkc = jnp.concatenate([wk, jnp.asarray(place.reshape(LANES, H_A * LANES))], axis=0).astype(BF16)
    wv = wukv[:, :, NOPE_A:]
    wv_even = jnp.pad(wv, ((0, 0), (0, 0), (0, LANES - V_A)))
    wv_odd = jnp.pad(wv, ((0, 0), (0, 0), (LANES - V_A, 0)))
    odd = (np.arange(H_A) % 2 == 1)[None, :, None]
    wv_p = jnp.where(odd, wv_odd, wv_even).reshape(KV_RANK, H_A * LANES).astype(BF16)
    pad_g = lambda g: jnp.pad(g, (0, LANES - QK_A)).reshape(1, LANES)
    return dict(wuq=wuq_p, wkc=wkc, wv=wv_p, gqn=pad_g(lp['g_mla_qn']), gkn=pad_g(lp['g_mla_kn']),
                gq=lp['g_mla_qlat'].reshape(1, Q_RANK), gkv=lp['g_mla_kvlat'].reshape(1, KV_RANK))


def mla_prep(p, mw, tabs):
    tm = MLA_TM
    n = R_ALL // tm
    full = lambda shape: pl.BlockSpec(shape, lambda i: (0, 0))
    tab = pl.BlockSpec((tm, LANES), lambda i: (_pos_block_of_tile(i, tm), 0))
    wide = H_A * LANES
    out_shape = (jax.ShapeDtypeStruct((R_ALL, wide), BF16), jax.ShapeDtypeStruct((R_ALL, wide), BF16),
                 jax.ShapeDtypeStruct((R_ALL, wide), BF16), jax.ShapeDtypeStruct((R_ALL, KV_RANK), F32),
                 jax.ShapeDtypeStruct((R_ALL, LANES), F32))
    row = lambda w: pl.BlockSpec((tm, w), lambda i: (i, 0))
    return pl.pallas_call(
        functools.partial(_mla_prep_kernel, has_q=True, norm_ckv=True),
        grid=(n,),
        in_specs=[pl.BlockSpec((tm, Q_RANK), lambda i: (i, P_QLAT // Q_RANK)), full((1, Q_RANK)),
                  full((Q_RANK, wide)), full((1, LANES)),
                  pl.BlockSpec((tm, KV_RANK), lambda i: (i, P_KVLAT // KV_RANK)),
                  pl.BlockSpec((tm, LANES), lambda i: (i, P_SMALL // LANES)), full((1, KV_RANK)),
                  full((2 * LANES, wide)), full((KV_RANK, wide)), full((1, LANES)), tab, tab, tab],
        out_specs=(row(wide), row(wide), row(wide), row(KV_RANK), row(LANES)),
        out_shape=out_shape,
        compiler_params=_cparams(("arbitrary",)),
        name="mla_prep",
    )(p, mw['gq'], mw['wuq'], mw['gqn'], p, p, mw['gkv'], mw['wkc'], mw['wv'], mw['gkn'], *tabs)


def mla_prep_cache(ckv_c, kr_c, mw, tabs):
    tm = MLA_TM
    r = ckv_c.shape[0]
    full = lambda shape: pl.BlockSpec(shape, lambda i: (0, 0))
    ident = pl.BlockSpec((tm, LANES), lambda i: (T_LAT // tm, 0))
    wide = H_A * LANES
    row = lambda w: pl.BlockSpec((tm, w), lambda i: (i, 0))
    return pl.pallas_call(
        functools.partial(_mla_prep_kernel, has_q=False, norm_ckv=False),
        grid=(r // tm,),
        in_specs=[row(KV_RANK), row(LANES), full((1, KV_RANK)), full((2 * LANES, wide)), full((KV_RANK, wide)),
                  full((1, LANES)), ident, ident, ident],
        out_specs=(row(wide), row(wide)),
        out_shape=(jax.ShapeDtypeStruct((r, wide), BF16), jax.ShapeDtypeStruct((r, wide), BF16)),
        compiler_params=_cparams(("arbitrary",)),
        name="mla_prep_cache",
    )(ckv_c, kr_c, mw['gkv'], mw['wkc'], mw['wv'], mw['gkn'], *tabs)


_NT = (((1,), (1,)), ((), ()))


def _mla_attn_kernel(*refs, has_cache):
    if has_cache:
        q_ref, ko_ref, vo_ref, kc_ref, vc_ref, o_ref = refs
    else:
        q_ref, ko_ref, vo_ref, o_ref = refs
    acc = jnp.zeros(o_ref.shape, F32)
    for hh in range(2):
        sl = slice(hh * LANES, (hh + 1) * LANES)
        q = q_ref[:, sl]
        s_o = lax.dot_general(q, ko_ref[:, sl], _NT, preferred_element_type=F32)
        m = jnp.max(s_o, axis=-1, keepdims=True)
        if has_cache:
            s_c = lax.dot_general(q, kc_ref[:, sl], _NT, preferred_element_type=F32)
            m = jnp.maximum(m, jnp.max(s_c, axis=-1, keepdims=True))
        e_o = jnp.exp(s_o - m)
        l = jnp.sum(e_o, axis=-1, keepdims=True)
        pv = jnp.dot(e_o.astype(BF16), vo_ref[:, sl], preferred_element_type=F32)
        if has_cache:
            e_c = jnp.exp(s_c - m)
            l = l + jnp.sum(e_c, axis=-1, keepdims=True)
            pv = pv + jnp.dot(e_c.astype(BF16), vc_ref[:, sl], preferred_element_type=F32)
        acc = acc + pv * (1.0 / l)
    o_ref[...] = acc.astype(o_ref.dtype)


def mla_attention(q, k, v, kc, vc, *, row0, n_seq, t_seq, tq):
    has_cache = kc is not None
    n_pair = H_A // 2
    nq = t_seq // tq
    qb0, kb0 = row0 // tq, row0 // t_seq
    in_specs = [pl.BlockSpec((tq, 2 * LANES), lambda s, p, i: (qb0 + s * nq + i, p)),
                pl.BlockSpec((t_seq, 2 * LANES), lambda s, p, i: (kb0 + s, p)),
                pl.BlockSpec((t_seq, 2 * LANES), lambda s, p, i: (kb0 + s, p))]
    args = [q, k, v]
    if has_cache:
        in_specs += [pl.BlockSpec((PAST_LEN, 2 * LANES), lambda s, p, i: (s, p)),
                     pl.BlockSpec((PAST_LEN, 2 * LANES), lambda s, p, i: (s, p))]
        args += [kc, vc]
    return pl.pallas_call(
        functools.partial(_mla_attn_kernel, has_cache=has_cache),
        grid=(n_seq, n_pair, nq),
        in_specs=in_specs,
        out_specs=pl.BlockSpec((tq, LANES), lambda s, p, i: (s * nq + i, p)),
        out_shape=jax.ShapeDtypeStruct((n_seq * t_seq, W_BRANCH), BF16),
        compiler_params=_cparams(("arbitrary", "arbitrary", "arbitrary")),
        name="mla_attention",
    )(*args)


DIFF_TM = 512


def _diff_prep_kernel(dq_ref, dk_ref, gq_ref, gk_ref, c_ref, sa_ref, sb_ref, q_o, ko_o, kp_o):
    c, sa, sb = c_ref[...], sa_ref[...], sb_ref[...]
    lane = lax.broadcasted_iota(jnp.int32, (1, LANES), 1)
    lo = lane < DH_D

    def normed(x, g):
        x2 = x * x
        s_lo = jnp.sum(jnp.where(lo, x2, 0.0), axis=-1, keepdims=True)
        s_hi = jnp.sum(jnp.where(lo, 0.0, x2), axis=-1, keepdims=True)
        ms = jnp.where(lo, s_lo, s_hi) * (1.0 / DH_D)
        return x * lax.rsqrt(ms + EPS) * g

    for j in range(W_BRANCH // LANES):
        sl = slice(j * LANES, (j + 1) * LANES)
        qn = normed(dq_ref[:, sl].astype(F32), gq_ref[...])
        q_o[:, sl] = (_rope(qn, c, sa, sb, DH_D // 4) * DIFF_SCALE).astype(q_o.dtype)
        kn = normed(dk_ref[:, sl].astype(F32), gk_ref[...])
        kp_o[:, sl] = kn
        ko_o[:, sl] = _rope(kn, c, sa, sb, DH_D // 4).astype(ko_o.dtype)


def diff_prep(p, lp, tabs):
    tm = DIFF_TM
    tile2 = lambda g: jnp.concatenate([g, g]).reshape(1, LANES)
    full = pl.BlockSpec((1, LANES), lambda i: (0, 0))
    tab = pl.BlockSpec((tm, LANES), lambda i: (_pos_block_of_tile(i, tm), 0))
    row = pl.BlockSpec((tm, W_BRANCH), lambda i: (i, 0))
    dq_b, dk_b = _seg_block('dq'), _seg_block('dk')
    return pl.pallas_call(
        _diff_prep_kernel,
        grid=(R_ALL // tm,),
        in_specs=[pl.BlockSpec((tm, W_BRANCH), lambda i: (i, dq_b)), pl.BlockSpec((tm, W_BRANCH), lambda i: (i, dk_b)),
                  full, full, tab, tab, tab],
        out_specs=(row, row, row),
        out_shape=(jax.ShapeDtypeStruct((R_ALL, W_BRANCH), BF16), jax.ShapeDtypeStruct((R_ALL, W_BRANCH), BF16),
                   jax.ShapeDtypeStruct((R_ALL, W_BRANCH), F32)),
        compiler_params=_cparams(("arbitrary",)),
        name="diff_prep",
    )(p, p, tile2(lp['g_diff_qn']), tile2(lp['g_diff_kn']), *tabs)


def _diff_attn_kernel(*refs, has_cache, lambda_init):
    if has_cache:
        dl_ref, q1_ref, q2_ref, k1_ref, k2_ref, v_ref, k1c_ref, k2c_ref, vc_ref, g_ref, o_ref = refs
    else:
        dl_ref, q1_ref, q2_ref, k1_ref, k2_ref, v_ref, g_ref, o_ref = refs
    half = pl.program_id(1) % 2
    lane = lax.broadcasted_iota(jnp.int32, (1, LANES), 1)
    mine = (lane // DH_D) == half
    dl = dl_ref[...]
    lam = (jnp.exp(jnp.sum(dl[0:1] * dl[1:2], axis=-1, keepdims=True))
           - jnp.exp(jnp.sum(dl[2:3] * dl[3:4], axis=-1, keepdims=True)) + lambda_init)

    def attend(q_ref, k_ref, kc_ref):
        q = jnp.where(mine, q_ref[...], jnp.zeros((), q_ref.dtype))
        s_o = lax.dot_general(q, k_ref[...].astype(BF16), _NT, preferred_element_type=F32)
        m = jnp.max(s_o, axis=-1, keepdims=True)
        if has_cache:
            s_c = lax.dot_general(q, kc_ref[...].astype(BF16), _NT, preferred_element_type=F32)
            m = jnp.maximum(m, jnp.max(s_c, axis=-1, keepdims=True))
        e_o = jnp.exp(s_o - m)
        l = jnp.sum(e_o, axis=-1, keepdims=True)
        pv = jnp.dot(e_o.astype(BF16), v_ref[...].astype(BF16), preferred_element_type=F32)
        if has_cache:
            e_c = jnp.exp(s_c - m)
            l = l + jnp.sum(e_c, axis=-1, keepdims=True)
            pv = pv + jnp.dot(e_c.astype(BF16), vc_ref[...].astype(BF16), preferred_element_type=F32)
        return pv * (1.0 / l)

    y = attend(q1_ref, k1_ref, k1c_ref if has_cache else None) - lam * attend(q2_ref, k2_ref, k2c_ref if has_cache else None)
    r = lax.rsqrt(jnp.mean(y * y, axis=-1, keepdims=True) + EPS)
    o_ref[...] = ((y * r * g_ref[...]) * (1.0 - lambda_init)).astype(o_ref.dtype)


def diff_attention(dl, qd, kd, p, kc, vc, g_sub, *, row0, n_seq, t_seq, tq, lambda_init):
    has_cache = kc is not None
    nq = t_seq // tq
    qb0, kb0 = row0 // tq, row0 // t_seq
    dv_b = _seg_block('dv') * (W_BRANCH // LANES)
    in_specs = [pl.BlockSpec((4, DH_D), lambda s, h, i: (0, 0)),
                pl.BlockSpec((tq, LANES), lambda s, h, i: (qb0 + s * nq + i, h // 2)),
                pl.BlockSpec((tq, LANES), lambda s, h, i: (qb0 + s * nq + i, 2 + h // 2)),
                pl.BlockSpec((t_seq, LANES), lambda s, h, i: (kb0 + s, h // 2)),
                pl.BlockSpec((t_seq, LANES), lambda s, h, i: (kb0 + s, 2 + h // 2)),
                pl.BlockSpec((t_seq, LANES), lambda s, h, i: (kb0 + s, dv_b + h))]
    args = [dl, qd, qd, kd, kd, p]
    if has_cache:
        in_specs += [pl.BlockSpec((PAST_LEN, LANES), lambda s, h, i: (s, h // 2)),
                     pl.BlockSpec((PAST_LEN, LANES), lambda s, h, i: (s, 2 + h // 2)),
                     pl.BlockSpec((PAST_LEN, LANES), lambda s, h, i: (s, h))]
        args += [kc, kc, vc]
    in_specs.append(pl.BlockSpec((1, LANES), lambda s, h, i: (0, 0)))
    args.append(g_sub.reshape(1, LANES))
    return pl.pallas_call(
        functools.partial(_diff_attn_kernel, has_cache=has_cache, lambda_init=lambda_init),
        grid=(n_seq, H_D, nq),
        in_specs=in_specs,
        out_specs=pl.BlockSpec((tq, LANES), lambda s, h, i: (s * nq + i, h)),
        out_shape=jax.ShapeDtypeStruct((n_seq * t_seq, W_BRANCH), BF16),
        compiler_params=_cparams(("arbitrary", "arbitrary", "arbitrary")),
        name="diff_attention",
    )(*args)


def _neg_expm1(z):
    series = -z * (1.0 + z * (0.5 + z * (1.0 / 6.0 + z * (1.0 / 24.0 + z * (1.0 / 120.0 + z * (1.0 / 720.0))))))
    return jnp.where(z > -0.25, series, 1.0 - jnp.exp(z))


def _softplus(z):
    return jnp.maximum(z, 0.0) + jnp.log(1.0 + jnp.exp(-jnp.abs(z)))


def _gelu_tanh(x):
    return 0.5 * x * (1.0 + jnp.tanh(math.sqrt(2.0 / math.pi) * (x + 0.044715 * (x * x * x))))


def _rglru_kernel(x_ref, gate_ref, wc_ref, bc_ref, wg_ref, bg_ref, lam_ref, h0_ref, y_ref, st_ref,
                  a_s, u_s, h_s, *, t_seq):
    t = t_seq
    x = x_ref[...].astype(F32)
    row = lax.broadcasted_iota(jnp.int32, (t, W_B), 0)
    wc = wc_ref[...]
    xc = (wc[0:1] * jnp.where(row >= 2, pltpu.roll(x, 2, 0), 0.0)
          + wc[1:2] * jnp.where(row >= 1, pltpu.roll(x, 1, 0), 0.0)
          + wc[2:3] * x
          + wc[3:4] * jnp.where(row < t - 1, pltpu.roll(x, t - 1, 0), 0.0)
          + bc_ref[...])
    gates = jnp.dot(xc.astype(BF16), wg_ref[...], preferred_element_type=F32) + bg_ref[...]
    r8 = row % SUBLANES
    nblk = t // SUBLANES
    hsum = None
    for d in range(2):
        rg = jax.nn.sigmoid(gates[:, (2 * d) * W_B:(2 * d + 1) * W_B])
        ig = jax.nn.sigmoid(gates[:, (2 * d + 1) * W_B:(2 * d + 2) * W_B])
        log_a = -RG_C * rg * _softplus(-lam_ref[d:d + 1, :])
        a = jnp.exp(log_a)
        u = jnp.sqrt(_neg_expm1(2.0 * log_a)) * (ig * xc)
        for step in (1, 2, 4):
            if d == 0:
                valid = r8 >= step
                a_sh, u_sh = pltpu.roll(a, step, 0), pltpu.roll(u, step, 0)
            else:
                valid = r8 + step < SUBLANES
                a_sh, u_sh = pltpu.roll(a, t - step, 0), pltpu.roll(u, t - step, 0)
            u = jnp.where(valid, a * u_sh + u, u)
            a = jnp.where(valid, a * a_sh, a)
        a_s[...] = a
        u_s[...] = u
        h0 = h0_ref[0, d:d + 1, :]

        def body(k, carry, d=d):
            blk = k if d == 0 else nblk - 1 - k
            rows = pl.ds(pl.multiple_of(blk * SUBLANES, SUBLANES), SUBLANES)
            h = a_s[rows, :] * carry + u_s[rows, :]
            h_s[rows, :] = h
            return h[SUBLANES - 1:SUBLANES, :] if d == 0 else h[0:1, :]

        last = lax.fori_loop(0, nblk, body, h0, unroll=8)
        st_ref[0, d:d + 1, :] = last
        hsum = h_s[...] if d == 0 else hsum + h_s[...]
    y_ref[...] = (_gelu_tanh(gate_ref[...].astype(F32)) * hsum).astype(y_ref.dtype)


def pack_rglru_weights(lp):
    def blockdiag(w):
        eye = jnp.eye(NB_B, dtype=w.dtype)
        return jnp.einsum('ncd,nm->ncmd', w, eye).reshape(W_B, W_B)
    wg = jnp.concatenate([blockdiag(lp['w_rg_a'][0]), blockdiag(lp['w_rg_x'][0]),
                          blockdiag(lp['w_rg_a'][1]), blockdiag(lp['w_rg_x'][1])], axis=1).astype(BF16)
    bg = jnp.concatenate([lp['b_rg_a'][0], lp['b_rg_x'][0], lp['b_rg_a'][1], lp['b_rg_x'][1]]).reshape(1, 4 * W_B)
    return dict(wg=wg, bg=bg, wc=lp['w_conv_rg'], bc=lp['b_conv_rg'].reshape(1, W_B), lam=lp['rg_lambda'])


def rglru(p, rw, h0, *, row0, n_seq, t_seq):
    rb0 = row0 // t_seq
    xb, gb = _seg_block('rg_x'), _seg_block('rg_gate')
    full = lambda shape: pl.BlockSpec(shape, lambda s: tuple(0 for _ in shape))
    return pl.pallas_call(
        functools.partial(_rglru_kernel, t_seq=t_seq),
        grid=(n_seq,),
        in_specs=[pl.BlockSpec((t_seq, W_B), lambda s: (rb0 + s, xb)),
                  pl.BlockSpec((t_seq, W_B), lambda s: (rb0 + s, gb)),
                  full((CONV_W, W_B)), full((1, W_B)), full((W_B, 4 * W_B)), full((1, 4 * W_B)), full((2, W_B)),
                  pl.BlockSpec((1, 2, W_B), lambda s: (s, 0, 0))],
        out_specs=(pl.BlockSpec((t_seq, W_B), lambda s: (s, 0)), pl.BlockSpec((1, 2, W_B), lambda s: (s, 0, 0))),
        out_shape=(jax.ShapeDtypeStruct((n_seq * t_seq, W_B), BF16), jax.ShapeDtypeStruct((n_seq, 2, W_B), F32)),
        scratch_shapes=[pltpu.VMEM((t_seq, W_B), F32)] * 3,
        compiler_params=_cparams(("arbitrary",)),
        name="rglru",
    )(p, p, rw['wc'], rw['bc'], rw['wg'], rw['bg'], rw['lam'], h0)


def _dot_split(a, b_bf16):
    hi = a.astype(BF16)
    lo = (a - hi.astype(F32)).astype(BF16)
    return (jnp.dot(hi, b_bf16, preferred_element_type=F32) + jnp.dot(lo, b_bf16, preferred_element_type=F32))


def _log_sigmoid(z):
    return jnp.minimum(z, 0.0) - jnp.log(1.0 + jnp.exp(-jnp.abs(z)))


_TN = (((0,), (0,)), ((), ()))


def _mlstm_kernel(q_ref, k_ref, v_ref, o_ref, sm_ref, bias_ref, g_ref, c0_ref, m0_ref,
                  y_ref, c_out, m_out, hm_s, c_s, *, t_seq):
    L = ML_CHUNK
    nchunk = t_seq // L
    scale = DH_C ** -0.5
    ri = lax.broadcasted_iota(jnp.int32, (L, L), 0)
    ci = lax.broadcasted_iota(jnp.int32, (L, L), 1)
    lane1 = lax.broadcasted_iota(jnp.int32, (L, LANES), 1)
    ones_col = jnp.where(lane1 == 0, 1.0, 0.0).astype(BF16)
    bias = bias_ref[...]

    for d in range(2):
        causal = (ci <= ri) if d == 0 else (ci >= ri)
        tri = jnp.where(causal, 1.0, 0.0).astype(BF16)
        tri_t = jnp.where((ri <= ci) if d == 0 else (ri >= ci), 1.0, 0.0).astype(BF16)
        c_s[...] = c0_ref[0, d]
        m_init = tuple(m0_ref[0, :, d * H_C + h:d * H_C + h + 1] for h in range(H_C))

        def chunk(kk, ms, d=d, causal=causal, tri=tri, tri_t=tri_t):
            cidx = kk if d == 0 else nchunk - 1 - kk
            rows = pl.ds(pl.multiple_of(cidx * L, L), L)
            gsm = sm_ref[rows, :].astype(F32) + bias
            lf_all = _log_sigmoid(gsm)
            cum_cols = _dot_split_left(tri, lf_all)
            g_t = gsm.T
            cum_rows = _dot_split(lf_all.T, tri_t)
            new_ms = []
            for h in range(H_C):
                jl, jf = SM_MI + d * H_C + h, SM_MF + d * H_C + h
                cum_c = cum_cols[:, jf:jf + 1]
                cum_r = cum_rows[jf:jf + 1, :]
                li_c = gsm[:, jl:jl + 1]
                li_r = g_t[jl:jl + 1, :]
                m_mem = ms[h]
                log_d = jnp.where(causal, cum_c - cum_r + li_r, -jnp.inf)
                inter = cum_c + m_mem
                m_row = jnp.maximum(inter, jnp.max(log_d, axis=-1, keepdims=True))
                dmat = jnp.exp(log_d - m_row)
                w_inter = jnp.exp(inter - m_row)
                sl = slice(h * DH_C, (h + 1) * DH_C)
                qh, kh, vh = q_ref[rows, sl], k_ref[rows, sl], v_ref[rows, sl]
                s = lax.dot_general(qh, kh, _NT, preferred_element_type=F32) * (scale * dmat)
                sv = jnp.dot(s.astype(BF16), vh, preferred_element_type=F32)
                qc = jnp.dot(qh, c_s[h].astype(BF16), preferred_element_type=F32) * scale
                num = sv + qc[:, :DH_C] * w_inter
                den = jnp.sum(s, axis=-1, keepdims=True) + w_inter * qc[:, DH_C:DH_C + 1]
                den = jnp.maximum(jnp.abs(den), jnp.exp(-m_row))
                h_out = num * (1.0 / den)
                if d == 0:
                    hm_s[rows, sl] = h_out
                else:
                    hm_s[rows, sl] = hm_s[rows, sl] + h_out
                last = cum_c[L - 1:L, :] if d == 0 else cum_c[0:1, :]
                w_s = last - cum_c + li_c
                m_new = jnp.maximum(last + m_mem, jnp.max(w_s, axis=0, keepdims=True))
                decay = jnp.exp(last + m_mem - m_new)
                ws = jnp.exp(w_s - m_new)
                kw_t = (kh.astype(F32) * ws).T.astype(BF16)
                v_aug = jnp.concatenate([vh, ones_col], axis=1)
                c_s[h] = decay * c_s[h] + jnp.dot(kw_t, v_aug, preferred_element_type=F32)
                new_ms.append(m_new)
            return tuple(new_ms)

        m_fin = lax.fori_loop(0, nchunk, chunk, m_init)
        c_out[0, d] = c_s[...]
        for h in range(H_C):
            m_out[0, :, d * H_C + h:d * H_C + h + 1] = m_fin[h]

    for h in range(H_C):
        sl = slice(h * DH_C, (h + 1) * DH_C)
        hm = hm_s[:, sl]
        r = lax.rsqrt(jnp.mean(hm * hm, axis=-1, keepdims=True) + EPS)
        y_ref[:, sl] = (jax.nn.sigmoid(o_ref[:, sl].astype(F32)) * (hm * r * g_ref[...])).astype(y_ref.dtype)


def _dot_split_left(a_bf16, b):
    hi = b.astype(BF16)
    lo = (b - hi.astype(F32)).astype(BF16)
    return (jnp.dot(a_bf16, hi, preferred_element_type=F32) + jnp.dot(a_bf16, lo, preferred_element_type=F32))


def mlstm(p, bias_sm, g_out, c0_aug, m0, *, row0, n_seq, t_seq):
    rb0 = row0 // t_seq
    seg = lambda nm: pl.BlockSpec((t_seq, W_BRANCH), lambda s, b=_seg_block(nm): (rb0 + s, b))
    full = lambda shape: pl.BlockSpec(shape, lambda s: tuple(0 for _ in shape))
    st_spec = pl.BlockSpec((1, 2, H_C, DH_C, 2 * DH_C), lambda s: (s, 0, 0, 0, 0))
    m_spec = pl.BlockSpec((1, 1, 2 * H_C), lambda s: (s, 0, 0))
    return pl.pallas_call(
        functools.partial(_mlstm_kernel, t_seq=t_seq),
        grid=(n_seq,),
        in_specs=[seg('mq'), seg('mk'), seg('mv'), seg('mo'),
                  pl.BlockSpec((t_seq, LANES), lambda s: (rb0 + s, P_SMALL // LANES)),
                  full((1, LANES)), full((1, DH_C)), st_spec, m_spec],
        out_specs=(pl.BlockSpec((t_seq, W_BRANCH), lambda s: (s, 0)), st_spec, m_spec),
        out_shape=(jax.ShapeDtypeStruct((n_seq * t_seq, W_BRANCH), BF16),
                   jax.ShapeDtypeStruct((n_seq, 2, H_C, DH_C, 2 * DH_C), F32),
                   jax.ShapeDtypeStruct((n_seq, 1, 2 * H_C), F32)),
        scratch_shapes=[pltpu.VMEM((t_seq, W_BRANCH), F32), pltpu.VMEM((H_C, DH_C, 2 * DH_C), F32)],
        compiler_params=_cparams(("arbitrary",)),
        name="mlstm",
    )(p, p, p, p, p, bias_sm, g_out.reshape(1, DH_C), c0_aug, m0)


MERGE_TM = 512


def _merge_kernel(ya_ref, yb_ref, yc_ref, yd_ref, g0_ref, g1_ref, g2_ref, g3_ref, x_ref, mod_ref,
                  wbr_ref, wout_ref, gn_ref, xo_ref, xn_ref):
    merged = None
    for g, (y_ref, gate_ref) in enumerate(((ya_ref, g0_ref), (yb_ref, g1_ref), (yc_ref, g2_ref), (yd_ref, g3_ref))):
        pg = jnp.dot(y_ref[...], wbr_ref[g], preferred_element_type=F32)
        term = jax.nn.sigmoid(gate_ref[...].astype(F32)) * pg
        merged = term if merged is None else merged + term
    y = jnp.dot(merged.astype(BF16), wout_ref[...], preferred_element_type=F32)
    x = x_ref[...] + mod_ref[0, 2:3, :] * y
    xo_ref[...] = x
    r = lax.rsqrt(jnp.mean(x * x, axis=-1, keepdims=True) + EPS)
    xn_ref[...] = ((x * r * gn_ref[...]) * (1.0 + mod_ref[0, 4:5, :]) + mod_ref[0, 3:4, :]).astype(xn_ref.dtype)


def merge(ys, p, x, mod, wbr, wout, g_ffn):
    tm = MERGE_TM
    br = pl.BlockSpec((tm, W_BRANCH), lambda i: (i, 0))
    gate = lambda g: pl.BlockSpec((tm, D_MODEL), lambda i, g=g: (i, g))
    row = pl.BlockSpec((tm, D_MODEL), lambda i: (i, 0))
    return pl.pallas_call(
        _merge_kernel,
        grid=(R_ALL // tm,),
        in_specs=[br, br, br, br, gate(0), gate(1), gate(2), gate(3), row,
                  pl.BlockSpec((1, 8, D_MODEL), lambda i: (_mod_row_of_tile(i, tm), 0, 0)),
                  pl.BlockSpec((N_BRANCH, W_BRANCH, D_MODEL), lambda i: (0, 0, 0)),
                  pl.BlockSpec((D_MODEL, D_MODEL), lambda i: (0, 0)),
                  pl.BlockSpec((1, D_MODEL), lambda i: (0, 0))],
        out_specs=(row, row),
        out_shape=(jax.ShapeDtypeStruct((R_ALL, D_MODEL), F32), jax.ShapeDtypeStruct((R_ALL, D_MODEL), BF16)),
        compiler_params=_cparams(("arbitrary",)),
        name="merge",
    )(*ys, p, p, p, p, x, mod, wbr, wout, g_ffn.reshape(1, D_MODEL))


def _ffn_up_kernel(te_ref, nt_ref, x_ref, wg_ref, wu_ref, h_ref):
    @pl.when(pl.program_id(1) < nt_ref[0])
    def _():
        x = x_ref[...]
        g = jnp.dot(x, wg_ref[0].astype(BF16), preferred_element_type=F32)
        u = jnp.dot(x, wu_ref[0].astype(BF16), preferred_element_type=F32)
        h_ref[...] = (g * jax.nn.sigmoid(g) * u).astype(h_ref.dtype)


def ffn_up(tile_expert, n_tiles, xs, wg, wu, tm, tf):
    r, d = xs.shape
    f = wg.shape[2]
    return pl.pallas_call(
        _ffn_up_kernel,
        grid_spec=pltpu.PrefetchScalarGridSpec(
            num_scalar_prefetch=2,
            grid=(f // tf, r // tm),
            in_specs=[pl.BlockSpec((tm, d), lambda j, i, te, nt: (i, 0)),
                      pl.BlockSpec((1, d, tf), lambda j, i, te, nt: (te[i], 0, j)),
                      pl.BlockSpec((1, d, tf), lambda j, i, te, nt: (te[i], 0, j))],
            out_specs=pl.BlockSpec((tm, tf), lambda j, i, te, nt: (i, j))),
        out_shape=jax.ShapeDtypeStruct((r, f), BF16),
        compiler_params=_cparams(("arbitrary", "arbitrary")),
        name="ffn_up",
    )(tile_expert, n_tiles, xs, wg, wu)


def _ffn_down_kernel(te_ref, nt_ref, h_ref, wd_ref, s_ref, y_ref):
    @pl.when(pl.program_id(1) < nt_ref[0])
    def _():
        y = jnp.dot(h_ref[...], wd_ref[0].astype(BF16), preferred_element_type=F32)
        y_ref[...] = (y * s_ref[...]).astype(y_ref.dtype)


def ffn_down(tile_expert, n_tiles, h, wd, row_scale, tm, tn):
    r, f = h.shape
    d = wd.shape[2]
    return pl.pallas_call(
        _ffn_down_kernel,
        grid_spec=pltpu.PrefetchScalarGridSpec(
            num_scalar_prefetch=2,
            grid=(d // tn, r // tm),
            in_specs=[pl.BlockSpec((tm, f), lambda j, i, te, nt: (i, 0)),
                      pl.BlockSpec((1, f, tn), lambda j, i, te, nt: (te[i], 0, j)),
                      pl.BlockSpec((tm, 1), lambda j, i, te, nt: (i, 0))],
            out_specs=pl.BlockSpec((tm, tn), lambda j, i, te, nt: (i, j))),
        out_shape=jax.ShapeDtypeStruct((r, d), F32),
        compiler_params=_cparams(("arbitrary", "arbitrary")),
        name="ffn_down",
    )(tile_expert, n_tiles, h, wd, row_scale)


def _ffn_down_res_kernel(h_ref, wd_ref, x_ref, mod_ref, y_ref):
    y = jnp.dot(h_ref[...], wd_ref[...].astype(BF16), preferred_element_type=F32)
    y_ref[...] = x_ref[...] + mod_ref[0, 5:6, :] * y


def ffn_down_residual(h, wd, x, mod):
    tm, tn = 1024, 512
    r, f = h.shape
    d = wd.shape[1]
    return pl.pallas_call(
        _ffn_down_res_kernel,
        grid=(d // tn, r // tm),
        in_specs=[pl.BlockSpec((tm, f), lambda j, i: (i, 0)),
                  pl.BlockSpec((f, tn), lambda j, i: (0, j)),
                  pl.BlockSpec((tm, tn), lambda j, i: (i, j)),
                  pl.BlockSpec((1, 8, tn), lambda j, i: (_mod_row_of_tile(i, tm), 0, j))],
        out_specs=pl.BlockSpec((tm, tn), lambda j, i: (i, j)),
        out_shape=jax.ShapeDtypeStruct((r, d), F32),
        compiler_params=_cparams(("arbitrary", "arbitrary")),
        name="ffn_down_residual",
    )(h, wd, x, mod)


def dense_swiglu_residual(xn, x, mod, wg, wu, wd):
    t = xn.shape[0]
    tm = 1024
    n_tiles = t // tm
    te = jnp.zeros((n_tiles,), jnp.int32)
    nt = jnp.full((1,), n_tiles, jnp.int32)
    h = ffn_up(te, nt, xn, wg[None], wu[None], tm=tm, tf=256)
    return ffn_down_residual(h, wd, x, mod)


MOE_TM = 512


def moe_swiglu(xn, w_router, wg, wu, wd):
    t, d = xn.shape
    tm = MOE_TM
    logits = jnp.dot(xn.astype(F32), w_router, precision=lax.Precision.HIGHEST)
    top_v, top_i = lax.top_k(logits, TOP_K)
    gate = jax.nn.softmax(top_v, axis=-1)

    flat_e = top_i.reshape(-1).astype(jnp.int32)
    flat_g = gate.reshape(-1)
    order = jnp.argsort(flat_e, stable=True)
    sorted_e = flat_e[order]
    counts = jnp.sum(jax.nn.one_hot(flat_e, N_EXP, dtype=jnp.int32), axis=0)
    padded = (counts + tm - 1) // tm * tm
    grp_start = jnp.cumsum(padded) - padded
    raw_start = jnp.cumsum(counts) - counts
    rank = jnp.arange(t * TOP_K, dtype=jnp.int32) - raw_start[sorted_e]
    dest = grp_start[sorted_e] + rank
    r_max = t * TOP_K + N_EXP * tm
    n_tiles_max = r_max // tm
    tok_of_slot = jnp.zeros((r_max,), jnp.int32).at[dest].set((order // TOP_K).astype(jnp.int32))
    gate_of_slot = jnp.zeros((r_max,), F32).at[dest].set(flat_g[order])
    slot_of_assign = jnp.zeros((t * TOP_K,), jnp.int32).at[order].set(dest)

    tile_start = jnp.arange(n_tiles_max, dtype=jnp.int32) * tm
    grp_end = grp_start + padded
    tile_expert = jnp.sum((tile_start[:, None] >= grp_end[None, :]).astype(jnp.int32), axis=1)
    tile_expert = jnp.minimum(tile_expert, N_EXP - 1).astype(jnp.int32)
    n_tiles = (jnp.sum(padded) // tm).astype(jnp.int32).reshape(1)

    xs = jnp.take(xn, tok_of_slot, axis=0)
    h = ffn_up(tile_expert, n_tiles, xs, wg, wu, tm=tm, tf=512)
    ys = ffn_down(tile_expert, n_tiles, h, wd, gate_of_slot[:, None], tm=tm, tn=512)
    pair = jnp.take(ys, slot_of_assign, axis=0).reshape(t, TOP_K, d)
    return pair[:, 0] + pair[:, 1]


def _layer(x, cond, lp, l, ctx, tabs_a, tabs_d):
    mod = modulation(cond, lp['w_mod'], lp['b_mod']).reshape(cond.shape[0], 6, D_MODEL)
    mod = jnp.pad(mod, ((0, 0), (0, 2), (0, 0)))
    p = in_proj(x, mod, lp['g_norm_mix'], pack_w_in(lp['w_in']))

    mw = pack_mla_weights(lp)
    q_a, k_a, v_a, ckv, kr = mla_prep(p, mw, tabs_a)
    kr_c = jnp.pad(ctx['mla_krope'].reshape(-1, ROPE_A), ((0, 0), (0, LANES - ROPE_A)))
    kc_a, vc_a = mla_prep_cache(ctx['mla_ckv'].reshape(-1, KV_RANK), kr_c, mw, tabs_a)
    ya = jnp.concatenate([
        mla_attention(q_a, k_a, v_a, None, None, row0=0, n_seq=N_CTX_SEQ, t_seq=T_CTX, tq=T_CTX),
        mla_attention(q_a, k_a, v_a, kc_a, vc_a, row0=R_CTX, n_seq=N_LAT_SEQ, t_seq=T_LAT, tq=256)], axis=0)

    rw = pack_rglru_weights(lp)
    yb_c, st_rg = rglru(p, rw, jnp.zeros((N_CTX_SEQ, 2, W_B), F32), row0=0, n_seq=N_CTX_SEQ, t_seq=T_CTX)
    yb_l, _ = rglru(p, rw, ctx['rglru'], row0=R_CTX, n_seq=N_LAT_SEQ, t_seq=T_LAT)
    yb = jnp.concatenate([yb_c, yb_l], axis=0)

    bias_sm = jnp.zeros((LANES,), F32).at[SM_MI:SM_MI + 2 * H_C].set(lp['b_ml_i'].reshape(-1))
    bias_sm = bias_sm.at[SM_MF:SM_MF + 2 * H_C].set(lp['b_ml_f'].reshape(-1)).reshape(1, LANES)
    c0_ctx = jnp.zeros((N_CTX_SEQ, 2, H_C, DH_C, 2 * DH_C), F32)
    m0_ctx = jnp.zeros((N_CTX_SEQ, 1, 2 * H_C), F32)
    c0_lat = jnp.concatenate([ctx['mlstm_C'], ctx['mlstm_n'][..., None],
                              jnp.zeros(ctx['mlstm_n'].shape + (DH_C - 1,), F32)], axis=-1)
    m0_lat = ctx['mlstm_m'].reshape(N_LAT_SEQ, 1, 2 * H_C)
    yc_c, c_fin, m_fin = mlstm(p, bias_sm, lp['g_ml_out'], c0_ctx, m0_ctx, row0=0, n_seq=N_CTX_SEQ, t_seq=T_CTX)
    yc_l, _, _ = mlstm(p, bias_sm, lp['g_ml_out'], c0_lat, m0_lat, row0=R_CTX, n_seq=N_LAT_SEQ, t_seq=T_LAT)
    yc = jnp.concatenate([yc_c, yc_l], axis=0)

    lambda_init = 0.8 - 0.6 * math.exp(-0.3 * l)
    qd, kd_own, kd_plain = diff_prep(p, lp, tabs_d)
    dkc = ctx['diff_k'].reshape(-1, W_BRANCH)
    dvc = ctx['diff_v'].reshape(-1, W_BRANCH)
    yd = jnp.concatenate([
        diff_attention(lp['diff_lambda'], qd, kd_own, p, None, None, lp['g_diff_sub'],
                       row0=0, n_seq=N_CTX_SEQ, t_seq=T_CTX, tq=T_CTX, lambda_init=lambda_init),
        diff_attention(lp['diff_lambda'], qd, kd_own, p, dkc, dvc, lp['g_diff_sub'],
                       row0=R_CTX, n_seq=N_LAT_SEQ, t_seq=T_LAT, tq=256, lambda_init=lambda_init)], axis=0)

    x, xn = merge((ya, yb, yc, yd), p, x, mod, lp['w_br'].astype(BF16), lp['w_out'].astype(BF16), lp['g_norm_ffn'])

    if 'ffn' in lp:
        x = dense_swiglu_residual(xn, x, mod, *lp['ffn'])
    else:
        f = moe_swiglu(xn, *lp['moe'])
        g2 = jnp.concatenate([jnp.broadcast_to(mod[0:1, 5], (R_CTX, D_MODEL)),
                              jnp.repeat(mod[1:1 + N_LAT_SEQ, 5], T_LAT, axis=0)], axis=0)
        x = x + g2 * f

    dv0 = _seg_block('dv') * W_BRANCH
    ctx_out = (ckv[:R_CTX].reshape(N_CTX_SEQ, T_CTX, KV_RANK),
               kr[:R_CTX, :ROPE_A].reshape(N_CTX_SEQ, T_CTX, ROPE_A),
               kd_plain[:R_CTX].reshape(N_CTX_SEQ, T_CTX, 2, H_D, DH_D),
               p[:R_CTX, dv0:dv0 + W_BRANCH].astype(F32).reshape(N_CTX_SEQ, T_CTX, H_D, 2 * DH_D),
               st_rg,
               c_fin[..., :DH_C],
               c_fin[..., DH_C],
               m_fin.reshape(N_CTX_SEQ, 2, H_C))
    return x, ctx_out


def kernel(x_prompt, x_sample, cache_mla_ckv, cache_mla_krope, cache_diff_k, cache_diff_v,
           state_rglru, state_mlstm_C, state_mlstm_n, state_mlstm_m, c, c_ctx,
           w_mod, b_mod, g_norm_mix, g_norm_ffn, w_in, g_mla_qlat, w_mla_uq, g_mla_kvlat, w_mla_ukv,
           g_mla_qn, g_mla_kn, w_conv_rg, b_conv_rg, w_rg_a, b_rg_a, w_rg_x, b_rg_x, rg_lambda,
           b_ml_i, b_ml_f, g_ml_out, g_diff_qn, g_diff_kn, diff_lambda, g_diff_sub, w_br, w_out,
           w_ffn_gate, w_ffn_up, w_ffn_down, w_router, w_moe_gate, w_moe_up, w_moe_down):
    assert x_prompt.shape == (N_CTX_SEQ, T_CTX, D_MODEL) and x_sample.shape == (N_LAT_SEQ, T_LAT, D_MODEL)
    tabs_a = _rope_tables(ROPE_A, (NOPE_A,), MLA_TM)
    tabs_d = _rope_tables(DH_D, (0, DH_D), DIFF_TM)
    cond = jnp.concatenate([c_ctx.reshape(1, D_MODEL), c, jnp.zeros((16 - 1 - N_LAT_SEQ, D_MODEL), F32)], axis=0)
    x = jnp.concatenate([x_prompt.reshape(R_CTX, D_MODEL), x_sample.reshape(R_LAT, D_MODEL)], axis=0)
    new = []
    for l in range(DEPTH):
        lp = dict(w_mod=w_mod[l], b_mod=b_mod[l], g_norm_mix=g_norm_mix[l], g_norm_ffn=g_norm_ffn[l], w_in=w_in[l],
                  g_mla_qlat=g_mla_qlat[l], w_mla_uq=w_mla_uq[l], g_mla_kvlat=g_mla_kvlat[l], w_mla_ukv=w_mla_ukv[l],
                  g_mla_qn=g_mla_qn[l], g_mla_kn=g_mla_kn[l], w_conv_rg=w_conv_rg[l], b_conv_rg=b_conv_rg[l],
                  w_rg_a=w_rg_a[l], b_rg_a=b_rg_a[l], w_rg_x=w_rg_x[l], b_rg_x=b_rg_x[l], rg_lambda=rg_lambda[l],
                  b_ml_i=b_ml_i[l], b_ml_f=b_ml_f[l], g_ml_out=g_ml_out[l], g_diff_qn=g_diff_qn[l],
                  g_diff_kn=g_diff_kn[l], diff_lambda=diff_lambda[l], g_diff_sub=g_diff_sub[l],
                  w_br=w_br[l], w_out=w_out[l])
        if l % 2 == 0:
            lp['ffn'] = (w_ffn_gate[l // 2], w_ffn_up[l // 2], w_ffn_down[l // 2])
        else:
            lp['moe'] = (w_router[l // 2], w_moe_gate[l // 2], w_moe_up[l // 2], w_moe_down[l // 2])
        ctx_l = dict(mla_ckv=cache_mla_ckv[:, l], mla_krope=cache_mla_krope[:, l], diff_k=cache_diff_k[:, l],
                     diff_v=cache_diff_v[:, l], rglru=state_rglru[:, l], mlstm_C=state_mlstm_C[:, l],
                     mlstm_n=state_mlstm_n[:, l], mlstm_m=state_mlstm_m[:, l])
        x, st = _layer(x, cond, lp, l, ctx_l, tabs_a, tabs_d)
        new.append(st)
    outs = tuple(jnp.stack([s[i] for s in new], axis=1) for i in range(8))
    return (x[:R_CTX].reshape(N_CTX_SEQ, T_CTX, D_MODEL), x[R_CTX:].reshape(N_LAT_SEQ, T_LAT, D_MODEL)) + outs
```

```python
import functools
import math

import jax
import jax.numpy as jnp
import numpy as np
from jax import lax
from jax.experimental import pallas as pl
from jax.experimental.pallas import tpu as pltpu

D_MODEL = 1024
DEPTH = 2
GRID_W = 64
ROPE_BASE = 10000.0
EPS = 1e-6
N_BRANCH = 4
W_BRANCH = D_MODEL // 2

H_A = 8
NOPE_A = 64
ROPE_A = 32
V_A = W_BRANCH // H_A
Q_RANK = D_MODEL // 4
KV_RANK = D_MODEL // 8
MLA_SCALE = (NOPE_A + ROPE_A) ** -0.5

W_B = W_BRANCH
NB_B = 8
BW_B = W_B // NB_B
CONV_W = 4
RG_C = 8.0

H_C = 4
DH_C = W_BRANCH // H_C
ML_CHUNK = 128

H_D = 4
DH_D = W_BRANCH // (2 * H_D)
DIFF_SCALE = DH_D ** -0.5

N_EXP = 8
TOP_K = 2

V7X_VMEM_LIMIT_BYTES = 56 * 1024 * 1024
LANES = 128
SUBLANES = 8

BF16 = jnp.bfloat16
F32 = jnp.float32

P_GATE = 0
P_SEG = N_BRANCH * D_MODEL
SEG_NAMES = ('rg_x', 'rg_gate', 'mq', 'mk', 'mv', 'mo', 'dq', 'dk', 'dv')
P_QLAT = P_SEG + 9 * W_BRANCH
P_KVLAT = P_QLAT + Q_RANK
P_SMALL = P_KVLAT + KV_RANK
P_WIDTH = P_SMALL + LANES
SM_MI = ROPE_A
SM_MF = ROPE_A + 2 * H_C


def _seg_block(name):
    return (P_SEG + SEG_NAMES.index(name) * W_BRANCH) // W_BRANCH


N_CTX_SEQ, T_CTX = 16, 256
N_LAT_SEQ, T_LAT = 8, 1024
PAST_LEN = 512
R_CTX = N_CTX_SEQ * T_CTX
R_LAT = N_LAT_SEQ * T_LAT
R_ALL = R_CTX + R_LAT


def _cparams(sem):
    return pltpu.CompilerParams(dimension_semantics=sem, vmem_limit_bytes=V7X_VMEM_LIMIT_BYTES)


def _mod_row_of_tile(i, tm):
    n_ctx_tiles = R_CTX // tm
    per_seq = T_LAT // tm
    return jnp.where(i < n_ctx_tiles, 0, 1 + (i - n_ctx_tiles) // per_seq)


def _pos_block_of_tile(i, tm):
    n_ctx_tiles = R_CTX // tm
    per_seq = T_LAT // tm
    return jnp.where(i < n_ctx_tiles, per_seq, (i - n_ctx_tiles) % per_seq)


def _mod_kernel(c_ref, w_ref, b_ref, o_ref):
    c = c_ref[...]
    s = (c * jax.nn.sigmoid(c)).astype(BF16)
    o_ref[...] = jnp.dot(s, w_ref[...].astype(BF16), preferred_element_type=F32) + b_ref[...]


def modulation(cond, w_mod, b_mod):
    m, d = cond.shape
    n = w_mod.shape[1]
    tn = 1536
    return pl.pallas_call(
        _mod_kernel,
        grid=(n // tn,),
        in_specs=[pl.BlockSpec((m, d), lambda j: (0, 0)),
                  pl.BlockSpec((d, tn), lambda j: (0, j)),
                  pl.BlockSpec((1, tn), lambda j: (0, j))],
        out_specs=pl.BlockSpec((m, tn), lambda j: (0, j)),
        out_shape=jax.ShapeDtypeStruct((m, n), F32),
        compiler_params=_cparams(("arbitrary",)),
        name="modulation",
    )(cond, w_mod, b_mod.reshape(1, n))


def _in_proj_kernel(x_ref, mod_ref, g_ref, w_ref, o_ref, xn_ref):
    @pl.when(pl.program_id(1) == 0)
    def _():
        x = x_ref[...]
        r = lax.rsqrt(jnp.mean(x * x, axis=-1, keepdims=True) + EPS)
        sh = mod_ref[0, 0:1, :]
        sc = mod_ref[0, 1:2, :]
        xn_ref[...] = ((x * r * g_ref[...]) * (1.0 + sc) + sh).astype(BF16)

    o_ref[...] = jnp.dot(xn_ref[...], w_ref[...], preferred_element_type=F32).astype(o_ref.dtype)


def in_proj(x, mod, g, w_p):
    tm, tn = 1024, 1024
    m, d = x.shape
    n = w_p.shape[1]
    return pl.pallas_call(
        _in_proj_kernel,
        grid=(m // tm, n // tn),
        in_specs=[pl.BlockSpec((tm, d), lambda i, j: (i, 0)),
                  pl.BlockSpec((1, 8, d), lambda i, j: (_mod_row_of_tile(i, tm), 0, 0)),
                  pl.BlockSpec((1, d), lambda i, j: (0, 0)),
                  pl.BlockSpec((d, tn), lambda i, j: (0, j))],
        out_specs=pl.BlockSpec((tm, tn), lambda i, j: (i, j)),
        out_shape=jax.ShapeDtypeStruct((m, n), BF16),
        scratch_shapes=[pltpu.VMEM((tm, d), BF16)],
        compiler_params=_cparams(("arbitrary", "arbitrary")),
        name="in_proj",
    )(x, mod, g.reshape(1, d), w_p)


def pack_w_in(w_in):
    o_rg = Q_RANK + KV_RANK + ROPE_A
    o_mi = o_rg + 6 * W_BRANCH
    o_dq = o_mi + 4 * H_C
    o_gate = o_dq + 3 * W_BRANCH
    pad = jnp.zeros((w_in.shape[0], LANES - ROPE_A - 4 * H_C), w_in.dtype)
    parts = [w_in[:, o_gate:], w_in[:, o_rg:o_mi], w_in[:, o_dq:o_gate], w_in[:, :Q_RANK + KV_RANK],
             w_in[:, Q_RANK + KV_RANK:o_rg], w_in[:, o_mi:o_dq], pad]
    return jnp.concatenate(parts, axis=1).astype(BF16)


def _rope_tables(rot_dim, lane_starts, tm):
    rows = T_LAT // GRID_W
    r, c = np.meshgrid(np.arange(rows, dtype=np.float32), np.arange(GRID_W, dtype=np.float32), indexing='ij')
    nf = rot_dim // 4
    inv = (np.float32(ROPE_BASE) ** (-np.arange(nf, dtype=np.float32) / np.float32(nf))).astype(np.float32)
    ang = np.stack([r.reshape(-1)[:, None] * inv, c.reshape(-1)[:, None] * inv], axis=1).astype(np.float32)
    cos, sin = np.cos(ang).astype(np.float32), np.sin(ang).astype(np.float32)
    tc = np.ones((T_LAT + tm, LANES), np.float32)
    ta = np.zeros((T_LAT + tm, LANES), np.float32)
    tb = np.zeros((T_LAT + tm, LANES), np.float32)
    for s0 in lane_starts:
        for a in range(2):
            lo = s0 + a * 2 * nf
            tc[:T_LAT, lo:lo + nf] = cos[:, a]
            tc[:T_LAT, lo + nf:lo + 2 * nf] = cos[:, a]
            ta[:T_LAT, lo:lo + nf] = -sin[:, a]
            tb[:T_LAT, lo + nf:lo + 2 * nf] = sin[:, a]
    return jnp.asarray(tc), jnp.asarray(ta), jnp.asarray(tb)


def _rope(x, c, sa, sb, half):
    return x * c + pltpu.roll(x, LANES - half, 1) * sa + pltpu.roll(x, half, 1) * sb


MLA_TM = 512
ATTN_TQ = 512
QK_A = NOPE_A + ROPE_A


def _mla_prep_kernel(*refs, has_q, norm_ckv):
    if has_q:
        (qlat_ref, gq_ref, wuq_ref, gqn_ref, kv_ref, sm_ref, gkv_ref, wkc_ref, wv_ref, gkn_ref,
         c_ref, sa_ref, sb_ref, q_o, k_o, v_o, ckv_o, kr_o) = refs
    else:
        (kv_ref, sm_ref, gkv_ref, wkc_ref, wv_ref, gkn_ref, c_ref, sa_ref, sb_ref, k_o, v_o) = refs
    c, sa, sb = c_ref[...], sa_ref[...], sb_ref[...]

    def heads(z, g, o_ref, scale):
        for h in range(H_A):
            s = z[:, h * LANES:(h + 1) * LANES]
            r = lax.rsqrt(jnp.sum(s * s, axis=-1, keepdims=True) * (1.0 / QK_A) + EPS)
            y = _rope(s * r * g, c, sa, sb, ROPE_A // 4)
            if scale != 1.0:
                y = y * scale
            o_ref[:, h * LANES:(h + 1) * LANES] = y.astype(o_ref.dtype)

    if has_q:
        ql = qlat_ref[...].astype(F32)
        qn = ql * lax.rsqrt(jnp.mean(ql * ql, axis=-1, keepdims=True) + EPS) * gq_ref[...]
        q = jnp.dot(qn.astype(BF16), wuq_ref[...], preferred_element_type=F32)
        heads(q, gqn_ref[...], q_o, MLA_SCALE)

    kv = kv_ref[...].astype(F32)
    if norm_ckv:
        ckv = kv * lax.rsqrt(jnp.mean(kv * kv, axis=-1, keepdims=True) + EPS) * gkv_ref[...]
    else:
        ckv = kv
    sm = sm_ref[...]
    ckv_b = ckv.astype(BF16)
    kin = jnp.concatenate([ckv_b, sm.astype(BF16)], axis=1)
    k = jnp.dot(kin, wkc_ref[...], preferred_element_type=F32)
    heads(k, gkn_ref[...], k_o, 1.0)
    v_o[...] = jnp.dot(ckv_b, wv_ref[...], preferred_element_type=F32).astype(v_o.dtype)
    if has_q:
        ckv_o[...] = ckv
        kr_o[...] = sm.astype(F32)


def pack_mla_weights(lp):
    wuq = lp['w_mla_uq'].reshape(Q_RANK, H_A, QK_A)
    wuq_p = jnp.pad(wuq, ((0, 0), (0, 0), (0, LANES - QK_A))).reshape(Q_RANK, H_A * LANES).astype(BF16)
    wukv = lp['w_mla_ukv'].reshape(KV_RANK, H_A, NOPE_A + V_A)
    wk = jnp.pad(wukv[:, :, :NOPE_A], ((0, 0), (0, 0), (0, LANES - NOPE_A))).reshape(KV_RANK, H_A * LANES)
    place = np.zeros((LANES, H_A, LANES), np.float32)
    for h in range(H_A):
        place[np.arange(ROPE_A), h, NOPE_A + np.arange(ROPE_A)] = 1.0
    wkc = jnp.concatenate([wk, jnp.asarray(place.reshape(LANES, H_A * LANES))], axis=0).astype(BF16)
    wv = wukv[:, :, NOPE_A:]
    wv_even = jnp.pad(wv, ((0, 0), (0, 0), (0, LANES - V_A)))
    wv_odd = jnp.pad(wv, ((0, 0), (0, 0), (LANES - V_A, 0)))
    odd = (np.arange(H_A) % 2 == 1)[None, :, None]
    wv_p = jnp.where(odd, wv_odd, wv_even).reshape(KV_RANK, H_A * LANES).astype(BF16)
    pad_g = lambda g: jnp.pad(g, (0, LANES - QK_A)).reshape(1, LANES)
    return dict(wuq=wuq_p, wkc=wkc, wv=wv_p, gqn=pad_g(lp['g_mla_qn']), gkn=pad_g(lp['g_mla_kn']),
                gq=lp['g_mla_qlat'].reshape(1, Q_RANK), gkv=lp['g_mla_kvlat'].reshape(1, KV_RANK))


def mla_prep(p, mw, tabs):
    tm = MLA_TM
    n = R_ALL // tm
    full = lambda shape: pl.BlockSpec(shape, lambda i: (0, 0))
    tab = pl.BlockSpec((tm, LANES), lambda i: (_pos_block_of_tile(i, tm), 0))
    wide = H_A * LANES
    out_shape = (jax.ShapeDtypeStruct((R_ALL, wide), BF16), jax.ShapeDtypeStruct((R_ALL, wide), BF16),
                 jax.ShapeDtypeStruct((R_ALL, wide), BF16), jax.ShapeDtypeStruct((R_ALL, KV_RANK), F32),
                 jax.ShapeDtypeStruct((R_ALL, LANES), F32))
    row = lambda w: pl.BlockSpec((tm, w), lambda i: (i, 0))
    return pl.pallas_call(
        functools.partial(_mla_prep_kernel, has_q=True, norm_ckv=True),
        grid=(n,),
        in_specs=[pl.BlockSpec((tm, Q_RANK), lambda i: (i, P_QLAT // Q_RANK)), full((1, Q_RANK)),
                  full((Q_RANK, wide)), full((1, LANES)),
                  pl.BlockSpec((tm, KV_RANK), lambda i: (i, P_KVLAT // KV_RANK)),
                  pl.BlockSpec((tm, LANES), lambda i: (i, P_SMALL // LANES)), full((1, KV_RANK)),
                  full((2 * LANES, wide)), full((KV_RANK, wide)), full((1, LANES)), tab, tab, tab],
        out_specs=(row(wide), row(wide), row(wide), row(KV_RANK), row(LANES)),
        out_shape=out_shape,
        compiler_params=_cparams(("arbitrary",)),
        name="mla_prep",
    )(p, mw['gq'], mw['wuq'], mw['gqn'], p, p, mw['gkv'], mw['wkc'], mw['wv'], mw['gkn'], *tabs)


def mla_prep_cache(ckv_c, kr_c, mw, tabs):
    tm = MLA_TM
    r = ckv_c.shape[0]
    full = lambda shape: pl.BlockSpec(shape, lambda i: (0, 0))
    ident = pl.BlockSpec((tm, LANES), lambda i: (T_LAT // tm, 0))
    wide = H_A * LANES
    row = lambda w: pl.BlockSpec((tm, w), lambda i: (i, 0))
    return pl.pallas_call(
        functools.partial(_mla_prep_kernel, has_q=False, norm_ckv=False),
        grid=(r // tm,),
        in_specs=[row(KV_RANK), row(LANES), full((1, KV_RANK)), full((2 * LANES, wide)), full((KV_RANK, wide)),
                  full((1, LANES)), ident, ident, ident],
        out_specs=(row(wide), row(wide)),
        out_shape=(jax.ShapeDtypeStruct((r, wide), BF16), jax.ShapeDtypeStruct((r, wide), BF16)),
        compiler_params=_cparams(("arbitrary",)),
        name="mla_prep_cache",
    )(ckv_c, kr_c, mw['gkv'], mw['wkc'], mw['wv'], mw['gkn'], *tabs)


_NT = (((1,), (1,)), ((), ()))


def _mla_attn_kernel(*refs, has_cache):
    if has_cache:
        q_ref, ko_ref, vo_ref, kc_ref, vc_ref, o_ref = refs
    else:
        q_ref, ko_ref, vo_ref, o_ref = refs
    acc = jnp.zeros(o_ref.shape, F32)
    for hh in range(2):
        sl = slice(hh * LANES, (hh + 1) * LANES)
        q = q_ref[:, sl]
        s_o = lax.dot_general(q, ko_ref[:, sl], _NT, preferred_element_type=F32)
        m = jnp.max(s_o, axis=-1, keepdims=True)
        if has_cache:
            s_c = lax.dot_general(q, kc_ref[:, sl], _NT, preferred_element_type=F32)
            m = jnp.maximum(m, jnp.max(s_c, axis=-1, keepdims=True))
        e_o = jnp.exp(s_o - m)
        l = jnp.sum(e_o, axis=-1, keepdims=True)
        pv = jnp.dot(e_o.astype(BF16), vo_ref[:, sl], preferred_element_type=F32)
        if has_cache:
            e_c = jnp.exp(s_c - m)
            l = l + jnp.sum(e_c, axis=-1, keepdims=True)
            pv = pv + jnp.dot(e_c.astype(BF16), vc_ref[:, sl], preferred_element_type=F32)
        acc = acc + pv * (1.0 / l)
    o_ref[...] = acc.astype(o_ref.dtype)


def _continue_output(kernel, in_specs, args, prev):
    if prev is None:
        return kernel, {}
    idx = len(args)
    in_specs.append(pl.BlockSpec(memory_space=pl.ANY))
    args.append(prev)

    def skipping(*refs):
        return kernel(*refs[:idx], *refs[idx + 1:])

    return skipping, {idx: 0}


def mla_attention(q, k, v, kc, vc, prev, *, row0, n_seq, t_seq, tq):
    has_cache = kc is not None
    n_pair = H_A // 2
    nq = t_seq // tq
    qb0, kb0 = row0 // tq, row0 // t_seq
    in_specs = [pl.BlockSpec((tq, 2 * LANES), lambda s, p, i: (qb0 + s * nq + i, p)),
                pl.BlockSpec((t_seq, 2 * LANES), lambda s, p, i: (kb0 + s, p)),
                pl.BlockSpec((t_seq, 2 * LANES), lambda s, p, i: (kb0 + s, p))]
    args = [q, k, v]
    if has_cache:
        in_specs += [pl.BlockSpec((PAST_LEN, 2 * LANES), lambda s, p, i: (s, p)),
                     pl.BlockSpec((PAST_LEN, 2 * LANES), lambda s, p, i: (s, p))]
        args += [kc, vc]
    body, aliases = _continue_output(functools.partial(_mla_attn_kernel, has_cache=has_cache), in_specs, args, prev)
    return pl.pallas_call(
        body,
        grid=(n_seq, n_pair, nq),
        in_specs=in_specs,
        out_specs=pl.BlockSpec((tq, LANES), lambda s, p, i: (qb0 + s * nq + i, p)),
        out_shape=jax.ShapeDtypeStruct((R_ALL, W_BRANCH), BF16),
        input_output_aliases=aliases,
        compiler_params=_cparams(("arbitrary", "arbitrary", "arbitrary")),
        name="mla_attention",
    )(*args)


DIFF_TM = 512


def _diff_prep_kernel(dq_ref, dk_ref, gq_ref, gk_ref, c_ref, sa_ref, sb_ref, q_o, ko_o, kp_o):
    c, sa, sb = c_ref[...], sa_ref[...], sb_ref[...]
    lane = lax.broadcasted_iota(jnp.int32, (1, LANES), 1)
    lo = lane < DH_D

    def normed(x, g):
        x2 = x * x
        s_lo = jnp.sum(jnp.where(lo, x2, 0.0), axis=-1, keepdims=True)
        s_hi = jnp.sum(jnp.where(lo, 0.0, x2), axis=-1, keepdims=True)
        ms = jnp.where(lo, s_lo, s_hi) * (1.0 / DH_D)
        return x * lax.rsqrt(ms + EPS) * g

    for j in range(W_BRANCH // LANES):
        sl = slice(j * LANES, (j + 1) * LANES)
        qn = normed(dq_ref[:, sl].astype(F32), gq_ref[...])
        q_o[:, sl] = (_rope(qn, c, sa, sb, DH_D // 4) * DIFF_SCALE).astype(q_o.dtype)
        kn = normed(dk_ref[:, sl].astype(F32), gk_ref[...])
        kp_o[:, sl] = kn
        ko_o[:, sl] = _rope(kn, c, sa, sb, DH_D // 4).astype(ko_o.dtype)


def diff_prep(p, lp, tabs):
    tm = DIFF_TM
    tile2 = lambda g: jnp.concatenate([g, g]).reshape(1, LANES)
    full = pl.BlockSpec((1, LANES), lambda i: (0, 0))
    tab = pl.BlockSpec((tm, LANES), lambda i: (_pos_block_of_tile(i, tm), 0))
    row = pl.BlockSpec((tm, W_BRANCH), lambda i: (i, 0))
    dq_b, dk_b = _seg_block('dq'), _seg_block('dk')
    return pl.pallas_call(
        _diff_prep_kernel,
        grid=(R_ALL // tm,),
        in_specs=[pl.BlockSpec((tm, W_BRANCH), lambda i: (i, dq_b)), pl.BlockSpec((tm, W_BRANCH), lambda i: (i, dk_b)),
                  full, full, tab, tab, tab],
        out_specs=(row, row, row),
        out_shape=(jax.ShapeDtypeStruct((R_ALL, W_BRANCH), BF16), jax.ShapeDtypeStruct((R_ALL, W_BRANCH), BF16),
                   jax.ShapeDtypeStruct((R_ALL, W_BRANCH), F32)),
        compiler_params=_cparams(("arbitrary",)),
        name="diff_prep",
    )(p, p, tile2(lp['g_diff_qn']), tile2(lp['g_diff_kn']), *tabs)


def _diff_attn_kernel(*refs, has_cache, lambda_init):
    if has_cache:
        dl_ref, q1_ref, q2_ref, k1_ref, k2_ref, v_ref, k1c_ref, k2c_ref, vc_ref, g_ref, o_ref = refs
    else:
        dl_ref, q1_ref, q2_ref, k1_ref, k2_ref, v_ref, g_ref, o_ref = refs
    half = pl.program_id(1) % 2
    lane = lax.broadcasted_iota(jnp.int32, (1, LANES), 1)
    mine = (lane // DH_D) == half
    dl = dl_ref[...]
    lam = (jnp.exp(jnp.sum(dl[0:1] * dl[1:2], axis=-1, keepdims=True))
           - jnp.exp(jnp.sum(dl[2:3] * dl[3:4], axis=-1, keepdims=True)) + lambda_init)

    def attend(q_ref, k_ref, kc_ref):
        q = jnp.where(mine, q_ref[...], jnp.zeros((), q_ref.dtype))
        s_o = lax.dot_general(q, k_ref[...].astype(BF16), _NT, preferred_element_type=F32)
        m = jnp.max(s_o, axis=-1, keepdims=True)
        if has_cache:
            s_c = lax.dot_general(q, kc_ref[...].astype(BF16), _NT, preferred_element_type=F32)
            m = jnp.maximum(m, jnp.max(s_c, axis=-1, keepdims=True))
        e_o = jnp.exp(s_o - m)
        l = jnp.sum(e_o, axis=-1, keepdims=True)
        pv = jnp.dot(e_o.astype(BF16), v_ref[...].astype(BF16), preferred_element_type=F32)
        if has_cache:
            e_c = jnp.exp(s_c - m)
            l = l + jnp.sum(e_c, axis=-1, keepdims=True)
            pv = pv + jnp.dot(e_c.astype(BF16), vc_ref[...].astype(BF16), preferred_element_type=F32)
        return pv * (1.0 / l)

    y = attend(q1_ref, k1_ref, k1c_ref if has_cache else None) - lam * attend(q2_ref, k2_ref, k2c_ref if has_cache else None)
    r = lax.rsqrt(jnp.mean(y * y, axis=-1, keepdims=True) + EPS)
    o_ref[...] = ((y * r * g_ref[...]) * (1.0 - lambda_init)).astype(o_ref.dtype)


def diff_attention(dl, qd, kd, p, kc, vc, g_sub, prev, *, row0, n_seq, t_seq, tq, lambda_init):
    has_cache = kc is not None
    nq = t_seq // tq
    qb0, kb0 = row0 // tq, row0 // t_seq
    dv_b = _seg_block('dv') * (W_BRANCH // LANES)
    in_specs = [pl.BlockSpec((4, DH_D), lambda s, h, i: (0, 0)),
                pl.BlockSpec((tq, LANES), lambda s, h, i: (qb0 + s * nq + i, h // 2)),
                pl.BlockSpec((tq, LANES), lambda s, h, i: (qb0 + s * nq + i, 2 + h // 2)),
                pl.BlockSpec((t_seq, LANES), lambda s, h, i: (kb0 + s, h // 2)),
                pl.BlockSpec((t_seq, LANES), lambda s, h, i: (kb0 + s, 2 + h // 2)),
                pl.BlockSpec((t_seq, LANES), lambda s, h, i: (kb0 + s, dv_b + h))]
    args = [dl, qd, qd, kd, kd, p]
    if has_cache:
        in_specs += [pl.BlockSpec((PAST_LEN, LANES), lambda s, h, i: (s, h // 2)),
                     pl.BlockSpec((PAST_LEN, LANES), lambda s, h, i: (s, 2 + h // 2)),
                     pl.BlockSpec((PAST_LEN, LANES), lambda s, h, i: (s, h))]
        args += [kc, kc, vc]
    in_specs.append(pl.BlockSpec((1, LANES), lambda s, h, i: (0, 0)))
    args.append(g_sub.reshape(1, LANES))
    body, aliases = _continue_output(
        functools.partial(_diff_attn_kernel, has_cache=has_cache, lambda_init=lambda_init), in_specs, args, prev)
    return pl.pallas_call(
        body,
        grid=(n_seq, H_D, nq),
        in_specs=in_specs,
        out_specs=pl.BlockSpec((tq, LANES), lambda s, h, i: (qb0 + s * nq + i, h)),
        out_shape=jax.ShapeDtypeStruct((R_ALL, W_BRANCH), BF16),
        input_output_aliases=aliases,
        compiler_params=_cparams(("arbitrary", "arbitrary", "arbitrary")),
        name="diff_attention",
    )(*args)


def _neg_expm1(z):
    series = -z * (1.0 + z * (0.5 + z * (1.0 / 6.0 + z * (1.0 / 24.0 + z * (1.0 / 120.0 + z * (1.0 / 720.0))))))
    return jnp.where(z > -0.25, series, 1.0 - jnp.exp(z))


def _softplus(z):
    return jnp.maximum(z, 0.0) + jnp.log(1.0 + jnp.exp(-jnp.abs(z)))


def _gelu_tanh(x):
    return 0.5 * x * (1.0 + jnp.tanh(math.sqrt(2.0 / math.pi) * (x + 0.044715 * (x * x * x))))


def _rglru_kernel(x_ref, gate_ref, wc_ref, bc_ref, wg_ref, bg_ref, lam_ref, h0_ref, y_ref, st_ref,
                  a_s, u_s, h_s, *, t_seq):
    t = t_seq
    x = x_ref[...].astype(F32)
    row = lax.broadcasted_iota(jnp.int32, (t, W_B), 0)
    wc = wc_ref[...]
    xc = (wc[0:1] * jnp.where(row >= 2, pltpu.roll(x, 2, 0), 0.0)
          + wc[1:2] * jnp.where(row >= 1, pltpu.roll(x, 1, 0), 0.0)
          + wc[2:3] * x
          + wc[3:4] * jnp.where(row < t - 1, pltpu.roll(x, t - 1, 0), 0.0)
          + bc_ref[...])
    gates = jnp.dot(xc.astype(BF16), wg_ref[...], preferred_element_type=F32) + bg_ref[...]
    r8 = row % SUBLANES
    nblk = t // SUBLANES
    hsum = None
    for d in range(2):
        rg = jax.nn.sigmoid(gates[:, (2 * d) * W_B:(2 * d + 1) * W_B])
        ig = jax.nn.sigmoid(gates[:, (2 * d + 1) * W_B:(2 * d + 2) * W_B])
        log_a = -RG_C * rg * _softplus(-lam_ref[d:d + 1, :])
        a = jnp.exp(log_a)
        u = jnp.sqrt(_neg_expm1(2.0 * log_a)) * (ig * xc)
        for step in (1, 2, 4):
            if d == 0:
                valid = r8 >= step
                a_sh, u_sh = pltpu.roll(a, step, 0), pltpu.roll(u, step, 0)
            else:
                valid = r8 + step < SUBLANES
                a_sh, u_sh = pltpu.roll(a, t - step, 0), pltpu.roll(u, t - step, 0)
            u = jnp.where(valid, a * u_sh + u, u)
            a = jnp.where(valid, a * a_sh, a)
        a_s[...] = a
        u_s[...] = u
        h0 = h0_ref[0, d:d + 1, :]

        def body(k, carry, d=d):
            blk = k if d == 0 else nblk - 1 - k
            rows = pl.ds(pl.multiple_of(blk * SUBLANES, SUBLANES), SUBLANES)
            h = a_s[rows, :] * carry + u_s[rows, :]
            h_s[rows, :] = h
            return h[SUBLANES - 1:SUBLANES, :] if d == 0 else h[0:1, :]

        last = lax.fori_loop(0, nblk, body, h0, unroll=8)
        st_ref[0, d:d + 1, :] = last
        hsum = h_s[...] if d == 0 else hsum + h_s[...]
    y_ref[...] = (_gelu_tanh(gate_ref[...].astype(F32)) * hsum).astype(y_ref.dtype)


def pack_rglru_weights(lp):
    def blockdiag(w):
        eye = jnp.eye(NB_B, dtype=w.dtype)
        return jnp.einsum('ncd,nm->ncmd', w, eye).reshape(W_B, W_B)
    wg = jnp.concatenate([blockdiag(lp['w_rg_a'][0]), blockdiag(lp['w_rg_x'][0]),
                          blockdiag(lp['w_rg_a'][1]), blockdiag(lp['w_rg_x'][1])], axis=1).astype(BF16)
    bg = jnp.concatenate([lp['b_rg_a'][0], lp['b_rg_x'][0], lp['b_rg_a'][1], lp['b_rg_x'][1]]).reshape(1, 4 * W_B)
    return dict(wg=wg, bg=bg, wc=lp['w_conv_rg'], bc=lp['b_conv_rg'].reshape(1, W_B), lam=lp['rg_lambda'])


def rglru(p, rw, h0, prev, *, row0, n_seq, t_seq):
    rb0 = row0 // t_seq
    xb, gb = _seg_block('rg_x'), _seg_block('rg_gate')
    full = lambda shape: pl.BlockSpec(shape, lambda s: tuple(0 for _ in shape))
    in_specs = [pl.BlockSpec((t_seq, W_B), lambda s: (rb0 + s, xb)),
                pl.BlockSpec((t_seq, W_B), lambda s: (rb0 + s, gb)),
                full((CONV_W, W_B)), full((1, W_B)), full((W_B, 4 * W_B)), full((1, 4 * W_B)), full((2, W_B)),
                pl.BlockSpec((1, 2, W_B), lambda s: (s, 0, 0))]
    args = [p, p, rw['wc'], rw['bc'], rw['wg'], rw['bg'], rw['lam'], h0]
    body, aliases = _continue_output(functools.partial(_rglru_kernel, t_seq=t_seq), in_specs, args, prev)
    return pl.pallas_call(
        body,
        grid=(n_seq,),
        in_specs=in_specs,
        out_specs=(pl.BlockSpec((t_seq, W_B), lambda s: (rb0 + s, 0)), pl.BlockSpec((1, 2, W_B), lambda s: (s, 0, 0))),
        out_shape=(jax.ShapeDtypeStruct((R_ALL, W_B), BF16), jax.ShapeDtypeStruct((n_seq, 2, W_B), F32)),
        scratch_shapes=[pltpu.VMEM((t_seq, W_B), F32)] * 3,
        input_output_aliases=aliases,
        compiler_params=_cparams(("arbitrary",)),
        name="rglru",
    )(*args)


def _dot_split(a, b_bf16):
    hi = a.astype(BF16)
    lo = (a - hi.astype(F32)).astype(BF16)
    return (jnp.dot(hi, b_bf16, preferred_element_type=F32) + jnp.dot(lo, b_bf16, preferred_element_type=F32))


def _log_sigmoid(z):
    return jnp.minimum(z, 0.0) - jnp.log(1.0 + jnp.exp(-jnp.abs(z)))


_TN = (((0,), (0,)), ((), ()))


def _mlstm_kernel(q_ref, k_ref, v_ref, o_ref, sm_ref, bias_ref, g_ref, c0_ref, m0_ref,
                  y_ref, c_out, m_out, hm_s, c_s, *, t_seq):
    L = ML_CHUNK
    nchunk = t_seq // L
    scale = DH_C ** -0.5
    ri = lax.broadcasted_iota(jnp.int32, (L, L), 0)
    ci = lax.broadcasted_iota(jnp.int32, (L, L), 1)
    lane1 = lax.broadcasted_iota(jnp.int32, (L, LANES), 1)
    ones_col = jnp.where(lane1 == 0, 1.0, 0.0).astype(BF16)
    bias = bias_ref[...]

    for d in range(2):
        causal = (ci <= ri) if d == 0 else (ci >= ri)
        tri = jnp.where(causal, 1.0, 0.0).astype(BF16)
        tri_t = jnp.where((ri <= ci) if d == 0 else (ri >= ci), 1.0, 0.0).astype(BF16)
        c_s[...] = c0_ref[0, d]
        m_init = tuple(m0_ref[0, :, d * H_C + h:d * H_C + h + 1] for h in range(H_C))

        def chunk(kk, ms, d=d, causal=causal, tri=tri, tri_t=tri_t):
            cidx = kk if d == 0 else nchunk - 1 - kk
            rows = pl.ds(pl.multiple_of(cidx * L, L), L)
            gsm = sm_ref[rows, :].astype(F32) + bias
            lf_all = _log_sigmoid(gsm)
            cum_cols = _dot_split_left(tri, lf_all)
            g_t = gsm.T
            cum_rows = _dot_split(lf_all.T, tri_t)
            new_ms = []
            for h in range(H_C):
                jl, jf = SM_MI + d * H_C + h, SM_MF + d * H_C + h
                cum_c = cum_cols[:, jf:jf + 1]
                cum_r = cum_rows[jf:jf + 1, :]
                li_c = gsm[:, jl:jl + 1]
                li_r = g_t[jl:jl + 1, :]
                m_mem = ms[h]
                log_d = jnp.where(causal, cum_c - cum_r + li_r, -jnp.inf)
                inter = cum_c + m_mem
                m_row = jnp.maximum(inter, jnp.max(log_d, axis=-1, keepdims=True))
                dmat = jnp.exp(log_d - m_row)
                w_inter = jnp.exp(inter - m_row)
                sl = slice(h * DH_C, (h + 1) * DH_C)
                qh, kh, vh = q_ref[rows, sl], k_ref[rows, sl], v_ref[rows, sl]
                s = lax.dot_general(qh, kh, _NT, preferred_element_type=F32) * (scale * dmat)
                sv = jnp.dot(s.astype(BF16), vh, preferred_element_type=F32)
                qc = jnp.dot(qh, c_s[h].astype(BF16), preferred_element_type=F32) * scale
                num = sv + qc[:, :DH_C] * w_inter
                den = jnp.sum(s, axis=-1, keepdims=True) + w_inter * qc[:, DH_C:DH_C + 1]
                den = jnp.maximum(jnp.abs(den), jnp.exp(-m_row))
                h_out = num * (1.0 / den)
                if d == 0:
                    hm_s[rows, sl] = h_out
                else:
                    hm_s[rows, sl] = hm_s[rows, sl] + h_out
                last = cum_c[L - 1:L, :] if d == 0 else cum_c[0:1, :]
                w_s = last - cum_c + li_c
                m_new = jnp.maximum(last + m_mem, jnp.max(w_s, axis=0, keepdims=True))
                decay = jnp.exp(last + m_mem - m_new)
                ws = jnp.exp(w_s - m_new)
                kw_t = (kh.astype(F32) * ws).T.astype(BF16)
                v_aug = jnp.concatenate([vh, ones_col], axis=1)
                c_s[h] = decay * c_s[h] + jnp.dot(kw_t, v_aug, preferred_element_type=F32)
                new_ms.append(m_new)
            return tuple(new_ms)

        m_fin = lax.fori_loop(0, nchunk, chunk, m_init)
        c_out[0, d] = c_s[...]
        for h in range(H_C):
            m_out[0, :, d * H_C + h:d * H_C + h + 1] = m_fin[h]

    for h in range(H_C):
        sl = slice(h * DH_C, (h + 1) * DH_C)
        hm = hm_s[:, sl]
        r = lax.rsqrt(jnp.mean(hm * hm, axis=-1, keepdims=True) + EPS)
        y_ref[:, sl] = (jax.nn.sigmoid(o_ref[:, sl].astype(F32)) * (hm * r * g_ref[...])).astype(y_ref.dtype)


def _dot_split_left(a_bf16, b):
    hi = b.astype(BF16)
    lo = (b - hi.astype(F32)).astype(BF16)
    return (jnp.dot(a_bf16, hi, preferred_element_type=F32) + jnp.dot(a_bf16, lo, preferred_element_type=F32))


def mlstm(p, bias_sm, g_out, c0_aug, m0, prev, *, row0, n_seq, t_seq):
    rb0 = row0 // t_seq
    seg = lambda nm: pl.BlockSpec((t_seq, W_BRANCH), lambda s, b=_seg_block(nm): (rb0 + s, b))
    full = lambda shape: pl.BlockSpec(shape, lambda s: tuple(0 for _ in shape))
    st_spec = pl.BlockSpec((1, 2, H_C, DH_C, 2 * DH_C), lambda s: (s, 0, 0, 0, 0))
    m_spec = pl.BlockSpec((1, 1, 2 * H_C), lambda s: (s, 0, 0))
    in_specs = [seg('mq'), seg('mk'), seg('mv'), seg('mo'),
                pl.BlockSpec((t_seq, LANES), lambda s: (rb0 + s, P_SMALL // LANES)),
                full((1, LANES)), full((1, DH_C)), st_spec, m_spec]
    args = [p, p, p, p, p, bias_sm, g_out.reshape(1, DH_C), c0_aug, m0]
    body, aliases = _continue_output(functools.partial(_mlstm_kernel, t_seq=t_seq), in_specs, args, prev)
    return pl.pallas_call(
        body,
        grid=(n_seq,),
        in_specs=in_specs,
        out_specs=(pl.BlockSpec((t_seq, W_BRANCH), lambda s: (rb0 + s, 0)), st_spec, m_spec),
        out_shape=(jax.ShapeDtypeStruct((R_ALL, W_BRANCH), BF16),
                   jax.ShapeDtypeStruct((n_seq, 2, H_C, DH_C, 2 * DH_C), F32),
                   jax.ShapeDtypeStruct((n_seq, 1, 2 * H_C), F32)),
        scratch_shapes=[pltpu.VMEM((t_seq, W_BRANCH), F32), pltpu.VMEM((H_C, DH_C, 2 * DH_C), F32)],
        input_output_aliases=aliases,
        compiler_params=_cparams(("arbitrary",)),
        name="mlstm",
    )(*args)


MERGE_TM = 512


def _merge_kernel(ya_ref, yb_ref, yc_ref, yd_ref, g0_ref, g1_ref, g2_ref, g3_ref, x_ref, mod_ref,
                  wbr_ref, wout_ref, gn_ref, xo_ref, xn_ref):
    merged = None
    for g, (y_ref, gate_ref) in enumerate(((ya_ref, g0_ref), (yb_ref, g1_ref), (yc_ref, g2_ref), (yd_ref, g3_ref))):
        pg = jnp.dot(y_ref[...], wbr_ref[g], preferred_element_type=F32)
        term = jax.nn.sigmoid(gate_ref[...].astype(F32)) * pg
        merged = term if merged is None else merged + term
    y = jnp.dot(merged.astype(BF16), wout_ref[...], preferred_element_type=F32)
    x = x_ref[...] + mod_ref[0, 2:3, :] * y
    xo_ref[...] = x
    r = lax.rsqrt(jnp.mean(x * x, axis=-1, keepdims=True) + EPS)
    xn_ref[...] = ((x * r * gn_ref[...]) * (1.0 + mod_ref[0, 4:5, :]) + mod_ref[0, 3:4, :]).astype(xn_ref.dtype)


def merge(ys, p, x, mod, wbr, wout, g_ffn):
    tm = MERGE_TM
    br = pl.BlockSpec((tm, W_BRANCH), lambda i: (i, 0))
    gate = lambda g: pl.BlockSpec((tm, D_MODEL), lambda i, g=g: (i, g))
    row = pl.BlockSpec((tm, D_MODEL), lambda i: (i, 0))
    return pl.pallas_call(
        _merge_kernel,
        grid=(R_ALL // tm,),
        in_specs=[br, br, br, br, gate(0), gate(1), gate(2), gate(3), row,
                  pl.BlockSpec((1, 8, D_MODEL), lambda i: (_mod_row_of_tile(i, tm), 0, 0)),
                  pl.BlockSpec((N_BRANCH, W_BRANCH, D_MODEL), lambda i: (0, 0, 0)),
                  pl.BlockSpec((D_MODEL, D_MODEL), lambda i: (0, 0)),
                  pl.BlockSpec((1, D_MODEL), lambda i: (0, 0))],
        out_specs=(row, row),
        out_shape=(jax.ShapeDtypeStruct((R_ALL, D_MODEL), F32), jax.ShapeDtypeStruct((R_ALL, D_MODEL), BF16)),
        compiler_params=_cparams(("arbitrary",)),
        name="merge",
    )(*ys, p, p, p, p, x, mod, wbr, wout, g_ffn.reshape(1, D_MODEL))


def _new_expert(te_ref, i):
    return jnp.logical_or(i == 0, te_ref[i] != te_ref[jnp.maximum(i - 1, 0)])


def _ffn_up_kernel(te_ref, nt_ref, x_ref, wg_ref, wu_ref, h_ref, wgb_ref, wub_ref):
    i = pl.program_id(1)

    @pl.when(_new_expert(te_ref, i))
    def _():
        wgb_ref[...] = wg_ref[0].astype(BF16)
        wub_ref[...] = wu_ref[0].astype(BF16)

    @pl.when(i < nt_ref[0])
    def _():
        x = x_ref[...]
        g = jnp.dot(x, wgb_ref[...], preferred_element_type=F32)
        u = jnp.dot(x, wub_ref[...], preferred_element_type=F32)
        h_ref[...] = (g * jax.nn.sigmoid(g) * u).astype(h_ref.dtype)


def ffn_up(tile_expert, n_tiles, xs, wg, wu, tm, tf):
    r, d = xs.shape
    f = wg.shape[2]
    return pl.pallas_call(
        _ffn_up_kernel,
        grid_spec=pltpu.PrefetchScalarGridSpec(
            num_scalar_prefetch=2,
            grid=(f // tf, r // tm),
            in_specs=[pl.BlockSpec((tm, d), lambda j, i, te, nt: (i, 0)),
                      pl.BlockSpec((1, d, tf), lambda j, i, te, nt: (te[i], 0, j)),
                      pl.BlockSpec((1, d, tf), lambda j, i, te, nt: (te[i], 0, j))],
            out_specs=pl.BlockSpec((tm, tf), lambda j, i, te, nt: (i, j)),
            scratch_shapes=[pltpu.VMEM((d, tf), BF16), pltpu.VMEM((d, tf), BF16)]),
        out_shape=jax.ShapeDtypeStruct((r, f), BF16),
        compiler_params=_cparams(("arbitrary", "arbitrary")),
        name="ffn_up",
    )(tile_expert, n_tiles, xs, wg, wu)


def _ffn_down_kernel(te_ref, nt_ref, h_ref, wd_ref, y_ref, wdb_ref):
    i = pl.program_id(0)

    @pl.when(_new_expert(te_ref, i))
    def _():
        wdb_ref[...] = wd_ref[0].astype(BF16)

    @pl.when(i < nt_ref[0])
    def _():
        y_ref[...] = jnp.dot(h_ref[...], wdb_ref[...], preferred_element_type=F32).astype(y_ref.dtype)


def ffn_down(tile_expert, n_tiles, h, wd, tm):
    r, f = h.shape
    d = wd.shape[2]
    return pl.pallas_call(
        _ffn_down_kernel,
        grid_spec=pltpu.PrefetchScalarGridSpec(
            num_scalar_prefetch=2,
            grid=(r // tm,),
            in_specs=[pl.BlockSpec((tm, f), lambda i, te, nt: (i, 0)),
                      pl.BlockSpec((1, f, d), lambda i, te, nt: (te[i], 0, 0))],
            out_specs=pl.BlockSpec((tm, d), lambda i, te, nt: (i, 0)),
            scratch_shapes=[pltpu.VMEM((f, d), BF16)]),
        out_shape=jax.ShapeDtypeStruct((r, d), BF16),
        compiler_params=_cparams(("arbitrary",)),
        name="ffn_down",
    )(tile_expert, n_tiles, h, wd)


def _ffn_down_res_kernel(h_ref, wd_ref, x_ref, mod_ref, y_ref, wdb_ref):
    @pl.when(pl.program_id(0) == 0)
    def _():
        wdb_ref[...] = wd_ref[...].astype(BF16)

    y = jnp.dot(h_ref[...], wdb_ref[...], preferred_element_type=F32)
    y_ref[...] = x_ref[...] + mod_ref[0, 5:6, :] * y


def ffn_down_residual(h, wd, x, mod):
    tm = 512
    r, f = h.shape
    d = wd.shape[1]
    return pl.pallas_call(
        _ffn_down_res_kernel,
        grid=(r // tm,),
        in_specs=[pl.BlockSpec((tm, f), lambda i: (i, 0)),
                  pl.BlockSpec((f, d), lambda i: (0, 0)),
                  pl.BlockSpec((tm, d), lambda i: (i, 0)),
                  pl.BlockSpec((1, 8, d), lambda i: (_mod_row_of_tile(i, tm), 0, 0))],
        out_specs=pl.BlockSpec((tm, d), lambda i: (i, 0)),
        out_shape=jax.ShapeDtypeStruct((r, d), F32),
        scratch_shapes=[pltpu.VMEM((f, d), BF16)],
        compiler_params=_cparams(("arbitrary",)),
        name="ffn_down_residual",
    )(h, wd, x, mod)


def dense_swiglu_residual(xn, x, mod, wg, wu, wd):
    t = xn.shape[0]
    tm = 512
    n_tiles = t // tm
    te = jnp.zeros((n_tiles,), jnp.int32)
    nt = jnp.full((1,), n_tiles, jnp.int32)
    h = ffn_up(te, nt, xn, wg[None], wu[None], tm=tm, tf=wg.shape[1] // 2)
    return ffn_down_residual(h, wd, x, mod)


MOE_TM = 512


def _moe_routing(logits, tm):
    t = logits.shape[0]
    n_assign = t * TOP_K
    top_v, top_i = lax.top_k(logits, TOP_K)
    gate = jax.nn.softmax(top_v, axis=-1)
    flat_e = top_i.reshape(-1).astype(jnp.int32)
    onehot = (flat_e[:, None] == jnp.arange(N_EXP, dtype=jnp.int32)[None, :])
    blk = LANES
    oh = onehot.astype(F32).reshape(n_assign // blk, blk, N_EXP)
    tril = jnp.tril(jnp.ones((blk, blk), F32))
    within = jnp.einsum('ij,bjk->bik', tril, oh)
    blk_tot = within[:, -1, :]
    blk_off = jnp.cumsum(blk_tot, axis=0) - blk_tot
    csum = (within + blk_off[:, None, :]).reshape(n_assign, N_EXP)
    rank = jnp.sum(jnp.where(onehot, csum - 1.0, 0.0), axis=1).astype(jnp.int32)
    counts = csum[-1].astype(jnp.int32)
    padded = (counts + tm - 1) // tm * tm
    grp_start = jnp.cumsum(padded) - padded
    raw_start = jnp.cumsum(counts) - counts
    slot_of_assign = jnp.sum(jnp.where(onehot, grp_start[None, :], 0), axis=1) + rank

    r_max = n_assign + N_EXP * tm
    tile_start = jnp.arange(r_max // tm, dtype=jnp.int32) * tm
    tile_expert = jnp.sum((tile_start[:, None] >= (grp_start + padded)[None, :]).astype(jnp.int32), axis=1)
    tile_expert = jnp.minimum(tile_expert, N_EXP - 1).astype(jnp.int32)
    n_tiles = (jnp.sum(padded) // tm).astype(jnp.int32).reshape(1)

    order = jnp.argsort(flat_e, stable=True).astype(jnp.int32)
    e_slot = jnp.repeat(tile_expert, tm)
    j = jnp.arange(r_max, dtype=jnp.int32) - grp_start[e_slot]
    src = jnp.clip(raw_start[e_slot] + j, 0, n_assign - 1)
    tok_of_slot = jnp.where(j < counts[e_slot], order[src] // TOP_K, 0)
    return gate, slot_of_assign, tok_of_slot, tile_expert, n_tiles


def moe_swiglu(xn, w_router, wg, wu, wd):
    t, d = xn.shape
    tm = MOE_TM
    logits = jnp.dot(xn.astype(F32), w_router, precision=lax.Precision.HIGHEST)
    gate, slot_of_assign, tok_of_slot, tile_expert, n_tiles = _moe_routing(logits, tm)
    xs = jnp.take(xn, tok_of_slot, axis=0)
    h = ffn_up(tile_expert, n_tiles, xs, wg, wu, tm=tm, tf=wg.shape[2] // 4)
    ys = ffn_down(tile_expert, n_tiles, h, wd, tm=tm)
    pair = jnp.take(ys, slot_of_assign, axis=0).reshape(t, TOP_K, d).astype(F32)
    return gate[:, 0:1] * pair[:, 0] + gate[:, 1:2] * pair[:, 1]


def _layer(x, cond, lp, l, ctx, tabs_a, tabs_d):
    mod = modulation(cond, lp['w_mod'], lp['b_mod']).reshape(cond.shape[0], 6, D_MODEL)
    mod = jnp.pad(mod, ((0, 0), (0, 2), (0, 0)))
    p = in_proj(x, mod, lp['g_norm_mix'], pack_w_in(lp['w_in']))

    mw = pack_mla_weights(lp)
    q_a, k_a, v_a, ckv, kr = mla_prep(p, mw, tabs_a)
    kr_c = jnp.pad(ctx['mla_krope'].reshape(-1, ROPE_A), ((0, 0), (0, LANES - ROPE_A)))
    kc_a, vc_a = mla_prep_cache(ctx['mla_ckv'].reshape(-1, KV_RANK), kr_c, mw, tabs_a)
    ya = mla_attention(q_a, k_a, v_a, None, None, None, row0=0, n_seq=N_CTX_SEQ, t_seq=T_CTX, tq=T_CTX)
    ya = mla_attention(q_a, k_a, v_a, kc_a, vc_a, ya, row0=R_CTX, n_seq=N_LAT_SEQ, t_seq=T_LAT, tq=ATTN_TQ)

    rw = pack_rglru_weights(lp)
    yb, st_rg = rglru(p, rw, jnp.zeros((N_CTX_SEQ, 2, W_B), F32), None, row0=0, n_seq=N_CTX_SEQ, t_seq=T_CTX)
    yb, _ = rglru(p, rw, ctx['rglru'], yb, row0=R_CTX, n_seq=N_LAT_SEQ, t_seq=T_LAT)

    bias_sm = jnp.zeros((LANES,), F32).at[SM_MI:SM_MI + 2 * H_C].set(lp['b_ml_i'].reshape(-1))
    bias_sm = bias_sm.at[SM_MF:SM_MF + 2 * H_C].set(lp['b_ml_f'].reshape(-1)).reshape(1, LANES)
    c0_ctx = jnp.zeros((N_CTX_SEQ, 2, H_C, DH_C, 2 * DH_C), F32)
    m0_ctx = jnp.zeros((N_CTX_SEQ, 1, 2 * H_C), F32)
    c0_lat = jnp.concatenate([ctx['mlstm_C'], ctx['mlstm_n'][..., None],
                              jnp.zeros(ctx['mlstm_n'].shape + (DH_C - 1,), F32)], axis=-1)
    m0_lat = ctx['mlstm_m'].reshape(N_LAT_SEQ, 1, 2 * H_C)
    yc, c_fin, m_fin = mlstm(p, bias_sm, lp['g_ml_out'], c0_ctx, m0_ctx, None, row0=0, n_seq=N_CTX_SEQ, t_seq=T_CTX)
    yc, _, _ = mlstm(p, bias_sm, lp['g_ml_out'], c0_lat, m0_lat, yc, row0=R_CTX, n_seq=N_LAT_SEQ, t_seq=T_LAT)

    lambda_init = 0.8 - 0.6 * math.exp(-0.3 * l)
    qd, kd_own, kd_plain = diff_prep(p, lp, tabs_d)
    dkc = ctx['diff_k'].reshape(-1, W_BRANCH)
    dvc = ctx['diff_v'].reshape(-1, W_BRANCH)
    yd = diff_attention(lp['diff_lambda'], qd, kd_own, p, None, None, lp['g_diff_sub'], None,
                        row0=0, n_seq=N_CTX_SEQ, t_seq=T_CTX, tq=T_CTX, lambda_init=lambda_init)
    yd = diff_attention(lp['diff_lambda'], qd, kd_own, p, dkc, dvc, lp['g_diff_sub'], yd,
                        row0=R_CTX, n_seq=N_LAT_SEQ, t_seq=T_LAT, tq=ATTN_TQ, lambda_init=lambda_init)

    x, xn = merge((ya, yb, yc, yd), p, x, mod, lp['w_br'].astype(BF16), lp['w_out'].astype(BF16), lp['g_norm_ffn'])

    if 'ffn' in lp:
        x = dense_swiglu_residual(xn, x, mod, *lp['ffn'])
    else:
        f = moe_swiglu(xn, *lp['moe'])
        g2 = jnp.concatenate([jnp.broadcast_to(mod[0:1, 5], (R_CTX, D_MODEL)),
                              jnp.repeat(mod[1:1 + N_LAT_SEQ, 5], T_LAT, axis=0)], axis=0)
        x = x + g2 * f

    dv0 = _seg_block('dv') * W_BRANCH
    ctx_out = (ckv[:R_CTX].reshape(N_CTX_SEQ, T_CTX, KV_RANK),
               kr[:R_CTX, :ROPE_A].reshape(N_CTX_SEQ, T_CTX, ROPE_A),
               kd_plain[:R_CTX].reshape(N_CTX_SEQ, T_CTX, 2, H_D, DH_D),
               p[:R_CTX, dv0:dv0 + W_BRANCH].astype(F32).reshape(N_CTX_SEQ, T_CTX, H_D, 2 * DH_D),
               st_rg,
               c_fin[..., :DH_C],
               c_fin[..., DH_C],
               m_fin.reshape(N_CTX_SEQ, 2, H_C))
    return x, ctx_out


def kernel(x_prompt, x_sample, cache_mla_ckv, cache_mla_krope, cache_diff_k, cache_diff_v,
           state_rglru, state_mlstm_C, state_mlstm_n, state_mlstm_m, c, c_ctx,
           w_mod, b_mod, g_norm_mix, g_norm_ffn, w_in, g_mla_qlat, w_mla_uq, g_mla_kvlat, w_mla_ukv,
           g_mla_qn, g_mla_kn, w_conv_rg, b_conv_rg, w_rg_a, b_rg_a, w_rg_x, b_rg_x, rg_lambda,
           b_ml_i, b_ml_f, g_ml_out, g_diff_qn, g_diff_kn, diff_lambda, g_diff_sub, w_br, w_out,
           w_ffn_gate, w_ffn_up, w_ffn_down, w_router, w_moe_gate, w_moe_up, w_moe_down):
    assert x_prompt.shape == (N_CTX_SEQ, T_CTX, D_MODEL) and x_sample.shape == (N_LAT_SEQ, T_LAT, D_MODEL)
    tabs_a = _rope_tables(ROPE_A, (NOPE_A,), MLA_TM)
    tabs_d = _rope_tables(DH_D, (0, DH_D), DIFF_TM)
    cond = jnp.concatenate([c_ctx.reshape(1, D_MODEL), c, jnp.zeros((16 - 1 - N_LAT_SEQ, D_MODEL), F32)], axis=0)
    x = jnp.concatenate([x_prompt.reshape(R_CTX, D_MODEL), x_sample.reshape(R_LAT, D_MODEL)], axis=0)
    new = []
    for l in range(DEPTH):
        lp = dict(w_mod=w_mod[l], b_mod=b_mod[l], g_norm_mix=g_norm_mix[l], g_norm_ffn=g_norm_ffn[l], w_in=w_in[l],
                  g_mla_qlat=g_mla_qlat[l], w_mla_uq=w_mla_uq[l], g_mla_kvlat=g_mla_kvlat[l], w_mla_ukv=w_mla_ukv[l],
                  g_mla_qn=g_mla_qn[l], g_mla_kn=g_mla_kn[l], w_conv_rg=w_conv_rg[l], b_conv_rg=b_conv_rg[l],
                  w_rg_a=w_rg_a[l], b_rg_a=b_rg_a[l], w_rg_x=w_rg_x[l], b_rg_x=b_rg_x[l], rg_lambda=rg_lambda[l],
                  b_ml_i=b_ml_i[l], b_ml_f=b_ml_f[l], g_ml_out=g_ml_out[l], g_diff_qn=g_diff_qn[l],
                  g_diff_kn=g_diff_kn[l], diff_lambda=diff_lambda[l], g_diff_sub=g_diff_sub[l],
                  w_br=w_br[l], w_out=w_out[l])
        if l % 2 == 0:
            lp['ffn'] = (w_ffn_gate[l // 2], w_ffn_up[l // 2], w_ffn_down[l // 2])
        else:
            lp['moe'] = (w_router[l // 2], w_moe_gate[l // 2], w_moe_up[l // 2], w_moe_down[l // 2])
        ctx_l = dict(mla_ckv=cache_mla_ckv[:, l], mla_krope=cache_mla_krope[:, l], diff_k=cache_diff_k[:, l],
                     diff_v=cache_diff_v[:, l], rglru=state_rglru[:, l], mlstm_C=state_mlstm_C[:, l],
                     mlstm_n=state_mlstm_n[:, l], mlstm_m=state_mlstm_m[:, l])
        x, st = _layer(x, cond, lp, l, ctx_l, tabs_a, tabs_d)
        new.append(st)
    outs = tuple(jnp.stack([s[i] for s in new], axis=1) for i in range(8))
    return (x[:R_CTX].reshape(N_CTX_SEQ, T_CTX, D_MODEL), x[R_CTX:].reshape(N_LAT_SEQ, T_LAT, D_MODEL)) + outs
```

```python
import functools
import math

import jax
import jax.numpy as jnp
import numpy as np
from jax import lax
from jax.experimental import pallas as pl
from jax.experimental.pallas import tpu as pltpu

D_MODEL = 1024
DEPTH = 2
GRID_W = 64
ROPE_BASE = 10000.0
EPS = 1e-6
N_BRANCH = 4
W_BRANCH = D_MODEL // 2

H_A = 8
NOPE_A = 64
ROPE_A = 32
V_A = W_BRANCH // H_A
Q_RANK = D_MODEL // 4
KV_RANK = D_MODEL // 8
MLA_SCALE = (NOPE_A + ROPE_A) ** -0.5

W_B = W_BRANCH
NB_B = 8
BW_B = W_B // NB_B
CONV_W = 4
RG_C = 8.0

H_C = 4
DH_C = W_BRANCH // H_C
ML_CHUNK = 128

H_D = 4
DH_D = W_BRANCH // (2 * H_D)
DIFF_SCALE = DH_D ** -0.5

N_EXP = 8
TOP_K = 2

V7X_VMEM_LIMIT_BYTES = 56 * 1024 * 1024
LANES = 128
SUBLANES = 8

BF16 = jnp.bfloat16
F32 = jnp.float32

P_GATE = 0
P_SEG = N_BRANCH * D_MODEL
SEG_NAMES = ('rg_x', 'rg_gate', 'mq', 'mk', 'mv', 'mo', 'dq', 'dk', 'dv')
P_QLAT = P_SEG + 9 * W_BRANCH
P_KVLAT = P_QLAT + Q_RANK
P_SMALL = P_KVLAT + KV_RANK
P_WIDTH = P_SMALL + LANES
SM_MI = ROPE_A
SM_MF = ROPE_A + 2 * H_C


def _seg_block(name):
    return (P_SEG + SEG_NAMES.index(name) * W_BRANCH) // W_BRANCH


N_CTX_SEQ, T_CTX = 16, 256
N_LAT_SEQ, T_LAT = 8, 1024
PAST_LEN = 512
R_CTX = N_CTX_SEQ * T_CTX
R_LAT = N_LAT_SEQ * T_LAT
R_ALL = R_CTX + R_LAT


def _cparams(sem):
    return pltpu.CompilerParams(dimension_semantics=sem, vmem_limit_bytes=V7X_VMEM_LIMIT_BYTES)


def _mod_row_of_tile(i, tm):
    n_ctx_tiles = R_CTX // tm
    per_seq = T_LAT // tm
    return jnp.where(i < n_ctx_tiles, 0, 1 + (i - n_ctx_tiles) // per_seq)


def _pos_block_of_tile(i, tm):
    n_ctx_tiles = R_CTX // tm
    per_seq = T_LAT // tm
    return jnp.where(i < n_ctx_tiles, per_seq, (i - n_ctx_tiles) % per_seq)


def _mod_kernel(c_ref, w_ref, b_ref, o_ref):
    c = c_ref[...]
    s = (c * jax.nn.sigmoid(c)).astype(BF16)
    o_ref[...] = jnp.dot(s, w_ref[...].astype(BF16), preferred_element_type=F32) + b_ref[...]


def modulation(cond, w_mod, b_mod):
    m, d = cond.shape
    n = w_mod.shape[1]
    tn = 1536
    return pl.pallas_call(
        _mod_kernel,
        grid=(n // tn,),
        in_specs=[pl.BlockSpec((m, d), lambda j: (0, 0)),
                  pl.BlockSpec((d, tn), lambda j: (0, j)),
                  pl.BlockSpec((1, tn), lambda j: (0, j))],
        out_specs=pl.BlockSpec((m, tn), lambda j: (0, j)),
        out_shape=jax.ShapeDtypeStruct((m, n), F32),
        compiler_params=_cparams(("arbitrary",)),
        name="modulation",
    )(cond, w_mod, b_mod.reshape(1, n))


def _in_proj_kernel(x_ref, mod_ref, g_ref, w_ref, o_ref, xn_ref):
    @pl.when(pl.program_id(1) == 0)
    def _():
        x = x_ref[...]
        r = lax.rsqrt(jnp.mean(x * x, axis=-1, keepdims=True) + EPS)
        sh = mod_ref[0, 0:1, :]
        sc = mod_ref[0, 1:2, :]
        xn_ref[...] = ((x * r * g_ref[...]) * (1.0 + sc) + sh).astype(BF16)

    o_ref[...] = jnp.dot(xn_ref[...], w_ref[...], preferred_element_type=F32).astype(o_ref.dtype)


def in_proj(x, mod, g, w_p):
    tm, tn = 1024, 1024
    m, d = x.shape
    n = w_p.shape[1]
    return pl.pallas_call(
        _in_proj_kernel,
        grid=(m // tm, n // tn),
        in_specs=[pl.BlockSpec((tm, d), lambda i, j: (i, 0)),
                  pl.BlockSpec((1, 8, d), lambda i, j: (_mod_row_of_tile(i, tm), 0, 0)),
                  pl.BlockSpec((1, d), lambda i, j: (0, 0)),
                  pl.BlockSpec((d, tn), lambda i, j: (0, j))],
        out_specs=pl.BlockSpec((tm, tn), lambda i, j: (i, j)),
        out_shape=jax.ShapeDtypeStruct((m, n), BF16),
        scratch_shapes=[pltpu.VMEM((tm, d), BF16)],
        compiler_params=_cparams(("arbitrary", "arbitrary")),
        name="in_proj",
    )(x, mod, g.reshape(1, d), w_p)


def pack_w_in(w_in):
    o_rg = Q_RANK + KV_RANK + ROPE_A
    o_mi = o_rg + 6 * W_BRANCH
    o_dq = o_mi + 4 * H_C
    o_gate = o_dq + 3 * W_BRANCH
    pad = jnp.zeros((w_in.shape[0], LANES - ROPE_A - 4 * H_C), w_in.dtype)
    parts = [w_in[:, o_gate:], w_in[:, o_rg:o_mi], w_in[:, o_dq:o_gate], w_in[:, :Q_RANK + KV_RANK],
             w_in[:, Q_RANK + KV_RANK:o_rg], w_in[:, o_mi:o_dq], pad]
    return jnp.concatenate(parts, axis=1).astype(BF16)


def _rope_tables(rot_dim, lane_starts, tm):
    rows = T_LAT // GRID_W
    r, c = np.meshgrid(np.arange(rows, dtype=np.float32), np.arange(GRID_W, dtype=np.float32), indexing='ij')
    nf = rot_dim // 4
    inv = (np.float32(ROPE_BASE) ** (-np.arange(nf, dtype=np.float32) / np.float32(nf))).astype(np.float32)
    ang = np.stack([r.reshape(-1)[:, None] * inv, c.reshape(-1)[:, None] * inv], axis=1).astype(np.float32)
    cos, sin = np.cos(ang).astype(np.float32), np.sin(ang).astype(np.float32)
    tc = np.ones((T_LAT + tm, LANES), np.float32)
    ta = np.zeros((T_LAT + tm, LANES), np.float32)
    tb = np.zeros((T_LAT + tm, LANES), np.float32)
    for s0 in lane_starts:
        for a in range(2):
            lo = s0 + a * 2 * nf
            tc[:T_LAT, lo:lo + nf] = cos[:, a]
            tc[:T_LAT, lo + nf:lo + 2 * nf] = cos[:, a]
            ta[:T_LAT, lo:lo + nf] = -sin[:, a]
            tb[:T_LAT, lo + nf:lo + 2 * nf] = sin[:, a]
    return jnp.asarray(tc), jnp.asarray(ta), jnp.asarray(tb)


def _rope(x, c, sa, sb, half):
    return x * c + pltpu.roll(x, LANES - half, 1) * sa + pltpu.roll(x, half, 1) * sb


MLA_TM = 512
ATTN_TQ = 512
QK_A = NOPE_A + ROPE_A


def _mla_prep_kernel(*refs, has_q, norm_ckv):
    if has_q:
        (qlat_ref, gq_ref, wuq_ref, gqn_ref, kv_ref, sm_ref, gkv_ref, wkc_ref, wv_ref, gkn_ref,
         c_ref, sa_ref, sb_ref, q_o, k_o, v_o, ckv_o, kr_o) = refs
    else:
        (kv_ref, sm_ref, gkv_ref, wkc_ref, wv_ref, gkn_ref, c_ref, sa_ref, sb_ref, k_o, v_o) = refs
    c, sa, sb = c_ref[...], sa_ref[...], sb_ref[...]

    def heads(z, g, o_ref, scale):
        for h in range(H_A):
            s = z[:, h * LANES:(h + 1) * LANES]
            r = lax.rsqrt(jnp.sum(s * s, axis=-1, keepdims=True) * (1.0 / QK_A) + EPS)
            y = _rope(s * r * g, c, sa, sb, ROPE_A // 4)
            if scale != 1.0:
                y = y * scale
            o_ref[:, h * LANES:(h + 1) * LANES] = y.astype(o_ref.dtype)

    if has_q:
        ql = qlat_ref[...].astype(F32)
        qn = ql * lax.rsqrt(jnp.mean(ql * ql, axis=-1, keepdims=True) + EPS) * gq_ref[...]
        q = jnp.dot(qn.astype(BF16), wuq_ref[...], preferred_element_type=F32)
        heads(q, gqn_ref[...], q_o, MLA_SCALE)

    kv = kv_ref[...].astype(F32)
    if norm_ckv:
        ckv = kv * lax.rsqrt(jnp.mean(kv * kv, axis=-1, keepdims=True) + EPS) * gkv_ref[...]
    else:
        ckv = kv
    sm = sm_ref[...]
    ckv_b = ckv.astype(BF16)
    kin = jnp.concatenate([ckv_b, sm.astype(BF16)], axis=1)
    k = jnp.dot(kin, wkc_ref[...], preferred_element_type=F32)
    heads(k, gkn_ref[...], k_o, 1.0)
    v_o[...] = jnp.dot(ckv_b, wv_ref[...], preferred_element_type=F32).astype(v_o.dtype)
    if has_q:
        ckv_o[...] = ckv
        kr_o[...] = sm.astype(F32)


def pack_mla_weights(lp):
    wuq = lp['w_mla_uq'].reshape(Q_RANK, H_A, QK_A)
    wuq_p = jnp.pad(wuq, ((0, 0), (0, 0), (0, LANES - QK_A))).reshape(Q_RANK, H_A * LANES).astype(BF16)
    wukv = lp['w_mla_ukv'].reshape(KV_RANK, H_A, NOPE_A + V_A)
    wk = jnp.pad(wukv[:, :, :NOPE_A], ((0, 0), (0, 0), (0, LANES - NOPE_A))).reshape(KV_RANK, H_A * LANES)
    place = np.zeros((LANES, H_A, LANES), np.float32)
    for h in range(H_A):
        place[np.arange(ROPE_A), h, NOPE_A + np.arange(ROPE_A)] = 1.0
    wkc = jnp.concatenate([wk, jnp.asarray(place.reshape(LANES, H_A * LANES))], axis=0).astype(BF16)
    wv = wukv[:, :, NOPE_A:]
    wv_even = jnp.pad(wv, ((0, 0), (0, 0), (0, LANES - V_A)))
    wv_odd = jnp.pad(wv, ((0, 0), (0, 0), (LANES - V_A, 0)))
    odd = (np.arange(H_A) % 2 == 1)[None, :, None]
    wv_p = jnp.where(odd, wv_odd, wv_even).reshape(KV_RANK, H_A * LANES).astype(BF16)
    pad_g = lambda g: jnp.pad(g, (0, LANES - QK_A)).reshape(1, LANES)
    return dict(wuq=wuq_p, wkc=wkc, wv=wv_p, gqn=pad_g(lp['g_mla_qn']), gkn=pad_g(lp['g_mla_kn']),
                gq=lp['g_mla_qlat'].reshape(1, Q_RANK), gkv=lp['g_mla_kvlat'].reshape(1, KV_RANK))


def mla_prep(p, mw, tabs):
    tm = MLA_TM
    n = R_ALL // tm
    full = lambda shape: pl.BlockSpec(shape, lambda i: (0, 0))
    tab = pl.BlockSpec((tm, LANES), lambda i: (_pos_block_of_tile(i, tm), 0))
    wide = H_A * LANES
    out_shape = (jax.ShapeDtypeStruct((R_ALL, wide), BF16), jax.ShapeDtypeStruct((R_ALL, wide), BF16),
                 jax.ShapeDtypeStruct((R_ALL, wide), BF16), jax.ShapeDtypeStruct((R_ALL, KV_RANK), F32),
                 jax.ShapeDtypeStruct((R_ALL, LANES), F32))
    row = lambda w: pl.BlockSpec((tm, w), lambda i: (i, 0))
    return pl.pallas_call(
        functools.partial(_mla_prep_kernel, has_q=True, norm_ckv=True),
        grid=(n,),
        in_specs=[pl.BlockSpec((tm, Q_RANK), lambda i: (i, P_QLAT // Q_RANK)), full((1, Q_RANK)),
                  full((Q_RANK, wide)), full((1, LANES)),
                  pl.BlockSpec((tm, KV_RANK), lambda i: (i, P_KVLAT // KV_RANK)),
                  pl.BlockSpec((tm, LANES), lambda i: (i, P_SMALL // LANES)), full((1, KV_RANK)),
                  full((2 * LANES, wide)), full((KV_RANK, wide)), full((1, LANES)), tab, tab, tab],
        out_specs=(row(wide), row(wide), row(wide), row(KV_RANK), row(LANES)),
        out_shape=out_shape,
        compiler_params=_cparams(("arbitrary",)),
        name="mla_prep",
    )(p, mw['gq'], mw['wuq'], mw['gqn'], p, p, mw['gkv'], mw['wkc'], mw['wv'], mw['gkn'], *tabs)


def mla_prep_cache(ckv_c, kr_c, mw, tabs):
    tm = MLA_TM
    r = ckv_c.shape[0]
    full = lambda shape: pl.BlockSpec(shape, lambda i: (0, 0))
    ident = pl.BlockSpec((tm, LANES), lambda i: (T_LAT // tm, 0))
    wide = H_A * LANES
    row = lambda w: pl.BlockSpec((tm, w), lambda i: (i, 0))
    return pl.pallas_call(
        functools.partial(_mla_prep_kernel, has_q=False, norm_ckv=False),
        grid=(r // tm,),
        in_specs=[row(KV_RANK), row(LANES), full((1, KV_RANK)), full((2 * LANES, wide)), full((KV_RANK, wide)),
                  full((1, LANES)), ident, ident, ident],
        out_specs=(row(wide), row(wide)),
        out_shape=(jax.ShapeDtypeStruct((r, wide), BF16), jax.ShapeDtypeStruct((r, wide), BF16)),
        compiler_params=_cparams(("arbitrary",)),
        name="mla_prep_cache",
    )(ckv_c, kr_c, mw['gkv'], mw['wkc'], mw['wv'], mw['gkn'], *tabs)


_NT = (((1,), (1,)), ((), ()))


def _mla_attn_kernel(*refs, has_cache):
    if has_cache:
        q_ref, ko_ref, vo_ref, kc_ref, vc_ref, o_ref = refs
    else:
        q_ref, ko_ref, vo_ref, o_ref = refs
    acc = jnp.zeros(o_ref.shape, F32)
    for hh in range(2):
        sl = slice(hh * LANES, (hh + 1) * LANES)
        q = q_ref[:, sl]
        s_o = lax.dot_general(q, ko_ref[:, sl], _NT, preferred_element_type=F32)
        m = jnp.max(s_o, axis=-1, keepdims=True)
        if has_cache:
            s_c = lax.dot_general(q, kc_ref[:, sl], _NT, preferred_element_type=F32)
            m = jnp.maximum(m, jnp.max(s_c, axis=-1, keepdims=True))
        e_o = jnp.exp(s_o - m)
        l = jnp.sum(e_o, axis=-1, keepdims=True)
        pv = jnp.dot(e_o.astype(BF16), vo_ref[:, sl], preferred_element_type=F32)
        if has_cache:
            e_c = jnp.exp(s_c - m)
            l = l + jnp.sum(e_c, axis=-1, keepdims=True)
            pv = pv + jnp.dot(e_c.astype(BF16), vc_ref[:, sl], preferred_element_type=F32)
        acc = acc + pv * (1.0 / l)
    o_ref[...] = acc.astype(o_ref.dtype)


def _continue_output(kernel, in_specs, args, prev):
    if prev is None:
        return kernel, {}
    idx = len(args)
    in_specs.append(pl.BlockSpec(memory_space=pl.ANY))
    args.append(prev)

    def skipping(*refs):
        return kernel(*refs[:idx], *refs[idx + 1:])

    return skipping, {idx: 0}


def mla_attention(q, k, v, kc, vc, prev, *, row0, n_seq, t_seq, tq):
    has_cache = kc is not None
    n_pair = H_A // 2
    nq = t_seq // tq
    qb0, kb0 = row0 // tq, row0 // t_seq
    in_specs = [pl.BlockSpec((tq, 2 * LANES), lambda s, p, i: (qb0 + s * nq + i, p)),
                pl.BlockSpec((t_seq, 2 * LANES), lambda s, p, i: (kb0 + s, p)),
                pl.BlockSpec((t_seq, 2 * LANES), lambda s, p, i: (kb0 + s, p))]
    args = [q, k, v]
    if has_cache:
        in_specs += [pl.BlockSpec((PAST_LEN, 2 * LANES), lambda s, p, i: (s, p)),
                     pl.BlockSpec((PAST_LEN, 2 * LANES), lambda s, p, i: (s, p))]
        args += [kc, vc]
    body, aliases = _continue_output(functools.partial(_mla_attn_kernel, has_cache=has_cache), in_specs, args, prev)
    return pl.pallas_call(
        body,
        grid=(n_seq, n_pair, nq),
        in_specs=in_specs,
        out_specs=pl.BlockSpec((tq, LANES), lambda s, p, i: (qb0 + s * nq + i, p)),
        out_shape=jax.ShapeDtypeStruct((R_ALL, W_BRANCH), BF16),
        input_output_aliases=aliases,
        compiler_params=_cparams(("arbitrary", "arbitrary", "arbitrary")),
        name="mla_attention",
    )(*args)


DIFF_TM = 512


def _diff_prep_kernel(dq_ref, dk_ref, gq_ref, gk_ref, c_ref, sa_ref, sb_ref, q_o, ko_o, kp_o):
    c, sa, sb = c_ref[...], sa_ref[...], sb_ref[...]
    lane = lax.broadcasted_iota(jnp.int32, (1, LANES), 1)
    lo = lane < DH_D

    def normed(x, g):
        x2 = x * x
        s_lo = jnp.sum(jnp.where(lo, x2, 0.0), axis=-1, keepdims=True)
        s_hi = jnp.sum(jnp.where(lo, 0.0, x2), axis=-1, keepdims=True)
        ms = jnp.where(lo, s_lo, s_hi) * (1.0 / DH_D)
        return x * lax.rsqrt(ms + EPS) * g

    for j in range(W_BRANCH // LANES):
        sl = slice(j * LANES, (j + 1) * LANES)
        qn = normed(dq_ref[:, sl].astype(F32), gq_ref[...])
        q_o[:, sl] = (_rope(qn, c, sa, sb, DH_D // 4) * DIFF_SCALE).astype(q_o.dtype)
        kn = normed(dk_ref[:, sl].astype(F32), gk_ref[...])
        kp_o[:, sl] = kn
        ko_o[:, sl] = _rope(kn, c, sa, sb, DH_D // 4).astype(ko_o.dtype)


def diff_prep(p, lp, tabs):
    tm = DIFF_TM
    tile2 = lambda g: jnp.concatenate([g, g]).reshape(1, LANES)
    full = pl.BlockSpec((1, LANES), lambda i: (0, 0))
    tab = pl.BlockSpec((tm, LANES), lambda i: (_pos_block_of_tile(i, tm), 0))
    row = pl.BlockSpec((tm, W_BRANCH), lambda i: (i, 0))
    dq_b, dk_b = _seg_block('dq'), _seg_block('dk')
    return pl.pallas_call(
        _diff_prep_kernel,
        grid=(R_ALL // tm,),
        in_specs=[pl.BlockSpec((tm, W_BRANCH), lambda i: (i, dq_b)), pl.BlockSpec((tm, W_BRANCH), lambda i: (i, dk_b)),
                  full, full, tab, tab, tab],
        out_specs=(row, row, row),
        out_shape=(jax.ShapeDtypeStruct((R_ALL, W_BRANCH), BF16), jax.ShapeDtypeStruct((R_ALL, W_BRANCH), BF16),
                   jax.ShapeDtypeStruct((R_ALL, W_BRANCH), F32)),
        compiler_params=_cparams(("arbitrary",)),
        name="diff_prep",
    )(p, p, tile2(lp['g_diff_qn']), tile2(lp['g_diff_kn']), *tabs)


def _diff_attn_kernel(*refs, has_cache, lambda_init):
    if has_cache:
        dl_ref, q1_ref, q2_ref, k1_ref, k2_ref, v_ref, k1c_ref, k2c_ref, vc_ref, g_ref, o_ref = refs
    else:
        dl_ref, q1_ref, q2_ref, k1_ref, k2_ref, v_ref, g_ref, o_ref = refs
    half = pl.program_id(1) % 2
    lane = lax.broadcasted_iota(jnp.int32, (1, LANES), 1)
    mine = (lane // DH_D) == half
    dl = dl_ref[...]
    lam = (jnp.exp(jnp.sum(dl[0:1] * dl[1:2], axis=-1, keepdims=True))
           - jnp.exp(jnp.sum(dl[2:3] * dl[3:4], axis=-1, keepdims=True)) + lambda_init)

    def attend(q_ref, k_ref, kc_ref):
        q = jnp.where(mine, q_ref[...], jnp.zeros((), q_ref.dtype))
        s_o = lax.dot_general(q, k_ref[...].astype(BF16), _NT, preferred_element_type=F32)
        m = jnp.max(s_o, axis=-1, keepdims=True)
        if has_cache:
            s_c = lax.dot_general(q, kc_ref[...].astype(BF16), _NT, preferred_element_type=F32)
            m = jnp.maximum(m, jnp.max(s_c, axis=-1, keepdims=True))
        e_o = jnp.exp(s_o - m)
        l = jnp.sum(e_o, axis=-1, keepdims=True)
        pv = jnp.dot(e_o.astype(BF16), v_ref[...].astype(BF16), preferred_element_type=F32)
        if has_cache:
            e_c = jnp.exp(s_c - m)
            l = l + jnp.sum(e_c, axis=-1, keepdims=True)
            pv = pv + jnp.dot(e_c.astype(BF16), vc_ref[...].astype(BF16), preferred_element_type=F32)
        return pv * (1.0 / l)

    y = attend(q1_ref, k1_ref, k1c_ref if has_cache else None) - lam * attend(q2_ref, k2_ref, k2c_ref if has_cache else None)
    r = lax.rsqrt(jnp.mean(y * y, axis=-1, keepdims=True) + EPS)
    o_ref[...] = ((y * r * g_ref[...]) * (1.0 - lambda_init)).astype(o_ref.dtype)


def diff_attention(dl, qd, kd, p, kc, vc, g_sub, prev, *, row0, n_seq, t_seq, tq, lambda_init):
    has_cache = kc is not None
    nq = t_seq // tq
    qb0, kb0 = row0 // tq, row0 // t_seq
    dv_b = _seg_block('dv') * (W_BRANCH // LANES)
    in_specs = [pl.BlockSpec((4, DH_D), lambda s, h, i: (0, 0)),
                pl.BlockSpec((tq, LANES), lambda s, h, i: (qb0 + s * nq + i, h // 2)),
                pl.BlockSpec((tq, LANES), lambda s, h, i: (qb0 + s * nq + i, 2 + h // 2)),
                pl.BlockSpec((t_seq, LANES), lambda s, h, i: (kb0 + s, h // 2)),
                pl.BlockSpec((t_seq, LANES), lambda s, h, i: (kb0 + s, 2 + h // 2)),
                pl.BlockSpec((t_seq, LANES), lambda s, h, i: (kb0 + s, dv_b + h))]
    args = [dl, qd, qd, kd, kd, p]
    if has_cache:
        in_specs += [pl.BlockSpec((PAST_LEN, LANES), lambda s, h, i: (s, h // 2)),
                     pl.BlockSpec((PAST_LEN, LANES), lambda s, h, i: (s, 2 + h // 2)),
                     pl.BlockSpec((PAST_LEN, LANES), lambda s, h, i: (s, h))]
        args += [kc, kc, vc]
    in_specs.append(pl.BlockSpec((1, LANES), lambda s, h, i: (0, 0)))
    args.append(g_sub.reshape(1, LANES))
    body, aliases = _continue_output(
        functools.partial(_diff_attn_kernel, has_cache=has_cache, lambda_init=lambda_init), in_specs, args, prev)
    return pl.pallas_call(
        body,
        grid=(n_seq, H_D, nq),
        in_specs=in_specs,
        out_specs=pl.BlockSpec((tq, LANES), lambda s, h, i: (qb0 + s * nq + i, h)),
        out_shape=jax.ShapeDtypeStruct((R_ALL, W_BRANCH), BF16),
        input_output_aliases=aliases,
        compiler_params=_cparams(("arbitrary", "arbitrary", "arbitrary")),
        name="diff_attention",
    )(*args)


def _neg_expm1(z):
    series = -z * (1.0 + z * (0.5 + z * (1.0 / 6.0 + z * (1.0 / 24.0 + z * (1.0 / 120.0 + z * (1.0 / 720.0))))))
    return jnp.where(z > -0.25, series, 1.0 - jnp.exp(z))


def _softplus(z):
    return jnp.maximum(z, 0.0) + jnp.log(1.0 + jnp.exp(-jnp.abs(z)))


def _gelu_tanh(x):
    return 0.5 * x * (1.0 + jnp.tanh(math.sqrt(2.0 / math.pi) * (x + 0.044715 * (x * x * x))))


def _rglru_kernel(x_ref, gate_ref, wc_ref, bc_ref, wg_ref, bg_ref, lam_ref, h0_ref, y_ref, st_ref,
                  a_s, u_s, h_s, *, t_seq):
    t = t_seq
    x = x_ref[...].astype(F32)
    row = lax.broadcasted_iota(jnp.int32, (t, W_B), 0)
    wc = wc_ref[...]
    xc = (wc[0:1] * jnp.where(row >= 2, pltpu.roll(x, 2, 0), 0.0)
          + wc[1:2] * jnp.where(row >= 1, pltpu.roll(x, 1, 0), 0.0)
          + wc[2:3] * x
          + wc[3:4] * jnp.where(row < t - 1, pltpu.roll(x, t - 1, 0), 0.0)
          + bc_ref[...])
    gates = jnp.dot(xc.astype(BF16), wg_ref[...], preferred_element_type=F32) + bg_ref[...]
    r8 = row % SUBLANES
    nblk = t // SUBLANES
    hsum = None
    for d in range(2):
        rg = jax.nn.sigmoid(gates[:, (2 * d) * W_B:(2 * d + 1) * W_B])
        ig = jax.nn.sigmoid(gates[:, (2 * d + 1) * W_B:(2 * d + 2) * W_B])
        log_a = -RG_C * rg * _softplus(-lam_ref[d:d + 1, :])
        a = jnp.exp(log_a)
        u = jnp.sqrt(_neg_expm1(2.0 * log_a)) * (ig * xc)
        for step in (1, 2, 4):
            if d == 0:
                valid = r8 >= step
                a_sh, u_sh = pltpu.roll(a, step, 0), pltpu.roll(u, step, 0)
            else:
                valid = r8 + step < SUBLANES
                a_sh, u_sh = pltpu.roll(a, t - step, 0), pltpu.roll(u, t - step, 0)
            u = jnp.where(valid, a * u_sh + u, u)
            a = jnp.where(valid, a * a_sh, a)
        a_s[...] = a
        u_s[...] = u
        h0 = h0_ref[0, d:d + 1, :]

        def body(k, carry, d=d):
            blk = k if d == 0 else nblk - 1 - k
            rows = pl.ds(pl.multiple_of(blk * SUBLANES, SUBLANES), SUBLANES)
            h = a_s[rows, :] * carry + u_s[rows, :]
            h_s[rows, :] = h
            return h[SUBLANES - 1:SUBLANES, :] if d == 0 else h[0:1, :]

        last = lax.fori_loop(0, nblk, body, h0, unroll=8)
        st_ref[0, d:d + 1, :] = last
        hsum = h_s[...] if d == 0 else hsum + h_s[...]
    y_ref[...] = (_gelu_tanh(gate_ref[...].astype(F32)) * hsum).astype(y_ref.dtype)


def pack_rglru_weights(lp):
    def blockdiag(w):
        eye = jnp.eye(NB_B, dtype=w.dtype)
        return jnp.einsum('ncd,nm->ncmd', w, eye).reshape(W_B, W_B)
    wg = jnp.concatenate([blockdiag(lp['w_rg_a'][0]), blockdiag(lp['w_rg_x'][0]),
                          blockdiag(lp['w_rg_a'][1]), blockdiag(lp['w_rg_x'][1])], axis=1).astype(BF16)
    bg = jnp.concatenate([lp['b_rg_a'][0], lp['b_rg_x'][0], lp['b_rg_a'][1], lp['b_rg_x'][1]]).reshape(1, 4 * W_B)
    return dict(wg=wg, bg=bg, wc=lp['w_conv_rg'], bc=lp['b_conv_rg'].reshape(1, W_B), lam=lp['rg_lambda'])


def rglru(p, rw, h0, prev, *, row0, n_seq, t_seq):
    rb0 = row0 // t_seq
    xb, gb = _seg_block('rg_x'), _seg_block('rg_gate')
    full = lambda shape: pl.BlockSpec(shape, lambda s: tuple(0 for _ in shape))
    in_specs = [pl.BlockSpec((t_seq, W_B), lambda s: (rb0 + s, xb)),
                pl.BlockSpec((t_seq, W_B), lambda s: (rb0 + s, gb)),
                full((CONV_W, W_B)), full((1, W_B)), full((W_B, 4 * W_B)), full((1, 4 * W_B)), full((2, W_B)),
                pl.BlockSpec((1, 2, W_B), lambda s: (s, 0, 0))]
    args = [p, p, rw['wc'], rw['bc'], rw['wg'], rw['bg'], rw['lam'], h0]
    body, aliases = _continue_output(functools.partial(_rglru_kernel, t_seq=t_seq), in_specs, args, prev)
    return pl.pallas_call(
        body,
        grid=(n_seq,),
        in_specs=in_specs,
        out_specs=(pl.BlockSpec((t_seq, W_B), lambda s: (rb0 + s, 0)), pl.BlockSpec((1, 2, W_B), lambda s: (s, 0, 0))),
        out_shape=(jax.ShapeDtypeStruct((R_ALL, W_B), BF16), jax.ShapeDtypeStruct((n_seq, 2, W_B), F32)),
        scratch_shapes=[pltpu.VMEM((t_seq, W_B), F32)] * 3,
        input_output_aliases=aliases,
        compiler_params=_cparams(("arbitrary",)),
        name="rglru",
    )(*args)


def _dot_split(a, b_bf16):
    hi = a.astype(BF16)
    lo = (a - hi.astype(F32)).astype(BF16)
    return (jnp.dot(hi, b_bf16, preferred_element_type=F32) + jnp.dot(lo, b_bf16, preferred_element_type=F32))


def _log_sigmoid(z):
    return jnp.minimum(z, 0.0) - jnp.log(1.0 + jnp.exp(-jnp.abs(z)))


_TN = (((0,), (0,)), ((), ()))


def _mlstm_kernel(q_ref, k_ref, v_ref, o_ref, sm_ref, bias_ref, g_ref, c0_ref, m0_ref,
                  y_ref, c_out, m_out, hm_s, c_s, *, t_seq):
    L = ML_CHUNK
    nchunk = t_seq // L
    scale = DH_C ** -0.5
    ri = lax.broadcasted_iota(jnp.int32, (L, L), 0)
    ci = lax.broadcasted_iota(jnp.int32, (L, L), 1)
    lane1 = lax.broadcasted_iota(jnp.int32, (L, LANES), 1)
    ones_col = jnp.where(lane1 == 0, 1.0, 0.0).astype(BF16)
    bias = bias_ref[...]

    for d in range(2):
        causal = (ci <= ri) if d == 0 else (ci >= ri)
        tri = jnp.where(causal, 1.0, 0.0).astype(BF16)
        tri_t = jnp.where((ri <= ci) if d == 0 else (ri >= ci), 1.0, 0.0).astype(BF16)
        c_s[...] = c0_ref[0, d]
        m_init = tuple(m0_ref[0, :, d * H_C + h:d * H_C + h + 1] for h in range(H_C))

        def chunk(kk, ms, d=d, causal=causal, tri=tri, tri_t=tri_t):
            cidx = kk if d == 0 else nchunk - 1 - kk
            rows = pl.ds(pl.multiple_of(cidx * L, L), L)
            gsm = sm_ref[rows, :].astype(F32) + bias
            lf_all = _log_sigmoid(gsm)
            cum_cols = _dot_split_left(tri, lf_all)
            g_t = gsm.T
            cum_rows = _dot_split(lf_all.T, tri_t)
            new_ms = []
            for h in range(H_C):
                jl, jf = SM_MI + d * H_C + h, SM_MF + d * H_C + h
                cum_c = cum_cols[:, jf:jf + 1]
                cum_r = cum_rows[jf:jf + 1, :]
                li_c = gsm[:, jl:jl + 1]
                li_r = g_t[jl:jl + 1, :]
                m_mem = ms[h]
                log_d = jnp.where(causal, cum_c - cum_r + li_r, -jnp.inf)
                inter = cum_c + m_mem
                m_row = jnp.maximum(inter, jnp.max(log_d, axis=-1, keepdims=True))
                dmat = jnp.exp(log_d - m_row)
                w_inter = jnp.exp(inter - m_row)
                sl = slice(h * DH_C, (h + 1) * DH_C)
                qh, kh, vh = q_ref[rows, sl], k_ref[rows, sl], v_ref[rows, sl]
                s = lax.dot_general(qh, kh, _NT, preferred_element_type=F32) * (scale * dmat)
                sv = jnp.dot(s.astype(BF16), vh, preferred_element_type=F32)
                qc = jnp.dot(qh, c_s[h].astype(BF16), preferred_element_type=F32) * scale
                num = sv + qc[:, :DH_C] * w_inter
                den = jnp.sum(s, axis=-1, keepdims=True) + w_inter * qc[:, DH_C:DH_C + 1]
                den = jnp.maximum(jnp.abs(den), jnp.exp(-m_row))
                h_out = num * (1.0 / den)
                if d == 0:
                    hm_s[rows, sl] = h_out
                else:
                    hm_s[rows, sl] = hm_s[rows, sl] + h_out
                last = cum_c[L - 1:L, :] if d == 0 else cum_c[0:1, :]
                w_s = last - cum_c + li_c
                m_new = jnp.maximum(last + m_mem, jnp.max(w_s, axis=0, keepdims=True))
                decay = jnp.exp(last + m_mem - m_new)
                ws = jnp.exp(w_s - m_new)
                kw_t = (kh.astype(F32) * ws).T.astype(BF16)
                v_aug = jnp.concatenate([vh, ones_col], axis=1)
                c_s[h] = decay * c_s[h] + jnp.dot(kw_t, v_aug, preferred_element_type=F32)
                new_ms.append(m_new)
            return tuple(new_ms)

        m_fin = lax.fori_loop(0, nchunk, chunk, m_init)
        c_out[0, d] = c_s[...]
        for h in range(H_C):
            m_out[0, :, d * H_C + h:d * H_C + h + 1] = m_fin[h]

    for h in range(H_C):
        sl = slice(h * DH_C, (h + 1) * DH_C)
        hm = hm_s[:, sl]
        r = lax.rsqrt(jnp.mean(hm * hm, axis=-1, keepdims=True) + EPS)
        y_ref[:, sl] = (jax.nn.sigmoid(o_ref[:, sl].astype(F32)) * (hm * r * g_ref[...])).astype(y_ref.dtype)


def _dot_split_left(a_bf16, b):
    hi = b.astype(BF16)
    lo = (b - hi.astype(F32)).astype(BF16)
    return (jnp.dot(a_bf16, hi, preferred_element_type=F32) + jnp.dot(a_bf16, lo, preferred_element_type=F32))


def mlstm(p, bias_sm, g_out, c0_aug, m0, prev, *, row0, n_seq, t_seq):
    rb0 = row0 // t_seq
    seg = lambda nm: pl.BlockSpec((t_seq, W_BRANCH), lambda s, b=_seg_block(nm): (rb0 + s, b))
    full = lambda shape: pl.BlockSpec(shape, lambda s: tuple(0 for _ in shape))
    st_spec = pl.BlockSpec((1, 2, H_C, DH_C, 2 * DH_C), lambda s: (s, 0, 0, 0, 0))
    m_spec = pl.BlockSpec((1, 1, 2 * H_C), lambda s: (s, 0, 0))
    in_specs = [seg('mq'), seg('mk'), seg('mv'), seg('mo'),
                pl.BlockSpec((t_seq, LANES), lambda s: (rb0 + s, P_SMALL // LANES)),
                full((1, LANES)), full((1, DH_C)), st_spec, m_spec]
    args = [p, p, p, p, p, bias_sm, g_out.reshape(1, DH_C), c0_aug, m0]
    body, aliases = _continue_output(functools.partial(_mlstm_kernel, t_seq=t_seq), in_specs, args, prev)
    return pl.pallas_call(
        body,
        grid=(n_seq,),
        in_specs=in_specs,
        out_specs=(pl.BlockSpec((t_seq, W_BRANCH), lambda s: (rb0 + s, 0)), st_spec, m_spec),
        out_shape=(jax.ShapeDtypeStruct((R_ALL, W_BRANCH), BF16),
                   jax.ShapeDtypeStruct((n_seq, 2, H_C, DH_C, 2 * DH_C), F32),
                   jax.ShapeDtypeStruct((n_seq, 1, 2 * H_C), F32)),
        scratch_shapes=[pltpu.VMEM((t_seq, W_BRANCH), F32), pltpu.VMEM((H_C, DH_C, 2 * DH_C), F32)],
        input_output_aliases=aliases,
        compiler_params=_cparams(("arbitrary",)),
        name="mlstm",
    )(*args)


MERGE_TM = 512


def _merge_kernel(ya_ref, yb_ref, yc_ref, yd_ref, g0_ref, g1_ref, g2_ref, g3_ref, x_ref, mod_ref,
                  wbr_ref, wout_ref, gn_ref, xo_ref, xn_ref):
    merged = None
    for g, (y_ref, gate_ref) in enumerate(((ya_ref, g0_ref), (yb_ref, g1_ref), (yc_ref, g2_ref), (yd_ref, g3_ref))):
        pg = jnp.dot(y_ref[...], wbr_ref[g], preferred_element_type=F32)
        term = jax.nn.sigmoid(gate_ref[...].astype(F32)) * pg
        merged = term if merged is None else merged + term
    y = jnp.dot(merged.astype(BF16), wout_ref[...], preferred_element_type=F32)
    x = x_ref[...] + mod_ref[0, 2:3, :] * y
    xo_ref[...] = x
    r = lax.rsqrt(jnp.mean(x * x, axis=-1, keepdims=True) + EPS)
    xn_ref[...] = ((x * r * gn_ref[...]) * (1.0 + mod_ref[0, 4:5, :]) + mod_ref[0, 3:4, :]).astype(xn_ref.dtype)


def merge(ys, p, x, mod, wbr, wout, g_ffn):
    tm = MERGE_TM
    br = pl.BlockSpec((tm, W_BRANCH), lambda i: (i, 0))
    gate = lambda g: pl.BlockSpec((tm, D_MODEL), lambda i, g=g: (i, g))
    row = pl.BlockSpec((tm, D_MODEL), lambda i: (i, 0))
    return pl.pallas_call(
        _merge_kernel,
        grid=(R_ALL // tm,),
        in_specs=[br, br, br, br, gate(0), gate(1), gate(2), gate(3), row,
                  pl.BlockSpec((1, 8, D_MODEL), lambda i: (_mod_row_of_tile(i, tm), 0, 0)),
                  pl.BlockSpec((N_BRANCH, W_BRANCH, D_MODEL), lambda i: (0, 0, 0)),
                  pl.BlockSpec((D_MODEL, D_MODEL), lambda i: (0, 0)),
                  pl.BlockSpec((1, D_MODEL), lambda i: (0, 0))],
        out_specs=(row, row),
        out_shape=(jax.ShapeDtypeStruct((R_ALL, D_MODEL), F32), jax.ShapeDtypeStruct((R_ALL, D_MODEL), BF16)),
        compiler_params=_cparams(("arbitrary",)),
        name="merge",
    )(*ys, p, p, p, p, x, mod, wbr, wout, g_ffn.reshape(1, D_MODEL))


def _new_expert(te_ref, i):
    return jnp.logical_or(i == 0, te_ref[i] != te_ref[jnp.maximum(i - 1, 0)])


def _ffn_up_kernel(te_ref, nt_ref, x_ref, wg_ref, wu_ref, h_ref, wgb_ref, wub_ref):
    i = pl.program_id(1)

    @pl.when(_new_expert(te_ref, i))
    def _():
        wgb_ref[...] = wg_ref[0].astype(BF16)
        wub_ref[...] = wu_ref[0].astype(BF16)

    @pl.when(i < nt_ref[0])
    def _():
        x = x_ref[...]
        g = jnp.dot(x, wgb_ref[...], preferred_element_type=F32)
        u = jnp.dot(x, wub_ref[...], preferred_element_type=F32)
        h_ref[...] = (g * jax.nn.sigmoid(g) * u).astype(h_ref.dtype)


def ffn_up(tile_expert, n_tiles, xs, wg, wu, tm, tf):
    r, d = xs.shape
    f = wg.shape[2]
    return pl.pallas_call(
        _ffn_up_kernel,
        grid_spec=pltpu.PrefetchScalarGridSpec(
            num_scalar_prefetch=2,
            grid=(f // tf, r // tm),
            in_specs=[pl.BlockSpec((tm, d), lambda j, i, te, nt: (i, 0)),
                      pl.BlockSpec((1, d, tf), lambda j, i, te, nt: (te[i], 0, j)),
                      pl.BlockSpec((1, d, tf), lambda j, i, te, nt: (te[i], 0, j))],
            out_specs=pl.BlockSpec((tm, tf), lambda j, i, te, nt: (i, j)),
            scratch_shapes=[pltpu.VMEM((d, tf), BF16), pltpu.VMEM((d, tf), BF16)]),
        out_shape=jax.ShapeDtypeStruct((r, f), BF16),
        compiler_params=_cparams(("arbitrary", "arbitrary")),
        name="ffn_up",
    )(tile_expert, n_tiles, xs, wg, wu)


def _ffn_down_kernel(te_ref, nt_ref, h_ref, wd_ref, y_ref, wdb_ref):
    i = pl.program_id(0)

    @pl.when(_new_expert(te_ref, i))
    def _():
        wdb_ref[...] = wd_ref[0].astype(BF16)

    @pl.when(i < nt_ref[0])
    def _():
        y_ref[...] = jnp.dot(h_ref[...], wdb_ref[...], preferred_element_type=F32).astype(y_ref.dtype)


def ffn_down(tile_expert, n_tiles, h, wd, tm, out_dtype):
    r, f = h.shape
    d = wd.shape[2]
    return pl.pallas_call(
        _ffn_down_kernel,
        grid_spec=pltpu.PrefetchScalarGridSpec(
            num_scalar_prefetch=2,
            grid=(r // tm,),
            in_specs=[pl.BlockSpec((tm, f), lambda i, te, nt: (i, 0)),
                      pl.BlockSpec((1, f, d), lambda i, te, nt: (te[i], 0, 0))],
            out_specs=pl.BlockSpec((tm, d), lambda i, te, nt: (i, 0)),
            scratch_shapes=[pltpu.VMEM((f, d), BF16)]),
        out_shape=jax.ShapeDtypeStruct((r, d), out_dtype),
        compiler_params=_cparams(("arbitrary",)),
        name="ffn_down",
    )(tile_expert, n_tiles, h, wd)


def _ffn_down_res_kernel(h_ref, wd_ref, x_ref, mod_ref, y_ref, wdb_ref):
    @pl.when(pl.program_id(0) == 0)
    def _():
        wdb_ref[...] = wd_ref[...].astype(BF16)

    y = jnp.dot(h_ref[...], wdb_ref[...], preferred_element_type=F32)
    y_ref[...] = x_ref[...] + mod_ref[0, 5:6, :] * y


def ffn_down_residual(h, wd, x, mod):
    tm = 512
    r, f = h.shape
    d = wd.shape[1]
    return pl.pallas_call(
        _ffn_down_res_kernel,
        grid=(r // tm,),
        in_specs=[pl.BlockSpec((tm, f), lambda i: (i, 0)),
                  pl.BlockSpec((f, d), lambda i: (0, 0)),
                  pl.BlockSpec((tm, d), lambda i: (i, 0)),
                  pl.BlockSpec((1, 8, d), lambda i: (_mod_row_of_tile(i, tm), 0, 0))],
        out_specs=pl.BlockSpec((tm, d), lambda i: (i, 0)),
        out_shape=jax.ShapeDtypeStruct((r, d), F32),
        scratch_shapes=[pltpu.VMEM((f, d), BF16)],
        compiler_params=_cparams(("arbitrary",)),
        name="ffn_down_residual",
    )(h, wd, x, mod)


def dense_swiglu_residual(xn, x, mod, wg, wu, wd):
    t = xn.shape[0]
    tm = 512
    n_tiles = t // tm
    te = jnp.zeros((n_tiles,), jnp.int32)
    nt = jnp.full((1,), n_tiles, jnp.int32)
    h = ffn_up(te, nt, xn, wg[None], wu[None], tm=tm, tf=wg.shape[1] // 2)
    return ffn_down_residual(h, wd, x, mod)


MOE_TM = 512


def _moe_routing(logits, tm):
    t = logits.shape[0]
    n_assign = t * TOP_K
    top_v, top_i = lax.top_k(logits, TOP_K)
    gate = jax.nn.softmax(top_v, axis=-1)
    flat_e = top_i.reshape(-1).astype(jnp.int32)
    onehot = (flat_e[:, None] == jnp.arange(N_EXP, dtype=jnp.int32)[None, :])
    blk = LANES
    oh = onehot.astype(F32).reshape(n_assign // blk, blk, N_EXP)
    tril = jnp.tril(jnp.ones((blk, blk), F32))
    within = jnp.einsum('ij,bjk->bik', tril, oh)
    blk_tot = within[:, -1, :]
    blk_off = jnp.cumsum(blk_tot, axis=0) - blk_tot
    csum = (within + blk_off[:, None, :]).reshape(n_assign, N_EXP)
    rank = jnp.sum(jnp.where(onehot, csum - 1.0, 0.0), axis=1).astype(jnp.int32)
    counts = csum[-1].astype(jnp.int32)
    padded = (counts + tm - 1) // tm * tm
    grp_start = jnp.cumsum(padded) - padded
    raw_start = jnp.cumsum(counts) - counts
    slot_of_assign = jnp.sum(jnp.where(onehot, grp_start[None, :], 0), axis=1) + rank

    r_max = n_assign + N_EXP * tm
    tile_start = jnp.arange(r_max // tm, dtype=jnp.int32) * tm
    tile_expert = jnp.sum((tile_start[:, None] >= (grp_start + padded)[None, :]).astype(jnp.int32), axis=1)
    tile_expert = jnp.minimum(tile_expert, N_EXP - 1).astype(jnp.int32)
    n_tiles = (jnp.sum(padded) // tm).astype(jnp.int32).reshape(1)

    order = jnp.argsort(flat_e, stable=True).astype(jnp.int32)
    e_slot = jnp.repeat(tile_expert, tm)
    j = jnp.arange(r_max, dtype=jnp.int32) - grp_start[e_slot]
    src = jnp.clip(raw_start[e_slot] + j, 0, n_assign - 1)
    tok_of_slot = jnp.where(j < counts[e_slot], order[src] // TOP_K, 0)
    return gate, slot_of_assign, tok_of_slot, tile_expert, n_tiles


def moe_swiglu(xn, w_router, wg, wu, wd):
    t, d = xn.shape
    tm = MOE_TM
    logits = jnp.dot(xn.astype(F32), w_router, precision=lax.Precision.HIGHEST)
    gate, slot_of_assign, tok_of_slot, tile_expert, n_tiles = _moe_routing(logits, tm)
    xn_w = lax.bitcast_convert_type(xn.reshape(t, d // 2, 2), jnp.uint32)
    xs_w = jnp.take(xn_w, tok_of_slot, axis=0)
    xs = lax.bitcast_convert_type(xs_w, BF16).reshape(xs_w.shape[0], d)
    h = ffn_up(tile_expert, n_tiles, xs, wg, wu, tm=tm, tf=wg.shape[2] // 4)
    ys = ffn_down(tile_expert, n_tiles, h, wd, tm=tm, out_dtype=F32)
    pair = jnp.take(ys, slot_of_assign, axis=0).reshape(t, TOP_K, d)
    return gate[:, 0:1] * pair[:, 0] + gate[:, 1:2] * pair[:, 1]


def _layer(x, cond, lp, l, ctx, tabs_a, tabs_d):
    mod = modulation(cond, lp['w_mod'], lp['b_mod']).reshape(cond.shape[0], 6, D_MODEL)
    mod = jnp.pad(mod, ((0, 0), (0, 2), (0, 0)))
    p = in_proj(x, mod, lp['g_norm_mix'], pack_w_in(lp['w_in']))

    mw = pack_mla_weights(lp)
    q_a, k_a, v_a, ckv, kr = mla_prep(p, mw, tabs_a)
    kr_c = jnp.pad(ctx['mla_krope'].reshape(-1, ROPE_A), ((0, 0), (0, LANES - ROPE_A)))
    kc_a, vc_a = mla_prep_cache(ctx['mla_ckv'].reshape(-1, KV_RANK), kr_c, mw, tabs_a)
    ya = mla_attention(q_a, k_a, v_a, None, None, None, row0=0, n_seq=N_CTX_SEQ, t_seq=T_CTX, tq=T_CTX)
    ya = mla_attention(q_a, k_a, v_a, kc_a, vc_a, ya, row0=R_CTX, n_seq=N_LAT_SEQ, t_seq=T_LAT, tq=ATTN_TQ)

    rw = pack_rglru_weights(lp)
    yb, st_rg = rglru(p, rw, jnp.zeros((N_CTX_SEQ, 2, W_B), F32), None, row0=0, n_seq=N_CTX_SEQ, t_seq=T_CTX)
    yb, _ = rglru(p, rw, ctx['rglru'], yb, row0=R_CTX, n_seq=N_LAT_SEQ, t_seq=T_LAT)

    bias_sm = jnp.zeros((LANES,), F32).at[SM_MI:SM_MI + 2 * H_C].set(lp['b_ml_i'].reshape(-1))
    bias_sm = bias_sm.at[SM_MF:SM_MF + 2 * H_C].set(lp['b_ml_f'].reshape(-1)).reshape(1, LANES)
    c0_ctx = jnp.zeros((N_CTX_SEQ, 2, H_C, DH_C, 2 * DH_C), F32)
    m0_ctx = jnp.zeros((N_CTX_SEQ, 1, 2 * H_C), F32)
    c0_lat = jnp.concatenate([ctx['mlstm_C'], ctx['mlstm_n'][..., None],
                              jnp.zeros(ctx['mlstm_n'].shape + (DH_C - 1,), F32)], axis=-1)
    m0_lat = ctx['mlstm_m'].reshape(N_LAT_SEQ, 1, 2 * H_C)
    yc, c_fin, m_fin = mlstm(p, bias_sm, lp['g_ml_out'], c0_ctx, m0_ctx, None, row0=0, n_seq=N_CTX_SEQ, t_seq=T_CTX)
    yc, _, _ = mlstm(p, bias_sm, lp['g_ml_out'], c0_lat, m0_lat, yc, row0=R_CTX, n_seq=N_LAT_SEQ, t_seq=T_LAT)

    lambda_init = 0.8 - 0.6 * math.exp(-0.3 * l)
    qd, kd_own, kd_plain = diff_prep(p, lp, tabs_d)
    dkc = ctx['diff_k'].reshape(-1, W_BRANCH)
    dvc = ctx['diff_v'].reshape(-1, W_BRANCH)
    yd = diff_attention(lp['diff_lambda'], qd, kd_own, p, None, None, lp['g_diff_sub'], None,
                        row0=0, n_seq=N_CTX_SEQ, t_seq=T_CTX, tq=T_CTX, lambda_init=lambda_init)
    yd = diff_attention(lp['diff_lambda'], qd, kd_own, p, dkc, dvc, lp['g_diff_sub'], yd,
                        row0=R_CTX, n_seq=N_LAT_SEQ, t_seq=T_LAT, tq=ATTN_TQ, lambda_init=lambda_init)

    x, xn = merge((ya, yb, yc, yd), p, x, mod, lp['w_br'].astype(BF16), lp['w_out'].astype(BF16), lp['g_norm_ffn'])

    if 'ffn' in lp:
        x = dense_swiglu_residual(xn, x, mod, *lp['ffn'])
    else:
        f = moe_swiglu(xn, *lp['moe'])
        g2 = jnp.concatenate([jnp.broadcast_to(mod[0:1, 5], (R_CTX, D_MODEL)),
                              jnp.repeat(mod[1:1 + N_LAT_SEQ, 5], T_LAT, axis=0)], axis=0)
        x = x + g2 * f

    dv0 = _seg_block('dv') * W_BRANCH
    ctx_out = (ckv[:R_CTX].reshape(N_CTX_SEQ, T_CTX, KV_RANK),
               kr[:R_CTX, :ROPE_A].reshape(N_CTX_SEQ, T_CTX, ROPE_A),
               kd_plain[:R_CTX].reshape(N_CTX_SEQ, T_CTX, 2, H_D, DH_D),
               p[:R_CTX, dv0:dv0 + W_BRANCH].astype(F32).reshape(N_CTX_SEQ, T_CTX, H_D, 2 * DH_D),
               st_rg,
               c_fin[..., :DH_C],
               c_fin[..., DH_C],
               m_fin.reshape(N_CTX_SEQ, 2, H_C))
    return x, ctx_out


def kernel(x_prompt, x_sample, cache_mla_ckv, cache_mla_krope, cache_diff_k, cache_diff_v,
           state_rglru, state_mlstm_C, state_mlstm_n, state_mlstm_m, c, c_ctx,
           w_mod, b_mod, g_norm_mix, g_norm_ffn, w_in, g_mla_qlat, w_mla_uq, g_mla_kvlat, w_mla_ukv,
           g_mla_qn, g_mla_kn, w_conv_rg, b_conv_rg, w_rg_a, b_rg_a, w_rg_x, b_rg_x, rg_lambda,
           b_ml_i, b_ml_f, g_ml_out, g_diff_qn, g_diff_kn, diff_lambda, g_diff_sub, w_br, w_out,
           w_ffn_gate, w_ffn_up, w_ffn_down, w_router, w_moe_gate, w_moe_up, w_moe_down):
    assert x_prompt.shape == (N_CTX_SEQ, T_CTX, D_MODEL) and x_sample.shape == (N_LAT_SEQ, T_LAT, D_MODEL)
    tabs_a = _rope_tables(ROPE_A, (NOPE_A,), MLA_TM)
    tabs_d = _rope_tables(DH_D, (0, DH_D), DIFF_TM)
    cond = jnp.concatenate([c_ctx.reshape(1, D_MODEL), c, jnp.zeros((16 - 1 - N_LAT_SEQ, D_MODEL), F32)], axis=0)
    x = jnp.concatenate([x_prompt.reshape(R_CTX, D_MODEL), x_sample.reshape(R_LAT, D_MODEL)], axis=0)
    new = []
    for l in range(DEPTH):
        lp = dict(w_mod=w_mod[l], b_mod=b_mod[l], g_norm_mix=g_norm_mix[l], g_norm_ffn=g_norm_ffn[l], w_in=w_in[l],
                  g_mla_qlat=g_mla_qlat[l], w_mla_uq=w_mla_uq[l], g_mla_kvlat=g_mla_kvlat[l], w_mla_ukv=w_mla_ukv[l],
                  g_mla_qn=g_mla_qn[l], g_mla_kn=g_mla_kn[l], w_conv_rg=w_conv_rg[l], b_conv_rg=b_conv_rg[l],
                  w_rg_a=w_rg_a[l], b_rg_a=b_rg_a[l], w_rg_x=w_rg_x[l], b_rg_x=b_rg_x[l], rg_lambda=rg_lambda[l],
                  b_ml_i=b_ml_i[l], b_ml_f=b_ml_f[l], g_ml_out=g_ml_out[l], g_diff_qn=g_diff_qn[l],
                  g_diff_kn=g_diff_kn[l], diff_lambda=diff_lambda[l], g_diff_sub=g_diff_sub[l],
                  w_br=w_br[l], w_out=w_out[l])
        if l % 2 == 0:
            lp['ffn'] = (w_ffn_gate[l // 2], w_ffn_up[l // 2], w_ffn_down[l // 2])
        else:
            lp['moe'] = (w_router[l // 2], w_moe_gate[l // 2], w_moe_up[l // 2], w_moe_down[l // 2])
        ctx_l = dict(mla_ckv=cache_mla_ckv[:, l], mla_krope=cache_mla_krope[:, l], diff_k=cache_diff_k[:, l],
                     diff_v=cache_diff_v[:, l], rglru=state_rglru[:, l], mlstm_C=state_mlstm_C[:, l],
                     mlstm_n=state_mlstm_n[:, l], mlstm_m=state_mlstm_m[:, l])
        x, st = _layer(x, cond, lp, l, ctx_l, tabs_a, tabs_d)
        new.append(st)
    outs = tuple(jnp.stack([s[i] for s in new], axis=1) for i in range(8))
    return (x[:R_CTX].reshape(N_CTX_SEQ, T_CTX, D_MODEL), x[R_CTX:].reshape(N_LAT_SEQ, T_LAT, D_MODEL)) + outs
```

```python
import functools
import math

import jax
import jax.numpy as jnp
import numpy as np
from jax import lax
from jax.experimental import pallas as pl
from jax.experimental.pallas import tpu as pltpu

D_MODEL = 1024
DEPTH = 2
GRID_W = 64
ROPE_BASE = 10000.0
EPS = 1e-6
N_BRANCH = 4
W_BRANCH = D_MODEL // 2

H_A = 8
NOPE_A = 64
ROPE_A = 32
V_A = W_BRANCH // H_A
Q_RANK = D_MODEL // 4
KV_RANK = D_MODEL // 8
MLA_SCALE = (NOPE_A + ROPE_A) ** -0.5

W_B = W_BRANCH
NB_B = 8
BW_B = W_B // NB_B
CONV_W = 4
RG_C = 8.0

H_C = 4
DH_C = W_BRANCH // H_C
ML_CHUNK = 128

H_D = 4
DH_D = W_BRANCH // (2 * H_D)
DIFF_SCALE = DH_D ** -0.5

N_EXP = 8
TOP_K = 2

V7X_VMEM_LIMIT_BYTES = 56 * 1024 * 1024
LANES = 128
SUBLANES = 8

BF16 = jnp.bfloat16
F32 = jnp.float32

P_GATE = 0
P_SEG = N_BRANCH * D_MODEL
SEG_NAMES = ('rg_x', 'rg_gate', 'mq', 'mk', 'mv', 'mo', 'dq', 'dk', 'dv')
P_QLAT = P_SEG + 9 * W_BRANCH
P_KVLAT = P_QLAT + Q_RANK
P_SMALL = P_KVLAT + KV_RANK
P_WIDTH = P_SMALL + LANES
SM_MI = ROPE_A
SM_MF = ROPE_A + 2 * H_C


def _seg_block(name):
    return (P_SEG + SEG_NAMES.index(name) * W_BRANCH) // W_BRANCH


N_CTX_SEQ, T_CTX = 16, 256
N_LAT_SEQ, T_LAT = 8, 1024
PAST_LEN = 512
R_CTX = N_CTX_SEQ * T_CTX
R_LAT = N_LAT_SEQ * T_LAT
R_ALL = R_CTX + R_LAT


def _cparams(sem):
    return pltpu.CompilerParams(dimension_semantics=sem, vmem_limit_bytes=V7X_VMEM_LIMIT_BYTES)


def _mod_row_of_tile(i, tm):
    n_ctx_tiles = R_CTX // tm
    per_seq = T_LAT // tm
    return jnp.where(i < n_ctx_tiles, 0, 1 + (i - n_ctx_tiles) // per_seq)


def _pos_block_of_tile(i, tm):
    n_ctx_tiles = R_CTX // tm
    per_seq = T_LAT // tm
    return jnp.where(i < n_ctx_tiles, per_seq, (i - n_ctx_tiles) % per_seq)


def _mod_kernel(c_ref, w_ref, b_ref, o_ref):
    c = c_ref[...]
    s = (c * jax.nn.sigmoid(c)).astype(BF16)
    o_ref[...] = jnp.dot(s, w_ref[...].astype(BF16), preferred_element_type=F32) + b_ref[...]


def modulation(cond, w_mod, b_mod):
    m, d = cond.shape
    n = w_mod.shape[1]
    tn = 1536
    return pl.pallas_call(
        _mod_kernel,
        grid=(n // tn,),
        in_specs=[pl.BlockSpec((m, d), lambda j: (0, 0)),
                  pl.BlockSpec((d, tn), lambda j: (0, j)),
                  pl.BlockSpec((1, tn), lambda j: (0, j))],
        out_specs=pl.BlockSpec((m, tn), lambda j: (0, j)),
        out_shape=jax.ShapeDtypeStruct((m, n), F32),
        compiler_params=_cparams(("arbitrary",)),
        name="modulation",
    )(cond, w_mod, b_mod.reshape(1, n))


def _in_proj_kernel(x_ref, mod_ref, g_ref, w_ref, o_ref, xn_ref):
    @pl.when(pl.program_id(1) == 0)
    def _():
        x = x_ref[...]
        r = lax.rsqrt(jnp.mean(x * x, axis=-1, keepdims=True) + EPS)
        sh = mod_ref[0, 0:1, :]
        sc = mod_ref[0, 1:2, :]
        xn_ref[...] = ((x * r * g_ref[...]) * (1.0 + sc) + sh).astype(BF16)

    o_ref[...] = jnp.dot(xn_ref[...], w_ref[...], preferred_element_type=F32).astype(o_ref.dtype)


def in_proj(x, mod, g, w_p):
    tm, tn = 1024, 1024
    m, d = x.shape
    n = w_p.shape[1]
    return pl.pallas_call(
        _in_proj_kernel,
        grid=(m // tm, n // tn),
        in_specs=[pl.BlockSpec((tm, d), lambda i, j: (i, 0)),
                  pl.BlockSpec((1, 8, d), lambda i, j: (_mod_row_of_tile(i, tm), 0, 0)),
                  pl.BlockSpec((1, d), lambda i, j: (0, 0)),
                  pl.BlockSpec((d, tn), lambda i, j: (0, j))],
        out_specs=pl.BlockSpec((tm, tn), lambda i, j: (i, j)),
        out_shape=jax.ShapeDtypeStruct((m, n), BF16),
        scratch_shapes=[pltpu.VMEM((tm, d), BF16)],
        compiler_params=_cparams(("arbitrary", "arbitrary")),
        name="in_proj",
    )(x, mod, g.reshape(1, d), w_p)


def pack_w_in(w_in):
    o_rg = Q_RANK + KV_RANK + ROPE_A
    o_mi = o_rg + 6 * W_BRANCH
    o_dq = o_mi + 4 * H_C
    o_gate = o_dq + 3 * W_BRANCH
    pad = jnp.zeros((w_in.shape[0], LANES - ROPE_A - 4 * H_C), w_in.dtype)
    parts = [w_in[:, o_gate:], w_in[:, o_rg:o_mi], w_in[:, o_dq:o_gate], w_in[:, :Q_RANK + KV_RANK],
             w_in[:, Q_RANK + KV_RANK:o_rg], w_in[:, o_mi:o_dq], pad]
    return jnp.concatenate(parts, axis=1).astype(BF16)


def _rope_tables(rot_dim, lane_starts, tm):
    rows = T_LAT // GRID_W
    r, c = np.meshgrid(np.arange(rows, dtype=np.float32), np.arange(GRID_W, dtype=np.float32), indexing='ij')
    nf = rot_dim // 4
    inv = (np.float32(ROPE_BASE) ** (-np.arange(nf, dtype=np.float32) / np.float32(nf))).astype(np.float32)
    ang = np.stack([r.reshape(-1)[:, None] * inv, c.reshape(-1)[:, None] * inv], axis=1).astype(np.float32)
    cos, sin = np.cos(ang).astype(np.float32), np.sin(ang).astype(np.float32)
    tc = np.ones((T_LAT + tm, LANES), np.float32)
    ta = np.zeros((T_LAT + tm, LANES), np.float32)
    tb = np.zeros((T_LAT + tm, LANES), np.float32)
    for s0 in lane_starts:
        for a in range(2):
            lo = s0 + a * 2 * nf
            tc[:T_LAT, lo:lo + nf] = cos[:, a]
            tc[:T_LAT, lo + nf:lo + 2 * nf] = cos[:, a]
            ta[:T_LAT, lo:lo + nf] = -sin[:, a]
            tb[:T_LAT, lo + nf:lo + 2 * nf] = sin[:, a]
    return jnp.asarray(tc), jnp.asarray(ta), jnp.asarray(tb)


def _rope(x, c, sa, sb, half):
    return x * c + pltpu.roll(x, LANES - half, 1) * sa + pltpu.roll(x, half, 1) * sb


MLA_TM = 512
ATTN_TQ = 512
QK_A = NOPE_A + ROPE_A


def _mla_prep_kernel(*refs, has_q, norm_ckv):
    if has_q:
        (qlat_ref, gq_ref, wuq_ref, gqn_ref, kv_ref, sm_ref, gkv_ref, wkc_ref, wv_ref, gkn_ref,
         c_ref, sa_ref, sb_ref, q_o, k_o, v_o, ckv_o, kr_o) = refs
    else:
        (kv_ref, sm_ref, gkv_ref, wkc_ref, wv_ref, gkn_ref, c_ref, sa_ref, sb_ref, k_o, v_o) = refs
    c, sa, sb = c_ref[...], sa_ref[...], sb_ref[...]

    def heads(z, g, o_ref, scale):
        for h in range(H_A):
            s = z[:, h * LANES:(h + 1) * LANES]
            r = lax.rsqrt(jnp.sum(s * s, axis=-1, keepdims=True) * (1.0 / QK_A) + EPS)
            y = _rope(s * r * g, c, sa, sb, ROPE_A // 4)
            if scale != 1.0:
                y = y * scale
            o_ref[:, h * LANES:(h + 1) * LANES] = y.astype(o_ref.dtype)

    if has_q:
        ql = qlat_ref[...].astype(F32)
        qn = ql * lax.rsqrt(jnp.mean(ql * ql, axis=-1, keepdims=True) + EPS) * gq_ref[...]
        q = jnp.dot(qn.astype(BF16), wuq_ref[...], preferred_element_type=F32)
        heads(q, gqn_ref[...], q_o, MLA_SCALE)

    kv = kv_ref[...].astype(F32)
    if norm_ckv:
        ckv = kv * lax.rsqrt(jnp.mean(kv * kv, axis=-1, keepdims=True) + EPS) * gkv_ref[...]
    else:
        ckv = kv
    sm = sm_ref[...]
    ckv_b = ckv.astype(BF16)
    kin = jnp.concatenate([ckv_b, sm.astype(BF16)], axis=1)
    k = jnp.dot(kin, wkc_ref[...], preferred_element_type=F32)
    heads(k, gkn_ref[...], k_o, 1.0)
    v_o[...] = jnp.dot(ckv_b, wv_ref[...], preferred_element_type=F32).astype(v_o.dtype)
    if has_q:
        ckv_o[...] = ckv
        kr_o[...] = sm.astype(F32)


def pack_mla_weights(lp):
    wuq = lp['w_mla_uq'].reshape(Q_RANK, H_A, QK_A)
    wuq_p = jnp.pad(wuq, ((0, 0), (0, 0), (0, LANES - QK_A))).reshape(Q_RANK, H_A * LANES).astype(BF16)
    wukv = lp['w_mla_ukv'].reshape(KV_RANK, H_A, NOPE_A + V_A)
    wk = jnp.pad(wukv[:, :, :NOPE_A], ((0, 0), (0, 0), (0, LANES - NOPE_A))).reshape(KV_RANK, H_A * LANES)
    place = np.zeros((LANES, H_A, LANES), np.float32)
    for h in range(H_A):
        place[np.arange(ROPE_A), h, NOPE_A + np.arange(ROPE_A)] = 1.0
    wkc = jnp.concatenate([wk, jnp.asarray(place.reshape(LANES, H_A * LANES))], axis=0).astype(BF16)
    wv = wukv[:, :, NOPE_A:]
    wv_even = jnp.pad(wv, ((0, 0), (0, 0), (0, LANES - V_A)))
    wv_odd = jnp.pad(wv, ((0, 0), (0, 0), (LANES - V_A, 0)))
    odd = (np.arange(H_A) % 2 == 1)[None, :, None]
    wv_p = jnp.where(odd, wv_odd, wv_even).reshape(KV_RANK, H_A * LANES).astype(BF16)
    pad_g = lambda g: jnp.pad(g, (0, LANES - QK_A)).reshape(1, LANES)
    return dict(wuq=wuq_p, wkc=wkc, wv=wv_p, gqn=pad_g(lp['g_mla_qn']), gkn=pad_g(lp['g_mla_kn']),
                gq=lp['g_mla_qlat'].reshape(1, Q_RANK), gkv=lp['g_mla_kvlat'].reshape(1, KV_RANK))


def mla_prep(p, mw, tabs):
    tm = MLA_TM
    n = R_ALL // tm
    full = lambda shape: pl.BlockSpec(shape, lambda i: (0, 0))
    tab = pl.BlockSpec((tm, LANES), lambda i: (_pos_block_of_tile(i, tm), 0))
    wide = H_A * LANES
    out_shape = (jax.ShapeDtypeStruct((R_ALL, wide), BF16), jax.ShapeDtypeStruct((R_ALL, wide), BF16),
                 jax.ShapeDtypeStruct((R_ALL, wide), BF16), jax.ShapeDtypeStruct((R_ALL, KV_RANK), F32),
                 jax.ShapeDtypeStruct((R_ALL, LANES), F32))
    row = lambda w: pl.BlockSpec((tm, w), lambda i: (i, 0))
    return pl.pallas_call(
        functools.partial(_mla_prep_kernel, has_q=True, norm_ckv=True),
        grid=(n,),
        in_specs=[pl.BlockSpec((tm, Q_RANK), lambda i: (i, P_QLAT // Q_RANK)), full((1, Q_RANK)),
                  full((Q_RANK, wide)), full((1, LANES)),
                  pl.BlockSpec((tm, KV_RANK), lambda i: (i, P_KVLAT // KV_RANK)),
                  pl.BlockSpec((tm, LANES), lambda i: (i, P_SMALL // LANES)), full((1, KV_RANK)),
                  full((2 * LANES, wide)), full((KV_RANK, wide)), full((1, LANES)), tab, tab, tab],
        out_specs=(row(wide), row(wide), row(wide), row(KV_RANK), row(LANES)),
        out_shape=out_shape,
        compiler_params=_cparams(("arbitrary",)),
        name="mla_prep",
    )(p, mw['gq'], mw['wuq'], mw['gqn'], p, p, mw['gkv'], mw['wkc'], mw['wv'], mw['gkn'], *tabs)


def mla_prep_cache(ckv_c, kr_c, mw, tabs):
    tm = MLA_TM
    r = ckv_c.shape[0]
    full = lambda shape: pl.BlockSpec(shape, lambda i: (0, 0))
    ident = pl.BlockSpec((tm, LANES), lambda i: (T_LAT // tm, 0))
    wide = H_A * LANES
    row = lambda w: pl.BlockSpec((tm, w), lambda i: (i, 0))
    return pl.pallas_call(
        functools.partial(_mla_prep_kernel, has_q=False, norm_ckv=False),
        grid=(r // tm,),
        in_specs=[row(KV_RANK), row(LANES), full((1, KV_RANK)), full((2 * LANES, wide)), full((KV_RANK, wide)),
                  full((1, LANES)), ident, ident, ident],
        out_specs=(row(wide), row(wide)),
        out_shape=(jax.ShapeDtypeStruct((r, wide), BF16), jax.ShapeDtypeStruct((r, wide), BF16)),
        compiler_params=_cparams(("arbitrary",)),
        name="mla_prep_cache",
    )(ckv_c, kr_c, mw['gkv'], mw['wkc'], mw['wv'], mw['gkn'], *tabs)


_NT = (((1,), (1,)), ((), ()))


def _mla_attn_kernel(*refs, has_cache):
    if has_cache:
        q_ref, ko_ref, vo_ref, kc_ref, vc_ref, o_ref = refs
    else:
        q_ref, ko_ref, vo_ref, o_ref = refs
    acc = jnp.zeros(o_ref.shape, F32)
    for hh in range(2):
        sl = slice(hh * LANES, (hh + 1) * LANES)
        q = q_ref[:, sl]
        s_o = lax.dot_general(q, ko_ref[:, sl], _NT, preferred_element_type=F32)
        m = jnp.max(s_o, axis=-1, keepdims=True)
        if has_cache:
            s_c = lax.dot_general(q, kc_ref[:, sl], _NT, preferred_element_type=F32)
            m = jnp.maximum(m, jnp.max(s_c, axis=-1, keepdims=True))
        e_o = jnp.exp(s_o - m)
        l = jnp.sum(e_o, axis=-1, keepdims=True)
        pv = jnp.dot(e_o.astype(BF16), vo_ref[:, sl], preferred_element_type=F32)
        if has_cache:
            e_c = jnp.exp(s_c - m)
            l = l + jnp.sum(e_c, axis=-1, keepdims=True)
            pv = pv + jnp.dot(e_c.astype(BF16), vc_ref[:, sl], preferred_element_type=F32)
        acc = acc + pv * (1.0 / l)
    o_ref[...] = acc.astype(o_ref.dtype)


def mla_attention(q, k, v, kc, vc, *, row0, n_seq, t_seq, tq):
    has_cache = kc is not None
    n_pair = H_A // 2
    nq = t_seq // tq
    qb0, kb0 = row0 // tq, row0 // t_seq
    in_specs = [pl.BlockSpec((tq, 2 * LANES), lambda s, p, i: (qb0 + s * nq + i, p)),
                pl.BlockSpec((t_seq, 2 * LANES), lambda s, p, i: (kb0 + s, p)),
                pl.BlockSpec((t_seq, 2 * LANES), lambda s, p, i: (kb0 + s, p))]
    args = [q, k, v]
    if has_cache:
        in_specs += [pl.BlockSpec((PAST_LEN, 2 * LANES), lambda s, p, i: (s, p)),
                     pl.BlockSpec((PAST_LEN, 2 * LANES), lambda s, p, i: (s, p))]
        args += [kc, vc]
    return pl.pallas_call(
        functools.partial(_mla_attn_kernel, has_cache=has_cache),
        grid=(n_seq, n_pair, nq),
        in_specs=in_specs,
        out_specs=pl.BlockSpec((tq, LANES), lambda s, p, i: (s * nq + i, p)),
        out_shape=jax.ShapeDtypeStruct((n_seq * t_seq, W_BRANCH), BF16),
        compiler_params=_cparams(("arbitrary", "arbitrary", "arbitrary")),
        name="mla_attention",
    )(*args)


DIFF_TM = 512


def _diff_prep_kernel(dq_ref, dk_ref, gq_ref, gk_ref, c_ref, sa_ref, sb_ref, q_o, ko_o, kp_o):
    c, sa, sb = c_ref[...], sa_ref[...], sb_ref[...]
    lane = lax.broadcasted_iota(jnp.int32, (1, LANES), 1)
    lo = lane < DH_D

    def normed(x, g):
        x2 = x * x
        s_lo = jnp.sum(jnp.where(lo, x2, 0.0), axis=-1, keepdims=True)
        s_hi = jnp.sum(jnp.where(lo, 0.0, x2), axis=-1, keepdims=True)
        ms = jnp.where(lo, s_lo, s_hi) * (1.0 / DH_D)
        return x * lax.rsqrt(ms + EPS) * g

    for j in range(W_BRANCH // LANES):
        sl = slice(j * LANES, (j + 1) * LANES)
        qn = normed(dq_ref[:, sl].astype(F32), gq_ref[...])
        q_o[:, sl] = (_rope(qn, c, sa, sb, DH_D // 4) * DIFF_SCALE).astype(q_o.dtype)
        kn = normed(dk_ref[:, sl].astype(F32), gk_ref[...])
        kp_o[:, sl] = kn
        ko_o[:, sl] = _rope(kn, c, sa, sb, DH_D // 4).astype(ko_o.dtype)


def diff_prep(p, lp, tabs):
    tm = DIFF_TM
    tile2 = lambda g: jnp.concatenate([g, g]).reshape(1, LANES)
    full = pl.BlockSpec((1, LANES), lambda i: (0, 0))
    tab = pl.BlockSpec((tm, LANES), lambda i: (_pos_block_of_tile(i, tm), 0))
    row = pl.BlockSpec((tm, W_BRANCH), lambda i: (i, 0))
    dq_b, dk_b = _seg_block('dq'), _seg_block('dk')
    return pl.pallas_call(
        _diff_prep_kernel,
        grid=(R_ALL // tm,),
        in_specs=[pl.BlockSpec((tm, W_BRANCH), lambda i: (i, dq_b)), pl.BlockSpec((tm, W_BRANCH), lambda i: (i, dk_b)),
                  full, full, tab, tab, tab],
        out_specs=(row, row, row),
        out_shape=(jax.ShapeDtypeStruct((R_ALL, W_BRANCH), BF16), jax.ShapeDtypeStruct((R_ALL, W_BRANCH), BF16),
                   jax.ShapeDtypeStruct((R_ALL, W_BRANCH), F32)),
        compiler_params=_cparams(("arbitrary",)),
        name="diff_prep",
    )(p, p, tile2(lp['g_diff_qn']), tile2(lp['g_diff_kn']), *tabs)


def _diff_attn_kernel(*refs, has_cache, lambda_init):
    if has_cache:
        dl_ref, q1_ref, q2_ref, k1_ref, k2_ref, v_ref, k1c_ref, k2c_ref, vc_ref, g_ref, o_ref = refs
    else:
        dl_ref, q1_ref, q2_ref, k1_ref, k2_ref, v_ref, g_ref, o_ref = refs
    half = pl.program_id(1) % 2
    lane = lax.broadcasted_iota(jnp.int32, (1, LANES), 1)
    mine = (lane // DH_D) == half
    dl = dl_ref[...]
    lam = (jnp.exp(jnp.sum(dl[0:1] * dl[1:2], axis=-1, keepdims=True))
           - jnp.exp(jnp.sum(dl[2:3] * dl[3:4], axis=-1, keepdims=True)) + lambda_init)

    def attend(q_ref, k_ref, kc_ref):
        q = jnp.where(mine, q_ref[...], jnp.zeros((), q_ref.dtype))
        s_o = lax.dot_general(q, k_ref[...].astype(BF16), _NT, preferred_element_type=F32)
        m = jnp.max(s_o, axis=-1, keepdims=True)
        if has_cache:
            s_c = lax.dot_general(q, kc_ref[...].astype(BF16), _NT, preferred_element_type=F32)
            m = jnp.maximum(m, jnp.max(s_c, axis=-1, keepdims=True))
        e_o = jnp.exp(s_o - m)
        l = jnp.sum(e_o, axis=-1, keepdims=True)
        pv = jnp.dot(e_o.astype(BF16), v_ref[...].astype(BF16), preferred_element_type=F32)
        if has_cache:
            e_c = jnp.exp(s_c - m)
            l = l + jnp.sum(e_c, axis=-1, keepdims=True)
            pv = pv + jnp.dot(e_c.astype(BF16), vc_ref[...].astype(BF16), preferred_element_type=F32)
        return pv * (1.0 / l)

    y = attend(q1_ref, k1_ref, k1c_ref if has_cache else None) - lam * attend(q2_ref, k2_ref, k2c_ref if has_cache else None)
    r = lax.rsqrt(jnp.mean(y * y, axis=-1, keepdims=True) + EPS)
    o_ref[...] = ((y * r * g_ref[...]) * (1.0 - lambda_init)).astype(o_ref.dtype)


def diff_attention(dl, qd, kd, p, kc, vc, g_sub, *, row0, n_seq, t_seq, tq, lambda_init):
    has_cache = kc is not None
    nq = t_seq // tq
    qb0, kb0 = row0 // tq, row0 // t_seq
    dv_b = _seg_block('dv') * (W_BRANCH // LANES)
    in_specs = [pl.BlockSpec((4, DH_D), lambda s, h, i: (0, 0)),
                pl.BlockSpec((tq, LANES), lambda s, h, i: (qb0 + s * nq + i, h // 2)),
                pl.BlockSpec((tq, LANES), lambda s, h, i: (qb0 + s * nq + i, 2 + h // 2)),
                pl.BlockSpec((t_seq, LANES), lambda s, h, i: (kb0 + s, h // 2)),
                pl.BlockSpec((t_seq, LANES), lambda s, h, i: (kb0 + s, 2 + h // 2)),
                pl.BlockSpec((t_seq, LANES), lambda s, h, i: (kb0 + s, dv_b + h))]
    args = [dl, qd, qd, kd, kd, p]
    if has_cache:
        in_specs += [pl.BlockSpec((PAST_LEN, LANES), lambda s, h, i: (s, h // 2)),
                     pl.BlockSpec((PAST_LEN, LANES), lambda s, h, i: (s, 2 + h // 2)),
                     pl.BlockSpec((PAST_LEN, LANES), lambda s, h, i: (s, h))]
        args += [kc, kc, vc]
    in_specs.append(pl.BlockSpec((1, LANES), lambda s, h, i: (0, 0)))
    args.append(g_sub.reshape(1, LANES))
    return pl.pallas_call(
        functools.partial(_diff_attn_kernel, has_cache=has_cache, lambda_init=lambda_init),
        grid=(n_seq, H_D, nq),
        in_specs=in_specs,
        out_specs=pl.BlockSpec((tq, LANES), lambda s, h, i: (s * nq + i, h)),
        out_shape=jax.ShapeDtypeStruct((n_seq * t_seq, W_BRANCH), BF16),
        compiler_params=_cparams(("arbitrary", "arbitrary", "arbitrary")),
        name="diff_attention",
    )(*args)


def _neg_expm1(z):
    series = -z * (1.0 + z * (0.5 + z * (1.0 / 6.0 + z * (1.0 / 24.0 + z * (1.0 / 120.0 + z * (1.0 / 720.0))))))
    return jnp.where(z > -0.25, series, 1.0 - jnp.exp(z))


def _softplus(z):
    return jnp.maximum(z, 0.0) + jnp.log(1.0 + jnp.exp(-jnp.abs(z)))


def _gelu_tanh(x):
    return 0.5 * x * (1.0 + jnp.tanh(math.sqrt(2.0 / math.pi) * (x + 0.044715 * (x * x * x))))


def _rglru_kernel(x_ref, gate_ref, wc_ref, bc_ref, wg_ref, bg_ref, lam_ref, h0_ref, y_ref, st_ref,
                  a_s, u_s, h_s, *, t_seq):
    t = t_seq
    x = x_ref[...].astype(F32)
    row = lax.broadcasted_iota(jnp.int32, (t, W_B), 0)
    wc = wc_ref[...]
    xc = (wc[0:1] * jnp.where(row >= 2, pltpu.roll(x, 2, 0), 0.0)
          + wc[1:2] * jnp.where(row >= 1, pltpu.roll(x, 1, 0), 0.0)
          + wc[2:3] * x
          + wc[3:4] * jnp.where(row < t - 1, pltpu.roll(x, t - 1, 0), 0.0)
          + bc_ref[...])
    gates = jnp.dot(xc.astype(BF16), wg_ref[...], preferred_element_type=F32) + bg_ref[...]
    r8 = row % SUBLANES
    nblk = t // SUBLANES
    hsum = None
    for d in range(2):
        rg = jax.nn.sigmoid(gates[:, (2 * d) * W_B:(2 * d + 1) * W_B])
        ig = jax.nn.sigmoid(gates[:, (2 * d + 1) * W_B:(2 * d + 2) * W_B])
        log_a = -RG_C * rg * _softplus(-lam_ref[d:d + 1, :])
        a = jnp.exp(log_a)
        u = jnp.sqrt(_neg_expm1(2.0 * log_a)) * (ig * xc)
        for step in (1, 2, 4):
            if d == 0:
                valid = r8 >= step
                a_sh, u_sh = pltpu.roll(a, step, 0), pltpu.roll(u, step, 0)
            else:
                valid = r8 + step < SUBLANES
                a_sh, u_sh = pltpu.roll(a, t - step, 0), pltpu.roll(u, t - step, 0)
            u = jnp.where(valid, a * u_sh + u, u)
            a = jnp.where(valid, a * a_sh, a)
        a_s[...] = a
        u_s[...] = u
        h0 = h0_ref[0, d:d + 1, :]

        def body(k, carry, d=d):
            blk = k if d == 0 else nblk - 1 - k
            rows = pl.ds(pl.multiple_of(blk * SUBLANES, SUBLANES), SUBLANES)
            h = a_s[rows, :] * carry + u_s[rows, :]
            h_s[rows, :] = h
            return h[SUBLANES - 1:SUBLANES, :] if d == 0 else h[0:1, :]

        last = lax.fori_loop(0, nblk, body, h0, unroll=8)
        st_ref[0, d:d + 1, :] = last
        hsum = h_s[...] if d == 0 else hsum + h_s[...]
    y_ref[...] = (_gelu_tanh(gate_ref[...].astype(F32)) * hsum).astype(y_ref.dtype)


def pack_rglru_weights(lp):
    def blockdiag(w):
        eye = jnp.eye(NB_B, dtype=w.dtype)
        return jnp.einsum('ncd,nm->ncmd', w, eye).reshape(W_B, W_B)
    wg = jnp.concatenate([blockdiag(lp['w_rg_a'][0]), blockdiag(lp['w_rg_x'][0]),
                          blockdiag(lp['w_rg_a'][1]), blockdiag(lp['w_rg_x'][1])], axis=1).astype(BF16)
    bg = jnp.concatenate([lp['b_rg_a'][0], lp['b_rg_x'][0], lp['b_rg_a'][1], lp['b_rg_x'][1]]).reshape(1, 4 * W_B)
    return dict(wg=wg, bg=bg, wc=lp['w_conv_rg'], bc=lp['b_conv_rg'].reshape(1, W_B), lam=lp['rg_lambda'])


def rglru(p, rw, h0, *, row0, n_seq, t_seq):
    rb0 = row0 // t_seq
    xb, gb = _seg_block('rg_x'), _seg_block('rg_gate')
    full = lambda shape: pl.BlockSpec(shape, lambda s: tuple(0 for _ in shape))
    in_specs = [pl.BlockSpec((t_seq, W_B), lambda s: (rb0 + s, xb)),
                pl.BlockSpec((t_seq, W_B), lambda s: (rb0 + s, gb)),
                full((CONV_W, W_B)), full((1, W_B)), full((W_B, 4 * W_B)), full((1, 4 * W_B)), full((2, W_B)),
                pl.BlockSpec((1, 2, W_B), lambda s: (s, 0, 0))]
    args = [p, p, rw['wc'], rw['bc'], rw['wg'], rw['bg'], rw['lam'], h0]
    return pl.pallas_call(
        functools.partial(_rglru_kernel, t_seq=t_seq),
        grid=(n_seq,),
        in_specs=in_specs,
        out_specs=(pl.BlockSpec((t_seq, W_B), lambda s: (s, 0)), pl.BlockSpec((1, 2, W_B), lambda s: (s, 0, 0))),
        out_shape=(jax.ShapeDtypeStruct((n_seq * t_seq, W_B), BF16), jax.ShapeDtypeStruct((n_seq, 2, W_B), F32)),
        scratch_shapes=[pltpu.VMEM((t_seq, W_B), F32)] * 3,
        compiler_params=_cparams(("arbitrary",)),
        name="rglru",
    )(*args)


def _dot_split(a, b_bf16):
    hi = a.astype(BF16)
    lo = (a - hi.astype(F32)).astype(BF16)
    return (jnp.dot(hi, b_bf16, preferred_element_type=F32) + jnp.dot(lo, b_bf16, preferred_element_type=F32))


def _log_sigmoid(z):
    return jnp.minimum(z, 0.0) - jnp.log(1.0 + jnp.exp(-jnp.abs(z)))


_TN = (((0,), (0,)), ((), ()))


def _mlstm_kernel(q_ref, k_ref, v_ref, o_ref, sm_ref, bias_ref, g_ref, c0_ref, m0_ref,
                  y_ref, c_out, m_out, hm_s, c_s, *, t_seq):
    L = ML_CHUNK
    nchunk = t_seq // L
    scale = DH_C ** -0.5
    ri = lax.broadcasted_iota(jnp.int32, (L, L), 0)
    ci = lax.broadcasted_iota(jnp.int32, (L, L), 1)
    lane1 = lax.broadcasted_iota(jnp.int32, (L, LANES), 1)
    ones_col = jnp.where(lane1 == 0, 1.0, 0.0).astype(BF16)
    bias = bias_ref[...]

    for d in range(2):
        causal = (ci <= ri) if d == 0 else (ci >= ri)
        tri = jnp.where(causal, 1.0, 0.0).astype(BF16)
        tri_t = jnp.where((ri <= ci) if d == 0 else (ri >= ci), 1.0, 0.0).astype(BF16)
        c_s[...] = c0_ref[0, d]
        m_init = tuple(m0_ref[0, :, d * H_C + h:d * H_C + h + 1] for h in range(H_C))

        def chunk(kk, ms, d=d, causal=causal, tri=tri, tri_t=tri_t):
            cidx = kk if d == 0 else nchunk - 1 - kk
            rows = pl.ds(pl.multiple_of(cidx * L, L), L)
            gsm = sm_ref[rows, :].astype(F32) + bias
            lf_all = _log_sigmoid(gsm)
            cum_cols = _dot_split_left(tri, lf_all)
            g_t = gsm.T
            cum_rows = _dot_split(lf_all.T, tri_t)
            new_ms = []
            for h in range(H_C):
                jl, jf = SM_MI + d * H_C + h, SM_MF + d * H_C + h
                cum_c = cum_cols[:, jf:jf + 1]
                cum_r = cum_rows[jf:jf + 1, :]
                li_c = gsm[:, jl:jl + 1]
                li_r = g_t[jl:jl + 1, :]
                m_mem = ms[h]
                log_d = jnp.where(causal, cum_c - cum_r + li_r, -jnp.inf)
                inter = cum_c + m_mem
                m_row = jnp.maximum(inter, jnp.max(log_d, axis=-1, keepdims=True))
                dmat = jnp.exp(log_d - m_row)
                w_inter = jnp.exp(inter - m_row)
                sl = slice(h * DH_C, (h + 1) * DH_C)
                qh, kh, vh = q_ref[rows, sl], k_ref[rows, sl], v_ref[rows, sl]
                s = lax.dot_general(qh, kh, _NT, preferred_element_type=F32) * (scale * dmat)
                sv = jnp.dot(s.astype(BF16), vh, preferred_element_type=F32)
                qc = jnp.dot(qh, c_s[h].astype(BF16), preferred_element_type=F32) * scale
                num = sv + qc[:, :DH_C] * w_inter
                den = jnp.sum(s, axis=-1, keepdims=True) + w_inter * qc[:, DH_C:DH_C + 1]
                den = jnp.maximum(jnp.abs(den), jnp.exp(-m_row))
                h_out = num * (1.0 / den)
                if d == 0:
                    hm_s[rows, sl] = h_out
                else:
                    hm_s[rows, sl] = hm_s[rows, sl] + h_out
                last = cum_c[L - 1:L, :] if d == 0 else cum_c[0:1, :]
                w_s = last - cum_c + li_c
                m_new = jnp.maximum(last + m_mem, jnp.max(w_s, axis=0, keepdims=True))
                decay = jnp.exp(last + m_mem - m_new)
                ws = jnp.exp(w_s - m_new)
                kw_t = (kh.astype(F32) * ws).T.astype(BF16)
                v_aug = jnp.concatenate([vh, ones_col], axis=1)
                c_s[h] = decay * c_s[h] + jnp.dot(kw_t, v_aug, preferred_element_type=F32)
                new_ms.append(m_new)
            return tuple(new_ms)

        m_fin = lax.fori_loop(0, nchunk, chunk, m_init)
        c_out[0, d] = c_s[...]
        for h in range(H_C):
            m_out[0, :, d * H_C + h:d * H_C + h + 1] = m_fin[h]

    for h in range(H_C):
        sl = slice(h * DH_C, (h + 1) * DH_C)
        hm = hm_s[:, sl]
        r = lax.rsqrt(jnp.mean(hm * hm, axis=-1, keepdims=True) + EPS)
        y_ref[:, sl] = (jax.nn.sigmoid(o_ref[:, sl].astype(F32)) * (hm * r * g_ref[...])).astype(y_ref.dtype)


def _dot_split_left(a_bf16, b):
    hi = b.astype(BF16)
    lo = (b - hi.astype(F32)).astype(BF16)
    return (jnp.dot(a_bf16, hi, preferred_element_type=F32) + jnp.dot(a_bf16, lo, preferred_element_type=F32))


def mlstm(p, bias_sm, g_out, c0_aug, m0, *, row0, n_seq, t_seq):
    rb0 = row0 // t_seq
    seg = lambda nm: pl.BlockSpec((t_seq, W_BRANCH), lambda s, b=_seg_block(nm): (rb0 + s, b))
    full = lambda shape: pl.BlockSpec(shape, lambda s: tuple(0 for _ in shape))
    st_spec = pl.BlockSpec((1, 2, H_C, DH_C, 2 * DH_C), lambda s: (s, 0, 0, 0, 0))
    m_spec = pl.BlockSpec((1, 1, 2 * H_C), lambda s: (s, 0, 0))
    in_specs = [seg('mq'), seg('mk'), seg('mv'), seg('mo'),
                pl.BlockSpec((t_seq, LANES), lambda s: (rb0 + s, P_SMALL // LANES)),
                full((1, LANES)), full((1, DH_C)), st_spec, m_spec]
    args = [p, p, p, p, p, bias_sm, g_out.reshape(1, DH_C), c0_aug, m0]
    return pl.pallas_call(
        functools.partial(_mlstm_kernel, t_seq=t_seq),
        grid=(n_seq,),
        in_specs=in_specs,
        out_specs=(pl.BlockSpec((t_seq, W_BRANCH), lambda s: (s, 0)), st_spec, m_spec),
        out_shape=(jax.ShapeDtypeStruct((n_seq * t_seq, W_BRANCH), BF16),
                   jax.ShapeDtypeStruct((n_seq, 2, H_C, DH_C, 2 * DH_C), F32),
                   jax.ShapeDtypeStruct((n_seq, 1, 2 * H_C), F32)),
        scratch_shapes=[pltpu.VMEM((t_seq, W_BRANCH), F32), pltpu.VMEM((H_C, DH_C, 2 * DH_C), F32)],
        compiler_params=_cparams(("arbitrary",)),
        name="mlstm",
    )(*args)


MERGE_TM = 512


def _merge_kernel(*refs):
    ctx_refs, lat_refs, gate_refs = refs[0:4], refs[4:8], refs[8:12]
    x_ref, mod_ref, wbr_ref, wout_ref, gn_ref, xo_ref, xn_ref = refs[12:19]
    xn3_ref = refs[19] if len(refs) > 19 else None
    is_ctx = pl.program_id(0) < R_CTX // MERGE_TM
    merged = None
    for g in range(N_BRANCH):
        yg = jnp.where(is_ctx, ctx_refs[g][...], lat_refs[g][...])
        pg = jnp.dot(yg, wbr_ref[g], preferred_element_type=F32)
        term = jax.nn.sigmoid(gate_refs[g][...].astype(F32)) * pg
        merged = term if merged is None else merged + term
    y = jnp.dot(merged.astype(BF16), wout_ref[...], preferred_element_type=F32)
    x = x_ref[...] + mod_ref[0, 2:3, :] * y
    xo_ref[...] = x
    r = lax.rsqrt(jnp.mean(x * x, axis=-1, keepdims=True) + EPS)
    xn = (x * r * gn_ref[...]) * (1.0 + mod_ref[0, 4:5, :]) + mod_ref[0, 3:4, :]
    xn_ref[...] = xn.astype(xn_ref.dtype)
    if xn3_ref is not None:
        _rows_to_tiles(xn3_ref, xn.astype(BF16))


def merge(ys_ctx, ys_lat, p, x, mod, wbr, wout, g_ffn, rows_as_tiles):
    tm = MERGE_TM
    n_ctx_tiles = R_CTX // tm
    br_ctx = pl.BlockSpec((tm, W_BRANCH), lambda i: (jnp.minimum(i, n_ctx_tiles - 1), 0))
    br_lat = pl.BlockSpec((tm, W_BRANCH), lambda i: (jnp.maximum(i - n_ctx_tiles, 0), 0))
    gate = lambda g: pl.BlockSpec((tm, D_MODEL), lambda i, g=g: (i, g))
    row = pl.BlockSpec((tm, D_MODEL), lambda i: (i, 0))
    out_specs = [row, row]
    out_shape = [jax.ShapeDtypeStruct((R_ALL, D_MODEL), F32), jax.ShapeDtypeStruct((R_ALL, D_MODEL), BF16)]
    if rows_as_tiles:
        out_specs.append(pl.BlockSpec((tm, ROW_SUB, LANES), lambda i: (i, 0, 0)))
        out_shape.append(jax.ShapeDtypeStruct((R_ALL, ROW_SUB, LANES), F32))
    return pl.pallas_call(
        _merge_kernel,
        grid=(R_ALL // tm,),
        in_specs=[br_ctx] * N_BRANCH + [br_lat] * N_BRANCH + [gate(0), gate(1), gate(2), gate(3), row,
                  pl.BlockSpec((1, 8, D_MODEL), lambda i: (_mod_row_of_tile(i, tm), 0, 0)),
                  pl.BlockSpec((N_BRANCH, W_BRANCH, D_MODEL), lambda i: (0, 0, 0)),
                  pl.BlockSpec((D_MODEL, D_MODEL), lambda i: (0, 0)),
                  pl.BlockSpec((1, D_MODEL), lambda i: (0, 0))],
        out_specs=tuple(out_specs),
        out_shape=tuple(out_shape),
        compiler_params=_cparams(("arbitrary",)),
        name="merge",
    )(*ys_ctx, *ys_lat, p, p, p, p, x, mod, wbr, wout, g_ffn.reshape(1, D_MODEL))


def _new_expert(te_ref, i):
    return jnp.logical_or(i == 0, te_ref[i] != te_ref[jnp.maximum(i - 1, 0)])


def _ffn_up_kernel(te_ref, nt_ref, x_ref, wg_ref, wu_ref, h_ref, wgb_ref, wub_ref):
    i = pl.program_id(1)

    @pl.when(_new_expert(te_ref, i))
    def _():
        wgb_ref[...] = wg_ref[0].astype(BF16)
        wub_ref[...] = wu_ref[0].astype(BF16)

    @pl.when(i < nt_ref[0])
    def _():
        x = x_ref[...]
        g = jnp.dot(x, wgb_ref[...], preferred_element_type=F32)
        u = jnp.dot(x, wub_ref[...], preferred_element_type=F32)
        h_ref[...] = (g * jax.nn.sigmoid(g) * u).astype(h_ref.dtype)

    @pl.when(i >= nt_ref[0])
    def _():
        h_ref[...] = jnp.zeros(h_ref.shape, h_ref.dtype)


def ffn_up(tile_expert, n_tiles, xs, wg, wu, tm, tf):
    r, d = xs.shape
    f = wg.shape[2]
    return pl.pallas_call(
        _ffn_up_kernel,
        grid_spec=pltpu.PrefetchScalarGridSpec(
            num_scalar_prefetch=2,
            grid=(f // tf, r // tm),
            in_specs=[pl.BlockSpec((tm, d), lambda j, i, te, nt: (i, 0)),
                      pl.BlockSpec((1, d, tf), lambda j, i, te, nt: (te[i], 0, j)),
                      pl.BlockSpec((1, d, tf), lambda j, i, te, nt: (te[i], 0, j))],
            out_specs=pl.BlockSpec((tm, tf), lambda j, i, te, nt: (i, j)),
            scratch_shapes=[pltpu.VMEM((d, tf), BF16), pltpu.VMEM((d, tf), BF16)]),
        out_shape=jax.ShapeDtypeStruct((r, f), BF16),
        compiler_params=_cparams(("arbitrary", "arbitrary")),
        name="ffn_up",
    )(tile_expert, n_tiles, xs, wg, wu)


def _ffn_down_kernel(te_ref, nt_ref, h_ref, wd_ref, y_ref, wdb_ref):
    i = pl.program_id(0)

    @pl.when(_new_expert(te_ref, i))
    def _():
        wdb_ref[...] = wd_ref[0].astype(BF16)

    @pl.when(i < nt_ref[0])
    def _():
        _rows_to_tiles(y_ref, jnp.dot(h_ref[...], wdb_ref[...], preferred_element_type=F32))

    @pl.when(i >= nt_ref[0])
    def _():
        y_ref[...] = jnp.zeros(y_ref.shape, y_ref.dtype)


def ffn_down(tile_expert, n_tiles, h, wd, tm):
    r, f = h.shape
    d = wd.shape[2]
    return pl.pallas_call(
        _ffn_down_kernel,
        grid_spec=pltpu.PrefetchScalarGridSpec(
            num_scalar_prefetch=2,
            grid=(r // tm,),
            in_specs=[pl.BlockSpec((tm, f), lambda i, te, nt: (i, 0)),
                      pl.BlockSpec((1, f, d), lambda i, te, nt: (te[i], 0, 0))],
            out_specs=pl.BlockSpec((tm, ROW_SUB, LANES), lambda i, te, nt: (i, 0, 0)),
            scratch_shapes=[pltpu.VMEM((f, d), BF16)]),
        out_shape=jax.ShapeDtypeStruct((r, ROW_SUB, LANES), F32),
        compiler_params=_cparams(("arbitrary",)),
        name="ffn_down",
    )(tile_expert, n_tiles, h, wd)


def _ffn_down_res_kernel(h_ref, wd_ref, x_ref, mod_ref, y_ref, wdb_ref):
    @pl.when(pl.program_id(0) == 0)
    def _():
        wdb_ref[...] = wd_ref[...].astype(BF16)

    y = jnp.dot(h_ref[...], wdb_ref[...], preferred_element_type=F32)
    y_ref[...] = x_ref[...] + mod_ref[0, 5:6, :] * y


def ffn_down_residual(h, wd, x, mod):
    tm = 512
    r, f = h.shape
    d = wd.shape[1]
    return pl.pallas_call(
        _ffn_down_res_kernel,
        grid=(r // tm,),
        in_specs=[pl.BlockSpec((tm, f), lambda i: (i, 0)),
                  pl.BlockSpec((f, d), lambda i: (0, 0)),
                  pl.BlockSpec((tm, d), lambda i: (i, 0)),
                  pl.BlockSpec((1, 8, d), lambda i: (_mod_row_of_tile(i, tm), 0, 0))],
        out_specs=pl.BlockSpec((tm, d), lambda i: (i, 0)),
        out_shape=jax.ShapeDtypeStruct((r, d), F32),
        scratch_shapes=[pltpu.VMEM((f, d), BF16)],
        compiler_params=_cparams(("arbitrary",)),
        name="ffn_down_residual",
    )(h, wd, x, mod)


def dense_swiglu_residual(xn, x, mod, wg, wu, wd):
    t = xn.shape[0]
    tm = 512
    n_tiles = t // tm
    te = jnp.zeros((n_tiles,), jnp.int32)
    nt = jnp.full((1,), n_tiles, jnp.int32)
    h = ffn_up(te, nt, xn, wg[None], wu[None], tm=tm, tf=wg.shape[1] // 2)
    return ffn_down_residual(h, wd, x, mod)


MOE_TM = 512


def _moe_routing(logits, tm):
    t = logits.shape[0]
    n_assign = t * TOP_K
    top_v, top_i = lax.top_k(logits, TOP_K)
    gate = jax.nn.softmax(top_v, axis=-1)
    flat_e = top_i.reshape(-1).astype(jnp.int32)
    onehot = (flat_e[:, None] == jnp.arange(N_EXP, dtype=jnp.int32)[None, :])
    blk = LANES
    oh = onehot.astype(F32).reshape(n_assign // blk, blk, N_EXP)
    tril = jnp.tril(jnp.ones((blk, blk), F32))
    within = jnp.einsum('ij,bjk->bik', tril, oh)
    blk_tot = within[:, -1, :]
    blk_off = jnp.cumsum(blk_tot, axis=0) - blk_tot
    csum = (within + blk_off[:, None, :]).reshape(n_assign, N_EXP)
    rank = jnp.sum(jnp.where(onehot, csum - 1.0, 0.0), axis=1).astype(jnp.int32)
    counts = csum[-1].astype(jnp.int32)
    padded = (counts + tm - 1) // tm * tm
    grp_start = jnp.cumsum(padded) - padded
    raw_start = jnp.cumsum(counts) - counts
    slot_of_assign = jnp.sum(jnp.where(onehot, grp_start[None, :], 0), axis=1) + rank

    r_max = n_assign + N_EXP * tm
    tile_start = jnp.arange(r_max // tm, dtype=jnp.int32) * tm
    tile_expert = jnp.sum((tile_start[:, None] >= (grp_start + padded)[None, :]).astype(jnp.int32), axis=1)
    tile_expert = jnp.minimum(tile_expert, N_EXP - 1).astype(jnp.int32)
    n_tiles = (jnp.sum(padded) // tm).astype(jnp.int32).reshape(1)

    order = jnp.argsort(flat_e, stable=True).astype(jnp.int32)
    e_slot = jnp.repeat(tile_expert, tm)
    j = jnp.arange(r_max, dtype=jnp.int32) - grp_start[e_slot]
    src = jnp.clip(raw_start[e_slot] + j, 0, n_assign - 1)
    tok_of_slot = jnp.where(j < counts[e_slot], order[src] // TOP_K, 0)
    return gate, slot_of_assign, tok_of_slot, tile_expert, n_tiles


ROW_SUB = D_MODEL // LANES


def _rows_to_tiles(o3_ref, x):
    for j in range(ROW_SUB):
        o3_ref[:, j, :] = x[:, j * LANES:(j + 1) * LANES].astype(o3_ref.dtype)


def _dispatch_kernel(nt_ref, idx_ref, src_ref, o_ref, buf, sem):
    tm = o_ref.shape[0]

    @pl.when(pl.program_id(0) < nt_ref[0])
    def _():
        def issue(r, carry):
            pltpu.make_async_copy(src_ref.at[idx_ref[0, 0, r]], buf.at[r], sem).start()
            return carry

        lax.fori_loop(0, tm, issue, 0, unroll=8)
        pltpu.make_async_copy(src_ref.at[pl.ds(0, tm)], buf, sem).wait()
        for j in range(ROW_SUB):
            o_ref[:, j * LANES:(j + 1) * LANES] = buf[:, j, :].astype(o_ref.dtype)

    @pl.when(pl.program_id(0) >= nt_ref[0])
    def _():
        o_ref[...] = jnp.zeros(o_ref.shape, o_ref.dtype)


def moe_dispatch(n_tiles, tok_of_slot, xn3, tm):
    r = tok_of_slot.shape[0]
    return pl.pallas_call(
        _dispatch_kernel,
        grid_spec=pltpu.PrefetchScalarGridSpec(
            num_scalar_prefetch=1,
            grid=(r // tm,),
            in_specs=[pl.BlockSpec((1, 1, tm), lambda i, nt: (i, 0, 0), memory_space=pltpu.SMEM),
                      pl.BlockSpec(memory_space=pl.ANY)],
            out_specs=pl.BlockSpec((tm, D_MODEL), lambda i, nt: (i, 0)),
            scratch_shapes=[pltpu.VMEM((tm, ROW_SUB, LANES), F32), pltpu.SemaphoreType.DMA(())]),
        out_shape=jax.ShapeDtypeStruct((r, D_MODEL), BF16),
        compiler_params=_cparams(("arbitrary",)),
        name="moe_dispatch",
    )(n_tiles, tok_of_slot.reshape(r // tm, 1, tm), xn3)


COMBINE_TM = 256


def _combine_kernel(idx_ref, ys_ref, x_ref, gate_ref, mod_ref, o_ref, buf, sem):
    tm = o_ref.shape[0]

    def issue(a, carry):
        pltpu.make_async_copy(ys_ref.at[idx_ref[0, 0, a]], buf.at[a], sem).start()
        return carry

    lax.fori_loop(0, TOP_K * tm, issue, 0, unroll=8)
    pltpu.make_async_copy(ys_ref.at[pl.ds(0, TOP_K * tm)], buf, sem).wait()
    g0 = gate_ref[:, 0:1]
    g1 = gate_ref[:, 1:2]
    for j in range(ROW_SUB):
        sl = slice(j * LANES, (j + 1) * LANES)
        f = g0 * buf[0:tm, j, :] + g1 * buf[tm:2 * tm, j, :]
        o_ref[:, sl] = x_ref[:, sl] + mod_ref[0, 5:6, sl] * f


def moe_combine(slot_of_assign, ys3, x, gate, mod):
    t, d = x.shape
    tm = COMBINE_TM
    idx = slot_of_assign.reshape(t // tm, tm, TOP_K).transpose(0, 2, 1).reshape(t // tm, 1, TOP_K * tm)
    return pl.pallas_call(
        _combine_kernel,
        grid=(t // tm,),
        in_specs=[pl.BlockSpec((1, 1, TOP_K * tm), lambda i: (i, 0, 0), memory_space=pltpu.SMEM),
                  pl.BlockSpec(memory_space=pl.ANY),
                  pl.BlockSpec((tm, d), lambda i: (i, 0)),
                  pl.BlockSpec((tm, TOP_K), lambda i: (i, 0)),
                  pl.BlockSpec((1, 8, d), lambda i: (_mod_row_of_tile(i, tm), 0, 0))],
        out_specs=pl.BlockSpec((tm, d), lambda i: (i, 0)),
        out_shape=jax.ShapeDtypeStruct((t, d), F32),
        scratch_shapes=[pltpu.VMEM((TOP_K * tm, ROW_SUB, LANES), F32), pltpu.SemaphoreType.DMA(())],
        compiler_params=_cparams(("arbitrary",)),
        name="moe_combine",
    )(idx, ys3, x, gate, mod)


def moe_swiglu_residual(xn3, xn, x, mod, w_router, wg, wu, wd):
    tm = MOE_TM
    logits = jnp.dot(xn.astype(F32), w_router, precision=lax.Precision.HIGHEST)
    gate, slot_of_assign, tok_of_slot, tile_expert, n_tiles = _moe_routing(logits, tm)
    xs = moe_dispatch(n_tiles, tok_of_slot, xn3, tm)
    h = ffn_up(tile_expert, n_tiles, xs, wg, wu, tm=tm, tf=wg.shape[2] // 4)
    ys3 = ffn_down(tile_expert, n_tiles, h, wd, tm=tm)
    return moe_combine(slot_of_assign, ys3, x, gate, mod)


def _layer(x, cond, lp, l, ctx, tabs_a, tabs_d):
    mod = modulation(cond, lp['w_mod'], lp['b_mod']).reshape(cond.shape[0], 6, D_MODEL)
    mod = jnp.pad(mod, ((0, 0), (0, 2), (0, 0)))
    p = in_proj(x, mod, lp['g_norm_mix'], pack_w_in(lp['w_in']))

    mw = pack_mla_weights(lp)
    q_a, k_a, v_a, ckv, kr = mla_prep(p, mw, tabs_a)
    kr_c = jnp.pad(ctx['mla_krope'].reshape(-1, ROPE_A), ((0, 0), (0, LANES - ROPE_A)))
    kc_a, vc_a = mla_prep_cache(ctx['mla_ckv'].reshape(-1, KV_RANK), kr_c, mw, tabs_a)
    ya_c = mla_attention(q_a, k_a, v_a, None, None, row0=0, n_seq=N_CTX_SEQ, t_seq=T_CTX, tq=T_CTX)
    ya_l = mla_attention(q_a, k_a, v_a, kc_a, vc_a, row0=R_CTX, n_seq=N_LAT_SEQ, t_seq=T_LAT, tq=ATTN_TQ)

    rw = pack_rglru_weights(lp)
    yb_c, st_rg = rglru(p, rw, jnp.zeros((N_CTX_SEQ, 2, W_B), F32), row0=0, n_seq=N_CTX_SEQ, t_seq=T_CTX)
    yb_l, _ = rglru(p, rw, ctx['rglru'], row0=R_CTX, n_seq=N_LAT_SEQ, t_seq=T_LAT)

    bias_sm = jnp.zeros((LANES,), F32).at[SM_MI:SM_MI + 2 * H_C].set(lp['b_ml_i'].reshape(-1))
    bias_sm = bias_sm.at[SM_MF:SM_MF + 2 * H_C].set(lp['b_ml_f'].reshape(-1)).reshape(1, LANES)
    c0_ctx = jnp.zeros((N_CTX_SEQ, 2, H_C, DH_C, 2 * DH_C), F32)
    m0_ctx = jnp.zeros((N_CTX_SEQ, 1, 2 * H_C), F32)
    c0_lat = jnp.concatenate([ctx['mlstm_C'], ctx['mlstm_n'][..., None],
                              jnp.zeros(ctx['mlstm_n'].shape + (DH_C - 1,), F32)], axis=-1)
    m0_lat = ctx['mlstm_m'].reshape(N_LAT_SEQ, 1, 2 * H_C)
    yc_c, c_fin, m_fin = mlstm(p, bias_sm, lp['g_ml_out'], c0_ctx, m0_ctx, row0=0, n_seq=N_CTX_SEQ, t_seq=T_CTX)
    yc_l, _, _ = mlstm(p, bias_sm, lp['g_ml_out'], c0_lat, m0_lat, row0=R_CTX, n_seq=N_LAT_SEQ, t_seq=T_LAT)

    lambda_init = 0.8 - 0.6 * math.exp(-0.3 * l)
    qd, kd_own, kd_plain = diff_prep(p, lp, tabs_d)
    dkc = ctx['diff_k'].reshape(-1, W_BRANCH)
    dvc = ctx['diff_v'].reshape(-1, W_BRANCH)
    yd_c = diff_attention(lp['diff_lambda'], qd, kd_own, p, None, None, lp['g_diff_sub'],
                          row0=0, n_seq=N_CTX_SEQ, t_seq=T_CTX, tq=T_CTX, lambda_init=lambda_init)
    yd_l = diff_attention(lp['diff_lambda'], qd, kd_own, p, dkc, dvc, lp['g_diff_sub'],
                          row0=R_CTX, n_seq=N_LAT_SEQ, t_seq=T_LAT, tq=ATTN_TQ, lambda_init=lambda_init)

    merged = merge((ya_c, yb_c, yc_c, yd_c), (ya_l, yb_l, yc_l, yd_l), p, x, mod, lp['w_br'].astype(BF16),
                   lp['w_out'].astype(BF16), lp['g_norm_ffn'], rows_as_tiles='moe' in lp)
    if 'ffn' in lp:
        x, xn = merged
        x = dense_swiglu_residual(xn, x, mod, *lp['ffn'])
    else:
        x, xn, xn3 = merged
        x = moe_swiglu_residual(xn3, xn, x, mod, *lp['moe'])

    dv0 = _seg_block('dv') * W_BRANCH
    ctx_out = (ckv[:R_CTX].reshape(N_CTX_SEQ, T_CTX, KV_RANK),
               kr[:R_CTX, :ROPE_A].reshape(N_CTX_SEQ, T_CTX, ROPE_A),
               kd_plain[:R_CTX].reshape(N_CTX_SEQ, T_CTX, 2, H_D, DH_D),
               p[:R_CTX, dv0:dv0 + W_BRANCH].astype(F32).reshape(N_CTX_SEQ, T_CTX, H_D, 2 * DH_D),
               st_rg,
               c_fin[..., :DH_C],
               c_fin[..., DH_C],
               m_fin.reshape(N_CTX_SEQ, 2, H_C))
    return x, ctx_out


def kernel(x_prompt, x_sample, cache_mla_ckv, cache_mla_krope, cache_diff_k, cache_diff_v,
           state_rglru, state_mlstm_C, state_mlstm_n, state_mlstm_m, c, c_ctx,
           w_mod, b_mod, g_norm_mix, g_norm_ffn, w_in, g_mla_qlat, w_mla_uq, g_mla_kvlat, w_mla_ukv,
           g_mla_qn, g_mla_kn, w_conv_rg, b_conv_rg, w_rg_a, b_rg_a, w_rg_x, b_rg_x, rg_lambda,
           b_ml_i, b_ml_f, g_ml_out, g_diff_qn, g_diff_kn, diff_lambda, g_diff_sub, w_br, w_out,
           w_ffn_gate, w_ffn_up, w_ffn_down, w_router, w_moe_gate, w_moe_up, w_moe_down):
    assert x_prompt.shape == (N_CTX_SEQ, T_CTX, D_MODEL) and x_sample.shape == (N_LAT_SEQ, T_LAT, D_MODEL)
    tabs_a = _rope_tables(ROPE_A, (NOPE_A,), MLA_TM)
    tabs_d = _rope_tables(DH_D, (0, DH_D), DIFF_TM)
    cond = jnp.concatenate([c_ctx.reshape(1, D_MODEL), c, jnp.zeros((16 - 1 - N_LAT_SEQ, D_MODEL), F32)], axis=0)
    x = jnp.concatenate([x_prompt.reshape(R_CTX, D_MODEL), x_sample.reshape(R_LAT, D_MODEL)], axis=0)
    new = []
    for l in range(DEPTH):
        lp = dict(w_mod=w_mod[l], b_mod=b_mod[l], g_norm_mix=g_norm_mix[l], g_norm_ffn=g_norm_ffn[l], w_in=w_in[l],
                  g_mla_qlat=g_mla_qlat[l], w_mla_uq=w_mla_uq[l], g_mla_kvlat=g_mla_kvlat[l], w_mla_ukv=w_mla_ukv[l],
                  g_mla_qn=g_mla_qn[l], g_mla_kn=g_mla_kn[l], w_conv_rg=w_conv_rg[l], b_conv_rg=b_conv_rg[l],
                  w_rg_a=w_rg_a[l], b_rg_a=b_rg_a[l], w_rg_x=w_rg_x[l], b_rg_x=b_rg_x[l], rg_lambda=rg_lambda[l],
                  b_ml_i=b_ml_i[l], b_ml_f=b_ml_f[l], g_ml_out=g_ml_out[l], g_diff_qn=g_diff_qn[l],
                  g_diff_kn=g_diff_kn[l], diff_lambda=diff_lambda[l], g_diff_sub=g_diff_sub[l],
                  w_br=w_br[l], w_out=w_out[l])
        if l % 2 == 0:
            lp['ffn'] = (w_ffn_gate[l // 2], w_ffn_up[l // 2], w_ffn_down[l // 2])
        else:
            lp['moe'] = (w_router[l // 2], w_moe_gate[l // 2], w_moe_up[l // 2], w_moe_down[l // 2])
        ctx_l = dict(mla_ckv=cache_mla_ckv[:, l], mla_krope=cache_mla_krope[:, l], diff_k=cache_diff_k[:, l],
                     diff_v=cache_diff_v[:, l], rglru=state_rglru[:, l], mlstm_C=state_mlstm_C[:, l],
                     mlstm_n=state_mlstm_n[:, l], mlstm_m=state_mlstm_m[:, l])
        x, st = _layer(x, cond, lp, l, ctx_l, tabs_a, tabs_d)
        new.append(st)
    outs = tuple(jnp.stack([s[i] for s in new], axis=1) for i in range(8))
    return (x[:R_CTX].reshape(N_CTX_SEQ, T_CTX, D_MODEL), x[R_CTX:].reshape(N_LAT_SEQ, T_LAT, D_MODEL)) + outs
```

```python
import functools
import math

import jax
import jax.numpy as jnp
import numpy as np
from jax import lax
from jax.experimental import pallas as pl
from jax.experimental.pallas import tpu as pltpu

D_MODEL = 1024
DEPTH = 2
GRID_W = 64
ROPE_BASE = 10000.0
EPS = 1e-6
N_BRANCH = 4
W_BRANCH = D_MODEL // 2

H_A = 8
NOPE_A = 64
ROPE_A = 32
V_A = W_BRANCH // H_A
Q_RANK = D_MODEL // 4
KV_RANK = D_MODEL // 8
MLA_SCALE = (NOPE_A + ROPE_A) ** -0.5

W_B = W_BRANCH
NB_B = 8
BW_B = W_B // NB_B
CONV_W = 4
RG_C = 8.0

H_C = 4
DH_C = W_BRANCH // H_C
ML_CHUNK = 128

H_D = 4
DH_D = W_BRANCH // (2 * H_D)
DIFF_SCALE = DH_D ** -0.5

N_EXP = 8
TOP_K = 2

V7X_VMEM_LIMIT_BYTES = 56 * 1024 * 1024
LANES = 128
SUBLANES = 8

BF16 = jnp.bfloat16
F32 = jnp.float32

P_GATE = 0
P_SEG = N_BRANCH * D_MODEL
SEG_NAMES = ('rg_x', 'rg_gate', 'mq', 'mk', 'mv', 'mo', 'dq', 'dk', 'dv')
P_QLAT = P_SEG + 9 * W_BRANCH
P_KVLAT = P_QLAT + Q_RANK
P_SMALL = P_KVLAT + KV_RANK
P_WIDTH = P_SMALL + LANES
SM_MI = ROPE_A
SM_MF = ROPE_A + 2 * H_C


def _seg_block(name):
    return (P_SEG + SEG_NAMES.index(name) * W_BRANCH) // W_BRANCH


N_CTX_SEQ, T_CTX = 16, 256
N_LAT_SEQ, T_LAT = 8, 1024
PAST_LEN = 512
R_CTX = N_CTX_SEQ * T_CTX
R_LAT = N_LAT_SEQ * T_LAT
R_ALL = R_CTX + R_LAT


def _cparams(sem):
    return pltpu.CompilerParams(dimension_semantics=sem, vmem_limit_bytes=V7X_VMEM_LIMIT_BYTES)


def _mod_row_of_tile(i, tm):
    n_ctx_tiles = R_CTX // tm
    per_seq = T_LAT // tm
    return jnp.where(i < n_ctx_tiles, 0, 1 + (i - n_ctx_tiles) // per_seq)


def _pos_block_of_tile(i, tm):
    n_ctx_tiles = R_CTX // tm
    per_seq = T_LAT // tm
    return jnp.where(i < n_ctx_tiles, per_seq, (i - n_ctx_tiles) % per_seq)


def _mod_kernel(c_ref, w_ref, b_ref, o_ref):
    c = c_ref[...]
    s = (c * jax.nn.sigmoid(c)).astype(BF16)
    o_ref[...] = jnp.dot(s, w_ref[...].astype(BF16), preferred_element_type=F32) + b_ref[...]


def modulation(cond, w_mod, b_mod):
    m, d = cond.shape
    n = w_mod.shape[1]
    tn = 1536
    return pl.pallas_call(
        _mod_kernel,
        grid=(n // tn,),
        in_specs=[pl.BlockSpec((m, d), lambda j: (0, 0)),
                  pl.BlockSpec((d, tn), lambda j: (0, j)),
                  pl.BlockSpec((1, tn), lambda j: (0, j))],
        out_specs=pl.BlockSpec((m, tn), lambda j: (0, j)),
        out_shape=jax.ShapeDtypeStruct((m, n), F32),
        compiler_params=_cparams(("arbitrary",)),
        name="modulation",
    )(cond, w_mod, b_mod.reshape(1, n))


def _in_proj_kernel(x_ref, mod_ref, g_ref, w_ref, o_ref, xn_ref):
    @pl.when(pl.program_id(1) == 0)
    def _():
        x = x_ref[...]
        r = lax.rsqrt(jnp.mean(x * x, axis=-1, keepdims=True) + EPS)
        sh = mod_ref[0, 0:1, :]
        sc = mod_ref[0, 1:2, :]
        xn_ref[...] = ((x * r * g_ref[...]) * (1.0 + sc) + sh).astype(BF16)

    o_ref[...] = jnp.dot(xn_ref[...], w_ref[...], preferred_element_type=F32).astype(o_ref.dtype)


def in_proj(x, mod, g, w_p):
    tm, tn = 1024, 1024
    m, d = x.shape
    n = w_p.shape[1]
    return pl.pallas_call(
        _in_proj_kernel,
        grid=(m // tm, n // tn),
        in_specs=[pl.BlockSpec((tm, d), lambda i, j: (i, 0)),
                  pl.BlockSpec((1, 8, d), lambda i, j: (_mod_row_of_tile(i, tm), 0, 0)),
                  pl.BlockSpec((1, d), lambda i, j: (0, 0)),
                  pl.BlockSpec((d, tn), lambda i, j: (0, j))],
        out_specs=pl.BlockSpec((tm, tn), lambda i, j: (i, j)),
        out_shape=jax.ShapeDtypeStruct((m, n), BF16),
        scratch_shapes=[pltpu.VMEM((tm, d), BF16)],
        compiler_params=_cparams(("arbitrary", "arbitrary")),
        name="in_proj",
    )(x, mod, g.reshape(1, d), w_p)


def pack_w_in(w_in):
    o_rg = Q_RANK + KV_RANK + ROPE_A
    o_mi = o_rg + 6 * W_BRANCH
    o_dq = o_mi + 4 * H_C
    o_gate = o_dq + 3 * W_BRANCH
    pad = jnp.zeros((w_in.shape[0], LANES - ROPE_A - 4 * H_C), w_in.dtype)
    parts = [w_in[:, o_gate:], w_in[:, o_rg:o_mi], w_in[:, o_dq:o_gate], w_in[:, :Q_RANK + KV_RANK],
             w_in[:, Q_RANK + KV_RANK:o_rg], w_in[:, o_mi:o_dq], pad]
    return jnp.concatenate(parts, axis=1).astype(BF16)


def _rope_tables(rot_dim, lane_starts, tm):
    rows = T_LAT // GRID_W
    r, c = np.meshgrid(np.arange(rows, dtype=np.float32), np.arange(GRID_W, dtype=np.float32), indexing='ij')
    nf = rot_dim // 4
    inv = (np.float32(ROPE_BASE) ** (-np.arange(nf, dtype=np.float32) / np.float32(nf))).astype(np.float32)
    ang = np.stack([r.reshape(-1)[:, None] * inv, c.reshape(-1)[:, None] * inv], axis=1).astype(np.float32)
    cos, sin = np.cos(ang).astype(np.float32), np.sin(ang).astype(np.float32)
    tc = np.ones((T_LAT + tm, LANES), np.float32)
    ta = np.zeros((T_LAT + tm, LANES), np.float32)
    tb = np.zeros((T_LAT + tm, LANES), np.float32)
    for s0 in lane_starts:
        for a in range(2):
            lo = s0 + a * 2 * nf
            tc[:T_LAT, lo:lo + nf] = cos[:, a]
            tc[:T_LAT, lo + nf:lo + 2 * nf] = cos[:, a]
            ta[:T_LAT, lo:lo + nf] = -sin[:, a]
            tb[:T_LAT, lo + nf:lo + 2 * nf] = sin[:, a]
    return jnp.asarray(tc), jnp.asarray(ta), jnp.asarray(tb)


def _rope(x, c, sa, sb, half):
    return x * c + pltpu.roll(x, LANES - half, 1) * sa + pltpu.roll(x, half, 1) * sb


MLA_TM = 512
ATTN_TQ = 512
QK_A = NOPE_A + ROPE_A


def _mla_prep_kernel(*refs, has_q, norm_ckv):
    if has_q:
        (qlat_ref, gq_ref, wuq_ref, gqn_ref, kv_ref, sm_ref, gkv_ref, wkc_ref, wv_ref, gkn_ref,
         c_ref, sa_ref, sb_ref, q_o, k_o, v_o, ckv_o, kr_o) = refs
    else:
        (kv_ref, sm_ref, gkv_ref, wkc_ref, wv_ref, gkn_ref, c_ref, sa_ref, sb_ref, k_o, v_o) = refs
    c, sa, sb = c_ref[...], sa_ref[...], sb_ref[...]

    def heads(z, g, o_ref, scale):
        for h in range(H_A):
            s = z[:, h * LANES:(h + 1) * LANES]
            r = lax.rsqrt(jnp.sum(s * s, axis=-1, keepdims=True) * (1.0 / QK_A) + EPS)
            y = _rope(s * r * g, c, sa, sb, ROPE_A // 4)
            if scale != 1.0:
                y = y * scale
            o_ref[:, h * LANES:(h + 1) * LANES] = y.astype(o_ref.dtype)

    if has_q:
        ql = qlat_ref[...].astype(F32)
        qn = ql * lax.rsqrt(jnp.mean(ql * ql, axis=-1, keepdims=True) + EPS) * gq_ref[...]
        q = jnp.dot(qn.astype(BF16), wuq_ref[...], preferred_element_type=F32)
        heads(q, gqn_ref[...], q_o, MLA_SCALE)

    kv = kv_ref[...].astype(F32)
    if norm_ckv:
        ckv = kv * lax.rsqrt(jnp.mean(kv * kv, axis=-1, keepdims=True) + EPS) * gkv_ref[...]
    else:
        ckv = kv
    sm = sm_ref[...]
    ckv_b = ckv.astype(BF16)
    kin = jnp.concatenate([ckv_b, sm.astype(BF16)], axis=1)
    k = jnp.dot(kin, wkc_ref[...], preferred_element_type=F32)
    heads(k, gkn_ref[...], k_o, 1.0)
    v_o[...] = jnp.dot(ckv_b, wv_ref[...], preferred_element_type=F32).astype(v_o.dtype)
    if has_q:
        ckv_o[...] = ckv
        kr_o[...] = sm.astype(F32)


def pack_mla_weights(lp):
    wuq = lp['w_mla_uq'].reshape(Q_RANK, H_A, QK_A)
    wuq_p = jnp.pad(wuq, ((0, 0), (0, 0), (0, LANES - QK_A))).reshape(Q_RANK, H_A * LANES).astype(BF16)
    wukv = lp['w_mla_ukv'].reshape(KV_RANK, H_A, NOPE_A + V_A)
    wk = jnp.pad(wukv[:, :, :NOPE_A], ((0, 0), (0, 0), (0, LANES - NOPE_A))).reshape(KV_RANK, H_A * LANES)
    place = np.zeros((LANES, H_A, LANES), np.float32)
    for h in range(H_A):
        place[np.arange(ROPE_A), h, NOPE_A + np.arange(ROPE_A)] = 1.0
    wkc = jnp.concatenate([wk, jnp.asarray(place.reshape(LANES, H_A * LANES))], axis=0).astype(BF16)
    wv = wukv[:, :, NOPE_A:]
    wv_even = jnp.pad(wv, ((0, 0), (0, 0), (0, LANES - V_A)))
    wv_odd = jnp.pad(wv, ((0, 0), (0, 0), (LANES - V_A, 0)))
    odd = (np.arange(H_A) % 2 == 1)[None, :, None]
    wv_p = jnp.where(odd, wv_odd, wv_even).reshape(KV_RANK, H_A * LANES).astype(BF16)
    pad_g = lambda g: jnp.pad(g, (0, LANES - QK_A)).reshape(1, LANES)
    return dict(wuq=wuq_p, wkc=wkc, wv=wv_p, gqn=pad_g(lp['g_mla_qn']), gkn=pad_g(lp['g_mla_kn']),
                gq=lp['g_mla_qlat'].reshape(1, Q_RANK), gkv=lp['g_mla_kvlat'].reshape(1, KV_RANK))


def mla_prep(p, mw, tabs):
    tm = MLA_TM
    n = R_ALL // tm
    full = lambda shape: pl.BlockSpec(shape, lambda i: (0, 0))
    tab = pl.BlockSpec((tm, LANES), lambda i: (_pos_block_of_tile(i, tm), 0))
    wide = H_A * LANES
    out_shape = (jax.ShapeDtypeStruct((R_ALL, wide), BF16), jax.ShapeDtypeStruct((R_ALL, wide), BF16),
                 jax.ShapeDtypeStruct((R_ALL, wide), BF16), jax.ShapeDtypeStruct((R_ALL, KV_RANK), F32),
                 jax.ShapeDtypeStruct((R_ALL, LANES), F32))
    row = lambda w: pl.BlockSpec((tm, w), lambda i: (i, 0))
    return pl.pallas_call(
        functools.partial(_mla_prep_kernel, has_q=True, norm_ckv=True),
        grid=(n,),
        in_specs=[pl.BlockSpec((tm, Q_RANK), lambda i: (i, P_QLAT // Q_RANK)), full((1, Q_RANK)),
                  full((Q_RANK, wide)), full((1, LANES)),
                  pl.BlockSpec((tm, KV_RANK), lambda i: (i, P_KVLAT // KV_RANK)),
                  pl.BlockSpec((tm, LANES), lambda i: (i, P_SMALL // LANES)), full((1, KV_RANK)),
                  full((2 * LANES, wide)), full((KV_RANK, wide)), full((1, LANES)), tab, tab, tab],
        out_specs=(row(wide), row(wide), row(wide), row(KV_RANK), row(LANES)),
        out_shape=out_shape,
        compiler_params=_cparams(("arbitrary",)),
        name="mla_prep",
    )(p, mw['gq'], mw['wuq'], mw['gqn'], p, p, mw['gkv'], mw['wkc'], mw['wv'], mw['gkn'], *tabs)


def mla_prep_cache(ckv_c, kr_c, mw, tabs):
    tm = MLA_TM
    r = ckv_c.shape[0]
    full = lambda shape: pl.BlockSpec(shape, lambda i: (0, 0))
    ident = pl.BlockSpec((tm, LANES), lambda i: (T_LAT // tm, 0))
    wide = H_A * LANES
    row = lambda w: pl.BlockSpec((tm, w), lambda i: (i, 0))
    return pl.pallas_call(
        functools.partial(_mla_prep_kernel, has_q=False, norm_ckv=False),
        grid=(r // tm,),
        in_specs=[row(KV_RANK), row(LANES), full((1, KV_RANK)), full((2 * LANES, wide)), full((KV_RANK, wide)),
                  full((1, LANES)), ident, ident, ident],
        out_specs=(row(wide), row(wide)),
        out_shape=(jax.ShapeDtypeStruct((r, wide), BF16), jax.ShapeDtypeStruct((r, wide), BF16)),
        compiler_params=_cparams(("arbitrary",)),
        name="mla_prep_cache",
    )(ckv_c, kr_c, mw['gkv'], mw['wkc'], mw['wv'], mw['gkn'], *tabs)


_NT = (((1,), (1,)), ((), ()))


def _mla_attn_kernel(*refs, has_cache):
    if has_cache:
        q_ref, ko_ref, vo_ref, kc_ref, vc_ref, o_ref = refs
    else:
        q_ref, ko_ref, vo_ref, o_ref = refs
    acc = jnp.zeros(o_ref.shape, F32)
    for hh in range(2):
        sl = slice(hh * LANES, (hh + 1) * LANES)
        q = q_ref[:, sl]
        s_o = lax.dot_general(q, ko_ref[:, sl], _NT, preferred_element_type=F32)
        m = jnp.max(s_o, axis=-1, keepdims=True)
        if has_cache:
            s_c = lax.dot_general(q, kc_ref[:, sl], _NT, preferred_element_type=F32)
            m = jnp.maximum(m, jnp.max(s_c, axis=-1, keepdims=True))
        e_o = jnp.exp(s_o - m)
        l = jnp.sum(e_o, axis=-1, keepdims=True)
        pv = jnp.dot(e_o.astype(BF16), vo_ref[:, sl], preferred_element_type=F32)
        if has_cache:
            e_c = jnp.exp(s_c - m)
            l = l + jnp.sum(e_c, axis=-1, keepdims=True)
            pv = pv + jnp.dot(e_c.astype(BF16), vc_ref[:, sl], preferred_element_type=F32)
        acc = acc + pv * (1.0 / l)
    o_ref[...] = acc.astype(o_ref.dtype)


def mla_attention(q, k, v, kc, vc, *, row0, n_seq, t_seq, tq):
    has_cache = kc is not None
    n_pair = H_A // 2
    nq = t_seq // tq
    qb0, kb0 = row0 // tq, row0 // t_seq
    in_specs = [pl.BlockSpec((tq, 2 * LANES), lambda s, p, i: (qb0 + s * nq + i, p)),
                pl.BlockSpec((t_seq, 2 * LANES), lambda s, p, i: (kb0 + s, p)),
                pl.BlockSpec((t_seq, 2 * LANES), lambda s, p, i: (kb0 + s, p))]
    args = [q, k, v]
    if has_cache:
        in_specs += [pl.BlockSpec((PAST_LEN, 2 * LANES), lambda s, p, i: (s, p)),
                     pl.BlockSpec((PAST_LEN, 2 * LANES), lambda s, p, i: (s, p))]
        args += [kc, vc]
    return pl.pallas_call(
        functools.partial(_mla_attn_kernel, has_cache=has_cache),
        grid=(n_seq, n_pair, nq),
        in_specs=in_specs,
        out_specs=pl.BlockSpec((tq, LANES), lambda s, p, i: (s * nq + i, p)),
        out_shape=jax.ShapeDtypeStruct((n_seq * t_seq, W_BRANCH), BF16),
        compiler_params=_cparams(("arbitrary", "arbitrary", "arbitrary")),
        name="mla_attention",
    )(*args)


DIFF_TM = 512


def _diff_prep_kernel(dq_ref, dk_ref, gq_ref, gk_ref, c_ref, sa_ref, sb_ref, q_o, ko_o, kp_o):
    c, sa, sb = c_ref[...], sa_ref[...], sb_ref[...]
    lane = lax.broadcasted_iota(jnp.int32, (1, LANES), 1)
    lo = lane < DH_D

    def normed(x, g):
        x2 = x * x
        s_lo = jnp.sum(jnp.where(lo, x2, 0.0), axis=-1, keepdims=True)
        s_hi = jnp.sum(jnp.where(lo, 0.0, x2), axis=-1, keepdims=True)
        ms = jnp.where(lo, s_lo, s_hi) * (1.0 / DH_D)
        return x * lax.rsqrt(ms + EPS) * g

    for j in range(W_BRANCH // LANES):
        sl = slice(j * LANES, (j + 1) * LANES)
        qn = normed(dq_ref[:, sl].astype(F32), gq_ref[...])
        q_o[:, sl] = (_rope(qn, c, sa, sb, DH_D // 4) * DIFF_SCALE).astype(q_o.dtype)
        kn = normed(dk_ref[:, sl].astype(F32), gk_ref[...])
        kp_o[:, sl] = kn
        ko_o[:, sl] = _rope(kn, c, sa, sb, DH_D // 4).astype(ko_o.dtype)


def diff_prep(p, lp, tabs):
    tm = DIFF_TM
    tile2 = lambda g: jnp.concatenate([g, g]).reshape(1, LANES)
    full = pl.BlockSpec((1, LANES), lambda i: (0, 0))
    tab = pl.BlockSpec((tm, LANES), lambda i: (_pos_block_of_tile(i, tm), 0))
    row = pl.BlockSpec((tm, W_BRANCH), lambda i: (i, 0))
    dq_b, dk_b = _seg_block('dq'), _seg_block('dk')
    return pl.pallas_call(
        _diff_prep_kernel,
        grid=(R_ALL // tm,),
        in_specs=[pl.BlockSpec((tm, W_BRANCH), lambda i: (i, dq_b)), pl.BlockSpec((tm, W_BRANCH), lambda i: (i, dk_b)),
                  full, full, tab, tab, tab],
        out_specs=(row, row, row),
        out_shape=(jax.ShapeDtypeStruct((R_ALL, W_BRANCH), BF16), jax.ShapeDtypeStruct((R_ALL, W_BRANCH), BF16),
                   jax.ShapeDtypeStruct((R_ALL, W_BRANCH), F32)),
        compiler_params=_cparams(("arbitrary",)),
        name="diff_prep",
    )(p, p, tile2(lp['g_diff_qn']), tile2(lp['g_diff_kn']), *tabs)


def _diff_attn_kernel(*refs, has_cache, lambda_init):
    if has_cache:
        dl_ref, q1_ref, q2_ref, k1_ref, k2_ref, v_ref, k1c_ref, k2c_ref, vc_ref, g_ref, o_ref = refs
    else:
        dl_ref, q1_ref, q2_ref, k1_ref, k2_ref, v_ref, g_ref, o_ref = refs
    half = pl.program_id(1) % 2
    lane = lax.broadcasted_iota(jnp.int32, (1, LANES), 1)
    mine = (lane // DH_D) == half
    dl = dl_ref[...]
    lam = (jnp.exp(jnp.sum(dl[0:1] * dl[1:2], axis=-1, keepdims=True))
           - jnp.exp(jnp.sum(dl[2:3] * dl[3:4], axis=-1, keepdims=True)) + lambda_init)

    def attend(q_ref, k_ref, kc_ref):
        q = jnp.where(mine, q_ref[...], jnp.zeros((), q_ref.dtype))
        s_o = lax.dot_general(q, k_ref[...].astype(BF16), _NT, preferred_element_type=F32)
        m = jnp.max(s_o, axis=-1, keepdims=True)
        if has_cache:
            s_c = lax.dot_general(q, kc_ref[...].astype(BF16), _NT, preferred_element_type=F32)
            m = jnp.maximum(m, jnp.max(s_c, axis=-1, keepdims=True))
        e_o = jnp.exp(s_o - m)
        l = jnp.sum(e_o, axis=-1, keepdims=True)
        pv = jnp.dot(e_o.astype(BF16), v_ref[...].astype(BF16), preferred_element_type=F32)
        if has_cache:
            e_c = jnp.exp(s_c - m)
            l = l + jnp.sum(e_c, axis=-1, keepdims=True)
            pv = pv + jnp.dot(e_c.astype(BF16), vc_ref[...].astype(BF16), preferred_element_type=F32)
        return pv * (1.0 / l)

    y = attend(q1_ref, k1_ref, k1c_ref if has_cache else None) - lam * attend(q2_ref, k2_ref, k2c_ref if has_cache else None)
    r = lax.rsqrt(jnp.mean(y * y, axis=-1, keepdims=True) + EPS)
    o_ref[...] = ((y * r * g_ref[...]) * (1.0 - lambda_init)).astype(o_ref.dtype)


def diff_attention(dl, qd, kd, p, kc, vc, g_sub, *, row0, n_seq, t_seq, tq, lambda_init):
    has_cache = kc is not None
    nq = t_seq // tq
    qb0, kb0 = row0 // tq, row0 // t_seq
    dv_b = _seg_block('dv') * (W_BRANCH // LANES)
    in_specs = [pl.BlockSpec((4, DH_D), lambda s, h, i: (0, 0)),
                pl.BlockSpec((tq, LANES), lambda s, h, i: (qb0 + s * nq + i, h // 2)),
                pl.BlockSpec((tq, LANES), lambda s, h, i: (qb0 + s * nq + i, 2 + h // 2)),
                pl.BlockSpec((t_seq, LANES), lambda s, h, i: (kb0 + s, h // 2)),
                pl.BlockSpec((t_seq, LANES), lambda s, h, i: (kb0 + s, 2 + h // 2)),
                pl.BlockSpec((t_seq, LANES), lambda s, h, i: (kb0 + s, dv_b + h))]
    args = [dl, qd, qd, kd, kd, p]
    if has_cache:
        in_specs += [pl.BlockSpec((PAST_LEN, LANES), lambda s, h, i: (s, h // 2)),
                     pl.BlockSpec((PAST_LEN, LANES), lambda s, h, i: (s, 2 + h // 2)),
                     pl.BlockSpec((PAST_LEN, LANES), lambda s, h, i: (s, h))]
        args += [kc, kc, vc]
    in_specs.append(pl.BlockSpec((1, LANES), lambda s, h, i: (0, 0)))
    args.append(g_sub.reshape(1, LANES))
    return pl.pallas_call(
        functools.partial(_diff_attn_kernel, has_cache=has_cache, lambda_init=lambda_init),
        grid=(n_seq, H_D, nq),
        in_specs=in_specs,
        out_specs=pl.BlockSpec((tq, LANES), lambda s, h, i: (s * nq + i, h)),
        out_shape=jax.ShapeDtypeStruct((n_seq * t_seq, W_BRANCH), BF16),
        compiler_params=_cparams(("arbitrary", "arbitrary", "arbitrary")),
        name="diff_attention",
    )(*args)


def _neg_expm1(z):
    series = -z * (1.0 + z * (0.5 + z * (1.0 / 6.0 + z * (1.0 / 24.0 + z * (1.0 / 120.0 + z * (1.0 / 720.0))))))
    return jnp.where(z > -0.25, series, 1.0 - jnp.exp(z))


def _softplus(z):
    return jnp.maximum(z, 0.0) + jnp.log(1.0 + jnp.exp(-jnp.abs(z)))


def _gelu_tanh(x):
    return 0.5 * x * (1.0 + jnp.tanh(math.sqrt(2.0 / math.pi) * (x + 0.044715 * (x * x * x))))


def _rglru_kernel(x_ref, gate_ref, wc_ref, bc_ref, wg_ref, bg_ref, lam_ref, h0_ref, y_ref, st_ref,
                  a_s, u_s, h_s, *, t_seq):
    t = t_seq
    x = x_ref[...].astype(F32)
    row = lax.broadcasted_iota(jnp.int32, (t, W_B), 0)
    wc = wc_ref[...]
    xc = (wc[0:1] * jnp.where(row >= 2, pltpu.roll(x, 2, 0), 0.0)
          + wc[1:2] * jnp.where(row >= 1, pltpu.roll(x, 1, 0), 0.0)
          + wc[2:3] * x
          + wc[3:4] * jnp.where(row < t - 1, pltpu.roll(x, t - 1, 0), 0.0)
          + bc_ref[...])
    gates = jnp.dot(xc.astype(BF16), wg_ref[...], preferred_element_type=F32) + bg_ref[...]
    r8 = row % SUBLANES
    nblk = t // SUBLANES
    hsum = None
    for d in range(2):
        rg = jax.nn.sigmoid(gates[:, (2 * d) * W_B:(2 * d + 1) * W_B])
        ig = jax.nn.sigmoid(gates[:, (2 * d + 1) * W_B:(2 * d + 2) * W_B])
        log_a = -RG_C * rg * _softplus(-lam_ref[d:d + 1, :])
        a = jnp.exp(log_a)
        u = jnp.sqrt(_neg_expm1(2.0 * log_a)) * (ig * xc)
        for step in (1, 2, 4):
            if d == 0:
                valid = r8 >= step
                a_sh, u_sh = pltpu.roll(a, step, 0), pltpu.roll(u, step, 0)
            else:
                valid = r8 + step < SUBLANES
                a_sh, u_sh = pltpu.roll(a, t - step, 0), pltpu.roll(u, t - step, 0)
            u = jnp.where(valid, a * u_sh + u, u)
            a = jnp.where(valid, a * a_sh, a)
        a_s[...] = a
        u_s[...] = u
        h0 = h0_ref[0, d:d + 1, :]

        def body(k, carry, d=d):
            blk = k if d == 0 else nblk - 1 - k
            rows = pl.ds(pl.multiple_of(blk * SUBLANES, SUBLANES), SUBLANES)
            h = a_s[rows, :] * carry + u_s[rows, :]
            h_s[rows, :] = h
            return h[SUBLANES - 1:SUBLANES, :] if d == 0 else h[0:1, :]

        last = lax.fori_loop(0, nblk, body, h0, unroll=8)
        st_ref[0, d:d + 1, :] = last
        hsum = h_s[...] if d == 0 else hsum + h_s[...]
    y_ref[...] = (_gelu_tanh(gate_ref[...].astype(F32)) * hsum).astype(y_ref.dtype)


def pack_rglru_weights(lp):
    def blockdiag(w):
        eye = jnp.eye(NB_B, dtype=w.dtype)
        return jnp.einsum('ncd,nm->ncmd', w, eye).reshape(W_B, W_B)
    wg = jnp.concatenate([blockdiag(lp['w_rg_a'][0]), blockdiag(lp['w_rg_x'][0]),
                          blockdiag(lp['w_rg_a'][1]), blockdiag(lp['w_rg_x'][1])], axis=1).astype(BF16)
    bg = jnp.concatenate([lp['b_rg_a'][0], lp['b_rg_x'][0], lp['b_rg_a'][1], lp['b_rg_x'][1]]).reshape(1, 4 * W_B)
    return dict(wg=wg, bg=bg, wc=lp['w_conv_rg'], bc=lp['b_conv_rg'].reshape(1, W_B), lam=lp['rg_lambda'])


def rglru(p, rw, h0, *, row0, n_seq, t_seq):
    rb0 = row0 // t_seq
    xb, gb = _seg_block('rg_x'), _seg_block('rg_gate')
    full = lambda shape: pl.BlockSpec(shape, lambda s: tuple(0 for _ in shape))
    in_specs = [pl.BlockSpec((t_seq, W_B), lambda s: (rb0 + s, xb)),
                pl.BlockSpec((t_seq, W_B), lambda s: (rb0 + s, gb)),
                full((CONV_W, W_B)), full((1, W_B)), full((W_B, 4 * W_B)), full((1, 4 * W_B)), full((2, W_B)),
                pl.BlockSpec((1, 2, W_B), lambda s: (s, 0, 0))]
    args = [p, p, rw['wc'], rw['bc'], rw['wg'], rw['bg'], rw['lam'], h0]
    return pl.pallas_call(
        functools.partial(_rglru_kernel, t_seq=t_seq),
        grid=(n_seq,),
        in_specs=in_specs,
        out_specs=(pl.BlockSpec((t_seq, W_B), lambda s: (s, 0)), pl.BlockSpec((1, 2, W_B), lambda s: (s, 0, 0))),
        out_shape=(jax.ShapeDtypeStruct((n_seq * t_seq, W_B), BF16), jax.ShapeDtypeStruct((n_seq, 2, W_B), F32)),
        scratch_shapes=[pltpu.VMEM((t_seq, W_B), F32)] * 3,
        compiler_params=_cparams(("arbitrary",)),
        name="rglru",
    )(*args)


def _dot_split(a, b_bf16):
    hi = a.astype(BF16)
    lo = (a - hi.astype(F32)).astype(BF16)
    return (jnp.dot(hi, b_bf16, preferred_element_type=F32) + jnp.dot(lo, b_bf16, preferred_element_type=F32))


def _log_sigmoid(z):
    return jnp.minimum(z, 0.0) - jnp.log(1.0 + jnp.exp(-jnp.abs(z)))


_TN = (((0,), (0,)), ((), ()))


def _mlstm_kernel(q_ref, k_ref, v_ref, o_ref, sm_ref, bias_ref, g_ref, c0_ref, m0_ref,
                  y_ref, c_out, m_out, hm_s, c_s, *, t_seq):
    L = ML_CHUNK
    nchunk = t_seq // L
    scale = DH_C ** -0.5
    ri = lax.broadcasted_iota(jnp.int32, (L, L), 0)
    ci = lax.broadcasted_iota(jnp.int32, (L, L), 1)
    lane1 = lax.broadcasted_iota(jnp.int32, (L, LANES), 1)
    ones_col = jnp.where(lane1 == 0, 1.0, 0.0).astype(BF16)
    bias = bias_ref[...]

    for d in range(2):
        causal = (ci <= ri) if d == 0 else (ci >= ri)
        tri = jnp.where(causal, 1.0, 0.0).astype(BF16)
        tri_t = jnp.where((ri <= ci) if d == 0 else (ri >= ci), 1.0, 0.0).astype(BF16)
        c_s[...] = c0_ref[0, d]
        m_init = tuple(m0_ref[0, :, d * H_C + h:d * H_C + h + 1] for h in range(H_C))

        def chunk(kk, ms, d=d, causal=causal, tri=tri, tri_t=tri_t):
            cidx = kk if d == 0 else nchunk - 1 - kk
            rows = pl.ds(pl.multiple_of(cidx * L, L), L)
            gsm = sm_ref[rows, :].astype(F32) + bias
            lf_all = _log_sigmoid(gsm)
            cum_cols = _dot_split_left(tri, lf_all)
            g_t = gsm.T
            cum_rows = _dot_split(lf_all.T, tri_t)
            new_ms = []
            for h in range(H_C):
                jl, jf = SM_MI + d * H_C + h, SM_MF + d * H_C + h
                cum_c = cum_cols[:, jf:jf + 1]
                cum_r = cum_rows[jf:jf + 1, :]
                li_c = gsm[:, jl:jl + 1]
                li_r = g_t[jl:jl + 1, :]
                m_mem = ms[h]
                log_d = jnp.where(causal, cum_c - cum_r + li_r, -jnp.inf)
                inter = cum_c + m_mem
                m_row = jnp.maximum(inter, jnp.max(log_d, axis=-1, keepdims=True))
                dmat = jnp.exp(log_d - m_row)
                w_inter = jnp.exp(inter - m_row)
                sl = slice(h * DH_C, (h + 1) * DH_C)
                qh, kh, vh = q_ref[rows, sl], k_ref[rows, sl], v_ref[rows, sl]
                s = lax.dot_general(qh, kh, _NT, preferred_element_type=F32) * (scale * dmat)
                sv = jnp.dot(s.astype(BF16), vh, preferred_element_type=F32)
                qc = jnp.dot(qh, c_s[h].astype(BF16), preferred_element_type=F32) * scale
                num = sv + qc[:, :DH_C] * w_inter
                den = jnp.sum(s, axis=-1, keepdims=True) + w_inter * qc[:, DH_C:DH_C + 1]
                den = jnp.maximum(jnp.abs(den), jnp.exp(-m_row))
                h_out = num * (1.0 / den)
                if d == 0:
                    hm_s[rows, sl] = h_out
                else:
                    hm_s[rows, sl] = hm_s[rows, sl] + h_out
                last = cum_c[L - 1:L, :] if d == 0 else cum_c[0:1, :]
                w_s = last - cum_c + li_c
                m_new = jnp.maximum(last + m_mem, jnp.max(w_s, axis=0, keepdims=True))
                decay = jnp.exp(last + m_mem - m_new)
                ws = jnp.exp(w_s - m_new)
                kw_t = (kh.astype(F32) * ws).T.astype(BF16)
                v_aug = jnp.concatenate([vh, ones_col], axis=1)
                c_s[h] = decay * c_s[h] + jnp.dot(kw_t, v_aug, preferred_element_type=F32)
                new_ms.append(m_new)
            return tuple(new_ms)

        m_fin = lax.fori_loop(0, nchunk, chunk, m_init)
        c_out[0, d] = c_s[...]
        for h in range(H_C):
            m_out[0, :, d * H_C + h:d * H_C + h + 1] = m_fin[h]

    for h in range(H_C):
        sl = slice(h * DH_C, (h + 1) * DH_C)
        hm = hm_s[:, sl]
        r = lax.rsqrt(jnp.mean(hm * hm, axis=-1, keepdims=True) + EPS)
        y_ref[:, sl] = (jax.nn.sigmoid(o_ref[:, sl].astype(F32)) * (hm * r * g_ref[...])).astype(y_ref.dtype)


def _dot_split_left(a_bf16, b):
    hi = b.astype(BF16)
    lo = (b - hi.astype(F32)).astype(BF16)
    return (jnp.dot(a_bf16, hi, preferred_element_type=F32) + jnp.dot(a_bf16, lo, preferred_element_type=F32))


def mlstm(p, bias_sm, g_out, c0_aug, m0, *, row0, n_seq, t_seq):
    rb0 = row0 // t_seq
    seg = lambda nm: pl.BlockSpec((t_seq, W_BRANCH), lambda s, b=_seg_block(nm): (rb0 + s, b))
    full = lambda shape: pl.BlockSpec(shape, lambda s: tuple(0 for _ in shape))
    st_spec = pl.BlockSpec((1, 2, H_C, DH_C, 2 * DH_C), lambda s: (s, 0, 0, 0, 0))
    m_spec = pl.BlockSpec((1, 1, 2 * H_C), lambda s: (s, 0, 0))
    in_specs = [seg('mq'), seg('mk'), seg('mv'), seg('mo'),
                pl.BlockSpec((t_seq, LANES), lambda s: (rb0 + s, P_SMALL // LANES)),
                full((1, LANES)), full((1, DH_C)), st_spec, m_spec]
    args = [p, p, p, p, p, bias_sm, g_out.reshape(1, DH_C), c0_aug, m0]
    return pl.pallas_call(
        functools.partial(_mlstm_kernel, t_seq=t_seq),
        grid=(n_seq,),
        in_specs=in_specs,
        out_specs=(pl.BlockSpec((t_seq, W_BRANCH), lambda s: (s, 0)), st_spec, m_spec),
        out_shape=(jax.ShapeDtypeStruct((n_seq * t_seq, W_BRANCH), BF16),
                   jax.ShapeDtypeStruct((n_seq, 2, H_C, DH_C, 2 * DH_C), F32),
                   jax.ShapeDtypeStruct((n_seq, 1, 2 * H_C), F32)),
        scratch_shapes=[pltpu.VMEM((t_seq, W_BRANCH), F32), pltpu.VMEM((H_C, DH_C, 2 * DH_C), F32)],
        compiler_params=_cparams(("arbitrary",)),
        name="mlstm",
    )(*args)


MERGE_TM = 512


def _merge_kernel(*refs):
    ctx_refs, lat_refs, gate_refs = refs[0:4], refs[4:8], refs[8:12]
    x_ref, mod_ref, wbr_ref, wout_ref, gn_ref, xo_ref, xn_ref = refs[12:19]
    xn3_ref = refs[19] if len(refs) > 19 else None
    is_ctx = pl.program_id(0) < R_CTX // MERGE_TM
    merged = None
    for g in range(N_BRANCH):
        yg = jnp.where(is_ctx, ctx_refs[g][...], lat_refs[g][...])
        pg = jnp.dot(yg, wbr_ref[g], preferred_element_type=F32)
        term = jax.nn.sigmoid(gate_refs[g][...].astype(F32)) * pg
        merged = term if merged is None else merged + term
    y = jnp.dot(merged.astype(BF16), wout_ref[...], preferred_element_type=F32)
    x = x_ref[...] + mod_ref[0, 2:3, :] * y
    xo_ref[...] = x
    r = lax.rsqrt(jnp.mean(x * x, axis=-1, keepdims=True) + EPS)
    xn = (x * r * gn_ref[...]) * (1.0 + mod_ref[0, 4:5, :]) + mod_ref[0, 3:4, :]
    xn_ref[...] = xn.astype(xn_ref.dtype)
    if xn3_ref is not None:
        _rows_to_tiles(xn3_ref, xn.astype(BF16))


def merge(ys_ctx, ys_lat, p, x, mod, wbr, wout, g_ffn, rows_as_tiles):
    tm = MERGE_TM
    n_ctx_tiles = R_CTX // tm
    br_ctx = pl.BlockSpec((tm, W_BRANCH), lambda i: (jnp.minimum(i, n_ctx_tiles - 1), 0))
    br_lat = pl.BlockSpec((tm, W_BRANCH), lambda i: (jnp.maximum(i - n_ctx_tiles, 0), 0))
    gate = lambda g: pl.BlockSpec((tm, D_MODEL), lambda i, g=g: (i, g))
    row = pl.BlockSpec((tm, D_MODEL), lambda i: (i, 0))
    out_specs = [row, row]
    out_shape = [jax.ShapeDtypeStruct((R_ALL, D_MODEL), F32), jax.ShapeDtypeStruct((R_ALL, D_MODEL), BF16)]
    if rows_as_tiles:
        out_specs.append(pl.BlockSpec((tm, ROW_SUB, LANES), lambda i: (i, 0, 0)))
        out_shape.append(jax.ShapeDtypeStruct((R_ALL, ROW_SUB, LANES), F32))
    return pl.pallas_call(
        _merge_kernel,
        grid=(R_ALL // tm,),
        in_specs=[br_ctx] * N_BRANCH + [br_lat] * N_BRANCH + [gate(0), gate(1), gate(2), gate(3), row,
                  pl.BlockSpec((1, 8, D_MODEL), lambda i: (_mod_row_of_tile(i, tm), 0, 0)),
                  pl.BlockSpec((N_BRANCH, W_BRANCH, D_MODEL), lambda i: (0, 0, 0)),
                  pl.BlockSpec((D_MODEL, D_MODEL), lambda i: (0, 0)),
                  pl.BlockSpec((1, D_MODEL), lambda i: (0, 0))],
        out_specs=tuple(out_specs),
        out_shape=tuple(out_shape),
        compiler_params=_cparams(("arbitrary",)),
        name="merge",
    )(*ys_ctx, *ys_lat, p, p, p, p, x, mod, wbr, wout, g_ffn.reshape(1, D_MODEL))


def _new_expert(te_ref, i):
    return jnp.logical_or(i == 0, te_ref[i] != te_ref[jnp.maximum(i - 1, 0)])


def _ffn_up_kernel(te_ref, nt_ref, x_ref, wg_ref, wu_ref, h_ref, wgb_ref, wub_ref):
    i = pl.program_id(1)

    @pl.when(_new_expert(te_ref, i))
    def _():
        wgb_ref[...] = wg_ref[0].astype(BF16)
        wub_ref[...] = wu_ref[0].astype(BF16)

    @pl.when(i < nt_ref[0])
    def _():
        x = x_ref[...]
        g = jnp.dot(x, wgb_ref[...], preferred_element_type=F32)
        u = jnp.dot(x, wub_ref[...], preferred_element_type=F32)
        h_ref[...] = (g * jax.nn.sigmoid(g) * u).astype(h_ref.dtype)

    @pl.when(i >= nt_ref[0])
    def _():
        h_ref[...] = jnp.zeros(h_ref.shape, h_ref.dtype)


def ffn_up(tile_expert, n_tiles, xs, wg, wu, tm, tf, weight_buffers):
    r, d = xs.shape
    f = wg.shape[2]
    w_spec = pl.BlockSpec((1, d, tf), lambda j, i, te, nt: (te[i], 0, j), pipeline_mode=pl.Buffered(weight_buffers))
    return pl.pallas_call(
        _ffn_up_kernel,
        grid_spec=pltpu.PrefetchScalarGridSpec(
            num_scalar_prefetch=2,
            grid=(f // tf, r // tm),
            in_specs=[pl.BlockSpec((tm, d), lambda j, i, te, nt: (i, 0)), w_spec, w_spec],
            out_specs=pl.BlockSpec((tm, tf), lambda j, i, te, nt: (i, j)),
            scratch_shapes=[pltpu.VMEM((d, tf), BF16), pltpu.VMEM((d, tf), BF16)]),
        out_shape=jax.ShapeDtypeStruct((r, f), BF16),
        compiler_params=_cparams(("arbitrary", "arbitrary")),
        name="ffn_up",
    )(tile_expert, n_tiles, xs, wg, wu)


def _ffn_down_kernel(te_ref, nt_ref, h_ref, wd_ref, y_ref, wdb_ref):
    i = pl.program_id(0)

    @pl.when(_new_expert(te_ref, i))
    def _():
        wdb_ref[...] = wd_ref[0].astype(BF16)

    @pl.when(i < nt_ref[0])
    def _():
        _rows_to_tiles(y_ref, jnp.dot(h_ref[...], wdb_ref[...], preferred_element_type=F32))

    @pl.when(i >= nt_ref[0])
    def _():
        y_ref[...] = jnp.zeros(y_ref.shape, y_ref.dtype)


def ffn_down(tile_expert, n_tiles, h, wd, tm):
    r, f = h.shape
    d = wd.shape[2]
    return pl.pallas_call(
        _ffn_down_kernel,
        grid_spec=pltpu.PrefetchScalarGridSpec(
            num_scalar_prefetch=2,
            grid=(r // tm,),
            in_specs=[pl.BlockSpec((tm, f), lambda i, te, nt: (i, 0)),
                      pl.BlockSpec((1, f, d), lambda i, te, nt: (te[i], 0, 0), pipeline_mode=pl.Buffered(1))],
            out_specs=pl.BlockSpec((tm, ROW_SUB, LANES), lambda i, te, nt: (i, 0, 0)),
            scratch_shapes=[pltpu.VMEM((f, d), BF16)]),
        out_shape=jax.ShapeDtypeStruct((r, ROW_SUB, LANES), F32),
        compiler_params=_cparams(("arbitrary",)),
        name="ffn_down",
    )(tile_expert, n_tiles, h, wd)


def _ffn_down_res_kernel(h_ref, wd_ref, x_ref, mod_ref, y_ref, wdb_ref):
    @pl.when(pl.program_id(0) == 0)
    def _():
        wdb_ref[...] = wd_ref[...].astype(BF16)

    y = jnp.dot(h_ref[...], wdb_ref[...], preferred_element_type=F32)
    y_ref[...] = x_ref[...] + mod_ref[0, 5:6, :] * y


def ffn_down_residual(h, wd, x, mod):
    tm = 1024
    r, f = h.shape
    d = wd.shape[1]
    return pl.pallas_call(
        _ffn_down_res_kernel,
        grid=(r // tm,),
        in_specs=[pl.BlockSpec((tm, f), lambda i: (i, 0)),
                  pl.BlockSpec((f, d), lambda i: (0, 0), pipeline_mode=pl.Buffered(1)),
                  pl.BlockSpec((tm, d), lambda i: (i, 0)),
                  pl.BlockSpec((1, 8, d), lambda i: (_mod_row_of_tile(i, tm), 0, 0))],
        out_specs=pl.BlockSpec((tm, d), lambda i: (i, 0)),
        out_shape=jax.ShapeDtypeStruct((r, d), F32),
        scratch_shapes=[pltpu.VMEM((f, d), BF16)],
        compiler_params=_cparams(("arbitrary",)),
        name="ffn_down_residual",
    )(h, wd, x, mod)


def dense_swiglu_residual(xn, x, mod, wg, wu, wd):
    t = xn.shape[0]
    tm = 1024
    n_tiles = t // tm
    te = jnp.zeros((n_tiles,), jnp.int32)
    nt = jnp.full((1,), n_tiles, jnp.int32)
    h = ffn_up(te, nt, xn, wg[None], wu[None], tm=tm, tf=wg.shape[1] // 2, weight_buffers=1)
    return ffn_down_residual(h, wd, x, mod)


MOE_TM = 1024
DISPATCH_TM = 512


def _moe_routing(logits, tm):
    t = logits.shape[0]
    n_assign = t * TOP_K
    top_v, top_i = lax.top_k(logits, TOP_K)
    gate = jax.nn.softmax(top_v, axis=-1)
    flat_e = top_i.reshape(-1).astype(jnp.int32)
    onehot = (flat_e[:, None] == jnp.arange(N_EXP, dtype=jnp.int32)[None, :])
    blk = LANES
    oh = onehot.astype(F32).reshape(n_assign // blk, blk, N_EXP)
    tril = jnp.tril(jnp.ones((blk, blk), F32))
    within = jnp.einsum('ij,bjk->bik', tril, oh)
    blk_tot = within[:, -1, :]
    blk_off = jnp.cumsum(blk_tot, axis=0) - blk_tot
    csum = (within + blk_off[:, None, :]).reshape(n_assign, N_EXP)
    rank = jnp.sum(jnp.where(onehot, csum - 1.0, 0.0), axis=1).astype(jnp.int32)
    counts = csum[-1].astype(jnp.int32)
    padded = (counts + tm - 1) // tm * tm
    grp_start = jnp.cumsum(padded) - padded
    raw_start = jnp.cumsum(counts) - counts
    slot_of_assign = jnp.sum(jnp.where(onehot, grp_start[None, :], 0), axis=1) + rank

    r_max = n_assign + N_EXP * tm
    tile_start = jnp.arange(r_max // tm, dtype=jnp.int32) * tm
    tile_expert = jnp.sum((tile_start[:, None] >= (grp_start + padded)[None, :]).astype(jnp.int32), axis=1)
    tile_expert = jnp.minimum(tile_expert, N_EXP - 1).astype(jnp.int32)
    n_tiles = (jnp.sum(padded) // tm).astype(jnp.int32).reshape(1)

    order = jnp.argsort(flat_e, stable=True).astype(jnp.int32)
    e_slot = jnp.repeat(tile_expert, tm)
    j = jnp.arange(r_max, dtype=jnp.int32) - grp_start[e_slot]
    src = jnp.clip(raw_start[e_slot] + j, 0, n_assign - 1)
    tok_of_slot = jnp.where(j < counts[e_slot], order[src] // TOP_K, 0)
    return gate, slot_of_assign, tok_of_slot, tile_expert, n_tiles


ROW_SUB = D_MODEL // LANES


def _rows_to_tiles(o3_ref, x):
    for j in range(ROW_SUB):
        o3_ref[:, j, :] = x[:, j * LANES:(j + 1) * LANES].astype(o3_ref.dtype)


def _start_row_gather(idx_ref, src_ref, dst, sem, n_rows):
    def body(q, carry):
        for u in range(2):
            r = 2 * q + u
            pltpu.make_async_copy(src_ref.at[idx_ref[0, 0, r]], dst.at[r], sem).start(priority=u)
        return carry

    lax.fori_loop(0, n_rows // 2, body, 0, unroll=4)


def _wait_row_gather(src_ref, dst, sem, n_rows):
    pltpu.make_async_copy(src_ref.at[pl.ds(0, n_rows)], dst, sem).wait()


def _tiles_to_rows(tiles, rows_ref):
    for j in range(ROW_SUB):
        rows_ref[:, j * LANES:(j + 1) * LANES] = tiles[:, j, :]


def _dispatch_kernel(nt_ref, idx_ref, idx_next_ref, src_ref, o_ref, buf, rows, sem):
    i = pl.program_id(0)
    nt = nt_ref[0]
    tm = o_ref.shape[0]
    slot = i % 2

    @pl.when(i == 0)
    def _():
        _start_row_gather(idx_ref, src_ref, buf.at[0], sem.at[0], tm)

    @pl.when(i + 1 < nt)
    def _():
        _start_row_gather(idx_next_ref, src_ref, buf.at[1 - slot], sem.at[1 - slot], tm)

    @pl.when(i < nt)
    def _():
        _wait_row_gather(src_ref, buf.at[slot], sem.at[slot], tm)
        _tiles_to_rows(buf.at[slot], rows)
        o_ref[...] = rows[...].astype(o_ref.dtype)

    @pl.when(i >= nt)
    def _():
        o_ref[...] = jnp.zeros(o_ref.shape, o_ref.dtype)


def moe_dispatch(n_tiles, tok_of_slot, xn3, tm):
    r = tok_of_slot.shape[0]
    last = r // tm - 1
    idx = tok_of_slot.reshape(r // tm, 1, tm)
    return pl.pallas_call(
        _dispatch_kernel,
        grid_spec=pltpu.PrefetchScalarGridSpec(
            num_scalar_prefetch=1,
            grid=(r // tm,),
            in_specs=[pl.BlockSpec((1, 1, tm), lambda i, nt: (i, 0, 0), memory_space=pltpu.SMEM),
                      pl.BlockSpec((1, 1, tm), lambda i, nt: (jnp.minimum(i + 1, last), 0, 0),
                                   memory_space=pltpu.SMEM),
                      pl.BlockSpec(memory_space=pl.ANY)],
            out_specs=pl.BlockSpec((tm, D_MODEL), lambda i, nt: (i, 0)),
            scratch_shapes=[pltpu.VMEM((2, tm, ROW_SUB, LANES), F32), pltpu.VMEM((tm, D_MODEL), F32),
                            pltpu.SemaphoreType.DMA((2,))]),
        out_shape=jax.ShapeDtypeStruct((r, D_MODEL), BF16),
        compiler_params=_cparams(("arbitrary",)),
        name="moe_dispatch",
    )(n_tiles, idx, idx, xn3)


COMBINE_TM = 256


def _combine_kernel(idx_ref, idx_next_ref, ys_ref, x_ref, gate_ref, mod_ref, o_ref, buf, rows, sem):
    i = pl.program_id(0)
    tm = o_ref.shape[0]
    n_rows = TOP_K * tm
    slot = i % 2

    @pl.when(i == 0)
    def _():
        _start_row_gather(idx_ref, ys_ref, buf.at[0], sem.at[0], n_rows)

    @pl.when(i + 1 < pl.num_programs(0))
    def _():
        _start_row_gather(idx_next_ref, ys_ref, buf.at[1 - slot], sem.at[1 - slot], n_rows)

    _wait_row_gather(ys_ref, buf.at[slot], sem.at[slot], n_rows)
    _tiles_to_rows(buf.at[slot], rows)
    f = gate_ref[:, 0:1] * rows[0:tm, :] + gate_ref[:, 1:2] * rows[tm:2 * tm, :]
    o_ref[...] = x_ref[...] + mod_ref[0, 5:6, :] * f


def moe_combine(slot_of_assign, ys3, x, gate, mod):
    t, d = x.shape
    tm = COMBINE_TM
    idx = slot_of_assign.reshape(t // tm, tm, TOP_K).transpose(0, 2, 1).reshape(t // tm, 1, TOP_K * tm)
    last = t // tm - 1
    return pl.pallas_call(
        _combine_kernel,
        grid=(t // tm,),
        in_specs=[pl.BlockSpec((1, 1, TOP_K * tm), lambda i: (i, 0, 0), memory_space=pltpu.SMEM),
                  pl.BlockSpec((1, 1, TOP_K * tm), lambda i: (jnp.minimum(i + 1, last), 0, 0),
                               memory_space=pltpu.SMEM),
                  pl.BlockSpec(memory_space=pl.ANY),
                  pl.BlockSpec((tm, d), lambda i: (i, 0)),
                  pl.BlockSpec((tm, TOP_K), lambda i: (i, 0)),
                  pl.BlockSpec((1, 8, d), lambda i: (_mod_row_of_tile(i, tm), 0, 0))],
        out_specs=pl.BlockSpec((tm, d), lambda i: (i, 0)),
        out_shape=jax.ShapeDtypeStruct((t, d), F32),
        scratch_shapes=[pltpu.VMEM((2, TOP_K * tm, ROW_SUB, LANES), F32), pltpu.VMEM((TOP_K * tm, d), F32),
                        pltpu.SemaphoreType.DMA((2,))],
        compiler_params=_cparams(("arbitrary",)),
        name="moe_combine",
    )(idx, idx, ys3, x, gate, mod)


def moe_swiglu_residual(xn3, xn, x, mod, w_router, wg, wu, wd):
    tm = MOE_TM
    logits = jnp.dot(xn.astype(F32), w_router, precision=lax.Precision.HIGHEST)
    gate, slot_of_assign, tok_of_slot, tile_expert, n_tiles = _moe_routing(logits, tm)
    xs = moe_dispatch(n_tiles * (tm // DISPATCH_TM), tok_of_slot, xn3, DISPATCH_TM)
    h = ffn_up(tile_expert, n_tiles, xs, wg, wu, tm=tm, tf=wg.shape[2] // 4, weight_buffers=2)
    ys3 = ffn_down(tile_expert, n_tiles, h, wd, tm=tm)
    return moe_combine(slot_of_assign, ys3, x, gate, mod)


def _layer(x, cond, lp, l, ctx, tabs_a, tabs_d):
    mod = modulation(cond, lp['w_mod'], lp['b_mod']).reshape(cond.shape[0], 6, D_MODEL)
    mod = jnp.pad(mod, ((0, 0), (0, 2), (0, 0)))
    p = in_proj(x, mod, lp['g_norm_mix'], pack_w_in(lp['w_in']))

    mw = pack_mla_weights(lp)
    q_a, k_a, v_a, ckv, kr = mla_prep(p, mw, tabs_a)
    kr_c = jnp.pad(ctx['mla_krope'].reshape(-1, ROPE_A), ((0, 0), (0, LANES - ROPE_A)))
    kc_a, vc_a = mla_prep_cache(ctx['mla_ckv'].reshape(-1, KV_RANK), kr_c, mw, tabs_a)
    ya_c = mla_attention(q_a, k_a, v_a, None, None, row0=0, n_seq=N_CTX_SEQ, t_seq=T_CTX, tq=T_CTX)
    ya_l = mla_attention(q_a, k_a, v_a, kc_a, vc_a, row0=R_CTX, n_seq=N_LAT_SEQ, t_seq=T_LAT, tq=ATTN_TQ)

    rw = pack_rglru_weights(lp)
    yb_c, st_rg = rglru(p, rw, jnp.zeros((N_CTX_SEQ, 2, W_B), F32), row0=0, n_seq=N_CTX_SEQ, t_seq=T_CTX)
    yb_l, _ = rglru(p, rw, ctx['rglru'], row0=R_CTX, n_seq=N_LAT_SEQ, t_seq=T_LAT)

    bias_sm = jnp.zeros((LANES,), F32).at[SM_MI:SM_MI + 2 * H_C].set(lp['b_ml_i'].reshape(-1))
    bias_sm = bias_sm.at[SM_MF:SM_MF + 2 * H_C].set(lp['b_ml_f'].reshape(-1)).reshape(1, LANES)
    c0_ctx = jnp.zeros((N_CTX_SEQ, 2, H_C, DH_C, 2 * DH_C), F32)
    m0_ctx = jnp.zeros((N_CTX_SEQ, 1, 2 * H_C), F32)
    c0_lat = jnp.concatenate([ctx['mlstm_C'], ctx['mlstm_n'][..., None],
                              jnp.zeros(ctx['mlstm_n'].shape + (DH_C - 1,), F32)], axis=-1)
    m0_lat = ctx['mlstm_m'].reshape(N_LAT_SEQ, 1, 2 * H_C)
    yc_c, c_fin, m_fin = mlstm(p, bias_sm, lp['g_ml_out'], c0_ctx, m0_ctx, row0=0, n_seq=N_CTX_SEQ, t_seq=T_CTX)
    yc_l, _, _ = mlstm(p, bias_sm, lp['g_ml_out'], c0_lat, m0_lat, row0=R_CTX, n_seq=N_LAT_SEQ, t_seq=T_LAT)

    lambda_init = 0.8 - 0.6 * math.exp(-0.3 * l)
    qd, kd_own, kd_plain = diff_prep(p, lp, tabs_d)
    dkc = ctx['diff_k'].reshape(-1, W_BRANCH)
    dvc = ctx['diff_v'].reshape(-1, W_BRANCH)
    yd_c = diff_attention(lp['diff_lambda'], qd, kd_own, p, None, None, lp['g_diff_sub'],
                          row0=0, n_seq=N_CTX_SEQ, t_seq=T_CTX, tq=T_CTX, lambda_init=lambda_init)
    yd_l = diff_attention(lp['diff_lambda'], qd, kd_own, p, dkc, dvc, lp['g_diff_sub'],
                          row0=R_CTX, n_seq=N_LAT_SEQ, t_seq=T_LAT, tq=ATTN_TQ, lambda_init=lambda_init)

    merged = merge((ya_c, yb_c, yc_c, yd_c), (ya_l, yb_l, yc_l, yd_l), p, x, mod, lp['w_br'].astype(BF16),
                   lp['w_out'].astype(BF16), lp['g_norm_ffn'], rows_as_tiles='moe' in lp)
    if 'ffn' in lp:
        x, xn = merged
        x = dense_swiglu_residual(xn, x, mod, *lp['ffn'])
    else:
        x, xn, xn3 = merged
        x = moe_swiglu_residual(xn3, xn, x, mod, *lp['moe'])

    dv0 = _seg_block('dv') * W_BRANCH
    ctx_out = (ckv[:R_CTX].reshape(N_CTX_SEQ, T_CTX, KV_RANK),
               kr[:R_CTX, :ROPE_A].reshape(N_CTX_SEQ, T_CTX, ROPE_A),
               kd_plain[:R_CTX].reshape(N_CTX_SEQ, T_CTX, 2, H_D, DH_D),
               p[:R_CTX, dv0:dv0 + W_BRANCH].astype(F32).reshape(N_CTX_SEQ, T_CTX, H_D, 2 * DH_D),
               st_rg,
               c_fin[..., :DH_C],
               c_fin[..., DH_C],
               m_fin.reshape(N_CTX_SEQ, 2, H_C))
    return x, ctx_out


def kernel(x_prompt, x_sample, cache_mla_ckv, cache_mla_krope, cache_diff_k, cache_diff_v,
           state_rglru, state_mlstm_C, state_mlstm_n, state_mlstm_m, c, c_ctx,
           w_mod, b_mod, g_norm_mix, g_norm_ffn, w_in, g_mla_qlat, w_mla_uq, g_mla_kvlat, w_mla_ukv,
           g_mla_qn, g_mla_kn, w_conv_rg, b_conv_rg, w_rg_a, b_rg_a, w_rg_x, b_rg_x, rg_lambda,
           b_ml_i, b_ml_f, g_ml_out, g_diff_qn, g_diff_kn, diff_lambda, g_diff_sub, w_br, w_out,
           w_ffn_gate, w_ffn_up, w_ffn_down, w_router, w_moe_gate, w_moe_up, w_moe_down):
    assert x_prompt.shape == (N_CTX_SEQ, T_CTX, D_MODEL) and x_sample.shape == (N_LAT_SEQ, T_LAT, D_MODEL)
    tabs_a = _rope_tables(ROPE_A, (NOPE_A,), MLA_TM)
    tabs_d = _rope_tables(DH_D, (0, DH_D), DIFF_TM)
    cond = jnp.concatenate([c_ctx.reshape(1, D_MODEL), c, jnp.zeros((16 - 1 - N_LAT_SEQ, D_MODEL), F32)], axis=0)
    x = jnp.concatenate([x_prompt.reshape(R_CTX, D_MODEL), x_sample.reshape(R_LAT, D_MODEL)], axis=0)
    new = []
    for l in range(DEPTH):
        lp = dict(w_mod=w_mod[l], b_mod=b_mod[l], g_norm_mix=g_norm_mix[l], g_norm_ffn=g_norm_ffn[l], w_in=w_in[l],
                  g_mla_qlat=g_mla_qlat[l], w_mla_uq=w_mla_uq[l], g_mla_kvlat=g_mla_kvlat[l], w_mla_ukv=w_mla_ukv[l],
                  g_mla_qn=g_mla_qn[l], g_mla_kn=g_mla_kn[l], w_conv_rg=w_conv_rg[l], b_conv_rg=b_conv_rg[l],
                  w_rg_a=w_rg_a[l], b_rg_a=b_rg_a[l], w_rg_x=w_rg_x[l], b_rg_x=b_rg_x[l], rg_lambda=rg_lambda[l],
                  b_ml_i=b_ml_i[l], b_ml_f=b_ml_f[l], g_ml_out=g_ml_out[l], g_diff_qn=g_diff_qn[l],
                  g_diff_kn=g_diff_kn[l], diff_lambda=diff_lambda[l], g_diff_sub=g_diff_sub[l],
                  w_br=w_br[l], w_out=w_out[l])
        if l % 2 == 0:
            lp['ffn'] = (w_ffn_gate[l // 2], w_ffn_up[l // 2], w_ffn_down[l // 2])
        else:
            lp['moe'] = (w_router[l // 2], w_moe_gate[l // 2], w_moe_up[l // 2], w_moe_down[l // 2])
        ctx_l = dict(mla_ckv=cache_mla_ckv[:, l], mla_krope=cache_mla_krope[:, l], diff_k=cache_diff_k[:, l],
                     diff_v=cache_diff_v[:, l], rglru=state_rglru[:, l], mlstm_C=state_mlstm_C[:, l],
                     mlstm_n=state_mlstm_n[:, l], mlstm_m=state_mlstm_m[:, l])
        x, st = _layer(x, cond, lp, l, ctx_l, tabs_a, tabs_d)
        new.append(st)
    outs = tuple(jnp.stack([s[i] for s in new], axis=1) for i in range(8))
    return (x[:R_CTX].reshape(N_CTX_SEQ, T_CTX, D_MODEL), x[R_CTX:].reshape(N_LAT_SEQ, T_LAT, D_MODEL)) + outs
```

```python
import functools
import math

import jax
import jax.numpy as jnp
import numpy as np
from jax import lax
from jax.experimental import pallas as pl
from jax.experimental.pallas import tpu as pltpu

D_MODEL = 1024
DEPTH = 2
GRID_W = 64
ROPE_BASE = 10000.0
EPS = 1e-6
N_BRANCH = 4
W_BRANCH = D_MODEL // 2

H_A = 8
NOPE_A = 64
ROPE_A = 32
V_A = W_BRANCH // H_A
Q_RANK = D_MODEL // 4
KV_RANK = D_MODEL // 8
MLA_SCALE = (NOPE_A + ROPE_A) ** -0.5

W_B = W_BRANCH
NB_B = 8
BW_B = W_B // NB_B
CONV_W = 4
RG_C = 8.0

H_C = 4
DH_C = W_BRANCH // H_C
ML_CHUNK = 128

H_D = 4
DH_D = W_BRANCH // (2 * H_D)
DIFF_SCALE = DH_D ** -0.5

N_EXP = 8
TOP_K = 2

V7X_VMEM_LIMIT_BYTES = 56 * 1024 * 1024
LANES = 128
SUBLANES = 8

BF16 = jnp.bfloat16
F32 = jnp.float32

P_GATE = 0
P_SEG = N_BRANCH * D_MODEL
SEG_NAMES = ('rg_x', 'rg_gate', 'mq', 'mk', 'mv', 'mo', 'dq', 'dk', 'dv')
P_QLAT = P_SEG + 9 * W_BRANCH
P_KVLAT = P_QLAT + Q_RANK
P_SMALL = P_KVLAT + KV_RANK
P_WIDTH = P_SMALL + LANES
SM_MI = ROPE_A
SM_MF = ROPE_A + 2 * H_C


def _seg_block(name):
    return (P_SEG + SEG_NAMES.index(name) * W_BRANCH) // W_BRANCH


N_CTX_SEQ, T_CTX = 16, 256
N_LAT_SEQ, T_LAT = 8, 1024
PAST_LEN = 512
R_CTX = N_CTX_SEQ * T_CTX
R_LAT = N_LAT_SEQ * T_LAT
R_ALL = R_CTX + R_LAT


def _cparams(sem):
    return pltpu.CompilerParams(dimension_semantics=sem, vmem_limit_bytes=V7X_VMEM_LIMIT_BYTES)


def _mod_row_of_tile(i, tm):
    n_ctx_tiles = R_CTX // tm
    per_seq = T_LAT // tm
    return jnp.where(i < n_ctx_tiles, 0, 1 + (i - n_ctx_tiles) // per_seq)


def _pos_block_of_tile(i, tm):
    n_ctx_tiles = R_CTX // tm
    per_seq = T_LAT // tm
    return jnp.where(i < n_ctx_tiles, per_seq, (i - n_ctx_tiles) % per_seq)


def _mod_kernel(c_ref, w_ref, b_ref, o_ref):
    c = c_ref[...]
    s = (c * jax.nn.sigmoid(c)).astype(BF16)
    o_ref[...] = jnp.dot(s, w_ref[...].astype(BF16), preferred_element_type=F32) + b_ref[...]


def modulation(cond, w_mod, b_mod):
    m, d = cond.shape
    n = w_mod.shape[1]
    tn = 1536
    return pl.pallas_call(
        _mod_kernel,
        grid=(n // tn,),
        in_specs=[pl.BlockSpec((m, d), lambda j: (0, 0)),
                  pl.BlockSpec((d, tn), lambda j: (0, j)),
                  pl.BlockSpec((1, tn), lambda j: (0, j))],
        out_specs=pl.BlockSpec((m, tn), lambda j: (0, j)),
        out_shape=jax.ShapeDtypeStruct((m, n), F32),
        compiler_params=_cparams(("arbitrary",)),
        name="modulation",
    )(cond, w_mod, b_mod.reshape(1, n))


def _in_proj_kernel(x_ref, mod_ref, g_ref, w_ref, o_ref, xn_ref):
    @pl.when(pl.program_id(1) == 0)
    def _():
        x = x_ref[...]
        r = lax.rsqrt(jnp.mean(x * x, axis=-1, keepdims=True) + EPS)
        sh = mod_ref[0, 0:1, :]
        sc = mod_ref[0, 1:2, :]
        xn_ref[...] = ((x * r * g_ref[...]) * (1.0 + sc) + sh).astype(BF16)

    o_ref[...] = jnp.dot(xn_ref[...], w_ref[...], preferred_element_type=F32).astype(o_ref.dtype)


def in_proj(x, mod, g, w_p):
    tm, tn = 1024, 1024
    m, d = x.shape
    n = w_p.shape[1]
    return pl.pallas_call(
        _in_proj_kernel,
        grid=(m // tm, n // tn),
        in_specs=[pl.BlockSpec((tm, d), lambda i, j: (i, 0)),
                  pl.BlockSpec((1, 8, d), lambda i, j: (_mod_row_of_tile(i, tm), 0, 0)),
                  pl.BlockSpec((1, d), lambda i, j: (0, 0)),
                  pl.BlockSpec((d, tn), lambda i, j: (0, j))],
        out_specs=pl.BlockSpec((tm, tn), lambda i, j: (i, j)),
        out_shape=jax.ShapeDtypeStruct((m, n), BF16),
        scratch_shapes=[pltpu.VMEM((tm, d), BF16)],
        compiler_params=_cparams(("arbitrary", "arbitrary")),
        name="in_proj",
    )(x, mod, g.reshape(1, d), w_p)


def pack_w_in(w_in):
    o_rg = Q_RANK + KV_RANK + ROPE_A
    o_mi = o_rg + 6 * W_BRANCH
    o_dq = o_mi + 4 * H_C
    o_gate = o_dq + 3 * W_BRANCH
    pad = jnp.zeros((w_in.shape[0], LANES - ROPE_A - 4 * H_C), w_in.dtype)
    parts = [w_in[:, o_gate:], w_in[:, o_rg:o_mi], w_in[:, o_dq:o_gate], w_in[:, :Q_RANK + KV_RANK],
             w_in[:, Q_RANK + KV_RANK:o_rg], w_in[:, o_mi:o_dq], pad]
    return jnp.concatenate(parts, axis=1).astype(BF16)


def _rope_tables(rot_dim, lane_starts, tm):
    rows = T_LAT // GRID_W
    r, c = np.meshgrid(np.arange(rows, dtype=np.float32), np.arange(GRID_W, dtype=np.float32), indexing='ij')
    nf = rot_dim // 4
    inv = (np.float32(ROPE_BASE) ** (-np.arange(nf, dtype=np.float32) / np.float32(nf))).astype(np.float32)
    ang = np.stack([r.reshape(-1)[:, None] * inv, c.reshape(-1)[:, None] * inv], axis=1).astype(np.float32)
    cos, sin = np.cos(ang).astype(np.float32), np.sin(ang).astype(np.float32)
    tc = np.ones((T_LAT + tm, LANES), np.float32)
    ta = np.zeros((T_LAT + tm, LANES), np.float32)
    tb = np.zeros((T_LAT + tm, LANES), np.float32)
    for s0 in lane_starts:
        for a in range(2):
            lo = s0 + a * 2 * nf
            tc[:T_LAT, lo:lo + nf] = cos[:, a]
            tc[:T_LAT, lo + nf:lo + 2 * nf] = cos[:, a]
            ta[:T_LAT, lo:lo + nf] = -sin[:, a]
            tb[:T_LAT, lo + nf:lo + 2 * nf] = sin[:, a]
    return jnp.asarray(tc), jnp.asarray(ta), jnp.asarray(tb)


def _rope(x, c, sa, sb, half):
    return x * c + pltpu.roll(x, LANES - half, 1) * sa + pltpu.roll(x, half, 1) * sb


MLA_TM = 512
ATTN_TQ = 512
MLA_HEADS_PER_STEP = 8
QK_A = NOPE_A + ROPE_A


def _mla_prep_kernel(*refs, has_q, norm_ckv):
    if has_q:
        (qlat_ref, gq_ref, wuq_ref, gqn_ref, kv_ref, sm_ref, gkv_ref, wkc_ref, wv_ref, gkn_ref,
         c_ref, sa_ref, sb_ref, q_o, k_o, v_o, ckv_o, kr_o) = refs
    else:
        (kv_ref, sm_ref, gkv_ref, wkc_ref, wv_ref, gkn_ref, c_ref, sa_ref, sb_ref, k_o, v_o) = refs
    c, sa, sb = c_ref[...], sa_ref[...], sb_ref[...]

    def heads(z, g, o_ref, scale):
        for h in range(H_A):
            s = z[:, h * LANES:(h + 1) * LANES]
            r = lax.rsqrt(jnp.sum(s * s, axis=-1, keepdims=True) * (1.0 / QK_A) + EPS)
            y = _rope(s * r * g, c, sa, sb, ROPE_A // 4)
            if scale != 1.0:
                y = y * scale
            o_ref[:, h * LANES:(h + 1) * LANES] = y.astype(o_ref.dtype)

    if has_q:
        ql = qlat_ref[...].astype(F32)
        qn = ql * lax.rsqrt(jnp.mean(ql * ql, axis=-1, keepdims=True) + EPS) * gq_ref[...]
        q = jnp.dot(qn.astype(BF16), wuq_ref[...], preferred_element_type=F32)
        heads(q, gqn_ref[...], q_o, MLA_SCALE)

    kv = kv_ref[...].astype(F32)
    if norm_ckv:
        ckv = kv * lax.rsqrt(jnp.mean(kv * kv, axis=-1, keepdims=True) + EPS) * gkv_ref[...]
    else:
        ckv = kv
    sm = sm_ref[...]
    ckv_b = ckv.astype(BF16)
    kin = jnp.concatenate([ckv_b, sm.astype(BF16)], axis=1)
    k = jnp.dot(kin, wkc_ref[...], preferred_element_type=F32)
    heads(k, gkn_ref[...], k_o, 1.0)
    v_o[...] = jnp.dot(ckv_b, wv_ref[...], preferred_element_type=F32).astype(v_o.dtype)
    if has_q:
        ckv_o[...] = ckv
        kr_o[...] = sm.astype(F32)


def pack_mla_weights(lp):
    wuq = lp['w_mla_uq'].reshape(Q_RANK, H_A, QK_A)
    wuq_p = jnp.pad(wuq, ((0, 0), (0, 0), (0, LANES - QK_A))).reshape(Q_RANK, H_A * LANES).astype(BF16)
    wukv = lp['w_mla_ukv'].reshape(KV_RANK, H_A, NOPE_A + V_A)
    wk = jnp.pad(wukv[:, :, :NOPE_A], ((0, 0), (0, 0), (0, LANES - NOPE_A))).reshape(KV_RANK, H_A * LANES)
    place = np.zeros((LANES, H_A, LANES), np.float32)
    for h in range(H_A):
        place[np.arange(ROPE_A), h, NOPE_A + np.arange(ROPE_A)] = 1.0
    wkc = jnp.concatenate([wk, jnp.asarray(place.reshape(LANES, H_A * LANES))], axis=0).astype(BF16)
    wv = wukv[:, :, NOPE_A:]
    wv_even = jnp.pad(wv, ((0, 0), (0, 0), (0, LANES - V_A)))
    wv_odd = jnp.pad(wv, ((0, 0), (0, 0), (LANES - V_A, 0)))
    odd = (np.arange(H_A) % 2 == 1)[None, :, None]
    wv_p = jnp.where(odd, wv_odd, wv_even).reshape(KV_RANK, H_A * LANES).astype(BF16)
    pad_g = lambda g: jnp.pad(g, (0, LANES - QK_A)).reshape(1, LANES)
    return dict(wuq=wuq_p, wkc=wkc, wv=wv_p, gqn=pad_g(lp['g_mla_qn']), gkn=pad_g(lp['g_mla_kn']),
                gq=lp['g_mla_qlat'].reshape(1, Q_RANK), gkv=lp['g_mla_kvlat'].reshape(1, KV_RANK))


def mla_prep(p, mw, tabs):
    tm = MLA_TM
    n = R_ALL // tm
    full = lambda shape: pl.BlockSpec(shape, lambda i: (0, 0))
    tab = pl.BlockSpec((tm, LANES), lambda i: (_pos_block_of_tile(i, tm), 0))
    wide = H_A * LANES
    out_shape = (jax.ShapeDtypeStruct((R_ALL, wide), BF16), jax.ShapeDtypeStruct((R_ALL, wide), BF16),
                 jax.ShapeDtypeStruct((R_ALL, wide), BF16), jax.ShapeDtypeStruct((R_ALL, KV_RANK), F32),
                 jax.ShapeDtypeStruct((R_ALL, LANES), F32))
    row = lambda w: pl.BlockSpec((tm, w), lambda i: (i, 0))
    return pl.pallas_call(
        functools.partial(_mla_prep_kernel, has_q=True, norm_ckv=True),
        grid=(n,),
        in_specs=[pl.BlockSpec((tm, Q_RANK), lambda i: (i, P_QLAT // Q_RANK)), full((1, Q_RANK)),
                  full((Q_RANK, wide)), full((1, LANES)),
                  pl.BlockSpec((tm, KV_RANK), lambda i: (i, P_KVLAT // KV_RANK)),
                  pl.BlockSpec((tm, LANES), lambda i: (i, P_SMALL // LANES)), full((1, KV_RANK)),
                  full((2 * LANES, wide)), full((KV_RANK, wide)), full((1, LANES)), tab, tab, tab],
        out_specs=(row(wide), row(wide), row(wide), row(KV_RANK), row(LANES)),
        out_shape=out_shape,
        compiler_params=_cparams(("arbitrary",)),
        name="mla_prep",
    )(p, mw['gq'], mw['wuq'], mw['gqn'], p, p, mw['gkv'], mw['wkc'], mw['wv'], mw['gkn'], *tabs)


def mla_prep_cache(ckv_c, kr_c, mw, tabs):
    tm = MLA_TM
    r = ckv_c.shape[0]
    full = lambda shape: pl.BlockSpec(shape, lambda i: (0, 0))
    ident = pl.BlockSpec((tm, LANES), lambda i: (T_LAT // tm, 0))
    wide = H_A * LANES
    row = lambda w: pl.BlockSpec((tm, w), lambda i: (i, 0))
    return pl.pallas_call(
        functools.partial(_mla_prep_kernel, has_q=False, norm_ckv=False),
        grid=(r // tm,),
        in_specs=[row(KV_RANK), row(LANES), full((1, KV_RANK)), full((2 * LANES, wide)), full((KV_RANK, wide)),
                  full((1, LANES)), ident, ident, ident],
        out_specs=(row(wide), row(wide)),
        out_shape=(jax.ShapeDtypeStruct((r, wide), BF16), jax.ShapeDtypeStruct((r, wide), BF16)),
        compiler_params=_cparams(("arbitrary",)),
        name="mla_prep_cache",
    )(ckv_c, kr_c, mw['gkv'], mw['wkc'], mw['wv'], mw['gkn'], *tabs)


_NT = (((1,), (1,)), ((), ()))


def _mla_attn_kernel(*refs, has_cache):
    if has_cache:
        q_ref, ko_ref, vo_ref, kc_ref, vc_ref, o_ref = refs
    else:
        q_ref, ko_ref, vo_ref, o_ref = refs
    n_heads = q_ref.shape[1] // LANES
    sls = [slice(h * LANES, (h + 1) * LANES) for h in range(n_heads)]
    def scores(sl):
        s_o = lax.dot_general(q_ref[:, sl], ko_ref[:, sl], _NT, preferred_element_type=F32)
        s_c = lax.dot_general(q_ref[:, sl], kc_ref[:, sl], _NT, preferred_element_type=F32) if has_cache else None
        return s_o, s_c

    outs = []
    nxt = scores(sls[0])
    for h, sl in enumerate(sls):
        s_o, s_c = nxt
        if h + 1 < n_heads:
            nxt = scores(sls[h + 1])
        m = jnp.max(s_o, axis=-1, keepdims=True)
        if has_cache:
            m = jnp.maximum(m, jnp.max(s_c, axis=-1, keepdims=True))
        e_o = jnp.exp(s_o - m)
        l = jnp.sum(e_o, axis=-1, keepdims=True)
        pv = jnp.dot(e_o.astype(BF16), vo_ref[:, sl], preferred_element_type=F32)
        if has_cache:
            e_c = jnp.exp(s_c - m)
            l = l + jnp.sum(e_c, axis=-1, keepdims=True)
            pv = pv + jnp.dot(e_c.astype(BF16), vc_ref[:, sl], preferred_element_type=F32)
        outs.append(pv * (1.0 / l))
    for p in range(n_heads // 2):
        o_ref[:, p * LANES:(p + 1) * LANES] = (outs[2 * p] + outs[2 * p + 1]).astype(o_ref.dtype)


def mla_attention(q, k, v, kc, vc, *, row0, n_seq, t_seq, tq):
    has_cache = kc is not None
    wh = MLA_HEADS_PER_STEP * LANES
    n_grp = H_A // MLA_HEADS_PER_STEP
    nq = t_seq // tq
    qb0, kb0 = row0 // tq, row0 // t_seq
    in_specs = [pl.BlockSpec((tq, wh), lambda s, p, i: (qb0 + s * nq + i, p)),
                pl.BlockSpec((t_seq, wh), lambda s, p, i: (kb0 + s, p)),
                pl.BlockSpec((t_seq, wh), lambda s, p, i: (kb0 + s, p))]
    args = [q, k, v]
    if has_cache:
        in_specs += [pl.BlockSpec((PAST_LEN, wh), lambda s, p, i: (s, p)),
                     pl.BlockSpec((PAST_LEN, wh), lambda s, p, i: (s, p))]
        args += [kc, vc]
    return pl.pallas_call(
        functools.partial(_mla_attn_kernel, has_cache=has_cache),
        grid=(n_seq, n_grp, nq),
        in_specs=in_specs,
        out_specs=pl.BlockSpec((tq, wh // 2), lambda s, p, i: (s * nq + i, p)),
        out_shape=jax.ShapeDtypeStruct((n_seq * t_seq, W_BRANCH), BF16),
        compiler_params=_cparams(("arbitrary", "arbitrary", "arbitrary")),
        name="mla_attention",
    )(*args)


DIFF_TM = 512


def _diff_prep_kernel(dq_ref, dk_ref, gq_ref, gk_ref, c_ref, sa_ref, sb_ref, q_o, ko_o, kp_o):
    c, sa, sb = c_ref[...], sa_ref[...], sb_ref[...]
    lane = lax.broadcasted_iota(jnp.int32, (1, LANES), 1)
    lo = lane < DH_D

    def normed(x, g):
        x2 = x * x
        s_lo = jnp.sum(jnp.where(lo, x2, 0.0), axis=-1, keepdims=True)
        s_hi = jnp.sum(jnp.where(lo, 0.0, x2), axis=-1, keepdims=True)
        ms = jnp.where(lo, s_lo, s_hi) * (1.0 / DH_D)
        return x * lax.rsqrt(ms + EPS) * g

    for j in range(W_BRANCH // LANES):
        sl = slice(j * LANES, (j + 1) * LANES)
        qn = normed(dq_ref[:, sl].astype(F32), gq_ref[...])
        q_o[:, sl] = (_rope(qn, c, sa, sb, DH_D // 4) * DIFF_SCALE).astype(q_o.dtype)
        kn = normed(dk_ref[:, sl].astype(F32), gk_ref[...])
        kp_o[:, sl] = kn
        ko_o[:, sl] = _rope(kn, c, sa, sb, DH_D // 4).astype(ko_o.dtype)


def diff_prep(p, lp, tabs):
    tm = DIFF_TM
    tile2 = lambda g: jnp.concatenate([g, g]).reshape(1, LANES)
    full = pl.BlockSpec((1, LANES), lambda i: (0, 0))
    tab = pl.BlockSpec((tm, LANES), lambda i: (_pos_block_of_tile(i, tm), 0))
    row = pl.BlockSpec((tm, W_BRANCH), lambda i: (i, 0))
    dq_b, dk_b = _seg_block('dq'), _seg_block('dk')
    return pl.pallas_call(
        _diff_prep_kernel,
        grid=(R_ALL // tm,),
        in_specs=[pl.BlockSpec((tm, W_BRANCH), lambda i: (i, dq_b)), pl.BlockSpec((tm, W_BRANCH), lambda i: (i, dk_b)),
                  full, full, tab, tab, tab],
        out_specs=(row, row, row),
        out_shape=(jax.ShapeDtypeStruct((R_ALL, W_BRANCH), BF16), jax.ShapeDtypeStruct((R_ALL, W_BRANCH), BF16),
                   jax.ShapeDtypeStruct((R_ALL, W_BRANCH), F32)),
        compiler_params=_cparams(("arbitrary",)),
        name="diff_prep",
    )(p, p, tile2(lp['g_diff_qn']), tile2(lp['g_diff_kn']), *tabs)


def _diff_attn_kernel(*refs, has_cache, lambda_init):
    if has_cache:
        dl_ref, q_ref, k_ref, v_ref, kc_ref, vc_ref, g_ref, o_ref = refs
    else:
        dl_ref, q_ref, k_ref, v_ref, g_ref, o_ref = refs
    lane = lax.broadcasted_iota(jnp.int32, (1, LANES), 1)
    dl = dl_ref[...]
    lam = (jnp.exp(jnp.sum(dl[0:1] * dl[1:2], axis=-1, keepdims=True))
           - jnp.exp(jnp.sum(dl[2:3] * dl[3:4], axis=-1, keepdims=True)) + lambda_init)
    units = [(h, w) for h in range(H_D) for w in range(2)]

    def scores(unit):
        h, w = unit
        sl = slice((2 * w + h // 2) * LANES, (2 * w + h // 2 + 1) * LANES)
        q = jnp.where((lane // DH_D) == (h % 2), q_ref[:, sl], jnp.zeros((), q_ref.dtype))
        s_o = lax.dot_general(q, k_ref[:, sl], _NT, preferred_element_type=F32)
        s_c = (lax.dot_general(q, kc_ref[:, sl].astype(BF16), _NT, preferred_element_type=F32)
               if has_cache else None)
        return s_o, s_c

    outs = []
    nxt = scores(units[0])
    for n, (h, w) in enumerate(units):
        s_o, s_c = nxt
        if n + 1 < len(units):
            nxt = scores(units[n + 1])
        vs = slice(h * LANES, (h + 1) * LANES)
        m = jnp.max(s_o, axis=-1, keepdims=True)
        if has_cache:
            m = jnp.maximum(m, jnp.max(s_c, axis=-1, keepdims=True))
        e_o = jnp.exp(s_o - m)
        l = jnp.sum(e_o, axis=-1, keepdims=True)
        pv = jnp.dot(e_o.astype(BF16), v_ref[:, vs], preferred_element_type=F32)
        if has_cache:
            e_c = jnp.exp(s_c - m)
            l = l + jnp.sum(e_c, axis=-1, keepdims=True)
            pv = pv + jnp.dot(e_c.astype(BF16), vc_ref[:, vs].astype(BF16), preferred_element_type=F32)
        outs.append(pv * (1.0 / l))
    for h in range(H_D):
        y = outs[2 * h] - lam * outs[2 * h + 1]
        r = lax.rsqrt(jnp.mean(y * y, axis=-1, keepdims=True) + EPS)
        o_ref[:, h * LANES:(h + 1) * LANES] = ((y * r * g_ref[...]) * (1.0 - lambda_init)).astype(o_ref.dtype)


def diff_attention(dl, qd, kd, p, kc, vc, g_sub, *, row0, n_seq, t_seq, tq, lambda_init):
    has_cache = kc is not None
    nq = t_seq // tq
    qb0, kb0 = row0 // tq, row0 // t_seq
    in_specs = [pl.BlockSpec((4, DH_D), lambda s, i: (0, 0)),
                pl.BlockSpec((tq, W_BRANCH), lambda s, i: (qb0 + s * nq + i, 0)),
                pl.BlockSpec((t_seq, W_BRANCH), lambda s, i: (kb0 + s, 0)),
                pl.BlockSpec((t_seq, W_BRANCH), lambda s, i: (kb0 + s, _seg_block('dv')))]
    args = [dl, qd, kd, p]
    if has_cache:
        in_specs += [pl.BlockSpec((PAST_LEN, W_BRANCH), lambda s, i: (s, 0)),
                     pl.BlockSpec((PAST_LEN, W_BRANCH), lambda s, i: (s, 0))]
        args += [kc, vc]
    in_specs.append(pl.BlockSpec((1, LANES), lambda s, i: (0, 0)))
    args.append(g_sub.reshape(1, LANES))
    return pl.pallas_call(
        functools.partial(_diff_attn_kernel, has_cache=has_cache, lambda_init=lambda_init),
        grid=(n_seq, nq),
        in_specs=in_specs,
        out_specs=pl.BlockSpec((tq, W_BRANCH), lambda s, i: (s * nq + i, 0)),
        out_shape=jax.ShapeDtypeStruct((n_seq * t_seq, W_BRANCH), BF16),
        compiler_params=_cparams(("arbitrary", "arbitrary")),
        name="diff_attention",
    )(*args)


def _neg_expm1(z):
    series = -z * (1.0 + z * (0.5 + z * (1.0 / 6.0 + z * (1.0 / 24.0 + z * (1.0 / 120.0 + z * (1.0 / 720.0))))))
    return jnp.where(z > -0.25, series, 1.0 - jnp.exp(z))


def _softplus(z):
    return jnp.maximum(z, 0.0) + jnp.log(1.0 + jnp.exp(-jnp.abs(z)))


def _gelu_tanh(x):
    return 0.5 * x * (1.0 + jnp.tanh(math.sqrt(2.0 / math.pi) * (x + 0.044715 * (x * x * x))))


def _rglru_kernel(x_ref, gate_ref, wc_ref, bc_ref, wg_ref, bg_ref, lam_ref, h0_ref, y_ref, st_ref,
                  a_s, u_s, h_s, *, t_seq):
    t = t_seq
    x = x_ref[...].astype(F32)
    row = lax.broadcasted_iota(jnp.int32, (t, W_B), 0)
    wc = wc_ref[...]
    xc = (wc[0:1] * jnp.where(row >= 2, pltpu.roll(x, 2, 0), 0.0)
          + wc[1:2] * jnp.where(row >= 1, pltpu.roll(x, 1, 0), 0.0)
          + wc[2:3] * x
          + wc[3:4] * jnp.where(row < t - 1, pltpu.roll(x, t - 1, 0), 0.0)
          + bc_ref[...])
    gates = jnp.dot(xc.astype(BF16), wg_ref[...], preferred_element_type=F32) + bg_ref[...]
    r8 = row % SUBLANES
    nblk = t // SUBLANES
    hsum = None
    for d in range(2):
        rg = jax.nn.sigmoid(gates[:, (2 * d) * W_B:(2 * d + 1) * W_B])
        ig = jax.nn.sigmoid(gates[:, (2 * d + 1) * W_B:(2 * d + 2) * W_B])
        log_a = -RG_C * rg * _softplus(-lam_ref[d:d + 1, :])
        a = jnp.exp(log_a)
        u = jnp.sqrt(_neg_expm1(2.0 * log_a)) * (ig * xc)
        for step in (1, 2, 4):
            if d == 0:
                valid = r8 >= step
                a_sh, u_sh = pltpu.roll(a, step, 0), pltpu.roll(u, step, 0)
            else:
                valid = r8 + step < SUBLANES
                a_sh, u_sh = pltpu.roll(a, t - step, 0), pltpu.roll(u, t - step, 0)
            u = jnp.where(valid, a * u_sh + u, u)
            a = jnp.where(valid, a * a_sh, a)
        a_s[...] = a
        u_s[...] = u
        h0 = h0_ref[0, d:d + 1, :]

        def body(k, carry, d=d):
            blk = k if d == 0 else nblk - 1 - k
            rows = pl.ds(pl.multiple_of(blk * SUBLANES, SUBLANES), SUBLANES)
            h = a_s[rows, :] * carry + u_s[rows, :]
            h_s[rows, :] = h
            return h[SUBLANES - 1:SUBLANES, :] if d == 0 else h[0:1, :]

        last = lax.fori_loop(0, nblk, body, h0, unroll=8)
        st_ref[0, d:d + 1, :] = last
        hsum = h_s[...] if d == 0 else hsum + h_s[...]
    y_ref[...] = (_gelu_tanh(gate_ref[...].astype(F32)) * hsum).astype(y_ref.dtype)


def pack_rglru_weights(lp):
    def blockdiag(w):
        eye = jnp.eye(NB_B, dtype=w.dtype)
        return jnp.einsum('ncd,nm->ncmd', w, eye).reshape(W_B, W_B)
    wg = jnp.concatenate([blockdiag(lp['w_rg_a'][0]), blockdiag(lp['w_rg_x'][0]),
                          blockdiag(lp['w_rg_a'][1]), blockdiag(lp['w_rg_x'][1])], axis=1).astype(BF16)
    bg = jnp.concatenate([lp['b_rg_a'][0], lp['b_rg_x'][0], lp['b_rg_a'][1], lp['b_rg_x'][1]]).reshape(1, 4 * W_B)
    return dict(wg=wg, bg=bg, wc=lp['w_conv_rg'], bc=lp['b_conv_rg'].reshape(1, W_B), lam=lp['rg_lambda'])


def rglru(p, rw, h0, *, row0, n_seq, t_seq):
    rb0 = row0 // t_seq
    xb, gb = _seg_block('rg_x'), _seg_block('rg_gate')
    full = lambda shape: pl.BlockSpec(shape, lambda s: tuple(0 for _ in shape))
    in_specs = [pl.BlockSpec((t_seq, W_B), lambda s: (rb0 + s, xb)),
                pl.BlockSpec((t_seq, W_B), lambda s: (rb0 + s, gb)),
                full((CONV_W, W_B)), full((1, W_B)), full((W_B, 4 * W_B)), full((1, 4 * W_B)), full((2, W_B)),
                pl.BlockSpec((1, 2, W_B), lambda s: (s, 0, 0))]
    args = [p, p, rw['wc'], rw['bc'], rw['wg'], rw['bg'], rw['lam'], h0]
    return pl.pallas_call(
        functools.partial(_rglru_kernel, t_seq=t_seq),
        grid=(n_seq,),
        in_specs=in_specs,
        out_specs=(pl.BlockSpec((t_seq, W_B), lambda s: (s, 0)), pl.BlockSpec((1, 2, W_B), lambda s: (s, 0, 0))),
        out_shape=(jax.ShapeDtypeStruct((n_seq * t_seq, W_B), BF16), jax.ShapeDtypeStruct((n_seq, 2, W_B), F32)),
        scratch_shapes=[pltpu.VMEM((t_seq, W_B), F32)] * 3,
        compiler_params=_cparams(("arbitrary",)),
        name="rglru",
    )(*args)


def _dot_split(a, b_bf16):
    hi = a.astype(BF16)
    lo = (a - hi.astype(F32)).astype(BF16)
    return (jnp.dot(hi, b_bf16, preferred_element_type=F32) + jnp.dot(lo, b_bf16, preferred_element_type=F32))


def _log_sigmoid(z):
    return jnp.minimum(z, 0.0) - jnp.log(1.0 + jnp.exp(-jnp.abs(z)))


_TN = (((0,), (0,)), ((), ()))


def _mlstm_kernel(q_ref, k_ref, v_ref, o_ref, sm_ref, bias_ref, g_ref, c0_ref, m0_ref,
                  y_ref, c_out, m_out, hm_s, c_s, *, t_seq):
    L = ML_CHUNK
    nchunk = t_seq // L
    scale = DH_C ** -0.5
    ri = lax.broadcasted_iota(jnp.int32, (L, L), 0)
    ci = lax.broadcasted_iota(jnp.int32, (L, L), 1)
    lane1 = lax.broadcasted_iota(jnp.int32, (L, LANES), 1)
    ones_col = jnp.where(lane1 == 0, 1.0, 0.0).astype(BF16)
    bias = bias_ref[...]

    for d in range(2):
        causal = (ci <= ri) if d == 0 else (ci >= ri)
        tri = jnp.where(causal, 1.0, 0.0).astype(BF16)
        tri_t = jnp.where((ri <= ci) if d == 0 else (ri >= ci), 1.0, 0.0).astype(BF16)
        c_s[...] = c0_ref[0, d]
        m_init = tuple(m0_ref[0, :, d * H_C + h:d * H_C + h + 1] for h in range(H_C))

        def chunk(kk, ms, d=d, causal=causal, tri=tri, tri_t=tri_t):
            cidx = kk if d == 0 else nchunk - 1 - kk
            rows = pl.ds(pl.multiple_of(cidx * L, L), L)
            gsm = sm_ref[rows, :].astype(F32) + bias
            lf_all = _log_sigmoid(gsm)
            cum_cols = _dot_split_left(tri, lf_all)
            g_t = gsm.T
            cum_rows = _dot_split(lf_all.T, tri_t)
            new_ms = []
            for h in range(H_C):
                jl, jf = SM_MI + d * H_C + h, SM_MF + d * H_C + h
                cum_c = cum_cols[:, jf:jf + 1]
                cum_r = cum_rows[jf:jf + 1, :]
                li_c = gsm[:, jl:jl + 1]
                li_r = g_t[jl:jl + 1, :]
                m_mem = ms[h]
                log_d = jnp.where(causal, cum_c - cum_r + li_r, -jnp.inf)
                inter = cum_c + m_mem
                m_row = jnp.maximum(inter, jnp.max(log_d, axis=-1, keepdims=True))
                dmat = jnp.exp(log_d - m_row)
                w_inter = jnp.exp(inter - m_row)
                sl = slice(h * DH_C, (h + 1) * DH_C)
                qh, kh, vh = q_ref[rows, sl], k_ref[rows, sl], v_ref[rows, sl]
                s = lax.dot_general(qh, kh, _NT, preferred_element_type=F32) * (scale * dmat)
                sv = jnp.dot(s.astype(BF16), vh, preferred_element_type=F32)
                qc = jnp.dot(qh, c_s[h].astype(BF16), preferred_element_type=F32) * scale
                num = sv + qc[:, :DH_C] * w_inter
                den = jnp.sum(s, axis=-1, keepdims=True) + w_inter * qc[:, DH_C:DH_C + 1]
                den = jnp.maximum(jnp.abs(den), jnp.exp(-m_row))
                h_out = num * (1.0 / den)
                if d == 0:
                    hm_s[rows, sl] = h_out
                else:
                    hm_s[rows, sl] = hm_s[rows, sl] + h_out
                last = cum_c[L - 1:L, :] if d == 0 else cum_c[0:1, :]
                w_s = last - cum_c + li_c
                m_new = jnp.maximum(last + m_mem, jnp.max(w_s, axis=0, keepdims=True))
                decay = jnp.exp(last + m_mem - m_new)
                ws = jnp.exp(w_s - m_new)
                kw_t = (kh.astype(F32) * ws).T.astype(BF16)
                v_aug = jnp.concatenate([vh, ones_col], axis=1)
                c_s[h] = decay * c_s[h] + jnp.dot(kw_t, v_aug, preferred_element_type=F32)
                new_ms.append(m_new)
            return tuple(new_ms)

        m_fin = lax.fori_loop(0, nchunk, chunk, m_init)
        c_out[0, d] = c_s[...]
        for h in range(H_C):
            m_out[0, :, d * H_C + h:d * H_C + h + 1] = m_fin[h]

    for h in range(H_C):
        sl = slice(h * DH_C, (h + 1) * DH_C)
        hm = hm_s[:, sl]
        r = lax.rsqrt(jnp.mean(hm * hm, axis=-1, keepdims=True) + EPS)
        y_ref[:, sl] = (jax.nn.sigmoid(o_ref[:, sl].astype(F32)) * (hm * r * g_ref[...])).astype(y_ref.dtype)


def _dot_split_left(a_bf16, b):
    hi = b.astype(BF16)
    lo = (b - hi.astype(F32)).astype(BF16)
    return (jnp.dot(a_bf16, hi, preferred_element_type=F32) + jnp.dot(a_bf16, lo, preferred_element_type=F32))


def mlstm(p, bias_sm, g_out, c0_aug, m0, *, row0, n_seq, t_seq):
    rb0 = row0 // t_seq
    seg = lambda nm: pl.BlockSpec((t_seq, W_BRANCH), lambda s, b=_seg_block(nm): (rb0 + s, b))
    full = lambda shape: pl.BlockSpec(shape, lambda s: tuple(0 for _ in shape))
    st_spec = pl.BlockSpec((1, 2, H_C, DH_C, 2 * DH_C), lambda s: (s, 0, 0, 0, 0))
    m_spec = pl.BlockSpec((1, 1, 2 * H_C), lambda s: (s, 0, 0))
    in_specs = [seg('mq'), seg('mk'), seg('mv'), seg('mo'),
                pl.BlockSpec((t_seq, LANES), lambda s: (rb0 + s, P_SMALL // LANES)),
                full((1, LANES)), full((1, DH_C)), st_spec, m_spec]
    args = [p, p, p, p, p, bias_sm, g_out.reshape(1, DH_C), c0_aug, m0]
    return pl.pallas_call(
        functools.partial(_mlstm_kernel, t_seq=t_seq),
        grid=(n_seq,),
        in_specs=in_specs,
        out_specs=(pl.BlockSpec((t_seq, W_BRANCH), lambda s: (s, 0)), st_spec, m_spec),
        out_shape=(jax.ShapeDtypeStruct((n_seq * t_seq, W_BRANCH), BF16),
                   jax.ShapeDtypeStruct((n_seq, 2, H_C, DH_C, 2 * DH_C), F32),
                   jax.ShapeDtypeStruct((n_seq, 1, 2 * H_C), F32)),
        scratch_shapes=[pltpu.VMEM((t_seq, W_BRANCH), F32), pltpu.VMEM((H_C, DH_C, 2 * DH_C), F32)],
        compiler_params=_cparams(("arbitrary",)),
        name="mlstm",
    )(*args)


MERGE_TM = 512


def _merge_kernel(*refs):
    ctx_refs, lat_refs, gate_refs = refs[0:4], refs[4:8], refs[8:12]
    x_ref, mod_ref, wbr_ref, wout_ref, gn_ref, xo_ref, xn_ref = refs[12:19]
    xn3_ref = refs[19] if len(refs) > 19 else None
    is_ctx = pl.program_id(0) < R_CTX // MERGE_TM
    merged = None
    for g in range(N_BRANCH):
        yg = jnp.where(is_ctx, ctx_refs[g][...], lat_refs[g][...])
        pg = jnp.dot(yg, wbr_ref[g], preferred_element_type=F32)
        term = jax.nn.sigmoid(gate_refs[g][...].astype(F32)) * pg
        merged = term if merged is None else merged + term
    y = jnp.dot(merged.astype(BF16), wout_ref[...], preferred_element_type=F32)
    x = x_ref[...] + mod_ref[0, 2:3, :] * y
    xo_ref[...] = x
    r = lax.rsqrt(jnp.mean(x * x, axis=-1, keepdims=True) + EPS)
    xn = (x * r * gn_ref[...]) * (1.0 + mod_ref[0, 4:5, :]) + mod_ref[0, 3:4, :]
    xn_ref[...] = xn.astype(xn_ref.dtype)
    if xn3_ref is not None:
        _rows_to_tiles(xn3_ref, xn.astype(BF16))


def merge(ys_ctx, ys_lat, p, x, mod, wbr, wout, g_ffn, rows_as_tiles):
    tm = MERGE_TM
    n_ctx_tiles = R_CTX // tm
    br_ctx = pl.BlockSpec((tm, W_BRANCH), lambda i: (jnp.minimum(i, n_ctx_tiles - 1), 0))
    br_lat = pl.BlockSpec((tm, W_BRANCH), lambda i: (jnp.maximum(i - n_ctx_tiles, 0), 0))
    gate = lambda g: pl.BlockSpec((tm, D_MODEL), lambda i, g=g: (i, g))
    row = pl.BlockSpec((tm, D_MODEL), lambda i: (i, 0))
    out_specs = [row, row]
    out_shape = [jax.ShapeDtypeStruct((R_ALL, D_MODEL), F32), jax.ShapeDtypeStruct((R_ALL, D_MODEL), BF16)]
    if rows_as_tiles:
        out_specs.append(pl.BlockSpec((tm, ROW_SUB, LANES), lambda i: (i, 0, 0)))
        out_shape.append(jax.ShapeDtypeStruct((R_ALL, ROW_SUB, LANES), F32))
    return pl.pallas_call(
        _merge_kernel,
        grid=(R_ALL // tm,),
        in_specs=[br_ctx] * N_BRANCH + [br_lat] * N_BRANCH + [gate(0), gate(1), gate(2), gate(3), row,
                  pl.BlockSpec((1, 8, D_MODEL), lambda i: (_mod_row_of_tile(i, tm), 0, 0)),
                  pl.BlockSpec((N_BRANCH, W_BRANCH, D_MODEL), lambda i: (0, 0, 0)),
                  pl.BlockSpec((D_MODEL, D_MODEL), lambda i: (0, 0)),
                  pl.BlockSpec((1, D_MODEL), lambda i: (0, 0))],
        out_specs=tuple(out_specs),
        out_shape=tuple(out_shape),
        compiler_params=_cparams(("arbitrary",)),
        name="merge",
    )(*ys_ctx, *ys_lat, p, p, p, p, x, mod, wbr, wout, g_ffn.reshape(1, D_MODEL))


def _new_expert(te_ref, i):
    return jnp.logical_or(i == 0, te_ref[i] != te_ref[jnp.maximum(i - 1, 0)])


def _ffn_up_kernel(te_ref, nt_ref, x_ref, wg_ref, wu_ref, h_ref, wgb_ref, wub_ref):
    i = pl.program_id(1)

    @pl.when(_new_expert(te_ref, i))
    def _():
        wgb_ref[...] = wg_ref[0].astype(BF16)
        wub_ref[...] = wu_ref[0].astype(BF16)

    @pl.when(i < nt_ref[0])
    def _():
        x = x_ref[...]
        g = jnp.dot(x, wgb_ref[...], preferred_element_type=F32)
        u = jnp.dot(x, wub_ref[...], preferred_element_type=F32)
        h_ref[...] = (g * jax.nn.sigmoid(g) * u).astype(h_ref.dtype)

    @pl.when(i >= nt_ref[0])
    def _():
        h_ref[...] = jnp.zeros(h_ref.shape, h_ref.dtype)


def ffn_up(tile_expert, n_tiles, xs, wg, wu, tm, tf, weight_buffers):
    r, d = xs.shape
    f = wg.shape[2]
    w_spec = pl.BlockSpec((1, d, tf), lambda j, i, te, nt: (te[i], 0, j), pipeline_mode=pl.Buffered(weight_buffers))
    return pl.pallas_call(
        _ffn_up_kernel,
        grid_spec=pltpu.PrefetchScalarGridSpec(
            num_scalar_prefetch=2,
            grid=(f // tf, r // tm),
            in_specs=[pl.BlockSpec((tm, d), lambda j, i, te, nt: (i, 0)), w_spec, w_spec],
            out_specs=pl.BlockSpec((tm, tf), lambda j, i, te, nt: (i, j)),
            scratch_shapes=[pltpu.VMEM((d, tf), BF16), pltpu.VMEM((d, tf), BF16)]),
        out_shape=jax.ShapeDtypeStruct((r, f), BF16),
        compiler_params=_cparams(("arbitrary", "arbitrary")),
        name="ffn_up",
    )(tile_expert, n_tiles, xs, wg, wu)


def _ffn_down_kernel(te_ref, nt_ref, h_ref, wd_ref, y_ref, wdb_ref):
    i = pl.program_id(0)

    @pl.when(_new_expert(te_ref, i))
    def _():
        wdb_ref[...] = wd_ref[0].astype(BF16)

    @pl.when(i < nt_ref[0])
    def _():
        _rows_to_tiles(y_ref, jnp.dot(h_ref[...], wdb_ref[...], preferred_element_type=F32))

    @pl.when(i >= nt_ref[0])
    def _():
        y_ref[...] = jnp.zeros(y_ref.shape, y_ref.dtype)


def ffn_down(tile_expert, n_tiles, h, wd, tm):
    r, f = h.shape
    d = wd.shape[2]
    return pl.pallas_call(
        _ffn_down_kernel,
        grid_spec=pltpu.PrefetchScalarGridSpec(
            num_scalar_prefetch=2,
            grid=(r // tm,),
            in_specs=[pl.BlockSpec((tm, f), lambda i, te, nt: (i, 0)),
                      pl.BlockSpec((1, f, d), lambda i, te, nt: (te[i], 0, 0), pipeline_mode=pl.Buffered(1))],
            out_specs=pl.BlockSpec((tm, ROW_SUB, LANES), lambda i, te, nt: (i, 0, 0)),
            scratch_shapes=[pltpu.VMEM((f, d), BF16)]),
        out_shape=jax.ShapeDtypeStruct((r, ROW_SUB, LANES), F32),
        compiler_params=_cparams(("arbitrary",)),
        name="ffn_down",
    )(tile_expert, n_tiles, h, wd)


def _ffn_down_res_kernel(h_ref, wd_ref, x_ref, mod_ref, y_ref, wdb_ref):
    @pl.when(pl.program_id(0) == 0)
    def _():
        wdb_ref[...] = wd_ref[...].astype(BF16)

    y = jnp.dot(h_ref[...], wdb_ref[...], preferred_element_type=F32)
    y_ref[...] = x_ref[...] + mod_ref[0, 5:6, :] * y


def ffn_down_residual(h, wd, x, mod):
    tm = 1024
    r, f = h.shape
    d = wd.shape[1]
    return pl.pallas_call(
        _ffn_down_res_kernel,
        grid=(r // tm,),
        in_specs=[pl.BlockSpec((tm, f), lambda i: (i, 0)),
                  pl.BlockSpec((f, d), lambda i: (0, 0), pipeline_mode=pl.Buffered(1)),
                  pl.BlockSpec((tm, d), lambda i: (i, 0)),
                  pl.BlockSpec((1, 8, d), lambda i: (_mod_row_of_tile(i, tm), 0, 0))],
        out_specs=pl.BlockSpec((tm, d), lambda i: (i, 0)),
        out_shape=jax.ShapeDtypeStruct((r, d), F32),
        scratch_shapes=[pltpu.VMEM((f, d), BF16)],
        compiler_params=_cparams(("arbitrary",)),
        name="ffn_down_residual",
    )(h, wd, x, mod)


def dense_swiglu_residual(xn, x, mod, wg, wu, wd):
    t = xn.shape[0]
    tm = 1024
    n_tiles = t // tm
    te = jnp.zeros((n_tiles,), jnp.int32)
    nt = jnp.full((1,), n_tiles, jnp.int32)
    h = ffn_up(te, nt, xn, wg[None], wu[None], tm=tm, tf=wg.shape[1] // 2, weight_buffers=1)
    return ffn_down_residual(h, wd, x, mod)


MOE_TM = 1024
DISPATCH_TM = 512


def _moe_routing(logits, tm):
    t = logits.shape[0]
    n_assign = t * TOP_K
    top_v, top_i = lax.top_k(logits, TOP_K)
    gate = jax.nn.softmax(top_v, axis=-1)
    flat_e = top_i.reshape(-1).astype(jnp.int32)
    onehot = (flat_e[:, None] == jnp.arange(N_EXP, dtype=jnp.int32)[None, :])
    blk = LANES
    oh = onehot.astype(F32).reshape(n_assign // blk, blk, N_EXP)
    tril = jnp.tril(jnp.ones((blk, blk), F32))
    within = jnp.einsum('ij,bjk->bik', tril, oh)
    blk_tot = within[:, -1, :]
    blk_off = jnp.cumsum(blk_tot, axis=0) - blk_tot
    csum = (within + blk_off[:, None, :]).reshape(n_assign, N_EXP)
    rank = jnp.sum(jnp.where(onehot, csum - 1.0, 0.0), axis=1).astype(jnp.int32)
    counts = csum[-1].astype(jnp.int32)
    padded = (counts + tm - 1) // tm * tm
    grp_start = jnp.cumsum(padded) - padded
    raw_start = jnp.cumsum(counts) - counts
    slot_of_assign = jnp.sum(jnp.where(onehot, grp_start[None, :], 0), axis=1) + rank

    r_max = n_assign + N_EXP * tm
    tile_start = jnp.arange(r_max // tm, dtype=jnp.int32) * tm
    tile_expert = jnp.sum((tile_start[:, None] >= (grp_start + padded)[None, :]).astype(jnp.int32), axis=1)
    tile_expert = jnp.minimum(tile_expert, N_EXP - 1).astype(jnp.int32)
    n_tiles = (jnp.sum(padded) // tm).astype(jnp.int32).reshape(1)

    order = jnp.argsort(flat_e, stable=True).astype(jnp.int32)
    e_slot = jnp.repeat(tile_expert, tm)
    j = jnp.arange(r_max, dtype=jnp.int32) - grp_start[e_slot]
    src = jnp.clip(raw_start[e_slot] + j, 0, n_assign - 1)
    tok_of_slot = jnp.where(j < counts[e_slot], order[src] // TOP_K, 0)
    return gate, slot_of_assign, tok_of_slot, tile_expert, n_tiles


ROW_SUB = D_MODEL // LANES


def _rows_to_tiles(o3_ref, x):
    for j in range(ROW_SUB):
        o3_ref[:, j, :] = x[:, j * LANES:(j + 1) * LANES].astype(o3_ref.dtype)


def _start_row_gather(idx_ref, src_ref, dst, sem, n_rows):
    def body(q, carry):
        for u in range(2):
            r = 2 * q + u
            pltpu.make_async_copy(src_ref.at[idx_ref[0, 0, r]], dst.at[r], sem).start(priority=u)
        return carry

    lax.fori_loop(0, n_rows // 2, body, 0, unroll=4)


def _wait_row_gather(src_ref, dst, sem, n_rows):
    pltpu.make_async_copy(src_ref.at[pl.ds(0, n_rows)], dst, sem).wait()


def _tiles_to_rows(tiles, rows_ref):
    for j in range(ROW_SUB):
        rows_ref[:, j * LANES:(j + 1) * LANES] = tiles[:, j, :]


def _dispatch_kernel(nt_ref, idx_ref, idx_next_ref, src_ref, o_ref, buf, rows, sem):
    i = pl.program_id(0)
    nt = nt_ref[0]
    tm = o_ref.shape[0]
    slot = i % 2

    @pl.when(i == 0)
    def _():
        _start_row_gather(idx_ref, src_ref, buf.at[0], sem.at[0], tm)

    @pl.when(i + 1 < nt)
    def _():
        _start_row_gather(idx_next_ref, src_ref, buf.at[1 - slot], sem.at[1 - slot], tm)

    @pl.when(i < nt)
    def _():
        _wait_row_gather(src_ref, buf.at[slot], sem.at[slot], tm)
        _tiles_to_rows(buf.at[slot], rows)
        o_ref[...] = rows[...].astype(o_ref.dtype)

    @pl.when(i >= nt)
    def _():
        o_ref[...] = jnp.zeros(o_ref.shape, o_ref.dtype)


def moe_dispatch(n_tiles, tok_of_slot, xn3, tm):
    r = tok_of_slot.shape[0]
    last = r // tm - 1
    idx = tok_of_slot.reshape(r // tm, 1, tm)
    return pl.pallas_call(
        _dispatch_kernel,
        grid_spec=pltpu.PrefetchScalarGridSpec(
            num_scalar_prefetch=1,
            grid=(r // tm,),
            in_specs=[pl.BlockSpec((1, 1, tm), lambda i, nt: (i, 0, 0), memory_space=pltpu.SMEM),
                      pl.BlockSpec((1, 1, tm), lambda i, nt: (jnp.minimum(i + 1, last), 0, 0),
                                   memory_space=pltpu.SMEM),
                      pl.BlockSpec(memory_space=pl.ANY)],
            out_specs=pl.BlockSpec((tm, D_MODEL), lambda i, nt: (i, 0)),
            scratch_shapes=[pltpu.VMEM((2, tm, ROW_SUB, LANES), F32), pltpu.VMEM((tm, D_MODEL), F32),
                            pltpu.SemaphoreType.DMA((2,))]),
        out_shape=jax.ShapeDtypeStruct((r, D_MODEL), BF16),
        compiler_params=_cparams(("arbitrary",)),
        name="moe_dispatch",
    )(n_tiles, idx, idx, xn3)


COMBINE_TM = 256


def _combine_kernel(idx_ref, idx_next_ref, ys_ref, x_ref, gate_ref, mod_ref, o_ref, buf, rows, sem):
    i = pl.program_id(0)
    tm = o_ref.shape[0]
    n_rows = TOP_K * tm
    slot = i % 2

    @pl.when(i == 0)
    def _():
        _start_row_gather(idx_ref, ys_ref, buf.at[0], sem.at[0], n_rows)

    @pl.when(i + 1 < pl.num_programs(0))
    def _():
        _start_row_gather(idx_next_ref, ys_ref, buf.at[1 - slot], sem.at[1 - slot], n_rows)

    _wait_row_gather(ys_ref, buf.at[slot], sem.at[slot], n_rows)
    _tiles_to_rows(buf.at[slot], rows)
    f = gate_ref[:, 0:1] * rows[0:tm, :] + gate_ref[:, 1:2] * rows[tm:2 * tm, :]
    o_ref[...] = x_ref[...] + mod_ref[0, 5:6, :] * f


def moe_combine(slot_of_assign, ys3, x, gate, mod):
    t, d = x.shape
    tm = COMBINE_TM
    idx = slot_of_assign.reshape(t // tm, tm, TOP_K).transpose(0, 2, 1).reshape(t // tm, 1, TOP_K * tm)
    last = t // tm - 1
    return pl.pallas_call(
        _combine_kernel,
        grid=(t // tm,),
        in_specs=[pl.BlockSpec((1, 1, TOP_K * tm), lambda i: (i, 0, 0), memory_space=pltpu.SMEM),
                  pl.BlockSpec((1, 1, TOP_K * tm), lambda i: (jnp.minimum(i + 1, last), 0, 0),
                               memory_space=pltpu.SMEM),
                  pl.BlockSpec(memory_space=pl.ANY),
                  pl.BlockSpec((tm, d), lambda i: (i, 0)),
                  pl.BlockSpec((tm, TOP_K), lambda i: (i, 0)),
                  pl.BlockSpec((1, 8, d), lambda i: (_mod_row_of_tile(i, tm), 0, 0))],
        out_specs=pl.BlockSpec((tm, d), lambda i: (i, 0)),
        out_shape=jax.ShapeDtypeStruct((t, d), F32),
        scratch_shapes=[pltpu.VMEM((2, TOP_K * tm, ROW_SUB, LANES), F32), pltpu.VMEM((TOP_K * tm, d), F32),
                        pltpu.SemaphoreType.DMA((2,))],
        compiler_params=_cparams(("arbitrary",)),
        name="moe_combine",
    )(idx, idx, ys3, x, gate, mod)


def moe_swiglu_residual(xn3, xn, x, mod, w_router, wg, wu, wd):
    tm = MOE_TM
    logits = jnp.dot(xn.astype(F32), w_router, precision=lax.Precision.HIGHEST)
    gate, slot_of_assign, tok_of_slot, tile_expert, n_tiles = _moe_routing(logits, tm)
    xs = moe_dispatch(n_tiles * (tm // DISPATCH_TM), tok_of_slot, xn3, DISPATCH_TM)
    h = ffn_up(tile_expert, n_tiles, xs, wg, wu, tm=tm, tf=wg.shape[2] // 4, weight_buffers=2)
    ys3 = ffn_down(tile_expert, n_tiles, h, wd, tm=tm)
    return moe_combine(slot_of_assign, ys3, x, gate, mod)


def _layer(x, cond, lp, l, ctx, tabs_a, tabs_d):
    mod = modulation(cond, lp['w_mod'], lp['b_mod']).reshape(cond.shape[0], 6, D_MODEL)
    mod = jnp.pad(mod, ((0, 0), (0, 2), (0, 0)))
    p = in_proj(x, mod, lp['g_norm_mix'], pack_w_in(lp['w_in']))

    mw = pack_mla_weights(lp)
    q_a, k_a, v_a, ckv, kr = mla_prep(p, mw, tabs_a)
    kr_c = jnp.pad(ctx['mla_krope'].reshape(-1, ROPE_A), ((0, 0), (0, LANES - ROPE_A)))
    kc_a, vc_a = mla_prep_cache(ctx['mla_ckv'].reshape(-1, KV_RANK), kr_c, mw, tabs_a)
    ya_c = mla_attention(q_a, k_a, v_a, None, None, row0=0, n_seq=N_CTX_SEQ, t_seq=T_CTX, tq=T_CTX)
    ya_l = mla_attention(q_a, k_a, v_a, kc_a, vc_a, row0=R_CTX, n_seq=N_LAT_SEQ, t_seq=T_LAT, tq=ATTN_TQ)

    rw = pack_rglru_weights(lp)
    yb_c, st_rg = rglru(p, rw, jnp.zeros((N_CTX_SEQ, 2, W_B), F32), row0=0, n_seq=N_CTX_SEQ, t_seq=T_CTX)
    yb_l, _ = rglru(p, rw, ctx['rglru'], row0=R_CTX, n_seq=N_LAT_SEQ, t_seq=T_LAT)

    bias_sm = jnp.zeros((LANES,), F32).at[SM_MI:SM_MI + 2 * H_C].set(lp['b_ml_i'].reshape(-1))
    bias_sm = bias_sm.at[SM_MF:SM_MF + 2 * H_C].set(lp['b_ml_f'].reshape(-1)).reshape(1, LANES)
    c0_ctx = jnp.zeros((N_CTX_SEQ, 2, H_C, DH_C, 2 * DH_C), F32)
    m0_ctx = jnp.zeros((N_CTX_SEQ, 1, 2 * H_C), F32)
    c0_lat = jnp.concatenate([ctx['mlstm_C'], ctx['mlstm_n'][..., None],
                              jnp.zeros(ctx['mlstm_n'].shape + (DH_C - 1,), F32)], axis=-1)
    m0_lat = ctx['mlstm_m'].reshape(N_LAT_SEQ, 1, 2 * H_C)
    yc_c, c_fin, m_fin = mlstm(p, bias_sm, lp['g_ml_out'], c0_ctx, m0_ctx, row0=0, n_seq=N_CTX_SEQ, t_seq=T_CTX)
    yc_l, _, _ = mlstm(p, bias_sm, lp['g_ml_out'], c0_lat, m0_lat, row0=R_CTX, n_seq=N_LAT_SEQ, t_seq=T_LAT)

    lambda_init = 0.8 - 0.6 * math.exp(-0.3 * l)
    qd, kd_own, kd_plain = diff_prep(p, lp, tabs_d)
    dkc = ctx['diff_k'].reshape(-1, W_BRANCH)
    dvc = ctx['diff_v'].reshape(-1, W_BRANCH)
    yd_c = diff_attention(lp['diff_lambda'], qd, kd_own, p, None, None, lp['g_diff_sub'],
                          row0=0, n_seq=N_CTX_SEQ, t_seq=T_CTX, tq=T_CTX, lambda_init=lambda_init)
    yd_l = diff_attention(lp['diff_lambda'], qd, kd_own, p, dkc, dvc, lp['g_diff_sub'],
                          row0=R_CTX, n_seq=N_LAT_SEQ, t_seq=T_LAT, tq=ATTN_TQ, lambda_init=lambda_init)

    merged = merge((ya_c, yb_c, yc_c, yd_c), (ya_l, yb_l, yc_l, yd_l), p, x, mod, lp['w_br'].astype(BF16),
                   lp['w_out'].astype(BF16), lp['g_norm_ffn'], rows_as_tiles='moe' in lp)
    if 'ffn' in lp:
        x, xn = merged
        x = dense_swiglu_residual(xn, x, mod, *lp['ffn'])
    else:
        x, xn, xn3 = merged
        x = moe_swiglu_residual(xn3, xn, x, mod, *lp['moe'])

    dv0 = _seg_block('dv') * W_BRANCH
    ctx_out = (ckv[:R_CTX].reshape(N_CTX_SEQ, T_CTX, KV_RANK),
               kr[:R_CTX, :ROPE_A].reshape(N_CTX_SEQ, T_CTX, ROPE_A),
               kd_plain[:R_CTX].reshape(N_CTX_SEQ, T_CTX, 2, H_D, DH_D),
               p[:R_CTX, dv0:dv0 + W_BRANCH].astype(F32).reshape(N_CTX_SEQ, T_CTX, H_D, 2 * DH_D),
               st_rg,
               c_fin[..., :DH_C],
               c_fin[..., DH_C],
               m_fin.reshape(N_CTX_SEQ, 2, H_C))
    return x, ctx_out


def kernel(x_prompt, x_sample, cache_mla_ckv, cache_mla_krope, cache_diff_k, cache_diff_v,
           state_rglru, state_mlstm_C, state_mlstm_n, state_mlstm_m, c, c_ctx,
           w_mod, b_mod, g_norm_mix, g_norm_ffn, w_in, g_mla_qlat, w_mla_uq, g_mla_kvlat, w_mla_ukv,
           g_mla_qn, g_mla_kn, w_conv_rg, b_conv_rg, w_rg_a, b_rg_a, w_rg_x, b_rg_x, rg_lambda,
           b_ml_i, b_ml_f, g_ml_out, g_diff_qn, g_diff_kn, diff_lambda, g_diff_sub, w_br, w_out,
           w_ffn_gate, w_ffn_up, w_ffn_down, w_router, w_moe_gate, w_moe_up, w_moe_down):
    assert x_prompt.shape == (N_CTX_SEQ, T_CTX, D_MODEL) and x_sample.shape == (N_LAT_SEQ, T_LAT, D_MODEL)
    tabs_a = _rope_tables(ROPE_A, (NOPE_A,), MLA_TM)
    tabs_d = _rope_tables(DH_D, (0, DH_D), DIFF_TM)
    cond = jnp.concatenate([c_ctx.reshape(1, D_MODEL), c, jnp.zeros((16 - 1 - N_LAT_SEQ, D_MODEL), F32)], axis=0)
    x = jnp.concatenate([x_prompt.reshape(R_CTX, D_MODEL), x_sample.reshape(R_LAT, D_MODEL)], axis=0)
    new = []
    for l in range(DEPTH):
        lp = dict(w_mod=w_mod[l], b_mod=b_mod[l], g_norm_mix=g_norm_mix[l], g_norm_ffn=g_norm_ffn[l], w_in=w_in[l],
                  g_mla_qlat=g_mla_qlat[l], w_mla_uq=w_mla_uq[l], g_mla_kvlat=g_mla_kvlat[l], w_mla_ukv=w_mla_ukv[l],
                  g_mla_qn=g_mla_qn[l], g_mla_kn=g_mla_kn[l], w_conv_rg=w_conv_rg[l], b_conv_rg=b_conv_rg[l],
                  w_rg_a=w_rg_a[l], b_rg_a=b_rg_a[l], w_rg_x=w_rg_x[l], b_rg_x=b_rg_x[l], rg_lambda=rg_lambda[l],
                  b_ml_i=b_ml_i[l], b_ml_f=b_ml_f[l], g_ml_out=g_ml_out[l], g_diff_qn=g_diff_qn[l],
                  g_diff_kn=g_diff_kn[l], diff_lambda=diff_lambda[l], g_diff_sub=g_diff_sub[l],
                  w_br=w_br[l], w_out=w_out[l])
        if l % 2 == 0:
            lp['ffn'] = (w_ffn_gate[l // 2], w_ffn_up[l // 2], w_ffn_down[l // 2])
        else:
            lp['moe'] = (w_router[l // 2], w_moe_gate[l // 2], w_moe_up[l // 2], w_moe_down[l // 2])
        ctx_l = dict(mla_ckv=cache_mla_ckv[:, l], mla_krope=cache_mla_krope[:, l], diff_k=cache_diff_k[:, l],
                     diff_v=cache_diff_v[:, l], rglru=state_rglru[:, l], mlstm_C=state_mlstm_C[:, l],
                     mlstm_n=state_mlstm_n[:, l], mlstm_m=state_mlstm_m[:, l])
        x, st = _layer(x, cond, lp, l, ctx_l, tabs_a, tabs_d)
        new.append(st)
    outs = tuple(jnp.stack([s[i] for s in new], axis=1) for i in range(8))
    return (x[:R_CTX].reshape(N_CTX_SEQ, T_CTX, D_MODEL), x[R_CTX:].reshape(N_LAT_SEQ, T_LAT, D_MODEL)) + outs
```

```python
import functools
import math

import jax
import jax.numpy as jnp
import numpy as np
from jax import lax
from jax.experimental import pallas as pl
from jax.experimental.pallas import tpu as pltpu

D_MODEL = 1024
DEPTH = 2
GRID_W = 64
ROPE_BASE = 10000.0
EPS = 1e-6
N_BRANCH = 4
W_BRANCH = D_MODEL // 2

H_A = 8
NOPE_A = 64
ROPE_A = 32
V_A = W_BRANCH // H_A
Q_RANK = D_MODEL // 4
KV_RANK = D_MODEL // 8
MLA_SCALE = (NOPE_A + ROPE_A) ** -0.5

W_B = W_BRANCH
NB_B = 8
BW_B = W_B // NB_B
CONV_W = 4
RG_C = 8.0

H_C = 4
DH_C = W_BRANCH // H_C
ML_CHUNK = 128

H_D = 4
DH_D = W_BRANCH // (2 * H_D)
DIFF_SCALE = DH_D ** -0.5

N_EXP = 8
TOP_K = 2

V7X_VMEM_LIMIT_BYTES = 56 * 1024 * 1024
LANES = 128
SUBLANES = 8

BF16 = jnp.bfloat16
F32 = jnp.float32

P_GATE = 0
P_SEG = N_BRANCH * D_MODEL
SEG_NAMES = ('rg_x', 'rg_gate', 'mq', 'mk', 'mv', 'mo', 'dq', 'dk', 'dv')
P_QLAT = P_SEG + 9 * W_BRANCH
P_KVLAT = P_QLAT + Q_RANK
P_SMALL = P_KVLAT + KV_RANK
P_WIDTH = P_SMALL + LANES
SM_MI = ROPE_A
SM_MF = ROPE_A + 2 * H_C


def _seg_block(name):
    return (P_SEG + SEG_NAMES.index(name) * W_BRANCH) // W_BRANCH


N_CTX_SEQ, T_CTX = 16, 256
N_LAT_SEQ, T_LAT = 8, 1024
PAST_LEN = 512
R_CTX = N_CTX_SEQ * T_CTX
R_LAT = N_LAT_SEQ * T_LAT
R_ALL = R_CTX + R_LAT


def _cparams(sem):
    return pltpu.CompilerParams(dimension_semantics=sem, vmem_limit_bytes=V7X_VMEM_LIMIT_BYTES)


def _mod_row_of_tile(i, tm):
    n_ctx_tiles = R_CTX // tm
    per_seq = T_LAT // tm
    return jnp.where(i < n_ctx_tiles, 0, 1 + (i - n_ctx_tiles) // per_seq)


def _pos_block_of_tile(i, tm):
    n_ctx_tiles = R_CTX // tm
    per_seq = T_LAT // tm
    return jnp.where(i < n_ctx_tiles, per_seq, (i - n_ctx_tiles) % per_seq)


def _mod_kernel(c_ref, w_ref, b_ref, o_ref):
    c = c_ref[...]
    s = (c * jax.nn.sigmoid(c)).astype(BF16)
    o_ref[...] = jnp.dot(s, w_ref[...].astype(BF16), preferred_element_type=F32) + b_ref[...]


def modulation(cond, w_mod, b_mod):
    m, d = cond.shape
    n = w_mod.shape[1]
    tn = 1536
    return pl.pallas_call(
        _mod_kernel,
        grid=(n // tn,),
        in_specs=[pl.BlockSpec((m, d), lambda j: (0, 0)),
                  pl.BlockSpec((d, tn), lambda j: (0, j)),
                  pl.BlockSpec((1, tn), lambda j: (0, j))],
        out_specs=pl.BlockSpec((m, tn), lambda j: (0, j)),
        out_shape=jax.ShapeDtypeStruct((m, n), F32),
        compiler_params=_cparams(("arbitrary",)),
        name="modulation",
    )(cond, w_mod, b_mod.reshape(1, n))


def _in_proj_kernel(x_ref, mod_ref, g_ref, w_ref, o_ref, xn_ref):
    @pl.when(pl.program_id(1) == 0)
    def _():
        x = x_ref[...]
        r = lax.rsqrt(jnp.mean(x * x, axis=-1, keepdims=True) + EPS)
        sh = mod_ref[0, 0:1, :]
        sc = mod_ref[0, 1:2, :]
        xn_ref[...] = ((x * r * g_ref[...]) * (1.0 + sc) + sh).astype(BF16)

    o_ref[...] = jnp.dot(xn_ref[...], w_ref[...], preferred_element_type=F32).astype(o_ref.dtype)


def in_proj(x, mod, g, w_p):
    tm, tn = 1024, P_WIDTH // 4
    m, d = x.shape
    n = w_p.shape[1]
    return pl.pallas_call(
        _in_proj_kernel,
        grid=(m // tm, n // tn),
        in_specs=[pl.BlockSpec((tm, d), lambda i, j: (i, 0)),
                  pl.BlockSpec((1, 8, d), lambda i, j: (_mod_row_of_tile(i, tm), 0, 0)),
                  pl.BlockSpec((1, d), lambda i, j: (0, 0)),
                  pl.BlockSpec((d, tn), lambda i, j: (0, j))],
        out_specs=pl.BlockSpec((tm, tn), lambda i, j: (i, j)),
        out_shape=jax.ShapeDtypeStruct((m, n), BF16),
        scratch_shapes=[pltpu.VMEM((tm, d), BF16)],
        compiler_params=_cparams(("arbitrary", "arbitrary")),
        name="in_proj",
    )(x, mod, g.reshape(1, d), w_p)


def pack_w_in(w_in):
    o_rg = Q_RANK + KV_RANK + ROPE_A
    o_mi = o_rg + 6 * W_BRANCH
    o_dq = o_mi + 4 * H_C
    o_gate = o_dq + 3 * W_BRANCH
    pad = jnp.zeros((w_in.shape[0], LANES - ROPE_A - 4 * H_C), w_in.dtype)
    parts = [w_in[:, o_gate:], w_in[:, o_rg:o_mi], w_in[:, o_dq:o_gate], w_in[:, :Q_RANK + KV_RANK],
             w_in[:, Q_RANK + KV_RANK:o_rg], w_in[:, o_mi:o_dq], pad]
    return jnp.concatenate(parts, axis=1).astype(BF16)


def _rope_partner(rot_dim, lane_starts):
    nf = rot_dim // 4
    partner = np.arange(LANES)
    for s0 in lane_starts:
        for a in range(2):
            lo = s0 + a * 2 * nf
            partner[lo:lo + nf] = np.arange(lo + nf, lo + 2 * nf)
            partner[lo + nf:lo + 2 * nf] = np.arange(lo, lo + nf)
    return partner


def _rope_tables(rot_dim, lane_starts, tm):
    rows = T_LAT // GRID_W
    r, c = np.meshgrid(np.arange(rows, dtype=np.float32), np.arange(GRID_W, dtype=np.float32), indexing='ij')
    nf = rot_dim // 4
    inv = (np.float32(ROPE_BASE) ** (-np.arange(nf, dtype=np.float32) / np.float32(nf))).astype(np.float32)
    ang = np.stack([r.reshape(-1)[:, None] * inv, c.reshape(-1)[:, None] * inv], axis=1).astype(np.float32)
    cos, sin = np.cos(ang).astype(np.float32), np.sin(ang).astype(np.float32)
    tc = np.ones((T_LAT + tm, LANES), np.float32)
    ta = np.zeros((T_LAT + tm, LANES), np.float32)
    tb = np.zeros((T_LAT + tm, LANES), np.float32)
    for s0 in lane_starts:
        for a in range(2):
            lo = s0 + a * 2 * nf
            tc[:T_LAT, lo:lo + nf] = cos[:, a]
            tc[:T_LAT, lo + nf:lo + 2 * nf] = cos[:, a]
            ta[:T_LAT, lo:lo + nf] = -sin[:, a]
            tb[:T_LAT, lo + nf:lo + 2 * nf] = sin[:, a]
    return jnp.asarray(tc), jnp.asarray(ta + tb)


MLA_TM = 512
ATTN_TQ = 512
MLA_HEADS_PER_STEP = 8
QK_A = NOPE_A + ROPE_A


def _mla_prep_kernel(*refs, has_q, norm_ckv):
    if has_q:
        (qlat_ref, gq_ref, wuq_ref, gqn_ref, kv_ref, sm_ref, gkv_ref, wkc_ref, wv_ref, gkn_ref,
         c_ref, s_ref, q_o, k_o, v_o, ckv_o, kr_o) = refs
        c, sn = c_ref[...], s_ref[...]
    else:
        (kv_ref, sm_ref, gkv_ref, wkc_ref, wv_ref, gkn_ref, k_o, v_o) = refs

    def heads(z, g_ref, o_ref, scale):
        for h in range(H_A):
            s = z[:, h * LANES:(h + 1) * LANES]
            r = lax.rsqrt(jnp.sum(s * s, axis=-1, keepdims=True) * (1.0 / QK_A) + EPS)
            y = s * r * g_ref[0:1, :]
            if has_q:
                sw = z[:, (H_A + h) * LANES:(H_A + h + 1) * LANES]
                y = y * c + (sw * r * g_ref[1:2, :]) * sn
            if scale != 1.0:
                y = y * scale
            o_ref[:, h * LANES:(h + 1) * LANES] = y.astype(o_ref.dtype)

    if has_q:
        ql = qlat_ref[...].astype(F32)
        qn = ql * lax.rsqrt(jnp.mean(ql * ql, axis=-1, keepdims=True) + EPS) * gq_ref[...]
        q = jnp.dot(qn.astype(BF16), wuq_ref[...], preferred_element_type=F32)
        heads(q, gqn_ref, q_o, MLA_SCALE)

    kv = kv_ref[...].astype(F32)
    if norm_ckv:
        ckv = kv * lax.rsqrt(jnp.mean(kv * kv, axis=-1, keepdims=True) + EPS) * gkv_ref[...]
    else:
        ckv = kv
    sm = sm_ref[...]
    ckv_b = ckv.astype(BF16)
    kin = jnp.concatenate([ckv_b, sm.astype(BF16)], axis=1)
    wkc = wkc_ref[...] if has_q else wkc_ref[:, :H_A * LANES]
    k = jnp.dot(kin, wkc, preferred_element_type=F32)
    heads(k, gkn_ref, k_o, 1.0)
    v_o[...] = jnp.dot(ckv_b, wv_ref[...], preferred_element_type=F32).astype(v_o.dtype)
    if has_q:
        ckv_o[...] = ckv
        kr_o[...] = sm.astype(F32)


def pack_mla_weights(lp):
    wuq = lp['w_mla_uq'].reshape(Q_RANK, H_A, QK_A)
    wuq_p = jnp.pad(wuq, ((0, 0), (0, 0), (0, LANES - QK_A))).reshape(Q_RANK, H_A * LANES).astype(BF16)
    wukv = lp['w_mla_ukv'].reshape(KV_RANK, H_A, NOPE_A + V_A)
    wk = jnp.pad(wukv[:, :, :NOPE_A], ((0, 0), (0, 0), (0, LANES - NOPE_A))).reshape(KV_RANK, H_A * LANES)
    place = np.zeros((LANES, H_A, LANES), np.float32)
    for h in range(H_A):
        place[np.arange(ROPE_A), h, NOPE_A + np.arange(ROPE_A)] = 1.0
    wkc = jnp.concatenate([wk, jnp.asarray(place.reshape(LANES, H_A * LANES))], axis=0).astype(BF16)
    wv = wukv[:, :, NOPE_A:]
    wv_even = jnp.pad(wv, ((0, 0), (0, 0), (0, LANES - V_A)))
    wv_odd = jnp.pad(wv, ((0, 0), (0, 0), (LANES - V_A, 0)))
    odd = (np.arange(H_A) % 2 == 1)[None, :, None]
    wv_p = jnp.where(odd, wv_odd, wv_even).reshape(KV_RANK, H_A * LANES).astype(BF16)
    partner = _rope_partner(ROPE_A, (NOPE_A,))
    cols = (np.arange(H_A)[:, None] * LANES + partner[None, :]).reshape(-1)
    with_partner = lambda w: jnp.concatenate([w, w[:, cols]], axis=1)
    pad_g = lambda g: jnp.pad(g, (0, LANES - QK_A))
    gain2 = lambda g: jnp.stack([pad_g(g), pad_g(g)[partner]])
    return dict(wuq=with_partner(wuq_p), wkc=with_partner(wkc), wv=wv_p,
                gqn=gain2(lp['g_mla_qn']), gkn=gain2(lp['g_mla_kn']),
                gq=lp['g_mla_qlat'].reshape(1, Q_RANK), gkv=lp['g_mla_kvlat'].reshape(1, KV_RANK))


def mla_prep(p, mw, tabs):
    tm = MLA_TM
    n = R_ALL // tm
    full = lambda shape: pl.BlockSpec(shape, lambda i: (0, 0))
    tab = pl.BlockSpec((tm, LANES), lambda i: (_pos_block_of_tile(i, tm), 0))
    wide = H_A * LANES
    out_shape = (jax.ShapeDtypeStruct((R_ALL, wide), BF16), jax.ShapeDtypeStruct((R_ALL, wide), BF16),
                 jax.ShapeDtypeStruct((R_ALL, wide), BF16), jax.ShapeDtypeStruct((R_ALL, KV_RANK), F32),
                 jax.ShapeDtypeStruct((R_ALL, LANES), F32))
    row = lambda w: pl.BlockSpec((tm, w), lambda i: (i, 0))
    return pl.pallas_call(
        functools.partial(_mla_prep_kernel, has_q=True, norm_ckv=True),
        grid=(n,),
        in_specs=[pl.BlockSpec((tm, Q_RANK), lambda i: (i, P_QLAT // Q_RANK)), full((1, Q_RANK)),
                  full((Q_RANK, 2 * wide)), full((2, LANES)),
                  pl.BlockSpec((tm, KV_RANK), lambda i: (i, P_KVLAT // KV_RANK)),
                  pl.BlockSpec((tm, LANES), lambda i: (i, P_SMALL // LANES)), full((1, KV_RANK)),
                  full((2 * LANES, 2 * wide)), full((KV_RANK, wide)), full((2, LANES)), tab, tab],
        out_specs=(row(wide), row(wide), row(wide), row(KV_RANK), row(LANES)),
        out_shape=out_shape,
        compiler_params=_cparams(("arbitrary",)),
        name="mla_prep",
    )(p, mw['gq'], mw['wuq'], mw['gqn'], p, p, mw['gkv'], mw['wkc'], mw['wv'], mw['gkn'], *tabs)


def mla_prep_cache(ckv_c, kr_c, mw):
    tm = MLA_TM
    r = ckv_c.shape[0]
    full = lambda shape: pl.BlockSpec(shape, lambda i: (0, 0))
    wide = H_A * LANES
    row = lambda w: pl.BlockSpec((tm, w), lambda i: (i, 0))
    return pl.pallas_call(
        functools.partial(_mla_prep_kernel, has_q=False, norm_ckv=False),
        grid=(r // tm,),
        in_specs=[row(KV_RANK), row(LANES), full((1, KV_RANK)), full((2 * LANES, 2 * wide)), full((KV_RANK, wide)),
                  full((2, LANES))],
        out_specs=(row(wide), row(wide)),
        out_shape=(jax.ShapeDtypeStruct((r, wide), BF16), jax.ShapeDtypeStruct((r, wide), BF16)),
        compiler_params=_cparams(("arbitrary",)),
        name="mla_prep_cache",
    )(ckv_c, kr_c, mw['gkv'], mw['wkc'], mw['wv'], mw['gkn'])


_NT = (((1,), (1,)), ((), ()))


def _mla_attn_kernel(*refs, has_cache):
    if has_cache:
        q_ref, ko_ref, vo_ref, kc_ref, vc_ref, o_ref = refs
    else:
        q_ref, ko_ref, vo_ref, o_ref = refs
    n_heads = q_ref.shape[1] // LANES
    sls = [slice(h * LANES, (h + 1) * LANES) for h in range(n_heads)]
    def scores(sl):
        s_o = lax.dot_general(q_ref[:, sl], ko_ref[:, sl], _NT, preferred_element_type=F32)
        s_c = lax.dot_general(q_ref[:, sl], kc_ref[:, sl], _NT, preferred_element_type=F32) if has_cache else None
        return s_o, s_c

    outs = []
    nxt = scores(sls[0])
    for h, sl in enumerate(sls):
        s_o, s_c = nxt
        if h + 1 < n_heads:
            nxt = scores(sls[h + 1])
        m = jnp.max(s_o, axis=-1, keepdims=True)
        if has_cache:
            m = jnp.maximum(m, jnp.max(s_c, axis=-1, keepdims=True))
        e_o = jnp.exp(s_o - m)
        l = jnp.sum(e_o, axis=-1, keepdims=True)
        pv = jnp.dot(e_o.astype(BF16), vo_ref[:, sl], preferred_element_type=F32)
        if has_cache:
            e_c = jnp.exp(s_c - m)
            l = l + jnp.sum(e_c, axis=-1, keepdims=True)
            pv = pv + jnp.dot(e_c.astype(BF16), vc_ref[:, sl], preferred_element_type=F32)
        outs.append(pv * (1.0 / l))
    for p in range(n_heads // 2):
        o_ref[:, p * LANES:(p + 1) * LANES] = (outs[2 * p] + outs[2 * p + 1]).astype(o_ref.dtype)


def mla_attention(q, k, v, kc, vc, *, row0, n_seq, t_seq, tq):
    has_cache = kc is not None
    wh = MLA_HEADS_PER_STEP * LANES
    n_grp = H_A // MLA_HEADS_PER_STEP
    nq = t_seq // tq
    qb0, kb0 = row0 // tq, row0 // t_seq
    in_specs = [pl.BlockSpec((tq, wh), lambda s, p, i: (qb0 + s * nq + i, p)),
                pl.BlockSpec((t_seq, wh), lambda s, p, i: (kb0 + s, p)),
                pl.BlockSpec((t_seq, wh), lambda s, p, i: (kb0 + s, p))]
    args = [q, k, v]
    if has_cache:
        in_specs += [pl.BlockSpec((PAST_LEN, wh), lambda s, p, i: (s, p)),
                     pl.BlockSpec((PAST_LEN, wh), lambda s, p, i: (s, p))]
        args += [kc, vc]
    return pl.pallas_call(
        functools.partial(_mla_attn_kernel, has_cache=has_cache),
        grid=(n_seq, n_grp, nq),
        in_specs=in_specs,
        out_specs=pl.BlockSpec((tq, wh // 2), lambda s, p, i: (s * nq + i, p)),
        out_shape=jax.ShapeDtypeStruct((n_seq * t_seq, W_BRANCH), BF16),
        compiler_params=_cparams(("arbitrary", "arbitrary", "arbitrary")),
        name="mla_attention",
    )(*args)


DIFF_TM = 512


def _diff_prep_kernel(dq_ref, dk_ref, gq_ref, gk_ref, perm_ref, c_ref, s_ref, q_o, ko_o, kp_o):
    c, sn = c_ref[...], s_ref[...]
    lane = lax.broadcasted_iota(jnp.int32, (1, LANES), 1)
    lo = lane < DH_D

    def inv_rms(x):
        x2 = x * x
        s_lo = jnp.sum(jnp.where(lo, x2, 0.0), axis=-1, keepdims=True)
        s_hi = jnp.sum(jnp.where(lo, 0.0, x2), axis=-1, keepdims=True)
        return lax.rsqrt(jnp.where(lo, s_lo, s_hi) * (1.0 / DH_D) + EPS)

    def rotated(n, x_sw, r, g_ref):
        return n * c + (x_sw * r * g_ref[1:2, :]) * sn

    for j in range(W_BRANCH // LANES):
        sl = slice(j * LANES, (j + 1) * LANES)
        xq, xk = dq_ref[:, sl], dk_ref[:, sl]
        q_sw = jnp.dot(xq, perm_ref[...], preferred_element_type=F32)
        k_sw = jnp.dot(xk, perm_ref[...], preferred_element_type=F32)
        xq, xk = xq.astype(F32), xk.astype(F32)
        rq, rk = inv_rms(xq), inv_rms(xk)
        qn = xq * rq * gq_ref[0:1, :]
        kn = xk * rk * gk_ref[0:1, :]
        q_o[:, sl] = (rotated(qn, q_sw, rq, gq_ref) * DIFF_SCALE).astype(q_o.dtype)
        kp_o[:, sl] = kn
        ko_o[:, sl] = rotated(kn, k_sw, rk, gk_ref).astype(ko_o.dtype)


def diff_prep(p, lp, tabs):
    tm = DIFF_TM
    partner = _rope_partner(DH_D, (0, DH_D))
    perm = np.zeros((LANES, LANES), np.float32)
    perm[partner, np.arange(LANES)] = 1.0
    gain2 = lambda g: jnp.stack([jnp.concatenate([g, g]), jnp.concatenate([g, g])[partner]])
    full = lambda shape: pl.BlockSpec(shape, lambda i: (0, 0))
    tab = pl.BlockSpec((tm, LANES), lambda i: (_pos_block_of_tile(i, tm), 0))
    row = pl.BlockSpec((tm, W_BRANCH), lambda i: (i, 0))
    dq_b, dk_b = _seg_block('dq'), _seg_block('dk')
    return pl.pallas_call(
        _diff_prep_kernel,
        grid=(R_ALL // tm,),
        in_specs=[pl.BlockSpec((tm, W_BRANCH), lambda i: (i, dq_b)), pl.BlockSpec((tm, W_BRANCH), lambda i: (i, dk_b)),
                  full((2, LANES)), full((2, LANES)), full((LANES, LANES)), tab, tab],
        out_specs=(row, row, row),
        out_shape=(jax.ShapeDtypeStruct((R_ALL, W_BRANCH), BF16), jax.ShapeDtypeStruct((R_ALL, W_BRANCH), BF16),
                   jax.ShapeDtypeStruct((R_ALL, W_BRANCH), F32)),
        compiler_params=_cparams(("arbitrary",)),
        name="diff_prep",
    )(p, p, gain2(lp['g_diff_qn']), gain2(lp['g_diff_kn']), jnp.asarray(perm, BF16), *tabs)


def _diff_attn_kernel(*refs, has_cache, lambda_init):
    if has_cache:
        dl_ref, q_ref, k_ref, v_ref, kc_ref, vc_ref, g_ref, o_ref = refs
    else:
        dl_ref, q_ref, k_ref, v_ref, g_ref, o_ref = refs
    lane = lax.broadcasted_iota(jnp.int32, (1, LANES), 1)
    dl = dl_ref[...]
    lam = (jnp.exp(jnp.sum(dl[0:1] * dl[1:2], axis=-1, keepdims=True))
           - jnp.exp(jnp.sum(dl[2:3] * dl[3:4], axis=-1, keepdims=True)) + lambda_init)
    units = [(h, w) for h in range(H_D) for w in range(2)]

    def scores(unit):
        h, w = unit
        sl = slice((2 * w + h // 2) * LANES, (2 * w + h // 2 + 1) * LANES)
        q = jnp.where((lane // DH_D) == (h % 2), q_ref[:, sl], jnp.zeros((), q_ref.dtype))
        s_o = lax.dot_general(q, k_ref[:, sl], _NT, preferred_element_type=F32)
        s_c = (lax.dot_general(q, kc_ref[:, sl].astype(BF16), _NT, preferred_element_type=F32)
               if has_cache else None)
        return s_o, s_c

    outs = []
    nxt = scores(units[0])
    for n, (h, w) in enumerate(units):
        s_o, s_c = nxt
        if n + 1 < len(units):
            nxt = scores(units[n + 1])
        vs = slice(h * LANES, (h + 1) * LANES)
        m = jnp.max(s_o, axis=-1, keepdims=True)
        if has_cache:
            m = jnp.maximum(m, jnp.max(s_c, axis=-1, keepdims=True))
        e_o = jnp.exp(s_o - m)
        l = jnp.sum(e_o, axis=-1, keepdims=True)
        pv = jnp.dot(e_o.astype(BF16), v_ref[:, vs], preferred_element_type=F32)
        if has_cache:
            e_c = jnp.exp(s_c - m)
            l = l + jnp.sum(e_c, axis=-1, keepdims=True)
            pv = pv + jnp.dot(e_c.astype(BF16), vc_ref[:, vs].astype(BF16), preferred_element_type=F32)
        outs.append(pv * (1.0 / l))
    for h in range(H_D):
        y = outs[2 * h] - lam * outs[2 * h + 1]
        r = lax.rsqrt(jnp.mean(y * y, axis=-1, keepdims=True) + EPS)
        o_ref[:, h * LANES:(h + 1) * LANES] = ((y * r * g_ref[...]) * (1.0 - lambda_init)).astype(o_ref.dtype)


def diff_attention(dl, qd, kd, p, kc, vc, g_sub, *, row0, n_seq, t_seq, tq, lambda_init):
    has_cache = kc is not None
    nq = t_seq // tq
    qb0, kb0 = row0 // tq, row0 // t_seq
    in_specs = [pl.BlockSpec((4, DH_D), lambda s, i: (0, 0)),
                pl.BlockSpec((tq, W_BRANCH), lambda s, i: (qb0 + s * nq + i, 0)),
                pl.BlockSpec((t_seq, W_BRANCH), lambda s, i: (kb0 + s, 0)),
                pl.BlockSpec((t_seq, W_BRANCH), lambda s, i: (kb0 + s, _seg_block('dv')))]
    args = [dl, qd, kd, p]
    if has_cache:
        in_specs += [pl.BlockSpec((PAST_LEN, W_BRANCH), lambda s, i: (s, 0)),
                     pl.BlockSpec((PAST_LEN, W_BRANCH), lambda s, i: (s, 0))]
        args += [kc, vc]
    in_specs.append(pl.BlockSpec((1, LANES), lambda s, i: (0, 0)))
    args.append(g_sub.reshape(1, LANES))
    return pl.pallas_call(
        functools.partial(_diff_attn_kernel, has_cache=has_cache, lambda_init=lambda_init),
        grid=(n_seq, nq),
        in_specs=in_specs,
        out_specs=pl.BlockSpec((tq, W_BRANCH), lambda s, i: (s * nq + i, 0)),
        out_shape=jax.ShapeDtypeStruct((n_seq * t_seq, W_BRANCH), BF16),
        compiler_params=_cparams(("arbitrary", "arbitrary")),
        name="diff_attention",
    )(*args)


def _softplus(z):
    return jnp.maximum(z, 0.0) + jnp.log(1.0 + jnp.exp(-jnp.abs(z)))


def _gelu_tanh(x):
    return 0.5 * x * (1.0 + jnp.tanh(math.sqrt(2.0 / math.pi) * (x + 0.044715 * (x * x * x))))


def _rglru_kernel(x_ref, gate_ref, wc_ref, bc_ref, wg_ref, bg_ref, lam_ref, h0_ref, y_ref, st_ref,
                  af_s, uf_s, ab_s, ub_s, *, t_seq):
    t = t_seq
    x = x_ref[...].astype(F32)
    row = lax.broadcasted_iota(jnp.int32, (t, W_B), 0)
    wc = wc_ref[...]
    xc = (wc[0:1] * jnp.where(row >= 2, pltpu.roll(x, 2, 0), 0.0)
          + wc[1:2] * jnp.where(row >= 1, pltpu.roll(x, 1, 0), 0.0)
          + wc[2:3] * x
          + wc[3:4] * jnp.where(row < t - 1, pltpu.roll(x, t - 1, 0), 0.0)
          + bc_ref[...])
    gates = jnp.dot(xc.astype(BF16), wg_ref[...], preferred_element_type=F32) + bg_ref[...]
    for d, (a_s, u_s) in enumerate(((af_s, uf_s), (ab_s, ub_s))):
        rg = jax.nn.sigmoid(gates[:, (2 * d) * W_B:(2 * d + 1) * W_B])
        ig = jax.nn.sigmoid(gates[:, (2 * d + 1) * W_B:(2 * d + 2) * W_B])
        log_a = -RG_C * rg * _softplus(-lam_ref[d:d + 1, :])
        a = jnp.exp(log_a)
        a_s[...] = a
        u_s[...] = jnp.sqrt(-jnp.tanh(log_a) * (a * a + 1.0)) * (ig * xc)

    nblk = t // SUBLANES

    def body(k, carry):
        hf, hb = carry
        base_f = pl.multiple_of(k * SUBLANES, SUBLANES)
        base_b = pl.multiple_of((nblk - 1 - k) * SUBLANES, SUBLANES)
        for r in range(SUBLANES):
            rf = pl.ds(base_f + r, 1)
            rb = pl.ds(base_b + (SUBLANES - 1 - r), 1)
            hf = af_s[rf, :] * hf + uf_s[rf, :]
            hb = ab_s[rb, :] * hb + ub_s[rb, :]
            uf_s[rf, :] = hf
            ub_s[rb, :] = hb
        return hf, hb

    hf, hb = lax.fori_loop(0, nblk, body, (h0_ref[0, 0:1, :], h0_ref[0, 1:2, :]))
    st_ref[0, 0:1, :] = hf
    st_ref[0, 1:2, :] = hb
    y_ref[...] = (_gelu_tanh(gate_ref[...].astype(F32)) * (uf_s[...] + ub_s[...])).astype(y_ref.dtype)


def pack_rglru_weights(lp):
    def blockdiag(w):
        eye = jnp.eye(NB_B, dtype=w.dtype)
        return jnp.einsum('ncd,nm->ncmd', w, eye).reshape(W_B, W_B)
    wg = jnp.concatenate([blockdiag(lp['w_rg_a'][0]), blockdiag(lp['w_rg_x'][0]),
                          blockdiag(lp['w_rg_a'][1]), blockdiag(lp['w_rg_x'][1])], axis=1).astype(BF16)
    bg = jnp.concatenate([lp['b_rg_a'][0], lp['b_rg_x'][0], lp['b_rg_a'][1], lp['b_rg_x'][1]]).reshape(1, 4 * W_B)
    return dict(wg=wg, bg=bg, wc=lp['w_conv_rg'], bc=lp['b_conv_rg'].reshape(1, W_B), lam=lp['rg_lambda'])


def rglru(p, rw, h0, *, row0, n_seq, t_seq):
    rb0 = row0 // t_seq
    xb, gb = _seg_block('rg_x'), _seg_block('rg_gate')
    full = lambda shape: pl.BlockSpec(shape, lambda s: tuple(0 for _ in shape))
    in_specs = [pl.BlockSpec((t_seq, W_B), lambda s: (rb0 + s, xb)),
                pl.BlockSpec((t_seq, W_B), lambda s: (rb0 + s, gb)),
                full((CONV_W, W_B)), full((1, W_B)), full((W_B, 4 * W_B)), full((1, 4 * W_B)), full((2, W_B)),
                pl.BlockSpec((1, 2, W_B), lambda s: (s, 0, 0))]
    args = [p, p, rw['wc'], rw['bc'], rw['wg'], rw['bg'], rw['lam'], h0]
    return pl.pallas_call(
        functools.partial(_rglru_kernel, t_seq=t_seq),
        grid=(n_seq,),
        in_specs=in_specs,
        out_specs=(pl.BlockSpec((t_seq, W_B), lambda s: (s, 0)), pl.BlockSpec((1, 2, W_B), lambda s: (s, 0, 0))),
        out_shape=(jax.ShapeDtypeStruct((n_seq * t_seq, W_B), BF16), jax.ShapeDtypeStruct((n_seq, 2, W_B), F32)),
        scratch_shapes=[pltpu.VMEM((t_seq, W_B), F32)] * 4,
        compiler_params=_cparams(("arbitrary",)),
        name="rglru",
    )(*args)


def _dot_split(a, b_bf16):
    hi = a.astype(BF16)
    lo = (a - hi.astype(F32)).astype(BF16)
    return (jnp.dot(hi, b_bf16, preferred_element_type=F32) + jnp.dot(lo, b_bf16, preferred_element_type=F32))


def _log_sigmoid(z):
    return jnp.minimum(z, 0.0) - jnp.log(1.0 + jnp.exp(-jnp.abs(z)))


_TN = (((0,), (0,)), ((), ()))


def _mlstm_kernel(q_ref, k_ref, v_ref, o_ref, sm_ref, bias_ref, g_ref, c0_ref, m0_ref,
                  y_ref, c_out, m_out, hm_s, c_s, *, t_seq):
    L = ML_CHUNK
    nchunk = t_seq // L
    scale = DH_C ** -0.5
    ri = lax.broadcasted_iota(jnp.int32, (L, L), 0)
    ci = lax.broadcasted_iota(jnp.int32, (L, L), 1)
    lane1 = lax.broadcasted_iota(jnp.int32, (L, LANES), 1)
    ones_col = jnp.where(lane1 == 0, 1.0, 0.0).astype(BF16)
    bias = bias_ref[...]

    for d in range(2):
        causal = (ci <= ri) if d == 0 else (ci >= ri)
        tri = jnp.where(causal, 1.0, 0.0).astype(BF16)
        tri_t = jnp.where((ri <= ci) if d == 0 else (ri >= ci), 1.0, 0.0).astype(BF16)
        c_s[...] = c0_ref[0, d]
        m_init = tuple(m0_ref[0, :, d * H_C + h:d * H_C + h + 1] for h in range(H_C))

        def chunk(kk, ms, d=d, causal=causal, tri=tri, tri_t=tri_t):
            cidx = kk if d == 0 else nchunk - 1 - kk
            rows = pl.ds(pl.multiple_of(cidx * L, L), L)
            gsm = sm_ref[rows, :].astype(F32) + bias
            lf_all = _log_sigmoid(gsm)
            cum_cols = _dot_split_left(tri, lf_all)
            g_t = gsm.T
            cum_rows = _dot_split(lf_all.T, tri_t)
            new_ms = []
            for h in range(H_C):
                jl, jf = SM_MI + d * H_C + h, SM_MF + d * H_C + h
                cum_c = cum_cols[:, jf:jf + 1]
                cum_r = cum_rows[jf:jf + 1, :]
                li_c = gsm[:, jl:jl + 1]
                li_r = g_t[jl:jl + 1, :]
                m_mem = ms[h]
                log_d = jnp.where(causal, cum_c - cum_r + li_r, -jnp.inf)
                inter = cum_c + m_mem
                m_row = jnp.maximum(inter, jnp.max(log_d, axis=-1, keepdims=True))
                dmat = jnp.exp(log_d - m_row)
                w_inter = jnp.exp(inter - m_row)
                sl = slice(h * DH_C, (h + 1) * DH_C)
                qh, kh, vh = q_ref[rows, sl], k_ref[rows, sl], v_ref[rows, sl]
                s = lax.dot_general(qh, kh, _NT, preferred_element_type=F32) * (scale * dmat)
                sv = jnp.dot(s.astype(BF16), vh, preferred_element_type=F32)
                qc = jnp.dot(qh, c_s[h].astype(BF16), preferred_element_type=F32) * scale
                num = sv + qc[:, :DH_C] * w_inter
                den = jnp.sum(s, axis=-1, keepdims=True) + w_inter * qc[:, DH_C:DH_C + 1]
                den = jnp.maximum(jnp.abs(den), jnp.exp(-m_row))
                h_out = num * (1.0 / den)
                if d == 0:
                    hm_s[rows, sl] = h_out
                else:
                    hm_s[rows, sl] = hm_s[rows, sl] + h_out
                last = cum_c[L - 1:L, :] if d == 0 else cum_c[0:1, :]
                w_s = last - cum_c + li_c
                m_new = jnp.maximum(last + m_mem, jnp.max(w_s, axis=0, keepdims=True))
                decay = jnp.exp(last + m_mem - m_new)
                ws = jnp.exp(w_s - m_new)
                kw_t = (kh.astype(F32) * ws).T.astype(BF16)
                v_aug = jnp.concatenate([vh, ones_col], axis=1)
                c_s[h] = decay * c_s[h] + jnp.dot(kw_t, v_aug, preferred_element_type=F32)
                new_ms.append(m_new)
            return tuple(new_ms)

        m_fin = lax.fori_loop(0, nchunk, chunk, m_init)
        c_out[0, d] = c_s[...]
        for h in range(H_C):
            m_out[0, :, d * H_C + h:d * H_C + h + 1] = m_fin[h]

    for h in range(H_C):
        sl = slice(h * DH_C, (h + 1) * DH_C)
        hm = hm_s[:, sl]
        r = lax.rsqrt(jnp.mean(hm * hm, axis=-1, keepdims=True) + EPS)
        y_ref[:, sl] = (jax.nn.sigmoid(o_ref[:, sl].astype(F32)) * (hm * r * g_ref[...])).astype(y_ref.dtype)


def _dot_split_left(a_bf16, b):
    hi = b.astype(BF16)
    lo = (b - hi.astype(F32)).astype(BF16)
    return (jnp.dot(a_bf16, hi, preferred_element_type=F32) + jnp.dot(a_bf16, lo, preferred_element_type=F32))


def mlstm(p, bias_sm, g_out, c0_aug, m0, *, row0, n_seq, t_seq):
    rb0 = row0 // t_seq
    seg = lambda nm: pl.BlockSpec((t_seq, W_BRANCH), lambda s, b=_seg_block(nm): (rb0 + s, b))
    full = lambda shape: pl.BlockSpec(shape, lambda s: tuple(0 for _ in shape))
    st_spec = pl.BlockSpec((1, 2, H_C, DH_C, 2 * DH_C), lambda s: (s, 0, 0, 0, 0))
    m_spec = pl.BlockSpec((1, 1, 2 * H_C), lambda s: (s, 0, 0))
    in_specs = [seg('mq'), seg('mk'), seg('mv'), seg('mo'),
                pl.BlockSpec((t_seq, LANES), lambda s: (rb0 + s, P_SMALL // LANES)),
                full((1, LANES)), full((1, DH_C)), st_spec, m_spec]
    args = [p, p, p, p, p, bias_sm, g_out.reshape(1, DH_C), c0_aug, m0]
    return pl.pallas_call(
        functools.partial(_mlstm_kernel, t_seq=t_seq),
        grid=(n_seq,),
        in_specs=in_specs,
        out_specs=(pl.BlockSpec((t_seq, W_BRANCH), lambda s: (s, 0)), st_spec, m_spec),
        out_shape=(jax.ShapeDtypeStruct((n_seq * t_seq, W_BRANCH), BF16),
                   jax.ShapeDtypeStruct((n_seq, 2, H_C, DH_C, 2 * DH_C), F32),
                   jax.ShapeDtypeStruct((n_seq, 1, 2 * H_C), F32)),
        scratch_shapes=[pltpu.VMEM((t_seq, W_BRANCH), F32), pltpu.VMEM((H_C, DH_C, 2 * DH_C), F32)],
        compiler_params=_cparams(("arbitrary",)),
        name="mlstm",
    )(*args)


MERGE_TM = 512


def _merge_kernel(*refs):
    ctx_refs, lat_refs, gate_refs = refs[0:4], refs[4:8], refs[8:12]
    x_ref, mod_ref, wbr_ref, wout_ref, gn_ref, xo_ref, xn_ref = refs[12:19]
    xn3_ref = refs[19] if len(refs) > 19 else None
    is_ctx = pl.program_id(0) < R_CTX // MERGE_TM
    merged = None
    for g in range(N_BRANCH):
        yg = jnp.where(is_ctx, ctx_refs[g][...], lat_refs[g][...])
        pg = jnp.dot(yg, wbr_ref[g], preferred_element_type=F32)
        term = jax.nn.sigmoid(gate_refs[g][...].astype(F32)) * pg
        merged = term if merged is None else merged + term
    y = jnp.dot(merged.astype(BF16), wout_ref[...], preferred_element_type=F32)
    x = x_ref[...] + mod_ref[0, 2:3, :] * y
    xo_ref[...] = x
    r = lax.rsqrt(jnp.mean(x * x, axis=-1, keepdims=True) + EPS)
    xn = (x * r * gn_ref[...]) * (1.0 + mod_ref[0, 4:5, :]) + mod_ref[0, 3:4, :]
    xn_ref[...] = xn.astype(xn_ref.dtype)
    if xn3_ref is not None:
        _rows_to_tiles(xn3_ref, xn.astype(BF16))


def merge(ys_ctx, ys_lat, p, x, mod, wbr, wout, g_ffn, rows_as_tiles):
    tm = MERGE_TM
    n_ctx_tiles = R_CTX // tm
    br_ctx = pl.BlockSpec((tm, W_BRANCH), lambda i: (jnp.minimum(i, n_ctx_tiles - 1), 0))
    br_lat = pl.BlockSpec((tm, W_BRANCH), lambda i: (jnp.maximum(i - n_ctx_tiles, 0), 0))
    gate = lambda g: pl.BlockSpec((tm, D_MODEL), lambda i, g=g: (i, g))
    row = pl.BlockSpec((tm, D_MODEL), lambda i: (i, 0))
    out_specs = [row, row]
    out_shape = [jax.ShapeDtypeStruct((R_ALL, D_MODEL), F32), jax.ShapeDtypeStruct((R_ALL, D_MODEL), BF16)]
    if rows_as_tiles:
        out_specs.append(pl.BlockSpec((tm, ROW_SUB, LANES), lambda i: (i, 0, 0)))
        out_shape.append(jax.ShapeDtypeStruct((R_ALL, ROW_SUB, LANES), F32))
    return pl.pallas_call(
        _merge_kernel,
        grid=(R_ALL // tm,),
        in_specs=[br_ctx] * N_BRANCH + [br_lat] * N_BRANCH + [gate(0), gate(1), gate(2), gate(3), row,
                  pl.BlockSpec((1, 8, D_MODEL), lambda i: (_mod_row_of_tile(i, tm), 0, 0)),
                  pl.BlockSpec((N_BRANCH, W_BRANCH, D_MODEL), lambda i: (0, 0, 0)),
                  pl.BlockSpec((D_MODEL, D_MODEL), lambda i: (0, 0)),
                  pl.BlockSpec((1, D_MODEL), lambda i: (0, 0))],
        out_specs=tuple(out_specs),
        out_shape=tuple(out_shape),
        compiler_params=_cparams(("arbitrary",)),
        name="merge",
    )(*ys_ctx, *ys_lat, p, p, p, p, x, mod, wbr, wout, g_ffn.reshape(1, D_MODEL))


def _new_expert(te_ref, i):
    return jnp.logical_or(i == 0, te_ref[i] != te_ref[jnp.maximum(i - 1, 0)])


def _ffn_up_kernel(te_ref, nt_ref, x_ref, wg_ref, wu_ref, h_ref, wgb_ref, wub_ref):
    i = pl.program_id(1)

    @pl.when(_new_expert(te_ref, i))
    def _():
        wgb_ref[...] = wg_ref[0].astype(BF16)
        wub_ref[...] = wu_ref[0].astype(BF16)

    @pl.when(i < nt_ref[0])
    def _():
        x = x_ref[...]
        g = jnp.dot(x, wgb_ref[...], preferred_element_type=F32)
        u = jnp.dot(x, wub_ref[...], preferred_element_type=F32)
        h_ref[...] = (g * jax.nn.sigmoid(g) * u).astype(h_ref.dtype)

    @pl.when(i >= nt_ref[0])
    def _():
        h_ref[...] = jnp.zeros(h_ref.shape, h_ref.dtype)


def ffn_up(tile_expert, n_tiles, xs, wg, wu, tm, tf, weight_buffers):
    r, d = xs.shape
    f = wg.shape[2]
    w_spec = pl.BlockSpec((1, d, tf), lambda j, i, te, nt: (te[i], 0, j), pipeline_mode=pl.Buffered(weight_buffers))
    return pl.pallas_call(
        _ffn_up_kernel,
        grid_spec=pltpu.PrefetchScalarGridSpec(
            num_scalar_prefetch=2,
            grid=(f // tf, r // tm),
            in_specs=[pl.BlockSpec((tm, d), lambda j, i, te, nt: (i, 0)), w_spec, w_spec],
            out_specs=pl.BlockSpec((tm, tf), lambda j, i, te, nt: (i, j)),
            scratch_shapes=[pltpu.VMEM((d, tf), BF16), pltpu.VMEM((d, tf), BF16)]),
        out_shape=jax.ShapeDtypeStruct((r, f), BF16),
        compiler_params=_cparams(("arbitrary", "arbitrary")),
        name="ffn_up",
    )(tile_expert, n_tiles, xs, wg, wu)


def _ffn_down_kernel(te_ref, nt_ref, h_ref, wd_ref, y_ref, wdb_ref):
    i = pl.program_id(0)

    @pl.when(_new_expert(te_ref, i))
    def _():
        wdb_ref[...] = wd_ref[0].astype(BF16)

    @pl.when(i < nt_ref[0])
    def _():
        _rows_to_tiles(y_ref, jnp.dot(h_ref[...], wdb_ref[...], preferred_element_type=F32))

    @pl.when(i >= nt_ref[0])
    def _():
        y_ref[...] = jnp.zeros(y_ref.shape, y_ref.dtype)


def ffn_down(tile_expert, n_tiles, h, wd, tm):
    r, f = h.shape
    d = wd.shape[2]
    return pl.pallas_call(
        _ffn_down_kernel,
        grid_spec=pltpu.PrefetchScalarGridSpec(
            num_scalar_prefetch=2,
            grid=(r // tm,),
            in_specs=[pl.BlockSpec((tm, f), lambda i, te, nt: (i, 0)),
                      pl.BlockSpec((1, f, d), lambda i, te, nt: (te[i], 0, 0))],
            out_specs=pl.BlockSpec((tm, ROW_SUB, LANES), lambda i, te, nt: (i, 0, 0)),
            scratch_shapes=[pltpu.VMEM((f, d), BF16)]),
        out_shape=jax.ShapeDtypeStruct((r, ROW_SUB, LANES), F32),
        compiler_params=_cparams(("arbitrary",)),
        name="ffn_down",
    )(tile_expert, n_tiles, h, wd)


def _ffn_down_res_kernel(h_ref, wd_ref, x_ref, mod_ref, y_ref, wdb_ref):
    @pl.when(pl.program_id(0) == 0)
    def _():
        wdb_ref[...] = wd_ref[...].astype(BF16)

    y = jnp.dot(h_ref[...], wdb_ref[...], preferred_element_type=F32)
    y_ref[...] = x_ref[...] + mod_ref[0, 5:6, :] * y


def ffn_down_residual(h, wd, x, mod):
    tm = 1024
    r, f = h.shape
    d = wd.shape[1]
    return pl.pallas_call(
        _ffn_down_res_kernel,
        grid=(r // tm,),
        in_specs=[pl.BlockSpec((tm, f), lambda i: (i, 0)),
                  pl.BlockSpec((f, d), lambda i: (0, 0), pipeline_mode=pl.Buffered(1)),
                  pl.BlockSpec((tm, d), lambda i: (i, 0)),
                  pl.BlockSpec((1, 8, d), lambda i: (_mod_row_of_tile(i, tm), 0, 0))],
        out_specs=pl.BlockSpec((tm, d), lambda i: (i, 0)),
        out_shape=jax.ShapeDtypeStruct((r, d), F32),
        scratch_shapes=[pltpu.VMEM((f, d), BF16)],
        compiler_params=_cparams(("arbitrary",)),
        name="ffn_down_residual",
    )(h, wd, x, mod)


def dense_swiglu_residual(xn, x, mod, wg, wu, wd):
    t = xn.shape[0]
    tm = 1024
    n_tiles = t // tm
    te = jnp.zeros((n_tiles,), jnp.int32)
    nt = jnp.full((1,), n_tiles, jnp.int32)
    h = ffn_up(te, nt, xn, wg[None], wu[None], tm=tm, tf=wg.shape[1] // 2, weight_buffers=1)
    return ffn_down_residual(h, wd, x, mod)


MOE_TM = 512
DISPATCH_TM = 512


def _moe_routing(logits, tm):
    t = logits.shape[0]
    n_assign = t * TOP_K
    top_v, top_i = lax.top_k(logits, TOP_K)
    gate = jax.nn.softmax(top_v, axis=-1)
    flat_e = top_i.reshape(-1).astype(jnp.int32)
    onehot = (flat_e[:, None] == jnp.arange(N_EXP, dtype=jnp.int32)[None, :])
    blk = LANES
    oh = onehot.astype(F32).reshape(n_assign // blk, blk, N_EXP)
    tril = jnp.tril(jnp.ones((blk, blk), F32))
    within = jnp.einsum('ij,bjk->bik', tril, oh)
    blk_tot = within[:, -1, :]
    blk_off = jnp.cumsum(blk_tot, axis=0) - blk_tot
    csum = (within + blk_off[:, None, :]).reshape(n_assign, N_EXP)
    rank = jnp.sum(jnp.where(onehot, csum - 1.0, 0.0), axis=1).astype(jnp.int32)
    counts = csum[-1].astype(jnp.int32)
    padded = (counts + tm - 1) // tm * tm
    grp_start = jnp.cumsum(padded) - padded
    raw_start = jnp.cumsum(counts) - counts
    slot_of_assign = jnp.sum(jnp.where(onehot, grp_start[None, :], 0), axis=1) + rank

    r_max = n_assign + N_EXP * tm
    tile_start = jnp.arange(r_max // tm, dtype=jnp.int32) * tm
    tile_expert = jnp.sum((tile_start[:, None] >= (grp_start + padded)[None, :]).astype(jnp.int32), axis=1)
    tile_expert = jnp.minimum(tile_expert, N_EXP - 1).astype(jnp.int32)
    n_tiles = (jnp.sum(padded) // tm).astype(jnp.int32).reshape(1)

    order = jnp.argsort(flat_e, stable=True).astype(jnp.int32)
    e_slot = jnp.repeat(tile_expert, tm)
    j = jnp.arange(r_max, dtype=jnp.int32) - grp_start[e_slot]
    src = jnp.clip(raw_start[e_slot] + j, 0, n_assign - 1)
    tok_of_slot = jnp.where(j < counts[e_slot], order[src] // TOP_K, 0)
    return gate, slot_of_assign, tok_of_slot, tile_expert, n_tiles


ROW_SUB = D_MODEL // LANES


def _rows_to_tiles(o3_ref, x):
    for j in range(ROW_SUB):
        o3_ref[:, j, :] = x[:, j * LANES:(j + 1) * LANES].astype(o3_ref.dtype)


def _start_row_gather(idx_ref, src_ref, dst, sem, n_rows):
    def body(q, carry):
        for u in range(2):
            r = 2 * q + u
            pltpu.make_async_copy(src_ref.at[idx_ref[0, 0, r]], dst.at[r], sem).start(priority=u)
        return carry

    lax.fori_loop(0, n_rows // 2, body, 0, unroll=4)


def _wait_row_gather(src_ref, dst, sem, n_rows):
    pltpu.make_async_copy(src_ref.at[pl.ds(0, n_rows)], dst, sem).wait()


def _tiles_to_rows(tiles, rows_ref):
    for j in range(ROW_SUB):
        rows_ref[:, j * LANES:(j + 1) * LANES] = tiles[:, j, :]


def _dispatch_kernel(nt_ref, idx_ref, idx_next_ref, src_ref, o_ref, buf, rows, sem):
    i = pl.program_id(0)
    nt = nt_ref[0]
    tm = o_ref.shape[0]
    slot = i % 2

    @pl.when(i == 0)
    def _():
        _start_row_gather(idx_ref, src_ref, buf.at[0], sem.at[0], tm)

    @pl.when(i + 1 < nt)
    def _():
        _start_row_gather(idx_next_ref, src_ref, buf.at[1 - slot], sem.at[1 - slot], tm)

    @pl.when(i < nt)
    def _():
        _wait_row_gather(src_ref, buf.at[slot], sem.at[slot], tm)
        _tiles_to_rows(buf.at[slot], rows)
        o_ref[...] = rows[...].astype(o_ref.dtype)

    @pl.when(i >= nt)
    def _():
        o_ref[...] = jnp.zeros(o_ref.shape, o_ref.dtype)


def moe_dispatch(n_tiles, tok_of_slot, xn3, tm):
    r = tok_of_slot.shape[0]
    last = r // tm - 1
    idx = tok_of_slot.reshape(r // tm, 1, tm)
    return pl.pallas_call(
        _dispatch_kernel,
        grid_spec=pltpu.PrefetchScalarGridSpec(
            num_scalar_prefetch=1,
            grid=(r // tm,),
            in_specs=[pl.BlockSpec((1, 1, tm), lambda i, nt: (i, 0, 0), memory_space=pltpu.SMEM),
                      pl.BlockSpec((1, 1, tm), lambda i, nt: (jnp.minimum(i + 1, last), 0, 0),
                                   memory_space=pltpu.SMEM),
                      pl.BlockSpec(memory_space=pl.ANY)],
            out_specs=pl.BlockSpec((tm, D_MODEL), lambda i, nt: (i, 0)),
            scratch_shapes=[pltpu.VMEM((2, tm, ROW_SUB, LANES), F32), pltpu.VMEM((tm, D_MODEL), F32),
                            pltpu.SemaphoreType.DMA((2,))]),
        out_shape=jax.ShapeDtypeStruct((r, D_MODEL), BF16),
        compiler_params=_cparams(("arbitrary",)),
        name="moe_dispatch",
    )(n_tiles, idx, idx, xn3)


COMBINE_TM = 256


def _combine_kernel(idx_ref, idx_next_ref, ys_ref, x_ref, gate_ref, mod_ref, o_ref, buf, rows, sem):
    i = pl.program_id(0)
    tm = o_ref.shape[0]
    n_rows = TOP_K * tm
    slot = i % 2

    @pl.when(i == 0)
    def _():
        _start_row_gather(idx_ref, ys_ref, buf.at[0], sem.at[0], n_rows)

    @pl.when(i + 1 < pl.num_programs(0))
    def _():
        _start_row_gather(idx_next_ref, ys_ref, buf.at[1 - slot], sem.at[1 - slot], n_rows)

    _wait_row_gather(ys_ref, buf.at[slot], sem.at[slot], n_rows)
    _tiles_to_rows(buf.at[slot], rows)
    f = gate_ref[:, 0:1] * rows[0:tm, :] + gate_ref[:, 1:2] * rows[tm:2 * tm, :]
    o_ref[...] = x_ref[...] + mod_ref[0, 5:6, :] * f


def moe_combine(slot_of_assign, ys3, x, gate, mod):
    t, d = x.shape
    tm = COMBINE_TM
    idx = slot_of_assign.reshape(t // tm, tm, TOP_K).transpose(0, 2, 1).reshape(t // tm, 1, TOP_K * tm)
    last = t // tm - 1
    return pl.pallas_call(
        _combine_kernel,
        grid=(t // tm,),
        in_specs=[pl.BlockSpec((1, 1, TOP_K * tm), lambda i: (i, 0, 0), memory_space=pltpu.SMEM),
                  pl.BlockSpec((1, 1, TOP_K * tm), lambda i: (jnp.minimum(i + 1, last), 0, 0),
                               memory_space=pltpu.SMEM),
                  pl.BlockSpec(memory_space=pl.ANY),
                  pl.BlockSpec((tm, d), lambda i: (i, 0)),
                  pl.BlockSpec((tm, TOP_K), lambda i: (i, 0)),
                  pl.BlockSpec((1, 8, d), lambda i: (_mod_row_of_tile(i, tm), 0, 0))],
        out_specs=pl.BlockSpec((tm, d), lambda i: (i, 0)),
        out_shape=jax.ShapeDtypeStruct((t, d), F32),
        scratch_shapes=[pltpu.VMEM((2, TOP_K * tm, ROW_SUB, LANES), F32), pltpu.VMEM((TOP_K * tm, d), F32),
                        pltpu.SemaphoreType.DMA((2,))],
        compiler_params=_cparams(("arbitrary",)),
        name="moe_combine",
    )(idx, idx, ys3, x, gate, mod)


def moe_swiglu_residual(xn3, xn, x, mod, w_router, wg, wu, wd):
    tm = MOE_TM
    logits = jnp.dot(xn.astype(F32), w_router, precision=lax.Precision.HIGHEST)
    gate, slot_of_assign, tok_of_slot, tile_expert, n_tiles = _moe_routing(logits, tm)
    xs = moe_dispatch(n_tiles * (tm // DISPATCH_TM), tok_of_slot, xn3, DISPATCH_TM)
    h = ffn_up(tile_expert, n_tiles, xs, wg, wu, tm=tm, tf=wg.shape[2] // 4, weight_buffers=2)
    ys3 = ffn_down(tile_expert, n_tiles, h, wd, tm=tm)
    return moe_combine(slot_of_assign, ys3, x, gate, mod)


def _layer(x, cond, lp, l, ctx, tabs_a, tabs_d):
    mod = modulation(cond, lp['w_mod'], lp['b_mod']).reshape(cond.shape[0], 6, D_MODEL)
    mod = jnp.pad(mod, ((0, 0), (0, 2), (0, 0)))
    p = in_proj(x, mod, lp['g_norm_mix'], pack_w_in(lp['w_in']))

    mw = pack_mla_weights(lp)
    q_a, k_a, v_a, ckv, kr = mla_prep(p, mw, tabs_a)
    kr_c = jnp.pad(ctx['mla_krope'].reshape(-1, ROPE_A), ((0, 0), (0, LANES - ROPE_A)))
    kc_a, vc_a = mla_prep_cache(ctx['mla_ckv'].reshape(-1, KV_RANK), kr_c, mw)
    ya_c = mla_attention(q_a, k_a, v_a, None, None, row0=0, n_seq=N_CTX_SEQ, t_seq=T_CTX, tq=T_CTX)
    ya_l = mla_attention(q_a, k_a, v_a, kc_a, vc_a, row0=R_CTX, n_seq=N_LAT_SEQ, t_seq=T_LAT, tq=ATTN_TQ)

    rw = pack_rglru_weights(lp)
    yb_c, st_rg = rglru(p, rw, jnp.zeros((N_CTX_SEQ, 2, W_B), F32), row0=0, n_seq=N_CTX_SEQ, t_seq=T_CTX)
    yb_l, _ = rglru(p, rw, ctx['rglru'], row0=R_CTX, n_seq=N_LAT_SEQ, t_seq=T_LAT)

    bias_sm = jnp.zeros((LANES,), F32).at[SM_MI:SM_MI + 2 * H_C].set(lp['b_ml_i'].reshape(-1))
    bias_sm = bias_sm.at[SM_MF:SM_MF + 2 * H_C].set(lp['b_ml_f'].reshape(-1)).reshape(1, LANES)
    c0_ctx = jnp.zeros((N_CTX_SEQ, 2, H_C, DH_C, 2 * DH_C), F32)
    m0_ctx = jnp.zeros((N_CTX_SEQ, 1, 2 * H_C), F32)
    c0_lat = jnp.concatenate([ctx['mlstm_C'], ctx['mlstm_n'][..., None],
                              jnp.zeros(ctx['mlstm_n'].shape + (DH_C - 1,), F32)], axis=-1)
    m0_lat = ctx['mlstm_m'].reshape(N_LAT_SEQ, 1, 2 * H_C)
    yc_c, c_fin, m_fin = mlstm(p, bias_sm, lp['g_ml_out'], c0_ctx, m0_ctx, row0=0, n_seq=N_CTX_SEQ, t_seq=T_CTX)
    yc_l, _, _ = mlstm(p, bias_sm, lp['g_ml_out'], c0_lat, m0_lat, row0=R_CTX, n_seq=N_LAT_SEQ, t_seq=T_LAT)

    lambda_init = 0.8 - 0.6 * math.exp(-0.3 * l)
    qd, kd_own, kd_plain = diff_prep(p, lp, tabs_d)
    dkc = ctx['diff_k'].reshape(-1, W_BRANCH)
    dvc = ctx['diff_v'].reshape(-1, W_BRANCH)
    yd_c = diff_attention(lp['diff_lambda'], qd, kd_own, p, None, None, lp['g_diff_sub'],
                          row0=0, n_seq=N_CTX_SEQ, t_seq=T_CTX, tq=T_CTX, lambda_init=lambda_init)
    yd_l = diff_attention(lp['diff_lambda'], qd, kd_own, p, dkc, dvc, lp['g_diff_sub'],
                          row0=R_CTX, n_seq=N_LAT_SEQ, t_seq=T_LAT, tq=ATTN_TQ, lambda_init=lambda_init)

    merged = merge((ya_c, yb_c, yc_c, yd_c), (ya_l, yb_l, yc_l, yd_l), p, x, mod, lp['w_br'].astype(BF16),
                   lp['w_out'].astype(BF16), lp['g_norm_ffn'], rows_as_tiles='moe' in lp)
    if 'ffn' in lp:
        x, xn = merged
        x = dense_swiglu_residual(xn, x, mod, *lp['ffn'])
    else:
        x, xn, xn3 = merged
        x = moe_swiglu_residual(xn3, xn, x, mod, *lp['moe'])

    dv0 = _seg_block('dv') * W_BRANCH
    ctx_out = (ckv[:R_CTX].reshape(N_CTX_SEQ, T_CTX, KV_RANK),
               kr[:R_CTX, :ROPE_A].reshape(N_CTX_SEQ, T_CTX, ROPE_A),
               kd_plain[:R_CTX].reshape(N_CTX_SEQ, T_CTX, 2, H_D, DH_D),
               p[:R_CTX, dv0:dv0 + W_BRANCH].astype(F32).reshape(N_CTX_SEQ, T_CTX, H_D, 2 * DH_D),
               st_rg,
               c_fin[..., :DH_C],
               c_fin[..., DH_C],
               m_fin.reshape(N_CTX_SEQ, 2, H_C))
    return x, ctx_out


def kernel(x_prompt, x_sample, cache_mla_ckv, cache_mla_krope, cache_diff_k, cache_diff_v,
           state_rglru, state_mlstm_C, state_mlstm_n, state_mlstm_m, c, c_ctx,
           w_mod, b_mod, g_norm_mix, g_norm_ffn, w_in, g_mla_qlat, w_mla_uq, g_mla_kvlat, w_mla_ukv,
           g_mla_qn, g_mla_kn, w_conv_rg, b_conv_rg, w_rg_a, b_rg_a, w_rg_x, b_rg_x, rg_lambda,
           b_ml_i, b_ml_f, g_ml_out, g_diff_qn, g_diff_kn, diff_lambda, g_diff_sub, w_br, w_out,
           w_ffn_gate, w_ffn_up, w_ffn_down, w_router, w_moe_gate, w_moe_up, w_moe_down):
    assert x_prompt.shape == (N_CTX_SEQ, T_CTX, D_MODEL) and x_sample.shape == (N_LAT_SEQ, T_LAT, D_MODEL)
    tabs_a = _rope_tables(ROPE_A, (NOPE_A,), MLA_TM)
    tabs_d = _rope_tables(DH_D, (0, DH_D), DIFF_TM)
    cond = jnp.concatenate([c_ctx.reshape(1, D_MODEL), c, jnp.zeros((16 - 1 - N_LAT_SEQ, D_MODEL), F32)], axis=0)
    x = jnp.concatenate([x_prompt.reshape(R_CTX, D_MODEL), x_sample.reshape(R_LAT, D_MODEL)], axis=0)
    new = []
    for l in range(DEPTH):
        lp = dict(w_mod=w_mod[l], b_mod=b_mod[l], g_norm_mix=g_norm_mix[l], g_norm_ffn=g_norm_ffn[l], w_in=w_in[l],
                  g_mla_qlat=g_mla_qlat[l], w_mla_uq=w_mla_uq[l], g_mla_kvlat=g_mla_kvlat[l], w_mla_ukv=w_mla_ukv[l],
                  g_mla_qn=g_mla_qn[l], g_mla_kn=g_mla_kn[l], w_conv_rg=w_conv_rg[l], b_conv_rg=b_conv_rg[l],
                  w_rg_a=w_rg_a[l], b_rg_a=b_rg_a[l], w_rg_x=w_rg_x[l], b_rg_x=b_rg_x[l], rg_lambda=rg_lambda[l],
                  b_ml_i=b_ml_i[l], b_ml_f=b_ml_f[l], g_ml_out=g_ml_out[l], g_diff_qn=g_diff_qn[l],
                  g_diff_kn=g_diff_kn[l], diff_lambda=diff_lambda[l], g_diff_sub=g_diff_sub[l],
                  w_br=w_br[l], w_out=w_out[l])
        if l % 2 == 0:
            lp['ffn'] = (w_ffn_gate[l // 2], w_ffn_up[l // 2], w_ffn_down[l // 2])
        else:
            lp['moe'] = (w_router[l // 2], w_moe_gate[l // 2], w_moe_up[l // 2], w_moe_down[l // 2])
        ctx_l = dict(mla_ckv=cache_mla_ckv[:, l], mla_krope=cache_mla_krope[:, l], diff_k=cache_diff_k[:, l],
                     diff_v=cache_diff_v[:, l], rglru=state_rglru[:, l], mlstm_C=state_mlstm_C[:, l],
                     mlstm_n=state_mlstm_n[:, l], mlstm_m=state_mlstm_m[:, l])
        x, st = _layer(x, cond, lp, l, ctx_l, tabs_a, tabs_d)
        new.append(st)
    outs = tuple(jnp.stack([s[i] for s in new], axis=1) for i in range(8))
    return (x[:R_CTX].reshape(N_CTX_SEQ, T_CTX, D_MODEL), x[R_CTX:].reshape(N_LAT_SEQ, T_LAT, D_MODEL)) + outs
```

```python
import functools
import math

import jax
import jax.numpy as jnp
import numpy as np
from jax import lax
from jax.experimental import pallas as pl
from jax.experimental.pallas import tpu as pltpu

D_MODEL = 1024
DEPTH = 2
GRID_W = 64
ROPE_BASE = 10000.0
EPS = 1e-6
N_BRANCH = 4
W_BRANCH = D_MODEL // 2

H_A = 8
NOPE_A = 64
ROPE_A = 32
V_A = W_BRANCH // H_A
Q_RANK = D_MODEL // 4
KV_RANK = D_MODEL // 8
MLA_SCALE = (NOPE_A + ROPE_A) ** -0.5

W_B = W_BRANCH
NB_B = 8
BW_B = W_B // NB_B
CONV_W = 4
RG_C = 8.0

H_C = 4
DH_C = W_BRANCH // H_C
ML_CHUNK = 128

H_D = 4
DH_D = W_BRANCH // (2 * H_D)
DIFF_SCALE = DH_D ** -0.5

N_EXP = 8
TOP_K = 2

V7X_VMEM_LIMIT_BYTES = 56 * 1024 * 1024
LANES = 128
SUBLANES = 8

BF16 = jnp.bfloat16
F32 = jnp.float32

P_GATE = 0
P_SEG = N_BRANCH * D_MODEL
SEG_NAMES = ('rg_x', 'rg_gate', 'mq', 'mk', 'mv', 'mo', 'dq', 'dk', 'dv')
P_QLAT = P_SEG + 9 * W_BRANCH
P_KVLAT = P_QLAT + Q_RANK
P_SMALL = P_KVLAT + KV_RANK
P_WIDTH = P_SMALL + LANES
SM_MI = ROPE_A
SM_MF = ROPE_A + 2 * H_C


def _seg_block(name):
    return (P_SEG + SEG_NAMES.index(name) * W_BRANCH) // W_BRANCH


N_CTX_SEQ, T_CTX = 16, 256
N_LAT_SEQ, T_LAT = 8, 1024
PAST_LEN = 512
R_CTX = N_CTX_SEQ * T_CTX
R_LAT = N_LAT_SEQ * T_LAT
R_ALL = R_CTX + R_LAT


def _cparams(sem):
    return pltpu.CompilerParams(dimension_semantics=sem, vmem_limit_bytes=V7X_VMEM_LIMIT_BYTES)


def _mod_row_of_tile(i, tm):
    n_ctx_tiles = R_CTX // tm
    per_seq = T_LAT // tm
    return jnp.where(i < n_ctx_tiles, 0, 1 + (i - n_ctx_tiles) // per_seq)


def _pos_block_of_tile(i, tm):
    n_ctx_tiles = R_CTX // tm
    per_seq = T_LAT // tm
    return jnp.where(i < n_ctx_tiles, per_seq, (i - n_ctx_tiles) % per_seq)


def _mod_kernel(c_ref, w_ref, b_ref, o_ref):
    c = c_ref[...]
    s = (c * jax.nn.sigmoid(c)).astype(BF16)
    o_ref[...] = jnp.dot(s, w_ref[...].astype(BF16), preferred_element_type=F32) + b_ref[...]


def modulation(cond, w_mod, b_mod):
    m, d = cond.shape
    n = w_mod.shape[1]
    tn = 1536
    return pl.pallas_call(
        _mod_kernel,
        grid=(n // tn,),
        in_specs=[pl.BlockSpec((m, d), lambda j: (0, 0)),
                  pl.BlockSpec((d, tn), lambda j: (0, j)),
                  pl.BlockSpec((1, tn), lambda j: (0, j))],
        out_specs=pl.BlockSpec((m, tn), lambda j: (0, j)),
        out_shape=jax.ShapeDtypeStruct((m, n), F32),
        compiler_params=_cparams(("arbitrary",)),
        name="modulation",
    )(cond, w_mod, b_mod.reshape(1, n))


def _in_proj_kernel(x_ref, mod_ref, g_ref, w_ref, o_ref, xn_ref):
    @pl.when(pl.program_id(1) == 0)
    def _():
        x = x_ref[...]
        r = lax.rsqrt(jnp.mean(x * x, axis=-1, keepdims=True) + EPS)
        sh = mod_ref[0, 0:1, :]
        sc = mod_ref[0, 1:2, :]
        xn_ref[...] = ((x * r * g_ref[...]) * (1.0 + sc) + sh).astype(BF16)

    o_ref[...] = jnp.dot(xn_ref[...], w_ref[...], preferred_element_type=F32).astype(o_ref.dtype)


def in_proj(x, mod, g, w_p):
    tm, tn = 1024, P_WIDTH // 4
    m, d = x.shape
    n = w_p.shape[1]
    return pl.pallas_call(
        _in_proj_kernel,
        grid=(m // tm, n // tn),
        in_specs=[pl.BlockSpec((tm, d), lambda i, j: (i, 0)),
                  pl.BlockSpec((1, 8, d), lambda i, j: (_mod_row_of_tile(i, tm), 0, 0)),
                  pl.BlockSpec((1, d), lambda i, j: (0, 0)),
                  pl.BlockSpec((d, tn), lambda i, j: (0, j))],
        out_specs=pl.BlockSpec((tm, tn), lambda i, j: (i, j)),
        out_shape=jax.ShapeDtypeStruct((m, n), BF16),
        scratch_shapes=[pltpu.VMEM((tm, d), BF16)],
        compiler_params=_cparams(("arbitrary", "arbitrary")),
        name="in_proj",
    )(x, mod, g.reshape(1, d), w_p)


def pack_w_in(w_in):
    o_rg = Q_RANK + KV_RANK + ROPE_A
    o_mi = o_rg + 6 * W_BRANCH
    o_dq = o_mi + 4 * H_C
    o_gate = o_dq + 3 * W_BRANCH
    pad = jnp.zeros((w_in.shape[0], LANES - ROPE_A - 4 * H_C), w_in.dtype)
    parts = [w_in[:, o_gate:], w_in[:, o_rg:o_mi], w_in[:, o_dq:o_gate], w_in[:, :Q_RANK + KV_RANK],
             w_in[:, Q_RANK + KV_RANK:o_rg], w_in[:, o_mi:o_dq], pad]
    return jnp.concatenate(parts, axis=1).astype(BF16)


def _rope_partner(rot_dim, lane_starts):
    nf = rot_dim // 4
    partner = np.arange(LANES)
    for s0 in lane_starts:
        for a in range(2):
            lo = s0 + a * 2 * nf
            partner[lo:lo + nf] = np.arange(lo + nf, lo + 2 * nf)
            partner[lo + nf:lo + 2 * nf] = np.arange(lo, lo + nf)
    return partner


def _rope_tables(rot_dim, lane_starts, tm):
    rows = T_LAT // GRID_W
    r, c = np.meshgrid(np.arange(rows, dtype=np.float32), np.arange(GRID_W, dtype=np.float32), indexing='ij')
    nf = rot_dim // 4
    inv = (np.float32(ROPE_BASE) ** (-np.arange(nf, dtype=np.float32) / np.float32(nf))).astype(np.float32)
    ang = np.stack([r.reshape(-1)[:, None] * inv, c.reshape(-1)[:, None] * inv], axis=1).astype(np.float32)
    cos, sin = np.cos(ang).astype(np.float32), np.sin(ang).astype(np.float32)
    tc = np.ones((T_LAT + tm, LANES), np.float32)
    ta = np.zeros((T_LAT + tm, LANES), np.float32)
    tb = np.zeros((T_LAT + tm, LANES), np.float32)
    for s0 in lane_starts:
        for a in range(2):
            lo = s0 + a * 2 * nf
            tc[:T_LAT, lo:lo + nf] = cos[:, a]
            tc[:T_LAT, lo + nf:lo + 2 * nf] = cos[:, a]
            ta[:T_LAT, lo:lo + nf] = -sin[:, a]
            tb[:T_LAT, lo + nf:lo + 2 * nf] = sin[:, a]
    return jnp.asarray(tc), jnp.asarray(ta + tb)


MLA_TM = 512
ATTN_TQ = 512
MLA_HEADS_PER_STEP = 8
QK_A = NOPE_A + ROPE_A


def _mla_prep_kernel(*refs, has_q, norm_ckv):
    if has_q:
        (qlat_ref, gq_ref, wuq_ref, gqn_ref, kv_ref, sm_ref, gkv_ref, wkc_ref, wv_ref, gkn_ref,
         c_ref, s_ref, q_o, k_o, v_o, ckv_o, kr_o) = refs
        c, sn = c_ref[...], s_ref[...]
    else:
        (kv_ref, sm_ref, gkv_ref, wkc_ref, wv_ref, gkn_ref, k_o, v_o) = refs

    def heads(z, g_ref, o_ref, scale):
        for h in range(H_A):
            s = z[:, h * LANES:(h + 1) * LANES]
            r = lax.rsqrt(jnp.sum(s * s, axis=-1, keepdims=True) * (1.0 / QK_A) + EPS)
            y = s * r * g_ref[0:1, :]
            if has_q:
                sw = z[:, (H_A + h) * LANES:(H_A + h + 1) * LANES]
                y = y * c + (sw * r * g_ref[1:2, :]) * sn
            if scale != 1.0:
                y = y * scale
            o_ref[:, h * LANES:(h + 1) * LANES] = y.astype(o_ref.dtype)

    if has_q:
        ql = qlat_ref[...].astype(F32)
        qn = ql * lax.rsqrt(jnp.mean(ql * ql, axis=-1, keepdims=True) + EPS) * gq_ref[...]
        q = jnp.dot(qn.astype(BF16), wuq_ref[...], preferred_element_type=F32)
        heads(q, gqn_ref, q_o, MLA_SCALE)

    kv = kv_ref[...].astype(F32)
    if norm_ckv:
        ckv = kv * lax.rsqrt(jnp.mean(kv * kv, axis=-1, keepdims=True) + EPS) * gkv_ref[...]
    else:
        ckv = kv
    sm = sm_ref[...]
    ckv_b = ckv.astype(BF16)
    kin = jnp.concatenate([ckv_b, sm.astype(BF16)], axis=1)
    wkc = wkc_ref[...] if has_q else wkc_ref[:, :H_A * LANES]
    k = jnp.dot(kin, wkc, preferred_element_type=F32)
    heads(k, gkn_ref, k_o, 1.0)
    v_o[...] = jnp.dot(ckv_b, wv_ref[...], preferred_element_type=F32).astype(v_o.dtype)
    if has_q:
        ckv_o[...] = ckv
        kr_o[...] = sm.astype(F32)


def pack_mla_weights(lp):
    wuq = lp['w_mla_uq'].reshape(Q_RANK, H_A, QK_A)
    wuq_p = jnp.pad(wuq, ((0, 0), (0, 0), (0, LANES - QK_A))).reshape(Q_RANK, H_A * LANES).astype(BF16)
    wukv = lp['w_mla_ukv'].reshape(KV_RANK, H_A, NOPE_A + V_A)
    wk = jnp.pad(wukv[:, :, :NOPE_A], ((0, 0), (0, 0), (0, LANES - NOPE_A))).reshape(KV_RANK, H_A * LANES)
    place = np.zeros((LANES, H_A, LANES), np.float32)
    for h in range(H_A):
        place[np.arange(ROPE_A), h, NOPE_A + np.arange(ROPE_A)] = 1.0
    wkc = jnp.concatenate([wk, jnp.asarray(place.reshape(LANES, H_A * LANES))], axis=0).astype(BF16)
    wv = wukv[:, :, NOPE_A:]
    wv_even = jnp.pad(wv, ((0, 0), (0, 0), (0, LANES - V_A)))
    wv_odd = jnp.pad(wv, ((0, 0), (0, 0), (LANES - V_A, 0)))
    odd = (np.arange(H_A) % 2 == 1)[None, :, None]
    wv_p = jnp.where(odd, wv_odd, wv_even).reshape(KV_RANK, H_A * LANES).astype(BF16)
    partner = _rope_partner(ROPE_A, (NOPE_A,))
    cols = (np.arange(H_A)[:, None] * LANES + partner[None, :]).reshape(-1)
    with_partner = lambda w: jnp.concatenate([w, w[:, cols]], axis=1)
    pad_g = lambda g: jnp.pad(g, (0, LANES - QK_A))
    gain2 = lambda g: jnp.stack([pad_g(g), pad_g(g)[partner]])
    return dict(wuq=with_partner(wuq_p), wkc=with_partner(wkc), wv=wv_p,
                gqn=gain2(lp['g_mla_qn']), gkn=gain2(lp['g_mla_kn']),
                gq=lp['g_mla_qlat'].reshape(1, Q_RANK), gkv=lp['g_mla_kvlat'].reshape(1, KV_RANK))


def mla_prep(p, mw, tabs):
    tm = MLA_TM
    n = R_ALL // tm
    full = lambda shape: pl.BlockSpec(shape, lambda i: (0, 0))
    tab = pl.BlockSpec((tm, LANES), lambda i: (_pos_block_of_tile(i, tm), 0))
    wide = H_A * LANES
    out_shape = (jax.ShapeDtypeStruct((R_ALL, wide), BF16), jax.ShapeDtypeStruct((R_ALL, wide), BF16),
                 jax.ShapeDtypeStruct((R_ALL, wide), BF16), jax.ShapeDtypeStruct((R_ALL, KV_RANK), F32),
                 jax.ShapeDtypeStruct((R_ALL, LANES), F32))
    row = lambda w: pl.BlockSpec((tm, w), lambda i: (i, 0))
    return pl.pallas_call(
        functools.partial(_mla_prep_kernel, has_q=True, norm_ckv=True),
        grid=(n,),
        in_specs=[pl.BlockSpec((tm, Q_RANK), lambda i: (i, P_QLAT // Q_RANK)), full((1, Q_RANK)),
                  full((Q_RANK, 2 * wide)), full((2, LANES)),
                  pl.BlockSpec((tm, KV_RANK), lambda i: (i, P_KVLAT // KV_RANK)),
                  pl.BlockSpec((tm, LANES), lambda i: (i, P_SMALL // LANES)), full((1, KV_RANK)),
                  full((2 * LANES, 2 * wide)), full((KV_RANK, wide)), full((2, LANES)), tab, tab],
        out_specs=(row(wide), row(wide), row(wide), row(KV_RANK), row(LANES)),
        out_shape=out_shape,
        compiler_params=_cparams(("arbitrary",)),
        name="mla_prep",
    )(p, mw['gq'], mw['wuq'], mw['gqn'], p, p, mw['gkv'], mw['wkc'], mw['wv'], mw['gkn'], *tabs)


def mla_prep_cache(ckv_c, kr_c, mw):
    tm = MLA_TM
    r = ckv_c.shape[0]
    full = lambda shape: pl.BlockSpec(shape, lambda i: (0, 0))
    wide = H_A * LANES
    row = lambda w: pl.BlockSpec((tm, w), lambda i: (i, 0))
    return pl.pallas_call(
        functools.partial(_mla_prep_kernel, has_q=False, norm_ckv=False),
        grid=(r // tm,),
        in_specs=[row(KV_RANK), row(LANES), full((1, KV_RANK)), full((2 * LANES, 2 * wide)), full((KV_RANK, wide)),
                  full((2, LANES))],
        out_specs=(row(wide), row(wide)),
        out_shape=(jax.ShapeDtypeStruct((r, wide), BF16), jax.ShapeDtypeStruct((r, wide), BF16)),
        compiler_params=_cparams(("arbitrary",)),
        name="mla_prep_cache",
    )(ckv_c, kr_c, mw['gkv'], mw['wkc'], mw['wv'], mw['gkn'])


_NT = (((1,), (1,)), ((), ()))


def _mla_attn_kernel(*refs, has_cache):
    if has_cache:
        q_ref, ko_ref, vo_ref, kc_ref, vc_ref, o_ref = refs
    else:
        q_ref, ko_ref, vo_ref, o_ref = refs
    n_heads = q_ref.shape[1] // LANES
    sls = [slice(h * LANES, (h + 1) * LANES) for h in range(n_heads)]
    def scores(sl):
        s_o = lax.dot_general(q_ref[:, sl], ko_ref[:, sl], _NT, preferred_element_type=F32)
        s_c = lax.dot_general(q_ref[:, sl], kc_ref[:, sl], _NT, preferred_element_type=F32) if has_cache else None
        return s_o, s_c

    outs = []
    nxt = scores(sls[0])
    for h, sl in enumerate(sls):
        s_o, s_c = nxt
        if h + 1 < n_heads:
            nxt = scores(sls[h + 1])
        m = jnp.max(s_o, axis=-1, keepdims=True)
        if has_cache:
            m = jnp.maximum(m, jnp.max(s_c, axis=-1, keepdims=True))
        e_o = jnp.exp(s_o - m)
        l = jnp.sum(e_o, axis=-1, keepdims=True)
        pv = jnp.dot(e_o.astype(BF16), vo_ref[:, sl], preferred_element_type=F32)
        if has_cache:
            e_c = jnp.exp(s_c - m)
            l = l + jnp.sum(e_c, axis=-1, keepdims=True)
            pv = pv + jnp.dot(e_c.astype(BF16), vc_ref[:, sl], preferred_element_type=F32)
        outs.append(pv * (1.0 / l))
    for p in range(n_heads // 2):
        o_ref[:, p * LANES:(p + 1) * LANES] = (outs[2 * p] + outs[2 * p + 1]).astype(o_ref.dtype)


def mla_attention(q, k, v, kc, vc, *, row0, n_seq, t_seq, tq):
    has_cache = kc is not None
    wh = MLA_HEADS_PER_STEP * LANES
    n_grp = H_A // MLA_HEADS_PER_STEP
    nq = t_seq // tq
    qb0, kb0 = row0 // tq, row0 // t_seq
    in_specs = [pl.BlockSpec((tq, wh), lambda s, p, i: (qb0 + s * nq + i, p)),
                pl.BlockSpec((t_seq, wh), lambda s, p, i: (kb0 + s, p)),
                pl.BlockSpec((t_seq, wh), lambda s, p, i: (kb0 + s, p))]
    args = [q, k, v]
    if has_cache:
        in_specs += [pl.BlockSpec((PAST_LEN, wh), lambda s, p, i: (s, p)),
                     pl.BlockSpec((PAST_LEN, wh), lambda s, p, i: (s, p))]
        args += [kc, vc]
    return pl.pallas_call(
        functools.partial(_mla_attn_kernel, has_cache=has_cache),
        grid=(n_seq, n_grp, nq),
        in_specs=in_specs,
        out_specs=pl.BlockSpec((tq, wh // 2), lambda s, p, i: (s * nq + i, p)),
        out_shape=jax.ShapeDtypeStruct((n_seq * t_seq, W_BRANCH), BF16),
        compiler_params=_cparams(("arbitrary", "arbitrary", "arbitrary")),
        name="mla_attention",
    )(*args)


DIFF_TM = 512


def _diff_prep_kernel(dq_ref, dk_ref, gq_ref, gk_ref, perm_ref, c_ref, s_ref, q_o, ko_o, kp_o):
    c, sn = c_ref[...], s_ref[...]
    lane = lax.broadcasted_iota(jnp.int32, (1, LANES), 1)
    lo = lane < DH_D

    def inv_rms(x):
        x2 = x * x
        s_lo = jnp.sum(jnp.where(lo, x2, 0.0), axis=-1, keepdims=True)
        s_hi = jnp.sum(jnp.where(lo, 0.0, x2), axis=-1, keepdims=True)
        return lax.rsqrt(jnp.where(lo, s_lo, s_hi) * (1.0 / DH_D) + EPS)

    def rotated(n, x_sw, r, g_ref):
        return n * c + (x_sw * r * g_ref[1:2, :]) * sn

    for j in range(W_BRANCH // LANES):
        sl = slice(j * LANES, (j + 1) * LANES)
        xq, xk = dq_ref[:, sl], dk_ref[:, sl]
        q_sw = jnp.dot(xq, perm_ref[...], preferred_element_type=F32)
        k_sw = jnp.dot(xk, perm_ref[...], preferred_element_type=F32)
        xq, xk = xq.astype(F32), xk.astype(F32)
        rq, rk = inv_rms(xq), inv_rms(xk)
        qn = xq * rq * gq_ref[0:1, :]
        kn = xk * rk * gk_ref[0:1, :]
        q_o[:, sl] = (rotated(qn, q_sw, rq, gq_ref) * DIFF_SCALE).astype(q_o.dtype)
        kp_o[:, sl] = kn
        ko_o[:, sl] = rotated(kn, k_sw, rk, gk_ref).astype(ko_o.dtype)


def diff_prep(p, lp, tabs):
    tm = DIFF_TM
    partner = _rope_partner(DH_D, (0, DH_D))
    perm = np.zeros((LANES, LANES), np.float32)
    perm[partner, np.arange(LANES)] = 1.0
    gain2 = lambda g: jnp.stack([jnp.concatenate([g, g]), jnp.concatenate([g, g])[partner]])
    full = lambda shape: pl.BlockSpec(shape, lambda i: (0, 0))
    tab = pl.BlockSpec((tm, LANES), lambda i: (_pos_block_of_tile(i, tm), 0))
    row = pl.BlockSpec((tm, W_BRANCH), lambda i: (i, 0))
    dq_b, dk_b = _seg_block('dq'), _seg_block('dk')
    return pl.pallas_call(
        _diff_prep_kernel,
        grid=(R_ALL // tm,),
        in_specs=[pl.BlockSpec((tm, W_BRANCH), lambda i: (i, dq_b)), pl.BlockSpec((tm, W_BRANCH), lambda i: (i, dk_b)),
                  full((2, LANES)), full((2, LANES)), full((LANES, LANES)), tab, tab],
        out_specs=(row, row, row),
        out_shape=(jax.ShapeDtypeStruct((R_ALL, W_BRANCH), BF16), jax.ShapeDtypeStruct((R_ALL, W_BRANCH), BF16),
                   jax.ShapeDtypeStruct((R_ALL, W_BRANCH), F32)),
        compiler_params=_cparams(("arbitrary",)),
        name="diff_prep",
    )(p, p, gain2(lp['g_diff_qn']), gain2(lp['g_diff_kn']), jnp.asarray(perm, BF16), *tabs)


def _diff_attn_kernel(*refs, has_cache, lambda_init):
    if has_cache:
        dl_ref, q_ref, k_ref, v_ref, kc_ref, vc_ref, g_ref, o_ref = refs
    else:
        dl_ref, q_ref, k_ref, v_ref, g_ref, o_ref = refs
    lane = lax.broadcasted_iota(jnp.int32, (1, LANES), 1)
    dl = dl_ref[...]
    lam = (jnp.exp(jnp.sum(dl[0:1] * dl[1:2], axis=-1, keepdims=True))
           - jnp.exp(jnp.sum(dl[2:3] * dl[3:4], axis=-1, keepdims=True)) + lambda_init)
    units = [(h, w) for h in range(H_D) for w in range(2)]

    def scores(unit):
        h, w = unit
        sl = slice((2 * w + h // 2) * LANES, (2 * w + h // 2 + 1) * LANES)
        q = jnp.where((lane // DH_D) == (h % 2), q_ref[:, sl], jnp.zeros((), q_ref.dtype))
        s_o = lax.dot_general(q, k_ref[:, sl], _NT, preferred_element_type=F32)
        s_c = (lax.dot_general(q, kc_ref[:, sl].astype(BF16), _NT, preferred_element_type=F32)
               if has_cache else None)
        return s_o, s_c

    outs = []
    nxt = scores(units[0])
    for n, (h, w) in enumerate(units):
        s_o, s_c = nxt
        if n + 1 < len(units):
            nxt = scores(units[n + 1])
        vs = slice(h * LANES, (h + 1) * LANES)
        m = jnp.max(s_o, axis=-1, keepdims=True)
        if has_cache:
            m = jnp.maximum(m, jnp.max(s_c, axis=-1, keepdims=True))
        e_o = jnp.exp(s_o - m)
        l = jnp.sum(e_o, axis=-1, keepdims=True)
        pv = jnp.dot(e_o.astype(BF16), v_ref[:, vs], preferred_element_type=F32)
        if has_cache:
            e_c = jnp.exp(s_c - m)
            l = l + jnp.sum(e_c, axis=-1, keepdims=True)
            pv = pv + jnp.dot(e_c.astype(BF16), vc_ref[:, vs].astype(BF16), preferred_element_type=F32)
        outs.append(pv * (1.0 / l))
    for h in range(H_D):
        y = outs[2 * h] - lam * outs[2 * h + 1]
        r = lax.rsqrt(jnp.mean(y * y, axis=-1, keepdims=True) + EPS)
        o_ref[:, h * LANES:(h + 1) * LANES] = ((y * r * g_ref[...]) * (1.0 - lambda_init)).astype(o_ref.dtype)


def diff_attention(dl, qd, kd, p, kc, vc, g_sub, *, row0, n_seq, t_seq, tq, lambda_init):
    has_cache = kc is not None
    nq = t_seq // tq
    qb0, kb0 = row0 // tq, row0 // t_seq
    in_specs = [pl.BlockSpec((4, DH_D), lambda s, i: (0, 0)),
                pl.BlockSpec((tq, W_BRANCH), lambda s, i: (qb0 + s * nq + i, 0)),
                pl.BlockSpec((t_seq, W_BRANCH), lambda s, i: (kb0 + s, 0)),
                pl.BlockSpec((t_seq, W_BRANCH), lambda s, i: (kb0 + s, _seg_block('dv')))]
    args = [dl, qd, kd, p]
    if has_cache:
        in_specs += [pl.BlockSpec((PAST_LEN, W_BRANCH), lambda s, i: (s, 0)),
                     pl.BlockSpec((PAST_LEN, W_BRANCH), lambda s, i: (s, 0))]
        args += [kc, vc]
    in_specs.append(pl.BlockSpec((1, LANES), lambda s, i: (0, 0)))
    args.append(g_sub.reshape(1, LANES))
    return pl.pallas_call(
        functools.partial(_diff_attn_kernel, has_cache=has_cache, lambda_init=lambda_init),
        grid=(n_seq, nq),
        in_specs=in_specs,
        out_specs=pl.BlockSpec((tq, W_BRANCH), lambda s, i: (s * nq + i, 0)),
        out_shape=jax.ShapeDtypeStruct((n_seq * t_seq, W_BRANCH), BF16),
        compiler_params=_cparams(("arbitrary", "arbitrary")),
        name="diff_attention",
    )(*args)


def _softplus(z):
    return jnp.maximum(z, 0.0) + jnp.log(1.0 + jnp.exp(-jnp.abs(z)))


def _gelu_tanh(x):
    return 0.5 * x * (1.0 + jnp.tanh(math.sqrt(2.0 / math.pi) * (x + 0.044715 * (x * x * x))))


def _rglru_kernel(x_ref, gate_ref, wc_ref, bc_ref, wg_ref, bg_ref, lam_ref, h0_ref, y_ref, st_ref,
                  af_s, uf_s, ab_s, ub_s, *, t_seq):
    t = t_seq
    x = x_ref[...].astype(F32)
    row = lax.broadcasted_iota(jnp.int32, (t, W_B), 0)
    wc = wc_ref[...]
    xc = (wc[0:1] * jnp.where(row >= 2, pltpu.roll(x, 2, 0), 0.0)
          + wc[1:2] * jnp.where(row >= 1, pltpu.roll(x, 1, 0), 0.0)
          + wc[2:3] * x
          + wc[3:4] * jnp.where(row < t - 1, pltpu.roll(x, t - 1, 0), 0.0)
          + bc_ref[...])
    gates = jnp.dot(xc.astype(BF16), wg_ref[...], preferred_element_type=F32) + bg_ref[...]
    for d, (a_s, u_s) in enumerate(((af_s, uf_s), (ab_s, ub_s))):
        rg = jax.nn.sigmoid(gates[:, (2 * d) * W_B:(2 * d + 1) * W_B])
        ig = jax.nn.sigmoid(gates[:, (2 * d + 1) * W_B:(2 * d + 2) * W_B])
        log_a = -RG_C * rg * _softplus(-lam_ref[d:d + 1, :])
        a = jnp.exp(log_a)
        a_s[...] = a
        u_s[...] = jnp.sqrt(-jnp.tanh(log_a) * (a * a + 1.0)) * (ig * xc)

    nblk = t // SUBLANES

    def body(k, carry):
        hf, hb = carry
        base_f = pl.multiple_of(k * SUBLANES, SUBLANES)
        base_b = pl.multiple_of((nblk - 1 - k) * SUBLANES, SUBLANES)
        for r in range(SUBLANES):
            rf = pl.ds(base_f + r, 1)
            rb = pl.ds(base_b + (SUBLANES - 1 - r), 1)
            hf = af_s[rf, :] * hf + uf_s[rf, :]
            hb = ab_s[rb, :] * hb + ub_s[rb, :]
            uf_s[rf, :] = hf
            ub_s[rb, :] = hb
        return hf, hb

    hf, hb = lax.fori_loop(0, nblk, body, (h0_ref[0, 0:1, :], h0_ref[0, 1:2, :]))
    st_ref[0, 0:1, :] = hf
    st_ref[0, 1:2, :] = hb
    y_ref[...] = (_gelu_tanh(gate_ref[...].astype(F32)) * (uf_s[...] + ub_s[...])).astype(y_ref.dtype)


def pack_rglru_weights(lp):
    def blockdiag(w):
        eye = jnp.eye(NB_B, dtype=w.dtype)
        return jnp.einsum('ncd,nm->ncmd', w, eye).reshape(W_B, W_B)
    wg = jnp.concatenate([blockdiag(lp['w_rg_a'][0]), blockdiag(lp['w_rg_x'][0]),
                          blockdiag(lp['w_rg_a'][1]), blockdiag(lp['w_rg_x'][1])], axis=1).astype(BF16)
    bg = jnp.concatenate([lp['b_rg_a'][0], lp['b_rg_x'][0], lp['b_rg_a'][1], lp['b_rg_x'][1]]).reshape(1, 4 * W_B)
    return dict(wg=wg, bg=bg, wc=lp['w_conv_rg'], bc=lp['b_conv_rg'].reshape(1, W_B), lam=lp['rg_lambda'])


def rglru(p, rw, h0, *, row0, n_seq, t_seq):
    rb0 = row0 // t_seq
    xb, gb = _seg_block('rg_x'), _seg_block('rg_gate')
    full = lambda shape: pl.BlockSpec(shape, lambda s: tuple(0 for _ in shape))
    in_specs = [pl.BlockSpec((t_seq, W_B), lambda s: (rb0 + s, xb)),
                pl.BlockSpec((t_seq, W_B), lambda s: (rb0 + s, gb)),
                full((CONV_W, W_B)), full((1, W_B)), full((W_B, 4 * W_B)), full((1, 4 * W_B)), full((2, W_B)),
                pl.BlockSpec((1, 2, W_B), lambda s: (s, 0, 0))]
    args = [p, p, rw['wc'], rw['bc'], rw['wg'], rw['bg'], rw['lam'], h0]
    return pl.pallas_call(
        functools.partial(_rglru_kernel, t_seq=t_seq),
        grid=(n_seq,),
        in_specs=in_specs,
        out_specs=(pl.BlockSpec((t_seq, W_B), lambda s: (s, 0)), pl.BlockSpec((1, 2, W_B), lambda s: (s, 0, 0))),
        out_shape=(jax.ShapeDtypeStruct((n_seq * t_seq, W_B), BF16), jax.ShapeDtypeStruct((n_seq, 2, W_B), F32)),
        scratch_shapes=[pltpu.VMEM((t_seq, W_B), F32)] * 4,
        compiler_params=_cparams(("arbitrary",)),
        name="rglru",
    )(*args)


def _dot_split(a, b_bf16):
    hi = a.astype(BF16)
    lo = (a - hi.astype(F32)).astype(BF16)
    return (jnp.dot(hi, b_bf16, preferred_element_type=F32) + jnp.dot(lo, b_bf16, preferred_element_type=F32))


def _log_sigmoid(z):
    return jnp.minimum(z, 0.0) - jnp.log(1.0 + jnp.exp(-jnp.abs(z)))


_TN = (((0,), (0,)), ((), ()))


def _mlstm_kernel(q_ref, k_ref, v_ref, o_ref, sm_ref, bias_ref, g_ref, c0_ref, n0_ref, m0_ref,
                  y_ref, c_out, n_out, m_out, hm_s, c_s, *, t_seq):
    L = ML_CHUNK
    nchunk = t_seq // L
    scale = DH_C ** -0.5
    ri = lax.broadcasted_iota(jnp.int32, (L, L), 0)
    ci = lax.broadcasted_iota(jnp.int32, (L, L), 1)
    lane1 = lax.broadcasted_iota(jnp.int32, (L, LANES), 1)
    ones_col = jnp.where(lane1 == 0, 1.0, 0.0).astype(BF16)
    bias = bias_ref[...]

    for d in range(2):
        causal = (ci <= ri) if d == 0 else (ci >= ri)
        tri = jnp.where(causal, 1.0, 0.0).astype(BF16)
        tri_t = jnp.where((ri <= ci) if d == 0 else (ri >= ci), 1.0, 0.0).astype(BF16)
        for h in range(H_C):
            c_s[h, :, 0:DH_C] = c0_ref[0, d, h]
            c_s[h, :, DH_C:2 * DH_C] = jnp.where(lane1 == 0, n0_ref[0, d, h], 0.0)
        m_init = tuple(m0_ref[0, :, d * H_C + h:d * H_C + h + 1] for h in range(H_C))

        def chunk(kk, ms, d=d, causal=causal, tri=tri, tri_t=tri_t):
            cidx = kk if d == 0 else nchunk - 1 - kk
            rows = pl.ds(pl.multiple_of(cidx * L, L), L)
            gsm = sm_ref[rows, :].astype(F32) + bias
            lf_all = _log_sigmoid(gsm)
            cum_cols = _dot_split_left(tri, lf_all)
            g_t = gsm.T
            cum_rows = _dot_split(lf_all.T, tri_t)
            heads = range(H_C)
            sls = [slice(h * DH_C, (h + 1) * DH_C) for h in heads]
            qs = [q_ref[rows, sl] for sl in sls]
            ks = [k_ref[rows, sl] for sl in sls]
            v_augs = [jnp.concatenate([v_ref[rows, sl], ones_col], axis=1) for sl in sls]
            qk = [lax.dot_general(qs[h], ks[h], _NT, preferred_element_type=F32) for h in heads]
            qc = [jnp.dot(qs[h], c_s[h].astype(BF16), preferred_element_type=F32) * scale for h in heads]
            cum_c = [cum_cols[:, SM_MF + d * H_C + h:SM_MF + d * H_C + h + 1] for h in heads]
            li_c = [gsm[:, SM_MI + d * H_C + h:SM_MI + d * H_C + h + 1] for h in heads]
            m_row, s = [], []
            for h in heads:
                jl, jf = SM_MI + d * H_C + h, SM_MF + d * H_C + h
                log_d = jnp.where(causal, cum_c[h] - cum_rows[jf:jf + 1, :] + g_t[jl:jl + 1, :], -jnp.inf)
                m_row.append(jnp.maximum(cum_c[h] + ms[h], jnp.max(log_d, axis=-1, keepdims=True)))
                s.append(qk[h] * (scale * jnp.exp(log_d - m_row[h])))
            sv = [jnp.dot(s[h].astype(BF16), v_augs[h], preferred_element_type=F32) for h in heads]
            new_ms = []
            for h in heads:
                w_inter = jnp.exp(cum_c[h] + ms[h] - m_row[h])
                nd = sv[h] + qc[h] * w_inter
                den = jnp.maximum(jnp.abs(nd[:, DH_C:DH_C + 1]), jnp.exp(-m_row[h]))
                h_out = nd[:, :DH_C] * (1.0 / den)
                if d == 0:
                    hm_s[rows, sls[h]] = h_out
                else:
                    hm_s[rows, sls[h]] = hm_s[rows, sls[h]] + h_out
                last = cum_c[h][L - 1:L, :] if d == 0 else cum_c[h][0:1, :]
                w_s = last - cum_c[h] + li_c[h]
                m_new = jnp.maximum(last + ms[h], jnp.max(w_s, axis=0, keepdims=True))
                decay = jnp.exp(last + ms[h] - m_new)
                kw_t = (ks[h].astype(F32) * jnp.exp(w_s - m_new)).T.astype(BF16)
                c_s[h] = decay * c_s[h] + jnp.dot(kw_t, v_augs[h], preferred_element_type=F32)
                new_ms.append(m_new)
            return tuple(new_ms)

        m_fin = lax.fori_loop(0, nchunk, chunk, m_init)
        for h in range(H_C):
            c_out[0, d, h] = c_s[h, :, 0:DH_C]
            n_out[0, d, h] = c_s[h, :, DH_C:DH_C + 1]
            m_out[0, :, d * H_C + h:d * H_C + h + 1] = m_fin[h]

    for h in range(H_C):
        sl = slice(h * DH_C, (h + 1) * DH_C)
        hm = hm_s[:, sl]
        r = lax.rsqrt(jnp.mean(hm * hm, axis=-1, keepdims=True) + EPS)
        y_ref[:, sl] = (jax.nn.sigmoid(o_ref[:, sl].astype(F32)) * (hm * r * g_ref[...])).astype(y_ref.dtype)


def _dot_split_left(a_bf16, b):
    hi = b.astype(BF16)
    lo = (b - hi.astype(F32)).astype(BF16)
    return (jnp.dot(a_bf16, hi, preferred_element_type=F32) + jnp.dot(a_bf16, lo, preferred_element_type=F32))


def mlstm(p, bias_sm, g_out, c0, n0, m0, *, row0, n_seq, t_seq):
    rb0 = row0 // t_seq
    seg = lambda nm: pl.BlockSpec((t_seq, W_BRANCH), lambda s, b=_seg_block(nm): (rb0 + s, b))
    full = lambda shape: pl.BlockSpec(shape, lambda s: tuple(0 for _ in shape))
    c_spec = pl.BlockSpec((1, 2, H_C, DH_C, DH_C), lambda s: (s, 0, 0, 0, 0))
    n_spec = pl.BlockSpec((1, 2, H_C, DH_C, 1), lambda s: (s, 0, 0, 0, 0))
    m_spec = pl.BlockSpec((1, 1, 2 * H_C), lambda s: (s, 0, 0))
    in_specs = [seg('mq'), seg('mk'), seg('mv'), seg('mo'),
                pl.BlockSpec((t_seq, LANES), lambda s: (rb0 + s, P_SMALL // LANES)),
                full((1, LANES)), full((1, DH_C)), c_spec, n_spec, m_spec]
    args = [p, p, p, p, p, bias_sm, g_out.reshape(1, DH_C), c0, n0, m0]
    return pl.pallas_call(
        functools.partial(_mlstm_kernel, t_seq=t_seq),
        grid=(n_seq,),
        in_specs=in_specs,
        out_specs=(pl.BlockSpec((t_seq, W_BRANCH), lambda s: (s, 0)), c_spec, n_spec, m_spec),
        out_shape=(jax.ShapeDtypeStruct((n_seq * t_seq, W_BRANCH), BF16),
                   jax.ShapeDtypeStruct((n_seq, 2, H_C, DH_C, DH_C), F32),
                   jax.ShapeDtypeStruct((n_seq, 2, H_C, DH_C, 1), F32),
                   jax.ShapeDtypeStruct((n_seq, 1, 2 * H_C), F32)),
        scratch_shapes=[pltpu.VMEM((t_seq, W_BRANCH), F32), pltpu.VMEM((H_C, DH_C, 2 * DH_C), F32)],
        compiler_params=_cparams(("arbitrary",)),
        name="mlstm",
    )(*args)


MERGE_TM = 512


def _merge_kernel(*refs):
    ctx_refs, lat_refs, gate_refs = refs[0:4], refs[4:8], refs[8:12]
    x_ref, mod_ref, wbr_ref, wout_ref, gn_ref, xo_ref, xn_ref = refs[12:19]
    xn3_ref = refs[19] if len(refs) > 19 else None
    is_ctx = pl.program_id(0) < R_CTX // MERGE_TM
    merged = None
    for g in range(N_BRANCH):
        yg = jnp.where(is_ctx, ctx_refs[g][...], lat_refs[g][...])
        pg = jnp.dot(yg, wbr_ref[g], preferred_element_type=F32)
        term = jax.nn.sigmoid(gate_refs[g][...].astype(F32)) * pg
        merged = term if merged is None else merged + term
    y = jnp.dot(merged.astype(BF16), wout_ref[...], preferred_element_type=F32)
    x = x_ref[...] + mod_ref[0, 2:3, :] * y
    xo_ref[...] = x
    r = lax.rsqrt(jnp.mean(x * x, axis=-1, keepdims=True) + EPS)
    xn = (x * r * gn_ref[...]) * (1.0 + mod_ref[0, 4:5, :]) + mod_ref[0, 3:4, :]
    xn_ref[...] = xn.astype(xn_ref.dtype)
    if xn3_ref is not None:
        _rows_to_tiles(xn3_ref, xn.astype(BF16))


def merge(ys_ctx, ys_lat, p, x, mod, wbr, wout, g_ffn, rows_as_tiles):
    tm = MERGE_TM
    n_ctx_tiles = R_CTX // tm
    br_ctx = pl.BlockSpec((tm, W_BRANCH), lambda i: (jnp.minimum(i, n_ctx_tiles - 1), 0))
    br_lat = pl.BlockSpec((tm, W_BRANCH), lambda i: (jnp.maximum(i - n_ctx_tiles, 0), 0))
    gate = lambda g: pl.BlockSpec((tm, D_MODEL), lambda i, g=g: (i, g))
    row = pl.BlockSpec((tm, D_MODEL), lambda i: (i, 0))
    out_specs = [row, row]
    out_shape = [jax.ShapeDtypeStruct((R_ALL, D_MODEL), F32), jax.ShapeDtypeStruct((R_ALL, D_MODEL), BF16)]
    if rows_as_tiles:
        out_specs.append(pl.BlockSpec((tm, ROW_SUB, LANES), lambda i: (i, 0, 0)))
        out_shape.append(jax.ShapeDtypeStruct((R_ALL, ROW_SUB, LANES), F32))
    return pl.pallas_call(
        _merge_kernel,
        grid=(R_ALL // tm,),
        in_specs=[br_ctx] * N_BRANCH + [br_lat] * N_BRANCH + [gate(0), gate(1), gate(2), gate(3), row,
                  pl.BlockSpec((1, 8, D_MODEL), lambda i: (_mod_row_of_tile(i, tm), 0, 0)),
                  pl.BlockSpec((N_BRANCH, W_BRANCH, D_MODEL), lambda i: (0, 0, 0)),
                  pl.BlockSpec((D_MODEL, D_MODEL), lambda i: (0, 0)),
                  pl.BlockSpec((1, D_MODEL), lambda i: (0, 0))],
        out_specs=tuple(out_specs),
        out_shape=tuple(out_shape),
        compiler_params=_cparams(("arbitrary",)),
        name="merge",
    )(*ys_ctx, *ys_lat, p, p, p, p, x, mod, wbr, wout, g_ffn.reshape(1, D_MODEL))


def _new_expert(te_ref, i):
    return jnp.logical_or(i == 0, te_ref[i] != te_ref[jnp.maximum(i - 1, 0)])


def _ffn_up_kernel(te_ref, nt_ref, x_ref, wg_ref, wu_ref, h_ref, wgb_ref, wub_ref):
    i = pl.program_id(1)

    @pl.when(_new_expert(te_ref, i))
    def _():
        wgb_ref[...] = wg_ref[0].astype(BF16)
        wub_ref[...] = wu_ref[0].astype(BF16)

    @pl.when(i < nt_ref[0])
    def _():
        x = x_ref[...]
        g = jnp.dot(x, wgb_ref[...], preferred_element_type=F32)
        u = jnp.dot(x, wub_ref[...], preferred_element_type=F32)
        h_ref[...] = (g * jax.nn.sigmoid(g) * u).astype(h_ref.dtype)

    @pl.when(i >= nt_ref[0])
    def _():
        h_ref[...] = jnp.zeros(h_ref.shape, h_ref.dtype)


def ffn_up(tile_expert, n_tiles, xs, wg, wu, tm, tf, weight_buffers):
    r, d = xs.shape
    f = wg.shape[2]
    w_spec = pl.BlockSpec((1, d, tf), lambda j, i, te, nt: (te[i], 0, j), pipeline_mode=pl.Buffered(weight_buffers))
    return pl.pallas_call(
        _ffn_up_kernel,
        grid_spec=pltpu.PrefetchScalarGridSpec(
            num_scalar_prefetch=2,
            grid=(f // tf, r // tm),
            in_specs=[pl.BlockSpec((tm, d), lambda j, i, te, nt: (i, 0)), w_spec, w_spec],
            out_specs=pl.BlockSpec((tm, tf), lambda j, i, te, nt: (i, j)),
            scratch_shapes=[pltpu.VMEM((d, tf), BF16), pltpu.VMEM((d, tf), BF16)]),
        out_shape=jax.ShapeDtypeStruct((r, f), BF16),
        compiler_params=_cparams(("arbitrary", "arbitrary")),
        name="ffn_up",
    )(tile_expert, n_tiles, xs, wg, wu)


def _ffn_down_kernel(te_ref, nt_ref, h_ref, wd_ref, y_ref, wdb_ref):
    i = pl.program_id(0)

    @pl.when(_new_expert(te_ref, i))
    def _():
        wdb_ref[...] = wd_ref[0].astype(BF16)

    @pl.when(i < nt_ref[0])
    def _():
        _rows_to_tiles(y_ref, jnp.dot(h_ref[...], wdb_ref[...], preferred_element_type=F32))

    @pl.when(i >= nt_ref[0])
    def _():
        y_ref[...] = jnp.zeros(y_ref.shape, y_ref.dtype)


def ffn_down(tile_expert, n_tiles, h, wd, tm):
    r, f = h.shape
    d = wd.shape[2]
    return pl.pallas_call(
        _ffn_down_kernel,
        grid_spec=pltpu.PrefetchScalarGridSpec(
            num_scalar_prefetch=2,
            grid=(r // tm,),
            in_specs=[pl.BlockSpec((tm, f), lambda i, te, nt: (i, 0)),
                      pl.BlockSpec((1, f, d), lambda i, te, nt: (te[i], 0, 0))],
            out_specs=pl.BlockSpec((tm, ROW_SUB, LANES), lambda i, te, nt: (i, 0, 0)),
            scratch_shapes=[pltpu.VMEM((f, d), BF16)]),
        out_shape=jax.ShapeDtypeStruct((r, ROW_SUB, LANES), F32),
        compiler_params=_cparams(("arbitrary",)),
        name="ffn_down",
    )(tile_expert, n_tiles, h, wd)


def _ffn_down_res_kernel(h_ref, wd_ref, x_ref, mod_ref, y_ref, wdb_ref):
    @pl.when(pl.program_id(0) == 0)
    def _():
        wdb_ref[...] = wd_ref[...].astype(BF16)

    y = jnp.dot(h_ref[...], wdb_ref[...], preferred_element_type=F32)
    y_ref[...] = x_ref[...] + mod_ref[0, 5:6, :] * y


def ffn_down_residual(h, wd, x, mod):
    tm = 1024
    r, f = h.shape
    d = wd.shape[1]
    return pl.pallas_call(
        _ffn_down_res_kernel,
        grid=(r // tm,),
        in_specs=[pl.BlockSpec((tm, f), lambda i: (i, 0)),
                  pl.BlockSpec((f, d), lambda i: (0, 0), pipeline_mode=pl.Buffered(1)),
                  pl.BlockSpec((tm, d), lambda i: (i, 0)),
                  pl.BlockSpec((1, 8, d), lambda i: (_mod_row_of_tile(i, tm), 0, 0))],
        out_specs=pl.BlockSpec((tm, d), lambda i: (i, 0)),
        out_shape=jax.ShapeDtypeStruct((r, d), F32),
        scratch_shapes=[pltpu.VMEM((f, d), BF16)],
        compiler_params=_cparams(("arbitrary",)),
        name="ffn_down_residual",
    )(h, wd, x, mod)


def dense_swiglu_residual(xn, x, mod, wg, wu, wd):
    t = xn.shape[0]
    tm = 1024
    n_tiles = t // tm
    te = jnp.zeros((n_tiles,), jnp.int32)
    nt = jnp.full((1,), n_tiles, jnp.int32)
    h = ffn_up(te, nt, xn, wg[None], wu[None], tm=tm, tf=wg.shape[1] // 2, weight_buffers=1)
    return ffn_down_residual(h, wd, x, mod)


MOE_TM = 512
DISPATCH_TM = 512


def _moe_routing(logits, tm):
    t = logits.shape[0]
    n_assign = t * TOP_K
    top_v, top_i = lax.top_k(logits, TOP_K)
    gate = jax.nn.softmax(top_v, axis=-1)
    flat_e = top_i.reshape(-1).astype(jnp.int32)
    onehot = (flat_e[:, None] == jnp.arange(N_EXP, dtype=jnp.int32)[None, :])
    blk = LANES
    oh = onehot.astype(F32).reshape(n_assign // blk, blk, N_EXP)
    tril = jnp.tril(jnp.ones((blk, blk), F32))
    within = jnp.einsum('ij,bjk->bik', tril, oh)
    blk_tot = within[:, -1, :]
    blk_off = jnp.cumsum(blk_tot, axis=0) - blk_tot
    csum = (within + blk_off[:, None, :]).reshape(n_assign, N_EXP)
    rank = jnp.sum(jnp.where(onehot, csum - 1.0, 0.0), axis=1).astype(jnp.int32)
    counts = csum[-1].astype(jnp.int32)
    padded = (counts + tm - 1) // tm * tm
    grp_start = jnp.cumsum(padded) - padded
    raw_start = jnp.cumsum(counts) - counts
    slot_of_assign = jnp.sum(jnp.where(onehot, grp_start[None, :], 0), axis=1) + rank

    r_max = n_assign + N_EXP * tm
    tile_start = jnp.arange(r_max // tm, dtype=jnp.int32) * tm
    tile_expert = jnp.sum((tile_start[:, None] >= (grp_start + padded)[None, :]).astype(jnp.int32), axis=1)
    tile_expert = jnp.minimum(tile_expert, N_EXP - 1).astype(jnp.int32)
    n_tiles = (jnp.sum(padded) // tm).astype(jnp.int32).reshape(1)

    order = jnp.argsort(flat_e, stable=True).astype(jnp.int32)
    e_slot = jnp.repeat(tile_expert, tm)
    j = jnp.arange(r_max, dtype=jnp.int32) - grp_start[e_slot]
    src = jnp.clip(raw_start[e_slot] + j, 0, n_assign - 1)
    tok_of_slot = jnp.where(j < counts[e_slot], order[src] // TOP_K, 0)
    return gate, slot_of_assign, tok_of_slot, tile_expert, n_tiles


ROW_SUB = D_MODEL // LANES


def _rows_to_tiles(o3_ref, x):
    for j in range(ROW_SUB):
        o3_ref[:, j, :] = x[:, j * LANES:(j + 1) * LANES].astype(o3_ref.dtype)


def _start_row_gather(idx_ref, src_ref, dst, sem, n_rows):
    def body(q, carry):
        for u in range(2):
            r = 2 * q + u
            pltpu.make_async_copy(src_ref.at[idx_ref[0, 0, r]], dst.at[r], sem).start(priority=u)
        return carry

    lax.fori_loop(0, n_rows // 2, body, 0, unroll=4)


def _wait_row_gather(src_ref, dst, sem, n_rows):
    pltpu.make_async_copy(src_ref.at[pl.ds(0, n_rows)], dst, sem).wait()


def _tiles_to_rows(tiles, rows_ref):
    for j in range(ROW_SUB):
        rows_ref[:, j * LANES:(j + 1) * LANES] = tiles[:, j, :]


def _dispatch_kernel(nt_ref, idx_ref, idx_next_ref, src_ref, o_ref, buf, rows, sem):
    i = pl.program_id(0)
    nt = nt_ref[0]
    tm = o_ref.shape[0]
    slot = i % 2

    @pl.when(i == 0)
    def _():
        _start_row_gather(idx_ref, src_ref, buf.at[0], sem.at[0], tm)

    @pl.when(i + 1 < nt)
    def _():
        _start_row_gather(idx_next_ref, src_ref, buf.at[1 - slot], sem.at[1 - slot], tm)

    @pl.when(i < nt)
    def _():
        _wait_row_gather(src_ref, buf.at[slot], sem.at[slot], tm)
        _tiles_to_rows(buf.at[slot], rows)
        o_ref[...] = rows[...].astype(o_ref.dtype)

    @pl.when(i >= nt)
    def _():
        o_ref[...] = jnp.zeros(o_ref.shape, o_ref.dtype)


def moe_dispatch(n_tiles, tok_of_slot, xn3, tm):
    r = tok_of_slot.shape[0]
    last = r // tm - 1
    idx = tok_of_slot.reshape(r // tm, 1, tm)
    return pl.pallas_call(
        _dispatch_kernel,
        grid_spec=pltpu.PrefetchScalarGridSpec(
            num_scalar_prefetch=1,
            grid=(r // tm,),
            in_specs=[pl.BlockSpec((1, 1, tm), lambda i, nt: (i, 0, 0), memory_space=pltpu.SMEM),
                      pl.BlockSpec((1, 1, tm), lambda i, nt: (jnp.minimum(i + 1, last), 0, 0),
                                   memory_space=pltpu.SMEM),
                      pl.BlockSpec(memory_space=pl.ANY)],
            out_specs=pl.BlockSpec((tm, D_MODEL), lambda i, nt: (i, 0)),
            scratch_shapes=[pltpu.VMEM((2, tm, ROW_SUB, LANES), F32), pltpu.VMEM((tm, D_MODEL), F32),
                            pltpu.SemaphoreType.DMA((2,))]),
        out_shape=jax.ShapeDtypeStruct((r, D_MODEL), BF16),
        compiler_params=_cparams(("arbitrary",)),
        name="moe_dispatch",
    )(n_tiles, idx, idx, xn3)


COMBINE_TM = 256


def _combine_kernel(idx_ref, idx_next_ref, ys_ref, x_ref, gate_ref, mod_ref, oc_ref, ol_ref, buf, rows, sem):
    i = pl.program_id(0)
    tm = oc_ref.shape[0]
    n_rows = TOP_K * tm
    slot = i % 2

    @pl.when(i == 0)
    def _():
        _start_row_gather(idx_ref, ys_ref, buf.at[0], sem.at[0], n_rows)

    @pl.when(i + 1 < pl.num_programs(0))
    def _():
        _start_row_gather(idx_next_ref, ys_ref, buf.at[1 - slot], sem.at[1 - slot], n_rows)

    _wait_row_gather(ys_ref, buf.at[slot], sem.at[slot], n_rows)
    _tiles_to_rows(buf.at[slot], rows)
    f = gate_ref[:, 0:1] * rows[0:tm, :] + gate_ref[:, 1:2] * rows[tm:2 * tm, :]
    res = x_ref[...] + mod_ref[0, 5:6, :] * f

    @pl.when(i < R_CTX // COMBINE_TM)
    def _():
        oc_ref[...] = res

    @pl.when(i >= R_CTX // COMBINE_TM)
    def _():
        ol_ref[...] = res


def moe_combine(slot_of_assign, ys3, x, gate, mod):
    t, d = x.shape
    tm = COMBINE_TM
    idx = slot_of_assign.reshape(t // tm, tm, TOP_K).transpose(0, 2, 1).reshape(t // tm, 1, TOP_K * tm)
    last = t // tm - 1
    n_ctx = R_CTX // tm
    return pl.pallas_call(
        _combine_kernel,
        grid=(t // tm,),
        in_specs=[pl.BlockSpec((1, 1, TOP_K * tm), lambda i: (i, 0, 0), memory_space=pltpu.SMEM),
                  pl.BlockSpec((1, 1, TOP_K * tm), lambda i: (jnp.minimum(i + 1, last), 0, 0),
                               memory_space=pltpu.SMEM),
                  pl.BlockSpec(memory_space=pl.ANY),
                  pl.BlockSpec((tm, d), lambda i: (i, 0)),
                  pl.BlockSpec((tm, TOP_K), lambda i: (i, 0)),
                  pl.BlockSpec((1, 8, d), lambda i: (_mod_row_of_tile(i, tm), 0, 0))],
        out_specs=(pl.BlockSpec((tm, d), lambda i: (jnp.minimum(i, n_ctx - 1), 0)),
                   pl.BlockSpec((tm, d), lambda i: (jnp.maximum(i - n_ctx, 0), 0))),
        out_shape=(jax.ShapeDtypeStruct((R_CTX, d), F32), jax.ShapeDtypeStruct((t - R_CTX, d), F32)),
        scratch_shapes=[pltpu.VMEM((2, TOP_K * tm, ROW_SUB, LANES), F32), pltpu.VMEM((TOP_K * tm, d), F32),
                        pltpu.SemaphoreType.DMA((2,))],
        compiler_params=_cparams(("arbitrary",)),
        name="moe_combine",
    )(idx, idx, ys3, x, gate, mod)


def moe_swiglu_residual(xn3, xn, x, mod, w_router, wg, wu, wd):
    tm = MOE_TM
    logits = jnp.dot(xn.astype(F32), w_router, precision=lax.Precision.HIGHEST)
    gate, slot_of_assign, tok_of_slot, tile_expert, n_tiles = _moe_routing(logits, tm)
    xs = moe_dispatch(n_tiles * (tm // DISPATCH_TM), tok_of_slot, xn3, DISPATCH_TM)
    h = ffn_up(tile_expert, n_tiles, xs, wg, wu, tm=tm, tf=wg.shape[2] // 4, weight_buffers=2)
    ys3 = ffn_down(tile_expert, n_tiles, h, wd, tm=tm)
    return moe_combine(slot_of_assign, ys3, x, gate, mod)


def _layer(x, cond, lp, l, ctx, tabs_a, tabs_d):
    mod = modulation(cond, lp['w_mod'], lp['b_mod']).reshape(cond.shape[0], 6, D_MODEL)
    mod = jnp.pad(mod, ((0, 0), (0, 2), (0, 0)))
    p = in_proj(x, mod, lp['g_norm_mix'], pack_w_in(lp['w_in']))

    mw = pack_mla_weights(lp)
    q_a, k_a, v_a, ckv, kr = mla_prep(p, mw, tabs_a)
    kr_c = jnp.pad(ctx['mla_krope'].reshape(-1, ROPE_A), ((0, 0), (0, LANES - ROPE_A)))
    kc_a, vc_a = mla_prep_cache(ctx['mla_ckv'].reshape(-1, KV_RANK), kr_c, mw)
    ya_c = mla_attention(q_a, k_a, v_a, None, None, row0=0, n_seq=N_CTX_SEQ, t_seq=T_CTX, tq=T_CTX)
    ya_l = mla_attention(q_a, k_a, v_a, kc_a, vc_a, row0=R_CTX, n_seq=N_LAT_SEQ, t_seq=T_LAT, tq=ATTN_TQ)

    rw = pack_rglru_weights(lp)
    yb_c, st_rg = rglru(p, rw, jnp.zeros((N_CTX_SEQ, 2, W_B), F32), row0=0, n_seq=N_CTX_SEQ, t_seq=T_CTX)
    yb_l, _ = rglru(p, rw, ctx['rglru'], row0=R_CTX, n_seq=N_LAT_SEQ, t_seq=T_LAT)

    bias_sm = jnp.zeros((LANES,), F32).at[SM_MI:SM_MI + 2 * H_C].set(lp['b_ml_i'].reshape(-1))
    bias_sm = bias_sm.at[SM_MF:SM_MF + 2 * H_C].set(lp['b_ml_f'].reshape(-1)).reshape(1, LANES)
    c0_ctx = jnp.zeros((N_CTX_SEQ, 2, H_C, DH_C, DH_C), F32)
    n0_ctx = jnp.zeros((N_CTX_SEQ, 2, H_C, DH_C, 1), F32)
    m0_ctx = jnp.zeros((N_CTX_SEQ, 1, 2 * H_C), F32)
    m0_lat = ctx['mlstm_m'].reshape(N_LAT_SEQ, 1, 2 * H_C)
    yc_c, c_fin, n_fin, m_fin = mlstm(p, bias_sm, lp['g_ml_out'], c0_ctx, n0_ctx, m0_ctx,
                                      row0=0, n_seq=N_CTX_SEQ, t_seq=T_CTX)
    yc_l, _, _, _ = mlstm(p, bias_sm, lp['g_ml_out'], ctx['mlstm_C'], ctx['mlstm_n'][..., None], m0_lat,
                          row0=R_CTX, n_seq=N_LAT_SEQ, t_seq=T_LAT)

    lambda_init = 0.8 - 0.6 * math.exp(-0.3 * l)
    qd, kd_own, kd_plain = diff_prep(p, lp, tabs_d)
    dkc = ctx['diff_k'].reshape(-1, W_BRANCH)
    dvc = ctx['diff_v'].reshape(-1, W_BRANCH)
    yd_c = diff_attention(lp['diff_lambda'], qd, kd_own, p, None, None, lp['g_diff_sub'],
                          row0=0, n_seq=N_CTX_SEQ, t_seq=T_CTX, tq=T_CTX, lambda_init=lambda_init)
    yd_l = diff_attention(lp['diff_lambda'], qd, kd_own, p, dkc, dvc, lp['g_diff_sub'],
                          row0=R_CTX, n_seq=N_LAT_SEQ, t_seq=T_LAT, tq=ATTN_TQ, lambda_init=lambda_init)

    merged = merge((ya_c, yb_c, yc_c, yd_c), (ya_l, yb_l, yc_l, yd_l), p, x, mod, lp['w_br'].astype(BF16),
                   lp['w_out'].astype(BF16), lp['g_norm_ffn'], rows_as_tiles='moe' in lp)
    if 'ffn' in lp:
        x, xn = merged
        x = dense_swiglu_residual(xn, x, mod, *lp['ffn'])
    else:
        x, xn, xn3 = merged
        x = moe_swiglu_residual(xn3, xn, x, mod, *lp['moe'])

    dv0 = _seg_block('dv') * W_BRANCH
    ctx_out = (ckv[:R_CTX].reshape(N_CTX_SEQ, T_CTX, KV_RANK),
               kr[:R_CTX, :ROPE_A].reshape(N_CTX_SEQ, T_CTX, ROPE_A),
               kd_plain[:R_CTX].reshape(N_CTX_SEQ, T_CTX, 2, H_D, DH_D),
               p[:R_CTX, dv0:dv0 + W_BRANCH].astype(F32).reshape(N_CTX_SEQ, T_CTX, H_D, 2 * DH_D),
               st_rg,
               c_fin,
               n_fin.reshape(N_CTX_SEQ, 2, H_C, DH_C),
               m_fin.reshape(N_CTX_SEQ, 2, H_C))
    return x, ctx_out


def kernel(x_prompt, x_sample, cache_mla_ckv, cache_mla_krope, cache_diff_k, cache_diff_v,
           state_rglru, state_mlstm_C, state_mlstm_n, state_mlstm_m, c, c_ctx,
           w_mod, b_mod, g_norm_mix, g_norm_ffn, w_in, g_mla_qlat, w_mla_uq, g_mla_kvlat, w_mla_ukv,
           g_mla_qn, g_mla_kn, w_conv_rg, b_conv_rg, w_rg_a, b_rg_a, w_rg_x, b_rg_x, rg_lambda,
           b_ml_i, b_ml_f, g_ml_out, g_diff_qn, g_diff_kn, diff_lambda, g_diff_sub, w_br, w_out,
           w_ffn_gate, w_ffn_up, w_ffn_down, w_router, w_moe_gate, w_moe_up, w_moe_down):
    assert x_prompt.shape == (N_CTX_SEQ, T_CTX, D_MODEL) and x_sample.shape == (N_LAT_SEQ, T_LAT, D_MODEL)
    tabs_a = _rope_tables(ROPE_A, (NOPE_A,), MLA_TM)
    tabs_d = _rope_tables(DH_D, (0, DH_D), DIFF_TM)
    cond = jnp.concatenate([c_ctx.reshape(1, D_MODEL), c, jnp.zeros((16 - 1 - N_LAT_SEQ, D_MODEL), F32)], axis=0)
    x = jnp.concatenate([x_prompt.reshape(R_CTX, D_MODEL), x_sample.reshape(R_LAT, D_MODEL)], axis=0)
    new = []
    for l in range(DEPTH):
        lp = dict(w_mod=w_mod[l], b_mod=b_mod[l], g_norm_mix=g_norm_mix[l], g_norm_ffn=g_norm_ffn[l], w_in=w_in[l],
                  g_mla_qlat=g_mla_qlat[l], w_mla_uq=w_mla_uq[l], g_mla_kvlat=g_mla_kvlat[l], w_mla_ukv=w_mla_ukv[l],
                  g_mla_qn=g_mla_qn[l], g_mla_kn=g_mla_kn[l], w_conv_rg=w_conv_rg[l], b_conv_rg=b_conv_rg[l],
                  w_rg_a=w_rg_a[l], b_rg_a=b_rg_a[l], w_rg_x=w_rg_x[l], b_rg_x=b_rg_x[l], rg_lambda=rg_lambda[l],
                  b_ml_i=b_ml_i[l], b_ml_f=b_ml_f[l], g_ml_out=g_ml_out[l], g_diff_qn=g_diff_qn[l],
                  g_diff_kn=g_diff_kn[l], diff_lambda=diff_lambda[l], g_diff_sub=g_diff_sub[l],
                  w_br=w_br[l], w_out=w_out[l])
        if l % 2 == 0:
            lp['ffn'] = (w_ffn_gate[l // 2], w_ffn_up[l // 2], w_ffn_down[l // 2])
        else:
            lp['moe'] = (w_router[l // 2], w_moe_gate[l // 2], w_moe_up[l // 2], w_moe_down[l // 2])
        ctx_l = dict(mla_ckv=cache_mla_ckv[:, l], mla_krope=cache_mla_krope[:, l], diff_k=cache_diff_k[:, l],
                     diff_v=cache_diff_v[:, l], rglru=state_rglru[:, l], mlstm_C=state_mlstm_C[:, l],
                     mlstm_n=state_mlstm_n[:, l], mlstm_m=state_mlstm_m[:, l])
        if isinstance(x, tuple):
            x = jnp.concatenate(x, axis=0)
        x, st = _layer(x, cond, lp, l, ctx_l, tabs_a, tabs_d)
        new.append(st)
    outs = tuple(jnp.stack([s[i] for s in new], axis=1) for i in range(8))
    x_ctx, x_lat = x if isinstance(x, tuple) else (x[:R_CTX], x[R_CTX:])
    return (x_ctx.reshape(N_CTX_SEQ, T_CTX, D_MODEL), x_lat.reshape(N_LAT_SEQ, T_LAT, D_MODEL)) + outs
```

```python
import functools
import math

import jax
import jax.numpy as jnp
import numpy as np
from jax import lax
from jax.experimental import pallas as pl
from jax.experimental.pallas import tpu as pltpu

D_MODEL = 1024
DEPTH = 2
GRID_W = 64
ROPE_BASE = 10000.0
EPS = 1e-6
N_BRANCH = 4
W_BRANCH = D_MODEL // 2

H_A = 8
NOPE_A = 64
ROPE_A = 32
V_A = W_BRANCH // H_A
Q_RANK = D_MODEL // 4
KV_RANK = D_MODEL // 8
MLA_SCALE = (NOPE_A + ROPE_A) ** -0.5

W_B = W_BRANCH
NB_B = 8
BW_B = W_B // NB_B
CONV_W = 4
RG_C = 8.0

H_C = 4
DH_C = W_BRANCH // H_C
ML_CHUNK = 128

H_D = 4
DH_D = W_BRANCH // (2 * H_D)
DIFF_SCALE = DH_D ** -0.5

N_EXP = 8
TOP_K = 2

V7X_VMEM_LIMIT_BYTES = 56 * 1024 * 1024
LANES = 128
SUBLANES = 8

BF16 = jnp.bfloat16
F32 = jnp.float32

P_GATE = 0
P_SEG = N_BRANCH * D_MODEL
SEG_NAMES = ('rg_x', 'rg_gate', 'mq', 'mk', 'mv', 'mo', 'dq', 'dk', 'dv')
P_QLAT = P_SEG + 9 * W_BRANCH
P_KVLAT = P_QLAT + Q_RANK
P_SMALL = P_KVLAT + KV_RANK
P_WIDTH = P_SMALL + LANES
SM_MI = ROPE_A
SM_MF = ROPE_A + 2 * H_C


def _seg_block(name):
    return (P_SEG + SEG_NAMES.index(name) * W_BRANCH) // W_BRANCH


N_CTX_SEQ, T_CTX = 16, 256
N_LAT_SEQ, T_LAT = 8, 1024
PAST_LEN = 512
R_CTX = N_CTX_SEQ * T_CTX
R_LAT = N_LAT_SEQ * T_LAT
R_ALL = R_CTX + R_LAT


def _cparams(sem):
    return pltpu.CompilerParams(dimension_semantics=sem, vmem_limit_bytes=V7X_VMEM_LIMIT_BYTES)


def _mod_row_of_tile(i, tm):
    n_ctx_tiles = R_CTX // tm
    per_seq = T_LAT // tm
    return jnp.where(i < n_ctx_tiles, 0, 1 + (i - n_ctx_tiles) // per_seq)


def _pos_block_of_tile(i, tm):
    n_ctx_tiles = R_CTX // tm
    per_seq = T_LAT // tm
    return jnp.where(i < n_ctx_tiles, per_seq, (i - n_ctx_tiles) % per_seq)


def _mod_kernel(c_ref, w_ref, b_ref, o_ref):
    c = c_ref[...]
    s = (c * jax.nn.sigmoid(c)).astype(BF16)
    o_ref[...] = jnp.dot(s, w_ref[...].astype(BF16), preferred_element_type=F32) + b_ref[...]


def modulation(cond, w_mod, b_mod):
    m, d = cond.shape
    n = w_mod.shape[1]
    tn = 1536
    return pl.pallas_call(
        _mod_kernel,
        grid=(n // tn,),
        in_specs=[pl.BlockSpec((m, d), lambda j: (0, 0)),
                  pl.BlockSpec((d, tn), lambda j: (0, j)),
                  pl.BlockSpec((1, tn), lambda j: (0, j))],
        out_specs=pl.BlockSpec((m, tn), lambda j: (0, j)),
        out_shape=jax.ShapeDtypeStruct((m, n), F32),
        compiler_params=_cparams(("arbitrary",)),
        name="modulation",
    )(cond, w_mod, b_mod.reshape(1, n))


def _in_proj_kernel(x_ref, mod_ref, g_ref, w_ref, o_ref, xn_ref):
    @pl.when(pl.program_id(1) == 0)
    def _():
        x = x_ref[...]
        r = lax.rsqrt(jnp.mean(x * x, axis=-1, keepdims=True) + EPS)
        sh = mod_ref[0, 0:1, :]
        sc = mod_ref[0, 1:2, :]
        xn_ref[...] = ((x * r * g_ref[...]) * (1.0 + sc) + sh).astype(BF16)

    o_ref[...] = jnp.dot(xn_ref[...], w_ref[...], preferred_element_type=F32).astype(o_ref.dtype)


def in_proj(x, mod, g, w_p):
    tm, tn = 1024, P_WIDTH // 4
    m, d = x.shape
    n = w_p.shape[1]
    return pl.pallas_call(
        _in_proj_kernel,
        grid=(m // tm, n // tn),
        in_specs=[pl.BlockSpec((tm, d), lambda i, j: (i, 0)),
                  pl.BlockSpec((1, 8, d), lambda i, j: (_mod_row_of_tile(i, tm), 0, 0)),
                  pl.BlockSpec((1, d), lambda i, j: (0, 0)),
                  pl.BlockSpec((d, tn), lambda i, j: (0, j))],
        out_specs=pl.BlockSpec((tm, tn), lambda i, j: (i, j)),
        out_shape=jax.ShapeDtypeStruct((m, n), BF16),
        scratch_shapes=[pltpu.VMEM((tm, d), BF16)],
        compiler_params=_cparams(("arbitrary", "arbitrary")),
        name="in_proj",
    )(x, mod, g.reshape(1, d), w_p)


def pack_w_in(w_in):
    o_rg = Q_RANK + KV_RANK + ROPE_A
    o_mi = o_rg + 6 * W_BRANCH
    o_dq = o_mi + 4 * H_C
    o_gate = o_dq + 3 * W_BRANCH
    pad = jnp.zeros((w_in.shape[0], LANES - ROPE_A - 4 * H_C), w_in.dtype)
    parts = [w_in[:, o_gate:], w_in[:, o_rg:o_mi], w_in[:, o_dq:o_gate], w_in[:, :Q_RANK + KV_RANK],
             w_in[:, Q_RANK + KV_RANK:o_rg], w_in[:, o_mi:o_dq], pad]
    return jnp.concatenate(parts, axis=1).astype(BF16)


def _rope_partner(rot_dim, lane_starts):
    nf = rot_dim // 4
    partner = np.arange(LANES)
    for s0 in lane_starts:
        for a in range(2):
            lo = s0 + a * 2 * nf
            partner[lo:lo + nf] = np.arange(lo + nf, lo + 2 * nf)
            partner[lo + nf:lo + 2 * nf] = np.arange(lo, lo + nf)
    return partner


def _rope_tables(rot_dim, lane_starts, tm):
    rows = T_LAT // GRID_W
    r, c = np.meshgrid(np.arange(rows, dtype=np.float32), np.arange(GRID_W, dtype=np.float32), indexing='ij')
    nf = rot_dim // 4
    inv = (np.float32(ROPE_BASE) ** (-np.arange(nf, dtype=np.float32) / np.float32(nf))).astype(np.float32)
    ang = np.stack([r.reshape(-1)[:, None] * inv, c.reshape(-1)[:, None] * inv], axis=1).astype(np.float32)
    cos, sin = np.cos(ang).astype(np.float32), np.sin(ang).astype(np.float32)
    tc = np.ones((T_LAT + tm, LANES), np.float32)
    ta = np.zeros((T_LAT + tm, LANES), np.float32)
    tb = np.zeros((T_LAT + tm, LANES), np.float32)
    for s0 in lane_starts:
        for a in range(2):
            lo = s0 + a * 2 * nf
            tc[:T_LAT, lo:lo + nf] = cos[:, a]
            tc[:T_LAT, lo + nf:lo + 2 * nf] = cos[:, a]
            ta[:T_LAT, lo:lo + nf] = -sin[:, a]
            tb[:T_LAT, lo + nf:lo + 2 * nf] = sin[:, a]
    return jnp.asarray(tc), jnp.asarray(ta + tb)


MLA_TM = 512
ATTN_TQ = 512
MLA_HEADS_PER_STEP = 8
QK_A = NOPE_A + ROPE_A


def _mla_prep_kernel(*refs, has_q, norm_ckv):
    if has_q:
        (qlat_ref, gq_ref, wuq_ref, gqn_ref, kv_ref, sm_ref, gkv_ref, wkc_ref, wv_ref, gkn_ref,
         c_ref, s_ref, q_o, k_o, v_o, ckv_o, kr_o) = refs
        c, sn = c_ref[...], s_ref[...]
    else:
        (kv_ref, sm_ref, gkv_ref, wkc_ref, wv_ref, gkn_ref, k_o, v_o) = refs

    def heads(z, g_ref, o_ref, scale):
        for h in range(H_A):
            s = z[:, h * LANES:(h + 1) * LANES]
            r = lax.rsqrt(jnp.sum(s * s, axis=-1, keepdims=True) * (1.0 / QK_A) + EPS)
            y = s * r * g_ref[0:1, :]
            if has_q:
                sw = z[:, (H_A + h) * LANES:(H_A + h + 1) * LANES]
                y = y * c + (sw * r * g_ref[1:2, :]) * sn
            if scale != 1.0:
                y = y * scale
            o_ref[:, h * LANES:(h + 1) * LANES] = y.astype(o_ref.dtype)

    if has_q:
        ql = qlat_ref[...].astype(F32)
        qn = ql * lax.rsqrt(jnp.mean(ql * ql, axis=-1, keepdims=True) + EPS) * gq_ref[...]
        q = jnp.dot(qn.astype(BF16), wuq_ref[...], preferred_element_type=F32)
        heads(q, gqn_ref, q_o, MLA_SCALE)

    kv = kv_ref[...].astype(F32)
    if norm_ckv:
        ckv = kv * lax.rsqrt(jnp.mean(kv * kv, axis=-1, keepdims=True) + EPS) * gkv_ref[...]
    else:
        ckv = kv
    sm = sm_ref[...]
    ckv_b = ckv.astype(BF16)
    kin = jnp.concatenate([ckv_b, sm.astype(BF16)], axis=1)
    wkc = wkc_ref[...] if has_q else wkc_ref[:, :H_A * LANES]
    k = jnp.dot(kin, wkc, preferred_element_type=F32)
    heads(k, gkn_ref, k_o, 1.0)
    v_o[...] = jnp.dot(ckv_b, wv_ref[...], preferred_element_type=F32).astype(v_o.dtype)
    if has_q:
        ckv_o[...] = ckv
        kr_o[...] = sm.astype(F32)


def pack_mla_weights(lp):
    wuq = lp['w_mla_uq'].reshape(Q_RANK, H_A, QK_A)
    wuq_p = jnp.pad(wuq, ((0, 0), (0, 0), (0, LANES - QK_A))).reshape(Q_RANK, H_A * LANES).astype(BF16)
    wukv = lp['w_mla_ukv'].reshape(KV_RANK, H_A, NOPE_A + V_A)
    wk = jnp.pad(wukv[:, :, :NOPE_A], ((0, 0), (0, 0), (0, LANES - NOPE_A))).reshape(KV_RANK, H_A * LANES)
    place = np.zeros((LANES, H_A, LANES), np.float32)
    for h in range(H_A):
        place[np.arange(ROPE_A), h, NOPE_A + np.arange(ROPE_A)] = 1.0
    wkc = jnp.concatenate([wk, jnp.asarray(place.reshape(LANES, H_A * LANES))], axis=0).astype(BF16)
    wv = wukv[:, :, NOPE_A:]
    wv_even = jnp.pad(wv, ((0, 0), (0, 0), (0, LANES - V_A)))
    wv_odd = jnp.pad(wv, ((0, 0), (0, 0), (LANES - V_A, 0)))
    odd = (np.arange(H_A) % 2 == 1)[None, :, None]
    wv_p = jnp.where(odd, wv_odd, wv_even).reshape(KV_RANK, H_A * LANES).astype(BF16)
    partner = _rope_partner(ROPE_A, (NOPE_A,))
    cols = (np.arange(H_A)[:, None] * LANES + partner[None, :]).reshape(-1)
    with_partner = lambda w: jnp.concatenate([w, w[:, cols]], axis=1)
    pad_g = lambda g: jnp.pad(g, (0, LANES - QK_A))
    gain2 = lambda g: jnp.stack([pad_g(g), pad_g(g)[partner]])
    return dict(wuq=with_partner(wuq_p), wkc=with_partner(wkc), wv=wv_p,
                gqn=gain2(lp['g_mla_qn']), gkn=gain2(lp['g_mla_kn']),
                gq=lp['g_mla_qlat'].reshape(1, Q_RANK), gkv=lp['g_mla_kvlat'].reshape(1, KV_RANK))


def mla_prep(p, mw, tabs):
    tm = MLA_TM
    n = R_ALL // tm
    full = lambda shape: pl.BlockSpec(shape, lambda i: (0, 0))
    tab = pl.BlockSpec((tm, LANES), lambda i: (_pos_block_of_tile(i, tm), 0))
    wide = H_A * LANES
    out_shape = (jax.ShapeDtypeStruct((R_ALL, wide), BF16), jax.ShapeDtypeStruct((R_ALL, wide), BF16),
                 jax.ShapeDtypeStruct((R_ALL, wide), BF16), jax.ShapeDtypeStruct((R_ALL, KV_RANK), F32),
                 jax.ShapeDtypeStruct((R_ALL, LANES), F32))
    row = lambda w: pl.BlockSpec((tm, w), lambda i: (i, 0))
    return pl.pallas_call(
        functools.partial(_mla_prep_kernel, has_q=True, norm_ckv=True),
        grid=(n,),
        in_specs=[pl.BlockSpec((tm, Q_RANK), lambda i: (i, P_QLAT // Q_RANK)), full((1, Q_RANK)),
                  full((Q_RANK, 2 * wide)), full((2, LANES)),
                  pl.BlockSpec((tm, KV_RANK), lambda i: (i, P_KVLAT // KV_RANK)),
                  pl.BlockSpec((tm, LANES), lambda i: (i, P_SMALL // LANES)), full((1, KV_RANK)),
                  full((2 * LANES, 2 * wide)), full((KV_RANK, wide)), full((2, LANES)), tab, tab],
        out_specs=(row(wide), row(wide), row(wide), row(KV_RANK), row(LANES)),
        out_shape=out_shape,
        compiler_params=_cparams(("arbitrary",)),
        name="mla_prep",
    )(p, mw['gq'], mw['wuq'], mw['gqn'], p, p, mw['gkv'], mw['wkc'], mw['wv'], mw['gkn'], *tabs)


def mla_prep_cache(ckv_c, kr_c, mw):
    tm = MLA_TM
    r = ckv_c.shape[0]
    full = lambda shape: pl.BlockSpec(shape, lambda i: (0, 0))
    wide = H_A * LANES
    row = lambda w: pl.BlockSpec((tm, w), lambda i: (i, 0))
    return pl.pallas_call(
        functools.partial(_mla_prep_kernel, has_q=False, norm_ckv=False),
        grid=(r // tm,),
        in_specs=[row(KV_RANK), row(LANES), full((1, KV_RANK)), full((2 * LANES, 2 * wide)), full((KV_RANK, wide)),
                  full((2, LANES))],
        out_specs=(row(wide), row(wide)),
        out_shape=(jax.ShapeDtypeStruct((r, wide), BF16), jax.ShapeDtypeStruct((r, wide), BF16)),
        compiler_params=_cparams(("arbitrary",)),
        name="mla_prep_cache",
    )(ckv_c, kr_c, mw['gkv'], mw['wkc'], mw['wv'], mw['gkn'])


_NT = (((1,), (1,)), ((), ()))


def _mla_attn_kernel(*refs, has_cache):
    if has_cache:
        q_ref, ko_ref, vo_ref, kc_ref, vc_ref, o_ref = refs
    else:
        q_ref, ko_ref, vo_ref, o_ref = refs
    n_heads = q_ref.shape[1] // LANES
    sls = [slice(h * LANES, (h + 1) * LANES) for h in range(n_heads)]
    def scores(sl):
        s_o = lax.dot_general(q_ref[:, sl], ko_ref[:, sl], _NT, preferred_element_type=F32)
        s_c = lax.dot_general(q_ref[:, sl], kc_ref[:, sl], _NT, preferred_element_type=F32) if has_cache else None
        return s_o, s_c

    outs = []
    nxt = scores(sls[0])
    for h, sl in enumerate(sls):
        s_o, s_c = nxt
        if h + 1 < n_heads:
            nxt = scores(sls[h + 1])
        m = jnp.max(s_o, axis=-1, keepdims=True)
        if has_cache:
            m = jnp.maximum(m, jnp.max(s_c, axis=-1, keepdims=True))
        e_o = jnp.exp(s_o - m)
        l = jnp.sum(e_o, axis=-1, keepdims=True)
        pv = jnp.dot(e_o.astype(BF16), vo_ref[:, sl], preferred_element_type=F32)
        if has_cache:
            e_c = jnp.exp(s_c - m)
            l = l + jnp.sum(e_c, axis=-1, keepdims=True)
            pv = pv + jnp.dot(e_c.astype(BF16), vc_ref[:, sl], preferred_element_type=F32)
        outs.append(pv * (1.0 / l))
    for p in range(n_heads // 2):
        o_ref[:, p * LANES:(p + 1) * LANES] = (outs[2 * p] + outs[2 * p + 1]).astype(o_ref.dtype)


def mla_attention(q, k, v, kc, vc, *, row0, n_seq, t_seq, tq):
    has_cache = kc is not None
    wh = MLA_HEADS_PER_STEP * LANES
    n_grp = H_A // MLA_HEADS_PER_STEP
    nq = t_seq // tq
    qb0, kb0 = row0 // tq, row0 // t_seq
    in_specs = [pl.BlockSpec((tq, wh), lambda s, p, i: (qb0 + s * nq + i, p)),
                pl.BlockSpec((t_seq, wh), lambda s, p, i: (kb0 + s, p)),
                pl.BlockSpec((t_seq, wh), lambda s, p, i: (kb0 + s, p))]
    args = [q, k, v]
    if has_cache:
        in_specs += [pl.BlockSpec((PAST_LEN, wh), lambda s, p, i: (s, p)),
                     pl.BlockSpec((PAST_LEN, wh), lambda s, p, i: (s, p))]
        args += [kc, vc]
    return pl.pallas_call(
        functools.partial(_mla_attn_kernel, has_cache=has_cache),
        grid=(n_seq, n_grp, nq),
        in_specs=in_specs,
        out_specs=pl.BlockSpec((tq, wh // 2), lambda s, p, i: (s * nq + i, p)),
        out_shape=jax.ShapeDtypeStruct((n_seq * t_seq, W_BRANCH), BF16),
        compiler_params=_cparams(("arbitrary", "arbitrary", "arbitrary")),
        name="mla_attention",
    )(*args)


DIFF_TM = 512


def _diff_prep_kernel(dq_ref, dk_ref, gq_ref, gk_ref, perm_ref, c_ref, s_ref, q_o, ko_o, kp_o):
    c, sn = c_ref[...], s_ref[...]
    lane = lax.broadcasted_iota(jnp.int32, (1, LANES), 1)
    lo = lane < DH_D

    def inv_rms(x):
        x2 = x * x
        s_lo = jnp.sum(jnp.where(lo, x2, 0.0), axis=-1, keepdims=True)
        s_hi = jnp.sum(jnp.where(lo, 0.0, x2), axis=-1, keepdims=True)
        return lax.rsqrt(jnp.where(lo, s_lo, s_hi) * (1.0 / DH_D) + EPS)

    def rotated(n, x_sw, r, g_ref):
        return n * c + (x_sw * r * g_ref[1:2, :]) * sn

    for j in range(W_BRANCH // LANES):
        sl = slice(j * LANES, (j + 1) * LANES)
        xq, xk = dq_ref[:, sl], dk_ref[:, sl]
        q_sw = jnp.dot(xq, perm_ref[...], preferred_element_type=F32)
        k_sw = jnp.dot(xk, perm_ref[...], preferred_element_type=F32)
        xq, xk = xq.astype(F32), xk.astype(F32)
        rq, rk = inv_rms(xq), inv_rms(xk)
        qn = xq * rq * gq_ref[0:1, :]
        kn = xk * rk * gk_ref[0:1, :]
        q_o[:, sl] = (rotated(qn, q_sw, rq, gq_ref) * DIFF_SCALE).astype(q_o.dtype)
        kp_o[:, sl] = kn
        ko_o[:, sl] = rotated(kn, k_sw, rk, gk_ref).astype(ko_o.dtype)


def diff_prep(p, lp, tabs):
    tm = DIFF_TM
    partner = _rope_partner(DH_D, (0, DH_D))
    perm = np.zeros((LANES, LANES), np.float32)
    perm[partner, np.arange(LANES)] = 1.0
    gain2 = lambda g: jnp.stack([jnp.concatenate([g, g]), jnp.concatenate([g, g])[partner]])
    full = lambda shape: pl.BlockSpec(shape, lambda i: (0, 0))
    tab = pl.BlockSpec((tm, LANES), lambda i: (_pos_block_of_tile(i, tm), 0))
    row = pl.BlockSpec((tm, W_BRANCH), lambda i: (i, 0))
    dq_b, dk_b = _seg_block('dq'), _seg_block('dk')
    return pl.pallas_call(
        _diff_prep_kernel,
        grid=(R_ALL // tm,),
        in_specs=[pl.BlockSpec((tm, W_BRANCH), lambda i: (i, dq_b)), pl.BlockSpec((tm, W_BRANCH), lambda i: (i, dk_b)),
                  full((2, LANES)), full((2, LANES)), full((LANES, LANES)), tab, tab],
        out_specs=(row, row, row),
        out_shape=(jax.ShapeDtypeStruct((R_ALL, W_BRANCH), BF16), jax.ShapeDtypeStruct((R_ALL, W_BRANCH), BF16),
                   jax.ShapeDtypeStruct((R_ALL, W_BRANCH), F32)),
        compiler_params=_cparams(("arbitrary",)),
        name="diff_prep",
    )(p, p, gain2(lp['g_diff_qn']), gain2(lp['g_diff_kn']), jnp.asarray(perm, BF16), *tabs)


def _diff_attn_kernel(*refs, has_cache, lambda_init):
    if has_cache:
        dl_ref, q_ref, k_ref, v_ref, kc_ref, vc_ref, g_ref, o_ref = refs
    else:
        dl_ref, q_ref, k_ref, v_ref, g_ref, o_ref = refs
    lane = lax.broadcasted_iota(jnp.int32, (1, LANES), 1)
    dl = dl_ref[...]
    lam = (jnp.exp(jnp.sum(dl[0:1] * dl[1:2], axis=-1, keepdims=True))
           - jnp.exp(jnp.sum(dl[2:3] * dl[3:4], axis=-1, keepdims=True)) + lambda_init)
    units = [(h, w) for h in range(H_D) for w in range(2)]

    def scores(unit):
        h, w = unit
        sl = slice((2 * w + h // 2) * LANES, (2 * w + h // 2 + 1) * LANES)
        q = jnp.where((lane // DH_D) == (h % 2), q_ref[:, sl], jnp.zeros((), q_ref.dtype))
        s_o = lax.dot_general(q, k_ref[:, sl], _NT, preferred_element_type=F32)
        s_c = (lax.dot_general(q, kc_ref[:, sl].astype(BF16), _NT, preferred_element_type=F32)
               if has_cache else None)
        return s_o, s_c

    outs = []
    nxt = scores(units[0])
    for n, (h, w) in enumerate(units):
        s_o, s_c = nxt
        if n + 1 < len(units):
            nxt = scores(units[n + 1])
        vs = slice(h * LANES, (h + 1) * LANES)
        m = jnp.max(s_o, axis=-1, keepdims=True)
        if has_cache:
            m = jnp.maximum(m, jnp.max(s_c, axis=-1, keepdims=True))
        e_o = jnp.exp(s_o - m)
        l = jnp.sum(e_o, axis=-1, keepdims=True)
        pv = jnp.dot(e_o.astype(BF16), v_ref[:, vs], preferred_element_type=F32)
        if has_cache:
            e_c = jnp.exp(s_c - m)
            l = l + jnp.sum(e_c, axis=-1, keepdims=True)
            pv = pv + jnp.dot(e_c.astype(BF16), vc_ref[:, vs].astype(BF16), preferred_element_type=F32)
        outs.append(pv * (1.0 / l))
    for h in range(H_D):
        y = outs[2 * h] - lam * outs[2 * h + 1]
        r = lax.rsqrt(jnp.mean(y * y, axis=-1, keepdims=True) + EPS)
        o_ref[:, h * LANES:(h + 1) * LANES] = ((y * r * g_ref[...]) * (1.0 - lambda_init)).astype(o_ref.dtype)


def diff_attention(dl, qd, kd, p, kc, vc, g_sub, *, row0, n_seq, t_seq, tq, lambda_init):
    has_cache = kc is not None
    nq = t_seq // tq
    qb0, kb0 = row0 // tq, row0 // t_seq
    in_specs = [pl.BlockSpec((4, DH_D), lambda s, i: (0, 0)),
                pl.BlockSpec((tq, W_BRANCH), lambda s, i: (qb0 + s * nq + i, 0)),
                pl.BlockSpec((t_seq, W_BRANCH), lambda s, i: (kb0 + s, 0)),
                pl.BlockSpec((t_seq, W_BRANCH), lambda s, i: (kb0 + s, _seg_block('dv')))]
    args = [dl, qd, kd, p]
    if has_cache:
        in_specs += [pl.BlockSpec((PAST_LEN, W_BRANCH), lambda s, i: (s, 0)),
                     pl.BlockSpec((PAST_LEN, W_BRANCH), lambda s, i: (s, 0))]
        args += [kc, vc]
    in_specs.append(pl.BlockSpec((1, LANES), lambda s, i: (0, 0)))
    args.append(g_sub.reshape(1, LANES))
    return pl.pallas_call(
        functools.partial(_diff_attn_kernel, has_cache=has_cache, lambda_init=lambda_init),
        grid=(n_seq, nq),
        in_specs=in_specs,
        out_specs=pl.BlockSpec((tq, W_BRANCH), lambda s, i: (s * nq + i, 0)),
        out_shape=jax.ShapeDtypeStruct((n_seq * t_seq, W_BRANCH), BF16),
        compiler_params=_cparams(("arbitrary", "arbitrary")),
        name="diff_attention",
    )(*args)


def _softplus(z):
    return jnp.maximum(z, 0.0) + jnp.log(1.0 + jnp.exp(-jnp.abs(z)))


def _gelu_tanh(x):
    return 0.5 * x * (1.0 + jnp.tanh(math.sqrt(2.0 / math.pi) * (x + 0.044715 * (x * x * x))))


def _rglru_kernel(x_ref, gate_ref, wc_ref, bc_ref, wg_ref, bg_ref, lam_ref, h0_ref, y_ref, st_ref,
                  af_s, uf_s, ab_s, ub_s, *, t_seq):
    t = t_seq
    x = x_ref[...].astype(F32)
    row = lax.broadcasted_iota(jnp.int32, (t, W_B), 0)
    wc = wc_ref[...]
    xc = (wc[0:1] * jnp.where(row >= 2, pltpu.roll(x, 2, 0), 0.0)
          + wc[1:2] * jnp.where(row >= 1, pltpu.roll(x, 1, 0), 0.0)
          + wc[2:3] * x
          + wc[3:4] * jnp.where(row < t - 1, pltpu.roll(x, t - 1, 0), 0.0)
          + bc_ref[...])
    gates = jnp.dot(xc.astype(BF16), wg_ref[...], preferred_element_type=F32) + bg_ref[...]
    for d, (a_s, u_s) in enumerate(((af_s, uf_s), (ab_s, ub_s))):
        rg = jax.nn.sigmoid(gates[:, (2 * d) * W_B:(2 * d + 1) * W_B])
        ig = jax.nn.sigmoid(gates[:, (2 * d + 1) * W_B:(2 * d + 2) * W_B])
        log_a = -RG_C * rg * _softplus(-lam_ref[d:d + 1, :])
        a = jnp.exp(log_a)
        a_s[...] = a
        u_s[...] = jnp.sqrt(-jnp.tanh(log_a) * (a * a + 1.0)) * (ig * xc)

    nblk = t // SUBLANES

    def body(k, carry):
        hf, hb = carry
        base_f = pl.multiple_of(k * SUBLANES, SUBLANES)
        base_b = pl.multiple_of((nblk - 1 - k) * SUBLANES, SUBLANES)
        for r in range(SUBLANES):
            rf = pl.ds(base_f + r, 1)
            rb = pl.ds(base_b + (SUBLANES - 1 - r), 1)
            hf = af_s[rf, :] * hf + uf_s[rf, :]
            hb = ab_s[rb, :] * hb + ub_s[rb, :]
            uf_s[rf, :] = hf
            ub_s[rb, :] = hb
        return hf, hb

    hf, hb = lax.fori_loop(0, nblk, body, (h0_ref[0, 0:1, :], h0_ref[0, 1:2, :]))
    st_ref[0, 0:1, :] = hf
    st_ref[0, 1:2, :] = hb
    y_ref[...] = (_gelu_tanh(gate_ref[...].astype(F32)) * (uf_s[...] + ub_s[...])).astype(y_ref.dtype)


def pack_rglru_weights(lp):
    def blockdiag(w):
        eye = jnp.eye(NB_B, dtype=w.dtype)
        return jnp.einsum('ncd,nm->ncmd', w, eye).reshape(W_B, W_B)
    wg = jnp.concatenate([blockdiag(lp['w_rg_a'][0]), blockdiag(lp['w_rg_x'][0]),
                          blockdiag(lp['w_rg_a'][1]), blockdiag(lp['w_rg_x'][1])], axis=1).astype(BF16)
    bg = jnp.concatenate([lp['b_rg_a'][0], lp['b_rg_x'][0], lp['b_rg_a'][1], lp['b_rg_x'][1]]).reshape(1, 4 * W_B)
    return dict(wg=wg, bg=bg, wc=lp['w_conv_rg'], bc=lp['b_conv_rg'].reshape(1, W_B), lam=lp['rg_lambda'])


def rglru(p, rw, h0, *, row0, n_seq, t_seq):
    rb0 = row0 // t_seq
    xb, gb = _seg_block('rg_x'), _seg_block('rg_gate')
    full = lambda shape: pl.BlockSpec(shape, lambda s: tuple(0 for _ in shape))
    in_specs = [pl.BlockSpec((t_seq, W_B), lambda s: (rb0 + s, xb)),
                pl.BlockSpec((t_seq, W_B), lambda s: (rb0 + s, gb)),
                full((CONV_W, W_B)), full((1, W_B)), full((W_B, 4 * W_B)), full((1, 4 * W_B)), full((2, W_B)),
                pl.BlockSpec((1, 2, W_B), lambda s: (s, 0, 0))]
    args = [p, p, rw['wc'], rw['bc'], rw['wg'], rw['bg'], rw['lam'], h0]
    return pl.pallas_call(
        functools.partial(_rglru_kernel, t_seq=t_seq),
        grid=(n_seq,),
        in_specs=in_specs,
        out_specs=(pl.BlockSpec((t_seq, W_B), lambda s: (s, 0)), pl.BlockSpec((1, 2, W_B), lambda s: (s, 0, 0))),
        out_shape=(jax.ShapeDtypeStruct((n_seq * t_seq, W_B), BF16), jax.ShapeDtypeStruct((n_seq, 2, W_B), F32)),
        scratch_shapes=[pltpu.VMEM((t_seq, W_B), F32)] * 4,
        compiler_params=_cparams(("arbitrary",)),
        name="rglru",
    )(*args)


def _dot_split(a, b_bf16):
    hi = a.astype(BF16)
    lo = (a - hi.astype(F32)).astype(BF16)
    return (jnp.dot(hi, b_bf16, preferred_element_type=F32) + jnp.dot(lo, b_bf16, preferred_element_type=F32))


def _log_sigmoid(z):
    return jnp.minimum(z, 0.0) - jnp.log(1.0 + jnp.exp(-jnp.abs(z)))


_TN = (((0,), (0,)), ((), ()))


def _mlstm_kernel(q_ref, k_ref, v_ref, o_ref, sm_ref, bias_ref, g_ref, c0_ref, n0_ref, m0_ref,
                  y_ref, c_out, n_out, m_out, hm_s, c_s, *, t_seq):
    L = ML_CHUNK
    nchunk = t_seq // L
    scale = DH_C ** -0.5
    ri = lax.broadcasted_iota(jnp.int32, (L, L), 0)
    ci = lax.broadcasted_iota(jnp.int32, (L, L), 1)
    lane1 = lax.broadcasted_iota(jnp.int32, (L, LANES), 1)
    ones_col = jnp.where(lane1 == 0, 1.0, 0.0).astype(BF16)
    bias = bias_ref[...]

    for d in range(2):
        causal = (ci <= ri) if d == 0 else (ci >= ri)
        tri = jnp.where(causal, 1.0, 0.0).astype(BF16)
        tri_t = jnp.where((ri <= ci) if d == 0 else (ri >= ci), 1.0, 0.0).astype(BF16)
        for h in range(H_C):
            c_s[h, :, 0:DH_C] = c0_ref[0, d, h]
            c_s[h, :, DH_C:2 * DH_C] = jnp.where(lane1 == 0, n0_ref[0, d, h], 0.0)
        m_init = tuple(m0_ref[0, :, d * H_C + h:d * H_C + h + 1] for h in range(H_C))

        def chunk(kk, ms, d=d, causal=causal, tri=tri, tri_t=tri_t):
            cidx = kk if d == 0 else nchunk - 1 - kk
            rows = pl.ds(pl.multiple_of(cidx * L, L), L)
            gsm = sm_ref[rows, :].astype(F32) + bias
            lf_all = _log_sigmoid(gsm)
            cum_cols = _dot_split_left(tri, lf_all)
            g_t = gsm.T
            cum_rows = _dot_split(lf_all.T, tri_t)
            heads = range(H_C)
            sls = [slice(h * DH_C, (h + 1) * DH_C) for h in heads]
            qs = [q_ref[rows, sl] for sl in sls]
            ks = [k_ref[rows, sl] for sl in sls]
            v_augs = [jnp.concatenate([v_ref[rows, sl], ones_col], axis=1) for sl in sls]
            qk = [lax.dot_general(qs[h], ks[h], _NT, preferred_element_type=F32) for h in heads]
            qc = [jnp.dot(qs[h], c_s[h].astype(BF16), preferred_element_type=F32) * scale for h in heads]
            cum_c = [cum_cols[:, SM_MF + d * H_C + h:SM_MF + d * H_C + h + 1] for h in heads]
            li_c = [gsm[:, SM_MI + d * H_C + h:SM_MI + d * H_C + h + 1] for h in heads]
            m_row, s = [], []
            for h in heads:
                jl, jf = SM_MI + d * H_C + h, SM_MF + d * H_C + h
                log_d = jnp.where(causal, cum_c[h] - cum_rows[jf:jf + 1, :] + g_t[jl:jl + 1, :], -jnp.inf)
                m_row.append(jnp.maximum(cum_c[h] + ms[h], jnp.max(log_d, axis=-1, keepdims=True)))
                s.append(qk[h] * (scale * jnp.exp(log_d - m_row[h])))
            sv = [jnp.dot(s[h].astype(BF16), v_augs[h], preferred_element_type=F32) for h in heads]
            new_ms = []
            for h in heads:
                w_inter = jnp.exp(cum_c[h] + ms[h] - m_row[h])
                nd = sv[h] + qc[h] * w_inter
                den = jnp.maximum(jnp.abs(nd[:, DH_C:DH_C + 1]), jnp.exp(-m_row[h]))
                h_out = nd[:, :DH_C] * (1.0 / den)
                if d == 0:
                    hm_s[rows, sls[h]] = h_out
                else:
                    hm_s[rows, sls[h]] = hm_s[rows, sls[h]] + h_out
                last = cum_c[h][L - 1:L, :] if d == 0 else cum_c[h][0:1, :]
                w_s = last - cum_c[h] + li_c[h]
                m_new = jnp.maximum(last + ms[h], jnp.max(w_s, axis=0, keepdims=True))
                decay = jnp.exp(last + ms[h] - m_new)
                kw_t = (ks[h].astype(F32) * jnp.exp(w_s - m_new)).T.astype(BF16)
                c_s[h] = decay * c_s[h] + jnp.dot(kw_t, v_augs[h], preferred_element_type=F32)
                new_ms.append(m_new)
            return tuple(new_ms)

        m_fin = lax.fori_loop(0, nchunk, chunk, m_init)
        for h in range(H_C):
            c_out[0, d, h] = c_s[h, :, 0:DH_C]
            n_out[0, d, h] = c_s[h, :, DH_C:DH_C + 1]
            m_out[0, :, d * H_C + h:d * H_C + h + 1] = m_fin[h]

    for h in range(H_C):
        sl = slice(h * DH_C, (h + 1) * DH_C)
        hm = hm_s[:, sl]
        r = lax.rsqrt(jnp.mean(hm * hm, axis=-1, keepdims=True) + EPS)
        y_ref[:, sl] = (jax.nn.sigmoid(o_ref[:, sl].astype(F32)) * (hm * r * g_ref[...])).astype(y_ref.dtype)


def _dot_split_left(a_bf16, b):
    hi = b.astype(BF16)
    lo = (b - hi.astype(F32)).astype(BF16)
    return (jnp.dot(a_bf16, hi, preferred_element_type=F32) + jnp.dot(a_bf16, lo, preferred_element_type=F32))


def mlstm(p, bias_sm, g_out, c0, n0, m0, *, row0, n_seq, t_seq):
    rb0 = row0 // t_seq
    seg = lambda nm: pl.BlockSpec((t_seq, W_BRANCH), lambda s, b=_seg_block(nm): (rb0 + s, b))
    full = lambda shape: pl.BlockSpec(shape, lambda s: tuple(0 for _ in shape))
    c_spec = pl.BlockSpec((1, 2, H_C, DH_C, DH_C), lambda s: (s, 0, 0, 0, 0))
    n_spec = pl.BlockSpec((1, 2, H_C, DH_C, 1), lambda s: (s, 0, 0, 0, 0))
    m_spec = pl.BlockSpec((1, 1, 2 * H_C), lambda s: (s, 0, 0))
    in_specs = [seg('mq'), seg('mk'), seg('mv'), seg('mo'),
                pl.BlockSpec((t_seq, LANES), lambda s: (rb0 + s, P_SMALL // LANES)),
                full((1, LANES)), full((1, DH_C)), c_spec, n_spec, m_spec]
    args = [p, p, p, p, p, bias_sm, g_out.reshape(1, DH_C), c0, n0, m0]
    return pl.pallas_call(
        functools.partial(_mlstm_kernel, t_seq=t_seq),
        grid=(n_seq,),
        in_specs=in_specs,
        out_specs=(pl.BlockSpec((t_seq, W_BRANCH), lambda s: (s, 0)), c_spec, n_spec, m_spec),
        out_shape=(jax.ShapeDtypeStruct((n_seq * t_seq, W_BRANCH), BF16),
                   jax.ShapeDtypeStruct((n_seq, 2, H_C, DH_C, DH_C), F32),
                   jax.ShapeDtypeStruct((n_seq, 2, H_C, DH_C, 1), F32),
                   jax.ShapeDtypeStruct((n_seq, 1, 2 * H_C), F32)),
        scratch_shapes=[pltpu.VMEM((t_seq, W_BRANCH), F32), pltpu.VMEM((H_C, DH_C, 2 * DH_C), F32)],
        compiler_params=_cparams(("arbitrary",)),
        name="mlstm",
    )(*args)


MERGE_TM = 512


def _merge_kernel(*refs):
    ctx_refs, lat_refs, gate_refs = refs[0:4], refs[4:8], refs[8:12]
    x_ref, mod_ref, wbr_ref, wout_ref, gn_ref = refs[12:17]
    routed = len(refs) > 19
    if routed:
        wr_hi_ref, wr_lo_ref, xo_ref, xn3_ref, lg_ref = refs[17:22]
    else:
        xo_ref, xn_ref = refs[17:19]
    is_ctx = pl.program_id(0) < R_CTX // MERGE_TM
    merged = None
    for g in range(N_BRANCH):
        yg = jnp.where(is_ctx, ctx_refs[g][...], lat_refs[g][...])
        pg = jnp.dot(yg, wbr_ref[g], preferred_element_type=F32)
        term = jax.nn.sigmoid(gate_refs[g][...].astype(F32)) * pg
        merged = term if merged is None else merged + term
    y = jnp.dot(merged.astype(BF16), wout_ref[...], preferred_element_type=F32)
    x = x_ref[...] + mod_ref[0, 2:3, :] * y
    xo_ref[...] = x
    r = lax.rsqrt(jnp.mean(x * x, axis=-1, keepdims=True) + EPS)
    xn = (x * r * gn_ref[...]) * (1.0 + mod_ref[0, 4:5, :]) + mod_ref[0, 3:4, :]
    if routed:
        x_hi = xn.astype(BF16)
        x_lo = (xn - x_hi.astype(F32)).astype(BF16)
        lg_ref[...] = (jnp.dot(x_hi, wr_hi_ref[...], preferred_element_type=F32)
                       + jnp.dot(x_lo, wr_hi_ref[...], preferred_element_type=F32)
                       + jnp.dot(x_hi, wr_lo_ref[...], preferred_element_type=F32))
        _rows_to_tiles(xn3_ref, x_hi)
    else:
        xn_ref[...] = xn.astype(xn_ref.dtype)


def merge(ys_ctx, ys_lat, p, x, mod, wbr, wout, g_ffn, w_router=None):
    tm = MERGE_TM
    rows_as_tiles = w_router is not None
    n_ctx_tiles = R_CTX // tm
    br_ctx = pl.BlockSpec((tm, W_BRANCH), lambda i: (jnp.minimum(i, n_ctx_tiles - 1), 0))
    br_lat = pl.BlockSpec((tm, W_BRANCH), lambda i: (jnp.maximum(i - n_ctx_tiles, 0), 0))
    gate = lambda g: pl.BlockSpec((tm, D_MODEL), lambda i, g=g: (i, g))
    row = pl.BlockSpec((tm, D_MODEL), lambda i: (i, 0))
    in_specs = [br_ctx] * N_BRANCH + [br_lat] * N_BRANCH + [
        gate(0), gate(1), gate(2), gate(3), row,
        pl.BlockSpec((1, 8, D_MODEL), lambda i: (_mod_row_of_tile(i, tm), 0, 0)),
        pl.BlockSpec((N_BRANCH, W_BRANCH, D_MODEL), lambda i: (0, 0, 0)),
        pl.BlockSpec((D_MODEL, D_MODEL), lambda i: (0, 0)),
        pl.BlockSpec((1, D_MODEL), lambda i: (0, 0))]
    args = [*ys_ctx, *ys_lat, p, p, p, p, x, mod, wbr, wout, g_ffn.reshape(1, D_MODEL)]
    if rows_as_tiles:
        wr = jnp.pad(w_router, ((0, 0), (0, LANES - N_EXP)))
        wr_hi = wr.astype(BF16)
        wr_lo = (wr - wr_hi.astype(F32)).astype(BF16)
        in_specs += [pl.BlockSpec((D_MODEL, LANES), lambda i: (0, 0))] * 2
        args += [wr_hi, wr_lo]
        out_specs = (row, pl.BlockSpec((tm, ROW_SUB, LANES), lambda i: (i, 0, 0)),
                     pl.BlockSpec((tm, LANES), lambda i: (i, 0)))
        out_shape = (jax.ShapeDtypeStruct((R_ALL, D_MODEL), F32), jax.ShapeDtypeStruct((R_ALL, ROW_SUB, LANES), F32),
                     jax.ShapeDtypeStruct((R_ALL, LANES), F32))
    else:
        out_specs = (row, row)
        out_shape = (jax.ShapeDtypeStruct((R_ALL, D_MODEL), F32), jax.ShapeDtypeStruct((R_ALL, D_MODEL), BF16))
    return pl.pallas_call(
        _merge_kernel,
        grid=(R_ALL // tm,),
        in_specs=in_specs,
        out_specs=out_specs,
        out_shape=out_shape,
        compiler_params=_cparams(("arbitrary",)),
        name="merge",
    )(*args)


def _new_expert(te_ref, i):
    return jnp.logical_or(i == 0, te_ref[i] != te_ref[jnp.maximum(i - 1, 0)])


def _ffn_up_kernel(te_ref, nt_ref, x_ref, wg_ref, wu_ref, h_ref, wgb_ref, wub_ref):
    i = pl.program_id(1)

    @pl.when(_new_expert(te_ref, i))
    def _():
        wgb_ref[...] = wg_ref[0].astype(BF16)
        wub_ref[...] = wu_ref[0].astype(BF16)

    @pl.when(i < nt_ref[0])
    def _():
        x = x_ref[...]
        g = jnp.dot(x, wgb_ref[...], preferred_element_type=F32)
        u = jnp.dot(x, wub_ref[...], preferred_element_type=F32)
        h_ref[...] = (g * jax.nn.sigmoid(g) * u).astype(h_ref.dtype)

    @pl.when(i >= nt_ref[0])
    def _():
        h_ref[...] = jnp.zeros(h_ref.shape, h_ref.dtype)


def ffn_up(tile_expert, n_tiles, xs, wg, wu, tm, tf, weight_buffers):
    r, d = xs.shape
    f = wg.shape[2]
    w_spec = pl.BlockSpec((1, d, tf), lambda j, i, te, nt: (te[i], 0, j), pipeline_mode=pl.Buffered(weight_buffers))
    return pl.pallas_call(
        _ffn_up_kernel,
        grid_spec=pltpu.PrefetchScalarGridSpec(
            num_scalar_prefetch=2,
            grid=(f // tf, r // tm),
            in_specs=[pl.BlockSpec((tm, d), lambda j, i, te, nt: (i, 0)), w_spec, w_spec],
            out_specs=pl.BlockSpec((tm, tf), lambda j, i, te, nt: (i, j)),
            scratch_shapes=[pltpu.VMEM((d, tf), BF16), pltpu.VMEM((d, tf), BF16)]),
        out_shape=jax.ShapeDtypeStruct((r, f), BF16),
        compiler_params=_cparams(("arbitrary", "arbitrary")),
        name="ffn_up",
    )(tile_expert, n_tiles, xs, wg, wu)


def _ffn_down_kernel(te_ref, nt_ref, h_ref, wd_ref, y_ref, wdb_ref):
    i = pl.program_id(0)

    @pl.when(_new_expert(te_ref, i))
    def _():
        wdb_ref[...] = wd_ref[0].astype(BF16)

    @pl.when(i < nt_ref[0])
    def _():
        _rows_to_tiles(y_ref, jnp.dot(h_ref[...], wdb_ref[...], preferred_element_type=F32))

    @pl.when(i >= nt_ref[0])
    def _():
        y_ref[...] = jnp.zeros(y_ref.shape, y_ref.dtype)


def ffn_down(tile_expert, n_tiles, h, wd, tm):
    r, f = h.shape
    d = wd.shape[2]
    return pl.pallas_call(
        _ffn_down_kernel,
        grid_spec=pltpu.PrefetchScalarGridSpec(
            num_scalar_prefetch=2,
            grid=(r // tm,),
            in_specs=[pl.BlockSpec((tm, f), lambda i, te, nt: (i, 0)),
                      pl.BlockSpec((1, f, d), lambda i, te, nt: (te[i], 0, 0))],
            out_specs=pl.BlockSpec((tm, ROW_SUB, LANES), lambda i, te, nt: (i, 0, 0)),
            scratch_shapes=[pltpu.VMEM((f, d), BF16)]),
        out_shape=jax.ShapeDtypeStruct((r, ROW_SUB, LANES), F32),
        compiler_params=_cparams(("arbitrary",)),
        name="ffn_down",
    )(tile_expert, n_tiles, h, wd)


def _ffn_down_res_kernel(h_ref, wd_ref, x_ref, mod_ref, y_ref, wdb_ref):
    @pl.when(pl.program_id(0) == 0)
    def _():
        wdb_ref[...] = wd_ref[...].astype(BF16)

    y = jnp.dot(h_ref[...], wdb_ref[...], preferred_element_type=F32)
    y_ref[...] = x_ref[...] + mod_ref[0, 5:6, :] * y


def ffn_down_residual(h, wd, x, mod):
    tm = 1024
    r, f = h.shape
    d = wd.shape[1]
    return pl.pallas_call(
        _ffn_down_res_kernel,
        grid=(r // tm,),
        in_specs=[pl.BlockSpec((tm, f), lambda i: (i, 0)),
                  pl.BlockSpec((f, d), lambda i: (0, 0), pipeline_mode=pl.Buffered(1)),
                  pl.BlockSpec((tm, d), lambda i: (i, 0)),
                  pl.BlockSpec((1, 8, d), lambda i: (_mod_row_of_tile(i, tm), 0, 0))],
        out_specs=pl.BlockSpec((tm, d), lambda i: (i, 0)),
        out_shape=jax.ShapeDtypeStruct((r, d), F32),
        scratch_shapes=[pltpu.VMEM((f, d), BF16)],
        compiler_params=_cparams(("arbitrary",)),
        name="ffn_down_residual",
    )(h, wd, x, mod)


def dense_swiglu_residual(xn, x, mod, wg, wu, wd):
    t = xn.shape[0]
    tm = 1024
    n_tiles = t // tm
    te = jnp.zeros((n_tiles,), jnp.int32)
    nt = jnp.full((1,), n_tiles, jnp.int32)
    h = ffn_up(te, nt, xn, wg[None], wu[None], tm=tm, tf=wg.shape[1] // 2, weight_buffers=1)
    return ffn_down_residual(h, wd, x, mod)


MOE_TM = 512
DISPATCH_TM = 512


def _moe_routing(logits, tm):
    t = logits.shape[0]
    n_assign = t * TOP_K
    top_v, top_i = lax.top_k(logits, TOP_K)
    gate = jax.nn.softmax(top_v, axis=-1)
    flat_e = top_i.reshape(-1).astype(jnp.int32)
    onehot = (flat_e[:, None] == jnp.arange(N_EXP, dtype=jnp.int32)[None, :])
    blk = LANES
    oh = onehot.astype(F32).reshape(n_assign // blk, blk, N_EXP)
    tril = jnp.tril(jnp.ones((blk, blk), F32))
    within = jnp.einsum('ij,bjk->bik', tril, oh)
    blk_tot = within[:, -1, :]
    blk_off = jnp.cumsum(blk_tot, axis=0) - blk_tot
    csum = (within + blk_off[:, None, :]).reshape(n_assign, N_EXP)
    rank = jnp.sum(jnp.where(onehot, csum - 1.0, 0.0), axis=1).astype(jnp.int32)
    counts = csum[-1].astype(jnp.int32)
    padded = (counts + tm - 1) // tm * tm
    grp_start = jnp.cumsum(padded) - padded
    raw_start = jnp.cumsum(counts) - counts
    slot_of_assign = jnp.sum(jnp.where(onehot, grp_start[None, :], 0), axis=1) + rank

    r_max = n_assign + N_EXP * tm
    tile_start = jnp.arange(r_max // tm, dtype=jnp.int32) * tm
    tile_expert = jnp.sum((tile_start[:, None] >= (grp_start + padded)[None, :]).astype(jnp.int32), axis=1)
    tile_expert = jnp.minimum(tile_expert, N_EXP - 1).astype(jnp.int32)
    n_tiles = (jnp.sum(padded) // tm).astype(jnp.int32).reshape(1)

    order = jnp.argsort(flat_e, stable=True).astype(jnp.int32)
    e_slot = jnp.repeat(tile_expert, tm)
    j = jnp.arange(r_max, dtype=jnp.int32) - grp_start[e_slot]
    src = jnp.clip(raw_start[e_slot] + j, 0, n_assign - 1)
    tok_of_slot = jnp.where(j < counts[e_slot], order[src] // TOP_K, 0)
    return gate, slot_of_assign, tok_of_slot, tile_expert, n_tiles


ROW_SUB = D_MODEL // LANES


def _rows_to_tiles(o3_ref, x):
    for j in range(ROW_SUB):
        o3_ref[:, j, :] = x[:, j * LANES:(j + 1) * LANES].astype(o3_ref.dtype)


def _start_row_gather(idx_ref, src_ref, dst, sem, n_rows):
    def body(q, carry):
        for u in range(2):
            r = 2 * q + u
            pltpu.make_async_copy(src_ref.at[idx_ref[0, 0, r]], dst.at[r], sem).start(priority=u)
        return carry

    lax.fori_loop(0, n_rows // 2, body, 0, unroll=4)


def _wait_row_gather(src_ref, dst, sem, n_rows):
    pltpu.make_async_copy(src_ref.at[pl.ds(0, n_rows)], dst, sem).wait()


def _tiles_to_rows(tiles, rows_ref):
    for j in range(ROW_SUB):
        rows_ref[:, j * LANES:(j + 1) * LANES] = tiles[:, j, :]


def _dispatch_kernel(nt_ref, idx_ref, idx_next_ref, src_ref, o_ref, buf, rows, sem):
    i = pl.program_id(0)
    nt = nt_ref[0]
    tm = o_ref.shape[0]
    slot = i % 2

    @pl.when(i == 0)
    def _():
        _start_row_gather(idx_ref, src_ref, buf.at[0], sem.at[0], tm)

    @pl.when(i + 1 < nt)
    def _():
        _start_row_gather(idx_next_ref, src_ref, buf.at[1 - slot], sem.at[1 - slot], tm)

    @pl.when(i < nt)
    def _():
        _wait_row_gather(src_ref, buf.at[slot], sem.at[slot], tm)
        _tiles_to_rows(buf.at[slot], rows)
        o_ref[...] = rows[...].astype(o_ref.dtype)

    @pl.when(i >= nt)
    def _():
        o_ref[...] = jnp.zeros(o_ref.shape, o_ref.dtype)


def moe_dispatch(n_tiles, tok_of_slot, xn3, tm):
    r = tok_of_slot.shape[0]
    last = r // tm - 1
    idx = tok_of_slot.reshape(r // tm, 1, tm)
    return pl.pallas_call(
        _dispatch_kernel,
        grid_spec=pltpu.PrefetchScalarGridSpec(
            num_scalar_prefetch=1,
            grid=(r // tm,),
            in_specs=[pl.BlockSpec((1, 1, tm), lambda i, nt: (i, 0, 0), memory_space=pltpu.SMEM),
                      pl.BlockSpec((1, 1, tm), lambda i, nt: (jnp.minimum(i + 1, last), 0, 0),
                                   memory_space=pltpu.SMEM),
                      pl.BlockSpec(memory_space=pl.ANY)],
            out_specs=pl.BlockSpec((tm, D_MODEL), lambda i, nt: (i, 0)),
            scratch_shapes=[pltpu.VMEM((2, tm, ROW_SUB, LANES), F32), pltpu.VMEM((tm, D_MODEL), F32),
                            pltpu.SemaphoreType.DMA((2,))]),
        out_shape=jax.ShapeDtypeStruct((r, D_MODEL), BF16),
        compiler_params=_cparams(("arbitrary",)),
        name="moe_dispatch",
    )(n_tiles, idx, idx, xn3)


COMBINE_TM = 256


def _combine_kernel(idx_ref, idx_next_ref, ys_ref, x_ref, gate_ref, mod_ref, oc_ref, ol_ref, buf, rows, sem):
    i = pl.program_id(0)
    tm = oc_ref.shape[0]
    n_rows = TOP_K * tm
    slot = i % 2

    @pl.when(i == 0)
    def _():
        _start_row_gather(idx_ref, ys_ref, buf.at[0], sem.at[0], n_rows)

    @pl.when(i + 1 < pl.num_programs(0))
    def _():
        _start_row_gather(idx_next_ref, ys_ref, buf.at[1 - slot], sem.at[1 - slot], n_rows)

    _wait_row_gather(ys_ref, buf.at[slot], sem.at[slot], n_rows)
    _tiles_to_rows(buf.at[slot], rows)
    f = gate_ref[:, 0:1] * rows[0:tm, :] + gate_ref[:, 1:2] * rows[tm:2 * tm, :]
    res = x_ref[...] + mod_ref[0, 5:6, :] * f

    @pl.when(i < R_CTX // COMBINE_TM)
    def _():
        oc_ref[...] = res

    @pl.when(i >= R_CTX // COMBINE_TM)
    def _():
        ol_ref[...] = res


def moe_combine(slot_of_assign, ys3, x, gate, mod):
    t, d = x.shape
    tm = COMBINE_TM
    idx = slot_of_assign.reshape(t // tm, tm, TOP_K).transpose(0, 2, 1).reshape(t // tm, 1, TOP_K * tm)
    last = t // tm - 1
    n_ctx = R_CTX // tm
    return pl.pallas_call(
        _combine_kernel,
        grid=(t // tm,),
        in_specs=[pl.BlockSpec((1, 1, TOP_K * tm), lambda i: (i, 0, 0), memory_space=pltpu.SMEM),
                  pl.BlockSpec((1, 1, TOP_K * tm), lambda i: (jnp.minimum(i + 1, last), 0, 0),
                               memory_space=pltpu.SMEM),
                  pl.BlockSpec(memory_space=pl.ANY),
                  pl.BlockSpec((tm, d), lambda i: (i, 0)),
                  pl.BlockSpec((tm, TOP_K), lambda i: (i, 0)),
                  pl.BlockSpec((1, 8, d), lambda i: (_mod_row_of_tile(i, tm), 0, 0))],
        out_specs=(pl.BlockSpec((tm, d), lambda i: (jnp.minimum(i, n_ctx - 1), 0)),
                   pl.BlockSpec((tm, d), lambda i: (jnp.maximum(i - n_ctx, 0), 0))),
        out_shape=(jax.ShapeDtypeStruct((R_CTX, d), F32), jax.ShapeDtypeStruct((t - R_CTX, d), F32)),
        scratch_shapes=[pltpu.VMEM((2, TOP_K * tm, ROW_SUB, LANES), F32), pltpu.VMEM((TOP_K * tm, d), F32),
                        pltpu.SemaphoreType.DMA((2,))],
        compiler_params=_cparams(("arbitrary",)),
        name="moe_combine",
    )(idx, idx, ys3, x, gate, mod)


def moe_swiglu_residual(xn3, logits, x, mod, wg, wu, wd):
    tm = MOE_TM
    gate, slot_of_assign, tok_of_slot, tile_expert, n_tiles = _moe_routing(logits[:, :N_EXP], tm)
    xs = moe_dispatch(n_tiles * (tm // DISPATCH_TM), tok_of_slot, xn3, DISPATCH_TM)
    h = ffn_up(tile_expert, n_tiles, xs, wg, wu, tm=tm, tf=wg.shape[2] // 4, weight_buffers=2)
    ys3 = ffn_down(tile_expert, n_tiles, h, wd, tm=tm)
    return moe_combine(slot_of_assign, ys3, x, gate, mod)


def _layer(x, cond, lp, l, ctx, tabs_a, tabs_d):
    mod = modulation(cond, lp['w_mod'], lp['b_mod']).reshape(cond.shape[0], 6, D_MODEL)
    mod = jnp.pad(mod, ((0, 0), (0, 2), (0, 0)))
    p = in_proj(x, mod, lp['g_norm_mix'], pack_w_in(lp['w_in']))

    mw = pack_mla_weights(lp)
    q_a, k_a, v_a, ckv, kr = mla_prep(p, mw, tabs_a)
    kr_c = jnp.pad(ctx['mla_krope'].reshape(-1, ROPE_A), ((0, 0), (0, LANES - ROPE_A)))
    kc_a, vc_a = mla_prep_cache(ctx['mla_ckv'].reshape(-1, KV_RANK), kr_c, mw)
    ya_c = mla_attention(q_a, k_a, v_a, None, None, row0=0, n_seq=N_CTX_SEQ, t_seq=T_CTX, tq=T_CTX)
    ya_l = mla_attention(q_a, k_a, v_a, kc_a, vc_a, row0=R_CTX, n_seq=N_LAT_SEQ, t_seq=T_LAT, tq=ATTN_TQ)

    rw = pack_rglru_weights(lp)
    yb_c, st_rg = rglru(p, rw, jnp.zeros((N_CTX_SEQ, 2, W_B), F32), row0=0, n_seq=N_CTX_SEQ, t_seq=T_CTX)
    yb_l, _ = rglru(p, rw, ctx['rglru'], row0=R_CTX, n_seq=N_LAT_SEQ, t_seq=T_LAT)

    bias_sm = jnp.zeros((LANES,), F32).at[SM_MI:SM_MI + 2 * H_C].set(lp['b_ml_i'].reshape(-1))
    bias_sm = bias_sm.at[SM_MF:SM_MF + 2 * H_C].set(lp['b_ml_f'].reshape(-1)).reshape(1, LANES)
    c0_ctx = jnp.zeros((N_CTX_SEQ, 2, H_C, DH_C, DH_C), F32)
    n0_ctx = jnp.zeros((N_CTX_SEQ, 2, H_C, DH_C, 1), F32)
    m0_ctx = jnp.zeros((N_CTX_SEQ, 1, 2 * H_C), F32)
    m0_lat = ctx['mlstm_m'].reshape(N_LAT_SEQ, 1, 2 * H_C)
    yc_c, c_fin, n_fin, m_fin = mlstm(p, bias_sm, lp['g_ml_out'], c0_ctx, n0_ctx, m0_ctx,
                                      row0=0, n_seq=N_CTX_SEQ, t_seq=T_CTX)
    yc_l, _, _, _ = mlstm(p, bias_sm, lp['g_ml_out'], ctx['mlstm_C'], ctx['mlstm_n'][..., None], m0_lat,
                          row0=R_CTX, n_seq=N_LAT_SEQ, t_seq=T_LAT)

    lambda_init = 0.8 - 0.6 * math.exp(-0.3 * l)
    qd, kd_own, kd_plain = diff_prep(p, lp, tabs_d)
    dkc = ctx['diff_k'].reshape(-1, W_BRANCH)
    dvc = ctx['diff_v'].reshape(-1, W_BRANCH)
    yd_c = diff_attention(lp['diff_lambda'], qd, kd_own, p, None, None, lp['g_diff_sub'],
                          row0=0, n_seq=N_CTX_SEQ, t_seq=T_CTX, tq=T_CTX, lambda_init=lambda_init)
    yd_l = diff_attention(lp['diff_lambda'], qd, kd_own, p, dkc, dvc, lp['g_diff_sub'],
                          row0=R_CTX, n_seq=N_LAT_SEQ, t_seq=T_LAT, tq=ATTN_TQ, lambda_init=lambda_init)

    merged = merge((ya_c, yb_c, yc_c, yd_c), (ya_l, yb_l, yc_l, yd_l), p, x, mod, lp['w_br'].astype(BF16),
                   lp['w_out'].astype(BF16), lp['g_norm_ffn'], w_router=lp['moe'][0] if 'moe' in lp else None)
    if 'ffn' in lp:
        x, xn = merged
        x = dense_swiglu_residual(xn, x, mod, *lp['ffn'])
    else:
        x, xn3, logits = merged
        x = moe_swiglu_residual(xn3, logits, x, mod, *lp['moe'][1:])

    dv0 = _seg_block('dv') * W_BRANCH
    ctx_out = (ckv[:R_CTX].reshape(N_CTX_SEQ, T_CTX, KV_RANK),
               kr[:R_CTX, :ROPE_A].reshape(N_CTX_SEQ, T_CTX, ROPE_A),
               kd_plain[:R_CTX].reshape(N_CTX_SEQ, T_CTX, 2, H_D, DH_D),
               p[:R_CTX, dv0:dv0 + W_BRANCH].astype(F32).reshape(N_CTX_SEQ, T_CTX, H_D, 2 * DH_D),
               st_rg,
               c_fin,
               n_fin.reshape(N_CTX_SEQ, 2, H_C, DH_C),
               m_fin.reshape(N_CTX_SEQ, 2, H_C))
    return x, ctx_out


def kernel(x_prompt, x_sample, cache_mla_ckv, cache_mla_krope, cache_diff_k, cache_diff_v,
           state_rglru, state_mlstm_C, state_mlstm_n, state_mlstm_m, c, c_ctx,
           w_mod, b_mod, g_norm_mix, g_norm_ffn, w_in, g_mla_qlat, w_mla_uq, g_mla_kvlat, w_mla_ukv,
           g_mla_qn, g_mla_kn, w_conv_rg, b_conv_rg, w_rg_a, b_rg_a, w_rg_x, b_rg_x, rg_lambda,
           b_ml_i, b_ml_f, g_ml_out, g_diff_qn, g_diff_kn, diff_lambda, g_diff_sub, w_br, w_out,
           w_ffn_gate, w_ffn_up, w_ffn_down, w_router, w_moe_gate, w_moe_up, w_moe_down):
    assert x_prompt.shape == (N_CTX_SEQ, T_CTX, D_MODEL) and x_sample.shape == (N_LAT_SEQ, T_LAT, D_MODEL)
    tabs_a = _rope_tables(ROPE_A, (NOPE_A,), MLA_TM)
    tabs_d = _rope_tables(DH_D, (0, DH_D), DIFF_TM)
    cond = jnp.concatenate([c_ctx.reshape(1, D_MODEL), c, jnp.zeros((16 - 1 - N_LAT_SEQ, D_MODEL), F32)], axis=0)
    x = jnp.concatenate([x_prompt.reshape(R_CTX, D_MODEL), x_sample.reshape(R_LAT, D_MODEL)], axis=0)
    new = []
    for l in range(DEPTH):
        lp = dict(w_mod=w_mod[l], b_mod=b_mod[l], g_norm_mix=g_norm_mix[l], g_norm_ffn=g_norm_ffn[l], w_in=w_in[l],
                  g_mla_qlat=g_mla_qlat[l], w_mla_uq=w_mla_uq[l], g_mla_kvlat=g_mla_kvlat[l], w_mla_ukv=w_mla_ukv[l],
                  g_mla_qn=g_mla_qn[l], g_mla_kn=g_mla_kn[l], w_conv_rg=w_conv_rg[l], b_conv_rg=b_conv_rg[l],
                  w_rg_a=w_rg_a[l], b_rg_a=b_rg_a[l], w_rg_x=w_rg_x[l], b_rg_x=b_rg_x[l], rg_lambda=rg_lambda[l],
                  b_ml_i=b_ml_i[l], b_ml_f=b_ml_f[l], g_ml_out=g_ml_out[l], g_diff_qn=g_diff_qn[l],
                  g_diff_kn=g_diff_kn[l], diff_lambda=diff_lambda[l], g_diff_sub=g_diff_sub[l],
                  w_br=w_br[l], w_out=w_out[l])
        if l % 2 == 0:
            lp['ffn'] = (w_ffn_gate[l // 2], w_ffn_up[l // 2], w_ffn_down[l // 2])
        else:
            lp['moe'] = (w_router[l // 2], w_moe_gate[l // 2], w_moe_up[l // 2], w_moe_down[l // 2])
        ctx_l = dict(mla_ckv=cache_mla_ckv[:, l], mla_krope=cache_mla_krope[:, l], diff_k=cache_diff_k[:, l],
                     diff_v=cache_diff_v[:, l], rglru=state_rglru[:, l], mlstm_C=state_mlstm_C[:, l],
                     mlstm_n=state_mlstm_n[:, l], mlstm_m=state_mlstm_m[:, l])
        if isinstance(x, tuple):
            x = jnp.concatenate(x, axis=0)
        x, st = _layer(x, cond, lp, l, ctx_l, tabs_a, tabs_d)
        new.append(st)
    outs = tuple(jnp.stack([s[i] for s in new], axis=1) for i in range(8))
    x_ctx, x_lat = x if isinstance(x, tuple) else (x[:R_CTX], x[R_CTX:])
    return (x_ctx.reshape(N_CTX_SEQ, T_CTX, D_MODEL), x_lat.reshape(N_LAT_SEQ, T_LAT, D_MODEL)) + outs
```

```python
import functools
import math

import jax
import jax.numpy as jnp
import numpy as np
from jax import lax
from jax.experimental import pallas as pl
from jax.experimental.pallas import tpu as pltpu

D_MODEL = 1024
DEPTH = 2
GRID_W = 64
ROPE_BASE = 10000.0
EPS = 1e-6
N_BRANCH = 4
W_BRANCH = D_MODEL // 2

H_A = 8
NOPE_A = 64
ROPE_A = 32
V_A = W_BRANCH // H_A
Q_RANK = D_MODEL // 4
KV_RANK = D_MODEL // 8
MLA_SCALE = (NOPE_A + ROPE_A) ** -0.5

W_B = W_BRANCH
NB_B = 8
BW_B = W_B // NB_B
CONV_W = 4
RG_C = 8.0

H_C = 4
DH_C = W_BRANCH // H_C
ML_CHUNK = 128

H_D = 4
DH_D = W_BRANCH // (2 * H_D)
DIFF_SCALE = DH_D ** -0.5

N_EXP = 8
TOP_K = 2

V7X_VMEM_LIMIT_BYTES = 56 * 1024 * 1024
LANES = 128
SUBLANES = 8

BF16 = jnp.bfloat16
F32 = jnp.float32

P_GATE = 0
P_SEG = N_BRANCH * D_MODEL
SEG_NAMES = ('rg_x', 'rg_gate', 'mq', 'mk', 'mv', 'mo', 'dq', 'dk', 'dv')
P_QLAT = P_SEG + 9 * W_BRANCH
P_KVLAT = P_QLAT + Q_RANK
P_SMALL = P_KVLAT + KV_RANK
P_WIDTH = P_SMALL + LANES
SM_MI = ROPE_A
SM_MF = ROPE_A + 2 * H_C


def _seg_block(name):
    return (P_SEG + SEG_NAMES.index(name) * W_BRANCH) // W_BRANCH


N_CTX_SEQ, T_CTX = 16, 256
N_LAT_SEQ, T_LAT = 8, 1024
PAST_LEN = 512
R_CTX = N_CTX_SEQ * T_CTX
R_LAT = N_LAT_SEQ * T_LAT
R_ALL = R_CTX + R_LAT


def _cparams(sem):
    return pltpu.CompilerParams(dimension_semantics=sem, vmem_limit_bytes=V7X_VMEM_LIMIT_BYTES)


def _mod_row_of_tile(i, tm):
    n_ctx_tiles = R_CTX // tm
    per_seq = T_LAT // tm
    return jnp.where(i < n_ctx_tiles, 0, 1 + (i - n_ctx_tiles) // per_seq)


def _pos_block_of_tile(i, tm):
    n_ctx_tiles = R_CTX // tm
    per_seq = T_LAT // tm
    return jnp.where(i < n_ctx_tiles, per_seq, (i - n_ctx_tiles) % per_seq)


def _mod_kernel(c_ref, w_ref, b_ref, o_ref):
    c = c_ref[...]
    s = (c * jax.nn.sigmoid(c)).astype(BF16)
    o_ref[...] = jnp.dot(s, w_ref[...].astype(BF16), preferred_element_type=F32) + b_ref[...]


def modulation(cond, w_mod, b_mod):
    m, d = cond.shape
    n = w_mod.shape[1]
    tn = 1536
    return pl.pallas_call(
        _mod_kernel,
        grid=(n // tn,),
        in_specs=[pl.BlockSpec((m, d), lambda j: (0, 0)),
                  pl.BlockSpec((d, tn), lambda j: (0, j)),
                  pl.BlockSpec((1, tn), lambda j: (0, j))],
        out_specs=pl.BlockSpec((m, tn), lambda j: (0, j)),
        out_shape=jax.ShapeDtypeStruct((m, n), F32),
        compiler_params=_cparams(("arbitrary",)),
        name="modulation",
    )(cond, w_mod, b_mod.reshape(1, n))


def _in_proj_kernel(x_ref, mod_ref, g_ref, w_ref, o_ref, xn_ref):
    @pl.when(pl.program_id(1) == 0)
    def _():
        x = x_ref[...]
        r = lax.rsqrt(jnp.mean(x * x, axis=-1, keepdims=True) + EPS)
        sh = mod_ref[0, 0:1, :]
        sc = mod_ref[0, 1:2, :]
        xn_ref[...] = ((x * r * g_ref[...]) * (1.0 + sc) + sh).astype(BF16)

    o_ref[...] = jnp.dot(xn_ref[...], w_ref[...], preferred_element_type=F32).astype(o_ref.dtype)


def in_proj(x, mod, g, w_p):
    tm, tn = 1024, P_WIDTH // 4
    m, d = x.shape
    n = w_p.shape[1]
    return pl.pallas_call(
        _in_proj_kernel,
        grid=(m // tm, n // tn),
        in_specs=[pl.BlockSpec((tm, d), lambda i, j: (i, 0)),
                  pl.BlockSpec((1, 8, d), lambda i, j: (_mod_row_of_tile(i, tm), 0, 0)),
                  pl.BlockSpec((1, d), lambda i, j: (0, 0)),
                  pl.BlockSpec((d, tn), lambda i, j: (0, j))],
        out_specs=pl.BlockSpec((tm, tn), lambda i, j: (i, j)),
        out_shape=jax.ShapeDtypeStruct((m, n), BF16),
        scratch_shapes=[pltpu.VMEM((tm, d), BF16)],
        compiler_params=_cparams(("arbitrary", "arbitrary")),
        name="in_proj",
    )(x, mod, g.reshape(1, d), w_p)


def pack_w_in(w_in):
    o_rg = Q_RANK + KV_RANK + ROPE_A
    o_mi = o_rg + 6 * W_BRANCH
    o_dq = o_mi + 4 * H_C
    o_gate = o_dq + 3 * W_BRANCH
    pad = jnp.zeros((w_in.shape[0], LANES - ROPE_A - 4 * H_C), w_in.dtype)
    parts = [w_in[:, o_gate:], w_in[:, o_rg:o_mi], w_in[:, o_dq:o_gate], w_in[:, :Q_RANK + KV_RANK],
             w_in[:, Q_RANK + KV_RANK:o_rg], w_in[:, o_mi:o_dq], pad]
    return jnp.concatenate(parts, axis=1).astype(BF16)


def _rope_partner(rot_dim, lane_starts):
    nf = rot_dim // 4
    partner = np.arange(LANES)
    for s0 in lane_starts:
        for a in range(2):
            lo = s0 + a * 2 * nf
            partner[lo:lo + nf] = np.arange(lo + nf, lo + 2 * nf)
            partner[lo + nf:lo + 2 * nf] = np.arange(lo, lo + nf)
    return partner


def _rope_tables(rot_dim, lane_starts, tm):
    rows = T_LAT // GRID_W
    r, c = np.meshgrid(np.arange(rows, dtype=np.float32), np.arange(GRID_W, dtype=np.float32), indexing='ij')
    nf = rot_dim // 4
    inv = (np.float32(ROPE_BASE) ** (-np.arange(nf, dtype=np.float32) / np.float32(nf))).astype(np.float32)
    ang = np.stack([r.reshape(-1)[:, None] * inv, c.reshape(-1)[:, None] * inv], axis=1).astype(np.float32)
    cos, sin = np.cos(ang).astype(np.float32), np.sin(ang).astype(np.float32)
    tc = np.ones((T_LAT + tm, LANES), np.float32)
    ta = np.zeros((T_LAT + tm, LANES), np.float32)
    tb = np.zeros((T_LAT + tm, LANES), np.float32)
    for s0 in lane_starts:
        for a in range(2):
            lo = s0 + a * 2 * nf
            tc[:T_LAT, lo:lo + nf] = cos[:, a]
            tc[:T_LAT, lo + nf:lo + 2 * nf] = cos[:, a]
            ta[:T_LAT, lo:lo + nf] = -sin[:, a]
            tb[:T_LAT, lo + nf:lo + 2 * nf] = sin[:, a]
    return jnp.asarray(tc), jnp.asarray(ta + tb)


MLA_TM = 512
ATTN_TQ = 512
MLA_HEADS_PER_STEP = 8
QK_A = NOPE_A + ROPE_A


def _mla_prep_kernel(*refs, has_q, norm_ckv):
    if has_q:
        (qlat_ref, gq_ref, wuq_ref, gqn_ref, kv_ref, sm_ref, gkv_ref, wkc_ref, wv_ref, gkn_ref,
         c_ref, s_ref, q_o, k_o, v_o, ckv_o, kr_o) = refs
        c, sn = c_ref[...], s_ref[...]
    else:
        (kv_ref, sm_ref, gkv_ref, wkc_ref, wv_ref, gkn_ref, k_o, v_o) = refs

    def heads(z, g_ref, o_ref, scale):
        for h in range(H_A):
            s = z[:, h * LANES:(h + 1) * LANES]
            r = lax.rsqrt(jnp.sum(s * s, axis=-1, keepdims=True) * (1.0 / QK_A) + EPS)
            y = s * r * g_ref[0:1, :]
            if has_q:
                sw = z[:, (H_A + h) * LANES:(H_A + h + 1) * LANES]
                y = y * c + (sw * r * g_ref[1:2, :]) * sn
            if scale != 1.0:
                y = y * scale
            o_ref[:, h * LANES:(h + 1) * LANES] = y.astype(o_ref.dtype)

    if has_q:
        ql = qlat_ref[...].astype(F32)
        qn = ql * lax.rsqrt(jnp.mean(ql * ql, axis=-1, keepdims=True) + EPS) * gq_ref[...]
        q = jnp.dot(qn.astype(BF16), wuq_ref[...], preferred_element_type=F32)
        heads(q, gqn_ref, q_o, MLA_SCALE)

    kv = kv_ref[...].astype(F32)
    if norm_ckv:
        ckv = kv * lax.rsqrt(jnp.mean(kv * kv, axis=-1, keepdims=True) + EPS) * gkv_ref[...]
    else:
        ckv = kv
    sm = sm_ref[...]
    ckv_b = ckv.astype(BF16)
    kin = jnp.concatenate([ckv_b, sm.astype(BF16)], axis=1)
    wkc = wkc_ref[...] if has_q else wkc_ref[:, :H_A * LANES]
    k = jnp.dot(kin, wkc, preferred_element_type=F32)
    heads(k, gkn_ref, k_o, 1.0)
    v_o[...] = jnp.dot(ckv_b, wv_ref[...], preferred_element_type=F32).astype(v_o.dtype)
    if has_q:
        ckv_o[...] = ckv
        kr_o[...] = sm.astype(F32)


def pack_mla_weights(lp):
    wuq = lp['w_mla_uq'].reshape(Q_RANK, H_A, QK_A)
    wuq_p = jnp.pad(wuq, ((0, 0), (0, 0), (0, LANES - QK_A))).reshape(Q_RANK, H_A * LANES).astype(BF16)
    wukv = lp['w_mla_ukv'].reshape(KV_RANK, H_A, NOPE_A + V_A)
    wk = jnp.pad(wukv[:, :, :NOPE_A], ((0, 0), (0, 0), (0, LANES - NOPE_A))).reshape(KV_RANK, H_A * LANES)
    place = np.zeros((LANES, H_A, LANES), np.float32)
    for h in range(H_A):
        place[np.arange(ROPE_A), h, NOPE_A + np.arange(ROPE_A)] = 1.0
    wkc = jnp.concatenate([wk, jnp.asarray(place.reshape(LANES, H_A * LANES))], axis=0).astype(BF16)
    wv = wukv[:, :, NOPE_A:]
    wv_even = jnp.pad(wv, ((0, 0), (0, 0), (0, LANES - V_A)))
    wv_odd = jnp.pad(wv, ((0, 0), (0, 0), (LANES - V_A, 0)))
    odd = (np.arange(H_A) % 2 == 1)[None, :, None]
    wv_p = jnp.where(odd, wv_odd, wv_even).reshape(KV_RANK, H_A * LANES).astype(BF16)
    partner = _rope_partner(ROPE_A, (NOPE_A,))
    cols = (np.arange(H_A)[:, None] * LANES + partner[None, :]).reshape(-1)
    with_partner = lambda w: jnp.concatenate([w, w[:, cols]], axis=1)
    pad_g = lambda g: jnp.pad(g, (0, LANES - QK_A))
    gain2 = lambda g: jnp.stack([pad_g(g), pad_g(g)[partner]])
    return dict(wuq=with_partner(wuq_p), wkc=with_partner(wkc), wv=wv_p,
                gqn=gain2(lp['g_mla_qn']), gkn=gain2(lp['g_mla_kn']),
                gq=lp['g_mla_qlat'].reshape(1, Q_RANK), gkv=lp['g_mla_kvlat'].reshape(1, KV_RANK))


def mla_prep(p, mw, tabs):
    tm = MLA_TM
    n = R_ALL // tm
    full = lambda shape: pl.BlockSpec(shape, lambda i: (0, 0))
    tab = pl.BlockSpec((tm, LANES), lambda i: (_pos_block_of_tile(i, tm), 0))
    wide = H_A * LANES
    out_shape = (jax.ShapeDtypeStruct((R_ALL, wide), BF16), jax.ShapeDtypeStruct((R_ALL, wide), BF16),
                 jax.ShapeDtypeStruct((R_ALL, wide), BF16), jax.ShapeDtypeStruct((R_ALL, KV_RANK), F32),
                 jax.ShapeDtypeStruct((R_ALL, LANES), F32))
    row = lambda w: pl.BlockSpec((tm, w), lambda i: (i, 0))
    return pl.pallas_call(
        functools.partial(_mla_prep_kernel, has_q=True, norm_ckv=True),
        grid=(n,),
        in_specs=[pl.BlockSpec((tm, Q_RANK), lambda i: (i, P_QLAT // Q_RANK)), full((1, Q_RANK)),
                  full((Q_RANK, 2 * wide)), full((2, LANES)),
                  pl.BlockSpec((tm, KV_RANK), lambda i: (i, P_KVLAT // KV_RANK)),
                  pl.BlockSpec((tm, LANES), lambda i: (i, P_SMALL // LANES)), full((1, KV_RANK)),
                  full((2 * LANES, 2 * wide)), full((KV_RANK, wide)), full((2, LANES)), tab, tab],
        out_specs=(row(wide), row(wide), row(wide), row(KV_RANK), row(LANES)),
        out_shape=out_shape,
        compiler_params=_cparams(("arbitrary",)),
        name="mla_prep",
    )(p, mw['gq'], mw['wuq'], mw['gqn'], p, p, mw['gkv'], mw['wkc'], mw['wv'], mw['gkn'], *tabs)


def mla_prep_cache(ckv_c, kr_c, mw):
    tm = MLA_TM
    r = ckv_c.shape[0]
    full = lambda shape: pl.BlockSpec(shape, lambda i: (0, 0))
    wide = H_A * LANES
    row = lambda w: pl.BlockSpec((tm, w), lambda i: (i, 0))
    return pl.pallas_call(
        functools.partial(_mla_prep_kernel, has_q=False, norm_ckv=False),
        grid=(r // tm,),
        in_specs=[row(KV_RANK), row(LANES), full((1, KV_RANK)), full((2 * LANES, 2 * wide)), full((KV_RANK, wide)),
                  full((2, LANES))],
        out_specs=(row(wide), row(wide)),
        out_shape=(jax.ShapeDtypeStruct((r, wide), BF16), jax.ShapeDtypeStruct((r, wide), BF16)),
        compiler_params=_cparams(("arbitrary",)),
        name="mla_prep_cache",
    )(ckv_c, kr_c, mw['gkv'], mw['wkc'], mw['wv'], mw['gkn'])


_NT = (((1,), (1,)), ((), ()))


def _mla_attn_kernel(*refs, has_cache):
    if has_cache:
        q_ref, ko_ref, vo_ref, kc_ref, vc_ref, o_ref = refs
    else:
        q_ref, ko_ref, vo_ref, o_ref = refs
    n_heads = q_ref.shape[1] // LANES
    sls = [slice(h * LANES, (h + 1) * LANES) for h in range(n_heads)]
    def scores(sl):
        s_o = lax.dot_general(q_ref[:, sl], ko_ref[:, sl], _NT, preferred_element_type=F32)
        s_c = lax.dot_general(q_ref[:, sl], kc_ref[:, sl], _NT, preferred_element_type=F32) if has_cache else None
        return s_o, s_c

    outs = []
    nxt = scores(sls[0])
    for h, sl in enumerate(sls):
        s_o, s_c = nxt
        if h + 1 < n_heads:
            nxt = scores(sls[h + 1])
        m = jnp.max(s_o, axis=-1, keepdims=True)
        if has_cache:
            m = jnp.maximum(m, jnp.max(s_c, axis=-1, keepdims=True))
        e_o = jnp.exp(s_o - m)
        l = jnp.sum(e_o, axis=-1, keepdims=True)
        pv = jnp.dot(e_o.astype(BF16), vo_ref[:, sl], preferred_element_type=F32)
        if has_cache:
            e_c = jnp.exp(s_c - m)
            l = l + jnp.sum(e_c, axis=-1, keepdims=True)
            pv = pv + jnp.dot(e_c.astype(BF16), vc_ref[:, sl], preferred_element_type=F32)
        outs.append(pv * (1.0 / l))
    for p in range(n_heads // 2):
        o_ref[:, p * LANES:(p + 1) * LANES] = (outs[2 * p] + outs[2 * p + 1]).astype(o_ref.dtype)


def mla_attention(q, k, v, kc, vc, *, row0, n_seq, t_seq, tq):
    has_cache = kc is not None
    wh = MLA_HEADS_PER_STEP * LANES
    n_grp = H_A // MLA_HEADS_PER_STEP
    nq = t_seq // tq
    qb0, kb0 = row0 // tq, row0 // t_seq
    in_specs = [pl.BlockSpec((tq, wh), lambda s, p, i: (qb0 + s * nq + i, p)),
                pl.BlockSpec((t_seq, wh), lambda s, p, i: (kb0 + s, p)),
                pl.BlockSpec((t_seq, wh), lambda s, p, i: (kb0 + s, p))]
    args = [q, k, v]
    if has_cache:
        in_specs += [pl.BlockSpec((PAST_LEN, wh), lambda s, p, i: (s, p)),
                     pl.BlockSpec((PAST_LEN, wh), lambda s, p, i: (s, p))]
        args += [kc, vc]
    return pl.pallas_call(
        functools.partial(_mla_attn_kernel, has_cache=has_cache),
        grid=(n_seq, n_grp, nq),
        in_specs=in_specs,
        out_specs=pl.BlockSpec((tq, wh // 2), lambda s, p, i: (s * nq + i, p)),
        out_shape=jax.ShapeDtypeStruct((n_seq * t_seq, W_BRANCH), BF16),
        compiler_params=_cparams(("arbitrary", "arbitrary", "arbitrary")),
        name="mla_attention",
    )(*args)


DIFF_TM = 512


def _diff_prep_kernel(dq_ref, dk_ref, gq_ref, gk_ref, perm_ref, c_ref, s_ref, q_o, ko_o, kp_o):
    c, sn = c_ref[...], s_ref[...]
    lane = lax.broadcasted_iota(jnp.int32, (1, LANES), 1)
    lo = lane < DH_D

    def inv_rms(x):
        x2 = x * x
        s_lo = jnp.sum(jnp.where(lo, x2, 0.0), axis=-1, keepdims=True)
        s_hi = jnp.sum(jnp.where(lo, 0.0, x2), axis=-1, keepdims=True)
        return lax.rsqrt(jnp.where(lo, s_lo, s_hi) * (1.0 / DH_D) + EPS)

    def rotated(n, x_sw, r, g_ref):
        return n * c + (x_sw * r * g_ref[1:2, :]) * sn

    for j in range(W_BRANCH // LANES):
        sl = slice(j * LANES, (j + 1) * LANES)
        xq, xk = dq_ref[:, sl], dk_ref[:, sl]
        q_sw = jnp.dot(xq, perm_ref[...], preferred_element_type=F32)
        k_sw = jnp.dot(xk, perm_ref[...], preferred_element_type=F32)
        xq, xk = xq.astype(F32), xk.astype(F32)
        rq, rk = inv_rms(xq), inv_rms(xk)
        qn = xq * rq * gq_ref[0:1, :]
        kn = xk * rk * gk_ref[0:1, :]
        q_o[:, sl] = (rotated(qn, q_sw, rq, gq_ref) * DIFF_SCALE).astype(q_o.dtype)
        kp_o[:, sl] = kn
        ko_o[:, sl] = rotated(kn, k_sw, rk, gk_ref).astype(ko_o.dtype)


def diff_prep(p, lp, tabs):
    tm = DIFF_TM
    partner = _rope_partner(DH_D, (0, DH_D))
    perm = np.zeros((LANES, LANES), np.float32)
    perm[partner, np.arange(LANES)] = 1.0
    gain2 = lambda g: jnp.stack([jnp.concatenate([g, g]), jnp.concatenate([g, g])[partner]])
    full = lambda shape: pl.BlockSpec(shape, lambda i: (0, 0))
    tab = pl.BlockSpec((tm, LANES), lambda i: (_pos_block_of_tile(i, tm), 0))
    row = pl.BlockSpec((tm, W_BRANCH), lambda i: (i, 0))
    dq_b, dk_b = _seg_block('dq'), _seg_block('dk')
    return pl.pallas_call(
        _diff_prep_kernel,
        grid=(R_ALL // tm,),
        in_specs=[pl.BlockSpec((tm, W_BRANCH), lambda i: (i, dq_b)), pl.BlockSpec((tm, W_BRANCH), lambda i: (i, dk_b)),
                  full((2, LANES)), full((2, LANES)), full((LANES, LANES)), tab, tab],
        out_specs=(row, row, row),
        out_shape=(jax.ShapeDtypeStruct((R_ALL, W_BRANCH), BF16), jax.ShapeDtypeStruct((R_ALL, W_BRANCH), BF16),
                   jax.ShapeDtypeStruct((R_ALL, W_BRANCH), F32)),
        compiler_params=_cparams(("arbitrary",)),
        name="diff_prep",
    )(p, p, gain2(lp['g_diff_qn']), gain2(lp['g_diff_kn']), jnp.asarray(perm, BF16), *tabs)


def _diff_attn_kernel(*refs, has_cache, lambda_init):
    if has_cache:
        dl_ref, q_ref, k_ref, v_ref, kc_ref, vc_ref, g_ref, o_ref = refs
    else:
        dl_ref, q_ref, k_ref, v_ref, g_ref, o_ref = refs
    lane = lax.broadcasted_iota(jnp.int32, (1, LANES), 1)
    dl = dl_ref[...]
    lam = (jnp.exp(jnp.sum(dl[0:1] * dl[1:2], axis=-1, keepdims=True))
           - jnp.exp(jnp.sum(dl[2:3] * dl[3:4], axis=-1, keepdims=True)) + lambda_init)
    units = [(h, w) for h in range(H_D) for w in range(2)]

    def scores(unit):
        h, w = unit
        sl = slice((2 * w + h // 2) * LANES, (2 * w + h // 2 + 1) * LANES)
        q = jnp.where((lane // DH_D) == (h % 2), q_ref[:, sl], jnp.zeros((), q_ref.dtype))
        s_o = lax.dot_general(q, k_ref[:, sl], _NT, preferred_element_type=F32)
        s_c = (lax.dot_general(q, kc_ref[:, sl].astype(BF16), _NT, preferred_element_type=F32)
               if has_cache else None)
        return s_o, s_c

    outs = []
    nxt = scores(units[0])
    for n, (h, w) in enumerate(units):
        s_o, s_c = nxt
        if n + 1 < len(units):
            nxt = scores(units[n + 1])
        vs = slice(h * LANES, (h + 1) * LANES)
        m = jnp.max(s_o, axis=-1, keepdims=True)
        if has_cache:
            m = jnp.maximum(m, jnp.max(s_c, axis=-1, keepdims=True))
        e_o = jnp.exp(s_o - m)
        l = jnp.sum(e_o, axis=-1, keepdims=True)
        pv = jnp.dot(e_o.astype(BF16), v_ref[:, vs], preferred_element_type=F32)
        if has_cache:
            e_c = jnp.exp(s_c - m)
            l = l + jnp.sum(e_c, axis=-1, keepdims=True)
            pv = pv + jnp.dot(e_c.astype(BF16), vc_ref[:, vs].astype(BF16), preferred_element_type=F32)
        outs.append(pv * (1.0 / l))
    for h in range(H_D):
        y = outs[2 * h] - lam * outs[2 * h + 1]
        r = lax.rsqrt(jnp.mean(y * y, axis=-1, keepdims=True) + EPS)
        o_ref[:, h * LANES:(h + 1) * LANES] = ((y * r * g_ref[...]) * (1.0 - lambda_init)).astype(o_ref.dtype)


def diff_attention(dl, qd, kd, p, kc, vc, g_sub, *, row0, n_seq, t_seq, tq, lambda_init):
    has_cache = kc is not None
    nq = t_seq // tq
    qb0, kb0 = row0 // tq, row0 // t_seq
    in_specs = [pl.BlockSpec((4, DH_D), lambda s, i: (0, 0)),
                pl.BlockSpec((tq, W_BRANCH), lambda s, i: (qb0 + s * nq + i, 0)),
                pl.BlockSpec((t_seq, W_BRANCH), lambda s, i: (kb0 + s, 0)),
                pl.BlockSpec((t_seq, W_BRANCH), lambda s, i: (kb0 + s, _seg_block('dv')))]
    args = [dl, qd, kd, p]
    if has_cache:
        in_specs += [pl.BlockSpec((PAST_LEN, W_BRANCH), lambda s, i: (s, 0)),
                     pl.BlockSpec((PAST_LEN, W_BRANCH), lambda s, i: (s, 0))]
        args += [kc, vc]
    in_specs.append(pl.BlockSpec((1, LANES), lambda s, i: (0, 0)))
    args.append(g_sub.reshape(1, LANES))
    return pl.pallas_call(
        functools.partial(_diff_attn_kernel, has_cache=has_cache, lambda_init=lambda_init),
        grid=(n_seq, nq),
        in_specs=in_specs,
        out_specs=pl.BlockSpec((tq, W_BRANCH), lambda s, i: (s * nq + i, 0)),
        out_shape=jax.ShapeDtypeStruct((n_seq * t_seq, W_BRANCH), BF16),
        compiler_params=_cparams(("arbitrary", "arbitrary")),
        name="diff_attention",
    )(*args)


def _softplus(z):
    return jnp.maximum(z, 0.0) + jnp.log(1.0 + jnp.exp(-jnp.abs(z)))


def _gelu_tanh(x):
    return 0.5 * x * (1.0 + jnp.tanh(math.sqrt(2.0 / math.pi) * (x + 0.044715 * (x * x * x))))


def _rglru_kernel(x_ref, gate_ref, wc_ref, bc_ref, wg_ref, bg_ref, lam_ref, h0_ref, y_ref, st_ref,
                  af_s, uf_s, ab_s, ub_s, *, t_seq):
    t = t_seq
    x = x_ref[...].astype(F32)
    row = lax.broadcasted_iota(jnp.int32, (t, W_B), 0)
    wc = wc_ref[...]
    xc = (wc[0:1] * jnp.where(row >= 2, pltpu.roll(x, 2, 0), 0.0)
          + wc[1:2] * jnp.where(row >= 1, pltpu.roll(x, 1, 0), 0.0)
          + wc[2:3] * x
          + wc[3:4] * jnp.where(row < t - 1, pltpu.roll(x, t - 1, 0), 0.0)
          + bc_ref[...])
    gates = jnp.dot(xc.astype(BF16), wg_ref[...], preferred_element_type=F32) + bg_ref[...]
    for d, (a_s, u_s) in enumerate(((af_s, uf_s), (ab_s, ub_s))):
        rg = jax.nn.sigmoid(gates[:, (2 * d) * W_B:(2 * d + 1) * W_B])
        ig = jax.nn.sigmoid(gates[:, (2 * d + 1) * W_B:(2 * d + 2) * W_B])
        log_a = -RG_C * rg * _softplus(-lam_ref[d:d + 1, :])
        a = jnp.exp(log_a)
        a_s[...] = a
        u_s[...] = jnp.sqrt(-jnp.tanh(log_a) * (a * a + 1.0)) * (ig * xc)

    nblk = t // SUBLANES

    def body(k, carry):
        hf, hb = carry
        base_f = pl.multiple_of(k * SUBLANES, SUBLANES)
        base_b = pl.multiple_of((nblk - 1 - k) * SUBLANES, SUBLANES)
        for r in range(SUBLANES):
            rf = pl.ds(base_f + r, 1)
            rb = pl.ds(base_b + (SUBLANES - 1 - r), 1)
            hf = af_s[rf, :] * hf + uf_s[rf, :]
            hb = ab_s[rb, :] * hb + ub_s[rb, :]
            uf_s[rf, :] = hf
            ub_s[rb, :] = hb
        return hf, hb

    hf, hb = lax.fori_loop(0, nblk, body, (h0_ref[0, 0:1, :], h0_ref[0, 1:2, :]))
    st_ref[0, 0:1, :] = hf
    st_ref[0, 1:2, :] = hb
    y_ref[...] = (_gelu_tanh(gate_ref[...].astype(F32)) * (uf_s[...] + ub_s[...])).astype(y_ref.dtype)


def pack_rglru_weights(lp):
    def blockdiag(w):
        eye = jnp.eye(NB_B, dtype=w.dtype)
        return jnp.einsum('ncd,nm->ncmd', w, eye).reshape(W_B, W_B)
    wg = jnp.concatenate([blockdiag(lp['w_rg_a'][0]), blockdiag(lp['w_rg_x'][0]),
                          blockdiag(lp['w_rg_a'][1]), blockdiag(lp['w_rg_x'][1])], axis=1).astype(BF16)
    bg = jnp.concatenate([lp['b_rg_a'][0], lp['b_rg_x'][0], lp['b_rg_a'][1], lp['b_rg_x'][1]]).reshape(1, 4 * W_B)
    return dict(wg=wg, bg=bg, wc=lp['w_conv_rg'], bc=lp['b_conv_rg'].reshape(1, W_B), lam=lp['rg_lambda'])


def rglru(p, rw, h0, *, row0, n_seq, t_seq):
    rb0 = row0 // t_seq
    xb, gb = _seg_block('rg_x'), _seg_block('rg_gate')
    full = lambda shape: pl.BlockSpec(shape, lambda s: tuple(0 for _ in shape))
    in_specs = [pl.BlockSpec((t_seq, W_B), lambda s: (rb0 + s, xb)),
                pl.BlockSpec((t_seq, W_B), lambda s: (rb0 + s, gb)),
                full((CONV_W, W_B)), full((1, W_B)), full((W_B, 4 * W_B)), full((1, 4 * W_B)), full((2, W_B)),
                pl.BlockSpec((1, 2, W_B), lambda s: (s, 0, 0))]
    args = [p, p, rw['wc'], rw['bc'], rw['wg'], rw['bg'], rw['lam'], h0]
    return pl.pallas_call(
        functools.partial(_rglru_kernel, t_seq=t_seq),
        grid=(n_seq,),
        in_specs=in_specs,
        out_specs=(pl.BlockSpec((t_seq, W_B), lambda s: (s, 0)), pl.BlockSpec((1, 2, W_B), lambda s: (s, 0, 0))),
        out_shape=(jax.ShapeDtypeStruct((n_seq * t_seq, W_B), BF16), jax.ShapeDtypeStruct((n_seq, 2, W_B), F32)),
        scratch_shapes=[pltpu.VMEM((t_seq, W_B), F32)] * 4,
        compiler_params=_cparams(("arbitrary",)),
        name="rglru",
    )(*args)


def _dot_split(a, b_bf16):
    hi = a.astype(BF16)
    lo = (a - hi.astype(F32)).astype(BF16)
    return (jnp.dot(hi, b_bf16, preferred_element_type=F32) + jnp.dot(lo, b_bf16, preferred_element_type=F32))


def _log_sigmoid(z):
    return jnp.minimum(z, 0.0) - jnp.log(1.0 + jnp.exp(-jnp.abs(z)))


_TN = (((0,), (0,)), ((), ()))


def _mlstm_kernel(q_ref, k_ref, v_ref, o_ref, sm_ref, bias_ref, g_ref, c0_ref, n0_ref, m0_ref,
                  y_ref, c_out, n_out, m_out, hm_s, c_s, *, t_seq):
    L = ML_CHUNK
    nchunk = t_seq // L
    scale = DH_C ** -0.5
    ri = lax.broadcasted_iota(jnp.int32, (L, L), 0)
    ci = lax.broadcasted_iota(jnp.int32, (L, L), 1)
    lane1 = lax.broadcasted_iota(jnp.int32, (L, LANES), 1)
    ones_col = jnp.where(lane1 == 0, 1.0, 0.0).astype(BF16)
    bias = bias_ref[...]

    for d in range(2):
        causal = (ci <= ri) if d == 0 else (ci >= ri)
        tri = jnp.where(causal, 1.0, 0.0).astype(BF16)
        tri_t = jnp.where((ri <= ci) if d == 0 else (ri >= ci), 1.0, 0.0).astype(BF16)
        for h in range(H_C):
            c_s[h, :, 0:DH_C] = c0_ref[0, d, h]
            c_s[h, :, DH_C:2 * DH_C] = jnp.where(lane1 == 0, n0_ref[0, d, h], 0.0)
        m_init = tuple(m0_ref[0, :, d * H_C + h:d * H_C + h + 1] for h in range(H_C))

        def chunk(kk, ms, d=d, causal=causal, tri=tri, tri_t=tri_t):
            cidx = kk if d == 0 else nchunk - 1 - kk
            rows = pl.ds(pl.multiple_of(cidx * L, L), L)
            gsm = sm_ref[rows, :].astype(F32) + bias
            lf_all = _log_sigmoid(gsm)
            cum_cols = _dot_split_left(tri, lf_all)
            g_t = gsm.T
            cum_rows = _dot_split(lf_all.T, tri_t)
            heads = range(H_C)
            sls = [slice(h * DH_C, (h + 1) * DH_C) for h in heads]
            qs = [q_ref[rows, sl] for sl in sls]
            ks = [k_ref[rows, sl] for sl in sls]
            v_augs = [jnp.concatenate([v_ref[rows, sl], ones_col], axis=1) for sl in sls]
            qk = [lax.dot_general(qs[h], ks[h], _NT, preferred_element_type=F32) for h in heads]
            qc = [jnp.dot(qs[h], c_s[h].astype(BF16), preferred_element_type=F32) * scale for h in heads]
            cum_c = [cum_cols[:, SM_MF + d * H_C + h:SM_MF + d * H_C + h + 1] for h in heads]
            li_c = [gsm[:, SM_MI + d * H_C + h:SM_MI + d * H_C + h + 1] for h in heads]
            m_row, s = [], []
            for h in heads:
                jl, jf = SM_MI + d * H_C + h, SM_MF + d * H_C + h
                log_d = jnp.where(causal, cum_c[h] - cum_rows[jf:jf + 1, :] + g_t[jl:jl + 1, :], -jnp.inf)
                m_row.append(jnp.maximum(cum_c[h] + ms[h], jnp.max(log_d, axis=-1, keepdims=True)))
                s.append(qk[h] * (scale * jnp.exp(log_d - m_row[h])))
            sv = [jnp.dot(s[h].astype(BF16), v_augs[h], preferred_element_type=F32) for h in heads]
            new_ms = []
            for h in heads:
                w_inter = jnp.exp(cum_c[h] + ms[h] - m_row[h])
                nd = sv[h] + qc[h] * w_inter
                den = jnp.maximum(jnp.abs(nd[:, DH_C:DH_C + 1]), jnp.exp(-m_row[h]))
                h_out = nd[:, :DH_C] * (1.0 / den)
                if d == 0:
                    hm_s[rows, sls[h]] = h_out
                else:
                    hm_s[rows, sls[h]] = hm_s[rows, sls[h]] + h_out
                last = cum_c[h][L - 1:L, :] if d == 0 else cum_c[h][0:1, :]
                w_s = last - cum_c[h] + li_c[h]
                m_new = jnp.maximum(last + ms[h], jnp.max(w_s, axis=0, keepdims=True))
                decay = jnp.exp(last + ms[h] - m_new)
                kw_t = (ks[h].astype(F32) * jnp.exp(w_s - m_new)).T.astype(BF16)
                c_s[h] = decay * c_s[h] + jnp.dot(kw_t, v_augs[h], preferred_element_type=F32)
                new_ms.append(m_new)
            return tuple(new_ms)

        m_fin = lax.fori_loop(0, nchunk, chunk, m_init)
        for h in range(H_C):
            c_out[0, d, h] = c_s[h, :, 0:DH_C]
            n_out[0, d, h] = c_s[h, :, DH_C:DH_C + 1]
            m_out[0, :, d * H_C + h:d * H_C + h + 1] = m_fin[h]

    for h in range(H_C):
        sl = slice(h * DH_C, (h + 1) * DH_C)
        hm = hm_s[:, sl]
        r = lax.rsqrt(jnp.mean(hm * hm, axis=-1, keepdims=True) + EPS)
        y_ref[:, sl] = (jax.nn.sigmoid(o_ref[:, sl].astype(F32)) * (hm * r * g_ref[...])).astype(y_ref.dtype)


def _dot_split_left(a_bf16, b):
    hi = b.astype(BF16)
    lo = (b - hi.astype(F32)).astype(BF16)
    return (jnp.dot(a_bf16, hi, preferred_element_type=F32) + jnp.dot(a_bf16, lo, preferred_element_type=F32))


def mlstm(p, bias_sm, g_out, c0, n0, m0, *, row0, n_seq, t_seq):
    rb0 = row0 // t_seq
    seg = lambda nm: pl.BlockSpec((t_seq, W_BRANCH), lambda s, b=_seg_block(nm): (rb0 + s, b))
    full = lambda shape: pl.BlockSpec(shape, lambda s: tuple(0 for _ in shape))
    c_spec = pl.BlockSpec((1, 2, H_C, DH_C, DH_C), lambda s: (s, 0, 0, 0, 0))
    n_spec = pl.BlockSpec((1, 2, H_C, DH_C, 1), lambda s: (s, 0, 0, 0, 0))
    m_spec = pl.BlockSpec((1, 1, 2 * H_C), lambda s: (s, 0, 0))
    in_specs = [seg('mq'), seg('mk'), seg('mv'), seg('mo'),
                pl.BlockSpec((t_seq, LANES), lambda s: (rb0 + s, P_SMALL // LANES)),
                full((1, LANES)), full((1, DH_C)), c_spec, n_spec, m_spec]
    args = [p, p, p, p, p, bias_sm, g_out.reshape(1, DH_C), c0, n0, m0]
    return pl.pallas_call(
        functools.partial(_mlstm_kernel, t_seq=t_seq),
        grid=(n_seq,),
        in_specs=in_specs,
        out_specs=(pl.BlockSpec((t_seq, W_BRANCH), lambda s: (s, 0)), c_spec, n_spec, m_spec),
        out_shape=(jax.ShapeDtypeStruct((n_seq * t_seq, W_BRANCH), BF16),
                   jax.ShapeDtypeStruct((n_seq, 2, H_C, DH_C, DH_C), F32),
                   jax.ShapeDtypeStruct((n_seq, 2, H_C, DH_C, 1), F32),
                   jax.ShapeDtypeStruct((n_seq, 1, 2 * H_C), F32)),
        scratch_shapes=[pltpu.VMEM((t_seq, W_BRANCH), F32), pltpu.VMEM((H_C, DH_C, 2 * DH_C), F32)],
        compiler_params=_cparams(("arbitrary",)),
        name="mlstm",
    )(*args)


MERGE_TM = 512


def _merge_kernel(*refs):
    ctx_refs, lat_refs, gate_refs = refs[0:4], refs[4:8], refs[8:12]
    x_ref, mod_ref, wbr_ref, wout_ref, gn_ref, xo_ref, xn_ref = refs[12:19]
    xn3_ref = refs[19] if len(refs) > 19 else None
    is_ctx = pl.program_id(0) < R_CTX // MERGE_TM
    merged = None
    for g in range(N_BRANCH):
        yg = jnp.where(is_ctx, ctx_refs[g][...], lat_refs[g][...])
        pg = jnp.dot(yg, wbr_ref[g], preferred_element_type=F32)
        term = jax.nn.sigmoid(gate_refs[g][...].astype(F32)) * pg
        merged = term if merged is None else merged + term
    y = jnp.dot(merged.astype(BF16), wout_ref[...], preferred_element_type=F32)
    x = x_ref[...] + mod_ref[0, 2:3, :] * y
    xo_ref[...] = x
    r = lax.rsqrt(jnp.mean(x * x, axis=-1, keepdims=True) + EPS)
    xn = (x * r * gn_ref[...]) * (1.0 + mod_ref[0, 4:5, :]) + mod_ref[0, 3:4, :]
    xn_ref[...] = xn.astype(xn_ref.dtype)
    if xn3_ref is not None:
        _rows_to_tiles(xn3_ref, xn.astype(BF16))


def merge(ys_ctx, ys_lat, p, x, mod, wbr, wout, g_ffn, rows_as_tiles):
    tm = MERGE_TM
    n_ctx_tiles = R_CTX // tm
    br_ctx = pl.BlockSpec((tm, W_BRANCH), lambda i: (jnp.minimum(i, n_ctx_tiles - 1), 0))
    br_lat = pl.BlockSpec((tm, W_BRANCH), lambda i: (jnp.maximum(i - n_ctx_tiles, 0), 0))
    gate = lambda g: pl.BlockSpec((tm, D_MODEL), lambda i, g=g: (i, g))
    row = pl.BlockSpec((tm, D_MODEL), lambda i: (i, 0))
    out_specs = [row, row]
    out_shape = [jax.ShapeDtypeStruct((R_ALL, D_MODEL), F32), jax.ShapeDtypeStruct((R_ALL, D_MODEL), BF16)]
    if rows_as_tiles:
        out_specs.append(pl.BlockSpec((tm, ROW_SUB, LANES), lambda i: (i, 0, 0)))
        out_shape.append(jax.ShapeDtypeStruct((R_ALL, ROW_SUB, LANES), F32))
    return pl.pallas_call(
        _merge_kernel,
        grid=(R_ALL // tm,),
        in_specs=[br_ctx] * N_BRANCH + [br_lat] * N_BRANCH + [gate(0), gate(1), gate(2), gate(3), row,
                  pl.BlockSpec((1, 8, D_MODEL), lambda i: (_mod_row_of_tile(i, tm), 0, 0)),
                  pl.BlockSpec((N_BRANCH, W_BRANCH, D_MODEL), lambda i: (0, 0, 0)),
                  pl.BlockSpec((D_MODEL, D_MODEL), lambda i: (0, 0)),
                  pl.BlockSpec((1, D_MODEL), lambda i: (0, 0))],
        out_specs=tuple(out_specs),
        out_shape=tuple(out_shape),
        compiler_params=_cparams(("arbitrary",)),
        name="merge",
    )(*ys_ctx, *ys_lat, p, p, p, p, x, mod, wbr, wout, g_ffn.reshape(1, D_MODEL))


def _new_expert(te_ref, i):
    return jnp.logical_or(i == 0, te_ref[i] != te_ref[jnp.maximum(i - 1, 0)])


def _ffn_up_kernel(te_ref, nt_ref, x_ref, wg_ref, wu_ref, h_ref, wgb_ref, wub_ref):
    i = pl.program_id(1)

    @pl.when(_new_expert(te_ref, i))
    def _():
        wgb_ref[...] = wg_ref[0].astype(BF16)
        wub_ref[...] = wu_ref[0].astype(BF16)

    @pl.when(i < nt_ref[0])
    def _():
        x = x_ref[...]
        g = jnp.dot(x, wgb_ref[...], preferred_element_type=F32)
        u = jnp.dot(x, wub_ref[...], preferred_element_type=F32)
        h_ref[...] = (g * jax.nn.sigmoid(g) * u).astype(h_ref.dtype)

    @pl.when(i >= nt_ref[0])
    def _():
        h_ref[...] = jnp.zeros(h_ref.shape, h_ref.dtype)


def ffn_up(tile_expert, n_tiles, xs, wg, wu, tm, tf, weight_buffers):
    r, d = xs.shape
    f = wg.shape[2]
    w_spec = pl.BlockSpec((1, d, tf), lambda j, i, te, nt: (te[i], 0, j), pipeline_mode=pl.Buffered(weight_buffers))
    return pl.pallas_call(
        _ffn_up_kernel,
        grid_spec=pltpu.PrefetchScalarGridSpec(
            num_scalar_prefetch=2,
            grid=(f // tf, r // tm),
            in_specs=[pl.BlockSpec((tm, d), lambda j, i, te, nt: (i, 0)), w_spec, w_spec],
            out_specs=pl.BlockSpec((tm, tf), lambda j, i, te, nt: (i, j)),
            scratch_shapes=[pltpu.VMEM((d, tf), BF16), pltpu.VMEM((d, tf), BF16)]),
        out_shape=jax.ShapeDtypeStruct((r, f), BF16),
        compiler_params=_cparams(("arbitrary", "arbitrary")),
        name="ffn_up",
    )(tile_expert, n_tiles, xs, wg, wu)


def _ffn_down_kernel(te_ref, nt_ref, h_ref, wd_ref, y_ref, wdb_ref):
    i = pl.program_id(0)

    @pl.when(_new_expert(te_ref, i))
    def _():
        wdb_ref[...] = wd_ref[0].astype(BF16)

    @pl.when(i < nt_ref[0])
    def _():
        _rows_to_tiles(y_ref, jnp.dot(h_ref[...], wdb_ref[...], preferred_element_type=F32))

    @pl.when(i >= nt_ref[0])
    def _():
        y_ref[...] = jnp.zeros(y_ref.shape, y_ref.dtype)


def ffn_down(tile_expert, n_tiles, h, wd, tm):
    r, f = h.shape
    d = wd.shape[2]
    return pl.pallas_call(
        _ffn_down_kernel,
        grid_spec=pltpu.PrefetchScalarGridSpec(
            num_scalar_prefetch=2,
            grid=(r // tm,),
            in_specs=[pl.BlockSpec((tm, f), lambda i, te, nt: (i, 0)),
                      pl.BlockSpec((1, f, d), lambda i, te, nt: (te[i], 0, 0))],
            out_specs=pl.BlockSpec((tm, ROW_SUB, LANES), lambda i, te, nt: (i, 0, 0)),
            scratch_shapes=[pltpu.VMEM((f, d), BF16)]),
        out_shape=jax.ShapeDtypeStruct((r, ROW_SUB, LANES), F32),
        compiler_params=_cparams(("arbitrary",)),
        name="ffn_down",
    )(tile_expert, n_tiles, h, wd)


def _ffn_down_res_kernel(h_ref, wd_ref, x_ref, mod_ref, y_ref, wdb_ref):
    @pl.when(pl.program_id(0) == 0)
    def _():
        wdb_ref[...] = wd_ref[...].astype(BF16)

    y = jnp.dot(h_ref[...], wdb_ref[...], preferred_element_type=F32)
    y_ref[...] = x_ref[...] + mod_ref[0, 5:6, :] * y


def ffn_down_residual(h, wd, x, mod):
    tm = 1024
    r, f = h.shape
    d = wd.shape[1]
    return pl.pallas_call(
        _ffn_down_res_kernel,
        grid=(r // tm,),
        in_specs=[pl.BlockSpec((tm, f), lambda i: (i, 0)),
                  pl.BlockSpec((f, d), lambda i: (0, 0), pipeline_mode=pl.Buffered(1)),
                  pl.BlockSpec((tm, d), lambda i: (i, 0)),
                  pl.BlockSpec((1, 8, d), lambda i: (_mod_row_of_tile(i, tm), 0, 0))],
        out_specs=pl.BlockSpec((tm, d), lambda i: (i, 0)),
        out_shape=jax.ShapeDtypeStruct((r, d), F32),
        scratch_shapes=[pltpu.VMEM((f, d), BF16)],
        compiler_params=_cparams(("arbitrary",)),
        name="ffn_down_residual",
    )(h, wd, x, mod)


def dense_swiglu_residual(xn, x, mod, wg, wu, wd):
    t = xn.shape[0]
    tm = 1024
    n_tiles = t // tm
    te = jnp.zeros((n_tiles,), jnp.int32)
    nt = jnp.full((1,), n_tiles, jnp.int32)
    h = ffn_up(te, nt, xn, wg[None], wu[None], tm=tm, tf=wg.shape[1] // 2, weight_buffers=1)
    return ffn_down_residual(h, wd, x, mod)


MOE_TM = 512
DISPATCH_TM = 512


def _moe_routing(logits, tm):
    t = logits.shape[0]
    n_assign = t * TOP_K
    top_v, top_i = lax.top_k(logits, TOP_K)
    gate = jax.nn.softmax(top_v, axis=-1)
    flat_e = top_i.reshape(-1).astype(jnp.int32)
    onehot = (flat_e[:, None] == jnp.arange(N_EXP, dtype=jnp.int32)[None, :])
    blk = LANES
    oh = onehot.astype(F32).reshape(n_assign // blk, blk, N_EXP)
    tril = jnp.tril(jnp.ones((blk, blk), F32))
    within = jnp.einsum('ij,bjk->bik', tril, oh)
    blk_tot = within[:, -1, :]
    blk_off = jnp.cumsum(blk_tot, axis=0) - blk_tot
    csum = (within + blk_off[:, None, :]).reshape(n_assign, N_EXP)
    rank = jnp.sum(jnp.where(onehot, csum - 1.0, 0.0), axis=1).astype(jnp.int32)
    counts = csum[-1].astype(jnp.int32)
    padded = (counts + tm - 1) // tm * tm
    grp_start = jnp.cumsum(padded) - padded
    raw_start = jnp.cumsum(counts) - counts
    slot_of_assign = jnp.sum(jnp.where(onehot, grp_start[None, :], 0), axis=1) + rank

    r_max = n_assign + N_EXP * tm
    tile_start = jnp.arange(r_max // tm, dtype=jnp.int32) * tm
    tile_expert = jnp.sum((tile_start[:, None] >= (grp_start + padded)[None, :]).astype(jnp.int32), axis=1)
    tile_expert = jnp.minimum(tile_expert, N_EXP - 1).astype(jnp.int32)
    n_tiles = (jnp.sum(padded) // tm).astype(jnp.int32).reshape(1)

    order = jnp.argsort(flat_e, stable=True).astype(jnp.int32)
    e_slot = jnp.repeat(tile_expert, tm)
    j = jnp.arange(r_max, dtype=jnp.int32) - grp_start[e_slot]
    src = jnp.clip(raw_start[e_slot] + j, 0, n_assign - 1)
    tok_of_slot = jnp.where(j < counts[e_slot], order[src] // TOP_K, 0)
    return gate, slot_of_assign, tok_of_slot, tile_expert, n_tiles


ROW_SUB = D_MODEL // LANES


def _rows_to_tiles(o3_ref, x):
    for j in range(ROW_SUB):
        o3_ref[:, j, :] = x[:, j * LANES:(j + 1) * LANES].astype(o3_ref.dtype)


def _start_row_gather(idx_ref, src_ref, dst, sem, n_rows):
    def body(q, carry):
        for u in range(2):
            r = 2 * q + u
            pltpu.make_async_copy(src_ref.at[idx_ref[0, 0, r]], dst.at[r], sem).start(priority=u)
        return carry

    lax.fori_loop(0, n_rows // 2, body, 0, unroll=4)


def _wait_row_gather(src_ref, dst, sem, n_rows):
    pltpu.make_async_copy(src_ref.at[pl.ds(0, n_rows)], dst, sem).wait()


def _tiles_to_rows(tiles, rows_ref):
    for j in range(ROW_SUB):
        rows_ref[:, j * LANES:(j + 1) * LANES] = tiles[:, j, :]


def _dispatch_kernel(nt_ref, idx_ref, idx_next_ref, src_ref, o_ref, buf, rows, sem):
    i = pl.program_id(0)
    nt = nt_ref[0]
    tm = o_ref.shape[0]
    slot = i % 2

    @pl.when(i == 0)
    def _():
        _start_row_gather(idx_ref, src_ref, buf.at[0], sem.at[0], tm)

    @pl.when(i + 1 < nt)
    def _():
        _start_row_gather(idx_next_ref, src_ref, buf.at[1 - slot], sem.at[1 - slot], tm)

    @pl.when(i < nt)
    def _():
        _wait_row_gather(src_ref, buf.at[slot], sem.at[slot], tm)
        _tiles_to_rows(buf.at[slot], rows)
        o_ref[...] = rows[...].astype(o_ref.dtype)

    @pl.when(i >= nt)
    def _():
        o_ref[...] = jnp.zeros(o_ref.shape, o_ref.dtype)


def moe_dispatch(n_tiles, tok_of_slot, xn3, tm):
    r = tok_of_slot.shape[0]
    last = r // tm - 1
    idx = tok_of_slot.reshape(r // tm, 1, tm)
    return pl.pallas_call(
        _dispatch_kernel,
        grid_spec=pltpu.PrefetchScalarGridSpec(
            num_scalar_prefetch=1,
            grid=(r // tm,),
            in_specs=[pl.BlockSpec((1, 1, tm), lambda i, nt: (i, 0, 0), memory_space=pltpu.SMEM),
                      pl.BlockSpec((1, 1, tm), lambda i, nt: (jnp.minimum(i + 1, last), 0, 0),
                                   memory_space=pltpu.SMEM),
                      pl.BlockSpec(memory_space=pl.ANY)],
            out_specs=pl.BlockSpec((tm, D_MODEL), lambda i, nt: (i, 0)),
            scratch_shapes=[pltpu.VMEM((2, tm, ROW_SUB, LANES), F32), pltpu.VMEM((tm, D_MODEL), F32),
                            pltpu.SemaphoreType.DMA((2,))]),
        out_shape=jax.ShapeDtypeStruct((r, D_MODEL), BF16),
        compiler_params=_cparams(("arbitrary",)),
        name="moe_dispatch",
    )(n_tiles, idx, idx, xn3)


COMBINE_TM = 256


def _combine_kernel(idx_ref, idx_next_ref, ys_ref, x_ref, gate_ref, mod_ref, oc_ref, ol_ref, buf, rows, sem):
    i = pl.program_id(0)
    tm = oc_ref.shape[0]
    n_rows = TOP_K * tm
    slot = i % 2

    @pl.when(i == 0)
    def _():
        _start_row_gather(idx_ref, ys_ref, buf.at[0], sem.at[0], n_rows)

    @pl.when(i + 1 < pl.num_programs(0))
    def _():
        _start_row_gather(idx_next_ref, ys_ref, buf.at[1 - slot], sem.at[1 - slot], n_rows)

    _wait_row_gather(ys_ref, buf.at[slot], sem.at[slot], n_rows)
    _tiles_to_rows(buf.at[slot], rows)
    f = gate_ref[:, 0:1] * rows[0:tm, :] + gate_ref[:, 1:2] * rows[tm:2 * tm, :]
    res = x_ref[...] + mod_ref[0, 5:6, :] * f

    @pl.when(i < R_CTX // COMBINE_TM)
    def _():
        oc_ref[...] = res

    @pl.when(i >= R_CTX // COMBINE_TM)
    def _():
        ol_ref[...] = res


def moe_combine(slot_of_assign, ys3, x, gate, mod):
    t, d = x.shape
    tm = COMBINE_TM
    idx = slot_of_assign.reshape(t // tm, tm, TOP_K).transpose(0, 2, 1).reshape(t // tm, 1, TOP_K * tm)
    last = t // tm - 1
    n_ctx = R_CTX // tm
    return pl.pallas_call(
        _combine_kernel,
        grid=(t // tm,),
        in_specs=[pl.BlockSpec((1, 1, TOP_K * tm), lambda i: (i, 0, 0), memory_space=pltpu.SMEM),
                  pl.BlockSpec((1, 1, TOP_K * tm), lambda i: (jnp.minimum(i + 1, last), 0, 0),
                               memory_space=pltpu.SMEM),
                  pl.BlockSpec(memory_space=pl.ANY),
                  pl.BlockSpec((tm, d), lambda i: (i, 0)),
                  pl.BlockSpec((tm, TOP_K), lambda i: (i, 0)),
                  pl.BlockSpec((1, 8, d), lambda i: (_mod_row_of_tile(i, tm), 0, 0))],
        out_specs=(pl.BlockSpec((tm, d), lambda i: (jnp.minimum(i, n_ctx - 1), 0)),
                   pl.BlockSpec((tm, d), lambda i: (jnp.maximum(i - n_ctx, 0), 0))),
        out_shape=(jax.ShapeDtypeStruct((R_CTX, d), F32), jax.ShapeDtypeStruct((t - R_CTX, d), F32)),
        scratch_shapes=[pltpu.VMEM((2, TOP_K * tm, ROW_SUB, LANES), F32), pltpu.VMEM((TOP_K * tm, d), F32),
                        pltpu.SemaphoreType.DMA((2,))],
        compiler_params=_cparams(("arbitrary",)),
        name="moe_combine",
    )(idx, idx, ys3, x, gate, mod)


def moe_swiglu_residual(xn3, xn, x, mod, w_router, wg, wu, wd):
    tm = MOE_TM
    logits = jnp.dot(xn.astype(F32), w_router, precision=lax.Precision.HIGHEST)
    gate, slot_of_assign, tok_of_slot, tile_expert, n_tiles = _moe_routing(logits, tm)
    xs = moe_dispatch(n_tiles * (tm // DISPATCH_TM), tok_of_slot, xn3, DISPATCH_TM)
    h = ffn_up(tile_expert, n_tiles, xs, wg, wu, tm=tm, tf=wg.shape[2] // 2, weight_buffers=2)
    ys3 = ffn_down(tile_expert, n_tiles, h, wd, tm=tm)
    return moe_combine(slot_of_assign, ys3, x, gate, mod)


def _layer(x, cond, lp, l, ctx, tabs_a, tabs_d):
    mod = modulation(cond, lp['w_mod'], lp['b_mod']).reshape(cond.shape[0], 6, D_MODEL)
    mod = jnp.pad(mod, ((0, 0), (0, 2), (0, 0)))
    p = in_proj(x, mod, lp['g_norm_mix'], pack_w_in(lp['w_in']))

    mw = pack_mla_weights(lp)
    q_a, k_a, v_a, ckv, kr = mla_prep(p, mw, tabs_a)
    kr_c = jnp.pad(ctx['mla_krope'].reshape(-1, ROPE_A), ((0, 0), (0, LANES - ROPE_A)))
    kc_a, vc_a = mla_prep_cache(ctx['mla_ckv'].reshape(-1, KV_RANK), kr_c, mw)
    ya_c = mla_attention(q_a, k_a, v_a, None, None, row0=0, n_seq=N_CTX_SEQ, t_seq=T_CTX, tq=T_CTX)
    ya_l = mla_attention(q_a, k_a, v_a, kc_a, vc_a, row0=R_CTX, n_seq=N_LAT_SEQ, t_seq=T_LAT, tq=ATTN_TQ)

    rw = pack_rglru_weights(lp)
    yb_c, st_rg = rglru(p, rw, jnp.zeros((N_CTX_SEQ, 2, W_B), F32), row0=0, n_seq=N_CTX_SEQ, t_seq=T_CTX)
    yb_l, _ = rglru(p, rw, ctx['rglru'], row0=R_CTX, n_seq=N_LAT_SEQ, t_seq=T_LAT)

    bias_sm = jnp.zeros((LANES,), F32).at[SM_MI:SM_MI + 2 * H_C].set(lp['b_ml_i'].reshape(-1))
    bias_sm = bias_sm.at[SM_MF:SM_MF + 2 * H_C].set(lp['b_ml_f'].reshape(-1)).reshape(1, LANES)
    c0_ctx = jnp.zeros((N_CTX_SEQ, 2, H_C, DH_C, DH_C), F32)
    n0_ctx = jnp.zeros((N_CTX_SEQ, 2, H_C, DH_C, 1), F32)
    m0_ctx = jnp.zeros((N_CTX_SEQ, 1, 2 * H_C), F32)
    m0_lat = ctx['mlstm_m'].reshape(N_LAT_SEQ, 1, 2 * H_C)
    yc_c, c_fin, n_fin, m_fin = mlstm(p, bias_sm, lp['g_ml_out'], c0_ctx, n0_ctx, m0_ctx,
                                      row0=0, n_seq=N_CTX_SEQ, t_seq=T_CTX)
    yc_l, _, _, _ = mlstm(p, bias_sm, lp['g_ml_out'], ctx['mlstm_C'], ctx['mlstm_n'][..., None], m0_lat,
                          row0=R_CTX, n_seq=N_LAT_SEQ, t_seq=T_LAT)

    lambda_init = 0.8 - 0.6 * math.exp(-0.3 * l)
    qd, kd_own, kd_plain = diff_prep(p, lp, tabs_d)
    dkc = ctx['diff_k'].reshape(-1, W_BRANCH)
    dvc = ctx['diff_v'].reshape(-1, W_BRANCH)
    yd_c = diff_attention(lp['diff_lambda'], qd, kd_own, p, None, None, lp['g_diff_sub'],
                          row0=0, n_seq=N_CTX_SEQ, t_seq=T_CTX, tq=T_CTX, lambda_init=lambda_init)
    yd_l = diff_attention(lp['diff_lambda'], qd, kd_own, p, dkc, dvc, lp['g_diff_sub'],
                          row0=R_CTX, n_seq=N_LAT_SEQ, t_seq=T_LAT, tq=ATTN_TQ, lambda_init=lambda_init)

    merged = merge((ya_c, yb_c, yc_c, yd_c), (ya_l, yb_l, yc_l, yd_l), p, x, mod, lp['w_br'].astype(BF16),
                   lp['w_out'].astype(BF16), lp['g_norm_ffn'], rows_as_tiles='moe' in lp)
    if 'ffn' in lp:
        x, xn = merged
        x = dense_swiglu_residual(xn, x, mod, *lp['ffn'])
    else:
        x, xn, xn3 = merged
        x = moe_swiglu_residual(xn3, xn, x, mod, *lp['moe'])

    dv0 = _seg_block('dv') * W_BRANCH
    ctx_out = (ckv[:R_CTX].reshape(N_CTX_SEQ, T_CTX, KV_RANK),
               kr[:R_CTX, :ROPE_A].reshape(N_CTX_SEQ, T_CTX, ROPE_A),
               kd_plain[:R_CTX].reshape(N_CTX_SEQ, T_CTX, 2, H_D, DH_D),
               p[:R_CTX, dv0:dv0 + W_BRANCH].astype(F32).reshape(N_CTX_SEQ, T_CTX, H_D, 2 * DH_D),
               st_rg,
               c_fin,
               n_fin.reshape(N_CTX_SEQ, 2, H_C, DH_C),
               m_fin.reshape(N_CTX_SEQ, 2, H_C))
    return x, ctx_out


def kernel(x_prompt, x_sample, cache_mla_ckv, cache_mla_krope, cache_diff_k, cache_diff_v,
           state_rglru, state_mlstm_C, state_mlstm_n, state_mlstm_m, c, c_ctx,
           w_mod, b_mod, g_norm_mix, g_norm_ffn, w_in, g_mla_qlat, w_mla_uq, g_mla_kvlat, w_mla_ukv,
           g_mla_qn, g_mla_kn, w_conv_rg, b_conv_rg, w_rg_a, b_rg_a, w_rg_x, b_rg_x, rg_lambda,
           b_ml_i, b_ml_f, g_ml_out, g_diff_qn, g_diff_kn, diff_lambda, g_diff_sub, w_br, w_out,
           w_ffn_gate, w_ffn_up, w_ffn_down, w_router, w_moe_gate, w_moe_up, w_moe_down):
    assert x_prompt.shape == (N_CTX_SEQ, T_CTX, D_MODEL) and x_sample.shape == (N_LAT_SEQ, T_LAT, D_MODEL)
    tabs_a = _rope_tables(ROPE_A, (NOPE_A,), MLA_TM)
    tabs_d = _rope_tables(DH_D, (0, DH_D), DIFF_TM)
    cond = jnp.concatenate([c_ctx.reshape(1, D_MODEL), c, jnp.zeros((16 - 1 - N_LAT_SEQ, D_MODEL), F32)], axis=0)
    x = jnp.concatenate([x_prompt.reshape(R_CTX, D_MODEL), x_sample.reshape(R_LAT, D_MODEL)], axis=0)
    new = []
    for l in range(DEPTH):
        lp = dict(w_mod=w_mod[l], b_mod=b_mod[l], g_norm_mix=g_norm_mix[l], g_norm_ffn=g_norm_ffn[l], w_in=w_in[l],
                  g_mla_qlat=g_mla_qlat[l], w_mla_uq=w_mla_uq[l], g_mla_kvlat=g_mla_kvlat[l], w_mla_ukv=w_mla_ukv[l],
                  g_mla_qn=g_mla_qn[l], g_mla_kn=g_mla_kn[l], w_conv_rg=w_conv_rg[l], b_conv_rg=b_conv_rg[l],
                  w_rg_a=w_rg_a[l], b_rg_a=b_rg_a[l], w_rg_x=w_rg_x[l], b_rg_x=b_rg_x[l], rg_lambda=rg_lambda[l],
                  b_ml_i=b_ml_i[l], b_ml_f=b_ml_f[l], g_ml_out=g_ml_out[l], g_diff_qn=g_diff_qn[l],
                  g_diff_kn=g_diff_kn[l], diff_lambda=diff_lambda[l], g_diff_sub=g_diff_sub[l],
                  w_br=w_br[l], w_out=w_out[l])
        if l % 2 == 0:
            lp['ffn'] = (w_ffn_gate[l // 2], w_ffn_up[l // 2], w_ffn_down[l // 2])
        else:
            lp['moe'] = (w_router[l // 2], w_moe_gate[l // 2], w_moe_up[l // 2], w_moe_down[l // 2])
        ctx_l = dict(mla_ckv=cache_mla_ckv[:, l], mla_krope=cache_mla_krope[:, l], diff_k=cache_diff_k[:, l],
                     diff_v=cache_diff_v[:, l], rglru=state_rglru[:, l], mlstm_C=state_mlstm_C[:, l],
                     mlstm_n=state_mlstm_n[:, l], mlstm_m=state_mlstm_m[:, l])
        if isinstance(x, tuple):
            x = jnp.concatenate(x, axis=0)
        x, st = _layer(x, cond, lp, l, ctx_l, tabs_a, tabs_d)
        new.append(st)
    outs = tuple(jnp.stack([s[i] for s in new], axis=1) for i in range(8))
    x_ctx, x_lat = x if isinstance(x, tuple) else (x[:R_CTX], x[R_CTX:])
    return (x_ctx.reshape(N_CTX_SEQ, T_CTX, D_MODEL), x_lat.reshape(N_LAT_SEQ, T_LAT, D_MODEL)) + outs
```

```python
import functools
import math

import jax
import jax.numpy as jnp
import numpy as np
from jax import lax
from jax.experimental import pallas as pl
from jax.experimental.pallas import tpu as pltpu

D_MODEL = 1024
DEPTH = 2
GRID_W = 64
ROPE_BASE = 10000.0
EPS = 1e-6
N_BRANCH = 4
W_BRANCH = D_MODEL // 2

H_A = 8
NOPE_A = 64
ROPE_A = 32
V_A = W_BRANCH // H_A
Q_RANK = D_MODEL // 4
KV_RANK = D_MODEL // 8
MLA_SCALE = (NOPE_A + ROPE_A) ** -0.5

W_B = W_BRANCH
NB_B = 8
BW_B = W_B // NB_B
CONV_W = 4
RG_C = 8.0

H_C = 4
DH_C = W_BRANCH // H_C
ML_CHUNK = 128

H_D = 4
DH_D = W_BRANCH // (2 * H_D)
DIFF_SCALE = DH_D ** -0.5

N_EXP = 8
TOP_K = 2

V7X_VMEM_LIMIT_BYTES = 56 * 1024 * 1024
LANES = 128
SUBLANES = 8

BF16 = jnp.bfloat16
F32 = jnp.float32

P_GATE = 0
P_SEG = N_BRANCH * D_MODEL
SEG_NAMES = ('rg_x', 'rg_gate', 'mq', 'mk', 'mv', 'mo', 'dq', 'dk', 'dv')
P_QLAT = P_SEG + 9 * W_BRANCH
P_KVLAT = P_QLAT + Q_RANK
P_SMALL = P_KVLAT + KV_RANK
P_WIDTH = P_SMALL + LANES
SM_MI = ROPE_A
SM_MF = ROPE_A + 2 * H_C


def _seg_block(name):
    return (P_SEG + SEG_NAMES.index(name) * W_BRANCH) // W_BRANCH


N_CTX_SEQ, T_CTX = 16, 256
N_LAT_SEQ, T_LAT = 8, 1024
PAST_LEN = 512
R_CTX = N_CTX_SEQ * T_CTX
R_LAT = N_LAT_SEQ * T_LAT
R_ALL = R_CTX + R_LAT


def _cparams(sem):
    return pltpu.CompilerParams(dimension_semantics=sem, vmem_limit_bytes=V7X_VMEM_LIMIT_BYTES)


def _mod_row_of_tile(i, tm):
    n_ctx_tiles = R_CTX // tm
    per_seq = T_LAT // tm
    return jnp.where(i < n_ctx_tiles, 0, 1 + (i - n_ctx_tiles) // per_seq)


def _pos_block_of_tile(i, tm):
    n_ctx_tiles = R_CTX // tm
    per_seq = T_LAT // tm
    return jnp.where(i < n_ctx_tiles, per_seq, (i - n_ctx_tiles) % per_seq)


def _mod_kernel(c_ref, w_ref, b_ref, o_ref):
    c = c_ref[...]
    s = (c * jax.nn.sigmoid(c)).astype(BF16)
    o_ref[...] = jnp.dot(s, w_ref[0].astype(BF16), preferred_element_type=F32) + b_ref[0]


def modulation(cond, w_mod, b_mod, l):
    m, d = cond.shape
    n = w_mod.shape[2]
    tn = 1536
    return pl.pallas_call(
        _mod_kernel,
        grid=(n // tn,),
        in_specs=[pl.BlockSpec((m, d), lambda j: (0, 0)),
                  pl.BlockSpec((1, d, tn), lambda j: (l, 0, j)),
                  pl.BlockSpec((1, 1, tn), lambda j: (l, 0, j))],
        out_specs=pl.BlockSpec((m, tn), lambda j: (0, j)),
        out_shape=jax.ShapeDtypeStruct((m, n), F32),
        compiler_params=_cparams(("arbitrary",)),
        name="modulation",
    )(cond, w_mod, b_mod.reshape(b_mod.shape[0], 1, n))


def _in_proj_kernel(x_ref, mod_ref, g_ref, w_ref, o_ref, xn_ref):
    @pl.when(pl.program_id(1) == 0)
    def _():
        x = x_ref[...]
        r = lax.rsqrt(jnp.mean(x * x, axis=-1, keepdims=True) + EPS)
        sh = mod_ref[0, 0:1, :]
        sc = mod_ref[0, 1:2, :]
        xn_ref[...] = ((x * r * g_ref[...]) * (1.0 + sc) + sh).astype(BF16)

    o_ref[...] = jnp.dot(xn_ref[...], w_ref[...], preferred_element_type=F32).astype(o_ref.dtype)


def in_proj(x, mod, g, w_p):
    tm, tn = 1024, P_WIDTH // 4
    m, d = x.shape
    n = w_p.shape[1]
    return pl.pallas_call(
        _in_proj_kernel,
        grid=(m // tm, n // tn),
        in_specs=[pl.BlockSpec((tm, d), lambda i, j: (i, 0)),
                  pl.BlockSpec((1, 8, d), lambda i, j: (_mod_row_of_tile(i, tm), 0, 0)),
                  pl.BlockSpec((1, d), lambda i, j: (0, 0)),
                  pl.BlockSpec((d, tn), lambda i, j: (0, j))],
        out_specs=pl.BlockSpec((tm, tn), lambda i, j: (i, j)),
        out_shape=jax.ShapeDtypeStruct((m, n), BF16),
        scratch_shapes=[pltpu.VMEM((tm, d), BF16)],
        compiler_params=_cparams(("arbitrary", "arbitrary")),
        name="in_proj",
    )(x, mod, g.reshape(1, d), w_p)


def pack_w_in(w_in):
    o_rg = Q_RANK + KV_RANK + ROPE_A
    o_mi = o_rg + 6 * W_BRANCH
    o_dq = o_mi + 4 * H_C
    o_gate = o_dq + 3 * W_BRANCH
    pad = jnp.zeros((w_in.shape[0], LANES - ROPE_A - 4 * H_C), w_in.dtype)
    parts = [w_in[:, o_gate:], w_in[:, o_rg:o_mi], w_in[:, o_dq:o_gate], w_in[:, :Q_RANK + KV_RANK],
             w_in[:, Q_RANK + KV_RANK:o_rg], w_in[:, o_mi:o_dq], pad]
    return jnp.concatenate(parts, axis=1).astype(BF16)


def _rope_partner(rot_dim, lane_starts):
    nf = rot_dim // 4
    partner = np.arange(LANES)
    for s0 in lane_starts:
        for a in range(2):
            lo = s0 + a * 2 * nf
            partner[lo:lo + nf] = np.arange(lo + nf, lo + 2 * nf)
            partner[lo + nf:lo + 2 * nf] = np.arange(lo, lo + nf)
    return partner


def _rope_tables(rot_dim, lane_starts, tm):
    rows = T_LAT // GRID_W
    r, c = np.meshgrid(np.arange(rows, dtype=np.float32), np.arange(GRID_W, dtype=np.float32), indexing='ij')
    nf = rot_dim // 4
    inv = (np.float32(ROPE_BASE) ** (-np.arange(nf, dtype=np.float32) / np.float32(nf))).astype(np.float32)
    ang = np.stack([r.reshape(-1)[:, None] * inv, c.reshape(-1)[:, None] * inv], axis=1).astype(np.float32)
    cos, sin = np.cos(ang).astype(np.float32), np.sin(ang).astype(np.float32)
    tc = np.ones((T_LAT + tm, LANES), np.float32)
    ta = np.zeros((T_LAT + tm, LANES), np.float32)
    tb = np.zeros((T_LAT + tm, LANES), np.float32)
    for s0 in lane_starts:
        for a in range(2):
            lo = s0 + a * 2 * nf
            tc[:T_LAT, lo:lo + nf] = cos[:, a]
            tc[:T_LAT, lo + nf:lo + 2 * nf] = cos[:, a]
            ta[:T_LAT, lo:lo + nf] = -sin[:, a]
            tb[:T_LAT, lo + nf:lo + 2 * nf] = sin[:, a]
    return jnp.asarray(tc), jnp.asarray(ta + tb)


MLA_TM = 512
ATTN_TQ = 512
MLA_HEADS_PER_STEP = 8
QK_A = NOPE_A + ROPE_A


def _mla_prep_kernel(*refs, has_q, norm_ckv):
    if has_q:
        (qlat_ref, gq_ref, wuq_ref, gqn_ref, kv_ref, sm_ref, gkv_ref, wkc_ref, wv_ref, gkn_ref,
         c_ref, s_ref, q_o, k_o, v_o, ckv_o, kr_o) = refs
        c, sn = c_ref[...], s_ref[...]
    else:
        (kv_ref, sm_ref, gkv_ref, wkc_ref, wv_ref, gkn_ref, k_o, v_o) = refs

    def heads(z, g_ref, o_ref, scale):
        for h in range(H_A):
            s = z[:, h * LANES:(h + 1) * LANES]
            r = lax.rsqrt(jnp.sum(s * s, axis=-1, keepdims=True) * (1.0 / QK_A) + EPS)
            y = s * r * g_ref[0:1, :]
            if has_q:
                sw = z[:, (H_A + h) * LANES:(H_A + h + 1) * LANES]
                y = y * c + (sw * r * g_ref[1:2, :]) * sn
            if scale != 1.0:
                y = y * scale
            o_ref[:, h * LANES:(h + 1) * LANES] = y.astype(o_ref.dtype)

    if has_q:
        ql = qlat_ref[...].astype(F32)
        qn = ql * lax.rsqrt(jnp.mean(ql * ql, axis=-1, keepdims=True) + EPS) * gq_ref[...]
        q = jnp.dot(qn.astype(BF16), wuq_ref[...], preferred_element_type=F32)
        heads(q, gqn_ref, q_o, MLA_SCALE)

    if has_q:
        kv, sm = kv_ref[...].astype(F32), sm_ref[...]
    else:
        kv, sm = kv_ref[0, 0].astype(F32), sm_ref[0, 0]
    if norm_ckv:
        ckv = kv * lax.rsqrt(jnp.mean(kv * kv, axis=-1, keepdims=True) + EPS) * gkv_ref[...]
    else:
        ckv = kv
    ckv_b = ckv.astype(BF16)
    kin = jnp.concatenate([ckv_b, sm.astype(BF16)], axis=1)
    wkc = wkc_ref[...] if has_q else wkc_ref[:, :H_A * LANES]
    k = jnp.dot(kin, wkc, preferred_element_type=F32)
    heads(k, gkn_ref, k_o, 1.0)
    v_o[...] = jnp.dot(ckv_b, wv_ref[...], preferred_element_type=F32).astype(v_o.dtype)
    if has_q:
        ckv_o[...] = ckv
        kr_o[...] = sm.astype(F32)


def pack_mla_weights(lp):
    wuq = lp['w_mla_uq'].reshape(Q_RANK, H_A, QK_A)
    wuq_p = jnp.pad(wuq, ((0, 0), (0, 0), (0, LANES - QK_A))).reshape(Q_RANK, H_A * LANES).astype(BF16)
    wukv = lp['w_mla_ukv'].reshape(KV_RANK, H_A, NOPE_A + V_A)
    wk = jnp.pad(wukv[:, :, :NOPE_A], ((0, 0), (0, 0), (0, LANES - NOPE_A))).reshape(KV_RANK, H_A * LANES)
    place = np.zeros((LANES, H_A, LANES), np.float32)
    for h in range(H_A):
        place[np.arange(ROPE_A), h, NOPE_A + np.arange(ROPE_A)] = 1.0
    wkc = jnp.concatenate([wk, jnp.asarray(place.reshape(LANES, H_A * LANES))], axis=0).astype(BF16)
    wv = wukv[:, :, NOPE_A:]
    wv_even = jnp.pad(wv, ((0, 0), (0, 0), (0, LANES - V_A)))
    wv_odd = jnp.pad(wv, ((0, 0), (0, 0), (LANES - V_A, 0)))
    odd = (np.arange(H_A) % 2 == 1)[None, :, None]
    wv_p = jnp.where(odd, wv_odd, wv_even).reshape(KV_RANK, H_A * LANES).astype(BF16)
    partner = _rope_partner(ROPE_A, (NOPE_A,))
    cols = (np.arange(H_A)[:, None] * LANES + partner[None, :]).reshape(-1)
    with_partner = lambda w: jnp.concatenate([w, w[:, cols]], axis=1)
    pad_g = lambda g: jnp.pad(g, (0, LANES - QK_A))
    gain2 = lambda g: jnp.stack([pad_g(g), pad_g(g)[partner]])
    return dict(wuq=with_partner(wuq_p), wkc=with_partner(wkc), wv=wv_p,
                gqn=gain2(lp['g_mla_qn']), gkn=gain2(lp['g_mla_kn']),
                gq=lp['g_mla_qlat'].reshape(1, Q_RANK), gkv=lp['g_mla_kvlat'].reshape(1, KV_RANK))


def mla_prep(p, mw, tabs):
    tm = MLA_TM
    n = R_ALL // tm
    full = lambda shape: pl.BlockSpec(shape, lambda i: (0, 0))
    tab = pl.BlockSpec((tm, LANES), lambda i: (_pos_block_of_tile(i, tm), 0))
    wide = H_A * LANES
    out_shape = (jax.ShapeDtypeStruct((R_ALL, wide), BF16), jax.ShapeDtypeStruct((R_ALL, wide), BF16),
                 jax.ShapeDtypeStruct((R_ALL, wide), BF16), jax.ShapeDtypeStruct((R_ALL, KV_RANK), F32),
                 jax.ShapeDtypeStruct((R_ALL, LANES), F32))
    row = lambda w: pl.BlockSpec((tm, w), lambda i: (i, 0))
    return pl.pallas_call(
        functools.partial(_mla_prep_kernel, has_q=True, norm_ckv=True),
        grid=(n,),
        in_specs=[pl.BlockSpec((tm, Q_RANK), lambda i: (i, P_QLAT // Q_RANK)), full((1, Q_RANK)),
                  full((Q_RANK, 2 * wide)), full((2, LANES)),
                  pl.BlockSpec((tm, KV_RANK), lambda i: (i, P_KVLAT // KV_RANK)),
                  pl.BlockSpec((tm, LANES), lambda i: (i, P_SMALL // LANES)), full((1, KV_RANK)),
                  full((2 * LANES, 2 * wide)), full((KV_RANK, wide)), full((2, LANES)), tab, tab],
        out_specs=(row(wide), row(wide), row(wide), row(KV_RANK), row(LANES)),
        out_shape=out_shape,
        compiler_params=_cparams(("arbitrary",)),
        name="mla_prep",
    )(p, mw['gq'], mw['wuq'], mw['gqn'], p, p, mw['gkv'], mw['wkc'], mw['wv'], mw['gkn'], *tabs)


def mla_prep_cache(ckv_c, kr_c, mw, l):
    tm = PAST_LEN
    r = ckv_c.shape[0] * PAST_LEN
    full = lambda shape: pl.BlockSpec(shape, lambda i: (0, 0))
    wide = H_A * LANES
    row = lambda w: pl.BlockSpec((tm, w), lambda i: (i, 0))
    cache = lambda w: pl.BlockSpec((1, 1, PAST_LEN, w), lambda i: (i, l, 0, 0))
    return pl.pallas_call(
        functools.partial(_mla_prep_kernel, has_q=False, norm_ckv=False),
        grid=(r // tm,),
        in_specs=[cache(KV_RANK), cache(LANES), full((1, KV_RANK)), full((2 * LANES, 2 * wide)),
                  full((KV_RANK, wide)), full((2, LANES))],
        out_specs=(row(wide), row(wide)),
        out_shape=(jax.ShapeDtypeStruct((r, wide), BF16), jax.ShapeDtypeStruct((r, wide), BF16)),
        compiler_params=_cparams(("arbitrary",)),
        name="mla_prep_cache",
    )(ckv_c, kr_c, mw['gkv'], mw['wkc'], mw['wv'], mw['gkn'])


_NT = (((1,), (1,)), ((), ()))


def _mla_attn_kernel(*refs, has_cache):
    if has_cache:
        q_ref, ko_ref, vo_ref, kc_ref, vc_ref, o_ref = refs
    else:
        q_ref, ko_ref, vo_ref, o_ref = refs
    n_heads = q_ref.shape[1] // LANES
    sls = [slice(h * LANES, (h + 1) * LANES) for h in range(n_heads)]
    def scores(sl):
        s_o = lax.dot_general(q_ref[:, sl], ko_ref[:, sl], _NT, preferred_element_type=F32)
        s_c = lax.dot_general(q_ref[:, sl], kc_ref[:, sl], _NT, preferred_element_type=F32) if has_cache else None
        return s_o, s_c

    outs = []
    nxt = scores(sls[0])
    for h, sl in enumerate(sls):
        s_o, s_c = nxt
        if h + 1 < n_heads:
            nxt = scores(sls[h + 1])
        m = jnp.max(s_o, axis=-1, keepdims=True)
        if has_cache:
            m = jnp.maximum(m, jnp.max(s_c, axis=-1, keepdims=True))
        e_o = jnp.exp(s_o - m)
        l = jnp.sum(e_o, axis=-1, keepdims=True)
        pv = jnp.dot(e_o.astype(BF16), vo_ref[:, sl], preferred_element_type=F32)
        if has_cache:
            e_c = jnp.exp(s_c - m)
            l = l + jnp.sum(e_c, axis=-1, keepdims=True)
            pv = pv + jnp.dot(e_c.astype(BF16), vc_ref[:, sl], preferred_element_type=F32)
        outs.append(pv * (1.0 / l))
    for p in range(n_heads // 2):
        o_ref[:, p * LANES:(p + 1) * LANES] = (outs[2 * p] + outs[2 * p + 1]).astype(o_ref.dtype)


def mla_attention(q, k, v, kc, vc, *, row0, n_seq, t_seq, tq):
    has_cache = kc is not None
    wh = MLA_HEADS_PER_STEP * LANES
    n_grp = H_A // MLA_HEADS_PER_STEP
    nq = t_seq // tq
    qb0, kb0 = row0 // tq, row0 // t_seq
    in_specs = [pl.BlockSpec((tq, wh), lambda s, p, i: (qb0 + s * nq + i, p)),
                pl.BlockSpec((t_seq, wh), lambda s, p, i: (kb0 + s, p)),
                pl.BlockSpec((t_seq, wh), lambda s, p, i: (kb0 + s, p))]
    args = [q, k, v]
    if has_cache:
        in_specs += [pl.BlockSpec((PAST_LEN, wh), lambda s, p, i: (s, p)),
                     pl.BlockSpec((PAST_LEN, wh), lambda s, p, i: (s, p))]
        args += [kc, vc]
    return pl.pallas_call(
        functools.partial(_mla_attn_kernel, has_cache=has_cache),
        grid=(n_seq, n_grp, nq),
        in_specs=in_specs,
        out_specs=pl.BlockSpec((tq, wh // 2), lambda s, p, i: (s * nq + i, p)),
        out_shape=jax.ShapeDtypeStruct((n_seq * t_seq, W_BRANCH), BF16),
        compiler_params=_cparams(("arbitrary", "arbitrary", "arbitrary")),
        name="mla_attention",
    )(*args)


DIFF_TM = 512


def _diff_prep_kernel(dq_ref, dk_ref, gq_ref, gk_ref, perm_ref, c_ref, s_ref, q_o, ko_o, kp_o):
    c, sn = c_ref[...], s_ref[...]
    lane = lax.broadcasted_iota(jnp.int32, (1, LANES), 1)
    lo = lane < DH_D

    def inv_rms(x):
        x2 = x * x
        s_lo = jnp.sum(jnp.where(lo, x2, 0.0), axis=-1, keepdims=True)
        s_hi = jnp.sum(jnp.where(lo, 0.0, x2), axis=-1, keepdims=True)
        return lax.rsqrt(jnp.where(lo, s_lo, s_hi) * (1.0 / DH_D) + EPS)

    def rotated(n, x_sw, r, g_ref):
        return n * c + (x_sw * r * g_ref[1:2, :]) * sn

    for j in range(W_BRANCH // LANES):
        sl = slice(j * LANES, (j + 1) * LANES)
        xq, xk = dq_ref[:, sl], dk_ref[:, sl]
        q_sw = jnp.dot(xq, perm_ref[...], preferred_element_type=F32)
        k_sw = jnp.dot(xk, perm_ref[...], preferred_element_type=F32)
        xq, xk = xq.astype(F32), xk.astype(F32)
        rq, rk = inv_rms(xq), inv_rms(xk)
        qn = xq * rq * gq_ref[0:1, :]
        kn = xk * rk * gk_ref[0:1, :]
        q_o[:, sl] = (rotated(qn, q_sw, rq, gq_ref) * DIFF_SCALE).astype(q_o.dtype)
        kp_o[:, sl] = kn
        ko_o[:, sl] = rotated(kn, k_sw, rk, gk_ref).astype(ko_o.dtype)


def diff_prep(p, lp, tabs):
    tm = DIFF_TM
    partner = _rope_partner(DH_D, (0, DH_D))
    perm = np.zeros((LANES, LANES), np.float32)
    perm[partner, np.arange(LANES)] = 1.0
    gain2 = lambda g: jnp.stack([jnp.concatenate([g, g]), jnp.concatenate([g, g])[partner]])
    full = lambda shape: pl.BlockSpec(shape, lambda i: (0, 0))
    tab = pl.BlockSpec((tm, LANES), lambda i: (_pos_block_of_tile(i, tm), 0))
    row = pl.BlockSpec((tm, W_BRANCH), lambda i: (i, 0))
    dq_b, dk_b = _seg_block('dq'), _seg_block('dk')
    return pl.pallas_call(
        _diff_prep_kernel,
        grid=(R_ALL // tm,),
        in_specs=[pl.BlockSpec((tm, W_BRANCH), lambda i: (i, dq_b)), pl.BlockSpec((tm, W_BRANCH), lambda i: (i, dk_b)),
                  full((2, LANES)), full((2, LANES)), full((LANES, LANES)), tab, tab],
        out_specs=(row, row, row),
        out_shape=(jax.ShapeDtypeStruct((R_ALL, W_BRANCH), BF16), jax.ShapeDtypeStruct((R_ALL, W_BRANCH), BF16),
                   jax.ShapeDtypeStruct((R_ALL, W_BRANCH), F32)),
        compiler_params=_cparams(("arbitrary",)),
        name="diff_prep",
    )(p, p, gain2(lp['g_diff_qn']), gain2(lp['g_diff_kn']), jnp.asarray(perm, BF16), *tabs)


def _diff_attn_kernel(*refs, has_cache, lambda_init):
    if has_cache:
        dl_ref, q_ref, k_ref, v_ref, kc_ref, vc_ref, g_ref, o_ref = refs
    else:
        dl_ref, q_ref, k_ref, v_ref, g_ref, o_ref = refs
    lane = lax.broadcasted_iota(jnp.int32, (1, LANES), 1)
    dl = dl_ref[...]
    lam = (jnp.exp(jnp.sum(dl[0:1] * dl[1:2], axis=-1, keepdims=True))
           - jnp.exp(jnp.sum(dl[2:3] * dl[3:4], axis=-1, keepdims=True)) + lambda_init)
    units = [(h, w) for h in range(H_D) for w in range(2)]

    def scores(unit):
        h, w = unit
        sl = slice((2 * w + h // 2) * LANES, (2 * w + h // 2 + 1) * LANES)
        q = jnp.where((lane // DH_D) == (h % 2), q_ref[:, sl], jnp.zeros((), q_ref.dtype))
        s_o = lax.dot_general(q, k_ref[:, sl], _NT, preferred_element_type=F32)
        s_c = (lax.dot_general(q, kc_ref[0, 0, :, sl].astype(BF16), _NT, preferred_element_type=F32)
               if has_cache else None)
        return s_o, s_c

    outs = []
    nxt = scores(units[0])
    for n, (h, w) in enumerate(units):
        s_o, s_c = nxt
        if n + 1 < len(units):
            nxt = scores(units[n + 1])
        vs = slice(h * LANES, (h + 1) * LANES)
        m = jnp.max(s_o, axis=-1, keepdims=True)
        if has_cache:
            m = jnp.maximum(m, jnp.max(s_c, axis=-1, keepdims=True))
        e_o = jnp.exp(s_o - m)
        l = jnp.sum(e_o, axis=-1, keepdims=True)
        pv = jnp.dot(e_o.astype(BF16), v_ref[:, vs], preferred_element_type=F32)
        if has_cache:
            e_c = jnp.exp(s_c - m)
            l = l + jnp.sum(e_c, axis=-1, keepdims=True)
            pv = pv + jnp.dot(e_c.astype(BF16), vc_ref[0, 0, :, vs].astype(BF16), preferred_element_type=F32)
        outs.append(pv * (1.0 / l))
    for h in range(H_D):
        y = outs[2 * h] - lam * outs[2 * h + 1]
        r = lax.rsqrt(jnp.mean(y * y, axis=-1, keepdims=True) + EPS)
        o_ref[:, h * LANES:(h + 1) * LANES] = ((y * r * g_ref[...]) * (1.0 - lambda_init)).astype(o_ref.dtype)


def diff_attention(dl, qd, kd, p, kc, vc, g_sub, *, row0, n_seq, t_seq, tq, lambda_init, layer=0):
    has_cache = kc is not None
    nq = t_seq // tq
    qb0, kb0 = row0 // tq, row0 // t_seq
    in_specs = [pl.BlockSpec((4, DH_D), lambda s, i: (0, 0)),
                pl.BlockSpec((tq, W_BRANCH), lambda s, i: (qb0 + s * nq + i, 0)),
                pl.BlockSpec((t_seq, W_BRANCH), lambda s, i: (kb0 + s, 0)),
                pl.BlockSpec((t_seq, W_BRANCH), lambda s, i: (kb0 + s, _seg_block('dv')))]
    args = [dl, qd, kd, p]
    if has_cache:
        in_specs += [pl.BlockSpec((1, 1, PAST_LEN, W_BRANCH), lambda s, i: (s, layer, 0, 0)),
                     pl.BlockSpec((1, 1, PAST_LEN, W_BRANCH), lambda s, i: (s, layer, 0, 0))]
        args += [kc, vc]
    in_specs.append(pl.BlockSpec((1, LANES), lambda s, i: (0, 0)))
    args.append(g_sub.reshape(1, LANES))
    return pl.pallas_call(
        functools.partial(_diff_attn_kernel, has_cache=has_cache, lambda_init=lambda_init),
        grid=(n_seq, nq),
        in_specs=in_specs,
        out_specs=pl.BlockSpec((tq, W_BRANCH), lambda s, i: (s * nq + i, 0)),
        out_shape=jax.ShapeDtypeStruct((n_seq * t_seq, W_BRANCH), BF16),
        compiler_params=_cparams(("arbitrary", "arbitrary")),
        name="diff_attention",
    )(*args)


def _softplus(z):
    return jnp.maximum(z, 0.0) + jnp.log(1.0 + jnp.exp(-jnp.abs(z)))


def _gelu_tanh(x):
    return 0.5 * x * (1.0 + jnp.tanh(math.sqrt(2.0 / math.pi) * (x + 0.044715 * (x * x * x))))


def _rglru_kernel(x_ref, gate_ref, wc_ref, bc_ref, wg_ref, bg_ref, lam_ref, h0_ref, y_ref, st_ref,
                  af_s, uf_s, ab_s, ub_s, *, t_seq):
    t = t_seq
    x = x_ref[...].astype(F32)
    row = lax.broadcasted_iota(jnp.int32, (t, W_B), 0)
    wc = wc_ref[...]
    xc = (wc[0:1] * jnp.where(row >= 2, pltpu.roll(x, 2, 0), 0.0)
          + wc[1:2] * jnp.where(row >= 1, pltpu.roll(x, 1, 0), 0.0)
          + wc[2:3] * x
          + wc[3:4] * jnp.where(row < t - 1, pltpu.roll(x, t - 1, 0), 0.0)
          + bc_ref[...])
    gates = jnp.dot(xc.astype(BF16), wg_ref[...], preferred_element_type=F32) + bg_ref[...]
    for d, (a_s, u_s) in enumerate(((af_s, uf_s), (ab_s, ub_s))):
        rg = jax.nn.sigmoid(gates[:, (2 * d) * W_B:(2 * d + 1) * W_B])
        ig = jax.nn.sigmoid(gates[:, (2 * d + 1) * W_B:(2 * d + 2) * W_B])
        log_a = -RG_C * rg * _softplus(-lam_ref[d:d + 1, :])
        a = jnp.exp(log_a)
        a_s[...] = a
        u_s[...] = jnp.sqrt(-jnp.tanh(log_a) * (a * a + 1.0)) * (ig * xc)

    nblk = t // SUBLANES

    def body(k, carry):
        hf, hb = carry
        base_f = pl.multiple_of(k * SUBLANES, SUBLANES)
        base_b = pl.multiple_of((nblk - 1 - k) * SUBLANES, SUBLANES)
        for r in range(SUBLANES):
            rf = pl.ds(base_f + r, 1)
            rb = pl.ds(base_b + (SUBLANES - 1 - r), 1)
            hf = af_s[rf, :] * hf + uf_s[rf, :]
            hb = ab_s[rb, :] * hb + ub_s[rb, :]
            uf_s[rf, :] = hf
            ub_s[rb, :] = hb
        return hf, hb

    hf, hb = lax.fori_loop(0, nblk, body, (h0_ref[0, 0:1, :], h0_ref[0, 1:2, :]))
    st_ref[0, 0:1, :] = hf
    st_ref[0, 1:2, :] = hb
    y_ref[...] = (_gelu_tanh(gate_ref[...].astype(F32)) * (uf_s[...] + ub_s[...])).astype(y_ref.dtype)


def pack_rglru_weights(lp):
    def blockdiag(w):
        eye = jnp.eye(NB_B, dtype=w.dtype)
        return jnp.einsum('ncd,nm->ncmd', w, eye).reshape(W_B, W_B)
    wg = jnp.concatenate([blockdiag(lp['w_rg_a'][0]), blockdiag(lp['w_rg_x'][0]),
                          blockdiag(lp['w_rg_a'][1]), blockdiag(lp['w_rg_x'][1])], axis=1).astype(BF16)
    bg = jnp.concatenate([lp['b_rg_a'][0], lp['b_rg_x'][0], lp['b_rg_a'][1], lp['b_rg_x'][1]]).reshape(1, 4 * W_B)
    return dict(wg=wg, bg=bg, wc=lp['w_conv_rg'], bc=lp['b_conv_rg'].reshape(1, W_B), lam=lp['rg_lambda'])


def rglru(p, rw, h0, *, row0, n_seq, t_seq):
    rb0 = row0 // t_seq
    xb, gb = _seg_block('rg_x'), _seg_block('rg_gate')
    full = lambda shape: pl.BlockSpec(shape, lambda s: tuple(0 for _ in shape))
    in_specs = [pl.BlockSpec((t_seq, W_B), lambda s: (rb0 + s, xb)),
                pl.BlockSpec((t_seq, W_B), lambda s: (rb0 + s, gb)),
                full((CONV_W, W_B)), full((1, W_B)), full((W_B, 4 * W_B)), full((1, 4 * W_B)), full((2, W_B)),
                pl.BlockSpec((1, 2, W_B), lambda s: (s, 0, 0))]
    args = [p, p, rw['wc'], rw['bc'], rw['wg'], rw['bg'], rw['lam'], h0]
    return pl.pallas_call(
        functools.partial(_rglru_kernel, t_seq=t_seq),
        grid=(n_seq,),
        in_specs=in_specs,
        out_specs=(pl.BlockSpec((t_seq, W_B), lambda s: (s, 0)), pl.BlockSpec((1, 2, W_B), lambda s: (s, 0, 0))),
        out_shape=(jax.ShapeDtypeStruct((n_seq * t_seq, W_B), BF16), jax.ShapeDtypeStruct((n_seq, 2, W_B), F32)),
        scratch_shapes=[pltpu.VMEM((t_seq, W_B), F32)] * 4,
        compiler_params=_cparams(("arbitrary",)),
        name="rglru",
    )(*args)


def _dot_split(a, b_bf16):
    hi = a.astype(BF16)
    lo = (a - hi.astype(F32)).astype(BF16)
    return (jnp.dot(hi, b_bf16, preferred_element_type=F32) + jnp.dot(lo, b_bf16, preferred_element_type=F32))


def _log_sigmoid(z):
    return jnp.minimum(z, 0.0) - jnp.log(1.0 + jnp.exp(-jnp.abs(z)))


_TN = (((0,), (0,)), ((), ()))


def _mlstm_kernel(q_ref, k_ref, v_ref, o_ref, sm_ref, bias_ref, g_ref, c0_ref, n0_ref, m0_ref,
                  y_ref, c_out, n_out, m_out, hm_s, c_s, *, t_seq):
    L = ML_CHUNK
    nchunk = t_seq // L
    scale = DH_C ** -0.5
    ri = lax.broadcasted_iota(jnp.int32, (L, L), 0)
    ci = lax.broadcasted_iota(jnp.int32, (L, L), 1)
    lane1 = lax.broadcasted_iota(jnp.int32, (L, LANES), 1)
    ones_col = jnp.where(lane1 == 0, 1.0, 0.0).astype(BF16)
    bias = bias_ref[...]

    for d in range(2):
        causal = (ci <= ri) if d == 0 else (ci >= ri)
        tri = jnp.where(causal, 1.0, 0.0).astype(BF16)
        tri_t = jnp.where((ri <= ci) if d == 0 else (ri >= ci), 1.0, 0.0).astype(BF16)
        for h in range(H_C):
            c_s[h, :, 0:DH_C] = c0_ref[0, 0, d, h]
            c_s[h, :, DH_C:2 * DH_C] = jnp.where(lane1 == 0, n0_ref[0, 0, d, h], 0.0)
        m_init = tuple(m0_ref[0, :, d * H_C + h:d * H_C + h + 1] for h in range(H_C))

        def chunk(kk, ms, d=d, causal=causal, tri=tri, tri_t=tri_t):
            cidx = kk if d == 0 else nchunk - 1 - kk
            rows = pl.ds(pl.multiple_of(cidx * L, L), L)
            gsm = sm_ref[rows, :].astype(F32) + bias
            lf_all = _log_sigmoid(gsm)
            cum_cols = _dot_split_left(tri, lf_all)
            g_t = gsm.T
            cum_rows = _dot_split(lf_all.T, tri_t)
            heads = range(H_C)
            sls = [slice(h * DH_C, (h + 1) * DH_C) for h in heads]
            qs = [q_ref[rows, sl] for sl in sls]
            ks = [k_ref[rows, sl] for sl in sls]
            v_augs = [jnp.concatenate([v_ref[rows, sl], ones_col], axis=1) for sl in sls]
            qk = [lax.dot_general(qs[h], ks[h], _NT, preferred_element_type=F32) for h in heads]
            qc = [jnp.dot(qs[h], c_s[h].astype(BF16), preferred_element_type=F32) * scale for h in heads]
            cum_c = [cum_cols[:, SM_MF + d * H_C + h:SM_MF + d * H_C + h + 1] for h in heads]
            li_c = [gsm[:, SM_MI + d * H_C + h:SM_MI + d * H_C + h + 1] for h in heads]
            m_row, s = [], []
            for h in heads:
                jl, jf = SM_MI + d * H_C + h, SM_MF + d * H_C + h
                log_d = jnp.where(causal, cum_c[h] - cum_rows[jf:jf + 1, :] + g_t[jl:jl + 1, :], -jnp.inf)
                m_row.append(jnp.maximum(cum_c[h] + ms[h], jnp.max(log_d, axis=-1, keepdims=True)))
                s.append(qk[h] * (scale * jnp.exp(log_d - m_row[h])))
            sv = [jnp.dot(s[h].astype(BF16), v_augs[h], preferred_element_type=F32) for h in heads]
            new_ms = []
            for h in heads:
                w_inter = jnp.exp(cum_c[h] + ms[h] - m_row[h])
                nd = sv[h] + qc[h] * w_inter
                den = jnp.maximum(jnp.abs(nd[:, DH_C:DH_C + 1]), jnp.exp(-m_row[h]))
                h_out = nd[:, :DH_C] * (1.0 / den)
                if d == 0:
                    hm_s[rows, sls[h]] = h_out
                else:
                    hm_s[rows, sls[h]] = hm_s[rows, sls[h]] + h_out
                last = cum_c[h][L - 1:L, :] if d == 0 else cum_c[h][0:1, :]
                w_s = last - cum_c[h] + li_c[h]
                m_new = jnp.maximum(last + ms[h], jnp.max(w_s, axis=0, keepdims=True))
                decay = jnp.exp(last + ms[h] - m_new)
                kw_t = (ks[h].astype(F32) * jnp.exp(w_s - m_new)).T.astype(BF16)
                c_s[h] = decay * c_s[h] + jnp.dot(kw_t, v_augs[h], preferred_element_type=F32)
                new_ms.append(m_new)
            return tuple(new_ms)

        m_fin = lax.fori_loop(0, nchunk, chunk, m_init)
        for h in range(H_C):
            c_out[0, d, h] = c_s[h, :, 0:DH_C]
            n_out[0, d, h] = c_s[h, :, DH_C:DH_C + 1]
            m_out[0, :, d * H_C + h:d * H_C + h + 1] = m_fin[h]

    for h in range(H_C):
        sl = slice(h * DH_C, (h + 1) * DH_C)
        hm = hm_s[:, sl]
        r = lax.rsqrt(jnp.mean(hm * hm, axis=-1, keepdims=True) + EPS)
        y_ref[:, sl] = (jax.nn.sigmoid(o_ref[:, sl].astype(F32)) * (hm * r * g_ref[...])).astype(y_ref.dtype)


def _dot_split_left(a_bf16, b):
    hi = b.astype(BF16)
    lo = (b - hi.astype(F32)).astype(BF16)
    return (jnp.dot(a_bf16, hi, preferred_element_type=F32) + jnp.dot(a_bf16, lo, preferred_element_type=F32))


def mlstm(p, bias_sm, g_out, c0, n0, m0, *, row0, n_seq, t_seq, layer=0):
    rb0 = row0 // t_seq
    seg = lambda nm: pl.BlockSpec((t_seq, W_BRANCH), lambda s, b=_seg_block(nm): (rb0 + s, b))
    full = lambda shape: pl.BlockSpec(shape, lambda s: tuple(0 for _ in shape))
    c_spec = pl.BlockSpec((1, 2, H_C, DH_C, DH_C), lambda s: (s, 0, 0, 0, 0))
    n_spec = pl.BlockSpec((1, 2, H_C, DH_C, 1), lambda s: (s, 0, 0, 0, 0))
    c_in = pl.BlockSpec((1, 1, 2, H_C, DH_C, DH_C), lambda s: (s, layer, 0, 0, 0, 0))
    n_in = pl.BlockSpec((1, 1, 2, H_C, DH_C, 1), lambda s: (s, layer, 0, 0, 0, 0))
    m_spec = pl.BlockSpec((1, 1, 2 * H_C), lambda s: (s, 0, 0))
    in_specs = [seg('mq'), seg('mk'), seg('mv'), seg('mo'),
                pl.BlockSpec((t_seq, LANES), lambda s: (rb0 + s, P_SMALL // LANES)),
                full((1, LANES)), full((1, DH_C)), c_in, n_in, m_spec]
    args = [p, p, p, p, p, bias_sm, g_out.reshape(1, DH_C), c0, n0, m0]
    return pl.pallas_call(
        functools.partial(_mlstm_kernel, t_seq=t_seq),
        grid=(n_seq,),
        in_specs=in_specs,
        out_specs=(pl.BlockSpec((t_seq, W_BRANCH), lambda s: (s, 0)), c_spec, n_spec, m_spec),
        out_shape=(jax.ShapeDtypeStruct((n_seq * t_seq, W_BRANCH), BF16),
                   jax.ShapeDtypeStruct((n_seq, 2, H_C, DH_C, DH_C), F32),
                   jax.ShapeDtypeStruct((n_seq, 2, H_C, DH_C, 1), F32),
                   jax.ShapeDtypeStruct((n_seq, 1, 2 * H_C), F32)),
        scratch_shapes=[pltpu.VMEM((t_seq, W_BRANCH), F32), pltpu.VMEM((H_C, DH_C, 2 * DH_C), F32)],
        compiler_params=_cparams(("arbitrary",)),
        name="mlstm",
    )(*args)


MERGE_TM = 512


def _merge_kernel(*refs):
    ctx_refs, lat_refs, gate_refs = refs[0:4], refs[4:8], refs[8:12]
    x_ref, mod_ref, wbr_ref, wout_ref, gn_ref, xo_ref, xn_ref = refs[12:19]
    xn3_ref = refs[19] if len(refs) > 19 else None
    is_ctx = pl.program_id(0) < R_CTX // MERGE_TM
    merged = None
    for g in range(N_BRANCH):
        yg = jnp.where(is_ctx, ctx_refs[g][...], lat_refs[g][...])
        pg = jnp.dot(yg, wbr_ref[g], preferred_element_type=F32)
        term = jax.nn.sigmoid(gate_refs[g][...].astype(F32)) * pg
        merged = term if merged is None else merged + term
    y = jnp.dot(merged.astype(BF16), wout_ref[...], preferred_element_type=F32)
    x = x_ref[...] + mod_ref[0, 2:3, :] * y
    xo_ref[...] = x
    r = lax.rsqrt(jnp.mean(x * x, axis=-1, keepdims=True) + EPS)
    xn = (x * r * gn_ref[...]) * (1.0 + mod_ref[0, 4:5, :]) + mod_ref[0, 3:4, :]
    xn_ref[...] = xn.astype(xn_ref.dtype)
    if xn3_ref is not None:
        _rows_to_tiles(xn3_ref, xn.astype(BF16))


def merge(ys_ctx, ys_lat, p, x, mod, wbr, wout, g_ffn, rows_as_tiles):
    tm = MERGE_TM
    n_ctx_tiles = R_CTX // tm
    br_ctx = pl.BlockSpec((tm, W_BRANCH), lambda i: (jnp.minimum(i, n_ctx_tiles - 1), 0))
    br_lat = pl.BlockSpec((tm, W_BRANCH), lambda i: (jnp.maximum(i - n_ctx_tiles, 0), 0))
    gate = lambda g: pl.BlockSpec((tm, D_MODEL), lambda i, g=g: (i, g))
    row = pl.BlockSpec((tm, D_MODEL), lambda i: (i, 0))
    out_specs = [row, row]
    out_shape = [jax.ShapeDtypeStruct((R_ALL, D_MODEL), F32), jax.ShapeDtypeStruct((R_ALL, D_MODEL), BF16)]
    if rows_as_tiles:
        out_specs.append(pl.BlockSpec((tm, ROW_SUB, LANES), lambda i: (i, 0, 0)))
        out_shape.append(jax.ShapeDtypeStruct((R_ALL, ROW_SUB, LANES), F32))
    return pl.pallas_call(
        _merge_kernel,
        grid=(R_ALL // tm,),
        in_specs=[br_ctx] * N_BRANCH + [br_lat] * N_BRANCH + [gate(0), gate(1), gate(2), gate(3), row,
                  pl.BlockSpec((1, 8, D_MODEL), lambda i: (_mod_row_of_tile(i, tm), 0, 0)),
                  pl.BlockSpec((N_BRANCH, W_BRANCH, D_MODEL), lambda i: (0, 0, 0)),
                  pl.BlockSpec((D_MODEL, D_MODEL), lambda i: (0, 0)),
                  pl.BlockSpec((1, D_MODEL), lambda i: (0, 0))],
        out_specs=tuple(out_specs),
        out_shape=tuple(out_shape),
        compiler_params=_cparams(("arbitrary",)),
        name="merge",
    )(*ys_ctx, *ys_lat, p, p, p, p, x, mod, wbr, wout, g_ffn.reshape(1, D_MODEL))


def _new_expert(te_ref, i):
    return jnp.logical_or(i == 0, te_ref[i] != te_ref[jnp.maximum(i - 1, 0)])


def _ffn_up_kernel(te_ref, nt_ref, x_ref, wg_ref, wu_ref, h_ref, wgb_ref, wub_ref):
    i = pl.program_id(1)

    @pl.when(_new_expert(te_ref, i))
    def _():
        wgb_ref[...] = wg_ref[0].astype(BF16)
        wub_ref[...] = wu_ref[0].astype(BF16)

    @pl.when(i < nt_ref[0])
    def _():
        x = x_ref[...]
        g = jnp.dot(x, wgb_ref[...], preferred_element_type=F32)
        u = jnp.dot(x, wub_ref[...], preferred_element_type=F32)
        h_ref[...] = (g * jax.nn.sigmoid(g) * u).astype(h_ref.dtype)

    @pl.when(i >= nt_ref[0])
    def _():
        h_ref[...] = jnp.zeros(h_ref.shape, h_ref.dtype)


def ffn_up(tile_expert, n_tiles, xs, wg, wu, tm, tf, weight_buffers):
    r, d = xs.shape
    f = wg.shape[2]
    w_spec = pl.BlockSpec((1, d, tf), lambda j, i, te, nt: (te[i], 0, j), pipeline_mode=pl.Buffered(weight_buffers))
    return pl.pallas_call(
        _ffn_up_kernel,
        grid_spec=pltpu.PrefetchScalarGridSpec(
            num_scalar_prefetch=2,
            grid=(f // tf, r // tm),
            in_specs=[pl.BlockSpec((tm, d), lambda j, i, te, nt: (i, 0)), w_spec, w_spec],
            out_specs=pl.BlockSpec((tm, tf), lambda j, i, te, nt: (i, j)),
            scratch_shapes=[pltpu.VMEM((d, tf), BF16), pltpu.VMEM((d, tf), BF16)]),
        out_shape=jax.ShapeDtypeStruct((r, f), BF16),
        compiler_params=_cparams(("arbitrary", "arbitrary")),
        name="ffn_up",
    )(tile_expert, n_tiles, xs, wg, wu)


def _ffn_down_kernel(te_ref, nt_ref, h_ref, wd_ref, y_ref, wdb_ref):
    i = pl.program_id(0)

    @pl.when(_new_expert(te_ref, i))
    def _():
        wdb_ref[...] = wd_ref[0].astype(BF16)

    @pl.when(i < nt_ref[0])
    def _():
        _rows_to_tiles(y_ref, jnp.dot(h_ref[...], wdb_ref[...], preferred_element_type=F32))

    @pl.when(i >= nt_ref[0])
    def _():
        y_ref[...] = jnp.zeros(y_ref.shape, y_ref.dtype)


def ffn_down(tile_expert, n_tiles, h, wd, tm):
    r, f = h.shape
    d = wd.shape[2]
    return pl.pallas_call(
        _ffn_down_kernel,
        grid_spec=pltpu.PrefetchScalarGridSpec(
            num_scalar_prefetch=2,
            grid=(r // tm,),
            in_specs=[pl.BlockSpec((tm, f), lambda i, te, nt: (i, 0)),
                      pl.BlockSpec((1, f, d), lambda i, te, nt: (te[i], 0, 0))],
            out_specs=pl.BlockSpec((tm, ROW_SUB, LANES), lambda i, te, nt: (i, 0, 0)),
            scratch_shapes=[pltpu.VMEM((f, d), BF16)]),
        out_shape=jax.ShapeDtypeStruct((r, ROW_SUB, LANES), F32),
        compiler_params=_cparams(("arbitrary",)),
        name="ffn_down",
    )(tile_expert, n_tiles, h, wd)


def _ffn_down_res_kernel(h_ref, wd_ref, x_ref, mod_ref, y_ref, wdb_ref):
    @pl.when(pl.program_id(0) == 0)
    def _():
        wdb_ref[...] = wd_ref[...].astype(BF16)

    y = jnp.dot(h_ref[...], wdb_ref[...], preferred_element_type=F32)
    y_ref[...] = x_ref[...] + mod_ref[0, 5:6, :] * y


def ffn_down_residual(h, wd, x, mod):
    tm = 1024
    r, f = h.shape
    d = wd.shape[1]
    return pl.pallas_call(
        _ffn_down_res_kernel,
        grid=(r // tm,),
        in_specs=[pl.BlockSpec((tm, f), lambda i: (i, 0)),
                  pl.BlockSpec((f, d), lambda i: (0, 0), pipeline_mode=pl.Buffered(1)),
                  pl.BlockSpec((tm, d), lambda i: (i, 0)),
                  pl.BlockSpec((1, 8, d), lambda i: (_mod_row_of_tile(i, tm), 0, 0))],
        out_specs=pl.BlockSpec((tm, d), lambda i: (i, 0)),
        out_shape=jax.ShapeDtypeStruct((r, d), F32),
        scratch_shapes=[pltpu.VMEM((f, d), BF16)],
        compiler_params=_cparams(("arbitrary",)),
        name="ffn_down_residual",
    )(h, wd, x, mod)


def dense_swiglu_residual(xn, x, mod, wg, wu, wd):
    t = xn.shape[0]
    tm = 1024
    n_tiles = t // tm
    te = jnp.zeros((n_tiles,), jnp.int32)
    nt = jnp.full((1,), n_tiles, jnp.int32)
    h = ffn_up(te, nt, xn, wg[None], wu[None], tm=tm, tf=wg.shape[1] // 2, weight_buffers=1)
    return ffn_down_residual(h, wd, x, mod)


MOE_TM = 512
DISPATCH_TM = 512


def _moe_routing(logits, tm):
    t = logits.shape[0]
    n_assign = t * TOP_K
    top_v, top_i = lax.top_k(logits, TOP_K)
    gate = jax.nn.softmax(top_v, axis=-1)
    flat_e = top_i.reshape(-1).astype(jnp.int32)
    onehot = (flat_e[:, None] == jnp.arange(N_EXP, dtype=jnp.int32)[None, :])
    blk = LANES
    oh = onehot.astype(F32).reshape(n_assign // blk, blk, N_EXP)
    tril = jnp.tril(jnp.ones((blk, blk), F32))
    within = jnp.einsum('ij,bjk->bik', tril, oh)
    blk_tot = within[:, -1, :]
    blk_off = jnp.cumsum(blk_tot, axis=0) - blk_tot
    csum = (within + blk_off[:, None, :]).reshape(n_assign, N_EXP)
    rank = jnp.sum(jnp.where(onehot, csum - 1.0, 0.0), axis=1).astype(jnp.int32)
    counts = csum[-1].astype(jnp.int32)
    padded = (counts + tm - 1) // tm * tm
    grp_start = jnp.cumsum(padded) - padded
    raw_start = jnp.cumsum(counts) - counts
    slot_of_assign = jnp.sum(jnp.where(onehot, grp_start[None, :], 0), axis=1) + rank

    r_max = n_assign + N_EXP * tm
    tile_start = jnp.arange(r_max // tm, dtype=jnp.int32) * tm
    tile_expert = jnp.sum((tile_start[:, None] >= (grp_start + padded)[None, :]).astype(jnp.int32), axis=1)
    tile_expert = jnp.minimum(tile_expert, N_EXP - 1).astype(jnp.int32)
    n_tiles = (jnp.sum(padded) // tm).astype(jnp.int32).reshape(1)

    order = jnp.argsort(flat_e, stable=True).astype(jnp.int32)
    e_slot = jnp.repeat(tile_expert, tm)
    j = jnp.arange(r_max, dtype=jnp.int32) - grp_start[e_slot]
    src = jnp.clip(raw_start[e_slot] + j, 0, n_assign - 1)
    tok_of_slot = jnp.where(j < counts[e_slot], order[src] // TOP_K, 0)
    return gate, slot_of_assign, tok_of_slot, tile_expert, n_tiles


ROW_SUB = D_MODEL // LANES


def _rows_to_tiles(o3_ref, x):
    for j in range(ROW_SUB):
        o3_ref[:, j, :] = x[:, j * LANES:(j + 1) * LANES].astype(o3_ref.dtype)


def _start_row_gather(idx_ref, src_ref, dst, sem, n_rows):
    def body(q, carry):
        for u in range(2):
            r = 2 * q + u
            pltpu.make_async_copy(src_ref.at[idx_ref[0, 0, r]], dst.at[r], sem).start(priority=u)
        return carry

    lax.fori_loop(0, n_rows // 2, body, 0, unroll=4)


def _wait_row_gather(src_ref, dst, sem, n_rows):
    pltpu.make_async_copy(src_ref.at[pl.ds(0, n_rows)], dst, sem).wait()


def _tiles_to_rows(tiles, rows_ref):
    for j in range(ROW_SUB):
        rows_ref[:, j * LANES:(j + 1) * LANES] = tiles[:, j, :]


def _dispatch_kernel(nt_ref, idx_ref, idx_next_ref, src_ref, o_ref, buf, rows, sem):
    i = pl.program_id(0)
    nt = nt_ref[0]
    tm = o_ref.shape[0]
    slot = i % 2

    @pl.when(i == 0)
    def _():
        _start_row_gather(idx_ref, src_ref, buf.at[0], sem.at[0], tm)

    @pl.when(i + 1 < nt)
    def _():
        _start_row_gather(idx_next_ref, src_ref, buf.at[1 - slot], sem.at[1 - slot], tm)

    @pl.when(i < nt)
    def _():
        _wait_row_gather(src_ref, buf.at[slot], sem.at[slot], tm)
        _tiles_to_rows(buf.at[slot], rows)
        o_ref[...] = rows[...].astype(o_ref.dtype)

    @pl.when(i >= nt)
    def _():
        o_ref[...] = jnp.zeros(o_ref.shape, o_ref.dtype)


def moe_dispatch(n_tiles, tok_of_slot, xn3, tm):
    r = tok_of_slot.shape[0]
    last = r // tm - 1
    idx = tok_of_slot.reshape(r // tm, 1, tm)
    return pl.pallas_call(
        _dispatch_kernel,
        grid_spec=pltpu.PrefetchScalarGridSpec(
            num_scalar_prefetch=1,
            grid=(r // tm,),
            in_specs=[pl.BlockSpec((1, 1, tm), lambda i, nt: (i, 0, 0), memory_space=pltpu.SMEM),
                      pl.BlockSpec((1, 1, tm), lambda i, nt: (jnp.minimum(i + 1, last), 0, 0),
                                   memory_space=pltpu.SMEM),
                      pl.BlockSpec(memory_space=pl.ANY)],
            out_specs=pl.BlockSpec((tm, D_MODEL), lambda i, nt: (i, 0)),
            scratch_shapes=[pltpu.VMEM((2, tm, ROW_SUB, LANES), F32), pltpu.VMEM((tm, D_MODEL), F32),
                            pltpu.SemaphoreType.DMA((2,))]),
        out_shape=jax.ShapeDtypeStruct((r, D_MODEL), BF16),
        compiler_params=_cparams(("arbitrary",)),
        name="moe_dispatch",
    )(n_tiles, idx, idx, xn3)


COMBINE_TM = 256


def _combine_kernel(idx_ref, idx_next_ref, ys_ref, x_ref, gate_ref, mod_ref, oc_ref, ol_ref, buf, rows, sem):
    i = pl.program_id(0)
    tm = oc_ref.shape[0]
    n_rows = TOP_K * tm
    slot = i % 2

    @pl.when(i == 0)
    def _():
        _start_row_gather(idx_ref, ys_ref, buf.at[0], sem.at[0], n_rows)

    @pl.when(i + 1 < pl.num_programs(0))
    def _():
        _start_row_gather(idx_next_ref, ys_ref, buf.at[1 - slot], sem.at[1 - slot], n_rows)

    _wait_row_gather(ys_ref, buf.at[slot], sem.at[slot], n_rows)
    _tiles_to_rows(buf.at[slot], rows)
    f = gate_ref[:, 0:1] * rows[0:tm, :] + gate_ref[:, 1:2] * rows[tm:2 * tm, :]
    res = x_ref[...] + mod_ref[0, 5:6, :] * f

    @pl.when(i < R_CTX // COMBINE_TM)
    def _():
        oc_ref[...] = res

    @pl.when(i >= R_CTX // COMBINE_TM)
    def _():
        ol_ref[...] = res


def moe_combine(slot_of_assign, ys3, x, gate, mod):
    t, d = x.shape
    tm = COMBINE_TM
    idx = slot_of_assign.reshape(t // tm, tm, TOP_K).transpose(0, 2, 1).reshape(t // tm, 1, TOP_K * tm)
    last = t // tm - 1
    n_ctx = R_CTX // tm
    return pl.pallas_call(
        _combine_kernel,
        grid=(t // tm,),
        in_specs=[pl.BlockSpec((1, 1, TOP_K * tm), lambda i: (i, 0, 0), memory_space=pltpu.SMEM),
                  pl.BlockSpec((1, 1, TOP_K * tm), lambda i: (jnp.minimum(i + 1, last), 0, 0),
                               memory_space=pltpu.SMEM),
                  pl.BlockSpec(memory_space=pl.ANY),
                  pl.BlockSpec((tm, d), lambda i: (i, 0)),
                  pl.BlockSpec((tm, TOP_K), lambda i: (i, 0)),
                  pl.BlockSpec((1, 8, d), lambda i: (_mod_row_of_tile(i, tm), 0, 0))],
        out_specs=(pl.BlockSpec((tm, d), lambda i: (jnp.minimum(i, n_ctx - 1), 0)),
                   pl.BlockSpec((tm, d), lambda i: (jnp.maximum(i - n_ctx, 0), 0))),
        out_shape=(jax.ShapeDtypeStruct((R_CTX, d), F32), jax.ShapeDtypeStruct((t - R_CTX, d), F32)),
        scratch_shapes=[pltpu.VMEM((2, TOP_K * tm, ROW_SUB, LANES), F32), pltpu.VMEM((TOP_K * tm, d), F32),
                        pltpu.SemaphoreType.DMA((2,))],
        compiler_params=_cparams(("arbitrary",)),
        name="moe_combine",
    )(idx, idx, ys3, x, gate, mod)


def moe_swiglu_residual(xn3, xn, x, mod, w_router, wg, wu, wd):
    tm = MOE_TM
    logits = jnp.dot(xn.astype(F32), w_router, precision=lax.Precision.HIGHEST)
    gate, slot_of_assign, tok_of_slot, tile_expert, n_tiles = _moe_routing(logits, tm)
    xs = moe_dispatch(n_tiles * (tm // DISPATCH_TM), tok_of_slot, xn3, DISPATCH_TM)
    h = ffn_up(tile_expert, n_tiles, xs, wg, wu, tm=tm, tf=wg.shape[2] // 2, weight_buffers=2)
    ys3 = ffn_down(tile_expert, n_tiles, h, wd, tm=tm)
    return moe_combine(slot_of_assign, ys3, x, gate, mod)


def _layer(x, cond, lp, l, ctx, tabs_a, tabs_d):
    mod = modulation(cond, lp['w_mod'], lp['b_mod'], l).reshape(cond.shape[0], 6, D_MODEL)
    mod = jnp.pad(mod, ((0, 0), (0, 2), (0, 0)))
    p = in_proj(x, mod, lp['g_norm_mix'], pack_w_in(lp['w_in']))

    mw = pack_mla_weights(lp)
    q_a, k_a, v_a, ckv, kr = mla_prep(p, mw, tabs_a)
    kc_a, vc_a = mla_prep_cache(ctx['mla_ckv'], ctx['mla_krope'], mw, l)
    ya_c = mla_attention(q_a, k_a, v_a, None, None, row0=0, n_seq=N_CTX_SEQ, t_seq=T_CTX, tq=T_CTX)
    ya_l = mla_attention(q_a, k_a, v_a, kc_a, vc_a, row0=R_CTX, n_seq=N_LAT_SEQ, t_seq=T_LAT, tq=ATTN_TQ)

    rw = pack_rglru_weights(lp)
    yb_c, st_rg = rglru(p, rw, jnp.zeros((N_CTX_SEQ, 2, W_B), F32), row0=0, n_seq=N_CTX_SEQ, t_seq=T_CTX)
    yb_l, _ = rglru(p, rw, ctx['rglru'], row0=R_CTX, n_seq=N_LAT_SEQ, t_seq=T_LAT)

    bias_sm = jnp.zeros((LANES,), F32).at[SM_MI:SM_MI + 2 * H_C].set(lp['b_ml_i'].reshape(-1))
    bias_sm = bias_sm.at[SM_MF:SM_MF + 2 * H_C].set(lp['b_ml_f'].reshape(-1)).reshape(1, LANES)
    c0_ctx = jnp.zeros((N_CTX_SEQ, 1, 2, H_C, DH_C, DH_C), F32)
    n0_ctx = jnp.zeros((N_CTX_SEQ, 1, 2, H_C, DH_C, 1), F32)
    m0_ctx = jnp.zeros((N_CTX_SEQ, 1, 2 * H_C), F32)
    m0_lat = ctx['mlstm_m'][:, l].reshape(N_LAT_SEQ, 1, 2 * H_C)
    yc_c, c_fin, n_fin, m_fin = mlstm(p, bias_sm, lp['g_ml_out'], c0_ctx, n0_ctx, m0_ctx,
                                      row0=0, n_seq=N_CTX_SEQ, t_seq=T_CTX)
    yc_l, _, _, _ = mlstm(p, bias_sm, lp['g_ml_out'], ctx['mlstm_C'], ctx['mlstm_n'][..., None], m0_lat,
                          row0=R_CTX, n_seq=N_LAT_SEQ, t_seq=T_LAT, layer=l)

    lambda_init = 0.8 - 0.6 * math.exp(-0.3 * l)
    qd, kd_own, kd_plain = diff_prep(p, lp, tabs_d)
    yd_c = diff_attention(lp['diff_lambda'], qd, kd_own, p, None, None, lp['g_diff_sub'],
                          row0=0, n_seq=N_CTX_SEQ, t_seq=T_CTX, tq=T_CTX, lambda_init=lambda_init)
    yd_l = diff_attention(lp['diff_lambda'], qd, kd_own, p, ctx['diff_k'], ctx['diff_v'], lp['g_diff_sub'],
                          row0=R_CTX, n_seq=N_LAT_SEQ, t_seq=T_LAT, tq=ATTN_TQ, lambda_init=lambda_init, layer=l)

    merged = merge((ya_c, yb_c, yc_c, yd_c), (ya_l, yb_l, yc_l, yd_l), p, x, mod, lp['w_br'].astype(BF16),
                   lp['w_out'].astype(BF16), lp['g_norm_ffn'], rows_as_tiles='moe' in lp)
    if 'ffn' in lp:
        x, xn = merged
        x = dense_swiglu_residual(xn, x, mod, *lp['ffn'])
    else:
        x, xn, xn3 = merged
        x = moe_swiglu_residual(xn3, xn, x, mod, *lp['moe'])

    dv0 = _seg_block('dv') * W_BRANCH
    ctx_out = (ckv[:R_CTX].reshape(N_CTX_SEQ, T_CTX, KV_RANK),
               kr[:R_CTX, :ROPE_A].reshape(N_CTX_SEQ, T_CTX, ROPE_A),
               kd_plain[:R_CTX].reshape(N_CTX_SEQ, T_CTX, 2, H_D, DH_D),
               p[:R_CTX, dv0:dv0 + W_BRANCH].astype(F32).reshape(N_CTX_SEQ, T_CTX, H_D, 2 * DH_D),
               st_rg,
               c_fin,
               n_fin.reshape(N_CTX_SEQ, 2, H_C, DH_C),
               m_fin.reshape(N_CTX_SEQ, 2, H_C))
    return x, ctx_out


def kernel(x_prompt, x_sample, cache_mla_ckv, cache_mla_krope, cache_diff_k, cache_diff_v,
           state_rglru, state_mlstm_C, state_mlstm_n, state_mlstm_m, c, c_ctx,
           w_mod, b_mod, g_norm_mix, g_norm_ffn, w_in, g_mla_qlat, w_mla_uq, g_mla_kvlat, w_mla_ukv,
           g_mla_qn, g_mla_kn, w_conv_rg, b_conv_rg, w_rg_a, b_rg_a, w_rg_x, b_rg_x, rg_lambda,
           b_ml_i, b_ml_f, g_ml_out, g_diff_qn, g_diff_kn, diff_lambda, g_diff_sub, w_br, w_out,
           w_ffn_gate, w_ffn_up, w_ffn_down, w_router, w_moe_gate, w_moe_up, w_moe_down):
    assert x_prompt.shape == (N_CTX_SEQ, T_CTX, D_MODEL) and x_sample.shape == (N_LAT_SEQ, T_LAT, D_MODEL)
    tabs_a = _rope_tables(ROPE_A, (NOPE_A,), MLA_TM)
    tabs_d = _rope_tables(DH_D, (0, DH_D), DIFF_TM)
    cond = jnp.concatenate([c_ctx.reshape(1, D_MODEL), c, jnp.zeros((16 - 1 - N_LAT_SEQ, D_MODEL), F32)], axis=0)
    x = jnp.concatenate([x_prompt.reshape(R_CTX, D_MODEL), x_sample.reshape(R_LAT, D_MODEL)], axis=0)
    krope_pad = jnp.pad(cache_mla_krope, ((0, 0), (0, 0), (0, 0), (0, LANES - ROPE_A)))
    diff_k_rows = cache_diff_k.reshape(N_LAT_SEQ, DEPTH, PAST_LEN, W_BRANCH)
    diff_v_rows = cache_diff_v.reshape(N_LAT_SEQ, DEPTH, PAST_LEN, W_BRANCH)
    new = []
    for l in range(DEPTH):
        lp = dict(w_mod=w_mod, b_mod=b_mod, g_norm_mix=g_norm_mix[l], g_norm_ffn=g_norm_ffn[l], w_in=w_in[l],
                  g_mla_qlat=g_mla_qlat[l], w_mla_uq=w_mla_uq[l], g_mla_kvlat=g_mla_kvlat[l], w_mla_ukv=w_mla_ukv[l],
                  g_mla_qn=g_mla_qn[l], g_mla_kn=g_mla_kn[l], w_conv_rg=w_conv_rg[l], b_conv_rg=b_conv_rg[l],
                  w_rg_a=w_rg_a[l], b_rg_a=b_rg_a[l], w_rg_x=w_rg_x[l], b_rg_x=b_rg_x[l], rg_lambda=rg_lambda[l],
                  b_ml_i=b_ml_i[l], b_ml_f=b_ml_f[l], g_ml_out=g_ml_out[l], g_diff_qn=g_diff_qn[l],
                  g_diff_kn=g_diff_kn[l], diff_lambda=diff_lambda[l], g_diff_sub=g_diff_sub[l],
                  w_br=w_br[l], w_out=w_out[l])
        if l % 2 == 0:
            lp['ffn'] = (w_ffn_gate[l // 2], w_ffn_up[l // 2], w_ffn_down[l // 2])
        else:
            lp['moe'] = (w_router[l // 2], w_moe_gate[l // 2], w_moe_up[l // 2], w_moe_down[l // 2])
        ctx_l = dict(mla_ckv=cache_mla_ckv, mla_krope=krope_pad, diff_k=diff_k_rows, diff_v=diff_v_rows,
                     rglru=state_rglru[:, l], mlstm_C=state_mlstm_C, mlstm_n=state_mlstm_n, mlstm_m=state_mlstm_m)
        if isinstance(x, tuple):
            x = jnp.concatenate(x, axis=0)
        x, st = _layer(x, cond, lp, l, ctx_l, tabs_a, tabs_d)
        new.append(st)
    outs = tuple(jnp.stack([s[i] for s in new], axis=1) for i in range(8))
    x_ctx, x_lat = x if isinstance(x, tuple) else (x[:R_CTX], x[R_CTX:])
    return (x_ctx.reshape(N_CTX_SEQ, T_CTX, D_MODEL), x_lat.reshape(N_LAT_SEQ, T_LAT, D_MODEL)) + outs
```

```python
import functools
import math

import jax
import jax.numpy as jnp
import numpy as np
from jax import lax
from jax.experimental import pallas as pl
from jax.experimental.pallas import tpu as pltpu

D_MODEL = 1024
DEPTH = 2
GRID_W = 64
ROPE_BASE = 10000.0
EPS = 1e-6
N_BRANCH = 4
W_BRANCH = D_MODEL // 2

H_A = 8
NOPE_A = 64
ROPE_A = 32
V_A = W_BRANCH // H_A
Q_RANK = D_MODEL // 4
KV_RANK = D_MODEL // 8
MLA_SCALE = (NOPE_A + ROPE_A) ** -0.5

W_B = W_BRANCH
NB_B = 8
BW_B = W_B // NB_B
CONV_W = 4
RG_C = 8.0

H_C = 4
DH_C = W_BRANCH // H_C
ML_CHUNK = 128

H_D = 4
DH_D = W_BRANCH // (2 * H_D)
DIFF_SCALE = DH_D ** -0.5

N_EXP = 8
TOP_K = 2

V7X_VMEM_LIMIT_BYTES = 56 * 1024 * 1024
LANES = 128
SUBLANES = 8

BF16 = jnp.bfloat16
F32 = jnp.float32

P_GATE = 0
P_SEG = N_BRANCH * D_MODEL
SEG_NAMES = ('rg_x', 'rg_gate', 'mq', 'mk', 'mv', 'mo', 'dq', 'dk', 'dv')
P_QLAT = P_SEG + 9 * W_BRANCH
P_KVLAT = P_QLAT + Q_RANK
P_SMALL = P_KVLAT + KV_RANK
P_WIDTH = P_SMALL + LANES
SM_MI = ROPE_A
SM_MF = ROPE_A + 2 * H_C


def _seg_block(name):
    return (P_SEG + SEG_NAMES.index(name) * W_BRANCH) // W_BRANCH


N_CTX_SEQ, T_CTX = 16, 256
N_LAT_SEQ, T_LAT = 8, 1024
PAST_LEN = 512
R_CTX = N_CTX_SEQ * T_CTX
R_LAT = N_LAT_SEQ * T_LAT
R_ALL = R_CTX + R_LAT


def _cparams(sem):
    return pltpu.CompilerParams(dimension_semantics=sem, vmem_limit_bytes=V7X_VMEM_LIMIT_BYTES)


def _mod_row_of_tile(i, tm):
    n_ctx_tiles = R_CTX // tm
    per_seq = T_LAT // tm
    return jnp.where(i < n_ctx_tiles, 0, 1 + (i - n_ctx_tiles) // per_seq)


def _pos_block_of_tile(i, tm):
    n_ctx_tiles = R_CTX // tm
    per_seq = T_LAT // tm
    return jnp.where(i < n_ctx_tiles, per_seq, (i - n_ctx_tiles) % per_seq)


def _mod_kernel(c_ref, w_ref, b_ref, o_ref):
    c = c_ref[...]
    s = (c * jax.nn.sigmoid(c)).astype(BF16)
    o_ref[...] = jnp.dot(s, w_ref[0].astype(BF16), preferred_element_type=F32) + b_ref[0]


def modulation(cond, w_mod, b_mod, l):
    m, d = cond.shape
    n = w_mod.shape[2]
    tn = 1536
    return pl.pallas_call(
        _mod_kernel,
        grid=(n // tn,),
        in_specs=[pl.BlockSpec((m, d), lambda j: (0, 0)),
                  pl.BlockSpec((1, d, tn), lambda j: (l, 0, j)),
                  pl.BlockSpec((1, 1, tn), lambda j: (l, 0, j))],
        out_specs=pl.BlockSpec((m, tn), lambda j: (0, j)),
        out_shape=jax.ShapeDtypeStruct((m, n), F32),
        compiler_params=_cparams(("arbitrary",)),
        name="modulation",
    )(cond, w_mod, b_mod.reshape(b_mod.shape[0], 1, n))


def _in_proj_kernel(x_ref, mod_ref, g_ref, w_ref, o_ref, xn_ref):
    @pl.when(pl.program_id(1) == 0)
    def _():
        x = x_ref[...]
        r = lax.rsqrt(jnp.mean(x * x, axis=-1, keepdims=True) + EPS)
        sh = mod_ref[0, 0:1, :]
        sc = mod_ref[0, 1:2, :]
        xn_ref[...] = ((x * r * g_ref[...]) * (1.0 + sc) + sh).astype(BF16)

    o_ref[...] = jnp.dot(xn_ref[...], w_ref[0], preferred_element_type=F32).astype(o_ref.dtype)


def in_proj(x, mod, g, w_p, l):
    tm, tn = 1024, P_WIDTH // 4
    m, d = x.shape
    n = w_p.shape[2]
    return pl.pallas_call(
        _in_proj_kernel,
        grid=(m // tm, n // tn),
        in_specs=[pl.BlockSpec((tm, d), lambda i, j: (i, 0)),
                  pl.BlockSpec((1, 8, d), lambda i, j: (_mod_row_of_tile(i, tm), 0, 0)),
                  pl.BlockSpec((1, d), lambda i, j: (0, 0)),
                  pl.BlockSpec((1, d, tn), lambda i, j: (l, 0, j))],
        out_specs=pl.BlockSpec((tm, tn), lambda i, j: (i, j)),
        out_shape=jax.ShapeDtypeStruct((m, n), BF16),
        scratch_shapes=[pltpu.VMEM((tm, d), BF16)],
        compiler_params=_cparams(("arbitrary", "arbitrary")),
        name="in_proj",
    )(x, mod, g.reshape(1, d), w_p)


_O_RG = Q_RANK + KV_RANK + ROPE_A
_O_MI = _O_RG + 6 * W_BRANCH
_O_DQ = _O_MI + 4 * H_C
_O_GATE = _O_DQ + 3 * W_BRANCH
_PACK_PIECES = ((_O_GATE, N_BRANCH * D_MODEL, P_GATE), (_O_RG, 6 * W_BRANCH, P_SEG),
                (_O_DQ, 3 * W_BRANCH, P_SEG + 6 * W_BRANCH), (0, Q_RANK + KV_RANK, P_QLAT),
                (Q_RANK + KV_RANK, ROPE_A, P_SMALL), (_O_MI, 4 * H_C, P_SMALL + ROPE_A))
PACK_ROWS = 128


def _pack_w_in_kernel(w_ref, o_ref):
    n_in = w_ref.shape[2]
    o_ref[0, :, P_SMALL:P_WIDTH] = jnp.zeros((PACK_ROWS, LANES), o_ref.dtype)
    for src, width, dst in _PACK_PIECES:
        lo = src // LANES * LANES
        hi = min(_round_up(src + width, LANES), n_in)
        win = w_ref[0, :, lo:hi]
        o_ref[0, :, dst:dst + width] = win[:, src - lo:src - lo + width].astype(o_ref.dtype)


def pack_w_in(w_in):
    depth, d, n_in = w_in.shape
    return pl.pallas_call(
        _pack_w_in_kernel,
        grid=(depth, d // PACK_ROWS),
        in_specs=[pl.BlockSpec((1, PACK_ROWS, n_in), lambda l, i: (l, i, 0))],
        out_specs=pl.BlockSpec((1, PACK_ROWS, P_WIDTH), lambda l, i: (l, i, 0)),
        out_shape=jax.ShapeDtypeStruct((depth, d, P_WIDTH), BF16),
        compiler_params=_cparams(("arbitrary", "arbitrary")),
        name="pack_w_in",
    )(w_in)


def _round_up(x, m):
    return (x + m - 1) // m * m


def _rope_partner(rot_dim, lane_starts):
    nf = rot_dim // 4
    partner = np.arange(LANES)
    for s0 in lane_starts:
        for a in range(2):
            lo = s0 + a * 2 * nf
            partner[lo:lo + nf] = np.arange(lo + nf, lo + 2 * nf)
            partner[lo + nf:lo + 2 * nf] = np.arange(lo, lo + nf)
    return partner


def _rope_tables(rot_dim, lane_starts, tm):
    rows = T_LAT // GRID_W
    r, c = np.meshgrid(np.arange(rows, dtype=np.float32), np.arange(GRID_W, dtype=np.float32), indexing='ij')
    nf = rot_dim // 4
    inv = (np.float32(ROPE_BASE) ** (-np.arange(nf, dtype=np.float32) / np.float32(nf))).astype(np.float32)
    ang = np.stack([r.reshape(-1)[:, None] * inv, c.reshape(-1)[:, None] * inv], axis=1).astype(np.float32)
    cos, sin = np.cos(ang).astype(np.float32), np.sin(ang).astype(np.float32)
    tc = np.ones((T_LAT + tm, LANES), np.float32)
    ta = np.zeros((T_LAT + tm, LANES), np.float32)
    tb = np.zeros((T_LAT + tm, LANES), np.float32)
    for s0 in lane_starts:
        for a in range(2):
            lo = s0 + a * 2 * nf
            tc[:T_LAT, lo:lo + nf] = cos[:, a]
            tc[:T_LAT, lo + nf:lo + 2 * nf] = cos[:, a]
            ta[:T_LAT, lo:lo + nf] = -sin[:, a]
            tb[:T_LAT, lo + nf:lo + 2 * nf] = sin[:, a]
    return jnp.asarray(tc), jnp.asarray(ta + tb)


MLA_TM = 512
ATTN_TQ = 512
MLA_HEADS_PER_STEP = 8
QK_A = NOPE_A + ROPE_A


def _mla_prep_kernel(*refs, has_q, norm_ckv):
    if has_q:
        (qlat_ref, gq_ref, wuq_ref, gqn_ref, kv_ref, sm_ref, gkv_ref, wkc_ref, wv_ref, gkn_ref,
         c_ref, s_ref, q_o, k_o, v_o, ckv_o, kr_o) = refs
        c, sn = c_ref[...], s_ref[...]
    else:
        (kv_ref, sm_ref, gkv_ref, wkc_ref, wv_ref, gkn_ref, k_o, v_o) = refs

    def heads(z, g_ref, o_ref, scale):
        for h in range(H_A):
            s = z[:, h * LANES:(h + 1) * LANES]
            r = lax.rsqrt(jnp.sum(s * s, axis=-1, keepdims=True) * (1.0 / QK_A) + EPS)
            y = s * r * g_ref[0:1, :]
            if has_q:
                sw = z[:, (H_A + h) * LANES:(H_A + h + 1) * LANES]
                y = y * c + (sw * r * g_ref[1:2, :]) * sn
            if scale != 1.0:
                y = y * scale
            o_ref[:, h * LANES:(h + 1) * LANES] = y.astype(o_ref.dtype)

    if has_q:
        ql = qlat_ref[...].astype(F32)
        qn = ql * lax.rsqrt(jnp.mean(ql * ql, axis=-1, keepdims=True) + EPS) * gq_ref[...]
        q = jnp.dot(qn.astype(BF16), wuq_ref[...], preferred_element_type=F32)
        heads(q, gqn_ref, q_o, MLA_SCALE)

    if has_q:
        kv, sm = kv_ref[...].astype(F32), sm_ref[...]
    else:
        kv, sm = kv_ref[0, 0].astype(F32), sm_ref[0, 0]
    if norm_ckv:
        ckv = kv * lax.rsqrt(jnp.mean(kv * kv, axis=-1, keepdims=True) + EPS) * gkv_ref[...]
    else:
        ckv = kv
    ckv_b = ckv.astype(BF16)
    kin = jnp.concatenate([ckv_b, sm.astype(BF16)], axis=1)
    wkc = wkc_ref[...] if has_q else wkc_ref[:, :H_A * LANES]
    k = jnp.dot(kin, wkc, preferred_element_type=F32)
    heads(k, gkn_ref, k_o, 1.0)
    v_o[...] = jnp.dot(ckv_b, wv_ref[...], preferred_element_type=F32).astype(v_o.dtype)
    if has_q:
        ckv_o[...] = ckv
        kr_o[...] = sm.astype(F32)


def pack_mla_weights(lp):
    wuq = lp['w_mla_uq'].reshape(Q_RANK, H_A, QK_A)
    wuq_p = jnp.pad(wuq, ((0, 0), (0, 0), (0, LANES - QK_A))).reshape(Q_RANK, H_A * LANES).astype(BF16)
    wukv = lp['w_mla_ukv'].reshape(KV_RANK, H_A, NOPE_A + V_A)
    wk = jnp.pad(wukv[:, :, :NOPE_A], ((0, 0), (0, 0), (0, LANES - NOPE_A))).reshape(KV_RANK, H_A * LANES)
    place = np.zeros((LANES, H_A, LANES), np.float32)
    for h in range(H_A):
        place[np.arange(ROPE_A), h, NOPE_A + np.arange(ROPE_A)] = 1.0
    wkc = jnp.concatenate([wk, jnp.asarray(place.reshape(LANES, H_A * LANES))], axis=0).astype(BF16)
    wv = wukv[:, :, NOPE_A:]
    wv_even = jnp.pad(wv, ((0, 0), (0, 0), (0, LANES - V_A)))
    wv_odd = jnp.pad(wv, ((0, 0), (0, 0), (LANES - V_A, 0)))
    odd = (np.arange(H_A) % 2 == 1)[None, :, None]
    wv_p = jnp.where(odd, wv_odd, wv_even).reshape(KV_RANK, H_A * LANES).astype(BF16)
    partner = _rope_partner(ROPE_A, (NOPE_A,))
    cols = (np.arange(H_A)[:, None] * LANES + partner[None, :]).reshape(-1)
    with_partner = lambda w: jnp.concatenate([w, w[:, cols]], axis=1)
    pad_g = lambda g: jnp.pad(g, (0, LANES - QK_A))
    gain2 = lambda g: jnp.stack([pad_g(g), pad_g(g)[partner]])
    return dict(wuq=with_partner(wuq_p), wkc=with_partner(wkc), wv=wv_p,
                gqn=gain2(lp['g_mla_qn']), gkn=gain2(lp['g_mla_kn']),
                gq=lp['g_mla_qlat'].reshape(1, Q_RANK), gkv=lp['g_mla_kvlat'].reshape(1, KV_RANK))


def mla_prep(p, mw, tabs):
    tm = MLA_TM
    n = R_ALL // tm
    full = lambda shape: pl.BlockSpec(shape, lambda i: (0, 0))
    tab = pl.BlockSpec((tm, LANES), lambda i: (_pos_block_of_tile(i, tm), 0))
    wide = H_A * LANES
    out_shape = (jax.ShapeDtypeStruct((R_ALL, wide), BF16), jax.ShapeDtypeStruct((R_ALL, wide), BF16),
                 jax.ShapeDtypeStruct((R_ALL, wide), BF16), jax.ShapeDtypeStruct((R_ALL, KV_RANK), F32),
                 jax.ShapeDtypeStruct((R_ALL, LANES), F32))
    row = lambda w: pl.BlockSpec((tm, w), lambda i: (i, 0))
    return pl.pallas_call(
        functools.partial(_mla_prep_kernel, has_q=True, norm_ckv=True),
        grid=(n,),
        in_specs=[pl.BlockSpec((tm, Q_RANK), lambda i: (i, P_QLAT // Q_RANK)), full((1, Q_RANK)),
                  full((Q_RANK, 2 * wide)), full((2, LANES)),
                  pl.BlockSpec((tm, KV_RANK), lambda i: (i, P_KVLAT // KV_RANK)),
                  pl.BlockSpec((tm, LANES), lambda i: (i, P_SMALL // LANES)), full((1, KV_RANK)),
                  full((2 * LANES, 2 * wide)), full((KV_RANK, wide)), full((2, LANES)), tab, tab],
        out_specs=(row(wide), row(wide), row(wide), row(KV_RANK), row(LANES)),
        out_shape=out_shape,
        compiler_params=_cparams(("arbitrary",)),
        name="mla_prep",
    )(p, mw['gq'], mw['wuq'], mw['gqn'], p, p, mw['gkv'], mw['wkc'], mw['wv'], mw['gkn'], *tabs)


def mla_prep_cache(ckv_c, kr_c, mw, l):
    tm = PAST_LEN
    r = ckv_c.shape[0] * PAST_LEN
    full = lambda shape: pl.BlockSpec(shape, lambda i: (0, 0))
    wide = H_A * LANES
    row = lambda w: pl.BlockSpec((tm, w), lambda i: (i, 0))
    cache = lambda w: pl.BlockSpec((1, 1, PAST_LEN, w), lambda i: (i, l, 0, 0))
    return pl.pallas_call(
        functools.partial(_mla_prep_kernel, has_q=False, norm_ckv=False),
        grid=(r // tm,),
        in_specs=[cache(KV_RANK), cache(LANES), full((1, KV_RANK)), full((2 * LANES, 2 * wide)),
                  full((KV_RANK, wide)), full((2, LANES))],
        out_specs=(row(wide), row(wide)),
        out_shape=(jax.ShapeDtypeStruct((r, wide), BF16), jax.ShapeDtypeStruct((r, wide), BF16)),
        compiler_params=_cparams(("arbitrary",)),
        name="mla_prep_cache",
    )(ckv_c, kr_c, mw['gkv'], mw['wkc'], mw['wv'], mw['gkn'])


_NT = (((1,), (1,)), ((), ()))


def _mla_attn_kernel(*refs, has_cache):
    if has_cache:
        q_ref, ko_ref, vo_ref, kc_ref, vc_ref, o_ref = refs
    else:
        q_ref, ko_ref, vo_ref, o_ref = refs
    n_heads = q_ref.shape[1] // LANES
    sls = [slice(h * LANES, (h + 1) * LANES) for h in range(n_heads)]
    def scores(sl):
        s_o = lax.dot_general(q_ref[:, sl], ko_ref[:, sl], _NT, preferred_element_type=F32)
        s_c = lax.dot_general(q_ref[:, sl], kc_ref[:, sl], _NT, preferred_element_type=F32) if has_cache else None
        return s_o, s_c

    outs = []
    nxt = scores(sls[0])
    for h, sl in enumerate(sls):
        s_o, s_c = nxt
        if h + 1 < n_heads:
            nxt = scores(sls[h + 1])
        m = jnp.max(s_o, axis=-1, keepdims=True)
        if has_cache:
            m = jnp.maximum(m, jnp.max(s_c, axis=-1, keepdims=True))
        e_o = jnp.exp(s_o - m)
        l = jnp.sum(e_o, axis=-1, keepdims=True)
        pv = jnp.dot(e_o.astype(BF16), vo_ref[:, sl], preferred_element_type=F32)
        if has_cache:
            e_c = jnp.exp(s_c - m)
            l = l + jnp.sum(e_c, axis=-1, keepdims=True)
            pv = pv + jnp.dot(e_c.astype(BF16), vc_ref[:, sl], preferred_element_type=F32)
        outs.append(pv * (1.0 / l))
    for p in range(n_heads // 2):
        o_ref[:, p * LANES:(p + 1) * LANES] = (outs[2 * p] + outs[2 * p + 1]).astype(o_ref.dtype)


def mla_attention(q, k, v, kc, vc, *, row0, n_seq, t_seq, tq):
    has_cache = kc is not None
    wh = MLA_HEADS_PER_STEP * LANES
    n_grp = H_A // MLA_HEADS_PER_STEP
    nq = t_seq // tq
    qb0, kb0 = row0 // tq, row0 // t_seq
    in_specs = [pl.BlockSpec((tq, wh), lambda s, p, i: (qb0 + s * nq + i, p)),
                pl.BlockSpec((t_seq, wh), lambda s, p, i: (kb0 + s, p)),
                pl.BlockSpec((t_seq, wh), lambda s, p, i: (kb0 + s, p))]
    args = [q, k, v]
    if has_cache:
        in_specs += [pl.BlockSpec((PAST_LEN, wh), lambda s, p, i: (s, p)),
                     pl.BlockSpec((PAST_LEN, wh), lambda s, p, i: (s, p))]
        args += [kc, vc]
    return pl.pallas_call(
        functools.partial(_mla_attn_kernel, has_cache=has_cache),
        grid=(n_seq, n_grp, nq),
        in_specs=in_specs,
        out_specs=pl.BlockSpec((tq, wh // 2), lambda s, p, i: (s * nq + i, p)),
        out_shape=jax.ShapeDtypeStruct((n_seq * t_seq, W_BRANCH), BF16),
        compiler_params=_cparams(("arbitrary", "arbitrary", "arbitrary")),
        name="mla_attention",
    )(*args)


DIFF_TM = 512


def _diff_prep_kernel(dq_ref, dk_ref, gq_ref, gk_ref, perm_ref, c_ref, s_ref, q_o, ko_o, kp_o):
    c, sn = c_ref[...], s_ref[...]
    lane = lax.broadcasted_iota(jnp.int32, (1, LANES), 1)
    lo = lane < DH_D

    def inv_rms(x):
        x2 = x * x
        s_lo = jnp.sum(jnp.where(lo, x2, 0.0), axis=-1, keepdims=True)
        s_hi = jnp.sum(jnp.where(lo, 0.0, x2), axis=-1, keepdims=True)
        return lax.rsqrt(jnp.where(lo, s_lo, s_hi) * (1.0 / DH_D) + EPS)

    def rotated(n, x_sw, r, g_ref):
        return n * c + (x_sw * r * g_ref[1:2, :]) * sn

    for j in range(W_BRANCH // LANES):
        sl = slice(j * LANES, (j + 1) * LANES)
        xq, xk = dq_ref[:, sl], dk_ref[:, sl]
        q_sw = jnp.dot(xq, perm_ref[...], preferred_element_type=F32)
        k_sw = jnp.dot(xk, perm_ref[...], preferred_element_type=F32)
        xq, xk = xq.astype(F32), xk.astype(F32)
        rq, rk = inv_rms(xq), inv_rms(xk)
        qn = xq * rq * gq_ref[0:1, :]
        kn = xk * rk * gk_ref[0:1, :]
        q_o[:, sl] = (rotated(qn, q_sw, rq, gq_ref) * DIFF_SCALE).astype(q_o.dtype)
        kp_o[:, sl] = kn
        ko_o[:, sl] = rotated(kn, k_sw, rk, gk_ref).astype(ko_o.dtype)


def diff_prep(p, lp, tabs):
    tm = DIFF_TM
    partner = _rope_partner(DH_D, (0, DH_D))
    perm = np.zeros((LANES, LANES), np.float32)
    perm[partner, np.arange(LANES)] = 1.0
    gain2 = lambda g: jnp.stack([jnp.concatenate([g, g]), jnp.concatenate([g, g])[partner]])
    full = lambda shape: pl.BlockSpec(shape, lambda i: (0, 0))
    tab = pl.BlockSpec((tm, LANES), lambda i: (_pos_block_of_tile(i, tm), 0))
    row = pl.BlockSpec((tm, W_BRANCH), lambda i: (i, 0))
    dq_b, dk_b = _seg_block('dq'), _seg_block('dk')
    return pl.pallas_call(
        _diff_prep_kernel,
        grid=(R_ALL // tm,),
        in_specs=[pl.BlockSpec((tm, W_BRANCH), lambda i: (i, dq_b)), pl.BlockSpec((tm, W_BRANCH), lambda i: (i, dk_b)),
                  full((2, LANES)), full((2, LANES)), full((LANES, LANES)), tab, tab],
        out_specs=(row, row, row),
        out_shape=(jax.ShapeDtypeStruct((R_ALL, W_BRANCH), BF16), jax.ShapeDtypeStruct((R_ALL, W_BRANCH), BF16),
                   jax.ShapeDtypeStruct((R_ALL, W_BRANCH), F32)),
        compiler_params=_cparams(("arbitrary",)),
        name="diff_prep",
    )(p, p, gain2(lp['g_diff_qn']), gain2(lp['g_diff_kn']), jnp.asarray(perm, BF16), *tabs)


def _diff_attn_kernel(*refs, has_cache, lambda_init):
    if has_cache:
        dl_ref, q_ref, k_ref, v_ref, kc_ref, vc_ref, g_ref, o_ref = refs
    else:
        dl_ref, q_ref, k_ref, v_ref, g_ref, o_ref = refs
    lane = lax.broadcasted_iota(jnp.int32, (1, LANES), 1)
    dl = dl_ref[...]
    lam = (jnp.exp(jnp.sum(dl[0:1] * dl[1:2], axis=-1, keepdims=True))
           - jnp.exp(jnp.sum(dl[2:3] * dl[3:4], axis=-1, keepdims=True)) + lambda_init)
    units = [(h, w) for h in range(H_D) for w in range(2)]

    def scores(unit):
        h, w = unit
        sl = slice((2 * w + h // 2) * LANES, (2 * w + h // 2 + 1) * LANES)
        q = jnp.where((lane // DH_D) == (h % 2), q_ref[:, sl], jnp.zeros((), q_ref.dtype))
        s_o = lax.dot_general(q, k_ref[:, sl], _NT, preferred_element_type=F32)
        s_c = (lax.dot_general(q, kc_ref[0, 0, :, sl].astype(BF16), _NT, preferred_element_type=F32)
               if has_cache else None)
        return s_o, s_c

    outs = []
    nxt = scores(units[0])
    for n, (h, w) in enumerate(units):
        s_o, s_c = nxt
        if n + 1 < len(units):
            nxt = scores(units[n + 1])
        vs = slice(h * LANES, (h + 1) * LANES)
        m = jnp.max(s_o, axis=-1, keepdims=True)
        if has_cache:
            m = jnp.maximum(m, jnp.max(s_c, axis=-1, keepdims=True))
        e_o = jnp.exp(s_o - m)
        l = jnp.sum(e_o, axis=-1, keepdims=True)
        pv = jnp.dot(e_o.astype(BF16), v_ref[:, vs], preferred_element_type=F32)
        if has_cache:
            e_c = jnp.exp(s_c - m)
            l = l + jnp.sum(e_c, axis=-1, keepdims=True)
            pv = pv + jnp.dot(e_c.astype(BF16), vc_ref[0, 0, :, vs].astype(BF16), preferred_element_type=F32)
        outs.append(pv * (1.0 / l))
    for h in range(H_D):
        y = outs[2 * h] - lam * outs[2 * h + 1]
        r = lax.rsqrt(jnp.mean(y * y, axis=-1, keepdims=True) + EPS)
        o_ref[:, h * LANES:(h + 1) * LANES] = ((y * r * g_ref[...]) * (1.0 - lambda_init)).astype(o_ref.dtype)


def diff_attention(dl, qd, kd, p, kc, vc, g_sub, *, row0, n_seq, t_seq, tq, lambda_init, layer=0):
    has_cache = kc is not None
    nq = t_seq // tq
    qb0, kb0 = row0 // tq, row0 // t_seq
    in_specs = [pl.BlockSpec((4, DH_D), lambda s, i: (0, 0)),
                pl.BlockSpec((tq, W_BRANCH), lambda s, i: (qb0 + s * nq + i, 0)),
                pl.BlockSpec((t_seq, W_BRANCH), lambda s, i: (kb0 + s, 0)),
                pl.BlockSpec((t_seq, W_BRANCH), lambda s, i: (kb0 + s, _seg_block('dv')))]
    args = [dl, qd, kd, p]
    if has_cache:
        in_specs += [pl.BlockSpec((1, 1, PAST_LEN, W_BRANCH), lambda s, i: (s, layer, 0, 0)),
                     pl.BlockSpec((1, 1, PAST_LEN, W_BRANCH), lambda s, i: (s, layer, 0, 0))]
        args += [kc, vc]
    in_specs.append(pl.BlockSpec((1, LANES), lambda s, i: (0, 0)))
    args.append(g_sub.reshape(1, LANES))
    return pl.pallas_call(
        functools.partial(_diff_attn_kernel, has_cache=has_cache, lambda_init=lambda_init),
        grid=(n_seq, nq),
        in_specs=in_specs,
        out_specs=pl.BlockSpec((tq, W_BRANCH), lambda s, i: (s * nq + i, 0)),
        out_shape=jax.ShapeDtypeStruct((n_seq * t_seq, W_BRANCH), BF16),
        compiler_params=_cparams(("arbitrary", "arbitrary")),
        name="diff_attention",
    )(*args)


def _softplus(z):
    return jnp.maximum(z, 0.0) + jnp.log(1.0 + jnp.exp(-jnp.abs(z)))


def _gelu_tanh(x):
    return 0.5 * x * (1.0 + jnp.tanh(math.sqrt(2.0 / math.pi) * (x + 0.044715 * (x * x * x))))


def _rglru_kernel(x_ref, gate_ref, wc_ref, bc_ref, wg_ref, bg_ref, lam_ref, h0_ref, y_ref, st_ref,
                  af_s, uf_s, ab_s, ub_s, *, t_seq):
    t = t_seq
    x = x_ref[...].astype(F32)
    row = lax.broadcasted_iota(jnp.int32, (t, W_B), 0)
    wc = wc_ref[...]
    xc = (wc[0:1] * jnp.where(row >= 2, pltpu.roll(x, 2, 0), 0.0)
          + wc[1:2] * jnp.where(row >= 1, pltpu.roll(x, 1, 0), 0.0)
          + wc[2:3] * x
          + wc[3:4] * jnp.where(row < t - 1, pltpu.roll(x, t - 1, 0), 0.0)
          + bc_ref[...])
    gates = jnp.dot(xc.astype(BF16), wg_ref[...], preferred_element_type=F32) + bg_ref[...]
    for d, (a_s, u_s) in enumerate(((af_s, uf_s), (ab_s, ub_s))):
        rg = jax.nn.sigmoid(gates[:, (2 * d) * W_B:(2 * d + 1) * W_B])
        ig = jax.nn.sigmoid(gates[:, (2 * d + 1) * W_B:(2 * d + 2) * W_B])
        log_a = -RG_C * rg * _softplus(-lam_ref[d:d + 1, :])
        a = jnp.exp(log_a)
        a_s[...] = a
        u_s[...] = jnp.sqrt(-jnp.tanh(log_a) * (a * a + 1.0)) * (ig * xc)

    nblk = t // SUBLANES

    def body(k, carry):
        hf, hb = carry
        base_f = pl.multiple_of(k * SUBLANES, SUBLANES)
        base_b = pl.multiple_of((nblk - 1 - k) * SUBLANES, SUBLANES)
        for r in range(SUBLANES):
            rf = pl.ds(base_f + r, 1)
            rb = pl.ds(base_b + (SUBLANES - 1 - r), 1)
            hf = af_s[rf, :] * hf + uf_s[rf, :]
            hb = ab_s[rb, :] * hb + ub_s[rb, :]
            uf_s[rf, :] = hf
            ub_s[rb, :] = hb
        return hf, hb

    hf, hb = lax.fori_loop(0, nblk, body, (h0_ref[0, 0:1, :], h0_ref[0, 1:2, :]))
    st_ref[0, 0:1, :] = hf
    st_ref[0, 1:2, :] = hb
    y_ref[...] = (_gelu_tanh(gate_ref[...].astype(F32)) * (uf_s[...] + ub_s[...])).astype(y_ref.dtype)


def pack_rglru_weights(lp):
    def blockdiag(w):
        eye = jnp.eye(NB_B, dtype=w.dtype)
        return jnp.einsum('ncd,nm->ncmd', w, eye).reshape(W_B, W_B)
    wg = jnp.concatenate([blockdiag(lp['w_rg_a'][0]), blockdiag(lp['w_rg_x'][0]),
                          blockdiag(lp['w_rg_a'][1]), blockdiag(lp['w_rg_x'][1])], axis=1).astype(BF16)
    bg = jnp.concatenate([lp['b_rg_a'][0], lp['b_rg_x'][0], lp['b_rg_a'][1], lp['b_rg_x'][1]]).reshape(1, 4 * W_B)
    return dict(wg=wg, bg=bg, wc=lp['w_conv_rg'], bc=lp['b_conv_rg'].reshape(1, W_B), lam=lp['rg_lambda'])


def rglru(p, rw, h0, *, row0, n_seq, t_seq):
    rb0 = row0 // t_seq
    xb, gb = _seg_block('rg_x'), _seg_block('rg_gate')
    full = lambda shape: pl.BlockSpec(shape, lambda s: tuple(0 for _ in shape))
    in_specs = [pl.BlockSpec((t_seq, W_B), lambda s: (rb0 + s, xb)),
                pl.BlockSpec((t_seq, W_B), lambda s: (rb0 + s, gb)),
                full((CONV_W, W_B)), full((1, W_B)), full((W_B, 4 * W_B)), full((1, 4 * W_B)), full((2, W_B)),
                pl.BlockSpec((1, 2, W_B), lambda s: (s, 0, 0))]
    args = [p, p, rw['wc'], rw['bc'], rw['wg'], rw['bg'], rw['lam'], h0]
    return pl.pallas_call(
        functools.partial(_rglru_kernel, t_seq=t_seq),
        grid=(n_seq,),
        in_specs=in_specs,
        out_specs=(pl.BlockSpec((t_seq, W_B), lambda s: (s, 0)), pl.BlockSpec((1, 2, W_B), lambda s: (s, 0, 0))),
        out_shape=(jax.ShapeDtypeStruct((n_seq * t_seq, W_B), BF16), jax.ShapeDtypeStruct((n_seq, 2, W_B), F32)),
        scratch_shapes=[pltpu.VMEM((t_seq, W_B), F32)] * 4,
        compiler_params=_cparams(("arbitrary",)),
        name="rglru",
    )(*args)


def _dot_split(a, b_bf16):
    hi = a.astype(BF16)
    lo = (a - hi.astype(F32)).astype(BF16)
    return (jnp.dot(hi, b_bf16, preferred_element_type=F32) + jnp.dot(lo, b_bf16, preferred_element_type=F32))


def _log_sigmoid(z):
    return jnp.minimum(z, 0.0) - jnp.log(1.0 + jnp.exp(-jnp.abs(z)))


_TN = (((0,), (0,)), ((), ()))


def _mlstm_kernel(q_ref, k_ref, v_ref, o_ref, sm_ref, bias_ref, g_ref, c0_ref, n0_ref, m0_ref,
                  y_ref, c_out, n_out, m_out, hm_s, c_s, *, t_seq):
    L = ML_CHUNK
    nchunk = t_seq // L
    scale = DH_C ** -0.5
    ri = lax.broadcasted_iota(jnp.int32, (L, L), 0)
    ci = lax.broadcasted_iota(jnp.int32, (L, L), 1)
    lane1 = lax.broadcasted_iota(jnp.int32, (L, LANES), 1)
    ones_col = jnp.where(lane1 == 0, 1.0, 0.0).astype(BF16)
    bias = bias_ref[...]

    for d in range(2):
        causal = (ci <= ri) if d == 0 else (ci >= ri)
        tri = jnp.where(causal, 1.0, 0.0).astype(BF16)
        tri_t = jnp.where((ri <= ci) if d == 0 else (ri >= ci), 1.0, 0.0).astype(BF16)
        for h in range(H_C):
            c_s[h, :, 0:DH_C] = c0_ref[0, 0, d, h]
            c_s[h, :, DH_C:2 * DH_C] = jnp.where(lane1 == 0, n0_ref[0, 0, d, h], 0.0)
        m_init = tuple(m0_ref[0, :, d * H_C + h:d * H_C + h + 1] for h in range(H_C))

        def chunk(kk, ms, d=d, causal=causal, tri=tri, tri_t=tri_t):
            cidx = kk if d == 0 else nchunk - 1 - kk
            rows = pl.ds(pl.multiple_of(cidx * L, L), L)
            gsm = sm_ref[rows, :].astype(F32) + bias
            lf_all = _log_sigmoid(gsm)
            cum_cols = _dot_split_left(tri, lf_all)
            g_t = gsm.T
            cum_rows = _dot_split(lf_all.T, tri_t)
            heads = range(H_C)
            sls = [slice(h * DH_C, (h + 1) * DH_C) for h in heads]
            qs = [q_ref[rows, sl] for sl in sls]
            ks = [k_ref[rows, sl] for sl in sls]
            v_augs = [jnp.concatenate([v_ref[rows, sl], ones_col], axis=1) for sl in sls]
            qk = [lax.dot_general(qs[h], ks[h], _NT, preferred_element_type=F32) for h in heads]
            qc = [jnp.dot(qs[h], c_s[h].astype(BF16), preferred_element_type=F32) * scale for h in heads]
            cum_c = [cum_cols[:, SM_MF + d * H_C + h:SM_MF + d * H_C + h + 1] for h in heads]
            li_c = [gsm[:, SM_MI + d * H_C + h:SM_MI + d * H_C + h + 1] for h in heads]
            m_row, s = [], []
            for h in heads:
                jl, jf = SM_MI + d * H_C + h, SM_MF + d * H_C + h
                log_d = jnp.where(causal, cum_c[h] - cum_rows[jf:jf + 1, :] + g_t[jl:jl + 1, :], -jnp.inf)
                m_row.append(jnp.maximum(cum_c[h] + ms[h], jnp.max(log_d, axis=-1, keepdims=True)))
                s.append(qk[h] * (scale * jnp.exp(log_d - m_row[h])))
            sv = [jnp.dot(s[h].astype(BF16), v_augs[h], preferred_element_type=F32) for h in heads]
            new_ms = []
            for h in heads:
                w_inter = jnp.exp(cum_c[h] + ms[h] - m_row[h])
                nd = sv[h] + qc[h] * w_inter
                den = jnp.maximum(jnp.abs(nd[:, DH_C:DH_C + 1]), jnp.exp(-m_row[h]))
                h_out = nd[:, :DH_C] * (1.0 / den)
                if d == 0:
                    hm_s[rows, sls[h]] = h_out
                else:
                    hm_s[rows, sls[h]] = hm_s[rows, sls[h]] + h_out
                last = cum_c[h][L - 1:L, :] if d == 0 else cum_c[h][0:1, :]
                w_s = last - cum_c[h] + li_c[h]
                m_new = jnp.maximum(last + ms[h], jnp.max(w_s, axis=0, keepdims=True))
                decay = jnp.exp(last + ms[h] - m_new)
                kw_t = (ks[h].astype(F32) * jnp.exp(w_s - m_new)).T.astype(BF16)
                c_s[h] = decay * c_s[h] + jnp.dot(kw_t, v_augs[h], preferred_element_type=F32)
                new_ms.append(m_new)
            return tuple(new_ms)

        m_fin = lax.fori_loop(0, nchunk, chunk, m_init)
        for h in range(H_C):
            c_out[0, d, h] = c_s[h, :, 0:DH_C]
            n_out[0, d, h] = c_s[h, :, DH_C:DH_C + 1]
            m_out[0, :, d * H_C + h:d * H_C + h + 1] = m_fin[h]

    for h in range(H_C):
        sl = slice(h * DH_C, (h + 1) * DH_C)
        hm = hm_s[:, sl]
        r = lax.rsqrt(jnp.mean(hm * hm, axis=-1, keepdims=True) + EPS)
        y_ref[:, sl] = (jax.nn.sigmoid(o_ref[:, sl].astype(F32)) * (hm * r * g_ref[...])).astype(y_ref.dtype)


def _dot_split_left(a_bf16, b):
    hi = b.astype(BF16)
    lo = (b - hi.astype(F32)).astype(BF16)
    return (jnp.dot(a_bf16, hi, preferred_element_type=F32) + jnp.dot(a_bf16, lo, preferred_element_type=F32))


def mlstm(p, bias_sm, g_out, c0, n0, m0, *, row0, n_seq, t_seq, layer=0):
    rb0 = row0 // t_seq
    seg = lambda nm: pl.BlockSpec((t_seq, W_BRANCH), lambda s, b=_seg_block(nm): (rb0 + s, b))
    full = lambda shape: pl.BlockSpec(shape, lambda s: tuple(0 for _ in shape))
    c_spec = pl.BlockSpec((1, 2, H_C, DH_C, DH_C), lambda s: (s, 0, 0, 0, 0))
    n_spec = pl.BlockSpec((1, 2, H_C, DH_C, 1), lambda s: (s, 0, 0, 0, 0))
    c_in = pl.BlockSpec((1, 1, 2, H_C, DH_C, DH_C), lambda s: (s, layer, 0, 0, 0, 0))
    n_in = pl.BlockSpec((1, 1, 2, H_C, DH_C, 1), lambda s: (s, layer, 0, 0, 0, 0))
    m_spec = pl.BlockSpec((1, 1, 2 * H_C), lambda s: (s, 0, 0))
    in_specs = [seg('mq'), seg('mk'), seg('mv'), seg('mo'),
                pl.BlockSpec((t_seq, LANES), lambda s: (rb0 + s, P_SMALL // LANES)),
                full((1, LANES)), full((1, DH_C)), c_in, n_in, m_spec]
    args = [p, p, p, p, p, bias_sm, g_out.reshape(1, DH_C), c0, n0, m0]
    return pl.pallas_call(
        functools.partial(_mlstm_kernel, t_seq=t_seq),
        grid=(n_seq,),
        in_specs=in_specs,
        out_specs=(pl.BlockSpec((t_seq, W_BRANCH), lambda s: (s, 0)), c_spec, n_spec, m_spec),
        out_shape=(jax.ShapeDtypeStruct((n_seq * t_seq, W_BRANCH), BF16),
                   jax.ShapeDtypeStruct((n_seq, 2, H_C, DH_C, DH_C), F32),
                   jax.ShapeDtypeStruct((n_seq, 2, H_C, DH_C, 1), F32),
                   jax.ShapeDtypeStruct((n_seq, 1, 2 * H_C), F32)),
        scratch_shapes=[pltpu.VMEM((t_seq, W_BRANCH), F32), pltpu.VMEM((H_C, DH_C, 2 * DH_C), F32)],
        compiler_params=_cparams(("arbitrary",)),
        name="mlstm",
    )(*args)


MERGE_TM = 512


def _merge_kernel(*refs):
    ctx_refs, lat_refs, gate_refs = refs[0:4], refs[4:8], refs[8:12]
    x_ref, mod_ref, wbr_ref, wout_ref, gn_ref, xo_ref, xn_ref = refs[12:19]
    xn3_ref = refs[19] if len(refs) > 19 else None
    is_ctx = pl.program_id(0) < R_CTX // MERGE_TM
    merged = None
    for g in range(N_BRANCH):
        yg = jnp.where(is_ctx, ctx_refs[g][...], lat_refs[g][...])
        pg = jnp.dot(yg, wbr_ref[g], preferred_element_type=F32)
        term = jax.nn.sigmoid(gate_refs[g][...].astype(F32)) * pg
        merged = term if merged is None else merged + term
    y = jnp.dot(merged.astype(BF16), wout_ref[...], preferred_element_type=F32)
    x = x_ref[...] + mod_ref[0, 2:3, :] * y
    xo_ref[...] = x
    r = lax.rsqrt(jnp.mean(x * x, axis=-1, keepdims=True) + EPS)
    xn = (x * r * gn_ref[...]) * (1.0 + mod_ref[0, 4:5, :]) + mod_ref[0, 3:4, :]
    xn_ref[...] = xn.astype(xn_ref.dtype)
    if xn3_ref is not None:
        _rows_to_tiles(xn3_ref, xn.astype(BF16))


def merge(ys_ctx, ys_lat, p, x, mod, wbr, wout, g_ffn, rows_as_tiles):
    tm = MERGE_TM
    n_ctx_tiles = R_CTX // tm
    br_ctx = pl.BlockSpec((tm, W_BRANCH), lambda i: (jnp.minimum(i, n_ctx_tiles - 1), 0))
    br_lat = pl.BlockSpec((tm, W_BRANCH), lambda i: (jnp.maximum(i - n_ctx_tiles, 0), 0))
    gate = lambda g: pl.BlockSpec((tm, D_MODEL), lambda i, g=g: (i, g))
    row = pl.BlockSpec((tm, D_MODEL), lambda i: (i, 0))
    out_specs = [row, row]
    out_shape = [jax.ShapeDtypeStruct((R_ALL, D_MODEL), F32), jax.ShapeDtypeStruct((R_ALL, D_MODEL), BF16)]
    if rows_as_tiles:
        out_specs.append(pl.BlockSpec((tm, ROW_SUB, LANES), lambda i: (i, 0, 0)))
        out_shape.append(jax.ShapeDtypeStruct((R_ALL, ROW_SUB, LANES), F32))
    return pl.pallas_call(
        _merge_kernel,
        grid=(R_ALL // tm,),
        in_specs=[br_ctx] * N_BRANCH + [br_lat] * N_BRANCH + [gate(0), gate(1), gate(2), gate(3), row,
                  pl.BlockSpec((1, 8, D_MODEL), lambda i: (_mod_row_of_tile(i, tm), 0, 0)),
                  pl.BlockSpec((N_BRANCH, W_BRANCH, D_MODEL), lambda i: (0, 0, 0)),
                  pl.BlockSpec((D_MODEL, D_MODEL), lambda i: (0, 0)),
                  pl.BlockSpec((1, D_MODEL), lambda i: (0, 0))],
        out_specs=tuple(out_specs),
        out_shape=tuple(out_shape),
        compiler_params=_cparams(("arbitrary",)),
        name="merge",
    )(*ys_ctx, *ys_lat, p, p, p, p, x, mod, wbr, wout, g_ffn.reshape(1, D_MODEL))


def _new_expert(te_ref, i):
    return jnp.logical_or(i == 0, te_ref[i] != te_ref[jnp.maximum(i - 1, 0)])


def _ffn_up_kernel(te_ref, nt_ref, x_ref, wg_ref, wu_ref, h_ref, wgb_ref, wub_ref):
    i = pl.program_id(1)

    @pl.when(_new_expert(te_ref, i))
    def _():
        wgb_ref[...] = wg_ref[0].astype(BF16)
        wub_ref[...] = wu_ref[0].astype(BF16)

    @pl.when(i < nt_ref[0])
    def _():
        x = x_ref[...]
        g = jnp.dot(x, wgb_ref[...], preferred_element_type=F32)
        u = jnp.dot(x, wub_ref[...], preferred_element_type=F32)
        h_ref[...] = (g * jax.nn.sigmoid(g) * u).astype(h_ref.dtype)

    @pl.when(i >= nt_ref[0])
    def _():
        h_ref[...] = jnp.zeros(h_ref.shape, h_ref.dtype)


def ffn_up(tile_expert, n_tiles, xs, wg, wu, tm, tf, weight_buffers):
    r, d = xs.shape
    f = wg.shape[2]
    w_spec = pl.BlockSpec((1, d, tf), lambda j, i, te, nt: (te[i], 0, j), pipeline_mode=pl.Buffered(weight_buffers))
    return pl.pallas_call(
        _ffn_up_kernel,
        grid_spec=pltpu.PrefetchScalarGridSpec(
            num_scalar_prefetch=2,
            grid=(f // tf, r // tm),
            in_specs=[pl.BlockSpec((tm, d), lambda j, i, te, nt: (i, 0)), w_spec, w_spec],
            out_specs=pl.BlockSpec((tm, tf), lambda j, i, te, nt: (i, j)),
            scratch_shapes=[pltpu.VMEM((d, tf), BF16), pltpu.VMEM((d, tf), BF16)]),
        out_shape=jax.ShapeDtypeStruct((r, f), BF16),
        compiler_params=_cparams(("arbitrary", "arbitrary")),
        name="ffn_up",
    )(tile_expert, n_tiles, xs, wg, wu)


def _ffn_down_kernel(te_ref, nt_ref, h_ref, wd_ref, y_ref, wdb_ref):
    i = pl.program_id(0)

    @pl.when(_new_expert(te_ref, i))
    def _():
        wdb_ref[...] = wd_ref[0].astype(BF16)

    @pl.when(i < nt_ref[0])
    def _():
        _rows_to_tiles(y_ref, jnp.dot(h_ref[...], wdb_ref[...], preferred_element_type=F32))

    @pl.when(i >= nt_ref[0])
    def _():
        y_ref[...] = jnp.zeros(y_ref.shape, y_ref.dtype)


def ffn_down(tile_expert, n_tiles, h, wd, tm):
    r, f = h.shape
    d = wd.shape[2]
    return pl.pallas_call(
        _ffn_down_kernel,
        grid_spec=pltpu.PrefetchScalarGridSpec(
            num_scalar_prefetch=2,
            grid=(r // tm,),
            in_specs=[pl.BlockSpec((tm, f), lambda i, te, nt: (i, 0)),
                      pl.BlockSpec((1, f, d), lambda i, te, nt: (te[i], 0, 0))],
            out_specs=pl.BlockSpec((tm, ROW_SUB, LANES), lambda i, te, nt: (i, 0, 0)),
            scratch_shapes=[pltpu.VMEM((f, d), BF16)]),
        out_shape=jax.ShapeDtypeStruct((r, ROW_SUB, LANES), F32),
        compiler_params=_cparams(("arbitrary",)),
        name="ffn_down",
    )(tile_expert, n_tiles, h, wd)


def _ffn_down_res_kernel(h_ref, wd_ref, x_ref, mod_ref, y_ref, wdb_ref):
    @pl.when(pl.program_id(0) == 0)
    def _():
        wdb_ref[...] = wd_ref[...].astype(BF16)

    y = jnp.dot(h_ref[...], wdb_ref[...], preferred_element_type=F32)
    y_ref[...] = x_ref[...] + mod_ref[0, 5:6, :] * y


def ffn_down_residual(h, wd, x, mod):
    tm = 1024
    r, f = h.shape
    d = wd.shape[1]
    return pl.pallas_call(
        _ffn_down_res_kernel,
        grid=(r // tm,),
        in_specs=[pl.BlockSpec((tm, f), lambda i: (i, 0)),
                  pl.BlockSpec((f, d), lambda i: (0, 0), pipeline_mode=pl.Buffered(1)),
                  pl.BlockSpec((tm, d), lambda i: (i, 0)),
                  pl.BlockSpec((1, 8, d), lambda i: (_mod_row_of_tile(i, tm), 0, 0))],
        out_specs=pl.BlockSpec((tm, d), lambda i: (i, 0)),
        out_shape=jax.ShapeDtypeStruct((r, d), F32),
        scratch_shapes=[pltpu.VMEM((f, d), BF16)],
        compiler_params=_cparams(("arbitrary",)),
        name="ffn_down_residual",
    )(h, wd, x, mod)


def dense_swiglu_residual(xn, x, mod, wg, wu, wd):
    t = xn.shape[0]
    tm = 1024
    n_tiles = t // tm
    te = jnp.zeros((n_tiles,), jnp.int32)
    nt = jnp.full((1,), n_tiles, jnp.int32)
    h = ffn_up(te, nt, xn, wg[None], wu[None], tm=tm, tf=wg.shape[1] // 2, weight_buffers=1)
    return ffn_down_residual(h, wd, x, mod)


MOE_TM = 512
DISPATCH_TM = 512


def _moe_routing(logits, tm):
    t = logits.shape[0]
    n_assign = t * TOP_K
    top_v, top_i = lax.top_k(logits, TOP_K)
    gate = jax.nn.softmax(top_v, axis=-1)
    flat_e = top_i.reshape(-1).astype(jnp.int32)
    onehot = (flat_e[:, None] == jnp.arange(N_EXP, dtype=jnp.int32)[None, :])
    blk = LANES
    oh = onehot.astype(F32).reshape(n_assign // blk, blk, N_EXP)
    tril = jnp.tril(jnp.ones((blk, blk), F32))
    within = jnp.einsum('ij,bjk->bik', tril, oh)
    blk_tot = within[:, -1, :]
    blk_off = jnp.cumsum(blk_tot, axis=0) - blk_tot
    csum = (within + blk_off[:, None, :]).reshape(n_assign, N_EXP)
    rank = jnp.sum(jnp.where(onehot, csum - 1.0, 0.0), axis=1).astype(jnp.int32)
    counts = csum[-1].astype(jnp.int32)
    padded = (counts + tm - 1) // tm * tm
    grp_start = jnp.cumsum(padded) - padded
    raw_start = jnp.cumsum(counts) - counts
    slot_of_assign = jnp.sum(jnp.where(onehot, grp_start[None, :], 0), axis=1) + rank

    r_max = n_assign + N_EXP * tm
    tile_start = jnp.arange(r_max // tm, dtype=jnp.int32) * tm
    tile_expert = jnp.sum((tile_start[:, None] >= (grp_start + padded)[None, :]).astype(jnp.int32), axis=1)
    tile_expert = jnp.minimum(tile_expert, N_EXP - 1).astype(jnp.int32)
    n_tiles = (jnp.sum(padded) // tm).astype(jnp.int32).reshape(1)

    order = jnp.argsort(flat_e, stable=True).astype(jnp.int32)
    e_slot = jnp.repeat(tile_expert, tm)
    j = jnp.arange(r_max, dtype=jnp.int32) - grp_start[e_slot]
    src = jnp.clip(raw_start[e_slot] + j, 0, n_assign - 1)
    tok_of_slot = jnp.where(j < counts[e_slot], order[src] // TOP_K, 0)
    return gate, slot_of_assign, tok_of_slot, tile_expert, n_tiles


ROW_SUB = D_MODEL // LANES


def _rows_to_tiles(o3_ref, x):
    for j in range(ROW_SUB):
        o3_ref[:, j, :] = x[:, j * LANES:(j + 1) * LANES].astype(o3_ref.dtype)


def _start_row_gather(idx_ref, src_ref, dst, sem, n_rows):
    def body(q, carry):
        for u in range(2):
            r = 2 * q + u
            pltpu.make_async_copy(src_ref.at[idx_ref[0, 0, r]], dst.at[r], sem).start(priority=u)
        return carry

    lax.fori_loop(0, n_rows // 2, body, 0, unroll=4)


def _wait_row_gather(src_ref, dst, sem, n_rows):
    pltpu.make_async_copy(src_ref.at[pl.ds(0, n_rows)], dst, sem).wait()


def _tiles_to_rows(tiles, rows_ref):
    for j in range(ROW_SUB):
        rows_ref[:, j * LANES:(j + 1) * LANES] = tiles[:, j, :]


def _dispatch_kernel(nt_ref, idx_ref, idx_next_ref, src_ref, o_ref, buf, rows, sem):
    i = pl.program_id(0)
    nt = nt_ref[0]
    tm = o_ref.shape[0]
    slot = i % 2

    @pl.when(i == 0)
    def _():
        _start_row_gather(idx_ref, src_ref, buf.at[0], sem.at[0], tm)

    @pl.when(i + 1 < nt)
    def _():
        _start_row_gather(idx_next_ref, src_ref, buf.at[1 - slot], sem.at[1 - slot], tm)

    @pl.when(i < nt)
    def _():
        _wait_row_gather(src_ref, buf.at[slot], sem.at[slot], tm)
        _tiles_to_rows(buf.at[slot], rows)
        o_ref[...] = rows[...].astype(o_ref.dtype)

    @pl.when(i >= nt)
    def _():
        o_ref[...] = jnp.zeros(o_ref.shape, o_ref.dtype)


def moe_dispatch(n_tiles, tok_of_slot, xn3, tm):
    r = tok_of_slot.shape[0]
    last = r // tm - 1
    idx = tok_of_slot.reshape(r // tm, 1, tm)
    return pl.pallas_call(
        _dispatch_kernel,
        grid_spec=pltpu.PrefetchScalarGridSpec(
            num_scalar_prefetch=1,
            grid=(r // tm,),
            in_specs=[pl.BlockSpec((1, 1, tm), lambda i, nt: (i, 0, 0), memory_space=pltpu.SMEM),
                      pl.BlockSpec((1, 1, tm), lambda i, nt: (jnp.minimum(i + 1, last), 0, 0),
                                   memory_space=pltpu.SMEM),
                      pl.BlockSpec(memory_space=pl.ANY)],
            out_specs=pl.BlockSpec((tm, D_MODEL), lambda i, nt: (i, 0)),
            scratch_shapes=[pltpu.VMEM((2, tm, ROW_SUB, LANES), F32), pltpu.VMEM((tm, D_MODEL), F32),
                            pltpu.SemaphoreType.DMA((2,))]),
        out_shape=jax.ShapeDtypeStruct((r, D_MODEL), BF16),
        compiler_params=_cparams(("arbitrary",)),
        name="moe_dispatch",
    )(n_tiles, idx, idx, xn3)


COMBINE_TM = 256


def _combine_kernel(idx_ref, idx_next_ref, ys_ref, x_ref, gate_ref, mod_ref, oc_ref, ol_ref, buf, rows, sem):
    i = pl.program_id(0)
    tm = oc_ref.shape[0]
    n_rows = TOP_K * tm
    slot = i % 2

    @pl.when(i == 0)
    def _():
        _start_row_gather(idx_ref, ys_ref, buf.at[0], sem.at[0], n_rows)

    @pl.when(i + 1 < pl.num_programs(0))
    def _():
        _start_row_gather(idx_next_ref, ys_ref, buf.at[1 - slot], sem.at[1 - slot], n_rows)

    _wait_row_gather(ys_ref, buf.at[slot], sem.at[slot], n_rows)
    _tiles_to_rows(buf.at[slot], rows)
    f = gate_ref[:, 0:1] * rows[0:tm, :] + gate_ref[:, 1:2] * rows[tm:2 * tm, :]
    res = x_ref[...] + mod_ref[0, 5:6, :] * f

    @pl.when(i < R_CTX // COMBINE_TM)
    def _():
        oc_ref[...] = res

    @pl.when(i >= R_CTX // COMBINE_TM)
    def _():
        ol_ref[...] = res


def moe_combine(slot_of_assign, ys3, x, gate, mod):
    t, d = x.shape
    tm = COMBINE_TM
    idx = slot_of_assign.reshape(t // tm, tm, TOP_K).transpose(0, 2, 1).reshape(t // tm, 1, TOP_K * tm)
    last = t // tm - 1
    n_ctx = R_CTX // tm
    return pl.pallas_call(
        _combine_kernel,
        grid=(t // tm,),
        in_specs=[pl.BlockSpec((1, 1, TOP_K * tm), lambda i: (i, 0, 0), memory_space=pltpu.SMEM),
                  pl.BlockSpec((1, 1, TOP_K * tm), lambda i: (jnp.minimum(i + 1, last), 0, 0),
                               memory_space=pltpu.SMEM),
                  pl.BlockSpec(memory_space=pl.ANY),
                  pl.BlockSpec((tm, d), lambda i: (i, 0)),
                  pl.BlockSpec((tm, TOP_K), lambda i: (i, 0)),
                  pl.BlockSpec((1, 8, d), lambda i: (_mod_row_of_tile(i, tm), 0, 0))],
        out_specs=(pl.BlockSpec((tm, d), lambda i: (jnp.minimum(i, n_ctx - 1), 0)),
                   pl.BlockSpec((tm, d), lambda i: (jnp.maximum(i - n_ctx, 0), 0))),
        out_shape=(jax.ShapeDtypeStruct((R_CTX, d), F32), jax.ShapeDtypeStruct((t - R_CTX, d), F32)),
        scratch_shapes=[pltpu.VMEM((2, TOP_K * tm, ROW_SUB, LANES), F32), pltpu.VMEM((TOP_K * tm, d), F32),
                        pltpu.SemaphoreType.DMA((2,))],
        compiler_params=_cparams(("arbitrary",)),
        name="moe_combine",
    )(idx, idx, ys3, x, gate, mod)


def moe_swiglu_residual(xn3, xn, x, mod, w_router, wg, wu, wd):
    tm = MOE_TM
    logits = jnp.dot(xn.astype(F32), w_router, precision=lax.Precision.HIGHEST)
    gate, slot_of_assign, tok_of_slot, tile_expert, n_tiles = _moe_routing(logits, tm)
    xs = moe_dispatch(n_tiles * (tm // DISPATCH_TM), tok_of_slot, xn3, DISPATCH_TM)
    h = ffn_up(tile_expert, n_tiles, xs, wg, wu, tm=tm, tf=wg.shape[2] // 2, weight_buffers=2)
    ys3 = ffn_down(tile_expert, n_tiles, h, wd, tm=tm)
    return moe_combine(slot_of_assign, ys3, x, gate, mod)


def _layer(x, cond, lp, l, ctx, tabs_a, tabs_d):
    mod = modulation(cond, lp['w_mod'], lp['b_mod'], l).reshape(cond.shape[0], 6, D_MODEL)
    mod = jnp.pad(mod, ((0, 0), (0, 2), (0, 0)))
    p = in_proj(x, mod, lp['g_norm_mix'], lp['w_in_p'], l)

    mw = pack_mla_weights(lp)
    q_a, k_a, v_a, ckv, kr = mla_prep(p, mw, tabs_a)
    kc_a, vc_a = mla_prep_cache(ctx['mla_ckv'], ctx['mla_krope'], mw, l)
    ya_c = mla_attention(q_a, k_a, v_a, None, None, row0=0, n_seq=N_CTX_SEQ, t_seq=T_CTX, tq=T_CTX)
    ya_l = mla_attention(q_a, k_a, v_a, kc_a, vc_a, row0=R_CTX, n_seq=N_LAT_SEQ, t_seq=T_LAT, tq=ATTN_TQ)

    rw = pack_rglru_weights(lp)
    yb_c, st_rg = rglru(p, rw, jnp.zeros((N_CTX_SEQ, 2, W_B), F32), row0=0, n_seq=N_CTX_SEQ, t_seq=T_CTX)
    yb_l, _ = rglru(p, rw, ctx['rglru'], row0=R_CTX, n_seq=N_LAT_SEQ, t_seq=T_LAT)

    bias_sm = jnp.zeros((LANES,), F32).at[SM_MI:SM_MI + 2 * H_C].set(lp['b_ml_i'].reshape(-1))
    bias_sm = bias_sm.at[SM_MF:SM_MF + 2 * H_C].set(lp['b_ml_f'].reshape(-1)).reshape(1, LANES)
    c0_ctx = jnp.zeros((N_CTX_SEQ, 1, 2, H_C, DH_C, DH_C), F32)
    n0_ctx = jnp.zeros((N_CTX_SEQ, 1, 2, H_C, DH_C, 1), F32)
    m0_ctx = jnp.zeros((N_CTX_SEQ, 1, 2 * H_C), F32)
    m0_lat = ctx['mlstm_m'][:, l].reshape(N_LAT_SEQ, 1, 2 * H_C)
    yc_c, c_fin, n_fin, m_fin = mlstm(p, bias_sm, lp['g_ml_out'], c0_ctx, n0_ctx, m0_ctx,
                                      row0=0, n_seq=N_CTX_SEQ, t_seq=T_CTX)
    yc_l, _, _, _ = mlstm(p, bias_sm, lp['g_ml_out'], ctx['mlstm_C'], ctx['mlstm_n'][..., None], m0_lat,
                          row0=R_CTX, n_seq=N_LAT_SEQ, t_seq=T_LAT, layer=l)

    lambda_init = 0.8 - 0.6 * math.exp(-0.3 * l)
    qd, kd_own, kd_plain = diff_prep(p, lp, tabs_d)
    yd_c = diff_attention(lp['diff_lambda'], qd, kd_own, p, None, None, lp['g_diff_sub'],
                          row0=0, n_seq=N_CTX_SEQ, t_seq=T_CTX, tq=T_CTX, lambda_init=lambda_init)
    yd_l = diff_attention(lp['diff_lambda'], qd, kd_own, p, ctx['diff_k'], ctx['diff_v'], lp['g_diff_sub'],
                          row0=R_CTX, n_seq=N_LAT_SEQ, t_seq=T_LAT, tq=ATTN_TQ, lambda_init=lambda_init, layer=l)

    merged = merge((ya_c, yb_c, yc_c, yd_c), (ya_l, yb_l, yc_l, yd_l), p, x, mod, lp['w_br'].astype(BF16),
                   lp['w_out'].astype(BF16), lp['g_norm_ffn'], rows_as_tiles='moe' in lp)
    if 'ffn' in lp:
        x, xn = merged
        x = dense_swiglu_residual(xn, x, mod, *lp['ffn'])
    else:
        x, xn, xn3 = merged
        x = moe_swiglu_residual(xn3, xn, x, mod, *lp['moe'])

    dv0 = _seg_block('dv') * W_BRANCH
    ctx_out = (ckv[:R_CTX].reshape(N_CTX_SEQ, T_CTX, KV_RANK),
               kr[:R_CTX, :ROPE_A].reshape(N_CTX_SEQ, T_CTX, ROPE_A),
               kd_plain[:R_CTX].reshape(N_CTX_SEQ, T_CTX, 2, H_D, DH_D),
               p[:R_CTX, dv0:dv0 + W_BRANCH].astype(F32).reshape(N_CTX_SEQ, T_CTX, H_D, 2 * DH_D),
               st_rg,
               c_fin,
               n_fin.reshape(N_CTX_SEQ, 2, H_C, DH_C),
               m_fin.reshape(N_CTX_SEQ, 2, H_C))
    return x, ctx_out


def kernel(x_prompt, x_sample, cache_mla_ckv, cache_mla_krope, cache_diff_k, cache_diff_v,
           state_rglru, state_mlstm_C, state_mlstm_n, state_mlstm_m, c, c_ctx,
           w_mod, b_mod, g_norm_mix, g_norm_ffn, w_in, g_mla_qlat, w_mla_uq, g_mla_kvlat, w_mla_ukv,
           g_mla_qn, g_mla_kn, w_conv_rg, b_conv_rg, w_rg_a, b_rg_a, w_rg_x, b_rg_x, rg_lambda,
           b_ml_i, b_ml_f, g_ml_out, g_diff_qn, g_diff_kn, diff_lambda, g_diff_sub, w_br, w_out,
           w_ffn_gate, w_ffn_up, w_ffn_down, w_router, w_moe_gate, w_moe_up, w_moe_down):
    assert x_prompt.shape == (N_CTX_SEQ, T_CTX, D_MODEL) and x_sample.shape == (N_LAT_SEQ, T_LAT, D_MODEL)
    tabs_a = _rope_tables(ROPE_A, (NOPE_A,), MLA_TM)
    tabs_d = _rope_tables(DH_D, (0, DH_D), DIFF_TM)
    cond = jnp.concatenate([c_ctx.reshape(1, D_MODEL), c, jnp.zeros((16 - 1 - N_LAT_SEQ, D_MODEL), F32)], axis=0)
    x = jnp.concatenate([x_prompt.reshape(R_CTX, D_MODEL), x_sample.reshape(R_LAT, D_MODEL)], axis=0)
    w_in_p = pack_w_in(w_in)
    krope_pad = jnp.pad(cache_mla_krope, ((0, 0), (0, 0), (0, 0), (0, LANES - ROPE_A)))
    diff_k_rows = cache_diff_k.reshape(N_LAT_SEQ, DEPTH, PAST_LEN, W_BRANCH)
    diff_v_rows = cache_diff_v.reshape(N_LAT_SEQ, DEPTH, PAST_LEN, W_BRANCH)
    new = []
    for l in range(DEPTH):
        lp = dict(w_mod=w_mod, b_mod=b_mod, g_norm_mix=g_norm_mix[l], g_norm_ffn=g_norm_ffn[l], w_in_p=w_in_p,
                  g_mla_qlat=g_mla_qlat[l], w_mla_uq=w_mla_uq[l], g_mla_kvlat=g_mla_kvlat[l], w_mla_ukv=w_mla_ukv[l],
                  g_mla_qn=g_mla_qn[l], g_mla_kn=g_mla_kn[l], w_conv_rg=w_conv_rg[l], b_conv_rg=b_conv_rg[l],
                  w_rg_a=w_rg_a[l], b_rg_a=b_rg_a[l], w_rg_x=w_rg_x[l], b_rg_x=b_rg_x[l], rg_lambda=rg_lambda[l],
                  b_ml_i=b_ml_i[l], b_ml_f=b_ml_f[l], g_ml_out=g_ml_out[l], g_diff_qn=g_diff_qn[l],
                  g_diff_kn=g_diff_kn[l], diff_lambda=diff_lambda[l], g_diff_sub=g_diff_sub[l],
                  w_br=w_br[l], w_out=w_out[l])
        if l % 2 == 0:
            lp['ffn'] = (w_ffn_gate[l // 2], w_ffn_up[l // 2], w_ffn_down[l // 2])
        else:
            lp['moe'] = (w_router[l // 2], w_moe_gate[l // 2], w_moe_up[l // 2], w_moe_down[l // 2])
        ctx_l = dict(mla_ckv=cache_mla_ckv, mla_krope=krope_pad, diff_k=diff_k_rows, diff_v=diff_v_rows,
                     rglru=state_rglru[:, l], mlstm_C=state_mlstm_C, mlstm_n=state_mlstm_n, mlstm_m=state_mlstm_m)
        if isinstance(x, tuple):
            x = jnp.concatenate(x, axis=0)
        x, st = _layer(x, cond, lp, l, ctx_l, tabs_a, tabs_d)
        new.append(st)
    outs = tuple(jnp.stack([s[i] for s in new], axis=1) for i in range(8))
    x_ctx, x_lat = x if isinstance(x, tuple) else (x[:R_CTX], x[R_CTX:])
    return (x_ctx.reshape(N_CTX_SEQ, T_CTX, D_MODEL), x_lat.reshape(N_LAT_SEQ, T_LAT, D_MODEL)) + outs
```

```python
import functools
import math

import jax
import jax.numpy as jnp
import numpy as np
from jax import lax
from jax.experimental import pallas as pl
from jax.experimental.pallas import tpu as pltpu

D_MODEL = 1024
DEPTH = 2
GRID_W = 64
ROPE_BASE = 10000.0
EPS = 1e-6
N_BRANCH = 4
W_BRANCH = D_MODEL // 2

H_A = 8
NOPE_A = 64
ROPE_A = 32
V_A = W_BRANCH // H_A
Q_RANK = D_MODEL // 4
KV_RANK = D_MODEL // 8
MLA_SCALE = (NOPE_A + ROPE_A) ** -0.5

W_B = W_BRANCH
NB_B = 8
BW_B = W_B // NB_B
CONV_W = 4
RG_C = 8.0

H_C = 4
DH_C = W_BRANCH // H_C
ML_CHUNK = 128

H_D = 4
DH_D = W_BRANCH // (2 * H_D)
DIFF_SCALE = DH_D ** -0.5

N_EXP = 8
TOP_K = 2

V7X_VMEM_LIMIT_BYTES = 56 * 1024 * 1024
LANES = 128
SUBLANES = 8

BF16 = jnp.bfloat16
F32 = jnp.float32

P_GATE = 0
P_SEG = N_BRANCH * D_MODEL
SEG_NAMES = ('rg_x', 'rg_gate', 'mq', 'mk', 'mv', 'mo', 'dq', 'dk', 'dv')
P_QLAT = P_SEG + 9 * W_BRANCH
P_KVLAT = P_QLAT + Q_RANK
P_SMALL = P_KVLAT + KV_RANK
P_WIDTH = P_SMALL + LANES
SM_MI = ROPE_A
SM_MF = ROPE_A + 2 * H_C


def _seg_block(name):
    return (P_SEG + SEG_NAMES.index(name) * W_BRANCH) // W_BRANCH


N_CTX_SEQ, T_CTX = 16, 256
N_LAT_SEQ, T_LAT = 8, 1024
PAST_LEN = 512
R_CTX = N_CTX_SEQ * T_CTX
R_LAT = N_LAT_SEQ * T_LAT
R_ALL = R_CTX + R_LAT


def _cparams(sem):
    return pltpu.CompilerParams(dimension_semantics=sem, vmem_limit_bytes=V7X_VMEM_LIMIT_BYTES)


def _mod_row_of_tile(i, tm):
    n_ctx_tiles = R_CTX // tm
    per_seq = T_LAT // tm
    return jnp.where(i < n_ctx_tiles, 0, 1 + (i - n_ctx_tiles) // per_seq)


def _pos_block_of_tile(i, tm):
    n_ctx_tiles = R_CTX // tm
    per_seq = T_LAT // tm
    return jnp.where(i < n_ctx_tiles, per_seq, (i - n_ctx_tiles) % per_seq)


def _mod_kernel(c_ref, w_ref, b_ref, o_ref):
    c = c_ref[...]
    s = (c * jax.nn.sigmoid(c)).astype(BF16)
    o_ref[...] = jnp.dot(s, w_ref[0].astype(BF16), preferred_element_type=F32) + b_ref[0]


def modulation(cond, w_mod, b_mod, l):
    m, d = cond.shape
    n = w_mod.shape[2]
    tn = 1536
    return pl.pallas_call(
        _mod_kernel,
        grid=(n // tn,),
        in_specs=[pl.BlockSpec((m, d), lambda j: (0, 0)),
                  pl.BlockSpec((1, d, tn), lambda j: (l, 0, j)),
                  pl.BlockSpec((1, 1, tn), lambda j: (l, 0, j))],
        out_specs=pl.BlockSpec((m, tn), lambda j: (0, j)),
        out_shape=jax.ShapeDtypeStruct((m, n), F32),
        compiler_params=_cparams(("arbitrary",)),
        name="modulation",
    )(cond, w_mod, b_mod.reshape(b_mod.shape[0], 1, n))


def _in_proj_kernel(xc_ref, xl_ref, mod_ref, g_ref, w_ref, o_ref, xn_ref):
    @pl.when(pl.program_id(1) == 0)
    def _():
        x = jnp.where(pl.program_id(0) < R_CTX // xc_ref.shape[0], xc_ref[...], xl_ref[...])
        r = lax.rsqrt(jnp.mean(x * x, axis=-1, keepdims=True) + EPS)
        sh = mod_ref[0, 0:1, :]
        sc = mod_ref[0, 1:2, :]
        xn_ref[...] = ((x * r * g_ref[...]) * (1.0 + sc) + sh).astype(BF16)

    o_ref[...] = jnp.dot(xn_ref[...], w_ref[0], preferred_element_type=F32).astype(o_ref.dtype)


def _ctx_lat_specs(tm, width, grid_rank):
    n_ctx = R_CTX // tm
    if grid_rank == 1:
        return (pl.BlockSpec((tm, width), lambda i: (jnp.minimum(i, n_ctx - 1), 0)),
                pl.BlockSpec((tm, width), lambda i: (jnp.maximum(i - n_ctx, 0), 0)))
    return (pl.BlockSpec((tm, width), lambda i, j: (jnp.minimum(i, n_ctx - 1), 0)),
            pl.BlockSpec((tm, width), lambda i, j: (jnp.maximum(i - n_ctx, 0), 0)))


def in_proj(x_pair, mod, g, w_p, l):
    tm, tn = 1024, P_WIDTH // 4
    m, d = R_ALL, D_MODEL
    n = w_p.shape[2]
    return pl.pallas_call(
        _in_proj_kernel,
        grid=(m // tm, n // tn),
        in_specs=[*_ctx_lat_specs(tm, d, 2),
                  pl.BlockSpec((1, 8, d), lambda i, j: (_mod_row_of_tile(i, tm), 0, 0)),
                  pl.BlockSpec((1, d), lambda i, j: (0, 0)),
                  pl.BlockSpec((1, d, tn), lambda i, j: (l, 0, j))],
        out_specs=pl.BlockSpec((tm, tn), lambda i, j: (i, j)),
        out_shape=jax.ShapeDtypeStruct((m, n), BF16),
        scratch_shapes=[pltpu.VMEM((tm, d), BF16)],
        compiler_params=_cparams(("arbitrary", "arbitrary")),
        name="in_proj",
    )(*x_pair, mod, g.reshape(1, d), w_p)


_O_RG = Q_RANK + KV_RANK + ROPE_A
_O_MI = _O_RG + 6 * W_BRANCH
_O_DQ = _O_MI + 4 * H_C
_O_GATE = _O_DQ + 3 * W_BRANCH
_PACK_PIECES = ((_O_GATE, N_BRANCH * D_MODEL, P_GATE), (_O_RG, 6 * W_BRANCH, P_SEG),
                (_O_DQ, 3 * W_BRANCH, P_SEG + 6 * W_BRANCH), (0, Q_RANK + KV_RANK, P_QLAT),
                (Q_RANK + KV_RANK, ROPE_A, P_SMALL), (_O_MI, 4 * H_C, P_SMALL + ROPE_A))
PACK_ROWS = 128


def _pack_w_in_kernel(w_ref, o_ref):
    n_in = w_ref.shape[2]
    o_ref[0, :, P_SMALL:P_WIDTH] = jnp.zeros((PACK_ROWS, LANES), o_ref.dtype)
    for src, width, dst in _PACK_PIECES:
        lo = src // LANES * LANES
        hi = min(_round_up(src + width, LANES), n_in)
        win = w_ref[0, :, lo:hi]
        o_ref[0, :, dst:dst + width] = win[:, src - lo:src - lo + width].astype(o_ref.dtype)


def pack_w_in(w_in):
    depth, d, n_in = w_in.shape
    return pl.pallas_call(
        _pack_w_in_kernel,
        grid=(depth, d // PACK_ROWS),
        in_specs=[pl.BlockSpec((1, PACK_ROWS, n_in), lambda l, i: (l, i, 0))],
        out_specs=pl.BlockSpec((1, PACK_ROWS, P_WIDTH), lambda l, i: (l, i, 0)),
        out_shape=jax.ShapeDtypeStruct((depth, d, P_WIDTH), BF16),
        compiler_params=_cparams(("arbitrary", "arbitrary")),
        name="pack_w_in",
    )(w_in)


def _round_up(x, m):
    return (x + m - 1) // m * m


def _rope_partner(rot_dim, lane_starts):
    nf = rot_dim // 4
    partner = np.arange(LANES)
    for s0 in lane_starts:
        for a in range(2):
            lo = s0 + a * 2 * nf
            partner[lo:lo + nf] = np.arange(lo + nf, lo + 2 * nf)
            partner[lo + nf:lo + 2 * nf] = np.arange(lo, lo + nf)
    return partner


def _rope_tables(rot_dim, lane_starts, tm):
    rows = T_LAT // GRID_W
    r, c = np.meshgrid(np.arange(rows, dtype=np.float32), np.arange(GRID_W, dtype=np.float32), indexing='ij')
    nf = rot_dim // 4
    inv = (np.float32(ROPE_BASE) ** (-np.arange(nf, dtype=np.float32) / np.float32(nf))).astype(np.float32)
    ang = np.stack([r.reshape(-1)[:, None] * inv, c.reshape(-1)[:, None] * inv], axis=1).astype(np.float32)
    cos, sin = np.cos(ang).astype(np.float32), np.sin(ang).astype(np.float32)
    tc = np.ones((T_LAT + tm, LANES), np.float32)
    ta = np.zeros((T_LAT + tm, LANES), np.float32)
    tb = np.zeros((T_LAT + tm, LANES), np.float32)
    for s0 in lane_starts:
        for a in range(2):
            lo = s0 + a * 2 * nf
            tc[:T_LAT, lo:lo + nf] = cos[:, a]
            tc[:T_LAT, lo + nf:lo + 2 * nf] = cos[:, a]
            ta[:T_LAT, lo:lo + nf] = -sin[:, a]
            tb[:T_LAT, lo + nf:lo + 2 * nf] = sin[:, a]
    return jnp.asarray(tc), jnp.asarray(ta + tb)


MLA_TM = 512
ATTN_TQ = 512
MLA_HEADS_PER_STEP = 8
QK_A = NOPE_A + ROPE_A


def _mla_prep_kernel(*refs, has_q, norm_ckv):
    if has_q:
        (qlat_ref, gq_ref, wuq_ref, gqn_ref, kv_ref, sm_ref, gkv_ref, wkc_ref, wv_ref, gkn_ref,
         c_ref, s_ref, q_o, k_o, v_o, ckv_o, kr_o) = refs
        c, sn = c_ref[...], s_ref[...]
    else:
        (kv_ref, sm_ref, gkv_ref, wkc_ref, wv_ref, gkn_ref, k_o, v_o) = refs

    def heads(z, g_ref, o_ref, scale):
        for h in range(H_A):
            s = z[:, h * LANES:(h + 1) * LANES]
            r = lax.rsqrt(jnp.sum(s * s, axis=-1, keepdims=True) * (1.0 / QK_A) + EPS)
            y = s * r * g_ref[0:1, :]
            if has_q:
                sw = z[:, (H_A + h) * LANES:(H_A + h + 1) * LANES]
                y = y * c + (sw * r * g_ref[1:2, :]) * sn
            if scale != 1.0:
                y = y * scale
            o_ref[:, h * LANES:(h + 1) * LANES] = y.astype(o_ref.dtype)

    if has_q:
        ql = qlat_ref[...].astype(F32)
        qn = ql * lax.rsqrt(jnp.mean(ql * ql, axis=-1, keepdims=True) + EPS) * gq_ref[...]
        q = jnp.dot(qn.astype(BF16), wuq_ref[...], preferred_element_type=F32)
        heads(q, gqn_ref, q_o, MLA_SCALE)

    if has_q:
        kv, sm = kv_ref[...].astype(F32), sm_ref[...]
    else:
        kv, sm = kv_ref[0, 0].astype(F32), sm_ref[0, 0]
    if norm_ckv:
        ckv = kv * lax.rsqrt(jnp.mean(kv * kv, axis=-1, keepdims=True) + EPS) * gkv_ref[...]
    else:
        ckv = kv
    ckv_b = ckv.astype(BF16)
    kin = jnp.concatenate([ckv_b, sm.astype(BF16)], axis=1)
    wkc = wkc_ref[...] if has_q else wkc_ref[:, :H_A * LANES]
    k = jnp.dot(kin, wkc, preferred_element_type=F32)
    heads(k, gkn_ref, k_o, 1.0)
    v_o[...] = jnp.dot(ckv_b, wv_ref[...], preferred_element_type=F32).astype(v_o.dtype)
    if has_q:
        ckv_o[...] = ckv
        kr_o[...] = sm.astype(F32)


def pack_mla_weights(lp):
    wuq = lp['w_mla_uq'].reshape(Q_RANK, H_A, QK_A)
    wuq_p = jnp.pad(wuq, ((0, 0), (0, 0), (0, LANES - QK_A))).reshape(Q_RANK, H_A * LANES).astype(BF16)
    wukv = lp['w_mla_ukv'].reshape(KV_RANK, H_A, NOPE_A + V_A)
    wk = jnp.pad(wukv[:, :, :NOPE_A], ((0, 0), (0, 0), (0, LANES - NOPE_A))).reshape(KV_RANK, H_A * LANES)
    place = np.zeros((LANES, H_A, LANES), np.float32)
    for h in range(H_A):
        place[np.arange(ROPE_A), h, NOPE_A + np.arange(ROPE_A)] = 1.0
    wkc = jnp.concatenate([wk, jnp.asarray(place.reshape(LANES, H_A * LANES))], axis=0).astype(BF16)
    wv = wukv[:, :, NOPE_A:]
    wv_even = jnp.pad(wv, ((0, 0), (0, 0), (0, LANES - V_A)))
    wv_odd = jnp.pad(wv, ((0, 0), (0, 0), (LANES - V_A, 0)))
    odd = (np.arange(H_A) % 2 == 1)[None, :, None]
    wv_p = jnp.where(odd, wv_odd, wv_even).reshape(KV_RANK, H_A * LANES).astype(BF16)
    partner = _rope_partner(ROPE_A, (NOPE_A,))
    cols = (np.arange(H_A)[:, None] * LANES + partner[None, :]).reshape(-1)
    with_partner = lambda w: jnp.concatenate([w, w[:, cols]], axis=1)
    pad_g = lambda g: jnp.pad(g, (0, LANES - QK_A))
    gain2 = lambda g: jnp.stack([pad_g(g), pad_g(g)[partner]])
    return dict(wuq=with_partner(wuq_p), wkc=with_partner(wkc), wv=wv_p,
                gqn=gain2(lp['g_mla_qn']), gkn=gain2(lp['g_mla_kn']),
                gq=lp['g_mla_qlat'].reshape(1, Q_RANK), gkv=lp['g_mla_kvlat'].reshape(1, KV_RANK))


def mla_prep(p, mw, tabs):
    tm = MLA_TM
    n = R_ALL // tm
    full = lambda shape: pl.BlockSpec(shape, lambda i: (0, 0))
    tab = pl.BlockSpec((tm, LANES), lambda i: (_pos_block_of_tile(i, tm), 0))
    wide = H_A * LANES
    out_shape = (jax.ShapeDtypeStruct((R_ALL, wide), BF16), jax.ShapeDtypeStruct((R_ALL, wide), BF16),
                 jax.ShapeDtypeStruct((R_ALL, wide), BF16), jax.ShapeDtypeStruct((R_ALL, KV_RANK), F32),
                 jax.ShapeDtypeStruct((R_ALL, LANES), F32))
    row = lambda w: pl.BlockSpec((tm, w), lambda i: (i, 0))
    return pl.pallas_call(
        functools.partial(_mla_prep_kernel, has_q=True, norm_ckv=True),
        grid=(n,),
        in_specs=[pl.BlockSpec((tm, Q_RANK), lambda i: (i, P_QLAT // Q_RANK)), full((1, Q_RANK)),
                  full((Q_RANK, 2 * wide)), full((2, LANES)),
                  pl.BlockSpec((tm, KV_RANK), lambda i: (i, P_KVLAT // KV_RANK)),
                  pl.BlockSpec((tm, LANES), lambda i: (i, P_SMALL // LANES)), full((1, KV_RANK)),
                  full((2 * LANES, 2 * wide)), full((KV_RANK, wide)), full((2, LANES)), tab, tab],
        out_specs=(row(wide), row(wide), row(wide), row(KV_RANK), row(LANES)),
        out_shape=out_shape,
        compiler_params=_cparams(("arbitrary",)),
        name="mla_prep",
    )(p, mw['gq'], mw['wuq'], mw['gqn'], p, p, mw['gkv'], mw['wkc'], mw['wv'], mw['gkn'], *tabs)


def mla_prep_cache(ckv_c, kr_c, mw, l):
    tm = PAST_LEN
    r = ckv_c.shape[0] * PAST_LEN
    full = lambda shape: pl.BlockSpec(shape, lambda i: (0, 0))
    wide = H_A * LANES
    row = lambda w: pl.BlockSpec((tm, w), lambda i: (i, 0))
    cache = lambda w: pl.BlockSpec((1, 1, PAST_LEN, w), lambda i: (i, l, 0, 0))
    return pl.pallas_call(
        functools.partial(_mla_prep_kernel, has_q=False, norm_ckv=False),
        grid=(r // tm,),
        in_specs=[cache(KV_RANK), cache(LANES), full((1, KV_RANK)), full((2 * LANES, 2 * wide)),
                  full((KV_RANK, wide)), full((2, LANES))],
        out_specs=(row(wide), row(wide)),
        out_shape=(jax.ShapeDtypeStruct((r, wide), BF16), jax.ShapeDtypeStruct((r, wide), BF16)),
        compiler_params=_cparams(("arbitrary",)),
        name="mla_prep_cache",
    )(ckv_c, kr_c, mw['gkv'], mw['wkc'], mw['wv'], mw['gkn'])


_NT = (((1,), (1,)), ((), ()))


def _mla_attn_kernel(*refs, has_cache):
    if has_cache:
        q_ref, ko_ref, vo_ref, kc_ref, vc_ref, o_ref = refs
    else:
        q_ref, ko_ref, vo_ref, o_ref = refs
    n_heads = q_ref.shape[1] // LANES
    sls = [slice(h * LANES, (h + 1) * LANES) for h in range(n_heads)]
    def scores(sl):
        s_o = lax.dot_general(q_ref[:, sl], ko_ref[:, sl], _NT, preferred_element_type=F32)
        s_c = lax.dot_general(q_ref[:, sl], kc_ref[:, sl], _NT, preferred_element_type=F32) if has_cache else None
        return s_o, s_c

    outs = []
    nxt = scores(sls[0])
    for h, sl in enumerate(sls):
        s_o, s_c = nxt
        if h + 1 < n_heads:
            nxt = scores(sls[h + 1])
        m = jnp.max(s_o, axis=-1, keepdims=True)
        if has_cache:
            m = jnp.maximum(m, jnp.max(s_c, axis=-1, keepdims=True))
        e_o = jnp.exp(s_o - m)
        l = jnp.sum(e_o, axis=-1, keepdims=True)
        pv = jnp.dot(e_o.astype(BF16), vo_ref[:, sl], preferred_element_type=F32)
        if has_cache:
            e_c = jnp.exp(s_c - m)
            l = l + jnp.sum(e_c, axis=-1, keepdims=True)
            pv = pv + jnp.dot(e_c.astype(BF16), vc_ref[:, sl], preferred_element_type=F32)
        outs.append(pv * (1.0 / l))
    for p in range(n_heads // 2):
        o_ref[:, p * LANES:(p + 1) * LANES] = (outs[2 * p] + outs[2 * p + 1]).astype(o_ref.dtype)


def mla_attention(q, k, v, kc, vc, *, row0, n_seq, t_seq, tq):
    has_cache = kc is not None
    wh = MLA_HEADS_PER_STEP * LANES
    n_grp = H_A // MLA_HEADS_PER_STEP
    nq = t_seq // tq
    qb0, kb0 = row0 // tq, row0 // t_seq
    in_specs = [pl.BlockSpec((tq, wh), lambda s, p, i: (qb0 + s * nq + i, p)),
                pl.BlockSpec((t_seq, wh), lambda s, p, i: (kb0 + s, p)),
                pl.BlockSpec((t_seq, wh), lambda s, p, i: (kb0 + s, p))]
    args = [q, k, v]
    if has_cache:
        in_specs += [pl.BlockSpec((PAST_LEN, wh), lambda s, p, i: (s, p)),
                     pl.BlockSpec((PAST_LEN, wh), lambda s, p, i: (s, p))]
        args += [kc, vc]
    return pl.pallas_call(
        functools.partial(_mla_attn_kernel, has_cache=has_cache),
        grid=(n_seq, n_grp, nq),
        in_specs=in_specs,
        out_specs=pl.BlockSpec((tq, wh // 2), lambda s, p, i: (s * nq + i, p)),
        out_shape=jax.ShapeDtypeStruct((n_seq * t_seq, W_BRANCH), BF16),
        compiler_params=_cparams(("arbitrary", "arbitrary", "arbitrary")),
        name="mla_attention",
    )(*args)


DIFF_TM = 512


def _diff_prep_kernel(dq_ref, dk_ref, gq_ref, gk_ref, perm_ref, c_ref, s_ref, q_o, ko_o, kp_o):
    c, sn = c_ref[...], s_ref[...]
    lane = lax.broadcasted_iota(jnp.int32, (1, LANES), 1)
    lo = lane < DH_D

    def inv_rms(x):
        x2 = x * x
        s_lo = jnp.sum(jnp.where(lo, x2, 0.0), axis=-1, keepdims=True)
        s_hi = jnp.sum(jnp.where(lo, 0.0, x2), axis=-1, keepdims=True)
        return lax.rsqrt(jnp.where(lo, s_lo, s_hi) * (1.0 / DH_D) + EPS)

    def rotated(n, x_sw, r, g_ref):
        return n * c + (x_sw * r * g_ref[1:2, :]) * sn

    for j in range(W_BRANCH // LANES):
        sl = slice(j * LANES, (j + 1) * LANES)
        xq, xk = dq_ref[:, sl], dk_ref[:, sl]
        q_sw = jnp.dot(xq, perm_ref[...], preferred_element_type=F32)
        k_sw = jnp.dot(xk, perm_ref[...], preferred_element_type=F32)
        xq, xk = xq.astype(F32), xk.astype(F32)
        rq, rk = inv_rms(xq), inv_rms(xk)
        qn = xq * rq * gq_ref[0:1, :]
        kn = xk * rk * gk_ref[0:1, :]
        q_o[:, sl] = (rotated(qn, q_sw, rq, gq_ref) * DIFF_SCALE).astype(q_o.dtype)
        kp_o[:, sl] = kn
        ko_o[:, sl] = rotated(kn, k_sw, rk, gk_ref).astype(ko_o.dtype)


def diff_prep(p, lp, tabs):
    tm = DIFF_TM
    partner = _rope_partner(DH_D, (0, DH_D))
    perm = np.zeros((LANES, LANES), np.float32)
    perm[partner, np.arange(LANES)] = 1.0
    gain2 = lambda g: jnp.stack([jnp.concatenate([g, g]), jnp.concatenate([g, g])[partner]])
    full = lambda shape: pl.BlockSpec(shape, lambda i: (0, 0))
    tab = pl.BlockSpec((tm, LANES), lambda i: (_pos_block_of_tile(i, tm), 0))
    row = pl.BlockSpec((tm, W_BRANCH), lambda i: (i, 0))
    dq_b, dk_b = _seg_block('dq'), _seg_block('dk')
    return pl.pallas_call(
        _diff_prep_kernel,
        grid=(R_ALL // tm,),
        in_specs=[pl.BlockSpec((tm, W_BRANCH), lambda i: (i, dq_b)), pl.BlockSpec((tm, W_BRANCH), lambda i: (i, dk_b)),
                  full((2, LANES)), full((2, LANES)), full((LANES, LANES)), tab, tab],
        out_specs=(row, row, row),
        out_shape=(jax.ShapeDtypeStruct((R_ALL, W_BRANCH), BF16), jax.ShapeDtypeStruct((R_ALL, W_BRANCH), BF16),
                   jax.ShapeDtypeStruct((R_ALL, W_BRANCH), F32)),
        compiler_params=_cparams(("arbitrary",)),
        name="diff_prep",
    )(p, p, gain2(lp['g_diff_qn']), gain2(lp['g_diff_kn']), jnp.asarray(perm, BF16), *tabs)


def _diff_attn_kernel(*refs, has_cache, lambda_init):
    if has_cache:
        dl_ref, q_ref, k_ref, v_ref, kc_ref, vc_ref, g_ref, o_ref = refs
    else:
        dl_ref, q_ref, k_ref, v_ref, g_ref, o_ref = refs
    lane = lax.broadcasted_iota(jnp.int32, (1, LANES), 1)
    dl = dl_ref[...]
    lam = (jnp.exp(jnp.sum(dl[0:1] * dl[1:2], axis=-1, keepdims=True))
           - jnp.exp(jnp.sum(dl[2:3] * dl[3:4], axis=-1, keepdims=True)) + lambda_init)
    units = [(h, w) for h in range(H_D) for w in range(2)]

    def scores(unit):
        h, w = unit
        sl = slice((2 * w + h // 2) * LANES, (2 * w + h // 2 + 1) * LANES)
        q = jnp.where((lane // DH_D) == (h % 2), q_ref[:, sl], jnp.zeros((), q_ref.dtype))
        s_o = lax.dot_general(q, k_ref[:, sl], _NT, preferred_element_type=F32)
        s_c = (lax.dot_general(q, kc_ref[0, 0, :, sl].astype(BF16), _NT, preferred_element_type=F32)
               if has_cache else None)
        return s_o, s_c

    outs = []
    nxt = scores(units[0])
    for n, (h, w) in enumerate(units):
        s_o, s_c = nxt
        if n + 1 < len(units):
            nxt = scores(units[n + 1])
        vs = slice(h * LANES, (h + 1) * LANES)
        m = jnp.max(s_o, axis=-1, keepdims=True)
        if has_cache:
            m = jnp.maximum(m, jnp.max(s_c, axis=-1, keepdims=True))
        e_o = jnp.exp(s_o - m)
        l = jnp.sum(e_o, axis=-1, keepdims=True)
        pv = jnp.dot(e_o.astype(BF16), v_ref[:, vs], preferred_element_type=F32)
        if has_cache:
            e_c = jnp.exp(s_c - m)
            l = l + jnp.sum(e_c, axis=-1, keepdims=True)
            pv = pv + jnp.dot(e_c.astype(BF16), vc_ref[0, 0, :, vs].astype(BF16), preferred_element_type=F32)
        outs.append(pv * (1.0 / l))
    for h in range(H_D):
        y = outs[2 * h] - lam * outs[2 * h + 1]
        r = lax.rsqrt(jnp.mean(y * y, axis=-1, keepdims=True) + EPS)
        o_ref[:, h * LANES:(h + 1) * LANES] = ((y * r * g_ref[...]) * (1.0 - lambda_init)).astype(o_ref.dtype)


def diff_attention(dl, qd, kd, p, kc, vc, g_sub, *, row0, n_seq, t_seq, tq, lambda_init, layer=0):
    has_cache = kc is not None
    nq = t_seq // tq
    qb0, kb0 = row0 // tq, row0 // t_seq
    in_specs = [pl.BlockSpec((4, DH_D), lambda s, i: (0, 0)),
                pl.BlockSpec((tq, W_BRANCH), lambda s, i: (qb0 + s * nq + i, 0)),
                pl.BlockSpec((t_seq, W_BRANCH), lambda s, i: (kb0 + s, 0)),
                pl.BlockSpec((t_seq, W_BRANCH), lambda s, i: (kb0 + s, _seg_block('dv')))]
    args = [dl, qd, kd, p]
    if has_cache:
        in_specs += [pl.BlockSpec((1, 1, PAST_LEN, W_BRANCH), lambda s, i: (s, layer, 0, 0)),
                     pl.BlockSpec((1, 1, PAST_LEN, W_BRANCH), lambda s, i: (s, layer, 0, 0))]
        args += [kc, vc]
    in_specs.append(pl.BlockSpec((1, LANES), lambda s, i: (0, 0)))
    args.append(g_sub.reshape(1, LANES))
    return pl.pallas_call(
        functools.partial(_diff_attn_kernel, has_cache=has_cache, lambda_init=lambda_init),
        grid=(n_seq, nq),
        in_specs=in_specs,
        out_specs=pl.BlockSpec((tq, W_BRANCH), lambda s, i: (s * nq + i, 0)),
        out_shape=jax.ShapeDtypeStruct((n_seq * t_seq, W_BRANCH), BF16),
        compiler_params=_cparams(("arbitrary", "arbitrary")),
        name="diff_attention",
    )(*args)


def _softplus(z):
    return jnp.maximum(z, 0.0) + jnp.log(1.0 + jnp.exp(-jnp.abs(z)))


def _gelu_tanh(x):
    return 0.5 * x * (1.0 + jnp.tanh(math.sqrt(2.0 / math.pi) * (x + 0.044715 * (x * x * x))))


def _rglru_kernel(x_ref, gate_ref, wc_ref, bc_ref, wg_ref, bg_ref, lam_ref, h0_ref, y_ref, st_ref,
                  af_s, uf_s, ab_s, ub_s, *, t_seq):
    t = t_seq
    x = x_ref[...].astype(F32)
    row = lax.broadcasted_iota(jnp.int32, (t, W_B), 0)
    wc = wc_ref[...]
    xc = (wc[0:1] * jnp.where(row >= 2, pltpu.roll(x, 2, 0), 0.0)
          + wc[1:2] * jnp.where(row >= 1, pltpu.roll(x, 1, 0), 0.0)
          + wc[2:3] * x
          + wc[3:4] * jnp.where(row < t - 1, pltpu.roll(x, t - 1, 0), 0.0)
          + bc_ref[...])
    gates = jnp.dot(xc.astype(BF16), wg_ref[...], preferred_element_type=F32) + bg_ref[...]
    for d, (a_s, u_s) in enumerate(((af_s, uf_s), (ab_s, ub_s))):
        rg = jax.nn.sigmoid(gates[:, (2 * d) * W_B:(2 * d + 1) * W_B])
        ig = jax.nn.sigmoid(gates[:, (2 * d + 1) * W_B:(2 * d + 2) * W_B])
        log_a = -RG_C * rg * _softplus(-lam_ref[d:d + 1, :])
        a = jnp.exp(log_a)
        a_s[...] = a
        u_s[...] = jnp.sqrt(-jnp.tanh(log_a) * (a * a + 1.0)) * (ig * xc)

    nblk = t // SUBLANES

    def body(k, carry):
        hf, hb = carry
        base_f = pl.multiple_of(k * SUBLANES, SUBLANES)
        base_b = pl.multiple_of((nblk - 1 - k) * SUBLANES, SUBLANES)
        for r in range(SUBLANES):
            rf = pl.ds(base_f + r, 1)
            rb = pl.ds(base_b + (SUBLANES - 1 - r), 1)
            hf = af_s[rf, :] * hf + uf_s[rf, :]
            hb = ab_s[rb, :] * hb + ub_s[rb, :]
            uf_s[rf, :] = hf
            ub_s[rb, :] = hb
        return hf, hb

    hf, hb = lax.fori_loop(0, nblk, body, (h0_ref[0, 0:1, :], h0_ref[0, 1:2, :]))
    st_ref[0, 0:1, :] = hf
    st_ref[0, 1:2, :] = hb
    y_ref[...] = (_gelu_tanh(gate_ref[...].astype(F32)) * (uf_s[...] + ub_s[...])).astype(y_ref.dtype)


def pack_rglru_weights(lp):
    def blockdiag(w):
        eye = jnp.eye(NB_B, dtype=w.dtype)
        return jnp.einsum('ncd,nm->ncmd', w, eye).reshape(W_B, W_B)
    wg = jnp.concatenate([blockdiag(lp['w_rg_a'][0]), blockdiag(lp['w_rg_x'][0]),
                          blockdiag(lp['w_rg_a'][1]), blockdiag(lp['w_rg_x'][1])], axis=1).astype(BF16)
    bg = jnp.concatenate([lp['b_rg_a'][0], lp['b_rg_x'][0], lp['b_rg_a'][1], lp['b_rg_x'][1]]).reshape(1, 4 * W_B)
    return dict(wg=wg, bg=bg, wc=lp['w_conv_rg'], bc=lp['b_conv_rg'].reshape(1, W_B), lam=lp['rg_lambda'])


def rglru(p, rw, h0, *, row0, n_seq, t_seq):
    rb0 = row0 // t_seq
    xb, gb = _seg_block('rg_x'), _seg_block('rg_gate')
    full = lambda shape: pl.BlockSpec(shape, lambda s: tuple(0 for _ in shape))
    in_specs = [pl.BlockSpec((t_seq, W_B), lambda s: (rb0 + s, xb)),
                pl.BlockSpec((t_seq, W_B), lambda s: (rb0 + s, gb)),
                full((CONV_W, W_B)), full((1, W_B)), full((W_B, 4 * W_B)), full((1, 4 * W_B)), full((2, W_B)),
                pl.BlockSpec((1, 2, W_B), lambda s: (s, 0, 0))]
    args = [p, p, rw['wc'], rw['bc'], rw['wg'], rw['bg'], rw['lam'], h0]
    return pl.pallas_call(
        functools.partial(_rglru_kernel, t_seq=t_seq),
        grid=(n_seq,),
        in_specs=in_specs,
        out_specs=(pl.BlockSpec((t_seq, W_B), lambda s: (s, 0)), pl.BlockSpec((1, 2, W_B), lambda s: (s, 0, 0))),
        out_shape=(jax.ShapeDtypeStruct((n_seq * t_seq, W_B), BF16), jax.ShapeDtypeStruct((n_seq, 2, W_B), F32)),
        scratch_shapes=[pltpu.VMEM((t_seq, W_B), F32)] * 4,
        compiler_params=_cparams(("arbitrary",)),
        name="rglru",
    )(*args)


def _dot_split(a, b_bf16):
    hi = a.astype(BF16)
    lo = (a - hi.astype(F32)).astype(BF16)
    return (jnp.dot(hi, b_bf16, preferred_element_type=F32) + jnp.dot(lo, b_bf16, preferred_element_type=F32))


def _log_sigmoid(z):
    return jnp.minimum(z, 0.0) - jnp.log(1.0 + jnp.exp(-jnp.abs(z)))


_TN = (((0,), (0,)), ((), ()))


def _mlstm_kernel(q_ref, k_ref, v_ref, o_ref, sm_ref, bias_ref, g_ref, c0_ref, n0_ref, m0_ref,
                  y_ref, c_out, n_out, m_out, hm_s, c_s, *, t_seq):
    L = ML_CHUNK
    nchunk = t_seq // L
    scale = DH_C ** -0.5
    ri = lax.broadcasted_iota(jnp.int32, (L, L), 0)
    ci = lax.broadcasted_iota(jnp.int32, (L, L), 1)
    lane1 = lax.broadcasted_iota(jnp.int32, (L, LANES), 1)
    ones_col = jnp.where(lane1 == 0, 1.0, 0.0).astype(BF16)
    bias = bias_ref[...]

    for d in range(2):
        causal = (ci <= ri) if d == 0 else (ci >= ri)
        tri = jnp.where(causal, 1.0, 0.0).astype(BF16)
        tri_t = jnp.where((ri <= ci) if d == 0 else (ri >= ci), 1.0, 0.0).astype(BF16)
        for h in range(H_C):
            c_s[h, :, 0:DH_C] = c0_ref[0, 0, d, h]
            c_s[h, :, DH_C:2 * DH_C] = jnp.where(lane1 == 0, n0_ref[0, 0, d, h], 0.0)
        m_init = tuple(m0_ref[0, :, d * H_C + h:d * H_C + h + 1] for h in range(H_C))

        def chunk(kk, ms, d=d, causal=causal, tri=tri, tri_t=tri_t):
            cidx = kk if d == 0 else nchunk - 1 - kk
            rows = pl.ds(pl.multiple_of(cidx * L, L), L)
            gsm = sm_ref[rows, :].astype(F32) + bias
            lf_all = _log_sigmoid(gsm)
            cum_cols = _dot_split_left(tri, lf_all)
            g_t = gsm.T
            cum_rows = _dot_split(lf_all.T, tri_t)
            heads = range(H_C)
            sls = [slice(h * DH_C, (h + 1) * DH_C) for h in heads]
            qs = [q_ref[rows, sl] for sl in sls]
            ks = [k_ref[rows, sl] for sl in sls]
            v_augs = [jnp.concatenate([v_ref[rows, sl], ones_col], axis=1) for sl in sls]
            qk = [lax.dot_general(qs[h], ks[h], _NT, preferred_element_type=F32) for h in heads]
            qc = [jnp.dot(qs[h], c_s[h].astype(BF16), preferred_element_type=F32) * scale for h in heads]
            cum_c = [cum_cols[:, SM_MF + d * H_C + h:SM_MF + d * H_C + h + 1] for h in heads]
            li_c = [gsm[:, SM_MI + d * H_C + h:SM_MI + d * H_C + h + 1] for h in heads]
            m_row, s = [], []
            for h in heads:
                jl, jf = SM_MI + d * H_C + h, SM_MF + d * H_C + h
                log_d = jnp.where(causal, cum_c[h] - cum_rows[jf:jf + 1, :] + g_t[jl:jl + 1, :], -jnp.inf)
                m_row.append(jnp.maximum(cum_c[h] + ms[h], jnp.max(log_d, axis=-1, keepdims=True)))
                s.append(qk[h] * (scale * jnp.exp(log_d - m_row[h])))
            sv = [jnp.dot(s[h].astype(BF16), v_augs[h], preferred_element_type=F32) for h in heads]
            new_ms = []
            for h in heads:
                w_inter = jnp.exp(cum_c[h] + ms[h] - m_row[h])
                nd = sv[h] + qc[h] * w_inter
                den = jnp.maximum(jnp.abs(nd[:, DH_C:DH_C + 1]), jnp.exp(-m_row[h]))
                h_out = nd[:, :DH_C] * (1.0 / den)
                if d == 0:
                    hm_s[rows, sls[h]] = h_out
                else:
                    hm_s[rows, sls[h]] = hm_s[rows, sls[h]] + h_out
                last = cum_c[h][L - 1:L, :] if d == 0 else cum_c[h][0:1, :]
                w_s = last - cum_c[h] + li_c[h]
                m_new = jnp.maximum(last + ms[h], jnp.max(w_s, axis=0, keepdims=True))
                decay = jnp.exp(last + ms[h] - m_new)
                kw_t = (ks[h].astype(F32) * jnp.exp(w_s - m_new)).T.astype(BF16)
                c_s[h] = decay * c_s[h] + jnp.dot(kw_t, v_augs[h], preferred_element_type=F32)
                new_ms.append(m_new)
            return tuple(new_ms)

        m_fin = lax.fori_loop(0, nchunk, chunk, m_init)
        for h in range(H_C):
            c_out[0, d, h] = c_s[h, :, 0:DH_C]
            n_out[0, d, h] = c_s[h, :, DH_C:DH_C + 1]
            m_out[0, :, d * H_C + h:d * H_C + h + 1] = m_fin[h]

    for h in range(H_C):
        sl = slice(h * DH_C, (h + 1) * DH_C)
        hm = hm_s[:, sl]
        r = lax.rsqrt(jnp.mean(hm * hm, axis=-1, keepdims=True) + EPS)
        y_ref[:, sl] = (jax.nn.sigmoid(o_ref[:, sl].astype(F32)) * (hm * r * g_ref[...])).astype(y_ref.dtype)


def _dot_split_left(a_bf16, b):
    hi = b.astype(BF16)
    lo = (b - hi.astype(F32)).astype(BF16)
    return (jnp.dot(a_bf16, hi, preferred_element_type=F32) + jnp.dot(a_bf16, lo, preferred_element_type=F32))


def mlstm(p, bias_sm, g_out, c0, n0, m0, *, row0, n_seq, t_seq, layer=0):
    rb0 = row0 // t_seq
    seg = lambda nm: pl.BlockSpec((t_seq, W_BRANCH), lambda s, b=_seg_block(nm): (rb0 + s, b))
    full = lambda shape: pl.BlockSpec(shape, lambda s: tuple(0 for _ in shape))
    c_spec = pl.BlockSpec((1, 2, H_C, DH_C, DH_C), lambda s: (s, 0, 0, 0, 0))
    n_spec = pl.BlockSpec((1, 2, H_C, DH_C, 1), lambda s: (s, 0, 0, 0, 0))
    c_in = pl.BlockSpec((1, 1, 2, H_C, DH_C, DH_C), lambda s: (s, layer, 0, 0, 0, 0))
    n_in = pl.BlockSpec((1, 1, 2, H_C, DH_C, 1), lambda s: (s, layer, 0, 0, 0, 0))
    m_spec = pl.BlockSpec((1, 1, 2 * H_C), lambda s: (s, 0, 0))
    in_specs = [seg('mq'), seg('mk'), seg('mv'), seg('mo'),
                pl.BlockSpec((t_seq, LANES), lambda s: (rb0 + s, P_SMALL // LANES)),
                full((1, LANES)), full((1, DH_C)), c_in, n_in, m_spec]
    args = [p, p, p, p, p, bias_sm, g_out.reshape(1, DH_C), c0, n0, m0]
    return pl.pallas_call(
        functools.partial(_mlstm_kernel, t_seq=t_seq),
        grid=(n_seq,),
        in_specs=in_specs,
        out_specs=(pl.BlockSpec((t_seq, W_BRANCH), lambda s: (s, 0)), c_spec, n_spec, m_spec),
        out_shape=(jax.ShapeDtypeStruct((n_seq * t_seq, W_BRANCH), BF16),
                   jax.ShapeDtypeStruct((n_seq, 2, H_C, DH_C, DH_C), F32),
                   jax.ShapeDtypeStruct((n_seq, 2, H_C, DH_C, 1), F32),
                   jax.ShapeDtypeStruct((n_seq, 1, 2 * H_C), F32)),
        scratch_shapes=[pltpu.VMEM((t_seq, W_BRANCH), F32), pltpu.VMEM((H_C, DH_C, 2 * DH_C), F32)],
        compiler_params=_cparams(("arbitrary",)),
        name="mlstm",
    )(*args)


MERGE_TM = 512


def _merge_kernel(*refs):
    ctx_refs, lat_refs, gate_refs = refs[0:4], refs[4:8], refs[8:12]
    xc_ref, xl_ref, mod_ref, wbr_ref, wout_ref, gn_ref, xo_ref, xn_ref = refs[12:20]
    xn3_ref = refs[20] if len(refs) > 20 else None
    is_ctx = pl.program_id(0) < R_CTX // MERGE_TM
    merged = None
    for g in range(N_BRANCH):
        yg = jnp.where(is_ctx, ctx_refs[g][...], lat_refs[g][...])
        pg = jnp.dot(yg, wbr_ref[g], preferred_element_type=F32)
        term = jax.nn.sigmoid(gate_refs[g][...].astype(F32)) * pg
        merged = term if merged is None else merged + term
    y = jnp.dot(merged.astype(BF16), wout_ref[...], preferred_element_type=F32)
    x = jnp.where(is_ctx, xc_ref[...], xl_ref[...]) + mod_ref[0, 2:3, :] * y
    xo_ref[...] = x
    r = lax.rsqrt(jnp.mean(x * x, axis=-1, keepdims=True) + EPS)
    xn = (x * r * gn_ref[...]) * (1.0 + mod_ref[0, 4:5, :]) + mod_ref[0, 3:4, :]
    xn_ref[...] = xn.astype(xn_ref.dtype)
    if xn3_ref is not None:
        _rows_to_tiles(xn3_ref, xn.astype(BF16))


def merge(ys_ctx, ys_lat, p, x_pair, mod, wbr, wout, g_ffn, rows_as_tiles):
    tm = MERGE_TM
    n_ctx_tiles = R_CTX // tm
    br_ctx = pl.BlockSpec((tm, W_BRANCH), lambda i: (jnp.minimum(i, n_ctx_tiles - 1), 0))
    br_lat = pl.BlockSpec((tm, W_BRANCH), lambda i: (jnp.maximum(i - n_ctx_tiles, 0), 0))
    gate = lambda g: pl.BlockSpec((tm, D_MODEL), lambda i, g=g: (i, g))
    row = pl.BlockSpec((tm, D_MODEL), lambda i: (i, 0))
    out_specs = [row, row]
    out_shape = [jax.ShapeDtypeStruct((R_ALL, D_MODEL), F32), jax.ShapeDtypeStruct((R_ALL, D_MODEL), BF16)]
    if rows_as_tiles:
        out_specs.append(pl.BlockSpec((tm, ROW_SUB, LANES), lambda i: (i, 0, 0)))
        out_shape.append(jax.ShapeDtypeStruct((R_ALL, ROW_SUB, LANES), F32))
    return pl.pallas_call(
        _merge_kernel,
        grid=(R_ALL // tm,),
        in_specs=[br_ctx] * N_BRANCH + [br_lat] * N_BRANCH + [gate(0), gate(1), gate(2), gate(3),
                  *_ctx_lat_specs(tm, D_MODEL, 1),
                  pl.BlockSpec((1, 8, D_MODEL), lambda i: (_mod_row_of_tile(i, tm), 0, 0)),
                  pl.BlockSpec((N_BRANCH, W_BRANCH, D_MODEL), lambda i: (0, 0, 0)),
                  pl.BlockSpec((D_MODEL, D_MODEL), lambda i: (0, 0)),
                  pl.BlockSpec((1, D_MODEL), lambda i: (0, 0))],
        out_specs=tuple(out_specs),
        out_shape=tuple(out_shape),
        compiler_params=_cparams(("arbitrary",)),
        name="merge",
    )(*ys_ctx, *ys_lat, p, p, p, p, *x_pair, mod, wbr, wout, g_ffn.reshape(1, D_MODEL))


def _new_expert(te_ref, i):
    return jnp.logical_or(i == 0, te_ref[i] != te_ref[jnp.maximum(i - 1, 0)])


def _ffn_up_kernel(te_ref, nt_ref, x_ref, wg_ref, wu_ref, h_ref, wgb_ref, wub_ref):
    i = pl.program_id(1)

    @pl.when(_new_expert(te_ref, i))
    def _():
        wgb_ref[...] = wg_ref[0].astype(BF16)
        wub_ref[...] = wu_ref[0].astype(BF16)

    @pl.when(i < nt_ref[0])
    def _():
        x = x_ref[...]
        g = jnp.dot(x, wgb_ref[...], preferred_element_type=F32)
        u = jnp.dot(x, wub_ref[...], preferred_element_type=F32)
        h_ref[...] = (g * jax.nn.sigmoid(g) * u).astype(h_ref.dtype)

    @pl.when(i >= nt_ref[0])
    def _():
        h_ref[...] = jnp.zeros(h_ref.shape, h_ref.dtype)


def ffn_up(tile_expert, n_tiles, xs, wg, wu, tm, tf, weight_buffers):
    r, d = xs.shape
    f = wg.shape[2]
    w_spec = pl.BlockSpec((1, d, tf), lambda j, i, te, nt: (te[i], 0, j), pipeline_mode=pl.Buffered(weight_buffers))
    return pl.pallas_call(
        _ffn_up_kernel,
        grid_spec=pltpu.PrefetchScalarGridSpec(
            num_scalar_prefetch=2,
            grid=(f // tf, r // tm),
            in_specs=[pl.BlockSpec((tm, d), lambda j, i, te, nt: (i, 0)), w_spec, w_spec],
            out_specs=pl.BlockSpec((tm, tf), lambda j, i, te, nt: (i, j)),
            scratch_shapes=[pltpu.VMEM((d, tf), BF16), pltpu.VMEM((d, tf), BF16)]),
        out_shape=jax.ShapeDtypeStruct((r, f), BF16),
        compiler_params=_cparams(("arbitrary", "arbitrary")),
        name="ffn_up",
    )(tile_expert, n_tiles, xs, wg, wu)


def _ffn_down_kernel(te_ref, nt_ref, h_ref, wd_ref, y_ref, wdb_ref):
    i = pl.program_id(0)

    @pl.when(_new_expert(te_ref, i))
    def _():
        wdb_ref[...] = wd_ref[0].astype(BF16)

    @pl.when(i < nt_ref[0])
    def _():
        _rows_to_tiles(y_ref, jnp.dot(h_ref[...], wdb_ref[...], preferred_element_type=F32))

    @pl.when(i >= nt_ref[0])
    def _():
        y_ref[...] = jnp.zeros(y_ref.shape, y_ref.dtype)


def ffn_down(tile_expert, n_tiles, h, wd, tm):
    r, f = h.shape
    d = wd.shape[2]
    return pl.pallas_call(
        _ffn_down_kernel,
        grid_spec=pltpu.PrefetchScalarGridSpec(
            num_scalar_prefetch=2,
            grid=(r // tm,),
            in_specs=[pl.BlockSpec((tm, f), lambda i, te, nt: (i, 0)),
                      pl.BlockSpec((1, f, d), lambda i, te, nt: (te[i], 0, 0))],
            out_specs=pl.BlockSpec((tm, ROW_SUB, LANES), lambda i, te, nt: (i, 0, 0)),
            scratch_shapes=[pltpu.VMEM((f, d), BF16)]),
        out_shape=jax.ShapeDtypeStruct((r, ROW_SUB, LANES), F32),
        compiler_params=_cparams(("arbitrary",)),
        name="ffn_down",
    )(tile_expert, n_tiles, h, wd)


def _ffn_down_res_kernel(h_ref, wd_ref, x_ref, mod_ref, yc_ref, yl_ref, wdb_ref):
    i = pl.program_id(0)

    @pl.when(i == 0)
    def _():
        wdb_ref[...] = wd_ref[...].astype(BF16)

    y = jnp.dot(h_ref[...], wdb_ref[...], preferred_element_type=F32)
    res = x_ref[...] + mod_ref[0, 5:6, :] * y
    n_ctx = R_CTX // h_ref.shape[0]

    @pl.when(i < n_ctx)
    def _():
        yc_ref[...] = res

    @pl.when(i >= n_ctx)
    def _():
        yl_ref[...] = res


def ffn_down_residual(h, wd, x, mod):
    tm = 1024
    r, f = h.shape
    d = wd.shape[1]
    return pl.pallas_call(
        _ffn_down_res_kernel,
        grid=(r // tm,),
        in_specs=[pl.BlockSpec((tm, f), lambda i: (i, 0)),
                  pl.BlockSpec((f, d), lambda i: (0, 0), pipeline_mode=pl.Buffered(1)),
                  pl.BlockSpec((tm, d), lambda i: (i, 0)),
                  pl.BlockSpec((1, 8, d), lambda i: (_mod_row_of_tile(i, tm), 0, 0))],
        out_specs=_ctx_lat_specs(tm, d, 1),
        out_shape=(jax.ShapeDtypeStruct((R_CTX, d), F32), jax.ShapeDtypeStruct((r - R_CTX, d), F32)),
        scratch_shapes=[pltpu.VMEM((f, d), BF16)],
        compiler_params=_cparams(("arbitrary",)),
        name="ffn_down_residual",
    )(h, wd, x, mod)


def dense_swiglu_residual(xn, x, mod, wg, wu, wd):
    t = xn.shape[0]
    tm = 1024
    n_tiles = t // tm
    te = jnp.zeros((n_tiles,), jnp.int32)
    nt = jnp.full((1,), n_tiles, jnp.int32)
    h = ffn_up(te, nt, xn, wg[None], wu[None], tm=tm, tf=wg.shape[1] // 2, weight_buffers=1)
    return ffn_down_residual(h, wd, x, mod)


MOE_TM = 512
DISPATCH_TM = 512


def _moe_routing(logits, tm):
    t = logits.shape[0]
    n_assign = t * TOP_K
    top_v, top_i = lax.top_k(logits, TOP_K)
    gate = jax.nn.softmax(top_v, axis=-1)
    flat_e = top_i.reshape(-1).astype(jnp.int32)
    onehot = (flat_e[:, None] == jnp.arange(N_EXP, dtype=jnp.int32)[None, :])
    blk = LANES
    oh = onehot.astype(F32).reshape(n_assign // blk, blk, N_EXP)
    tril = jnp.tril(jnp.ones((blk, blk), F32))
    within = jnp.einsum('ij,bjk->bik', tril, oh)
    blk_tot = within[:, -1, :]
    blk_off = jnp.cumsum(blk_tot, axis=0) - blk_tot
    csum = (within + blk_off[:, None, :]).reshape(n_assign, N_EXP)
    rank = jnp.sum(jnp.where(onehot, csum - 1.0, 0.0), axis=1).astype(jnp.int32)
    counts = csum[-1].astype(jnp.int32)
    padded = (counts + tm - 1) // tm * tm
    grp_start = jnp.cumsum(padded) - padded
    raw_start = jnp.cumsum(counts) - counts
    slot_of_assign = jnp.sum(jnp.where(onehot, grp_start[None, :], 0), axis=1) + rank

    r_max = n_assign + N_EXP * tm
    tile_start = jnp.arange(r_max // tm, dtype=jnp.int32) * tm
    tile_expert = jnp.sum((tile_start[:, None] >= (grp_start + padded)[None, :]).astype(jnp.int32), axis=1)
    tile_expert = jnp.minimum(tile_expert, N_EXP - 1).astype(jnp.int32)
    n_tiles = (jnp.sum(padded) // tm).astype(jnp.int32).reshape(1)

    order = jnp.argsort(flat_e, stable=True).astype(jnp.int32)
    e_slot = jnp.repeat(tile_expert, tm)
    j = jnp.arange(r_max, dtype=jnp.int32) - grp_start[e_slot]
    src = jnp.clip(raw_start[e_slot] + j, 0, n_assign - 1)
    tok_of_slot = jnp.where(j < counts[e_slot], order[src] // TOP_K, 0)
    return gate, slot_of_assign, tok_of_slot, tile_expert, n_tiles


ROW_SUB = D_MODEL // LANES


def _rows_to_tiles(o3_ref, x):
    for j in range(ROW_SUB):
        o3_ref[:, j, :] = x[:, j * LANES:(j + 1) * LANES].astype(o3_ref.dtype)


def _start_row_gather(idx_ref, src_ref, dst, sem, n_rows):
    def body(q, carry):
        for u in range(2):
            r = 2 * q + u
            pltpu.make_async_copy(src_ref.at[idx_ref[0, 0, r]], dst.at[r], sem).start(priority=u)
        return carry

    lax.fori_loop(0, n_rows // 2, body, 0, unroll=4)


def _wait_row_gather(src_ref, dst, sem, n_rows):
    pltpu.make_async_copy(src_ref.at[pl.ds(0, n_rows)], dst, sem).wait()


def _tiles_to_rows(tiles, rows_ref):
    for j in range(ROW_SUB):
        rows_ref[:, j * LANES:(j + 1) * LANES] = tiles[:, j, :]


def _dispatch_kernel(nt_ref, idx_ref, idx_next_ref, src_ref, o_ref, buf, rows, sem):
    i = pl.program_id(0)
    nt = nt_ref[0]
    tm = o_ref.shape[0]
    slot = i % 2

    @pl.when(i == 0)
    def _():
        _start_row_gather(idx_ref, src_ref, buf.at[0], sem.at[0], tm)

    @pl.when(i + 1 < nt)
    def _():
        _start_row_gather(idx_next_ref, src_ref, buf.at[1 - slot], sem.at[1 - slot], tm)

    @pl.when(i < nt)
    def _():
        _wait_row_gather(src_ref, buf.at[slot], sem.at[slot], tm)
        _tiles_to_rows(buf.at[slot], rows)
        o_ref[...] = rows[...].astype(o_ref.dtype)

    @pl.when(i >= nt)
    def _():
        o_ref[...] = jnp.zeros(o_ref.shape, o_ref.dtype)


def moe_dispatch(n_tiles, tok_of_slot, xn3, tm):
    r = tok_of_slot.shape[0]
    last = r // tm - 1
    idx = tok_of_slot.reshape(r // tm, 1, tm)
    return pl.pallas_call(
        _dispatch_kernel,
        grid_spec=pltpu.PrefetchScalarGridSpec(
            num_scalar_prefetch=1,
            grid=(r // tm,),
            in_specs=[pl.BlockSpec((1, 1, tm), lambda i, nt: (i, 0, 0), memory_space=pltpu.SMEM),
                      pl.BlockSpec((1, 1, tm), lambda i, nt: (jnp.minimum(i + 1, last), 0, 0),
                                   memory_space=pltpu.SMEM),
                      pl.BlockSpec(memory_space=pl.ANY)],
            out_specs=pl.BlockSpec((tm, D_MODEL), lambda i, nt: (i, 0)),
            scratch_shapes=[pltpu.VMEM((2, tm, ROW_SUB, LANES), F32), pltpu.VMEM((tm, D_MODEL), F32),
                            pltpu.SemaphoreType.DMA((2,))]),
        out_shape=jax.ShapeDtypeStruct((r, D_MODEL), BF16),
        compiler_params=_cparams(("arbitrary",)),
        name="moe_dispatch",
    )(n_tiles, idx, idx, xn3)


COMBINE_TM = 256


def _combine_kernel(idx_ref, idx_next_ref, ys_ref, x_ref, gate_ref, mod_ref, oc_ref, ol_ref, buf, rows, sem):
    i = pl.program_id(0)
    tm = oc_ref.shape[0]
    n_rows = TOP_K * tm
    slot = i % 2

    @pl.when(i == 0)
    def _():
        _start_row_gather(idx_ref, ys_ref, buf.at[0], sem.at[0], n_rows)

    @pl.when(i + 1 < pl.num_programs(0))
    def _():
        _start_row_gather(idx_next_ref, ys_ref, buf.at[1 - slot], sem.at[1 - slot], n_rows)

    _wait_row_gather(ys_ref, buf.at[slot], sem.at[slot], n_rows)
    _tiles_to_rows(buf.at[slot], rows)
    f = gate_ref[:, 0:1] * rows[0:tm, :] + gate_ref[:, 1:2] * rows[tm:2 * tm, :]
    res = x_ref[...] + mod_ref[0, 5:6, :] * f

    @pl.when(i < R_CTX // COMBINE_TM)
    def _():
        oc_ref[...] = res

    @pl.when(i >= R_CTX // COMBINE_TM)
    def _():
        ol_ref[...] = res


def moe_combine(slot_of_assign, ys3, x, gate, mod):
    t, d = x.shape
    tm = COMBINE_TM
    idx = slot_of_assign.reshape(t // tm, tm, TOP_K).transpose(0, 2, 1).reshape(t // tm, 1, TOP_K * tm)
    last = t // tm - 1
    n_ctx = R_CTX // tm
    return pl.pallas_call(
        _combine_kernel,
        grid=(t // tm,),
        in_specs=[pl.BlockSpec((1, 1, TOP_K * tm), lambda i: (i, 0, 0), memory_space=pltpu.SMEM),
                  pl.BlockSpec((1, 1, TOP_K * tm), lambda i: (jnp.minimum(i + 1, last), 0, 0),
                               memory_space=pltpu.SMEM),
                  pl.BlockSpec(memory_space=pl.ANY),
                  pl.BlockSpec((tm, d), lambda i: (i, 0)),
                  pl.BlockSpec((tm, TOP_K), lambda i: (i, 0)),
                  pl.BlockSpec((1, 8, d), lambda i: (_mod_row_of_tile(i, tm), 0, 0))],
        out_specs=(pl.BlockSpec((tm, d), lambda i: (jnp.minimum(i, n_ctx - 1), 0)),
                   pl.BlockSpec((tm, d), lambda i: (jnp.maximum(i - n_ctx, 0), 0))),
        out_shape=(jax.ShapeDtypeStruct((R_CTX, d), F32), jax.ShapeDtypeStruct((t - R_CTX, d), F32)),
        scratch_shapes=[pltpu.VMEM((2, TOP_K * tm, ROW_SUB, LANES), F32), pltpu.VMEM((TOP_K * tm, d), F32),
                        pltpu.SemaphoreType.DMA((2,))],
        compiler_params=_cparams(("arbitrary",)),
        name="moe_combine",
    )(idx, idx, ys3, x, gate, mod)


def moe_swiglu_residual(xn3, xn, x, mod, w_router, wg, wu, wd):
    tm = MOE_TM
    logits = jnp.dot(xn.astype(F32), w_router, precision=lax.Precision.HIGHEST)
    gate, slot_of_assign, tok_of_slot, tile_expert, n_tiles = _moe_routing(logits, tm)
    xs = moe_dispatch(n_tiles * (tm // DISPATCH_TM), tok_of_slot, xn3, DISPATCH_TM)
    h = ffn_up(tile_expert, n_tiles, xs, wg, wu, tm=tm, tf=wg.shape[2] // 2, weight_buffers=2)
    ys3 = ffn_down(tile_expert, n_tiles, h, wd, tm=tm)
    return moe_combine(slot_of_assign, ys3, x, gate, mod)


def _layer(x, cond, lp, l, ctx, tabs_a, tabs_d):
    mod = modulation(cond, lp['w_mod'], lp['b_mod'], l).reshape(cond.shape[0], 6, D_MODEL)
    mod = jnp.pad(mod, ((0, 0), (0, 2), (0, 0)))
    p = in_proj(x, mod, lp['g_norm_mix'], lp['w_in_p'], l)

    mw = pack_mla_weights(lp)
    q_a, k_a, v_a, ckv, kr = mla_prep(p, mw, tabs_a)
    kc_a, vc_a = mla_prep_cache(ctx['mla_ckv'], ctx['mla_krope'], mw, l)
    ya_c = mla_attention(q_a, k_a, v_a, None, None, row0=0, n_seq=N_CTX_SEQ, t_seq=T_CTX, tq=T_CTX)
    ya_l = mla_attention(q_a, k_a, v_a, kc_a, vc_a, row0=R_CTX, n_seq=N_LAT_SEQ, t_seq=T_LAT, tq=ATTN_TQ)

    rw = pack_rglru_weights(lp)
    yb_c, st_rg = rglru(p, rw, jnp.zeros((N_CTX_SEQ, 2, W_B), F32), row0=0, n_seq=N_CTX_SEQ, t_seq=T_CTX)
    yb_l, _ = rglru(p, rw, ctx['rglru'], row0=R_CTX, n_seq=N_LAT_SEQ, t_seq=T_LAT)

    bias_sm = jnp.zeros((LANES,), F32).at[SM_MI:SM_MI + 2 * H_C].set(lp['b_ml_i'].reshape(-1))
    bias_sm = bias_sm.at[SM_MF:SM_MF + 2 * H_C].set(lp['b_ml_f'].reshape(-1)).reshape(1, LANES)
    c0_ctx = jnp.zeros((N_CTX_SEQ, 1, 2, H_C, DH_C, DH_C), F32)
    n0_ctx = jnp.zeros((N_CTX_SEQ, 1, 2, H_C, DH_C, 1), F32)
    m0_ctx = jnp.zeros((N_CTX_SEQ, 1, 2 * H_C), F32)
    m0_lat = ctx['mlstm_m'][:, l].reshape(N_LAT_SEQ, 1, 2 * H_C)
    yc_c, c_fin, n_fin, m_fin = mlstm(p, bias_sm, lp['g_ml_out'], c0_ctx, n0_ctx, m0_ctx,
                                      row0=0, n_seq=N_CTX_SEQ, t_seq=T_CTX)
    yc_l, _, _, _ = mlstm(p, bias_sm, lp['g_ml_out'], ctx['mlstm_C'], ctx['mlstm_n'][..., None], m0_lat,
                          row0=R_CTX, n_seq=N_LAT_SEQ, t_seq=T_LAT, layer=l)

    lambda_init = 0.8 - 0.6 * math.exp(-0.3 * l)
    qd, kd_own, kd_plain = diff_prep(p, lp, tabs_d)
    yd_c = diff_attention(lp['diff_lambda'], qd, kd_own, p, None, None, lp['g_diff_sub'],
                          row0=0, n_seq=N_CTX_SEQ, t_seq=T_CTX, tq=T_CTX, lambda_init=lambda_init)
    yd_l = diff_attention(lp['diff_lambda'], qd, kd_own, p, ctx['diff_k'], ctx['diff_v'], lp['g_diff_sub'],
                          row0=R_CTX, n_seq=N_LAT_SEQ, t_seq=T_LAT, tq=ATTN_TQ, lambda_init=lambda_init, layer=l)

    merged = merge((ya_c, yb_c, yc_c, yd_c), (ya_l, yb_l, yc_l, yd_l), p, x, mod, lp['w_br'].astype(BF16),
                   lp['w_out'].astype(BF16), lp['g_norm_ffn'], rows_as_tiles='moe' in lp)
    if 'ffn' in lp:
        x, xn = merged
        x = dense_swiglu_residual(xn, x, mod, *lp['ffn'])
    else:
        x, xn, xn3 = merged
        x = moe_swiglu_residual(xn3, xn, x, mod, *lp['moe'])

    dv0 = _seg_block('dv') * W_BRANCH
    ctx_out = (ckv[:R_CTX].reshape(N_CTX_SEQ, T_CTX, KV_RANK),
               kr[:R_CTX, :ROPE_A].reshape(N_CTX_SEQ, T_CTX, ROPE_A),
               kd_plain[:R_CTX].reshape(N_CTX_SEQ, T_CTX, 2, H_D, DH_D),
               p[:R_CTX, dv0:dv0 + W_BRANCH].astype(F32).reshape(N_CTX_SEQ, T_CTX, H_D, 2 * DH_D),
               st_rg,
               c_fin,
               n_fin.reshape(N_CTX_SEQ, 2, H_C, DH_C),
               m_fin.reshape(N_CTX_SEQ, 2, H_C))
    return x, ctx_out


def kernel(x_prompt, x_sample, cache_mla_ckv, cache_mla_krope, cache_diff_k, cache_diff_v,
           state_rglru, state_mlstm_C, state_mlstm_n, state_mlstm_m, c, c_ctx,
           w_mod, b_mod, g_norm_mix, g_norm_ffn, w_in, g_mla_qlat, w_mla_uq, g_mla_kvlat, w_mla_ukv,
           g_mla_qn, g_mla_kn, w_conv_rg, b_conv_rg, w_rg_a, b_rg_a, w_rg_x, b_rg_x, rg_lambda,
           b_ml_i, b_ml_f, g_ml_out, g_diff_qn, g_diff_kn, diff_lambda, g_diff_sub, w_br, w_out,
           w_ffn_gate, w_ffn_up, w_ffn_down, w_router, w_moe_gate, w_moe_up, w_moe_down):
    assert x_prompt.shape == (N_CTX_SEQ, T_CTX, D_MODEL) and x_sample.shape == (N_LAT_SEQ, T_LAT, D_MODEL)
    tabs_a = _rope_tables(ROPE_A, (NOPE_A,), MLA_TM)
    tabs_d = _rope_tables(DH_D, (0, DH_D), DIFF_TM)
    cond = jnp.concatenate([c_ctx.reshape(1, D_MODEL), c, jnp.zeros((16 - 1 - N_LAT_SEQ, D_MODEL), F32)], axis=0)
    x = (x_prompt.reshape(R_CTX, D_MODEL), x_sample.reshape(R_LAT, D_MODEL))
    w_in_p = pack_w_in(w_in)
    krope_pad = jnp.pad(cache_mla_krope, ((0, 0), (0, 0), (0, 0), (0, LANES - ROPE_A)))
    diff_k_rows = cache_diff_k.reshape(N_LAT_SEQ, DEPTH, PAST_LEN, W_BRANCH)
    diff_v_rows = cache_diff_v.reshape(N_LAT_SEQ, DEPTH, PAST_LEN, W_BRANCH)
    new = []
    for l in range(DEPTH):
        lp = dict(w_mod=w_mod, b_mod=b_mod, g_norm_mix=g_norm_mix[l], g_norm_ffn=g_norm_ffn[l], w_in_p=w_in_p,
                  g_mla_qlat=g_mla_qlat[l], w_mla_uq=w_mla_uq[l], g_mla_kvlat=g_mla_kvlat[l], w_mla_ukv=w_mla_ukv[l],
                  g_mla_qn=g_mla_qn[l], g_mla_kn=g_mla_kn[l], w_conv_rg=w_conv_rg[l], b_conv_rg=b_conv_rg[l],
                  w_rg_a=w_rg_a[l], b_rg_a=b_rg_a[l], w_rg_x=w_rg_x[l], b_rg_x=b_rg_x[l], rg_lambda=rg_lambda[l],
                  b_ml_i=b_ml_i[l], b_ml_f=b_ml_f[l], g_ml_out=g_ml_out[l], g_diff_qn=g_diff_qn[l],
                  g_diff_kn=g_diff_kn[l], diff_lambda=diff_lambda[l], g_diff_sub=g_diff_sub[l],
                  w_br=w_br[l], w_out=w_out[l])
        if l % 2 == 0:
            lp['ffn'] = (w_ffn_gate[l // 2], w_ffn_up[l // 2], w_ffn_down[l // 2])
        else:
            lp['moe'] = (w_router[l // 2], w_moe_gate[l // 2], w_moe_up[l // 2], w_moe_down[l // 2])
        ctx_l = dict(mla_ckv=cache_mla_ckv, mla_krope=krope_pad, diff_k=diff_k_rows, diff_v=diff_v_rows,
                     rglru=state_rglru[:, l], mlstm_C=state_mlstm_C, mlstm_n=state_mlstm_n, mlstm_m=state_mlstm_m)
        x, st = _layer(x, cond, lp, l, ctx_l, tabs_a, tabs_d)
        new.append(st)
    outs = tuple(jnp.stack([s[i] for s in new], axis=1) for i in range(8))
    x_ctx, x_lat = x
    return (x_ctx.reshape(N_CTX_SEQ, T_CTX, D_MODEL), x_lat.reshape(N_LAT_SEQ, T_LAT, D_MODEL)) + outs
```

```python
import functools
import math

import jax
import jax.numpy as jnp
import numpy as np
from jax import lax
from jax.experimental import pallas as pl
from jax.experimental.pallas import tpu as pltpu

D_MODEL = 1024
DEPTH = 2
GRID_W = 64
ROPE_BASE = 10000.0
EPS = 1e-6
N_BRANCH = 4
W_BRANCH = D_MODEL // 2

H_A = 8
NOPE_A = 64
ROPE_A = 32
V_A = W_BRANCH // H_A
Q_RANK = D_MODEL // 4
KV_RANK = D_MODEL // 8
MLA_SCALE = (NOPE_A + ROPE_A) ** -0.5

W_B = W_BRANCH
NB_B = 8
BW_B = W_B // NB_B
CONV_W = 4
RG_C = 8.0

H_C = 4
DH_C = W_BRANCH // H_C
ML_CHUNK = 128

H_D = 4
DH_D = W_BRANCH // (2 * H_D)
DIFF_SCALE = DH_D ** -0.5

N_EXP = 8
TOP_K = 2

V7X_VMEM_LIMIT_BYTES = 56 * 1024 * 1024
LANES = 128
SUBLANES = 8

BF16 = jnp.bfloat16
F32 = jnp.float32

P_GATE = 0
P_SEG = N_BRANCH * D_MODEL
SEG_NAMES = ('rg_x', 'rg_gate', 'mq', 'mk', 'mv', 'mo', 'dq', 'dk', 'dv')
P_QLAT = P_SEG + 9 * W_BRANCH
P_KVLAT = P_QLAT + Q_RANK
P_SMALL = P_KVLAT + KV_RANK
P_WIDTH = P_SMALL + LANES
SM_MI = ROPE_A
SM_MF = ROPE_A + 2 * H_C


def _seg_block(name):
    return (P_SEG + SEG_NAMES.index(name) * W_BRANCH) // W_BRANCH


N_CTX_SEQ, T_CTX = 16, 256
N_LAT_SEQ, T_LAT = 8, 1024
PAST_LEN = 512
R_CTX = N_CTX_SEQ * T_CTX
R_LAT = N_LAT_SEQ * T_LAT
R_ALL = R_CTX + R_LAT


def _cparams(sem):
    return pltpu.CompilerParams(dimension_semantics=sem, vmem_limit_bytes=V7X_VMEM_LIMIT_BYTES)


def _mod_row_of_tile(i, tm):
    n_ctx_tiles = R_CTX // tm
    per_seq = T_LAT // tm
    return jnp.where(i < n_ctx_tiles, 0, 1 + (i - n_ctx_tiles) // per_seq)


def _pos_block_of_tile(i, tm):
    n_ctx_tiles = R_CTX // tm
    per_seq = T_LAT // tm
    return jnp.where(i < n_ctx_tiles, per_seq, (i - n_ctx_tiles) % per_seq)


def _mod_kernel(c_ref, w_ref, b_ref, o_ref):
    c = c_ref[...]
    s = (c * jax.nn.sigmoid(c)).astype(BF16)
    o_ref[...] = jnp.dot(s, w_ref[0].astype(BF16), preferred_element_type=F32) + b_ref[0]


def modulation(cond, w_mod, b_mod, l):
    m, d = cond.shape
    n = w_mod.shape[2]
    tn = 1536
    return pl.pallas_call(
        _mod_kernel,
        grid=(n // tn,),
        in_specs=[pl.BlockSpec((m, d), lambda j: (0, 0)),
                  pl.BlockSpec((1, d, tn), lambda j: (l, 0, j)),
                  pl.BlockSpec((1, 1, tn), lambda j: (l, 0, j))],
        out_specs=pl.BlockSpec((m, tn), lambda j: (0, j)),
        out_shape=jax.ShapeDtypeStruct((m, n), F32),
        compiler_params=_cparams(("arbitrary",)),
        name="modulation",
    )(cond, w_mod, b_mod.reshape(b_mod.shape[0], 1, n))


def _in_proj_kernel(xc_ref, xl_ref, mod_ref, g_ref, w_ref, o_ref, xn_ref):
    @pl.when(pl.program_id(1) == 0)
    def _():
        x = jnp.where(pl.program_id(0) < R_CTX // xc_ref.shape[0], xc_ref[...], xl_ref[...])
        r = lax.rsqrt(jnp.mean(x * x, axis=-1, keepdims=True) + EPS)
        sh = mod_ref[0, 0:1, :]
        sc = mod_ref[0, 1:2, :]
        xn_ref[...] = ((x * r * g_ref[...]) * (1.0 + sc) + sh).astype(BF16)

    o_ref[...] = jnp.dot(xn_ref[...], w_ref[0], preferred_element_type=F32).astype(o_ref.dtype)


def _ctx_lat_specs(tm, width, grid_rank):
    n_ctx = R_CTX // tm
    if grid_rank == 1:
        return (pl.BlockSpec((tm, width), lambda i: (jnp.minimum(i, n_ctx - 1), 0)),
                pl.BlockSpec((tm, width), lambda i: (jnp.maximum(i - n_ctx, 0), 0)))
    return (pl.BlockSpec((tm, width), lambda i, j: (jnp.minimum(i, n_ctx - 1), 0)),
            pl.BlockSpec((tm, width), lambda i, j: (jnp.maximum(i - n_ctx, 0), 0)))


def in_proj(x_pair, mod, g, w_p, l):
    tm, tn = 1024, P_WIDTH // 4
    m, d = R_ALL, D_MODEL
    n = w_p.shape[2]
    return pl.pallas_call(
        _in_proj_kernel,
        grid=(m // tm, n // tn),
        in_specs=[*_ctx_lat_specs(tm, d, 2),
                  pl.BlockSpec((1, 8, d), lambda i, j: (_mod_row_of_tile(i, tm), 0, 0)),
                  pl.BlockSpec((1, d), lambda i, j: (0, 0)),
                  pl.BlockSpec((1, d, tn), lambda i, j: (l, 0, j))],
        out_specs=pl.BlockSpec((tm, tn), lambda i, j: (i, j)),
        out_shape=jax.ShapeDtypeStruct((m, n), BF16),
        scratch_shapes=[pltpu.VMEM((tm, d), BF16)],
        compiler_params=_cparams(("arbitrary", "arbitrary")),
        name="in_proj",
    )(*x_pair, mod, g.reshape(1, d), w_p)


_O_RG = Q_RANK + KV_RANK + ROPE_A
_O_MI = _O_RG + 6 * W_BRANCH
_O_DQ = _O_MI + 4 * H_C
_O_GATE = _O_DQ + 3 * W_BRANCH
_PACK_PIECES = ((_O_GATE, N_BRANCH * D_MODEL, P_GATE), (_O_RG, 6 * W_BRANCH, P_SEG),
                (_O_DQ, 3 * W_BRANCH, P_SEG + 6 * W_BRANCH), (0, Q_RANK + KV_RANK, P_QLAT),
                (Q_RANK + KV_RANK, ROPE_A, P_SMALL), (_O_MI, 4 * H_C, P_SMALL + ROPE_A))
PACK_ROWS = 128


def _pack_w_in_kernel(w_ref, o_ref):
    n_in = w_ref.shape[2]
    o_ref[0, :, P_SMALL:P_WIDTH] = jnp.zeros((PACK_ROWS, LANES), o_ref.dtype)
    for src, width, dst in _PACK_PIECES:
        lo = src // LANES * LANES
        hi = min(_round_up(src + width, LANES), n_in)
        win = w_ref[0, :, lo:hi]
        o_ref[0, :, dst:dst + width] = win[:, src - lo:src - lo + width].astype(o_ref.dtype)


def pack_w_in(w_in):
    depth, d, n_in = w_in.shape
    return pl.pallas_call(
        _pack_w_in_kernel,
        grid=(depth, d // PACK_ROWS),
        in_specs=[pl.BlockSpec((1, PACK_ROWS, n_in), lambda l, i: (l, i, 0))],
        out_specs=pl.BlockSpec((1, PACK_ROWS, P_WIDTH), lambda l, i: (l, i, 0)),
        out_shape=jax.ShapeDtypeStruct((depth, d, P_WIDTH), BF16),
        compiler_params=_cparams(("arbitrary", "arbitrary")),
        name="pack_w_in",
    )(w_in)


def _round_up(x, m):
    return (x + m - 1) // m * m


def _rope_partner(rot_dim, lane_starts):
    nf = rot_dim // 4
    partner = np.arange(LANES)
    for s0 in lane_starts:
        for a in range(2):
            lo = s0 + a * 2 * nf
            partner[lo:lo + nf] = np.arange(lo + nf, lo + 2 * nf)
            partner[lo + nf:lo + 2 * nf] = np.arange(lo, lo + nf)
    return partner


def _rope_tables(rot_dim, lane_starts, tm):
    rows = T_LAT // GRID_W
    r, c = np.meshgrid(np.arange(rows, dtype=np.float32), np.arange(GRID_W, dtype=np.float32), indexing='ij')
    nf = rot_dim // 4
    inv = (np.float32(ROPE_BASE) ** (-np.arange(nf, dtype=np.float32) / np.float32(nf))).astype(np.float32)
    ang = np.stack([r.reshape(-1)[:, None] * inv, c.reshape(-1)[:, None] * inv], axis=1).astype(np.float32)
    cos, sin = np.cos(ang).astype(np.float32), np.sin(ang).astype(np.float32)
    tc = np.ones((T_LAT + tm, LANES), np.float32)
    ta = np.zeros((T_LAT + tm, LANES), np.float32)
    tb = np.zeros((T_LAT + tm, LANES), np.float32)
    for s0 in lane_starts:
        for a in range(2):
            lo = s0 + a * 2 * nf
            tc[:T_LAT, lo:lo + nf] = cos[:, a]
            tc[:T_LAT, lo + nf:lo + 2 * nf] = cos[:, a]
            ta[:T_LAT, lo:lo + nf] = -sin[:, a]
            tb[:T_LAT, lo + nf:lo + 2 * nf] = sin[:, a]
    return jnp.asarray(tc), jnp.asarray(ta + tb)


MLA_TM = 512
ATTN_TQ = 1024
MLA_HEADS_PER_STEP = 8
QK_A = NOPE_A + ROPE_A


def _mla_prep_kernel(*refs, has_q, norm_ckv):
    if has_q:
        (qlat_ref, gq_ref, wuq_ref, gqn_ref, kv_ref, sm_ref, gkv_ref, wkc_ref, wv_ref, gkn_ref,
         c_ref, s_ref, q_o, k_o, v_o, ckv_o, kr_o) = refs
        c, sn = c_ref[...], s_ref[...]
    else:
        (kv_ref, sm_ref, gkv_ref, wkc_ref, wv_ref, gkn_ref, k_o, v_o) = refs

    def heads(z, g_ref, o_ref, scale):
        for h in range(H_A):
            s = z[:, h * LANES:(h + 1) * LANES]
            r = lax.rsqrt(jnp.sum(s * s, axis=-1, keepdims=True) * (1.0 / QK_A) + EPS)
            y = s * r * g_ref[0:1, :]
            if has_q:
                sw = z[:, (H_A + h) * LANES:(H_A + h + 1) * LANES]
                y = y * c + (sw * r * g_ref[1:2, :]) * sn
            if scale != 1.0:
                y = y * scale
            o_ref[:, h * LANES:(h + 1) * LANES] = y.astype(o_ref.dtype)

    if has_q:
        ql = qlat_ref[...].astype(F32)
        qn = ql * lax.rsqrt(jnp.mean(ql * ql, axis=-1, keepdims=True) + EPS) * gq_ref[...]
        q = jnp.dot(qn.astype(BF16), wuq_ref[...], preferred_element_type=F32)
        heads(q, gqn_ref, q_o, MLA_SCALE)

    if has_q:
        kv, sm = kv_ref[...].astype(F32), sm_ref[...]
    else:
        kv, sm = kv_ref[0, 0].astype(F32), sm_ref[0, 0]
    if norm_ckv:
        ckv = kv * lax.rsqrt(jnp.mean(kv * kv, axis=-1, keepdims=True) + EPS) * gkv_ref[...]
    else:
        ckv = kv
    ckv_b = ckv.astype(BF16)
    kin = jnp.concatenate([ckv_b, sm.astype(BF16)], axis=1)
    wkc = wkc_ref[...] if has_q else wkc_ref[:, :H_A * LANES]
    k = jnp.dot(kin, wkc, preferred_element_type=F32)
    heads(k, gkn_ref, k_o, 1.0)
    v_o[...] = jnp.dot(ckv_b, wv_ref[...], preferred_element_type=F32).astype(v_o.dtype)
    if has_q:
        ckv_o[...] = ckv
        kr_o[...] = sm.astype(F32)


def pack_mla_weights(lp):
    wuq = lp['w_mla_uq'].reshape(Q_RANK, H_A, QK_A)
    wuq_p = jnp.pad(wuq, ((0, 0), (0, 0), (0, LANES - QK_A))).reshape(Q_RANK, H_A * LANES).astype(BF16)
    wukv = lp['w_mla_ukv'].reshape(KV_RANK, H_A, NOPE_A + V_A)
    wk = jnp.pad(wukv[:, :, :NOPE_A], ((0, 0), (0, 0), (0, LANES - NOPE_A))).reshape(KV_RANK, H_A * LANES)
    place = np.zeros((LANES, H_A, LANES), np.float32)
    for h in range(H_A):
        place[np.arange(ROPE_A), h, NOPE_A + np.arange(ROPE_A)] = 1.0
    wkc = jnp.concatenate([wk, jnp.asarray(place.reshape(LANES, H_A * LANES))], axis=0).astype(BF16)
    wv = wukv[:, :, NOPE_A:]
    wv_even = jnp.pad(wv, ((0, 0), (0, 0), (0, LANES - V_A)))
    wv_odd = jnp.pad(wv, ((0, 0), (0, 0), (LANES - V_A, 0)))
    odd = (np.arange(H_A) % 2 == 1)[None, :, None]
    wv_p = jnp.where(odd, wv_odd, wv_even).reshape(KV_RANK, H_A * LANES).astype(BF16)
    partner = _rope_partner(ROPE_A, (NOPE_A,))
    cols = (np.arange(H_A)[:, None] * LANES + partner[None, :]).reshape(-1)
    with_partner = lambda w: jnp.concatenate([w, w[:, cols]], axis=1)
    pad_g = lambda g: jnp.pad(g, (0, LANES - QK_A))
    gain2 = lambda g: jnp.stack([pad_g(g), pad_g(g)[partner]])
    return dict(wuq=with_partner(wuq_p), wkc=with_partner(wkc), wv=wv_p,
                gqn=gain2(lp['g_mla_qn']), gkn=gain2(lp['g_mla_kn']),
                gq=lp['g_mla_qlat'].reshape(1, Q_RANK), gkv=lp['g_mla_kvlat'].reshape(1, KV_RANK))


def mla_prep(p, mw, tabs):
    tm = MLA_TM
    n = R_ALL // tm
    full = lambda shape: pl.BlockSpec(shape, lambda i: (0, 0))
    tab = pl.BlockSpec((tm, LANES), lambda i: (_pos_block_of_tile(i, tm), 0))
    wide = H_A * LANES
    out_shape = (jax.ShapeDtypeStruct((R_ALL, wide), BF16), jax.ShapeDtypeStruct((R_ALL, wide), BF16),
                 jax.ShapeDtypeStruct((R_ALL, wide), BF16), jax.ShapeDtypeStruct((R_ALL, KV_RANK), F32),
                 jax.ShapeDtypeStruct((R_ALL, LANES), F32))
    row = lambda w: pl.BlockSpec((tm, w), lambda i: (i, 0))
    return pl.pallas_call(
        functools.partial(_mla_prep_kernel, has_q=True, norm_ckv=True),
        grid=(n,),
        in_specs=[pl.BlockSpec((tm, Q_RANK), lambda i: (i, P_QLAT // Q_RANK)), full((1, Q_RANK)),
                  full((Q_RANK, 2 * wide)), full((2, LANES)),
                  pl.BlockSpec((tm, KV_RANK), lambda i: (i, P_KVLAT // KV_RANK)),
                  pl.BlockSpec((tm, LANES), lambda i: (i, P_SMALL // LANES)), full((1, KV_RANK)),
                  full((2 * LANES, 2 * wide)), full((KV_RANK, wide)), full((2, LANES)), tab, tab],
        out_specs=(row(wide), row(wide), row(wide), row(KV_RANK), row(LANES)),
        out_shape=out_shape,
        compiler_params=_cparams(("arbitrary",)),
        name="mla_prep",
    )(p, mw['gq'], mw['wuq'], mw['gqn'], p, p, mw['gkv'], mw['wkc'], mw['wv'], mw['gkn'], *tabs)


def mla_prep_cache(ckv_c, kr_c, mw, l):
    tm = PAST_LEN
    r = ckv_c.shape[0] * PAST_LEN
    full = lambda shape: pl.BlockSpec(shape, lambda i: (0, 0))
    wide = H_A * LANES
    row = lambda w: pl.BlockSpec((tm, w), lambda i: (i, 0))
    cache = lambda w: pl.BlockSpec((1, 1, PAST_LEN, w), lambda i: (i, l, 0, 0))
    return pl.pallas_call(
        functools.partial(_mla_prep_kernel, has_q=False, norm_ckv=False),
        grid=(r // tm,),
        in_specs=[cache(KV_RANK), cache(LANES), full((1, KV_RANK)), full((2 * LANES, 2 * wide)),
                  full((KV_RANK, wide)), full((2, LANES))],
        out_specs=(row(wide), row(wide)),
        out_shape=(jax.ShapeDtypeStruct((r, wide), BF16), jax.ShapeDtypeStruct((r, wide), BF16)),
        compiler_params=_cparams(("arbitrary",)),
        name="mla_prep_cache",
    )(ckv_c, kr_c, mw['gkv'], mw['wkc'], mw['wv'], mw['gkn'])


_NT = (((1,), (1,)), ((), ()))


def _mla_attn_kernel(*refs, has_cache):
    if has_cache:
        q_ref, ko_ref, vo_ref, kc_ref, vc_ref, o_ref = refs
    else:
        q_ref, ko_ref, vo_ref, o_ref = refs
    n_heads = q_ref.shape[1] // LANES
    sls = [slice(h * LANES, (h + 1) * LANES) for h in range(n_heads)]
    def scores(sl):
        s_o = lax.dot_general(q_ref[:, sl], ko_ref[:, sl], _NT, preferred_element_type=F32)
        s_c = lax.dot_general(q_ref[:, sl], kc_ref[:, sl], _NT, preferred_element_type=F32) if has_cache else None
        return s_o, s_c

    outs = []
    nxt = scores(sls[0])
    for h, sl in enumerate(sls):
        s_o, s_c = nxt
        if h + 1 < n_heads:
            nxt = scores(sls[h + 1])
        m = jnp.max(s_o, axis=-1, keepdims=True)
        if has_cache:
            m = jnp.maximum(m, jnp.max(s_c, axis=-1, keepdims=True))
        e_o = jnp.exp(s_o - m)
        l = jnp.sum(e_o, axis=-1, keepdims=True)
        pv = jnp.dot(e_o.astype(BF16), vo_ref[:, sl], preferred_element_type=F32)
        if has_cache:
            e_c = jnp.exp(s_c - m)
            l = l + jnp.sum(e_c, axis=-1, keepdims=True)
            pv = pv + jnp.dot(e_c.astype(BF16), vc_ref[:, sl], preferred_element_type=F32)
        outs.append(pv * (1.0 / l))
    for p in range(n_heads // 2):
        o_ref[:, p * LANES:(p + 1) * LANES] = (outs[2 * p] + outs[2 * p + 1]).astype(o_ref.dtype)


def mla_attention(q, k, v, kc, vc, *, row0, n_seq, t_seq, tq):
    has_cache = kc is not None
    wh = MLA_HEADS_PER_STEP * LANES
    n_grp = H_A // MLA_HEADS_PER_STEP
    nq = t_seq // tq
    qb0, kb0 = row0 // tq, row0 // t_seq
    in_specs = [pl.BlockSpec((tq, wh), lambda s, p, i: (qb0 + s * nq + i, p)),
                pl.BlockSpec((t_seq, wh), lambda s, p, i: (kb0 + s, p)),
                pl.BlockSpec((t_seq, wh), lambda s, p, i: (kb0 + s, p))]
    args = [q, k, v]
    if has_cache:
        in_specs += [pl.BlockSpec((PAST_LEN, wh), lambda s, p, i: (s, p)),
                     pl.BlockSpec((PAST_LEN, wh), lambda s, p, i: (s, p))]
        args += [kc, vc]
    return pl.pallas_call(
        functools.partial(_mla_attn_kernel, has_cache=has_cache),
        grid=(n_seq, n_grp, nq),
        in_specs=in_specs,
        out_specs=pl.BlockSpec((tq, wh // 2), lambda s, p, i: (s * nq + i, p)),
        out_shape=jax.ShapeDtypeStruct((n_seq * t_seq, W_BRANCH), BF16),
        compiler_params=_cparams(("arbitrary", "arbitrary", "arbitrary")),
        name="mla_attention",
    )(*args)


DIFF_TM = 512


def _diff_prep_kernel(dq_ref, dk_ref, gq_ref, gk_ref, perm_ref, c_ref, s_ref, q_o, ko_o, kp_o):
    c, sn = c_ref[...], s_ref[...]
    lane = lax.broadcasted_iota(jnp.int32, (1, LANES), 1)
    lo = lane < DH_D

    def inv_rms(x):
        x2 = x * x
        s_lo = jnp.sum(jnp.where(lo, x2, 0.0), axis=-1, keepdims=True)
        s_hi = jnp.sum(jnp.where(lo, 0.0, x2), axis=-1, keepdims=True)
        return lax.rsqrt(jnp.where(lo, s_lo, s_hi) * (1.0 / DH_D) + EPS)

    def rotated(n, x_sw, r, g_ref):
        return n * c + (x_sw * r * g_ref[1:2, :]) * sn

    for j in range(W_BRANCH // LANES):
        sl = slice(j * LANES, (j + 1) * LANES)
        xq, xk = dq_ref[:, sl], dk_ref[:, sl]
        q_sw = jnp.dot(xq, perm_ref[...], preferred_element_type=F32)
        k_sw = jnp.dot(xk, perm_ref[...], preferred_element_type=F32)
        xq, xk = xq.astype(F32), xk.astype(F32)
        rq, rk = inv_rms(xq), inv_rms(xk)
        qn = xq * rq * gq_ref[0:1, :]
        kn = xk * rk * gk_ref[0:1, :]
        q_o[:, sl] = (rotated(qn, q_sw, rq, gq_ref) * DIFF_SCALE).astype(q_o.dtype)
        kp_o[:, sl] = kn
        ko_o[:, sl] = rotated(kn, k_sw, rk, gk_ref).astype(ko_o.dtype)


def diff_prep(p, lp, tabs):
    tm = DIFF_TM
    partner = _rope_partner(DH_D, (0, DH_D))
    perm = np.zeros((LANES, LANES), np.float32)
    perm[partner, np.arange(LANES)] = 1.0
    gain2 = lambda g: jnp.stack([jnp.concatenate([g, g]), jnp.concatenate([g, g])[partner]])
    full = lambda shape: pl.BlockSpec(shape, lambda i: (0, 0))
    tab = pl.BlockSpec((tm, LANES), lambda i: (_pos_block_of_tile(i, tm), 0))
    row = pl.BlockSpec((tm, W_BRANCH), lambda i: (i, 0))
    dq_b, dk_b = _seg_block('dq'), _seg_block('dk')
    return pl.pallas_call(
        _diff_prep_kernel,
        grid=(R_ALL // tm,),
        in_specs=[pl.BlockSpec((tm, W_BRANCH), lambda i: (i, dq_b)), pl.BlockSpec((tm, W_BRANCH), lambda i: (i, dk_b)),
                  full((2, LANES)), full((2, LANES)), full((LANES, LANES)), tab, tab],
        out_specs=(row, row, row),
        out_shape=(jax.ShapeDtypeStruct((R_ALL, W_BRANCH), BF16), jax.ShapeDtypeStruct((R_ALL, W_BRANCH), BF16),
                   jax.ShapeDtypeStruct((R_ALL, W_BRANCH), F32)),
        compiler_params=_cparams(("arbitrary",)),
        name="diff_prep",
    )(p, p, gain2(lp['g_diff_qn']), gain2(lp['g_diff_kn']), jnp.asarray(perm, BF16), *tabs)


def _diff_attn_kernel(*refs, has_cache, lambda_init):
    if has_cache:
        dl_ref, q_ref, k_ref, v_ref, kc_ref, vc_ref, g_ref, o_ref = refs
    else:
        dl_ref, q_ref, k_ref, v_ref, g_ref, o_ref = refs
    lane = lax.broadcasted_iota(jnp.int32, (1, LANES), 1)
    dl = dl_ref[...]
    lam = (jnp.exp(jnp.sum(dl[0:1] * dl[1:2], axis=-1, keepdims=True))
           - jnp.exp(jnp.sum(dl[2:3] * dl[3:4], axis=-1, keepdims=True)) + lambda_init)
    units = [(h, w) for h in range(H_D) for w in range(2)]

    def scores(unit):
        h, w = unit
        sl = slice((2 * w + h // 2) * LANES, (2 * w + h // 2 + 1) * LANES)
        q = jnp.where((lane // DH_D) == (h % 2), q_ref[:, sl], jnp.zeros((), q_ref.dtype))
        s_o = lax.dot_general(q, k_ref[:, sl], _NT, preferred_element_type=F32)
        s_c = (lax.dot_general(q, kc_ref[0, 0, :, sl].astype(BF16), _NT, preferred_element_type=F32)
               if has_cache else None)
        return s_o, s_c

    outs = []
    nxt = scores(units[0])
    for n, (h, w) in enumerate(units):
        s_o, s_c = nxt
        if n + 1 < len(units):
            nxt = scores(units[n + 1])
        vs = slice(h * LANES, (h + 1) * LANES)
        m = jnp.max(s_o, axis=-1, keepdims=True)
        if has_cache:
            m = jnp.maximum(m, jnp.max(s_c, axis=-1, keepdims=True))
        e_o = jnp.exp(s_o - m)
        l = jnp.sum(e_o, axis=-1, keepdims=True)
        pv = jnp.dot(e_o.astype(BF16), v_ref[:, vs], preferred_element_type=F32)
        if has_cache:
            e_c = jnp.exp(s_c - m)
            l = l + jnp.sum(e_c, axis=-1, keepdims=True)
            pv = pv + jnp.dot(e_c.astype(BF16), vc_ref[0, 0, :, vs].astype(BF16), preferred_element_type=F32)
        outs.append(pv * (1.0 / l))
    for h in range(H_D):
        y = outs[2 * h] - lam * outs[2 * h + 1]
        r = lax.rsqrt(jnp.mean(y * y, axis=-1, keepdims=True) + EPS)
        o_ref[:, h * LANES:(h + 1) * LANES] = ((y * r * g_ref[...]) * (1.0 - lambda_init)).astype(o_ref.dtype)


def diff_attention(dl, qd, kd, p, kc, vc, g_sub, *, row0, n_seq, t_seq, tq, lambda_init, layer=0):
    has_cache = kc is not None
    nq = t_seq // tq
    qb0, kb0 = row0 // tq, row0 // t_seq
    in_specs = [pl.BlockSpec((4, DH_D), lambda s, i: (0, 0)),
                pl.BlockSpec((tq, W_BRANCH), lambda s, i: (qb0 + s * nq + i, 0)),
                pl.BlockSpec((t_seq, W_BRANCH), lambda s, i: (kb0 + s, 0)),
                pl.BlockSpec((t_seq, W_BRANCH), lambda s, i: (kb0 + s, _seg_block('dv')))]
    args = [dl, qd, kd, p]
    if has_cache:
        in_specs += [pl.BlockSpec((1, 1, PAST_LEN, W_BRANCH), lambda s, i: (s, layer, 0, 0)),
                     pl.BlockSpec((1, 1, PAST_LEN, W_BRANCH), lambda s, i: (s, layer, 0, 0))]
        args += [kc, vc]
    in_specs.append(pl.BlockSpec((1, LANES), lambda s, i: (0, 0)))
    args.append(g_sub.reshape(1, LANES))
    return pl.pallas_call(
        functools.partial(_diff_attn_kernel, has_cache=has_cache, lambda_init=lambda_init),
        grid=(n_seq, nq),
        in_specs=in_specs,
        out_specs=pl.BlockSpec((tq, W_BRANCH), lambda s, i: (s * nq + i, 0)),
        out_shape=jax.ShapeDtypeStruct((n_seq * t_seq, W_BRANCH), BF16),
        compiler_params=_cparams(("arbitrary", "arbitrary")),
        name="diff_attention",
    )(*args)


def _softplus(z):
    return jnp.maximum(z, 0.0) + jnp.log(1.0 + jnp.exp(-jnp.abs(z)))


def _gelu_tanh(x):
    return 0.5 * x * (1.0 + jnp.tanh(math.sqrt(2.0 / math.pi) * (x + 0.044715 * (x * x * x))))


def _rglru_kernel(x_ref, gate_ref, wc_ref, bc_ref, wg_ref, bg_ref, lam_ref, h0_ref, y_ref, st_ref,
                  af_s, uf_s, ab_s, ub_s, *, t_seq):
    t = t_seq
    x = x_ref[...].astype(F32)
    row = lax.broadcasted_iota(jnp.int32, (t, W_B), 0)
    wc = wc_ref[...]
    xc = (wc[0:1] * jnp.where(row >= 2, pltpu.roll(x, 2, 0), 0.0)
          + wc[1:2] * jnp.where(row >= 1, pltpu.roll(x, 1, 0), 0.0)
          + wc[2:3] * x
          + wc[3:4] * jnp.where(row < t - 1, pltpu.roll(x, t - 1, 0), 0.0)
          + bc_ref[...])
    gates = jnp.dot(xc.astype(BF16), wg_ref[...], preferred_element_type=F32) + bg_ref[...]
    for d, (a_s, u_s) in enumerate(((af_s, uf_s), (ab_s, ub_s))):
        rg = jax.nn.sigmoid(gates[:, (2 * d) * W_B:(2 * d + 1) * W_B])
        ig = jax.nn.sigmoid(gates[:, (2 * d + 1) * W_B:(2 * d + 2) * W_B])
        log_a = -RG_C * rg * _softplus(-lam_ref[d:d + 1, :])
        a = jnp.exp(log_a)
        a_s[...] = a
        u_s[...] = jnp.sqrt(-jnp.tanh(log_a) * (a * a + 1.0)) * (ig * xc)

    nblk = t // SUBLANES

    def body(k, carry):
        hf, hb = carry
        base_f = pl.multiple_of(k * SUBLANES, SUBLANES)
        base_b = pl.multiple_of((nblk - 1 - k) * SUBLANES, SUBLANES)
        for r in range(SUBLANES):
            rf = pl.ds(base_f + r, 1)
            rb = pl.ds(base_b + (SUBLANES - 1 - r), 1)
            hf = af_s[rf, :] * hf + uf_s[rf, :]
            hb = ab_s[rb, :] * hb + ub_s[rb, :]
            uf_s[rf, :] = hf
            ub_s[rb, :] = hb
        return hf, hb

    hf, hb = lax.fori_loop(0, nblk, body, (h0_ref[0, 0:1, :], h0_ref[0, 1:2, :]))
    st_ref[0, 0:1, :] = hf
    st_ref[0, 1:2, :] = hb
    y_ref[...] = (_gelu_tanh(gate_ref[...].astype(F32)) * (uf_s[...] + ub_s[...])).astype(y_ref.dtype)


def pack_rglru_weights(lp):
    def blockdiag(w):
        eye = jnp.eye(NB_B, dtype=w.dtype)
        return jnp.einsum('ncd,nm->ncmd', w, eye).reshape(W_B, W_B)
    wg = jnp.concatenate([blockdiag(lp['w_rg_a'][0]), blockdiag(lp['w_rg_x'][0]),
                          blockdiag(lp['w_rg_a'][1]), blockdiag(lp['w_rg_x'][1])], axis=1).astype(BF16)
    bg = jnp.concatenate([lp['b_rg_a'][0], lp['b_rg_x'][0], lp['b_rg_a'][1], lp['b_rg_x'][1]]).reshape(1, 4 * W_B)
    return dict(wg=wg, bg=bg, wc=lp['w_conv_rg'], bc=lp['b_conv_rg'].reshape(1, W_B), lam=lp['rg_lambda'])


def rglru(p, rw, h0, *, row0, n_seq, t_seq):
    rb0 = row0 // t_seq
    xb, gb = _seg_block('rg_x'), _seg_block('rg_gate')
    full = lambda shape: pl.BlockSpec(shape, lambda s: tuple(0 for _ in shape))
    in_specs = [pl.BlockSpec((t_seq, W_B), lambda s: (rb0 + s, xb)),
                pl.BlockSpec((t_seq, W_B), lambda s: (rb0 + s, gb)),
                full((CONV_W, W_B)), full((1, W_B)), full((W_B, 4 * W_B)), full((1, 4 * W_B)), full((2, W_B)),
                pl.BlockSpec((1, 2, W_B), lambda s: (s, 0, 0))]
    args = [p, p, rw['wc'], rw['bc'], rw['wg'], rw['bg'], rw['lam'], h0]
    return pl.pallas_call(
        functools.partial(_rglru_kernel, t_seq=t_seq),
        grid=(n_seq,),
        in_specs=in_specs,
        out_specs=(pl.BlockSpec((t_seq, W_B), lambda s: (s, 0)), pl.BlockSpec((1, 2, W_B), lambda s: (s, 0, 0))),
        out_shape=(jax.ShapeDtypeStruct((n_seq * t_seq, W_B), BF16), jax.ShapeDtypeStruct((n_seq, 2, W_B), F32)),
        scratch_shapes=[pltpu.VMEM((t_seq, W_B), F32)] * 4,
        compiler_params=_cparams(("arbitrary",)),
        name="rglru",
    )(*args)


def _dot_split(a, b_bf16):
    hi = a.astype(BF16)
    lo = (a - hi.astype(F32)).astype(BF16)
    return (jnp.dot(hi, b_bf16, preferred_element_type=F32) + jnp.dot(lo, b_bf16, preferred_element_type=F32))


def _log_sigmoid(z):
    return jnp.minimum(z, 0.0) - jnp.log(1.0 + jnp.exp(-jnp.abs(z)))


_TN = (((0,), (0,)), ((), ()))


def _mlstm_kernel(q_ref, k_ref, v_ref, o_ref, sm_ref, bias_ref, g_ref, c0_ref, n0_ref, m0_ref,
                  y_ref, c_out, n_out, m_out, hm_s, c_s, *, t_seq):
    L = ML_CHUNK
    nchunk = t_seq // L
    scale = DH_C ** -0.5
    ri = lax.broadcasted_iota(jnp.int32, (L, L), 0)
    ci = lax.broadcasted_iota(jnp.int32, (L, L), 1)
    lane1 = lax.broadcasted_iota(jnp.int32, (L, LANES), 1)
    ones_col = jnp.where(lane1 == 0, 1.0, 0.0).astype(BF16)
    bias = bias_ref[...]

    for d in range(2):
        causal = (ci <= ri) if d == 0 else (ci >= ri)
        tri = jnp.where(causal, 1.0, 0.0).astype(BF16)
        tri_t = jnp.where((ri <= ci) if d == 0 else (ri >= ci), 1.0, 0.0).astype(BF16)
        for h in range(H_C):
            c_s[h, :, 0:DH_C] = c0_ref[0, 0, d, h]
            c_s[h, :, DH_C:2 * DH_C] = jnp.where(lane1 == 0, n0_ref[0, 0, d, h], 0.0)
        m_init = tuple(m0_ref[0, :, d * H_C + h:d * H_C + h + 1] for h in range(H_C))

        def chunk(kk, ms, d=d, causal=causal, tri=tri, tri_t=tri_t):
            cidx = kk if d == 0 else nchunk - 1 - kk
            rows = pl.ds(pl.multiple_of(cidx * L, L), L)
            gsm = sm_ref[rows, :].astype(F32) + bias
            lf_all = _log_sigmoid(gsm)
            cum_cols = _dot_split_left(tri, lf_all)
            g_t = gsm.T
            cum_rows = _dot_split(lf_all.T, tri_t)
            heads = range(H_C)
            sls = [slice(h * DH_C, (h + 1) * DH_C) for h in heads]
            qs = [q_ref[rows, sl] for sl in sls]
            ks = [k_ref[rows, sl] for sl in sls]
            v_augs = [jnp.concatenate([v_ref[rows, sl], ones_col], axis=1) for sl in sls]
            qk = [lax.dot_general(qs[h], ks[h], _NT, preferred_element_type=F32) for h in heads]
            qc = [jnp.dot(qs[h], c_s[h].astype(BF16), preferred_element_type=F32) * scale for h in heads]
            cum_c = [cum_cols[:, SM_MF + d * H_C + h:SM_MF + d * H_C + h + 1] for h in heads]
            li_c = [gsm[:, SM_MI + d * H_C + h:SM_MI + d * H_C + h + 1] for h in heads]
            m_row, s = [], []
            for h in heads:
                jl, jf = SM_MI + d * H_C + h, SM_MF + d * H_C + h
                log_d = jnp.where(causal, cum_c[h] - cum_rows[jf:jf + 1, :] + g_t[jl:jl + 1, :], -jnp.inf)
                m_row.append(jnp.maximum(cum_c[h] + ms[h], jnp.max(log_d, axis=-1, keepdims=True)))
                s.append(qk[h] * (scale * jnp.exp(log_d - m_row[h])))
            sv = [jnp.dot(s[h].astype(BF16), v_augs[h], preferred_element_type=F32) for h in heads]
            new_ms = []
            for h in heads:
                w_inter = jnp.exp(cum_c[h] + ms[h] - m_row[h])
                nd = sv[h] + qc[h] * w_inter
                den = jnp.maximum(jnp.abs(nd[:, DH_C:DH_C + 1]), jnp.exp(-m_row[h]))
                h_out = nd[:, :DH_C] * (1.0 / den)
                if d == 0:
                    hm_s[rows, sls[h]] = h_out
                else:
                    hm_s[rows, sls[h]] = hm_s[rows, sls[h]] + h_out
                last = cum_c[h][L - 1:L, :] if d == 0 else cum_c[h][0:1, :]
                w_s = last - cum_c[h] + li_c[h]
                m_new = jnp.maximum(last + ms[h], jnp.max(w_s, axis=0, keepdims=True))
                decay = jnp.exp(last + ms[h] - m_new)
                kw_t = (ks[h].astype(F32) * jnp.exp(w_s - m_new)).T.astype(BF16)
                c_s[h] = decay * c_s[h] + jnp.dot(kw_t, v_augs[h], preferred_element_type=F32)
                new_ms.append(m_new)
            return tuple(new_ms)

        m_fin = lax.fori_loop(0, nchunk, chunk, m_init)
        for h in range(H_C):
            c_out[0, d, h] = c_s[h, :, 0:DH_C]
            n_out[0, d, h] = c_s[h, :, DH_C:DH_C + 1]
            m_out[0, :, d * H_C + h:d * H_C + h + 1] = m_fin[h]

    for h in range(H_C):
        sl = slice(h * DH_C, (h + 1) * DH_C)
        hm = hm_s[:, sl]
        r = lax.rsqrt(jnp.mean(hm * hm, axis=-1, keepdims=True) + EPS)
        y_ref[:, sl] = (jax.nn.sigmoid(o_ref[:, sl].astype(F32)) * (hm * r * g_ref[...])).astype(y_ref.dtype)


def _dot_split_left(a_bf16, b):
    hi = b.astype(BF16)
    lo = (b - hi.astype(F32)).astype(BF16)
    return (jnp.dot(a_bf16, hi, preferred_element_type=F32) + jnp.dot(a_bf16, lo, preferred_element_type=F32))


def mlstm(p, bias_sm, g_out, c0, n0, m0, *, row0, n_seq, t_seq, layer=0):
    rb0 = row0 // t_seq
    seg = lambda nm: pl.BlockSpec((t_seq, W_BRANCH), lambda s, b=_seg_block(nm): (rb0 + s, b))
    full = lambda shape: pl.BlockSpec(shape, lambda s: tuple(0 for _ in shape))
    c_spec = pl.BlockSpec((1, 2, H_C, DH_C, DH_C), lambda s: (s, 0, 0, 0, 0))
    n_spec = pl.BlockSpec((1, 2, H_C, DH_C, 1), lambda s: (s, 0, 0, 0, 0))
    c_in = pl.BlockSpec((1, 1, 2, H_C, DH_C, DH_C), lambda s: (s, layer, 0, 0, 0, 0))
    n_in = pl.BlockSpec((1, 1, 2, H_C, DH_C, 1), lambda s: (s, layer, 0, 0, 0, 0))
    m_spec = pl.BlockSpec((1, 1, 2 * H_C), lambda s: (s, 0, 0))
    in_specs = [seg('mq'), seg('mk'), seg('mv'), seg('mo'),
                pl.BlockSpec((t_seq, LANES), lambda s: (rb0 + s, P_SMALL // LANES)),
                full((1, LANES)), full((1, DH_C)), c_in, n_in, m_spec]
    args = [p, p, p, p, p, bias_sm, g_out.reshape(1, DH_C), c0, n0, m0]
    return pl.pallas_call(
        functools.partial(_mlstm_kernel, t_seq=t_seq),
        grid=(n_seq,),
        in_specs=in_specs,
        out_specs=(pl.BlockSpec((t_seq, W_BRANCH), lambda s: (s, 0)), c_spec, n_spec, m_spec),
        out_shape=(jax.ShapeDtypeStruct((n_seq * t_seq, W_BRANCH), BF16),
                   jax.ShapeDtypeStruct((n_seq, 2, H_C, DH_C, DH_C), F32),
                   jax.ShapeDtypeStruct((n_seq, 2, H_C, DH_C, 1), F32),
                   jax.ShapeDtypeStruct((n_seq, 1, 2 * H_C), F32)),
        scratch_shapes=[pltpu.VMEM((t_seq, W_BRANCH), F32), pltpu.VMEM((H_C, DH_C, 2 * DH_C), F32)],
        compiler_params=_cparams(("arbitrary",)),
        name="mlstm",
    )(*args)


MERGE_TM = 512


def _merge_kernel(*refs):
    ctx_refs, lat_refs, gate_refs = refs[0:4], refs[4:8], refs[8:12]
    xc_ref, xl_ref, mod_ref, wbr_ref, wout_ref, gn_ref, xo_ref, xn_ref = refs[12:20]
    xn3_ref = refs[20] if len(refs) > 20 else None
    is_ctx = pl.program_id(0) < R_CTX // MERGE_TM
    merged = None
    for g in range(N_BRANCH):
        yg = jnp.where(is_ctx, ctx_refs[g][...], lat_refs[g][...])
        pg = jnp.dot(yg, wbr_ref[g], preferred_element_type=F32)
        term = jax.nn.sigmoid(gate_refs[g][...].astype(F32)) * pg
        merged = term if merged is None else merged + term
    y = jnp.dot(merged.astype(BF16), wout_ref[...], preferred_element_type=F32)
    x = jnp.where(is_ctx, xc_ref[...], xl_ref[...]) + mod_ref[0, 2:3, :] * y
    xo_ref[...] = x
    r = lax.rsqrt(jnp.mean(x * x, axis=-1, keepdims=True) + EPS)
    xn = (x * r * gn_ref[...]) * (1.0 + mod_ref[0, 4:5, :]) + mod_ref[0, 3:4, :]
    xn_ref[...] = xn.astype(xn_ref.dtype)
    if xn3_ref is not None:
        _rows_to_tiles(xn3_ref, xn.astype(BF16))


def merge(ys_ctx, ys_lat, p, x_pair, mod, wbr, wout, g_ffn, rows_as_tiles):
    tm = MERGE_TM
    n_ctx_tiles = R_CTX // tm
    br_ctx = pl.BlockSpec((tm, W_BRANCH), lambda i: (jnp.minimum(i, n_ctx_tiles - 1), 0))
    br_lat = pl.BlockSpec((tm, W_BRANCH), lambda i: (jnp.maximum(i - n_ctx_tiles, 0), 0))
    gate = lambda g: pl.BlockSpec((tm, D_MODEL), lambda i, g=g: (i, g))
    row = pl.BlockSpec((tm, D_MODEL), lambda i: (i, 0))
    out_specs = [row, row]
    out_shape = [jax.ShapeDtypeStruct((R_ALL, D_MODEL), F32), jax.ShapeDtypeStruct((R_ALL, D_MODEL), BF16)]
    if rows_as_tiles:
        out_specs.append(pl.BlockSpec((tm, ROW_SUB, LANES), lambda i: (i, 0, 0)))
        out_shape.append(jax.ShapeDtypeStruct((R_ALL, ROW_SUB, LANES), F32))
    return pl.pallas_call(
        _merge_kernel,
        grid=(R_ALL // tm,),
        in_specs=[br_ctx] * N_BRANCH + [br_lat] * N_BRANCH + [gate(0), gate(1), gate(2), gate(3),
                  *_ctx_lat_specs(tm, D_MODEL, 1),
                  pl.BlockSpec((1, 8, D_MODEL), lambda i: (_mod_row_of_tile(i, tm), 0, 0)),
                  pl.BlockSpec((N_BRANCH, W_BRANCH, D_MODEL), lambda i: (0, 0, 0)),
                  pl.BlockSpec((D_MODEL, D_MODEL), lambda i: (0, 0)),
                  pl.BlockSpec((1, D_MODEL), lambda i: (0, 0))],
        out_specs=tuple(out_specs),
        out_shape=tuple(out_shape),
        compiler_params=_cparams(("arbitrary",)),
        name="merge",
    )(*ys_ctx, *ys_lat, p, p, p, p, *x_pair, mod, wbr, wout, g_ffn.reshape(1, D_MODEL))


def _new_expert(te_ref, i):
    return jnp.logical_or(i == 0, te_ref[i] != te_ref[jnp.maximum(i - 1, 0)])


def _ffn_up_kernel(te_ref, nt_ref, x_ref, wg_ref, wu_ref, h_ref, wgb_ref, wub_ref):
    i = pl.program_id(1)

    @pl.when(_new_expert(te_ref, i))
    def _():
        wgb_ref[...] = wg_ref[0].astype(BF16)
        wub_ref[...] = wu_ref[0].astype(BF16)

    @pl.when(i < nt_ref[0])
    def _():
        x = x_ref[...]
        g = jnp.dot(x, wgb_ref[...], preferred_element_type=F32)
        u = jnp.dot(x, wub_ref[...], preferred_element_type=F32)
        h_ref[...] = (g * jax.nn.sigmoid(g) * u).astype(h_ref.dtype)

    @pl.when(i >= nt_ref[0])
    def _():
        h_ref[...] = jnp.zeros(h_ref.shape, h_ref.dtype)


def ffn_up(tile_expert, n_tiles, xs, wg, wu, tm, tf, weight_buffers):
    r, d = xs.shape
    f = wg.shape[2]
    w_spec = pl.BlockSpec((1, d, tf), lambda j, i, te, nt: (te[i], 0, j), pipeline_mode=pl.Buffered(weight_buffers))
    return pl.pallas_call(
        _ffn_up_kernel,
        grid_spec=pltpu.PrefetchScalarGridSpec(
            num_scalar_prefetch=2,
            grid=(f // tf, r // tm),
            in_specs=[pl.BlockSpec((tm, d), lambda j, i, te, nt: (i, 0)), w_spec, w_spec],
            out_specs=pl.BlockSpec((tm, tf), lambda j, i, te, nt: (i, j)),
            scratch_shapes=[pltpu.VMEM((d, tf), BF16), pltpu.VMEM((d, tf), BF16)]),
        out_shape=jax.ShapeDtypeStruct((r, f), BF16),
        compiler_params=_cparams(("arbitrary", "arbitrary")),
        name="ffn_up",
    )(tile_expert, n_tiles, xs, wg, wu)


def _ffn_down_kernel(te_ref, nt_ref, h_ref, wd_ref, y_ref, wdb_ref):
    i = pl.program_id(0)

    @pl.when(_new_expert(te_ref, i))
    def _():
        wdb_ref[...] = wd_ref[0].astype(BF16)

    @pl.when(i < nt_ref[0])
    def _():
        _rows_to_tiles(y_ref, jnp.dot(h_ref[...], wdb_ref[...], preferred_element_type=F32))

    @pl.when(i >= nt_ref[0])
    def _():
        y_ref[...] = jnp.zeros(y_ref.shape, y_ref.dtype)


def ffn_down(tile_expert, n_tiles, h, wd, tm):
    r, f = h.shape
    d = wd.shape[2]
    return pl.pallas_call(
        _ffn_down_kernel,
        grid_spec=pltpu.PrefetchScalarGridSpec(
            num_scalar_prefetch=2,
            grid=(r // tm,),
            in_specs=[pl.BlockSpec((tm, f), lambda i, te, nt: (i, 0)),
                      pl.BlockSpec((1, f, d), lambda i, te, nt: (te[i], 0, 0))],
            out_specs=pl.BlockSpec((tm, ROW_SUB, LANES), lambda i, te, nt: (i, 0, 0)),
            scratch_shapes=[pltpu.VMEM((f, d), BF16)]),
        out_shape=jax.ShapeDtypeStruct((r, ROW_SUB, LANES), F32),
        compiler_params=_cparams(("arbitrary",)),
        name="ffn_down",
    )(tile_expert, n_tiles, h, wd)


def _ffn_down_res_kernel(h_ref, wd_ref, x_ref, mod_ref, yc_ref, yl_ref, wdb_ref):
    i = pl.program_id(0)

    @pl.when(i == 0)
    def _():
        wdb_ref[...] = wd_ref[...].astype(BF16)

    y = jnp.dot(h_ref[...], wdb_ref[...], preferred_element_type=F32)
    res = x_ref[...] + mod_ref[0, 5:6, :] * y
    n_ctx = R_CTX // h_ref.shape[0]

    @pl.when(i < n_ctx)
    def _():
        yc_ref[...] = res

    @pl.when(i >= n_ctx)
    def _():
        yl_ref[...] = res


def ffn_down_residual(h, wd, x, mod):
    tm = 1024
    r, f = h.shape
    d = wd.shape[1]
    return pl.pallas_call(
        _ffn_down_res_kernel,
        grid=(r // tm,),
        in_specs=[pl.BlockSpec((tm, f), lambda i: (i, 0)),
                  pl.BlockSpec((f, d), lambda i: (0, 0), pipeline_mode=pl.Buffered(1)),
                  pl.BlockSpec((tm, d), lambda i: (i, 0)),
                  pl.BlockSpec((1, 8, d), lambda i: (_mod_row_of_tile(i, tm), 0, 0))],
        out_specs=_ctx_lat_specs(tm, d, 1),
        out_shape=(jax.ShapeDtypeStruct((R_CTX, d), F32), jax.ShapeDtypeStruct((r - R_CTX, d), F32)),
        scratch_shapes=[pltpu.VMEM((f, d), BF16)],
        compiler_params=_cparams(("arbitrary",)),
        name="ffn_down_residual",
    )(h, wd, x, mod)


def dense_swiglu_residual(xn, x, mod, wg, wu, wd):
    t = xn.shape[0]
    tm = 1024
    n_tiles = t // tm
    te = jnp.zeros((n_tiles,), jnp.int32)
    nt = jnp.full((1,), n_tiles, jnp.int32)
    h = ffn_up(te, nt, xn, wg[None], wu[None], tm=tm, tf=wg.shape[1] // 2, weight_buffers=1)
    return ffn_down_residual(h, wd, x, mod)


MOE_TM = 512
DISPATCH_TM = 512


def _moe_routing(logits, tm):
    t = logits.shape[0]
    n_assign = t * TOP_K
    top_v, top_i = lax.top_k(logits, TOP_K)
    gate = jax.nn.softmax(top_v, axis=-1)
    flat_e = top_i.reshape(-1).astype(jnp.int32)
    onehot = (flat_e[:, None] == jnp.arange(N_EXP, dtype=jnp.int32)[None, :])
    blk = LANES
    oh = onehot.astype(F32).reshape(n_assign // blk, blk, N_EXP)
    tril = jnp.tril(jnp.ones((blk, blk), F32))
    within = jnp.einsum('ij,bjk->bik', tril, oh)
    blk_tot = within[:, -1, :]
    blk_off = jnp.cumsum(blk_tot, axis=0) - blk_tot
    csum = (within + blk_off[:, None, :]).reshape(n_assign, N_EXP)
    rank = jnp.sum(jnp.where(onehot, csum - 1.0, 0.0), axis=1).astype(jnp.int32)
    counts = csum[-1].astype(jnp.int32)
    padded = (counts + tm - 1) // tm * tm
    grp_start = jnp.cumsum(padded) - padded
    raw_start = jnp.cumsum(counts) - counts
    slot_of_assign = jnp.sum(jnp.where(onehot, grp_start[None, :], 0), axis=1) + rank

    r_max = n_assign + N_EXP * tm
    tile_start = jnp.arange(r_max // tm, dtype=jnp.int32) * tm
    tile_expert = jnp.sum((tile_start[:, None] >= (grp_start + padded)[None, :]).astype(jnp.int32), axis=1)
    tile_expert = jnp.minimum(tile_expert, N_EXP - 1).astype(jnp.int32)
    n_tiles = (jnp.sum(padded) // tm).astype(jnp.int32).reshape(1)

    order = jnp.argsort(flat_e, stable=True).astype(jnp.int32)
    e_slot = jnp.repeat(tile_expert, tm)
    j = jnp.arange(r_max, dtype=jnp.int32) - grp_start[e_slot]
    src = jnp.clip(raw_start[e_slot] + j, 0, n_assign - 1)
    tok_of_slot = jnp.where(j < counts[e_slot], order[src] // TOP_K, 0)
    return gate, slot_of_assign, tok_of_slot, tile_expert, n_tiles


ROW_SUB = D_MODEL // LANES


def _rows_to_tiles(o3_ref, x):
    for j in range(ROW_SUB):
        o3_ref[:, j, :] = x[:, j * LANES:(j + 1) * LANES].astype(o3_ref.dtype)


def _start_row_gather(idx_ref, src_ref, dst, sem, n_rows):
    def body(q, carry):
        for u in range(2):
            r = 2 * q + u
            pltpu.make_async_copy(src_ref.at[idx_ref[0, 0, r]], dst.at[r], sem).start(priority=u)
        return carry

    lax.fori_loop(0, n_rows // 2, body, 0, unroll=4)


def _wait_row_gather(src_ref, dst, sem, n_rows):
    pltpu.make_async_copy(src_ref.at[pl.ds(0, n_rows)], dst, sem).wait()


def _tiles_to_rows(tiles, rows_ref):
    for j in range(ROW_SUB):
        rows_ref[:, j * LANES:(j + 1) * LANES] = tiles[:, j, :]


def _dispatch_kernel(nt_ref, idx_ref, idx_next_ref, src_ref, o_ref, buf, rows, sem):
    i = pl.program_id(0)
    nt = nt_ref[0]
    tm = o_ref.shape[0]
    slot = i % 2

    @pl.when(i == 0)
    def _():
        _start_row_gather(idx_ref, src_ref, buf.at[0], sem.at[0], tm)

    @pl.when(i + 1 < nt)
    def _():
        _start_row_gather(idx_next_ref, src_ref, buf.at[1 - slot], sem.at[1 - slot], tm)

    @pl.when(i < nt)
    def _():
        _wait_row_gather(src_ref, buf.at[slot], sem.at[slot], tm)
        _tiles_to_rows(buf.at[slot], rows)
        o_ref[...] = rows[...].astype(o_ref.dtype)

    @pl.when(i >= nt)
    def _():
        o_ref[...] = jnp.zeros(o_ref.shape, o_ref.dtype)


def moe_dispatch(n_tiles, tok_of_slot, xn3, tm):
    r = tok_of_slot.shape[0]
    last = r // tm - 1
    idx = tok_of_slot.reshape(r // tm, 1, tm)
    return pl.pallas_call(
        _dispatch_kernel,
        grid_spec=pltpu.PrefetchScalarGridSpec(
            num_scalar_prefetch=1,
            grid=(r // tm,),
            in_specs=[pl.BlockSpec((1, 1, tm), lambda i, nt: (i, 0, 0), memory_space=pltpu.SMEM),
                      pl.BlockSpec((1, 1, tm), lambda i, nt: (jnp.minimum(i + 1, last), 0, 0),
                                   memory_space=pltpu.SMEM),
                      pl.BlockSpec(memory_space=pl.ANY)],
            out_specs=pl.BlockSpec((tm, D_MODEL), lambda i, nt: (i, 0)),
            scratch_shapes=[pltpu.VMEM((2, tm, ROW_SUB, LANES), F32), pltpu.VMEM((tm, D_MODEL), F32),
                            pltpu.SemaphoreType.DMA((2,))]),
        out_shape=jax.ShapeDtypeStruct((r, D_MODEL), BF16),
        compiler_params=_cparams(("arbitrary",)),
        name="moe_dispatch",
    )(n_tiles, idx, idx, xn3)


COMBINE_TM = 512


def _combine_kernel(idx_ref, idx_next_ref, ys_ref, x_ref, gate_ref, mod_ref, oc_ref, ol_ref, buf, rows, sem):
    i = pl.program_id(0)
    tm = oc_ref.shape[0]
    n_rows = TOP_K * tm
    slot = i % 2

    @pl.when(i == 0)
    def _():
        _start_row_gather(idx_ref, ys_ref, buf.at[0], sem.at[0], n_rows)

    @pl.when(i + 1 < pl.num_programs(0))
    def _():
        _start_row_gather(idx_next_ref, ys_ref, buf.at[1 - slot], sem.at[1 - slot], n_rows)

    _wait_row_gather(ys_ref, buf.at[slot], sem.at[slot], n_rows)
    _tiles_to_rows(buf.at[slot], rows)
    f = gate_ref[:, 0:1] * rows[0:tm, :] + gate_ref[:, 1:2] * rows[tm:2 * tm, :]
    res = x_ref[...] + mod_ref[0, 5:6, :] * f

    @pl.when(i < R_CTX // COMBINE_TM)
    def _():
        oc_ref[...] = res

    @pl.when(i >= R_CTX // COMBINE_TM)
    def _():
        ol_ref[...] = res


def moe_combine(slot_of_assign, ys3, x, gate, mod):
    t, d = x.shape
    tm = COMBINE_TM
    idx = slot_of_assign.reshape(t // tm, tm, TOP_K).transpose(0, 2, 1).reshape(t // tm, 1, TOP_K * tm)
    last = t // tm - 1
    n_ctx = R_CTX // tm
    return pl.pallas_call(
        _combine_kernel,
        grid=(t // tm,),
        in_specs=[pl.BlockSpec((1, 1, TOP_K * tm), lambda i: (i, 0, 0), memory_space=pltpu.SMEM),
                  pl.BlockSpec((1, 1, TOP_K * tm), lambda i: (jnp.minimum(i + 1, last), 0, 0),
                               memory_space=pltpu.SMEM),
                  pl.BlockSpec(memory_space=pl.ANY),
                  pl.BlockSpec((tm, d), lambda i: (i, 0)),
                  pl.BlockSpec((tm, TOP_K), lambda i: (i, 0)),
                  pl.BlockSpec((1, 8, d), lambda i: (_mod_row_of_tile(i, tm), 0, 0))],
        out_specs=(pl.BlockSpec((tm, d), lambda i: (jnp.minimum(i, n_ctx - 1), 0)),
                   pl.BlockSpec((tm, d), lambda i: (jnp.maximum(i - n_ctx, 0), 0))),
        out_shape=(jax.ShapeDtypeStruct((R_CTX, d), F32), jax.ShapeDtypeStruct((t - R_CTX, d), F32)),
        scratch_shapes=[pltpu.VMEM((2, TOP_K * tm, ROW_SUB, LANES), F32), pltpu.VMEM((TOP_K * tm, d), F32),
                        pltpu.SemaphoreType.DMA((2,))],
        compiler_params=_cparams(("arbitrary",)),
        name="moe_combine",
    )(idx, idx, ys3, x, gate, mod)


def moe_swiglu_residual(xn3, xn, x, mod, w_router, wg, wu, wd):
    tm = MOE_TM
    logits = jnp.dot(xn.astype(F32), w_router, precision=lax.Precision.HIGHEST)
    gate, slot_of_assign, tok_of_slot, tile_expert, n_tiles = _moe_routing(logits, tm)
    xs = moe_dispatch(n_tiles * (tm // DISPATCH_TM), tok_of_slot, xn3, DISPATCH_TM)
    h = ffn_up(tile_expert, n_tiles, xs, wg, wu, tm=tm, tf=wg.shape[2] // 2, weight_buffers=2)
    ys3 = ffn_down(tile_expert, n_tiles, h, wd, tm=tm)
    return moe_combine(slot_of_assign, ys3, x, gate, mod)


def _layer(x, cond, lp, l, ctx, tabs_a, tabs_d):
    mod = modulation(cond, lp['w_mod'], lp['b_mod'], l).reshape(cond.shape[0], 6, D_MODEL)
    mod = jnp.pad(mod, ((0, 0), (0, 2), (0, 0)))
    p = in_proj(x, mod, lp['g_norm_mix'], lp['w_in_p'], l)

    mw = pack_mla_weights(lp)
    q_a, k_a, v_a, ckv, kr = mla_prep(p, mw, tabs_a)
    kc_a, vc_a = mla_prep_cache(ctx['mla_ckv'], ctx['mla_krope'], mw, l)
    ya_c = mla_attention(q_a, k_a, v_a, None, None, row0=0, n_seq=N_CTX_SEQ, t_seq=T_CTX, tq=T_CTX)
    ya_l = mla_attention(q_a, k_a, v_a, kc_a, vc_a, row0=R_CTX, n_seq=N_LAT_SEQ, t_seq=T_LAT, tq=ATTN_TQ)

    rw = pack_rglru_weights(lp)
    yb_c, st_rg = rglru(p, rw, jnp.zeros((N_CTX_SEQ, 2, W_B), F32), row0=0, n_seq=N_CTX_SEQ, t_seq=T_CTX)
    yb_l, _ = rglru(p, rw, ctx['rglru'], row0=R_CTX, n_seq=N_LAT_SEQ, t_seq=T_LAT)

    bias_sm = jnp.zeros((LANES,), F32).at[SM_MI:SM_MI + 2 * H_C].set(lp['b_ml_i'].reshape(-1))
    bias_sm = bias_sm.at[SM_MF:SM_MF + 2 * H_C].set(lp['b_ml_f'].reshape(-1)).reshape(1, LANES)
    c0_ctx = jnp.zeros((N_CTX_SEQ, 1, 2, H_C, DH_C, DH_C), F32)
    n0_ctx = jnp.zeros((N_CTX_SEQ, 1, 2, H_C, DH_C, 1), F32)
    m0_ctx = jnp.zeros((N_CTX_SEQ, 1, 2 * H_C), F32)
    m0_lat = ctx['mlstm_m'][:, l].reshape(N_LAT_SEQ, 1, 2 * H_C)
    yc_c, c_fin, n_fin, m_fin = mlstm(p, bias_sm, lp['g_ml_out'], c0_ctx, n0_ctx, m0_ctx,
                                      row0=0, n_seq=N_CTX_SEQ, t_seq=T_CTX)
    yc_l, _, _, _ = mlstm(p, bias_sm, lp['g_ml_out'], ctx['mlstm_C'], ctx['mlstm_n'][..., None], m0_lat,
                          row0=R_CTX, n_seq=N_LAT_SEQ, t_seq=T_LAT, layer=l)

    lambda_init = 0.8 - 0.6 * math.exp(-0.3 * l)
    qd, kd_own, kd_plain = diff_prep(p, lp, tabs_d)
    yd_c = diff_attention(lp['diff_lambda'], qd, kd_own, p, None, None, lp['g_diff_sub'],
                          row0=0, n_seq=N_CTX_SEQ, t_seq=T_CTX, tq=T_CTX, lambda_init=lambda_init)
    yd_l = diff_attention(lp['diff_lambda'], qd, kd_own, p, ctx['diff_k'], ctx['diff_v'], lp['g_diff_sub'],
                          row0=R_CTX, n_seq=N_LAT_SEQ, t_seq=T_LAT, tq=ATTN_TQ, lambda_init=lambda_init, layer=l)

    merged = merge((ya_c, yb_c, yc_c, yd_c), (ya_l, yb_l, yc_l, yd_l), p, x, mod, lp['w_br'].astype(BF16),
                   lp['w_out'].astype(BF16), lp['g_norm_ffn'], rows_as_tiles='moe' in lp)
    if 'ffn' in lp:
        x, xn = merged
        x = dense_swiglu_residual(xn, x, mod, *lp['ffn'])
    else:
        x, xn, xn3 = merged
        x = moe_swiglu_residual(xn3, xn, x, mod, *lp['moe'])

    dv0 = _seg_block('dv') * W_BRANCH
    ctx_out = (ckv[:R_CTX].reshape(N_CTX_SEQ, T_CTX, KV_RANK),
               kr[:R_CTX, :ROPE_A].reshape(N_CTX_SEQ, T_CTX, ROPE_A),
               kd_plain[:R_CTX].reshape(N_CTX_SEQ, T_CTX, 2, H_D, DH_D),
               p[:R_CTX, dv0:dv0 + W_BRANCH].astype(F32).reshape(N_CTX_SEQ, T_CTX, H_D, 2 * DH_D),
               st_rg,
               c_fin,
               n_fin.reshape(N_CTX_SEQ, 2, H_C, DH_C),
               m_fin.reshape(N_CTX_SEQ, 2, H_C))
    return x, ctx_out


def kernel(x_prompt, x_sample, cache_mla_ckv, cache_mla_krope, cache_diff_k, cache_diff_v,
           state_rglru, state_mlstm_C, state_mlstm_n, state_mlstm_m, c, c_ctx,
           w_mod, b_mod, g_norm_mix, g_norm_ffn, w_in, g_mla_qlat, w_mla_uq, g_mla_kvlat, w_mla_ukv,
           g_mla_qn, g_mla_kn, w_conv_rg, b_conv_rg, w_rg_a, b_rg_a, w_rg_x, b_rg_x, rg_lambda,
           b_ml_i, b_ml_f, g_ml_out, g_diff_qn, g_diff_kn, diff_lambda, g_diff_sub, w_br, w_out,
           w_ffn_gate, w_ffn_up, w_ffn_down, w_router, w_moe_gate, w_moe_up, w_moe_down):
    assert x_prompt.shape == (N_CTX_SEQ, T_CTX, D_MODEL) and x_sample.shape == (N_LAT_SEQ, T_LAT, D_MODEL)
    tabs_a = _rope_tables(ROPE_A, (NOPE_A,), MLA_TM)
    tabs_d = _rope_tables(DH_D, (0, DH_D), DIFF_TM)
    cond = jnp.concatenate([c_ctx.reshape(1, D_MODEL), c, jnp.zeros((16 - 1 - N_LAT_SEQ, D_MODEL), F32)], axis=0)
    x = (x_prompt.reshape(R_CTX, D_MODEL), x_sample.reshape(R_LAT, D_MODEL))
    w_in_p = pack_w_in(w_in)
    krope_pad = jnp.pad(cache_mla_krope, ((0, 0), (0, 0), (0, 0), (0, LANES - ROPE_A)))
    diff_k_rows = cache_diff_k.reshape(N_LAT_SEQ, DEPTH, PAST_LEN, W_BRANCH)
    diff_v_rows = cache_diff_v.reshape(N_LAT_SEQ, DEPTH, PAST_LEN, W_BRANCH)
    new = []
    for l in range(DEPTH):
        lp = dict(w_mod=w_mod, b_mod=b_mod, g_norm_mix=g_norm_mix[l], g_norm_ffn=g_norm_ffn[l], w_in_p=w_in_p,
                  g_mla_qlat=g_mla_qlat[l], w_mla_uq=w_mla_uq[l], g_mla_kvlat=g_mla_kvlat[l], w_mla_ukv=w_mla_ukv[l],
                  g_mla_qn=g_mla_qn[l], g_mla_kn=g_mla_kn[l], w_conv_rg=w_conv_rg[l], b_conv_rg=b_conv_rg[l],
                  w_rg_a=w_rg_a[l], b_rg_a=b_rg_a[l], w_rg_x=w_rg_x[l], b_rg_x=b_rg_x[l], rg_lambda=rg_lambda[l],
                  b_ml_i=b_ml_i[l], b_ml_f=b_ml_f[l], g_ml_out=g_ml_out[l], g_diff_qn=g_diff_qn[l],
                  g_diff_kn=g_diff_kn[l], diff_lambda=diff_lambda[l], g_diff_sub=g_diff_sub[l],
                  w_br=w_br[l], w_out=w_out[l])
        if l % 2 == 0:
            lp['ffn'] = (w_ffn_gate[l // 2], w_ffn_up[l // 2], w_ffn_down[l // 2])
        else:
            lp['moe'] = (w_router[l // 2], w_moe_gate[l // 2], w_moe_up[l // 2], w_moe_down[l // 2])
        ctx_l = dict(mla_ckv=cache_mla_ckv, mla_krope=krope_pad, diff_k=diff_k_rows, diff_v=diff_v_rows,
                     rglru=state_rglru[:, l], mlstm_C=state_mlstm_C, mlstm_n=state_mlstm_n, mlstm_m=state_mlstm_m)
        x, st = _layer(x, cond, lp, l, ctx_l, tabs_a, tabs_d)
        new.append(st)
    outs = tuple(jnp.stack([s[i] for s in new], axis=1) for i in range(8))
    x_ctx, x_lat = x
    return (x_ctx.reshape(N_CTX_SEQ, T_CTX, D_MODEL), x_lat.reshape(N_LAT_SEQ, T_LAT, D_MODEL)) + outs
```

```python
import functools
import math

import jax
import jax.numpy as jnp
import numpy as np
from jax import lax
from jax.experimental import pallas as pl
from jax.experimental.pallas import tpu as pltpu

D_MODEL = 1024
DEPTH = 2
GRID_W = 64
ROPE_BASE = 10000.0
EPS = 1e-6
N_BRANCH = 4
W_BRANCH = D_MODEL // 2

H_A = 8
NOPE_A = 64
ROPE_A = 32
V_A = W_BRANCH // H_A
Q_RANK = D_MODEL // 4
KV_RANK = D_MODEL // 8
MLA_SCALE = (NOPE_A + ROPE_A) ** -0.5

W_B = W_BRANCH
NB_B = 8
BW_B = W_B // NB_B
CONV_W = 4
RG_C = 8.0

H_C = 4
DH_C = W_BRANCH // H_C
ML_CHUNK = 128

H_D = 4
DH_D = W_BRANCH // (2 * H_D)
DIFF_SCALE = DH_D ** -0.5

N_EXP = 8
TOP_K = 2

V7X_VMEM_LIMIT_BYTES = 56 * 1024 * 1024
LANES = 128
SUBLANES = 8

BF16 = jnp.bfloat16
F32 = jnp.float32

P_GATE = 0
P_SEG = N_BRANCH * D_MODEL
SEG_NAMES = ('rg_x', 'rg_gate', 'mq', 'mk', 'mv', 'mo', 'dq', 'dk', 'dv')
P_QLAT = P_SEG + 9 * W_BRANCH
P_KVLAT = P_QLAT + Q_RANK
P_SMALL = P_KVLAT + KV_RANK
P_WIDTH = P_SMALL + LANES
SM_MI = ROPE_A
SM_MF = ROPE_A + 2 * H_C


def _seg_block(name):
    return (P_SEG + SEG_NAMES.index(name) * W_BRANCH) // W_BRANCH


N_CTX_SEQ, T_CTX = 16, 256
N_LAT_SEQ, T_LAT = 8, 1024
PAST_LEN = 512
R_CTX = N_CTX_SEQ * T_CTX
R_LAT = N_LAT_SEQ * T_LAT
R_ALL = R_CTX + R_LAT


def _cparams(sem):
    return pltpu.CompilerParams(dimension_semantics=sem, vmem_limit_bytes=V7X_VMEM_LIMIT_BYTES)


def _mod_row_of_tile(i, tm):
    n_ctx_tiles = R_CTX // tm
    per_seq = T_LAT // tm
    return jnp.where(i < n_ctx_tiles, 0, 1 + (i - n_ctx_tiles) // per_seq)


def _pos_block_of_tile(i, tm):
    n_ctx_tiles = R_CTX // tm
    per_seq = T_LAT // tm
    return jnp.where(i < n_ctx_tiles, per_seq, (i - n_ctx_tiles) % per_seq)


def _mod_kernel(c_ref, w_ref, b_ref, o_ref):
    c = c_ref[...]
    s = (c * jax.nn.sigmoid(c)).astype(BF16)
    o_ref[...] = jnp.dot(s, w_ref[0].astype(BF16), preferred_element_type=F32) + b_ref[0]


def modulation(cond, w_mod, b_mod, l):
    m, d = cond.shape
    n = w_mod.shape[2]
    tn = 1536
    return pl.pallas_call(
        _mod_kernel,
        grid=(n // tn,),
        in_specs=[pl.BlockSpec((m, d), lambda j: (0, 0)),
                  pl.BlockSpec((1, d, tn), lambda j: (l, 0, j)),
                  pl.BlockSpec((1, 1, tn), lambda j: (l, 0, j))],
        out_specs=pl.BlockSpec((m, tn), lambda j: (0, j)),
        out_shape=jax.ShapeDtypeStruct((m, n), F32),
        compiler_params=_cparams(("arbitrary",)),
        name="modulation",
    )(cond, w_mod, b_mod.reshape(b_mod.shape[0], 1, n))


def _in_proj_kernel(xc_ref, xl_ref, mod_ref, g_ref, w_ref, o_ref, xn_ref):
    @pl.when(pl.program_id(1) == 0)
    def _():
        x = jnp.where(pl.program_id(0) < R_CTX // xc_ref.shape[0], xc_ref[...], xl_ref[...])
        r = lax.rsqrt(jnp.mean(x * x, axis=-1, keepdims=True) + EPS)
        sh = mod_ref[0, 0:1, :]
        sc = mod_ref[0, 1:2, :]
        xn_ref[...] = ((x * r * g_ref[...]) * (1.0 + sc) + sh).astype(BF16)

    o_ref[...] = jnp.dot(xn_ref[...], w_ref[0], preferred_element_type=F32).astype(o_ref.dtype)


def _ctx_lat_specs(tm, width, grid_rank):
    n_ctx = R_CTX // tm
    if grid_rank == 1:
        return (pl.BlockSpec((tm, width), lambda i: (jnp.minimum(i, n_ctx - 1), 0)),
                pl.BlockSpec((tm, width), lambda i: (jnp.maximum(i - n_ctx, 0), 0)))
    return (pl.BlockSpec((tm, width), lambda i, j: (jnp.minimum(i, n_ctx - 1), 0)),
            pl.BlockSpec((tm, width), lambda i, j: (jnp.maximum(i - n_ctx, 0), 0)))


def in_proj(x_pair, mod, g, w_p, l):
    tm, tn = 1024, P_WIDTH // 4
    m, d = R_ALL, D_MODEL
    n = w_p.shape[2]
    return pl.pallas_call(
        _in_proj_kernel,
        grid=(m // tm, n // tn),
        in_specs=[*_ctx_lat_specs(tm, d, 2),
                  pl.BlockSpec((1, 8, d), lambda i, j: (_mod_row_of_tile(i, tm), 0, 0)),
                  pl.BlockSpec((1, d), lambda i, j: (0, 0)),
                  pl.BlockSpec((1, d, tn), lambda i, j: (l, 0, j))],
        out_specs=pl.BlockSpec((tm, tn), lambda i, j: (i, j)),
        out_shape=jax.ShapeDtypeStruct((m, n), BF16),
        scratch_shapes=[pltpu.VMEM((tm, d), BF16)],
        compiler_params=_cparams(("arbitrary", "arbitrary")),
        name="in_proj",
    )(*x_pair, mod, g.reshape(1, d), w_p)


_O_RG = Q_RANK + KV_RANK + ROPE_A
_O_MI = _O_RG + 6 * W_BRANCH
_O_DQ = _O_MI + 4 * H_C
_O_GATE = _O_DQ + 3 * W_BRANCH
_PACK_PIECES = ((_O_GATE, N_BRANCH * D_MODEL, P_GATE), (_O_RG, 6 * W_BRANCH, P_SEG),
                (_O_DQ, 3 * W_BRANCH, P_SEG + 6 * W_BRANCH), (0, Q_RANK + KV_RANK, P_QLAT),
                (Q_RANK + KV_RANK, ROPE_A, P_SMALL), (_O_MI, 4 * H_C, P_SMALL + ROPE_A))
PACK_ROWS = 128


def _pack_w_in_kernel(w_ref, o_ref):
    n_in = w_ref.shape[2]
    o_ref[0, :, P_SMALL:P_WIDTH] = jnp.zeros((PACK_ROWS, LANES), o_ref.dtype)
    for src, width, dst in _PACK_PIECES:
        lo = src // LANES * LANES
        hi = min(_round_up(src + width, LANES), n_in)
        win = w_ref[0, :, lo:hi]
        o_ref[0, :, dst:dst + width] = win[:, src - lo:src - lo + width].astype(o_ref.dtype)


def pack_w_in(w_in):
    depth, d, n_in = w_in.shape
    return pl.pallas_call(
        _pack_w_in_kernel,
        grid=(depth, d // PACK_ROWS),
        in_specs=[pl.BlockSpec((1, PACK_ROWS, n_in), lambda l, i: (l, i, 0))],
        out_specs=pl.BlockSpec((1, PACK_ROWS, P_WIDTH), lambda l, i: (l, i, 0)),
        out_shape=jax.ShapeDtypeStruct((depth, d, P_WIDTH), BF16),
        compiler_params=_cparams(("arbitrary", "arbitrary")),
        name="pack_w_in",
    )(w_in)


def _round_up(x, m):
    return (x + m - 1) // m * m


def _rope_partner(rot_dim, lane_starts):
    nf = rot_dim // 4
    partner = np.arange(LANES)
    for s0 in lane_starts:
        for a in range(2):
            lo = s0 + a * 2 * nf
            partner[lo:lo + nf] = np.arange(lo + nf, lo + 2 * nf)
            partner[lo + nf:lo + 2 * nf] = np.arange(lo, lo + nf)
    return partner


def _rope_tables(rot_dim, lane_starts, tm):
    rows = T_LAT // GRID_W
    r, c = np.meshgrid(np.arange(rows, dtype=np.float32), np.arange(GRID_W, dtype=np.float32), indexing='ij')
    nf = rot_dim // 4
    inv = (np.float32(ROPE_BASE) ** (-np.arange(nf, dtype=np.float32) / np.float32(nf))).astype(np.float32)
    ang = np.stack([r.reshape(-1)[:, None] * inv, c.reshape(-1)[:, None] * inv], axis=1).astype(np.float32)
    cos, sin = np.cos(ang).astype(np.float32), np.sin(ang).astype(np.float32)
    tc = np.ones((T_LAT + tm, LANES), np.float32)
    ta = np.zeros((T_LAT + tm, LANES), np.float32)
    tb = np.zeros((T_LAT + tm, LANES), np.float32)
    for s0 in lane_starts:
        for a in range(2):
            lo = s0 + a * 2 * nf
            tc[:T_LAT, lo:lo + nf] = cos[:, a]
            tc[:T_LAT, lo + nf:lo + 2 * nf] = cos[:, a]
            ta[:T_LAT, lo:lo + nf] = -sin[:, a]
            tb[:T_LAT, lo + nf:lo + 2 * nf] = sin[:, a]
    return jnp.asarray(tc), jnp.asarray(ta + tb)


MLA_TM = 1024
ATTN_TQ = 1024
MLA_HEADS_PER_STEP = 8
QK_A = NOPE_A + ROPE_A


def _mla_prep_kernel(*refs, has_q, norm_ckv):
    if has_q:
        (qlat_ref, gq_ref, wuq_ref, gqn_ref, kv_ref, sm_ref, gkv_ref, wkc_ref, wv_ref, gkn_ref,
         c_ref, s_ref, q_o, k_o, v_o, ckv_o, kr_o) = refs
        c, sn = c_ref[...], s_ref[...]
    else:
        (kv_ref, sm_ref, gkv_ref, wkc_ref, wv_ref, gkn_ref, k_o, v_o) = refs

    def heads(z, g_ref, o_ref, scale):
        for h in range(H_A):
            s = z[:, h * LANES:(h + 1) * LANES]
            r = lax.rsqrt(jnp.sum(s * s, axis=-1, keepdims=True) * (1.0 / QK_A) + EPS)
            y = s * r * g_ref[0:1, :]
            if has_q:
                sw = z[:, (H_A + h) * LANES:(H_A + h + 1) * LANES]
                y = y * c + (sw * r * g_ref[1:2, :]) * sn
            if scale != 1.0:
                y = y * scale
            o_ref[:, h * LANES:(h + 1) * LANES] = y.astype(o_ref.dtype)

    if has_q:
        ql = qlat_ref[...].astype(F32)
        qn = ql * lax.rsqrt(jnp.mean(ql * ql, axis=-1, keepdims=True) + EPS) * gq_ref[...]
        q = jnp.dot(qn.astype(BF16), wuq_ref[...], preferred_element_type=F32)
        heads(q, gqn_ref, q_o, MLA_SCALE)

    if has_q:
        kv, sm = kv_ref[...].astype(F32), sm_ref[...]
    else:
        kv, sm = kv_ref[0, 0].astype(F32), sm_ref[0, 0]
    if norm_ckv:
        ckv = kv * lax.rsqrt(jnp.mean(kv * kv, axis=-1, keepdims=True) + EPS) * gkv_ref[...]
    else:
        ckv = kv
    ckv_b = ckv.astype(BF16)
    kin = jnp.concatenate([ckv_b, sm.astype(BF16)], axis=1)
    wkc = wkc_ref[...] if has_q else wkc_ref[:, :H_A * LANES]
    k = jnp.dot(kin, wkc, preferred_element_type=F32)
    heads(k, gkn_ref, k_o, 1.0)
    v_o[...] = jnp.dot(ckv_b, wv_ref[...], preferred_element_type=F32).astype(v_o.dtype)
    if has_q:
        ckv_o[...] = ckv
        kr_o[...] = sm.astype(F32)


def pack_mla_weights(lp):
    wuq = lp['w_mla_uq'].reshape(Q_RANK, H_A, QK_A)
    wuq_p = jnp.pad(wuq, ((0, 0), (0, 0), (0, LANES - QK_A))).reshape(Q_RANK, H_A * LANES).astype(BF16)
    wukv = lp['w_mla_ukv'].reshape(KV_RANK, H_A, NOPE_A + V_A)
    wk = jnp.pad(wukv[:, :, :NOPE_A], ((0, 0), (0, 0), (0, LANES - NOPE_A))).reshape(KV_RANK, H_A * LANES)
    place = np.zeros((LANES, H_A, LANES), np.float32)
    for h in range(H_A):
        place[np.arange(ROPE_A), h, NOPE_A + np.arange(ROPE_A)] = 1.0
    wkc = jnp.concatenate([wk, jnp.asarray(place.reshape(LANES, H_A * LANES))], axis=0).astype(BF16)
    wv = wukv[:, :, NOPE_A:]
    wv_even = jnp.pad(wv, ((0, 0), (0, 0), (0, LANES - V_A)))
    wv_odd = jnp.pad(wv, ((0, 0), (0, 0), (LANES - V_A, 0)))
    odd = (np.arange(H_A) % 2 == 1)[None, :, None]
    wv_p = jnp.where(odd, wv_odd, wv_even).reshape(KV_RANK, H_A * LANES).astype(BF16)
    partner = _rope_partner(ROPE_A, (NOPE_A,))
    cols = (np.arange(H_A)[:, None] * LANES + partner[None, :]).reshape(-1)
    with_partner = lambda w: jnp.concatenate([w, w[:, cols]], axis=1)
    pad_g = lambda g: jnp.pad(g, (0, LANES - QK_A))
    gain2 = lambda g: jnp.stack([pad_g(g), pad_g(g)[partner]])
    return dict(wuq=with_partner(wuq_p), wkc=with_partner(wkc), wv=wv_p,
                gqn=gain2(lp['g_mla_qn']), gkn=gain2(lp['g_mla_kn']),
                gq=lp['g_mla_qlat'].reshape(1, Q_RANK), gkv=lp['g_mla_kvlat'].reshape(1, KV_RANK))


def mla_prep(p, mw, tabs):
    tm = MLA_TM
    n = R_ALL // tm
    full = lambda shape: pl.BlockSpec(shape, lambda i: (0, 0))
    tab = pl.BlockSpec((tm, LANES), lambda i: (_pos_block_of_tile(i, tm), 0))
    wide = H_A * LANES
    out_shape = (jax.ShapeDtypeStruct((R_ALL, wide), BF16), jax.ShapeDtypeStruct((R_ALL, wide), BF16),
                 jax.ShapeDtypeStruct((R_ALL, wide), BF16), jax.ShapeDtypeStruct((R_ALL, KV_RANK), F32),
                 jax.ShapeDtypeStruct((R_ALL, LANES), F32))
    row = lambda w: pl.BlockSpec((tm, w), lambda i: (i, 0))
    return pl.pallas_call(
        functools.partial(_mla_prep_kernel, has_q=True, norm_ckv=True),
        grid=(n,),
        in_specs=[pl.BlockSpec((tm, Q_RANK), lambda i: (i, P_QLAT // Q_RANK)), full((1, Q_RANK)),
                  full((Q_RANK, 2 * wide)), full((2, LANES)),
                  pl.BlockSpec((tm, KV_RANK), lambda i: (i, P_KVLAT // KV_RANK)),
                  pl.BlockSpec((tm, LANES), lambda i: (i, P_SMALL // LANES)), full((1, KV_RANK)),
                  full((2 * LANES, 2 * wide)), full((KV_RANK, wide)), full((2, LANES)), tab, tab],
        out_specs=(row(wide), row(wide), row(wide), row(KV_RANK), row(LANES)),
        out_shape=out_shape,
        compiler_params=_cparams(("arbitrary",)),
        name="mla_prep",
    )(p, mw['gq'], mw['wuq'], mw['gqn'], p, p, mw['gkv'], mw['wkc'], mw['wv'], mw['gkn'], *tabs)


def mla_prep_cache(ckv_c, kr_c, mw, l):
    tm = PAST_LEN
    r = ckv_c.shape[0] * PAST_LEN
    full = lambda shape: pl.BlockSpec(shape, lambda i: (0, 0))
    wide = H_A * LANES
    row = lambda w: pl.BlockSpec((tm, w), lambda i: (i, 0))
    cache = lambda w: pl.BlockSpec((1, 1, PAST_LEN, w), lambda i: (i, l, 0, 0))
    return pl.pallas_call(
        functools.partial(_mla_prep_kernel, has_q=False, norm_ckv=False),
        grid=(r // tm,),
        in_specs=[cache(KV_RANK), cache(LANES), full((1, KV_RANK)), full((2 * LANES, 2 * wide)),
                  full((KV_RANK, wide)), full((2, LANES))],
        out_specs=(row(wide), row(wide)),
        out_shape=(jax.ShapeDtypeStruct((r, wide), BF16), jax.ShapeDtypeStruct((r, wide), BF16)),
        compiler_params=_cparams(("arbitrary",)),
        name="mla_prep_cache",
    )(ckv_c, kr_c, mw['gkv'], mw['wkc'], mw['wv'], mw['gkn'])


_NT = (((1,), (1,)), ((), ()))


def _mla_attn_kernel(*refs, has_cache):
    if has_cache:
        q_ref, ko_ref, vo_ref, kc_ref, vc_ref, o_ref = refs
    else:
        q_ref, ko_ref, vo_ref, o_ref = refs
    n_heads = q_ref.shape[1] // LANES
    sls = [slice(h * LANES, (h + 1) * LANES) for h in range(n_heads)]
    def scores(sl):
        s_o = lax.dot_general(q_ref[:, sl], ko_ref[:, sl], _NT, preferred_element_type=F32)
        s_c = lax.dot_general(q_ref[:, sl], kc_ref[:, sl], _NT, preferred_element_type=F32) if has_cache else None
        return s_o, s_c

    outs = []
    nxt = scores(sls[0])
    for h, sl in enumerate(sls):
        s_o, s_c = nxt
        if h + 1 < n_heads:
            nxt = scores(sls[h + 1])
        m = jnp.max(s_o, axis=-1, keepdims=True)
        if has_cache:
            m = jnp.maximum(m, jnp.max(s_c, axis=-1, keepdims=True))
        e_o = jnp.exp(s_o - m)
        l = jnp.sum(e_o, axis=-1, keepdims=True)
        pv = jnp.dot(e_o.astype(BF16), vo_ref[:, sl], preferred_element_type=F32)
        if has_cache:
            e_c = jnp.exp(s_c - m)
            l = l + jnp.sum(e_c, axis=-1, keepdims=True)
            pv = pv + jnp.dot(e_c.astype(BF16), vc_ref[:, sl], preferred_element_type=F32)
        outs.append(pv * (1.0 / l))
    for p in range(n_heads // 2):
        o_ref[:, p * LANES:(p + 1) * LANES] = (outs[2 * p] + outs[2 * p + 1]).astype(o_ref.dtype)


def mla_attention(q, k, v, kc, vc, *, row0, n_seq, t_seq, tq):
    has_cache = kc is not None
    wh = MLA_HEADS_PER_STEP * LANES
    n_grp = H_A // MLA_HEADS_PER_STEP
    nq = t_seq // tq
    qb0, kb0 = row0 // tq, row0 // t_seq
    in_specs = [pl.BlockSpec((tq, wh), lambda s, p, i: (qb0 + s * nq + i, p)),
                pl.BlockSpec((t_seq, wh), lambda s, p, i: (kb0 + s, p)),
                pl.BlockSpec((t_seq, wh), lambda s, p, i: (kb0 + s, p))]
    args = [q, k, v]
    if has_cache:
        in_specs += [pl.BlockSpec((PAST_LEN, wh), lambda s, p, i: (s, p)),
                     pl.BlockSpec((PAST_LEN, wh), lambda s, p, i: (s, p))]
        args += [kc, vc]
    return pl.pallas_call(
        functools.partial(_mla_attn_kernel, has_cache=has_cache),
        grid=(n_seq, n_grp, nq),
        in_specs=in_specs,
        out_specs=pl.BlockSpec((tq, wh // 2), lambda s, p, i: (s * nq + i, p)),
        out_shape=jax.ShapeDtypeStruct((n_seq * t_seq, W_BRANCH), BF16),
        compiler_params=_cparams(("arbitrary", "arbitrary", "arbitrary")),
        name="mla_attention",
    )(*args)


DIFF_TM = 1024


def _diff_prep_kernel(dq_ref, dk_ref, gq_ref, gk_ref, perm_ref, c_ref, s_ref, q_o, ko_o, kp_o):
    c, sn = c_ref[...], s_ref[...]
    lane = lax.broadcasted_iota(jnp.int32, (1, LANES), 1)
    lo = lane < DH_D

    def inv_rms(x):
        x2 = x * x
        s_lo = jnp.sum(jnp.where(lo, x2, 0.0), axis=-1, keepdims=True)
        s_hi = jnp.sum(jnp.where(lo, 0.0, x2), axis=-1, keepdims=True)
        return lax.rsqrt(jnp.where(lo, s_lo, s_hi) * (1.0 / DH_D) + EPS)

    def rotated(n, x_sw, r, g_ref):
        return n * c + (x_sw * r * g_ref[1:2, :]) * sn

    for j in range(W_BRANCH // LANES):
        sl = slice(j * LANES, (j + 1) * LANES)
        xq, xk = dq_ref[:, sl], dk_ref[:, sl]
        q_sw = jnp.dot(xq, perm_ref[...], preferred_element_type=F32)
        k_sw = jnp.dot(xk, perm_ref[...], preferred_element_type=F32)
        xq, xk = xq.astype(F32), xk.astype(F32)
        rq, rk = inv_rms(xq), inv_rms(xk)
        qn = xq * rq * gq_ref[0:1, :]
        kn = xk * rk * gk_ref[0:1, :]
        q_o[:, sl] = (rotated(qn, q_sw, rq, gq_ref) * DIFF_SCALE).astype(q_o.dtype)
        kp_o[:, sl] = kn
        ko_o[:, sl] = rotated(kn, k_sw, rk, gk_ref).astype(ko_o.dtype)


def diff_prep(p, lp, tabs):
    tm = DIFF_TM
    partner = _rope_partner(DH_D, (0, DH_D))
    perm = np.zeros((LANES, LANES), np.float32)
    perm[partner, np.arange(LANES)] = 1.0
    gain2 = lambda g: jnp.stack([jnp.concatenate([g, g]), jnp.concatenate([g, g])[partner]])
    full = lambda shape: pl.BlockSpec(shape, lambda i: (0, 0))
    tab = pl.BlockSpec((tm, LANES), lambda i: (_pos_block_of_tile(i, tm), 0))
    row = pl.BlockSpec((tm, W_BRANCH), lambda i: (i, 0))
    dq_b, dk_b = _seg_block('dq'), _seg_block('dk')
    return pl.pallas_call(
        _diff_prep_kernel,
        grid=(R_ALL // tm,),
        in_specs=[pl.BlockSpec((tm, W_BRANCH), lambda i: (i, dq_b)), pl.BlockSpec((tm, W_BRANCH), lambda i: (i, dk_b)),
                  full((2, LANES)), full((2, LANES)), full((LANES, LANES)), tab, tab],
        out_specs=(row, row, row),
        out_shape=(jax.ShapeDtypeStruct((R_ALL, W_BRANCH), BF16), jax.ShapeDtypeStruct((R_ALL, W_BRANCH), BF16),
                   jax.ShapeDtypeStruct((R_ALL, W_BRANCH), F32)),
        compiler_params=_cparams(("arbitrary",)),
        name="diff_prep",
    )(p, p, gain2(lp['g_diff_qn']), gain2(lp['g_diff_kn']), jnp.asarray(perm, BF16), *tabs)


def _diff_attn_kernel(*refs, has_cache, lambda_init):
    if has_cache:
        dl_ref, q_ref, k_ref, v_ref, kc_ref, vc_ref, g_ref, o_ref = refs
    else:
        dl_ref, q_ref, k_ref, v_ref, g_ref, o_ref = refs
    lane = lax.broadcasted_iota(jnp.int32, (1, LANES), 1)
    dl = dl_ref[...]
    lam = (jnp.exp(jnp.sum(dl[0:1] * dl[1:2], axis=-1, keepdims=True))
           - jnp.exp(jnp.sum(dl[2:3] * dl[3:4], axis=-1, keepdims=True)) + lambda_init)
    units = [(h, w) for h in range(H_D) for w in range(2)]

    def scores(unit):
        h, w = unit
        sl = slice((2 * w + h // 2) * LANES, (2 * w + h // 2 + 1) * LANES)
        q = jnp.where((lane // DH_D) == (h % 2), q_ref[:, sl], jnp.zeros((), q_ref.dtype))
        s_o = lax.dot_general(q, k_ref[:, sl], _NT, preferred_element_type=F32)
        s_c = (lax.dot_general(q, kc_ref[0, 0, :, sl].astype(BF16), _NT, preferred_element_type=F32)
               if has_cache else None)
        return s_o, s_c

    outs = []
    nxt = scores(units[0])
    for n, (h, w) in enumerate(units):
        s_o, s_c = nxt
        if n + 1 < len(units):
            nxt = scores(units[n + 1])
        vs = slice(h * LANES, (h + 1) * LANES)
        m = jnp.max(s_o, axis=-1, keepdims=True)
        if has_cache:
            m = jnp.maximum(m, jnp.max(s_c, axis=-1, keepdims=True))
        e_o = jnp.exp(s_o - m)
        l = jnp.sum(e_o, axis=-1, keepdims=True)
        pv = jnp.dot(e_o.astype(BF16), v_ref[:, vs], preferred_element_type=F32)
        if has_cache:
            e_c = jnp.exp(s_c - m)
            l = l + jnp.sum(e_c, axis=-1, keepdims=True)
            pv = pv + jnp.dot(e_c.astype(BF16), vc_ref[0, 0, :, vs].astype(BF16), preferred_element_type=F32)
        outs.append(pv * (1.0 / l))
    for h in range(H_D):
        y = outs[2 * h] - lam * outs[2 * h + 1]
        r = lax.rsqrt(jnp.mean(y * y, axis=-1, keepdims=True) + EPS)
        o_ref[:, h * LANES:(h + 1) * LANES] = ((y * r * g_ref[...]) * (1.0 - lambda_init)).astype(o_ref.dtype)


def diff_attention(dl, qd, kd, p, kc, vc, g_sub, *, row0, n_seq, t_seq, tq, lambda_init, layer=0):
    has_cache = kc is not None
    nq = t_seq // tq
    qb0, kb0 = row0 // tq, row0 // t_seq
    in_specs = [pl.BlockSpec((4, DH_D), lambda s, i: (0, 0)),
                pl.BlockSpec((tq, W_BRANCH), lambda s, i: (qb0 + s * nq + i, 0)),
                pl.BlockSpec((t_seq, W_BRANCH), lambda s, i: (kb0 + s, 0)),
                pl.BlockSpec((t_seq, W_BRANCH), lambda s, i: (kb0 + s, _seg_block('dv')))]
    args = [dl, qd, kd, p]
    if has_cache:
        in_specs += [pl.BlockSpec((1, 1, PAST_LEN, W_BRANCH), lambda s, i: (s, layer, 0, 0)),
                     pl.BlockSpec((1, 1, PAST_LEN, W_BRANCH), lambda s, i: (s, layer, 0, 0))]
        args += [kc, vc]
    in_specs.append(pl.BlockSpec((1, LANES), lambda s, i: (0, 0)))
    args.append(g_sub.reshape(1, LANES))
    return pl.pallas_call(
        functools.partial(_diff_attn_kernel, has_cache=has_cache, lambda_init=lambda_init),
        grid=(n_seq, nq),
        in_specs=in_specs,
        out_specs=pl.BlockSpec((tq, W_BRANCH), lambda s, i: (s * nq + i, 0)),
        out_shape=jax.ShapeDtypeStruct((n_seq * t_seq, W_BRANCH), BF16),
        compiler_params=_cparams(("arbitrary", "arbitrary")),
        name="diff_attention",
    )(*args)


def _softplus(z):
    return jnp.maximum(z, 0.0) + jnp.log(1.0 + jnp.exp(-jnp.abs(z)))


def _gelu_tanh(x):
    return 0.5 * x * (1.0 + jnp.tanh(math.sqrt(2.0 / math.pi) * (x + 0.044715 * (x * x * x))))


def _rglru_kernel(x_ref, gate_ref, wc_ref, bc_ref, wg_ref, bg_ref, lam_ref, h0_ref, y_ref, st_ref,
                  af_s, uf_s, ab_s, ub_s, *, t_seq):
    t = t_seq
    x = x_ref[...].astype(F32)
    row = lax.broadcasted_iota(jnp.int32, (t, W_B), 0)
    wc = wc_ref[...]
    xc = (wc[0:1] * jnp.where(row >= 2, pltpu.roll(x, 2, 0), 0.0)
          + wc[1:2] * jnp.where(row >= 1, pltpu.roll(x, 1, 0), 0.0)
          + wc[2:3] * x
          + wc[3:4] * jnp.where(row < t - 1, pltpu.roll(x, t - 1, 0), 0.0)
          + bc_ref[...])
    gates = jnp.dot(xc.astype(BF16), wg_ref[...], preferred_element_type=F32) + bg_ref[...]
    for d, (a_s, u_s) in enumerate(((af_s, uf_s), (ab_s, ub_s))):
        rg = jax.nn.sigmoid(gates[:, (2 * d) * W_B:(2 * d + 1) * W_B])
        ig = jax.nn.sigmoid(gates[:, (2 * d + 1) * W_B:(2 * d + 2) * W_B])
        log_a = -RG_C * rg * _softplus(-lam_ref[d:d + 1, :])
        a = jnp.exp(log_a)
        a_s[...] = a
        u_s[...] = jnp.sqrt(-jnp.tanh(log_a) * (a * a + 1.0)) * (ig * xc)

    nblk = t // SUBLANES

    def body(k, carry):
        hf, hb = carry
        base_f = pl.multiple_of(k * SUBLANES, SUBLANES)
        base_b = pl.multiple_of((nblk - 1 - k) * SUBLANES, SUBLANES)
        for r in range(SUBLANES):
            rf = pl.ds(base_f + r, 1)
            rb = pl.ds(base_b + (SUBLANES - 1 - r), 1)
            hf = af_s[rf, :] * hf + uf_s[rf, :]
            hb = ab_s[rb, :] * hb + ub_s[rb, :]
            uf_s[rf, :] = hf
            ub_s[rb, :] = hb
        return hf, hb

    hf, hb = lax.fori_loop(0, nblk, body, (h0_ref[0, 0:1, :], h0_ref[0, 1:2, :]))
    st_ref[0, 0:1, :] = hf
    st_ref[0, 1:2, :] = hb
    y_ref[...] = (_gelu_tanh(gate_ref[...].astype(F32)) * (uf_s[...] + ub_s[...])).astype(y_ref.dtype)


def pack_rglru_weights(lp):
    def blockdiag(w):
        eye = jnp.eye(NB_B, dtype=w.dtype)
        return jnp.einsum('ncd,nm->ncmd', w, eye).reshape(W_B, W_B)
    wg = jnp.concatenate([blockdiag(lp['w_rg_a'][0]), blockdiag(lp['w_rg_x'][0]),
                          blockdiag(lp['w_rg_a'][1]), blockdiag(lp['w_rg_x'][1])], axis=1).astype(BF16)
    bg = jnp.concatenate([lp['b_rg_a'][0], lp['b_rg_x'][0], lp['b_rg_a'][1], lp['b_rg_x'][1]]).reshape(1, 4 * W_B)
    return dict(wg=wg, bg=bg, wc=lp['w_conv_rg'], bc=lp['b_conv_rg'].reshape(1, W_B), lam=lp['rg_lambda'])


def rglru(p, rw, h0, *, row0, n_seq, t_seq):
    rb0 = row0 // t_seq
    xb, gb = _seg_block('rg_x'), _seg_block('rg_gate')
    full = lambda shape: pl.BlockSpec(shape, lambda s: tuple(0 for _ in shape))
    in_specs = [pl.BlockSpec((t_seq, W_B), lambda s: (rb0 + s, xb)),
                pl.BlockSpec((t_seq, W_B), lambda s: (rb0 + s, gb)),
                full((CONV_W, W_B)), full((1, W_B)), full((W_B, 4 * W_B)), full((1, 4 * W_B)), full((2, W_B)),
                pl.BlockSpec((1, 2, W_B), lambda s: (s, 0, 0))]
    args = [p, p, rw['wc'], rw['bc'], rw['wg'], rw['bg'], rw['lam'], h0]
    return pl.pallas_call(
        functools.partial(_rglru_kernel, t_seq=t_seq),
        grid=(n_seq,),
        in_specs=in_specs,
        out_specs=(pl.BlockSpec((t_seq, W_B), lambda s: (s, 0)), pl.BlockSpec((1, 2, W_B), lambda s: (s, 0, 0))),
        out_shape=(jax.ShapeDtypeStruct((n_seq * t_seq, W_B), BF16), jax.ShapeDtypeStruct((n_seq, 2, W_B), F32)),
        scratch_shapes=[pltpu.VMEM((t_seq, W_B), F32)] * 4,
        compiler_params=_cparams(("arbitrary",)),
        name="rglru",
    )(*args)


def _dot_split(a, b_bf16):
    hi = a.astype(BF16)
    lo = (a - hi.astype(F32)).astype(BF16)
    return (jnp.dot(hi, b_bf16, preferred_element_type=F32) + jnp.dot(lo, b_bf16, preferred_element_type=F32))


def _log_sigmoid(z):
    return jnp.minimum(z, 0.0) - jnp.log(1.0 + jnp.exp(-jnp.abs(z)))


_TN = (((0,), (0,)), ((), ()))


def _mlstm_kernel(q_ref, k_ref, v_ref, o_ref, sm_ref, bias_ref, g_ref, c0_ref, n0_ref, m0_ref,
                  y_ref, c_out, n_out, m_out, hm_s, c_s, *, t_seq):
    L = ML_CHUNK
    nchunk = t_seq // L
    scale = DH_C ** -0.5
    ri = lax.broadcasted_iota(jnp.int32, (L, L), 0)
    ci = lax.broadcasted_iota(jnp.int32, (L, L), 1)
    lane1 = lax.broadcasted_iota(jnp.int32, (L, LANES), 1)
    ones_col = jnp.where(lane1 == 0, 1.0, 0.0).astype(BF16)
    bias = bias_ref[...]

    for d in range(2):
        causal = (ci <= ri) if d == 0 else (ci >= ri)
        tri = jnp.where(causal, 1.0, 0.0).astype(BF16)
        tri_t = jnp.where((ri <= ci) if d == 0 else (ri >= ci), 1.0, 0.0).astype(BF16)
        for h in range(H_C):
            c_s[h, :, 0:DH_C] = c0_ref[0, 0, d, h]
            c_s[h, :, DH_C:2 * DH_C] = jnp.where(lane1 == 0, n0_ref[0, 0, d, h], 0.0)
        m_init = tuple(m0_ref[0, :, d * H_C + h:d * H_C + h + 1] for h in range(H_C))

        def chunk(kk, ms, d=d, causal=causal, tri=tri, tri_t=tri_t):
            cidx = kk if d == 0 else nchunk - 1 - kk
            rows = pl.ds(pl.multiple_of(cidx * L, L), L)
            gsm = sm_ref[rows, :].astype(F32) + bias
            lf_all = _log_sigmoid(gsm)
            cum_cols = _dot_split_left(tri, lf_all)
            g_t = gsm.T
            cum_rows = _dot_split(lf_all.T, tri_t)
            heads = range(H_C)
            sls = [slice(h * DH_C, (h + 1) * DH_C) for h in heads]
            qs = [q_ref[rows, sl] for sl in sls]
            ks = [k_ref[rows, sl] for sl in sls]
            v_augs = [jnp.concatenate([v_ref[rows, sl], ones_col], axis=1) for sl in sls]
            qk = [lax.dot_general(qs[h], ks[h], _NT, preferred_element_type=F32) for h in heads]
            qc = [jnp.dot(qs[h], c_s[h].astype(BF16), preferred_element_type=F32) * scale for h in heads]
            cum_c = [cum_cols[:, SM_MF + d * H_C + h:SM_MF + d * H_C + h + 1] for h in heads]
            li_c = [gsm[:, SM_MI + d * H_C + h:SM_MI + d * H_C + h + 1] for h in heads]
            m_row, s = [], []
            for h in heads:
                jl, jf = SM_MI + d * H_C + h, SM_MF + d * H_C + h
                log_d = jnp.where(causal, cum_c[h] - cum_rows[jf:jf + 1, :] + g_t[jl:jl + 1, :], -jnp.inf)
                m_row.append(jnp.maximum(cum_c[h] + ms[h], jnp.max(log_d, axis=-1, keepdims=True)))
                s.append(qk[h] * (scale * jnp.exp(log_d - m_row[h])))
            sv = [jnp.dot(s[h].astype(BF16), v_augs[h], preferred_element_type=F32) for h in heads]
            new_ms = []
            for h in heads:
                w_inter = jnp.exp(cum_c[h] + ms[h] - m_row[h])
                nd = sv[h] + qc[h] * w_inter
                den = jnp.maximum(jnp.abs(nd[:, DH_C:DH_C + 1]), jnp.exp(-m_row[h]))
                h_out = nd[:, :DH_C] * (1.0 / den)
                if d == 0:
                    hm_s[rows, sls[h]] = h_out
                else:
                    hm_s[rows, sls[h]] = hm_s[rows, sls[h]] + h_out
                last = cum_c[h][L - 1:L, :] if d == 0 else cum_c[h][0:1, :]
                w_s = last - cum_c[h] + li_c[h]
                m_new = jnp.maximum(last + ms[h], jnp.max(w_s, axis=0, keepdims=True))
                decay = jnp.exp(last + ms[h] - m_new)
                kw_t = (ks[h].astype(F32) * jnp.exp(w_s - m_new)).T.astype(BF16)
                c_s[h] = decay * c_s[h] + jnp.dot(kw_t, v_augs[h], preferred_element_type=F32)
                new_ms.append(m_new)
            return tuple(new_ms)

        m_fin = lax.fori_loop(0, nchunk, chunk, m_init)
        for h in range(H_C):
            c_out[0, d, h] = c_s[h, :, 0:DH_C]
            n_out[0, d, h] = c_s[h, :, DH_C:DH_C + 1]
            m_out[0, :, d * H_C + h:d * H_C + h + 1] = m_fin[h]

    for h in range(H_C):
        sl = slice(h * DH_C, (h + 1) * DH_C)
        hm = hm_s[:, sl]
        r = lax.rsqrt(jnp.mean(hm * hm, axis=-1, keepdims=True) + EPS)
        y_ref[:, sl] = (jax.nn.sigmoid(o_ref[:, sl].astype(F32)) * (hm * r * g_ref[...])).astype(y_ref.dtype)


def _dot_split_left(a_bf16, b):
    hi = b.astype(BF16)
    lo = (b - hi.astype(F32)).astype(BF16)
    return (jnp.dot(a_bf16, hi, preferred_element_type=F32) + jnp.dot(a_bf16, lo, preferred_element_type=F32))


def mlstm(p, bias_sm, g_out, c0, n0, m0, *, row0, n_seq, t_seq, layer=0):
    rb0 = row0 // t_seq
    seg = lambda nm: pl.BlockSpec((t_seq, W_BRANCH), lambda s, b=_seg_block(nm): (rb0 + s, b))
    full = lambda shape: pl.BlockSpec(shape, lambda s: tuple(0 for _ in shape))
    c_spec = pl.BlockSpec((1, 2, H_C, DH_C, DH_C), lambda s: (s, 0, 0, 0, 0))
    n_spec = pl.BlockSpec((1, 2, H_C, DH_C, 1), lambda s: (s, 0, 0, 0, 0))
    c_in = pl.BlockSpec((1, 1, 2, H_C, DH_C, DH_C), lambda s: (s, layer, 0, 0, 0, 0))
    n_in = pl.BlockSpec((1, 1, 2, H_C, DH_C, 1), lambda s: (s, layer, 0, 0, 0, 0))
    m_spec = pl.BlockSpec((1, 1, 2 * H_C), lambda s: (s, 0, 0))
    in_specs = [seg('mq'), seg('mk'), seg('mv'), seg('mo'),
                pl.BlockSpec((t_seq, LANES), lambda s: (rb0 + s, P_SMALL // LANES)),
                full((1, LANES)), full((1, DH_C)), c_in, n_in, m_spec]
    args = [p, p, p, p, p, bias_sm, g_out.reshape(1, DH_C), c0, n0, m0]
    return pl.pallas_call(
        functools.partial(_mlstm_kernel, t_seq=t_seq),
        grid=(n_seq,),
        in_specs=in_specs,
        out_specs=(pl.BlockSpec((t_seq, W_BRANCH), lambda s: (s, 0)), c_spec, n_spec, m_spec),
        out_shape=(jax.ShapeDtypeStruct((n_seq * t_seq, W_BRANCH), BF16),
                   jax.ShapeDtypeStruct((n_seq, 2, H_C, DH_C, DH_C), F32),
                   jax.ShapeDtypeStruct((n_seq, 2, H_C, DH_C, 1), F32),
                   jax.ShapeDtypeStruct((n_seq, 1, 2 * H_C), F32)),
        scratch_shapes=[pltpu.VMEM((t_seq, W_BRANCH), F32), pltpu.VMEM((H_C, DH_C, 2 * DH_C), F32)],
        compiler_params=_cparams(("arbitrary",)),
        name="mlstm",
    )(*args)


MERGE_TM = 512


def _merge_kernel(*refs):
    ctx_refs, lat_refs, gate_refs = refs[0:4], refs[4:8], refs[8:12]
    xc_ref, xl_ref, mod_ref, wbr_ref, wout_ref, gn_ref, xo_ref, xn_ref = refs[12:20]
    xn3_ref = refs[20] if len(refs) > 20 else None
    is_ctx = pl.program_id(0) < R_CTX // MERGE_TM
    merged = None
    for g in range(N_BRANCH):
        yg = jnp.where(is_ctx, ctx_refs[g][...], lat_refs[g][...])
        pg = jnp.dot(yg, wbr_ref[g], preferred_element_type=F32)
        term = jax.nn.sigmoid(gate_refs[g][...].astype(F32)) * pg
        merged = term if merged is None else merged + term
    y = jnp.dot(merged.astype(BF16), wout_ref[...], preferred_element_type=F32)
    x = jnp.where(is_ctx, xc_ref[...], xl_ref[...]) + mod_ref[0, 2:3, :] * y
    xo_ref[...] = x
    r = lax.rsqrt(jnp.mean(x * x, axis=-1, keepdims=True) + EPS)
    xn = (x * r * gn_ref[...]) * (1.0 + mod_ref[0, 4:5, :]) + mod_ref[0, 3:4, :]
    xn_ref[...] = xn.astype(xn_ref.dtype)
    if xn3_ref is not None:
        _rows_to_tiles(xn3_ref, xn.astype(BF16))


def merge(ys_ctx, ys_lat, p, x_pair, mod, wbr, wout, g_ffn, rows_as_tiles):
    tm = MERGE_TM
    n_ctx_tiles = R_CTX // tm
    br_ctx = pl.BlockSpec((tm, W_BRANCH), lambda i: (jnp.minimum(i, n_ctx_tiles - 1), 0))
    br_lat = pl.BlockSpec((tm, W_BRANCH), lambda i: (jnp.maximum(i - n_ctx_tiles, 0), 0))
    gate = lambda g: pl.BlockSpec((tm, D_MODEL), lambda i, g=g: (i, g))
    row = pl.BlockSpec((tm, D_MODEL), lambda i: (i, 0))
    out_specs = [row, row]
    out_shape = [jax.ShapeDtypeStruct((R_ALL, D_MODEL), F32), jax.ShapeDtypeStruct((R_ALL, D_MODEL), BF16)]
    if rows_as_tiles:
        out_specs.append(pl.BlockSpec((tm, ROW_SUB, LANES), lambda i: (i, 0, 0)))
        out_shape.append(jax.ShapeDtypeStruct((R_ALL, ROW_SUB, LANES), F32))
    return pl.pallas_call(
        _merge_kernel,
        grid=(R_ALL // tm,),
        in_specs=[br_ctx] * N_BRANCH + [br_lat] * N_BRANCH + [gate(0), gate(1), gate(2), gate(3),
                  *_ctx_lat_specs(tm, D_MODEL, 1),
                  pl.BlockSpec((1, 8, D_MODEL), lambda i: (_mod_row_of_tile(i, tm), 0, 0)),
                  pl.BlockSpec((N_BRANCH, W_BRANCH, D_MODEL), lambda i: (0, 0, 0)),
                  pl.BlockSpec((D_MODEL, D_MODEL), lambda i: (0, 0)),
                  pl.BlockSpec((1, D_MODEL), lambda i: (0, 0))],
        out_specs=tuple(out_specs),
        out_shape=tuple(out_shape),
        compiler_params=_cparams(("arbitrary",)),
        name="merge",
    )(*ys_ctx, *ys_lat, p, p, p, p, *x_pair, mod, wbr, wout, g_ffn.reshape(1, D_MODEL))


def _new_expert(te_ref, i):
    return jnp.logical_or(i == 0, te_ref[i] != te_ref[jnp.maximum(i - 1, 0)])


def _ffn_up_kernel(te_ref, nt_ref, x_ref, wg_ref, wu_ref, h_ref, wgb_ref, wub_ref):
    i = pl.program_id(1)

    @pl.when(_new_expert(te_ref, i))
    def _():
        wgb_ref[...] = wg_ref[0].astype(BF16)
        wub_ref[...] = wu_ref[0].astype(BF16)

    @pl.when(i < nt_ref[0])
    def _():
        x = x_ref[...]
        g = jnp.dot(x, wgb_ref[...], preferred_element_type=F32)
        u = jnp.dot(x, wub_ref[...], preferred_element_type=F32)
        h_ref[...] = (g * jax.nn.sigmoid(g) * u).astype(h_ref.dtype)

    @pl.when(i >= nt_ref[0])
    def _():
        h_ref[...] = jnp.zeros(h_ref.shape, h_ref.dtype)


def ffn_up(tile_expert, n_tiles, xs, wg, wu, tm, tf, weight_buffers):
    r, d = xs.shape
    f = wg.shape[2]
    w_spec = pl.BlockSpec((1, d, tf), lambda j, i, te, nt: (te[i], 0, j), pipeline_mode=pl.Buffered(weight_buffers))
    return pl.pallas_call(
        _ffn_up_kernel,
        grid_spec=pltpu.PrefetchScalarGridSpec(
            num_scalar_prefetch=2,
            grid=(f // tf, r // tm),
            in_specs=[pl.BlockSpec((tm, d), lambda j, i, te, nt: (i, 0)), w_spec, w_spec],
            out_specs=pl.BlockSpec((tm, tf), lambda j, i, te, nt: (i, j)),
            scratch_shapes=[pltpu.VMEM((d, tf), BF16), pltpu.VMEM((d, tf), BF16)]),
        out_shape=jax.ShapeDtypeStruct((r, f), BF16),
        compiler_params=_cparams(("arbitrary", "arbitrary")),
        name="ffn_up",
    )(tile_expert, n_tiles, xs, wg, wu)


def _ffn_down_kernel(te_ref, nt_ref, h_ref, wd_ref, y_ref, wdb_ref):
    i = pl.program_id(0)

    @pl.when(_new_expert(te_ref, i))
    def _():
        wdb_ref[...] = wd_ref[0].astype(BF16)

    @pl.when(i < nt_ref[0])
    def _():
        _rows_to_tiles(y_ref, jnp.dot(h_ref[...], wdb_ref[...], preferred_element_type=F32))

    @pl.when(i >= nt_ref[0])
    def _():
        y_ref[...] = jnp.zeros(y_ref.shape, y_ref.dtype)


def ffn_down(tile_expert, n_tiles, h, wd, tm):
    r, f = h.shape
    d = wd.shape[2]
    return pl.pallas_call(
        _ffn_down_kernel,
        grid_spec=pltpu.PrefetchScalarGridSpec(
            num_scalar_prefetch=2,
            grid=(r // tm,),
            in_specs=[pl.BlockSpec((tm, f), lambda i, te, nt: (i, 0)),
                      pl.BlockSpec((1, f, d), lambda i, te, nt: (te[i], 0, 0))],
            out_specs=pl.BlockSpec((tm, ROW_SUB, LANES), lambda i, te, nt: (i, 0, 0)),
            scratch_shapes=[pltpu.VMEM((f, d), BF16)]),
        out_shape=jax.ShapeDtypeStruct((r, ROW_SUB, LANES), F32),
        compiler_params=_cparams(("arbitrary",)),
        name="ffn_down",
    )(tile_expert, n_tiles, h, wd)


def _ffn_down_res_kernel(h_ref, wd_ref, x_ref, mod_ref, yc_ref, yl_ref, wdb_ref):
    i = pl.program_id(0)

    @pl.when(i == 0)
    def _():
        wdb_ref[...] = wd_ref[...].astype(BF16)

    y = jnp.dot(h_ref[...], wdb_ref[...], preferred_element_type=F32)
    res = x_ref[...] + mod_ref[0, 5:6, :] * y
    n_ctx = R_CTX // h_ref.shape[0]

    @pl.when(i < n_ctx)
    def _():
        yc_ref[...] = res

    @pl.when(i >= n_ctx)
    def _():
        yl_ref[...] = res


def ffn_down_residual(h, wd, x, mod):
    tm = 1024
    r, f = h.shape
    d = wd.shape[1]
    return pl.pallas_call(
        _ffn_down_res_kernel,
        grid=(r // tm,),
        in_specs=[pl.BlockSpec((tm, f), lambda i: (i, 0)),
                  pl.BlockSpec((f, d), lambda i: (0, 0), pipeline_mode=pl.Buffered(1)),
                  pl.BlockSpec((tm, d), lambda i: (i, 0)),
                  pl.BlockSpec((1, 8, d), lambda i: (_mod_row_of_tile(i, tm), 0, 0))],
        out_specs=_ctx_lat_specs(tm, d, 1),
        out_shape=(jax.ShapeDtypeStruct((R_CTX, d), F32), jax.ShapeDtypeStruct((r - R_CTX, d), F32)),
        scratch_shapes=[pltpu.VMEM((f, d), BF16)],
        compiler_params=_cparams(("arbitrary",)),
        name="ffn_down_residual",
    )(h, wd, x, mod)


def dense_swiglu_residual(xn, x, mod, wg, wu, wd):
    t = xn.shape[0]
    tm = 1024
    n_tiles = t // tm
    te = jnp.zeros((n_tiles,), jnp.int32)
    nt = jnp.full((1,), n_tiles, jnp.int32)
    h = ffn_up(te, nt, xn, wg[None], wu[None], tm=tm, tf=wg.shape[1] // 2, weight_buffers=1)
    return ffn_down_residual(h, wd, x, mod)


MOE_TM = 512
DISPATCH_TM = 512


def _moe_routing(logits, tm):
    t = logits.shape[0]
    n_assign = t * TOP_K
    top_v, top_i = lax.top_k(logits, TOP_K)
    gate = jax.nn.softmax(top_v, axis=-1)
    flat_e = top_i.reshape(-1).astype(jnp.int32)
    onehot = (flat_e[:, None] == jnp.arange(N_EXP, dtype=jnp.int32)[None, :])
    blk = LANES
    oh = onehot.astype(F32).reshape(n_assign // blk, blk, N_EXP)
    tril = jnp.tril(jnp.ones((blk, blk), F32))
    within = jnp.einsum('ij,bjk->bik', tril, oh)
    blk_tot = within[:, -1, :]
    blk_off = jnp.cumsum(blk_tot, axis=0) - blk_tot
    csum = (within + blk_off[:, None, :]).reshape(n_assign, N_EXP)
    rank = jnp.sum(jnp.where(onehot, csum - 1.0, 0.0), axis=1).astype(jnp.int32)
    counts = csum[-1].astype(jnp.int32)
    padded = (counts + tm - 1) // tm * tm
    grp_start = jnp.cumsum(padded) - padded
    raw_start = jnp.cumsum(counts) - counts
    slot_of_assign = jnp.sum(jnp.where(onehot, grp_start[None, :], 0), axis=1) + rank

    r_max = n_assign + N_EXP * tm
    tile_start = jnp.arange(r_max // tm, dtype=jnp.int32) * tm
    tile_expert = jnp.sum((tile_start[:, None] >= (grp_start + padded)[None, :]).astype(jnp.int32), axis=1)
    tile_expert = jnp.minimum(tile_expert, N_EXP - 1).astype(jnp.int32)
    n_tiles = (jnp.sum(padded) // tm).astype(jnp.int32).reshape(1)

    order = jnp.argsort(flat_e, stable=True).astype(jnp.int32)
    e_slot = jnp.repeat(tile_expert, tm)
    j = jnp.arange(r_max, dtype=jnp.int32) - grp_start[e_slot]
    src = jnp.clip(raw_start[e_slot] + j, 0, n_assign - 1)
    tok_of_slot = jnp.where(j < counts[e_slot], order[src] // TOP_K, 0)
    return gate, slot_of_assign, tok_of_slot, tile_expert, n_tiles


ROW_SUB = D_MODEL // LANES


def _rows_to_tiles(o3_ref, x):
    for j in range(ROW_SUB):
        o3_ref[:, j, :] = x[:, j * LANES:(j + 1) * LANES].astype(o3_ref.dtype)


def _start_row_gather(idx_ref, src_ref, dst, sem, n_rows):
    def body(q, carry):
        for u in range(2):
            r = 2 * q + u
            pltpu.make_async_copy(src_ref.at[idx_ref[0, 0, r]], dst.at[r], sem).start(priority=u)
        return carry

    lax.fori_loop(0, n_rows // 2, body, 0, unroll=4)


def _wait_row_gather(src_ref, dst, sem, n_rows):
    pltpu.make_async_copy(src_ref.at[pl.ds(0, n_rows)], dst, sem).wait()


def _tiles_to_rows(tiles, rows_ref):
    for j in range(ROW_SUB):
        rows_ref[:, j * LANES:(j + 1) * LANES] = tiles[:, j, :]


def _dispatch_kernel(nt_ref, idx_ref, idx_next_ref, src_ref, o_ref, buf, rows, sem):
    i = pl.program_id(0)
    nt = nt_ref[0]
    tm = o_ref.shape[0]
    slot = i % 2

    @pl.when(i == 0)
    def _():
        _start_row_gather(idx_ref, src_ref, buf.at[0], sem.at[0], tm)

    @pl.when(i + 1 < nt)
    def _():
        _start_row_gather(idx_next_ref, src_ref, buf.at[1 - slot], sem.at[1 - slot], tm)

    @pl.when(i < nt)
    def _():
        _wait_row_gather(src_ref, buf.at[slot], sem.at[slot], tm)
        _tiles_to_rows(buf.at[slot], rows)
        o_ref[...] = rows[...].astype(o_ref.dtype)

    @pl.when(i >= nt)
    def _():
        o_ref[...] = jnp.zeros(o_ref.shape, o_ref.dtype)


def moe_dispatch(n_tiles, tok_of_slot, xn3, tm):
    r = tok_of_slot.shape[0]
    last = r // tm - 1
    idx = tok_of_slot.reshape(r // tm, 1, tm)
    return pl.pallas_call(
        _dispatch_kernel,
        grid_spec=pltpu.PrefetchScalarGridSpec(
            num_scalar_prefetch=1,
            grid=(r // tm,),
            in_specs=[pl.BlockSpec((1, 1, tm), lambda i, nt: (i, 0, 0), memory_space=pltpu.SMEM),
                      pl.BlockSpec((1, 1, tm), lambda i, nt: (jnp.minimum(i + 1, last), 0, 0),
                                   memory_space=pltpu.SMEM),
                      pl.BlockSpec(memory_space=pl.ANY)],
            out_specs=pl.BlockSpec((tm, D_MODEL), lambda i, nt: (i, 0)),
            scratch_shapes=[pltpu.VMEM((2, tm, ROW_SUB, LANES), F32), pltpu.VMEM((tm, D_MODEL), F32),
                            pltpu.SemaphoreType.DMA((2,))]),
        out_shape=jax.ShapeDtypeStruct((r, D_MODEL), BF16),
        compiler_params=_cparams(("arbitrary",)),
        name="moe_dispatch",
    )(n_tiles, idx, idx, xn3)


COMBINE_TM = 512


def _combine_kernel(idx_ref, idx_next_ref, ys_ref, x_ref, gate_ref, mod_ref, oc_ref, ol_ref, buf, rows, sem):
    i = pl.program_id(0)
    tm = oc_ref.shape[0]
    n_rows = TOP_K * tm
    slot = i % 2

    @pl.when(i == 0)
    def _():
        _start_row_gather(idx_ref, ys_ref, buf.at[0], sem.at[0], n_rows)

    @pl.when(i + 1 < pl.num_programs(0))
    def _():
        _start_row_gather(idx_next_ref, ys_ref, buf.at[1 - slot], sem.at[1 - slot], n_rows)

    _wait_row_gather(ys_ref, buf.at[slot], sem.at[slot], n_rows)
    _tiles_to_rows(buf.at[slot], rows)
    f = gate_ref[:, 0:1] * rows[0:tm, :] + gate_ref[:, 1:2] * rows[tm:2 * tm, :]
    res = x_ref[...] + mod_ref[0, 5:6, :] * f

    @pl.when(i < R_CTX // COMBINE_TM)
    def _():
        oc_ref[...] = res

    @pl.when(i >= R_CTX // COMBINE_TM)
    def _():
        ol_ref[...] = res


def moe_combine(slot_of_assign, ys3, x, gate, mod):
    t, d = x.shape
    tm = COMBINE_TM
    idx = slot_of_assign.reshape(t // tm, tm, TOP_K).transpose(0, 2, 1).reshape(t // tm, 1, TOP_K * tm)
    last = t // tm - 1
    n_ctx = R_CTX // tm
    return pl.pallas_call(
        _combine_kernel,
        grid=(t // tm,),
        in_specs=[pl.BlockSpec((1, 1, TOP_K * tm), lambda i: (i, 0, 0), memory_space=pltpu.SMEM),
                  pl.BlockSpec((1, 1, TOP_K * tm), lambda i: (jnp.minimum(i + 1, last), 0, 0),
                               memory_space=pltpu.SMEM),
                  pl.BlockSpec(memory_space=pl.ANY),
                  pl.BlockSpec((tm, d), lambda i: (i, 0)),
                  pl.BlockSpec((tm, TOP_K), lambda i: (i, 0)),
                  pl.BlockSpec((1, 8, d), lambda i: (_mod_row_of_tile(i, tm), 0, 0))],
        out_specs=(pl.BlockSpec((tm, d), lambda i: (jnp.minimum(i, n_ctx - 1), 0)),
                   pl.BlockSpec((tm, d), lambda i: (jnp.maximum(i - n_ctx, 0), 0))),
        out_shape=(jax.ShapeDtypeStruct((R_CTX, d), F32), jax.ShapeDtypeStruct((t - R_CTX, d), F32)),
        scratch_shapes=[pltpu.VMEM((2, TOP_K * tm, ROW_SUB, LANES), F32), pltpu.VMEM((TOP_K * tm, d), F32),
                        pltpu.SemaphoreType.DMA((2,))],
        compiler_params=_cparams(("arbitrary",)),
        name="moe_combine",
    )(idx, idx, ys3, x, gate, mod)


def moe_swiglu_residual(xn3, xn, x, mod, w_router, wg, wu, wd):
    tm = MOE_TM
    logits = jnp.dot(xn.astype(F32), w_router, precision=lax.Precision.HIGHEST)
    gate, slot_of_assign, tok_of_slot, tile_expert, n_tiles = _moe_routing(logits, tm)
    xs = moe_dispatch(n_tiles * (tm // DISPATCH_TM), tok_of_slot, xn3, DISPATCH_TM)
    h = ffn_up(tile_expert, n_tiles, xs, wg, wu, tm=tm, tf=wg.shape[2] // 2, weight_buffers=2)
    ys3 = ffn_down(tile_expert, n_tiles, h, wd, tm=tm)
    return moe_combine(slot_of_assign, ys3, x, gate, mod)


def _layer(x, cond, lp, l, ctx, tabs_a, tabs_d):
    mod = modulation(cond, lp['w_mod'], lp['b_mod'], l).reshape(cond.shape[0], 6, D_MODEL)
    mod = jnp.pad(mod, ((0, 0), (0, 2), (0, 0)))
    p = in_proj(x, mod, lp['g_norm_mix'], lp['w_in_p'], l)

    mw = pack_mla_weights(lp)
    q_a, k_a, v_a, ckv, kr = mla_prep(p, mw, tabs_a)
    kc_a, vc_a = mla_prep_cache(ctx['mla_ckv'], ctx['mla_krope'], mw, l)
    ya_c = mla_attention(q_a, k_a, v_a, None, None, row0=0, n_seq=N_CTX_SEQ, t_seq=T_CTX, tq=T_CTX)
    ya_l = mla_attention(q_a, k_a, v_a, kc_a, vc_a, row0=R_CTX, n_seq=N_LAT_SEQ, t_seq=T_LAT, tq=ATTN_TQ)

    rw = pack_rglru_weights(lp)
    yb_c, st_rg = rglru(p, rw, jnp.zeros((N_CTX_SEQ, 2, W_B), F32), row0=0, n_seq=N_CTX_SEQ, t_seq=T_CTX)
    yb_l, _ = rglru(p, rw, ctx['rglru'], row0=R_CTX, n_seq=N_LAT_SEQ, t_seq=T_LAT)

    bias_sm = jnp.zeros((LANES,), F32).at[SM_MI:SM_MI + 2 * H_C].set(lp['b_ml_i'].reshape(-1))
    bias_sm = bias_sm.at[SM_MF:SM_MF + 2 * H_C].set(lp['b_ml_f'].reshape(-1)).reshape(1, LANES)
    c0_ctx = jnp.zeros((N_CTX_SEQ, 1, 2, H_C, DH_C, DH_C), F32)
    n0_ctx = jnp.zeros((N_CTX_SEQ, 1, 2, H_C, DH_C, 1), F32)
    m0_ctx = jnp.zeros((N_CTX_SEQ, 1, 2 * H_C), F32)
    m0_lat = ctx['mlstm_m'][:, l].reshape(N_LAT_SEQ, 1, 2 * H_C)
    yc_c, c_fin, n_fin, m_fin = mlstm(p, bias_sm, lp['g_ml_out'], c0_ctx, n0_ctx, m0_ctx,
                                      row0=0, n_seq=N_CTX_SEQ, t_seq=T_CTX)
    yc_l, _, _, _ = mlstm(p, bias_sm, lp['g_ml_out'], ctx['mlstm_C'], ctx['mlstm_n'][..., None], m0_lat,
                          row0=R_CTX, n_seq=N_LAT_SEQ, t_seq=T_LAT, layer=l)

    lambda_init = 0.8 - 0.6 * math.exp(-0.3 * l)
    qd, kd_own, kd_plain = diff_prep(p, lp, tabs_d)
    yd_c = diff_attention(lp['diff_lambda'], qd, kd_own, p, None, None, lp['g_diff_sub'],
                          row0=0, n_seq=N_CTX_SEQ, t_seq=T_CTX, tq=T_CTX, lambda_init=lambda_init)
    yd_l = diff_attention(lp['diff_lambda'], qd, kd_own, p, ctx['diff_k'], ctx['diff_v'], lp['g_diff_sub'],
                          row0=R_CTX, n_seq=N_LAT_SEQ, t_seq=T_LAT, tq=ATTN_TQ, lambda_init=lambda_init, layer=l)

    merged = merge((ya_c, yb_c, yc_c, yd_c), (ya_l, yb_l, yc_l, yd_l), p, x, mod, lp['w_br'].astype(BF16),
                   lp['w_out'].astype(BF16), lp['g_norm_ffn'], rows_as_tiles='moe' in lp)
    if 'ffn' in lp:
        x, xn = merged
        x = dense_swiglu_residual(xn, x, mod, *lp['ffn'])
    else:
        x, xn, xn3 = merged
        x = moe_swiglu_residual(xn3, xn, x, mod, *lp['moe'])

    dv0 = _seg_block('dv') * W_BRANCH
    ctx_out = (ckv[:R_CTX].reshape(N_CTX_SEQ, T_CTX, KV_RANK),
               kr[:R_CTX, :ROPE_A].reshape(N_CTX_SEQ, T_CTX, ROPE_A),
               kd_plain[:R_CTX].reshape(N_CTX_SEQ, T_CTX, 2, H_D, DH_D),
               p[:R_CTX, dv0:dv0 + W_BRANCH].astype(F32).reshape(N_CTX_SEQ, T_CTX, H_D, 2 * DH_D),
               st_rg,
               c_fin,
               n_fin.reshape(N_CTX_SEQ, 2, H_C, DH_C),
               m_fin.reshape(N_CTX_SEQ, 2, H_C))
    return x, ctx_out


def kernel(x_prompt, x_sample, cache_mla_ckv, cache_mla_krope, cache_diff_k, cache_diff_v,
           state_rglru, state_mlstm_C, state_mlstm_n, state_mlstm_m, c, c_ctx,
           w_mod, b_mod, g_norm_mix, g_norm_ffn, w_in, g_mla_qlat, w_mla_uq, g_mla_kvlat, w_mla_ukv,
           g_mla_qn, g_mla_kn, w_conv_rg, b_conv_rg, w_rg_a, b_rg_a, w_rg_x, b_rg_x, rg_lambda,
           b_ml_i, b_ml_f, g_ml_out, g_diff_qn, g_diff_kn, diff_lambda, g_diff_sub, w_br, w_out,
           w_ffn_gate, w_ffn_up, w_ffn_down, w_router, w_moe_gate, w_moe_up, w_moe_down):
    assert x_prompt.shape == (N_CTX_SEQ, T_CTX, D_MODEL) and x_sample.shape == (N_LAT_SEQ, T_LAT, D_MODEL)
    tabs_a = _rope_tables(ROPE_A, (NOPE_A,), MLA_TM)
    tabs_d = _rope_tables(DH_D, (0, DH_D), DIFF_TM)
    cond = jnp.concatenate([c_ctx.reshape(1, D_MODEL), c, jnp.zeros((16 - 1 - N_LAT_SEQ, D_MODEL), F32)], axis=0)
    x = (x_prompt.reshape(R_CTX, D_MODEL), x_sample.reshape(R_LAT, D_MODEL))
    w_in_p = pack_w_in(w_in)
    krope_pad = jnp.pad(cache_mla_krope, ((0, 0), (0, 0), (0, 0), (0, LANES - ROPE_A)))
    diff_k_rows = cache_diff_k.reshape(N_LAT_SEQ, DEPTH, PAST_LEN, W_BRANCH)
    diff_v_rows = cache_diff_v.reshape(N_LAT_SEQ, DEPTH, PAST_LEN, W_BRANCH)
    new = []
    for l in range(DEPTH):
        lp = dict(w_mod=w_mod, b_mod=b_mod, g_norm_mix=g_norm_mix[l], g_norm_ffn=g_norm_ffn[l], w_in_p=w_in_p,
                  g_mla_qlat=g_mla_qlat[l], w_mla_uq=w_mla_uq[l], g_mla_kvlat=g_mla_kvlat[l], w_mla_ukv=w_mla_ukv[l],
                  g_mla_qn=g_mla_qn[l], g_mla_kn=g_mla_kn[l], w_conv_rg=w_conv_rg[l], b_conv_rg=b_conv_rg[l],
                  w_rg_a=w_rg_a[l], b_rg_a=b_rg_a[l], w_rg_x=w_rg_x[l], b_rg_x=b_rg_x[l], rg_lambda=rg_lambda[l],
                  b_ml_i=b_ml_i[l], b_ml_f=b_ml_f[l], g_ml_out=g_ml_out[l], g_diff_qn=g_diff_qn[l],
                  g_diff_kn=g_diff_kn[l], diff_lambda=diff_lambda[l], g_diff_sub=g_diff_sub[l],
                  w_br=w_br[l], w_out=w_out[l])
        if l % 2 == 0:
            lp['ffn'] = (w_ffn_gate[l // 2], w_ffn_up[l // 2], w_ffn_down[l // 2])
        else:
            lp['moe'] = (w_router[l // 2], w_moe_gate[l // 2], w_moe_up[l // 2], w_moe_down[l // 2])
        ctx_l = dict(mla_ckv=cache_mla_ckv, mla_krope=krope_pad, diff_k=diff_k_rows, diff_v=diff_v_rows,
                     rglru=state_rglru[:, l], mlstm_C=state_mlstm_C, mlstm_n=state_mlstm_n, mlstm_m=state_mlstm_m)
        x, st = _layer(x, cond, lp, l, ctx_l, tabs_a, tabs_d)
        new.append(st)
    outs = tuple(jnp.stack([s[i] for s in new], axis=1) for i in range(8))
    x_ctx, x_lat = x
    return (x_ctx.reshape(N_CTX_SEQ, T_CTX, D_MODEL), x_lat.reshape(N_LAT_SEQ, T_LAT, D_MODEL)) + outs
```
